```python
import math
import jax, jax.numpy as jnp
from jax import lax
import numpy as np

D_MODEL = 1024
BATCH = 2
SEQ = 8192
DEPTH = 4

CHUNK = 64
GDN_HEADS = 4
GDN_DK = 128
GDN_DV = 128
CONV_W = 4
GLA_HEADS = 4
GLA_DK = 64
GLA_DV = 128
GLA_RANK = 16
GLA_NORMALIZER = 16.0
HGRN_HEADS = 4
HGRN_EXPAND = 128
HGRN_DV = 128
LB_FLOOR = 1e-30
N_EXPERTS = 16
N_GROUPS = 4
EXPERTS_PER_GROUP = N_EXPERTS // N_GROUPS
TOP_K = 2
D_FF = 256
ALPHA = (2.0 * DEPTH) ** 0.25
BETA = (8.0 * DEPTH) ** -0.25
LN_EPS = 1e-5
RMS_EPS = 1e-6

GDN_QK = GDN_HEADS * GDN_DK
GDN_V = GDN_HEADS * GDN_DV
GLA_QK = GLA_HEADS * GLA_DK
GLA_V = GLA_HEADS * GLA_DV
HGRN_QK = HGRN_HEADS * HGRN_EXPAND
HGRN_V = HGRN_HEADS * HGRN_DV
SPLIT_SIZES = (GDN_QK, GDN_QK, GDN_V, GDN_HEADS, GDN_HEADS, GDN_V,
               GLA_QK, GLA_QK, GLA_V, GLA_RANK, GLA_V,
               HGRN_QK, HGRN_QK, HGRN_V, HGRN_V,
               D_MODEL, D_MODEL, D_MODEL)
SPLIT_POINTS = tuple(int(v) for v in np.cumsum(SPLIT_SIZES)[:-1])
IN_WIDTH = int(sum(SPLIT_SIZES))

kernel_name = "hybrid_gdn_gla_hgrn2_grouped_moe_deepnorm"


def layer_norm(x, g, b):
    xf = x.astype(jnp.float32)
    mu = xf.mean(-1, keepdims=True)
    var = jnp.square(xf - mu).mean(-1, keepdims=True)
    return ((xf - mu) * lax.rsqrt(var + LN_EPS) * g.astype(jnp.float32) + b.astype(jnp.float32)).astype(x.dtype)


def l2_normalize(x):
    return x * lax.rsqrt(jnp.sum(jnp.square(x), -1, keepdims=True) + RMS_EPS)


def to_chunks(x, n_heads):
    b, s, w = x.shape
    return x.reshape(b, s // CHUNK, CHUNK, n_heads, w // n_heads).transpose(0, 3, 1, 2, 4)


def scalar_to_chunks(x):
    b, s, h = x.shape
    return x.reshape(b, s // CHUNK, CHUNK, h).transpose(0, 3, 1, 2)


def from_chunks(o):
    b, h, n, c, d = o.shape
    return o.transpose(0, 2, 3, 1, 4).reshape(b, n * c, h * d)


def masked_exp(diff, mask):
    return jnp.where(mask, jnp.exp(jnp.where(mask, diff, 0.0)), 0.0)


def gated_rms_heads(o, gate, g, n_heads):
    b, s, w = o.shape
    oh = o.astype(jnp.float32).reshape(b, s, n_heads, w // n_heads)
    oh = oh * lax.rsqrt(jnp.mean(jnp.square(oh), -1, keepdims=True) + RMS_EPS) * g.astype(jnp.float32)
    gh = jax.nn.silu(gate.astype(jnp.float32)).reshape(b, s, n_heads, w // n_heads)
    return (oh * gh).reshape(b, s, w).astype(gate.dtype)


def causal_short_conv(x, w):
    s = x.shape[1]
    xp = jnp.pad(x, ((0, 0), (CONV_W - 1, 0), (0, 0)))
    y = xp[:, 0:s] * w[0]
    for k in range(1, CONV_W):
        y = y + xp[:, k:k + s] * w[k]
    return y


def gated_delta_rule(q, k, v, beta, log_decay):
    c = q.shape[-2]
    incl = jnp.tril(jnp.ones((c, c), bool))
    strict = jnp.tril(jnp.ones((c, c), bool), -1)
    cum = jnp.cumsum(log_decay, axis=-1)
    diff = cum[..., :, None] - cum[..., None, :]
    decay_mat = masked_exp(diff, incl)
    kk = jnp.einsum('bhnid,bhnjd->bhnij', k, k)
    a_strict = jnp.where(strict, beta[..., :, None] * kk * decay_mat, 0.0)
    t_mat = a_strict + jnp.eye(c, dtype=q.dtype)
    rhs = jnp.concatenate([v * beta[..., None], k * (beta * jnp.exp(cum))[..., None]], -1)
    sol = lax.linalg.triangular_solve(t_mat, rhs, left_side=True, lower=True, unit_diagonal=True)
    u, w = sol[..., :v.shape[-1]], sol[..., v.shape[-1]:]
    attn = jnp.einsum('bhnid,bhnjd->bhnij', q, k) * decay_mat
    q_inter = q * jnp.exp(cum)[..., None]
    k_state = k * jnp.exp(cum[..., -1:] - cum)[..., None]
    last = jnp.exp(cum[..., -1])

    def step(s, xs):
        qi_c, u_c, w_c, attn_c, ks_c, last_c = xs
        v_new = u_c - jnp.einsum('bhcd,bhde->bhce', w_c, s)
        o = jnp.einsum('bhcd,bhde->bhce', qi_c, s) + jnp.einsum('bhij,bhje->bhie', attn_c, v_new)
        s = s * last_c[..., None, None] + jnp.einsum('bhcd,bhce->bhde', ks_c, v_new)
        return s, o

    b, h = q.shape[0], q.shape[1]
    s0 = jnp.zeros((b, h, q.shape[-1], v.shape[-1]), q.dtype)
    xs = tuple(jnp.moveaxis(a, 2, 0) for a in (q_inter, u, w, attn, k_state, last))
    _, o = lax.scan(step, s0, xs)
    return jnp.moveaxis(o, 0, 2)


def diag_decay_linear_attention(q, k, v, log_a):
    c = q.shape[-2]
    incl = jnp.tril(jnp.ones((c, c), bool))[:, :, None]
    cum = jnp.cumsum(log_a, axis=-2)
    last = cum[..., -1:, :]
    q_inter = q * jnp.exp(cum)
    k_state = k * jnp.exp(last - cum)
    decay_last = jnp.exp(last[..., 0, :])

    def step(s, xs):
        q_c, k_c, v_c, cum_c, qi_c, ks_c, dl_c = xs
        diff = cum_c[..., :, None, :] - cum_c[..., None, :, :]
        rel = masked_exp(diff, incl)
        attn = jnp.einsum('bhid,bhjd,bhijd->bhij', q_c, k_c, rel)
        o = jnp.einsum('bhij,bhje->bhie', attn, v_c) + jnp.einsum('bhid,bhde->bhie', qi_c, s)
        s = s * dl_c[..., :, None] + jnp.einsum('bhjd,bhje->bhde', ks_c, v_c)
        return s, o

    b, h = q.shape[0], q.shape[1]
    s0 = jnp.zeros((b, h, q.shape[-1], v.shape[-1]), q.dtype)
    xs = tuple(jnp.moveaxis(a, 2, 0) for a in (q, k, v, cum, q_inter, k_state, decay_last))
    _, o = lax.scan(step, s0, xs)
    return jnp.moveaxis(o, 0, 2)


def hybrid_mixer(x, lower_bound, w_in, gdn_conv, gdn_a_log, gdn_dt_bias, gdn_norm, gla_w2, gla_b2,
                 gla_norm, hgrn_norm, w_br_a, w_br_b, w_br_c, w_out):
    f32 = jnp.float32
    proj = x @ w_in
    (a_q, a_k, a_v, a_beta, a_dt, a_g, b_q, b_k, b_v, b_lr, b_g,
     c_q, c_f, c_i, c_g, m_a, m_b, m_c) = jnp.split(proj, SPLIT_POINTS, axis=-1)

    qkv = jax.nn.silu(causal_short_conv(jnp.concatenate([a_q, a_k, a_v], -1), gdn_conv)).astype(f32)
    aq, ak, av = jnp.split(qkv, (GDN_QK, 2 * GDN_QK), axis=-1)
    aq = l2_normalize(to_chunks(aq, GDN_HEADS)) * (GDN_DK ** -0.5)
    ak = l2_normalize(to_chunks(ak, GDN_HEADS))
    av = to_chunks(av, GDN_HEADS)
    beta = scalar_to_chunks(jax.nn.sigmoid(a_beta.astype(f32)))
    g_a = -jnp.exp(gdn_a_log.astype(f32)) * jax.nn.softplus(a_dt.astype(f32) + gdn_dt_bias.astype(f32))
    o_a = from_chunks(gated_delta_rule(aq, ak, av, beta, scalar_to_chunks(g_a)))
    o_a = gated_rms_heads(o_a, a_g, gdn_norm, GDN_HEADS)

    bq = to_chunks(b_q.astype(f32), GLA_HEADS) * (GLA_DK ** -0.5)
    bk = to_chunks(b_k.astype(f32), GLA_HEADS)
    bv = to_chunks(b_v.astype(f32), GLA_HEADS)
    log_alpha = jax.nn.log_sigmoid((b_lr @ gla_w2 + gla_b2).astype(f32)) / GLA_NORMALIZER
    o_b = from_chunks(diag_decay_linear_attention(bq, bk, bv, to_chunks(log_alpha, GLA_HEADS)))
    o_b = gated_rms_heads(o_b, b_g, gla_norm, GLA_HEADS)

    cf = c_f.astype(f32)
    lb = lower_bound.astype(f32)
    log_lb = jnp.log(jnp.maximum(lb, LB_FLOOR))
    log_f = jnp.logaddexp(log_lb, jnp.log1p(-lb) + jax.nn.log_sigmoid(cf))
    k_c = (1.0 - lb) * jax.nn.sigmoid(-cf)
    cq = to_chunks(jax.nn.silu(c_q.astype(f32)), HGRN_HEADS) * (HGRN_EXPAND ** -0.5)
    o_c = from_chunks(diag_decay_linear_attention(cq, to_chunks(k_c, HGRN_HEADS),
                                                  to_chunks(c_i.astype(f32), HGRN_HEADS),
                                                  to_chunks(log_f, HGRN_HEADS)))
    o_c = gated_rms_heads(o_c, c_g, hgrn_norm, HGRN_HEADS)

    y = (jax.nn.sigmoid(m_a) * (o_a @ w_br_a) + jax.nn.sigmoid(m_b) * (o_b @ w_br_b)
         + jax.nn.sigmoid(m_c) * (o_c @ w_br_c))
    return y @ w_out


def grouped_moe(x, w_router, router_bias, w_gate, w_up, w_down):
    b, s, d = x.shape
    t = x.reshape(b * s, d)
    scores = jax.nn.sigmoid((t @ w_router).astype(jnp.float32))
    sel = (scores + router_bias.astype(jnp.float32)).reshape(-1, N_GROUPS, EXPERTS_PER_GROUP)
    group_score = lax.top_k(sel, TOP_K)[0].sum(-1)
    g_idx = jnp.argmax(group_score, axis=-1)
    in_group = jnp.take_along_axis(sel, g_idx[:, None, None], axis=1)[:, 0]
    _, e_local = lax.top_k(in_group, TOP_K)
    e_idx = g_idx[:, None] * EXPERTS_PER_GROUP + e_local
    w_sel = jnp.take_along_axis(scores, e_idx, axis=1)
    w_sel = w_sel / jnp.sum(w_sel, -1, keepdims=True)
    combine = jnp.sum(jax.nn.one_hot(e_idx, N_EXPERTS, dtype=jnp.float32) * w_sel[..., None], 1)
    h = jax.nn.silu(jnp.einsum('td,edf->tef', t, w_gate)) * jnp.einsum('td,edf->tef', t, w_up)
    y = jnp.einsum('tef,efd->td', h * combine[..., None].astype(h.dtype), w_down)
    return y.reshape(b, s, d)


def setup_inputs(seed: int = 0) -> dict:
    key = jax.random.key(seed)
    ks = jax.random.split(key, 26)
    f32 = jnp.float32

    def nrm(k, shape, scale):
        return jax.random.normal(k, shape, f32) * scale

    dt = jnp.exp(jax.random.uniform(ks[6], (DEPTH, GDN_HEADS), f32, math.log(1e-3), math.log(1e-1)))
    return {
        "x": nrm(ks[0], (BATCH, SEQ, D_MODEL), 1.0),
        "ln0_g": 1.0 + nrm(ks[1], (D_MODEL,), 0.02),
        "ln0_b": nrm(ks[2], (D_MODEL,), 0.02),
        "w_in": nrm(ks[3], (DEPTH, D_MODEL, IN_WIDTH), D_MODEL ** -0.5),
        "gdn_conv": nrm(ks[4], (DEPTH, CONV_W, 2 * GDN_QK + GDN_V), CONV_W ** -0.5),
        "gdn_a_log": jnp.log(jax.random.uniform(ks[5], (DEPTH, GDN_HEADS), f32, 1.0, 16.0)),
        "gdn_dt_bias": dt + jnp.log(-jnp.expm1(-dt)),
        "gdn_norm": 1.0 + nrm(ks[7], (DEPTH, GDN_DV), 0.02),
        "gla_w2": nrm(ks[8], (DEPTH, GLA_RANK, GLA_QK), GLA_RANK ** -0.5),
        "gla_b2": nrm(ks[9], (DEPTH, GLA_QK), 0.1),
        "gla_norm": 1.0 + nrm(ks[10], (DEPTH, GLA_DV), 0.02),
        "hgrn_lb_logits": nrm(ks[11], (DEPTH, HGRN_QK), 0.5),
        "hgrn_norm": 1.0 + nrm(ks[12], (DEPTH, HGRN_DV), 0.02),
        "w_br_a": nrm(ks[13], (DEPTH, GDN_V, D_MODEL), BETA * GDN_V ** -0.5),
        "w_br_b": nrm(ks[14], (DEPTH, GLA_V, D_MODEL), BETA * GLA_V ** -0.5),
        "w_br_c": nrm(ks[15], (DEPTH, HGRN_V, D_MODEL), BETA * HGRN_V ** -0.5),
        "w_out": nrm(ks[16], (DEPTH, D_MODEL, D_MODEL), BETA * D_MODEL ** -0.5),
        "ln1_g": 1.0 + nrm(ks[17], (DEPTH, D_MODEL), 0.02),
        "ln1_b": nrm(ks[18], (DEPTH, D_MODEL), 0.02),
        "w_router": nrm(ks[19], (D_MODEL, N_EXPERTS), D_MODEL ** -0.5),
        "router_bias": nrm(ks[20], (N_EXPERTS,), 0.01),
        "w_gate": nrm(ks[21], (DEPTH, N_EXPERTS, D_MODEL, D_FF), BETA * D_MODEL ** -0.5),
        "w_up": nrm(ks[22], (DEPTH, N_EXPERTS, D_MODEL, D_FF), BETA * D_MODEL ** -0.5),
        "w_down": nrm(ks[23], (DEPTH, N_EXPERTS, D_FF, D_MODEL), BETA * D_FF ** -0.5),
        "ln2_g": 1.0 + nrm(ks[24], (DEPTH, D_MODEL), 0.02),
        "ln2_b": nrm(ks[25], (DEPTH, D_MODEL), 0.02),
    }


def reference(x, ln0_g, ln0_b, w_in, gdn_conv, gdn_a_log, gdn_dt_bias, gdn_norm, gla_w2, gla_b2,
              gla_norm, hgrn_lb_logits, hgrn_norm, w_br_a, w_br_b, w_br_c, w_out, ln1_g, ln1_b,
              w_router, router_bias, w_gate, w_up, w_down, ln2_g, ln2_b):
    p = jax.nn.softmax(hgrn_lb_logits.astype(jnp.float32), axis=0)
    lower_bounds = jnp.clip(jnp.cumsum(p, axis=0) - p[0], 0.0, 1.0)
    h = layer_norm(x, ln0_g, ln0_b)
    for l in range(DEPTH):
        mix = hybrid_mixer(h, lower_bounds[l], w_in[l], gdn_conv[l], gdn_a_log[l], gdn_dt_bias[l],
                           gdn_norm[l], gla_w2[l], gla_b2[l], gla_norm[l], hgrn_norm[l],
                           w_br_a[l], w_br_b[l], w_br_c[l], w_out[l])
        h = layer_norm(ALPHA * h + mix, ln1_g[l], ln1_b[l])
        ffn = grouped_moe(h, w_router, router_bias, w_gate[l], w_up[l], w_down[l])
        h = layer_norm(ALPHA * h + ffn, ln2_g[l], ln2_b[l])
    return h
```

```python
import functools

import numpy as np
import jax
import jax.numpy as jnp
from jax import lax
from jax.experimental import pallas as pl
from jax.experimental.pallas import tpu as pltpu

F32 = jnp.float32
BF16 = jnp.bfloat16
HIGHEST = lax.Precision.HIGHEST

D_MODEL = 1024
DEPTH = 4
CHUNK = 64
GDN_HEADS, GDN_DK, GDN_DV, CONV_W = 4, 128, 128, 4
GLA_HEADS, GLA_DK, GLA_DV, GLA_RANK, GLA_NORMALIZER = 4, 64, 128, 16, 16.0
HGRN_HEADS, HGRN_EXPAND, HGRN_DV = 4, 128, 128
LB_FLOOR = 1e-30
N_EXPERTS, N_GROUPS, TOP_K, D_FF = 16, 4, 2, 256
EXPERTS_PER_GROUP = N_EXPERTS // N_GROUPS
ALPHA = (2.0 * DEPTH) ** 0.25
LN_EPS = 1e-5
RMS_EPS = 1e-6

GDN_QK = GDN_HEADS * GDN_DK
GDN_V = GDN_HEADS * GDN_DV
GLA_QK = GLA_HEADS * GLA_DK
GLA_V = GLA_HEADS * GLA_DV
HGRN_QK = HGRN_HEADS * HGRN_EXPAND
HGRN_V = HGRN_HEADS * HGRN_DV
SPLIT_SIZES = (GDN_QK, GDN_QK, GDN_V, GDN_HEADS, GDN_HEADS, GDN_V,
               GLA_QK, GLA_QK, GLA_V, GLA_RANK, GLA_V,
               HGRN_QK, HGRN_QK, HGRN_V, HGRN_V,
               D_MODEL, D_MODEL, D_MODEL)
SPLIT_POINTS = tuple(int(v) for v in np.cumsum(SPLIT_SIZES)[:-1])

LANE = 128
SUBLANE = 8
VMEM_LIMIT = 48 * 1024 * 1024

M_A, M_B, M_C = 0, 1024, 2048
A_Q, A_K, A_V, A_G, A_S = 3072, 3584, 4096, 4608, 5120
B_Q, B_K, B_LR, B_V, B_G = 5248, 5504, 5760, 5888, 6400
C_Q, C_F, C_I, C_G = 6912, 7424, 7936, 8448
NP = 8960

SUB = 8
LEVELS = (64, 32, 16, 8)


def _cparams(sem):
    return pltpu.CompilerParams(dimension_semantics=sem, vmem_limit_bytes=VMEM_LIMIT)


def _mm(a, b):
    return jnp.dot(a.astype(BF16), b.astype(BF16), preferred_element_type=F32)


def _mm_nt(a, b):
    return lax.dot_general(a.astype(BF16), b.astype(BF16), (((1,), (1,)), ((), ())),
                           preferred_element_type=F32)


def _mm_tn(a, b):
    return lax.dot_general(a.astype(BF16), b.astype(BF16), (((0,), (0,)), ((), ())),
                           preferred_element_type=F32)


def _mm_f32(a, b):
    return jnp.dot(a, b, preferred_element_type=F32, precision=HIGHEST)


def _mm_nt_f32(a, b):
    return lax.dot_general(a, b, (((1,), (1,)), ((), ())), preferred_element_type=F32,
                           precision=HIGHEST)


def _split3(x):
    hi = x.astype(BF16)
    r = x - hi.astype(F32)
    mid = r.astype(BF16)
    lo = (r - mid.astype(F32)).astype(BF16)
    return hi, mid, lo


def _sigmoid(x):
    return 1.0 / (1.0 + jnp.exp(-x))


def _silu(x):
    return x * _sigmoid(x)


def _softplus(x):
    return jnp.maximum(x, 0.0) + jnp.log1p(jnp.exp(-jnp.abs(x)))


def _log_sigmoid(x):
    return -_softplus(-x)


def _layer_norm(x, g, b):
    mu = jnp.mean(x, axis=-1, keepdims=True)
    xc = x - mu
    var = jnp.mean(xc * xc, axis=-1, keepdims=True)
    return xc * lax.rsqrt(var + LN_EPS) * g + b


def _ln_kernel(x_ref, g_ref, b_ref, o_ref):
    o_ref[...] = _layer_norm(x_ref[...], g_ref[...], b_ref[...])


def _ln_call(x, g, b, tm=512):
    t, d = x.shape
    tm = min(tm, t)
    return pl.pallas_call(
        _ln_kernel,
        grid=(t // tm,),
        in_specs=[pl.BlockSpec((tm, d), lambda i: (i, 0)),
                  pl.BlockSpec((1, d), lambda i: (0, 0)),
                  pl.BlockSpec((1, d), lambda i: (0, 0))],
        out_specs=pl.BlockSpec((tm, d), lambda i: (i, 0)),
        out_shape=jax.ShapeDtypeStruct((t, d), F32),
        compiler_params=_cparams(("parallel",)),
        name="ln0",
    )(x, g.reshape(1, d), b.reshape(1, d))


def _inproj_kernel(x_ref, w_ref, o_ref):
    o_ref[...] = jnp.dot(x_ref[...].astype(BF16), w_ref[...], preferred_element_type=F32)


def _inproj_call(h, w_packed, layer, tm=512, tn=1792):
    t, d = h.shape
    tm = min(tm, t)
    return pl.pallas_call(
        _inproj_kernel,
        grid=(NP // tn, t // tm),
        in_specs=[pl.BlockSpec((tm, d), lambda j, i: (i, 0)),
                  pl.BlockSpec((None, d, tn), lambda j, i: (layer, 0, j))],
        out_specs=pl.BlockSpec((tm, tn), lambda j, i: (i, j)),
        out_shape=jax.ShapeDtypeStruct((t, NP), F32),
        compiler_params=_cparams(("parallel", "parallel")),
        name="inproj",
    )(h, w_packed)


def _lane_col(x, idx):
    lane = lax.broadcasted_iota(jnp.int32, x.shape, 1)
    return jnp.sum(jnp.where(lane == idx, x, 0.0), axis=1, keepdims=True)


def _gdn_kernel(q_ref, k_ref, v_ref, s_ref, gate_ref, cwq_ref, cwk_ref, cwv_ref, par_ref, nw_ref,
                o_ref, state_ref, tail_ref, xbuf_ref, q_s, k_s, v_s, b_s, g_s, o_s, *, tb):
    h = pl.program_id(1)
    blk = pl.program_id(2)
    nchunk = tb // CHUNK

    @pl.when(blk == 0)
    def _():
        state_ref[...] = jnp.zeros_like(state_ref)
        tail_ref[...] = jnp.zeros_like(tail_ref)

    def conv(x_ref, cw_ref, slot):
        xbuf_ref[0:SUBLANE, :] = tail_ref[slot]
        xbuf_ref[SUBLANE:SUBLANE + tb, :] = x_ref[...]
        tail_ref[slot] = x_ref[tb - SUBLANE:tb, :]
        y = xbuf_ref[SUBLANE - (CONV_W - 1):SUBLANE - (CONV_W - 1) + tb, :] * cw_ref[0:1, :]
        for kk in range(1, CONV_W):
            off = SUBLANE - (CONV_W - 1) + kk
            y = y + xbuf_ref[off:off + tb, :] * cw_ref[kk:kk + 1, :]
        return _silu(y)

    q = conv(q_ref, cwq_ref, 0)
    k = conv(k_ref, cwk_ref, 1)
    v = conv(v_ref, cwv_ref, 2)
    q = q * lax.rsqrt(jnp.sum(q * q, axis=-1, keepdims=True) + RMS_EPS) * (GDN_DK ** -0.5)
    k = k * lax.rsqrt(jnp.sum(k * k, axis=-1, keepdims=True) + RMS_EPS)
    q_s[...] = q
    k_s[...] = k
    v_s[...] = v

    sc = s_ref[...]
    beta = _sigmoid(_lane_col(sc, h))
    a_log = par_ref[pl.ds(h, 1), :]
    dt_bias = par_ref[pl.ds(GDN_HEADS + h, 1), :]
    g = -jnp.exp(a_log) * _softplus(_lane_col(sc, GDN_HEADS + h) + dt_bias)
    b_s[...] = jnp.broadcast_to(beta, (tb, LANE))
    g_s[...] = g

    row = lax.broadcasted_iota(jnp.int32, (CHUNK, CHUNK), 0)
    col = lax.broadcasted_iota(jnp.int32, (CHUNK, CHUNK), 1)
    incl = col <= row
    strict = col < row
    tri = jnp.where(incl, 1.0, 0.0).astype(F32)
    eye = jnp.where(col == row, 1.0, 0.0).astype(F32)
    lane = lax.broadcasted_iota(jnp.int32, (CHUNK, LANE), 1)

    def chunk_body(c, carry):
        r0 = pl.multiple_of(c * CHUNK, CHUNK)
        qc = q_s[pl.ds(r0, CHUNK), :]
        kc = k_s[pl.ds(r0, CHUNK), :]
        vc = v_s[pl.ds(r0, CHUNK), :]
        bc = b_s[pl.ds(r0, CHUNK), :]
        gc = g_s[pl.ds(r0, CHUNK), :]
        cum = _mm_f32(tri, gc)
        lhs = jnp.where(lane == 0, cum, jnp.where(lane == 1, 1.0, 0.0))
        rhs = jnp.where(lane == 0, 1.0, jnp.where(lane == 1, -cum, 0.0))
        diff = _mm_nt_f32(lhs, rhs)
        decay = jnp.where(incl, jnp.exp(jnp.where(incl, diff, 0.0)), 0.0)
        kkm = _mm_nt(kc, kc)
        a_strict = jnp.where(strict, bc[:, 0:CHUNK] * kkm * decay, 0.0)
        x = eye - a_strict
        p = _mm_f32(a_strict, a_strict)
        for it in range(5):
            x = x + _mm_f32(x, p)
            if it < 4:
                p = _mm_f32(p, p)
        ecum = jnp.exp(cum)
        rhs_cat = jnp.concatenate([vc * bc, kc * (bc * ecum)], axis=1)
        sol = _mm_f32(x, rhs_cat)
        u = sol[:, 0:GDN_DV]
        w = sol[:, GDN_DV:GDN_DV + GDN_DK]
        attn = _mm_nt(qc, kc) * decay
        cum_last = cum[CHUNK - 1:CHUNK, :]
        q_inter = qc * ecum
        k_state = kc * jnp.exp(cum_last - cum)
        last = jnp.exp(cum_last)
        s = state_ref[...]
        v_new = u - _mm(w, s)
        o = _mm(q_inter, s) + _mm(attn, v_new)
        state_ref[...] = s * last + _mm_tn(k_state, v_new)
        o_s[pl.ds(r0, CHUNK), :] = o
        return carry

    lax.fori_loop(0, nchunk, chunk_body, 0)

    o = o_s[...]
    oh = o * lax.rsqrt(jnp.mean(o * o, axis=-1, keepdims=True) + RMS_EPS) * nw_ref[...]
    o_ref[...] = oh * _silu(gate_ref[...])


def _gdn_call(proj, conv_w, par, norm_w, layer, batch, seq, tb=512):
    tb = min(tb, seq)
    nb = seq // tb
    t = batch * seq
    cb = LANE

    def tok(colblk):
        return pl.BlockSpec((tb, cb), lambda b, h, i: (b * nb + i, colblk + h))

    def cw(colblk):
        return pl.BlockSpec((None, CONV_W, cb), lambda b, h, i: (layer, 0, colblk + h))

    kern = functools.partial(_gdn_kernel, tb=tb)
    return pl.pallas_call(
        kern,
        grid=(batch, GDN_HEADS, nb),
        in_specs=[tok(A_Q // cb), tok(A_K // cb), tok(A_V // cb),
                  pl.BlockSpec((tb, cb), lambda b, h, i: (b * nb + i, A_S // cb)),
                  tok(A_G // cb),
                  cw(0), cw(GDN_QK // cb), cw(2 * GDN_QK // cb),
                  pl.BlockSpec((None, SUBLANE, LANE), lambda b, h, i: (layer, 0, 0)),
                  pl.BlockSpec((None, 1, GDN_DV), lambda b, h, i: (layer, 0, 0))],
        out_specs=pl.BlockSpec((tb, cb), lambda b, h, i: (b * nb + i, h)),
        out_shape=jax.ShapeDtypeStruct((t, GDN_V), F32),
        scratch_shapes=[pltpu.VMEM((GDN_DK, GDN_DV), F32),
                        pltpu.VMEM((3, SUBLANE, cb), F32),
                        pltpu.VMEM((tb + SUBLANE, cb), F32),
                        pltpu.VMEM((tb, cb), F32), pltpu.VMEM((tb, cb), F32), pltpu.VMEM((tb, cb), F32),
                        pltpu.VMEM((tb, LANE), F32), pltpu.VMEM((tb, LANE), F32),
                        pltpu.VMEM((tb, cb), F32)],
        compiler_params=_cparams(("parallel", "parallel", "arbitrary")),
        name="gdn",
    )(proj, proj, proj, proj, proj, conv_w, conv_w, conv_w, par, norm_w)


def _level_matrix():
    i = np.arange(CHUNK)[:, None]
    j = np.arange(CHUNK)[None, :]
    mats = []
    for b in LEVELS:
        mats.append(((j <= i) & (j // b == i // b)).astype(np.float32))
    for b in LEVELS:
        mats.append(((j > i) & (j // b == i // b)).astype(np.float32))
    return np.concatenate(mats, axis=0)


def _dd_core(q_s, k_s, v_ref, la_s, o_s, state_ref, lm_ref, c8_s, p_s, *, tb, g_heads):
    nchunk = tb // CHUNK
    dkh = LANE // g_heads
    dvp = g_heads * LANE
    row = lax.broadcasted_iota(jnp.int32, (CHUNK, CHUNK), 0)
    col = lax.broadcasted_iota(jnp.int32, (CHUNK, CHUNK), 1)
    level_masks = []
    for sh in (5, 4, 3):
        same2b = jnp.right_shift(row, sh + 1) == jnp.right_shift(col, sh + 1)
        upper = (jnp.right_shift(row, sh) & 1) == 1
        lower = (jnp.right_shift(col, sh) & 1) == 0
        level_masks.append(jnp.where(same2b, jnp.where(upper, jnp.where(lower, 1.0, 0.0), 0.0), 0.0))
    lane128 = lax.broadcasted_iota(jnp.int32, (CHUNK, LANE), 1)
    head_masks = [jnp.where((lane128 >= g * dkh) & (lane128 < (g + 1) * dkh), 1.0, 0.0)
                  for g in range(g_heads)]
    sub = lax.broadcasted_iota(jnp.int32, (SUB, LANE), 0)
    dk_sh = dkh.bit_length() - 1
    lane_sh = LANE.bit_length() - 1
    orow = lax.broadcasted_iota(jnp.int32, (LANE, dvp), 0)
    ocol = lax.broadcasted_iota(jnp.int32, (LANE, dvp), 1)
    ones_bd = jnp.where(jnp.right_shift(orow, dk_sh) == jnp.right_shift(ocol, lane_sh), 1.0, 0.0).astype(BF16)
    srow = lax.broadcasted_iota(jnp.int32, (dvp, LANE), 0)
    scol = lax.broadcasted_iota(jnp.int32, (dvp, LANE), 1)
    state_mask = jnp.where(jnp.right_shift(srow, lane_sh) == jnp.right_shift(scol, dk_sh), 1.0, 0.0)
    lm = lm_ref[...]

    def chunk_body(c, carry):
        r0 = pl.multiple_of(c * CHUNK, CHUNK)
        qc = q_s[pl.ds(r0, CHUNK), :]
        kc = k_s[pl.ds(r0, CHUNK), :]
        vc = v_ref[pl.ds(r0, CHUNK), :]
        la = la_s[pl.ds(r0, CHUNK), :]
        hi, mid, lo = _split3(la)
        e = (jnp.dot(lm, hi, preferred_element_type=F32) + jnp.dot(lm, mid, preferred_element_type=F32)
             + jnp.dot(lm, lo, preferred_element_type=F32))
        nl = len(LEVELS)
        cpre = [e[l * CHUNK:(l + 1) * CHUNK, :] for l in range(nl)]
        csuf = [e[(nl + l) * CHUNK:(nl + l + 1) * CHUNK, :] for l in range(nl)]
        attn = [jnp.zeros((CHUNK, CHUNK), F32) for _ in range(g_heads)]
        for li in range(3):
            qs = qc * jnp.exp(cpre[li + 1])
            ks = kc * jnp.exp(csuf[li + 1])
            for g in range(g_heads):
                qg = qs * head_masks[g] if g_heads > 1 else qs
                attn[g] = attn[g] + _mm_nt(qg, ks) * level_masks[li]
        o = jnp.concatenate([_mm(attn[g], vc[:, g * LANE:(g + 1) * LANE]) for g in range(g_heads)], axis=1) \
            if g_heads > 1 else _mm(attn[0], vc)
        c8 = cpre[3]
        c8_s[...] = c8
        for r in range(CHUNK // SUB):
            qr = qc[r * SUB:(r + 1) * SUB, :]
            cr = c8[r * SUB:(r + 1) * SUB, :]
            for jj in range(SUB):
                krow = k_s[pl.ds(r0 + r * SUB + jj, 1), :]
                crow = c8_s[r * SUB + jj:r * SUB + jj + 1, :]
                m = sub >= jj
                pr = jnp.where(m, qr * krow * jnp.exp(jnp.where(m, cr - crow, 0.0)), 0.0)
                p_s[(r * SUB + jj) * SUB:(r * SUB + jj + 1) * SUB, :] = pr
        pp = p_s[...]
        p_hi = pp.astype(BF16)
        p_lo = (pp - p_hi.astype(F32)).astype(BF16)
        rs = (jnp.dot(p_hi, ones_bd, preferred_element_type=F32)
              + jnp.dot(p_lo, ones_bd, preferred_element_type=F32))
        od = []
        for r in range(CHUNK // SUB):
            acc = jnp.zeros((SUB, dvp), F32)
            for jj in range(SUB):
                vrow = v_ref[pl.ds(r0 + r * SUB + jj, 1), :]
                acc = acc + rs[(r * SUB + jj) * SUB:(r * SUB + jj + 1) * SUB, :] * vrow
            od.append(acc)
        o = o + jnp.concatenate(od, axis=0)
        st = state_ref[...]
        q_inter = qc * jnp.exp(cpre[0])
        k_state = kc * jnp.exp(csuf[0])
        decay_last = jnp.exp(cpre[0][CHUNK - 1:CHUNK, :])
        o = o + _mm_nt(q_inter, st)
        upd = _mm_tn(vc, k_state)
        if g_heads > 1:
            upd = upd * state_mask
        state_ref[...] = st * decay_last + upd
        o_s[pl.ds(r0, CHUNK), :] = o
        return carry

    lax.fori_loop(0, nchunk, chunk_body, 0)


def _gated_rms_out(o_s, gate_ref, nw_ref, o_ref, g_heads):
    for g in range(g_heads):
        o = o_s[:, g * LANE:(g + 1) * LANE]
        oh = o * lax.rsqrt(jnp.mean(o * o, axis=-1, keepdims=True) + RMS_EPS) * nw_ref[...]
        o_ref[:, g * LANE:(g + 1) * LANE] = oh * _silu(gate_ref[:, g * LANE:(g + 1) * LANE])


def _gla_kernel(q_ref, k_ref, v_ref, lr_ref, gate_ref, w2_ref, b2_ref, nw_ref, lm_ref,
                o_ref, state_ref, q_s, k_s, la_s, o_s, c8_s, p_s, *, tb):
    @pl.when(pl.program_id(2) == 0)
    def _():
        state_ref[...] = jnp.zeros_like(state_ref)

    q_s[...] = q_ref[...] * (GLA_DK ** -0.5)
    k_s[...] = k_ref[...]
    z = _mm(lr_ref[...], w2_ref[...]) + b2_ref[...]
    la_s[...] = _log_sigmoid(z) * (1.0 / GLA_NORMALIZER)
    _dd_core(q_s, k_s, v_ref, la_s, o_s, state_ref, lm_ref, c8_s, p_s, tb=tb, g_heads=2)
    _gated_rms_out(o_s, gate_ref, nw_ref, o_ref, 2)


def _gla_call(proj, w2p, b2, norm_w, lm, layer, batch, seq, tb=512):
    tb = min(tb, seq)
    nb = seq // tb
    t = batch * seq
    npair = GLA_HEADS // 2
    kern = functools.partial(_gla_kernel, tb=tb)
    return pl.pallas_call(
        kern,
        grid=(batch, npair, nb),
        in_specs=[pl.BlockSpec((tb, LANE), lambda b, p, i: (b * nb + i, B_Q // LANE + p)),
                  pl.BlockSpec((tb, LANE), lambda b, p, i: (b * nb + i, B_K // LANE + p)),
                  pl.BlockSpec((tb, 2 * LANE), lambda b, p, i: (b * nb + i, B_V // (2 * LANE) + p)),
                  pl.BlockSpec((tb, LANE), lambda b, p, i: (b * nb + i, B_LR // LANE)),
                  pl.BlockSpec((tb, 2 * LANE), lambda b, p, i: (b * nb + i, B_G // (2 * LANE) + p)),
                  pl.BlockSpec((None, LANE, LANE), lambda b, p, i: (layer, 0, p)),
                  pl.BlockSpec((None, 1, LANE), lambda b, p, i: (layer, 0, p)),
                  pl.BlockSpec((None, 1, GLA_DV), lambda b, p, i: (layer, 0, 0)),
                  pl.BlockSpec((2 * len(LEVELS) * CHUNK, CHUNK), lambda b, p, i: (0, 0))],
        out_specs=pl.BlockSpec((tb, 2 * LANE), lambda b, p, i: (b * nb + i, p)),
        out_shape=jax.ShapeDtypeStruct((t, GLA_V), F32),
        scratch_shapes=[pltpu.VMEM((2 * LANE, LANE), F32),
                        pltpu.VMEM((tb, LANE), F32), pltpu.VMEM((tb, LANE), F32), pltpu.VMEM((tb, LANE), F32),
                        pltpu.VMEM((tb, 2 * LANE), F32),
                        pltpu.VMEM((CHUNK, LANE), F32),
                        pltpu.VMEM((CHUNK * SUB, LANE), F32)],
        compiler_params=_cparams(("parallel", "parallel", "arbitrary")),
        name="gla",
    )(proj, proj, proj, proj, proj, w2p, b2, norm_w, lm)


def _hgrn_kernel(q_ref, f_ref, v_ref, gate_ref, lbl_ref, nw_ref, lm_ref,
                 o_ref, state_ref, q_s, k_s, la_s, o_s, c8_s, p_s, *, tb, layer):
    @pl.when(pl.program_id(2) == 0)
    def _():
        state_ref[...] = jnp.zeros_like(state_ref)

    logits = lbl_ref[...]
    mx = jnp.max(logits, axis=0, keepdims=True)
    ex = jnp.exp(logits - mx)
    p = ex / jnp.sum(ex, axis=0, keepdims=True)
    acc = p[0:1, :]
    for r in range(1, layer + 1):
        acc = acc + p[r:r + 1, :]
    lb = jnp.clip(acc - p[0:1, :], 0.0, 1.0)
    log_lb = jnp.log(jnp.maximum(lb, LB_FLOOR))
    log_1m = jnp.log1p(-lb)

    cf = f_ref[...]
    second = log_1m + _log_sigmoid(cf)
    mxab = jnp.maximum(log_lb, second)
    la_s[...] = mxab + jnp.log1p(jnp.exp(-jnp.abs(log_lb - second)))
    k_s[...] = (1.0 - lb) * _sigmoid(-cf)
    q_s[...] = _silu(q_ref[...]) * (HGRN_EXPAND ** -0.5)
    _dd_core(q_s, k_s, v_ref, la_s, o_s, state_ref, lm_ref, c8_s, p_s, tb=tb, g_heads=1)
    _gated_rms_out(o_s, gate_ref, nw_ref, o_ref, 1)


def _hgrn_call(proj, lb_logits, norm_w, lm, layer, batch, seq, tb=512):
    tb = min(tb, seq)
    nb = seq // tb
    t = batch * seq

    def tok(colblk):
        return pl.BlockSpec((tb, LANE), lambda b, h, i: (b * nb + i, colblk + h))

    kern = functools.partial(_hgrn_kernel, tb=tb, layer=layer)
    return pl.pallas_call(
        kern,
        grid=(batch, HGRN_HEADS, nb),
        in_specs=[tok(C_Q // LANE), tok(C_F // LANE), tok(C_I // LANE), tok(C_G // LANE),
                  pl.BlockSpec((DEPTH, LANE), lambda b, h, i: (0, h)),
                  pl.BlockSpec((None, 1, HGRN_DV), lambda b, h, i: (layer, 0, 0)),
                  pl.BlockSpec((2 * len(LEVELS) * CHUNK, CHUNK), lambda b, h, i: (0, 0))],
        out_specs=pl.BlockSpec((tb, LANE), lambda b, h, i: (b * nb + i, h)),
        out_shape=jax.ShapeDtypeStruct((t, HGRN_V), F32),
        scratch_shapes=[pltpu.VMEM((LANE, LANE), F32),
                        pltpu.VMEM((tb, LANE), F32), pltpu.VMEM((tb, LANE), F32), pltpu.VMEM((tb, LANE), F32),
                        pltpu.VMEM((tb, LANE), F32),
                        pltpu.VMEM((CHUNK, LANE), F32),
                        pltpu.VMEM((CHUNK * SUB, LANE), F32)],
        compiler_params=_cparams(("parallel", "parallel", "arbitrary")),
        name="hgrn",
    )(proj, proj, proj, proj, lb_logits, norm_w, lm)


def _merge_kernel(oa_ref, ob_ref, oc_ref, ma_ref, mb_ref, mc_ref, h_ref,
                  wa_ref, wb_ref, wc_ref, wo_ref, g_ref, b_ref, o_ref):
    y = (_sigmoid(ma_ref[...]) * _mm(oa_ref[...], wa_ref[...])
         + _sigmoid(mb_ref[...]) * _mm(ob_ref[...], wb_ref[...])
         + _sigmoid(mc_ref[...]) * _mm(oc_ref[...], wc_ref[...]))
    mix = _mm(y, wo_ref[...])
    o_ref[...] = _layer_norm(ALPHA * h_ref[...] + mix, g_ref[...], b_ref[...])


def _merge_call(o_a, o_b, o_c, proj, h, wa, wb, wc, wo, g, b, layer, tm=256):
    t, d = h.shape
    tm = min(tm, t)
    mblk = D_MODEL

    def row(width):
        return pl.BlockSpec((tm, width), lambda i: (i, 0))

    def wspec(kdim):
        return pl.BlockSpec((None, kdim, d), lambda i: (layer, 0, 0))

    vec = pl.BlockSpec((None, 1, d), lambda i: (layer, 0, 0))
    return pl.pallas_call(
        _merge_kernel,
        grid=(t // tm,),
        in_specs=[row(GDN_V), row(GLA_V), row(HGRN_V),
                  pl.BlockSpec((tm, mblk), lambda i: (i, M_A // mblk)),
                  pl.BlockSpec((tm, mblk), lambda i: (i, M_B // mblk)),
                  pl.BlockSpec((tm, mblk), lambda i: (i, M_C // mblk)),
                  row(d), wspec(GDN_V), wspec(GLA_V), wspec(HGRN_V), wspec(d), vec, vec],
        out_specs=row(d),
        out_shape=jax.ShapeDtypeStruct((t, d), F32),
        compiler_params=_cparams(("parallel",)),
        name="merge",
    )(o_a, o_b, o_c, proj, proj, proj, h, wa, wb, wc, wo, g, b)


def _route(scores_t, bias_ref):
    s = [scores_t[e:e + 1, :] for e in range(N_EXPERTS)]
    sel = [s[e] + bias_ref[e:e + 1, 0:1] for e in range(N_EXPERTS)]
    gscore = []
    for g in range(N_GROUPS):
        a, b, c, d = sel[4 * g:4 * g + 4]
        hi1, lo1 = jnp.maximum(a, b), jnp.minimum(a, b)
        hi2, lo2 = jnp.maximum(c, d), jnp.minimum(c, d)
        top1 = jnp.maximum(hi1, hi2)
        top2 = jnp.maximum(jnp.minimum(hi1, hi2), jnp.maximum(lo1, lo2))
        gscore.append(top1 + top2)
    best = gscore[0]
    gidx = jnp.zeros_like(best, dtype=jnp.int32)
    for g in range(1, N_GROUPS):
        take = gscore[g] > best
        best = jnp.where(take, gscore[g], best)
        gidx = jnp.where(take, g, gidx)
    ing, raw = [], []
    for kk in range(EXPERTS_PER_GROUP):
        vs, vr = sel[kk], s[kk]
        for g in range(1, N_GROUPS):
            pick = gidx == g
            vs = jnp.where(pick, sel[4 * g + kk], vs)
            vr = jnp.where(pick, s[4 * g + kk], vr)
        ing.append(vs)
        raw.append(vr)
    b1 = ing[0]
    i1 = jnp.zeros_like(gidx)
    for kk in range(1, EXPERTS_PER_GROUP):
        take = ing[kk] > b1
        b1 = jnp.where(take, ing[kk], b1)
        i1 = jnp.where(take, kk, i1)
    neg = jnp.full_like(b1, -jnp.inf)
    b2 = neg
    i2 = jnp.zeros_like(gidx)
    for kk in range(EXPERTS_PER_GROUP):
        cand = jnp.where(i1 == kk, neg, ing[kk])
        take = cand > b2
        b2 = jnp.where(take, cand, b2)
        i2 = jnp.where(take, kk, i2)
    w1 = raw[0]
    w2 = raw[0]
    for kk in range(1, EXPERTS_PER_GROUP):
        w1 = jnp.where(i1 == kk, raw[kk], w1)
        w2 = jnp.where(i2 == kk, raw[kk], w2)
    tot = w1 + w2
    w1 = w1 / tot
    w2 = w2 / tot
    e1 = gidx * EXPERTS_PER_GROUP + i1
    e2 = gidx * EXPERTS_PER_GROUP + i2
    rows = [jnp.where(e1 == e, w1, 0.0) + jnp.where(e2 == e, w2, 0.0) for e in range(N_EXPERTS)]
    return jnp.concatenate(rows, axis=0)


def _moe_kernel(h_ref, wr_ref, rb_ref, wg_ref, wu_ref, wd_ref, g_ref, b_ref, o_ref, comb_ref, acc_ref, *, tm):
    e = pl.program_id(1)

    @pl.when(e == 0)
    def _():
        logits_t = _mm_nt_f32(wr_ref[...], h_ref[...])
        comb_t = _route(_sigmoid(logits_t), rb_ref)
        pad = jnp.zeros((LANE - N_EXPERTS, tm), F32)
        comb_ref[...] = jnp.transpose(jnp.concatenate([comb_t, pad], axis=0))
        acc_ref[...] = jnp.zeros_like(acc_ref)

    x = h_ref[...].astype(BF16)
    hg = jnp.dot(x, wg_ref[...], preferred_element_type=F32)
    hu = jnp.dot(x, wu_ref[...], preferred_element_type=F32)
    cw = _lane_col(comb_ref[...], e)
    hid = _silu(hg) * hu * cw
    acc_ref[...] += jnp.dot(hid.astype(BF16), wd_ref[...], preferred_element_type=F32)

    @pl.when(e == N_EXPERTS - 1)
    def _():
        o_ref[...] = _layer_norm(ALPHA * h_ref[...] + acc_ref[...], g_ref[...], b_ref[...])


def _moe_call(h, wr_t, rbias, wg, wu, wd, g, b, layer, tm=1024):
    t, d = h.shape
    tm = min(tm, t)
    kern = functools.partial(_moe_kernel, tm=tm)
    vec = pl.BlockSpec((None, 1, d), lambda i, e: (layer, 0, 0))
    return pl.pallas_call(
        kern,
        grid=(t // tm, N_EXPERTS),
        in_specs=[pl.BlockSpec((tm, d), lambda i, e: (i, 0)),
                  pl.BlockSpec((N_EXPERTS, d), lambda i, e: (0, 0)),
                  pl.BlockSpec((N_EXPERTS, LANE), lambda i, e: (0, 0)),
                  pl.BlockSpec((None, None, d, D_FF), lambda i, e: (layer, e, 0, 0)),
                  pl.BlockSpec((None, None, d, D_FF), lambda i, e: (layer, e, 0, 0)),
                  pl.BlockSpec((None, None, D_FF, d), lambda i, e: (layer, e, 0, 0)),
                  vec, vec],
        out_specs=pl.BlockSpec((tm, d), lambda i, e: (i, 0)),
        out_shape=jax.ShapeDtypeStruct((t, d), F32),
        scratch_shapes=[pltpu.VMEM((tm, LANE), F32), pltpu.VMEM((tm, d), F32)],
        compiler_params=_cparams(("parallel", "arbitrary")),
        name="moe",
    )(h, wr_t, rbias, wg, wu, wd, g, b)


def _pack_w_in(w_in):
    (a_q, a_k, a_v, a_beta, a_dt, a_g, b_q, b_k, b_v, b_lr, b_g,
     c_q, c_f, c_i, c_g, m_a, m_b, m_c) = jnp.split(w_in, SPLIT_POINTS, axis=-1)
    lead = w_in.shape[:-1]
    a_s = jnp.concatenate([a_beta, a_dt, jnp.zeros(lead + (LANE - 2 * GDN_HEADS,), w_in.dtype)], -1)
    b_lrp = jnp.concatenate([b_lr, jnp.zeros(lead + (LANE - GLA_RANK,), w_in.dtype)], -1)
    packed = jnp.concatenate([m_a, m_b, m_c, a_q, a_k, a_v, a_g, a_s, b_q, b_k, b_lrp, b_v, b_g,
                              c_q, c_f, c_i, c_g], -1)
    assert packed.shape[-1] == NP
    return packed.astype(BF16)


def _prepare(w_in, gdn_conv, gdn_a_log, gdn_dt_bias, gdn_norm, gla_w2, gla_b2, gla_norm, hgrn_lb_logits,
             hgrn_norm, w_br_a, w_br_b, w_br_c, w_out, ln1_g, ln1_b, w_router, router_bias, w_gate, w_up,
             w_down, ln2_g, ln2_b):
    depth = w_in.shape[0]
    d = w_out.shape[-1]
    return dict(
        w_packed=_pack_w_in(w_in),
        gdn_conv=gdn_conv,
        gdn_par=jnp.concatenate([jnp.broadcast_to(gdn_a_log[:, :, None], (depth, GDN_HEADS, LANE)),
                                 jnp.broadcast_to(gdn_dt_bias[:, :, None], (depth, GDN_HEADS, LANE))], axis=1),
        gdn_norm=gdn_norm.reshape(depth, 1, GDN_DV),
        w2p=jnp.concatenate([gla_w2, jnp.zeros((depth, LANE - GLA_RANK, GLA_QK), gla_w2.dtype)], axis=1),
        gla_b2=gla_b2.reshape(depth, 1, GLA_QK),
        gla_norm=gla_norm.reshape(depth, 1, GLA_DV),
        lb_logits=hgrn_lb_logits,
        hgrn_norm=hgrn_norm.reshape(depth, 1, HGRN_DV),
        lm=jnp.asarray(_level_matrix(), dtype=BF16),
        wa=w_br_a.astype(BF16), wb=w_br_b.astype(BF16), wc=w_br_c.astype(BF16), wo=w_out.astype(BF16),
        ln1_g=ln1_g.reshape(depth, 1, d), ln1_b=ln1_b.reshape(depth, 1, d),
        wr_t=jnp.transpose(w_router),
        rbias=jnp.broadcast_to(router_bias[:, None], (N_EXPERTS, LANE)),
        wg=w_gate.astype(BF16), wu=w_up.astype(BF16), wd=w_down.astype(BF16),
        ln2_g=ln2_g.reshape(depth, 1, d), ln2_b=ln2_b.reshape(depth, 1, d),
    )


def _mixer_block(h, p, layer, batch, seq):
    proj = _inproj_call(h, p["w_packed"], layer)
    o_a = _gdn_call(proj, p["gdn_conv"], p["gdn_par"], p["gdn_norm"], layer, batch, seq)
    o_b = _gla_call(proj, p["w2p"], p["gla_b2"], p["gla_norm"], p["lm"], layer, batch, seq)
    o_c = _hgrn_call(proj, p["lb_logits"], p["hgrn_norm"], p["lm"], layer, batch, seq)
    return _merge_call(o_a, o_b, o_c, proj, h, p["wa"], p["wb"], p["wc"], p["wo"], p["ln1_g"], p["ln1_b"], layer)


def _ffn_block(h, p, layer):
    return _moe_call(h, p["wr_t"], p["rbias"], p["wg"], p["wu"], p["wd"], p["ln2_g"], p["ln2_b"], layer)


def kernel(x, ln0_g, ln0_b, w_in, gdn_conv, gdn_a_log, gdn_dt_bias, gdn_norm, gla_w2, gla_b2, gla_norm,
           hgrn_lb_logits, hgrn_norm, w_br_a, w_br_b, w_br_c, w_out, ln1_g, ln1_b, w_router, router_bias,
           w_gate, w_up, w_down, ln2_g, ln2_b):
    batch, seq, d = x.shape
    p = _prepare(w_in, gdn_conv, gdn_a_log, gdn_dt_bias, gdn_norm, gla_w2, gla_b2, gla_norm, hgrn_lb_logits,
                 hgrn_norm, w_br_a, w_br_b, w_br_c, w_out, ln1_g, ln1_b, w_router, router_bias, w_gate, w_up,
                 w_down, ln2_g, ln2_b)
    h = _ln_call(x.reshape(batch * seq, d), ln0_g, ln0_b)
    for layer in range(w_in.shape[0]):
        h = _mixer_block(h, p, layer, batch, seq)
        h = _ffn_block(h, p, layer)
    return h.reshape(batch, seq, d)
```

```python
import functools

import numpy as np
import jax
import jax.numpy as jnp
from jax import lax
from jax.experimental import pallas as pl
from jax.experimental.pallas import tpu as pltpu

F32 = jnp.float32
BF16 = jnp.bfloat16
HIGHEST = lax.Precision.HIGHEST

D_MODEL = 1024
DEPTH = 4
CHUNK = 64
GDN_HEADS, GDN_DK, GDN_DV, CONV_W = 4, 128, 128, 4
GLA_HEADS, GLA_DK, GLA_DV, GLA_RANK, GLA_NORMALIZER = 4, 64, 128, 16, 16.0
HGRN_HEADS, HGRN_EXPAND, HGRN_DV = 4, 128, 128
LB_FLOOR = 1e-30
N_EXPERTS, N_GROUPS, TOP_K, D_FF = 16, 4, 2, 256
EXPERTS_PER_GROUP = N_EXPERTS // N_GROUPS
ALPHA = (2.0 * DEPTH) ** 0.25
LN_EPS = 1e-5
RMS_EPS = 1e-6

GDN_QK = GDN_HEADS * GDN_DK
GDN_V = GDN_HEADS * GDN_DV
GLA_QK = GLA_HEADS * GLA_DK
GLA_V = GLA_HEADS * GLA_DV
HGRN_QK = HGRN_HEADS * HGRN_EXPAND
HGRN_V = HGRN_HEADS * HGRN_DV
SPLIT_SIZES = (GDN_QK, GDN_QK, GDN_V, GDN_HEADS, GDN_HEADS, GDN_V,
               GLA_QK, GLA_QK, GLA_V, GLA_RANK, GLA_V,
               HGRN_QK, HGRN_QK, HGRN_V, HGRN_V,
               D_MODEL, D_MODEL, D_MODEL)
SPLIT_POINTS = tuple(int(v) for v in np.cumsum(SPLIT_SIZES)[:-1])

LANE = 128
SUBLANE = 8
VMEM_LIMIT = 48 * 1024 * 1024

M_A, M_B, M_C = 0, 1024, 2048
A_Q, A_K, A_V, A_G, A_S = 3072, 3584, 4096, 4608, 5120
B_Q, B_K, B_LR, B_V, B_G = 5248, 5504, 5760, 5888, 6400
C_Q, C_F, C_I, C_G = 6912, 7424, 7936, 8448
NP = 8960

SUB = 8
LEVELS = (64, 32, 16, 8)
DD_NPAR = 4


def _cparams(sem):
    return pltpu.CompilerParams(dimension_semantics=sem, vmem_limit_bytes=VMEM_LIMIT)


def _mm(a, b):
    return jnp.dot(a.astype(BF16), b.astype(BF16), preferred_element_type=F32)


def _mm_nt(a, b):
    return lax.dot_general(a.astype(BF16), b.astype(BF16), (((1,), (1,)), ((), ())),
                           preferred_element_type=F32)


def _mm_tn(a, b):
    return lax.dot_general(a.astype(BF16), b.astype(BF16), (((0,), (0,)), ((), ())),
                           preferred_element_type=F32)


def _mm_f32(a, b):
    return jnp.dot(a, b, preferred_element_type=F32, precision=HIGHEST)


def _mm_nt_f32(a, b):
    return lax.dot_general(a, b, (((1,), (1,)), ((), ())), preferred_element_type=F32,
                           precision=HIGHEST)


def _split3(x):
    hi = x.astype(BF16)
    r = x - hi.astype(F32)
    mid = r.astype(BF16)
    lo = (r - mid.astype(F32)).astype(BF16)
    return hi, mid, lo


def _sigmoid(x):
    return 1.0 / (1.0 + jnp.exp(-x))


def _silu(x):
    return x * _sigmoid(x)


def _softplus(x):
    return jnp.maximum(x, 0.0) + jnp.log1p(jnp.exp(-jnp.abs(x)))


def _log_sigmoid(x):
    return -_softplus(-x)


def _layer_norm(x, g, b):
    mu = jnp.mean(x, axis=-1, keepdims=True)
    xc = x - mu
    var = jnp.mean(xc * xc, axis=-1, keepdims=True)
    return xc * lax.rsqrt(var + LN_EPS) * g + b


def _ln_kernel(x_ref, g_ref, b_ref, o_ref):
    o_ref[...] = _layer_norm(x_ref[...], g_ref[...], b_ref[...])


def _ln_call(x, g, b, tm=512):
    t, d = x.shape
    tm = min(tm, t)
    return pl.pallas_call(
        _ln_kernel,
        grid=(t // tm,),
        in_specs=[pl.BlockSpec((tm, d), lambda i: (i, 0)),
                  pl.BlockSpec((1, d), lambda i: (0, 0)),
                  pl.BlockSpec((1, d), lambda i: (0, 0))],
        out_specs=pl.BlockSpec((tm, d), lambda i: (i, 0)),
        out_shape=jax.ShapeDtypeStruct((t, d), F32),
        compiler_params=_cparams(("parallel",)),
        name="ln0",
    )(x, g.reshape(1, d), b.reshape(1, d))


def _inproj_kernel(x_ref, w_ref, o_ref):
    o_ref[...] = jnp.dot(x_ref[...].astype(BF16), w_ref[...], preferred_element_type=F32)


def _inproj_call(h, w_packed, layer, tm=512, tn=1792):
    t, d = h.shape
    tm = min(tm, t)
    return pl.pallas_call(
        _inproj_kernel,
        grid=(NP // tn, t // tm),
        in_specs=[pl.BlockSpec((tm, d), lambda j, i: (i, 0)),
                  pl.BlockSpec((None, d, tn), lambda j, i: (layer, 0, j))],
        out_specs=pl.BlockSpec((tm, tn), lambda j, i: (i, j)),
        out_shape=jax.ShapeDtypeStruct((t, NP), F32),
        compiler_params=_cparams(("parallel", "parallel")),
        name="inproj",
    )(h, w_packed)


def _lane_col(x, idx):
    lane = lax.broadcasted_iota(jnp.int32, x.shape, 1)
    return jnp.sum(jnp.where(lane == idx, x, 0.0), axis=1, keepdims=True)


def _gdn_kernel(q_ref, k_ref, v_ref, s_ref, gate_ref, cwq_ref, cwk_ref, cwv_ref, par_ref, nw_ref,
                o_ref, state_ref, tail_ref, xbuf_ref, q_s, k_s, v_s, b_s, g_s, o_s, *, tb):
    h = pl.program_id(1)
    blk = pl.program_id(2)
    nchunk = tb // CHUNK

    @pl.when(blk == 0)
    def _():
        state_ref[...] = jnp.zeros_like(state_ref)
        tail_ref[...] = jnp.zeros_like(tail_ref)

    def conv(x_ref, cw_ref, slot):
        xbuf_ref[0:SUBLANE, :] = tail_ref[slot]
        xbuf_ref[SUBLANE:SUBLANE + tb, :] = x_ref[...]
        tail_ref[slot] = x_ref[tb - SUBLANE:tb, :]
        y = xbuf_ref[SUBLANE - (CONV_W - 1):SUBLANE - (CONV_W - 1) + tb, :] * cw_ref[0:1, :]
        for kk in range(1, CONV_W):
            off = SUBLANE - (CONV_W - 1) + kk
            y = y + xbuf_ref[off:off + tb, :] * cw_ref[kk:kk + 1, :]
        return _silu(y)

    q = conv(q_ref, cwq_ref, 0)
    k = conv(k_ref, cwk_ref, 1)
    v = conv(v_ref, cwv_ref, 2)
    q = q * lax.rsqrt(jnp.sum(q * q, axis=-1, keepdims=True) + RMS_EPS) * (GDN_DK ** -0.5)
    k = k * lax.rsqrt(jnp.sum(k * k, axis=-1, keepdims=True) + RMS_EPS)
    q_s[...] = q
    k_s[...] = k
    v_s[...] = v

    sc = s_ref[...]
    beta = _sigmoid(_lane_col(sc, h))
    a_log = par_ref[pl.ds(h, 1), :]
    dt_bias = par_ref[pl.ds(GDN_HEADS + h, 1), :]
    g = -jnp.exp(a_log) * _softplus(_lane_col(sc, GDN_HEADS + h) + dt_bias)
    b_s[...] = jnp.broadcast_to(beta, (tb, LANE))
    g_s[...] = g

    row = lax.broadcasted_iota(jnp.int32, (CHUNK, CHUNK), 0)
    col = lax.broadcasted_iota(jnp.int32, (CHUNK, CHUNK), 1)
    incl = col <= row
    strict = col < row
    tri = jnp.where(incl, 1.0, 0.0).astype(F32)
    eye = jnp.where(col == row, 1.0, 0.0).astype(F32)
    lane = lax.broadcasted_iota(jnp.int32, (CHUNK, LANE), 1)

    def chunk_body(c, carry):
        r0 = pl.multiple_of(c * CHUNK, CHUNK)
        qc = q_s[pl.ds(r0, CHUNK), :]
        kc = k_s[pl.ds(r0, CHUNK), :]
        vc = v_s[pl.ds(r0, CHUNK), :]
        bc = b_s[pl.ds(r0, CHUNK), :]
        gc = g_s[pl.ds(r0, CHUNK), :]
        cum = _mm_f32(tri, gc)
        lhs = jnp.where(lane == 0, cum, jnp.where(lane == 1, 1.0, 0.0))
        rhs = jnp.where(lane == 0, 1.0, jnp.where(lane == 1, -cum, 0.0))
        diff = _mm_nt_f32(lhs, rhs)
        decay = jnp.where(incl, jnp.exp(jnp.where(incl, diff, 0.0)), 0.0)
        kkm = _mm_nt(kc, kc)
        a_strict = jnp.where(strict, bc[:, 0:CHUNK] * kkm * decay, 0.0)
        x = eye - a_strict
        p = _mm_f32(a_strict, a_strict)
        for it in range(5):
            x = x + _mm_f32(x, p)
            if it < 4:
                p = _mm_f32(p, p)
        ecum = jnp.exp(cum)
        rhs_cat = jnp.concatenate([vc * bc, kc * (bc * ecum)], axis=1)
        sol = _mm_f32(x, rhs_cat)
        u = sol[:, 0:GDN_DV]
        w = sol[:, GDN_DV:GDN_DV + GDN_DK]
        attn = _mm_nt(qc, kc) * decay
        cum_last = cum[CHUNK - 1:CHUNK, :]
        q_inter = qc * ecum
        k_state = kc * jnp.exp(cum_last - cum)
        last = jnp.exp(cum_last)
        s = state_ref[...]
        v_new = u - _mm(w, s)
        o = _mm(q_inter, s) + _mm(attn, v_new)
        state_ref[...] = s * last + _mm_tn(k_state, v_new)
        o_s[pl.ds(r0, CHUNK), :] = o
        return carry

    lax.fori_loop(0, nchunk, chunk_body, 0)

    o = o_s[...]
    oh = o * lax.rsqrt(jnp.mean(o * o, axis=-1, keepdims=True) + RMS_EPS) * nw_ref[...]
    o_ref[...] = oh * _silu(gate_ref[...])


def _gdn_call(proj, conv_w, par, norm_w, layer, batch, seq, tb=512):
    tb = min(tb, seq)
    nb = seq // tb
    t = batch * seq
    cb = LANE

    def tok(colblk):
        return pl.BlockSpec((tb, cb), lambda b, h, i: (b * nb + i, colblk + h))

    def cw(colblk):
        return pl.BlockSpec((None, CONV_W, cb), lambda b, h, i: (layer, 0, colblk + h))

    kern = functools.partial(_gdn_kernel, tb=tb)
    return pl.pallas_call(
        kern,
        grid=(batch, GDN_HEADS, nb),
        in_specs=[tok(A_Q // cb), tok(A_K // cb), tok(A_V // cb),
                  pl.BlockSpec((tb, cb), lambda b, h, i: (b * nb + i, A_S // cb)),
                  tok(A_G // cb),
                  cw(0), cw(GDN_QK // cb), cw(2 * GDN_QK // cb),
                  pl.BlockSpec((None, SUBLANE, LANE), lambda b, h, i: (layer, 0, 0)),
                  pl.BlockSpec((None, 1, GDN_DV), lambda b, h, i: (layer, 0, 0))],
        out_specs=pl.BlockSpec((tb, cb), lambda b, h, i: (b * nb + i, h)),
        out_shape=jax.ShapeDtypeStruct((t, GDN_V), F32),
        scratch_shapes=[pltpu.VMEM((GDN_DK, GDN_DV), F32),
                        pltpu.VMEM((3, SUBLANE, cb), F32),
                        pltpu.VMEM((tb + SUBLANE, cb), F32),
                        pltpu.VMEM((tb, cb), F32), pltpu.VMEM((tb, cb), F32), pltpu.VMEM((tb, cb), F32),
                        pltpu.VMEM((tb, LANE), F32), pltpu.VMEM((tb, LANE), F32),
                        pltpu.VMEM((tb, cb), F32)],
        compiler_params=_cparams(("parallel", "parallel", "arbitrary")),
        name="gdn",
    )(proj, proj, proj, proj, proj, conv_w, conv_w, conv_w, par, norm_w)


def _gdn4_kernel(qkv_ref, s_ref, gate_ref, cw_ref, par_ref, nw_ref, o_ref,
                 state_ref, tail_ref, xbuf_ref, q_s, k_s, v_s, cumb_s, betab_s, cumrow_s, o_s, *, tb):
    blk = pl.program_id(1)
    nc = tb // CHUNK
    nh = GDN_HEADS
    off0 = SUBLANE - (CONV_W - 1)

    @pl.when(blk == 0)
    def _():
        state_ref[...] = jnp.zeros_like(state_ref)
        tail_ref[...] = jnp.zeros_like(tail_ref)

    xbuf_ref[0:SUBLANE, :] = tail_ref[...]
    xbuf_ref[SUBLANE:SUBLANE + tb, :] = qkv_ref[...]
    tail_ref[...] = qkv_ref[tb - SUBLANE:tb, :]
    for j in range(3 * nh):
        cs = slice(j * LANE, (j + 1) * LANE)
        y = xbuf_ref[off0:off0 + tb, cs] * cw_ref[0:1, cs]
        for kk in range(1, CONV_W):
            y = y + xbuf_ref[off0 + kk:off0 + kk + tb, cs] * cw_ref[kk:kk + 1, cs]
        y = _silu(y)
        if j < nh:
            q_s[j] = y * lax.rsqrt(jnp.sum(y * y, axis=-1, keepdims=True) + RMS_EPS) * (GDN_DK ** -0.5)
        elif j < 2 * nh:
            k_s[j - nh] = y * lax.rsqrt(jnp.sum(y * y, axis=-1, keepdims=True) + RMS_EPS)
        else:
            v_s[j - 2 * nh] = y

    row = lax.broadcasted_iota(jnp.int32, (CHUNK, CHUNK), 0)
    col = lax.broadcasted_iota(jnp.int32, (CHUNK, CHUNK), 1)
    incl = col <= row
    strict = col < row
    tri = jnp.where(incl, 1.0, 0.0).astype(BF16)
    eye = jnp.where(col == row, 1.0, 0.0).astype(F32)

    sc = s_ref[...]
    beta_all = _sigmoid(sc)
    g_all = -jnp.exp(par_ref[0:1, :]) * _softplus(sc + par_ref[1:2, :])
    cums = []
    for c in range(nc):
        hi, mid, lo = _split3(g_all[c * CHUNK:(c + 1) * CHUNK, :])
        cums.append(jnp.dot(tri, hi, preferred_element_type=F32) + jnp.dot(tri, mid, preferred_element_type=F32)
                    + jnp.dot(tri, lo, preferred_element_type=F32))
    cum_all = jnp.concatenate(cums, axis=0)
    cum_t = jnp.transpose(cum_all)
    for c in range(nc):
        cumrow_s[c] = cum_t[0:SUBLANE, c * CHUNK:(c + 1) * CHUNK]
    for h in range(nh):
        cumb_s[h] = jnp.broadcast_to(cum_all[:, nh + h:nh + h + 1], (tb, LANE))
        betab_s[h] = jnp.broadcast_to(beta_all[:, h:h + 1], (tb, LANE))

    npar = 2

    def group_body(cp, carry):
        chains = [(cp * npar + cc, h) for cc in range(npar) for h in range(nh)]
        r0s = [pl.multiple_of(c * CHUNK, CHUNK) for c, _ in chains]
        qc = [q_s[h, pl.ds(r0, CHUNK), :] for (_, h), r0 in zip(chains, r0s)]
        kc = [k_s[h, pl.ds(r0, CHUNK), :] for (_, h), r0 in zip(chains, r0s)]
        vc = [v_s[h, pl.ds(r0, CHUNK), :] for (_, h), r0 in zip(chains, r0s)]
        cumc = [cumb_s[h, pl.ds(r0, CHUNK), :] for (_, h), r0 in zip(chains, r0s)]
        bc = [betab_s[h, pl.ds(r0, CHUNK), :] for (_, h), r0 in zip(chains, r0s)]
        n = len(chains)
        kk = [_mm_nt(kc[i], kc[i]) for i in range(n)]
        qk = [_mm_nt(qc[i], kc[i]) for i in range(n)]
        decay = []
        for i, (c, h) in enumerate(chains):
            diff = cumc[i][:, 0:CHUNK] - cumrow_s[c][nh + h:nh + h + 1, :]
            decay.append(jnp.where(incl, jnp.exp(jnp.where(incl, diff, 0.0)), 0.0))
        a = [jnp.where(strict, bc[i][:, 0:CHUNK] * kk[i] * decay[i], 0.0) for i in range(n)]
        x = [eye - a[i] for i in range(n)]
        p = [_mm(a[i], a[i]) for i in range(n)]
        for it in range(5):
            x = [x[i] + _mm(x[i], p[i]) for i in range(n)]
            if it < 4:
                p = [_mm(p[i], p[i]) for i in range(n)]
        ecum = [jnp.exp(cumc[i]) for i in range(n)]
        sol = [_mm(x[i], jnp.concatenate([vc[i] * bc[i], kc[i] * (bc[i] * ecum[i])], axis=1)) for i in range(n)]
        attn = [qk[i] * decay[i] for i in range(n)]
        cum_last = [cumc[i][CHUNK - 1:CHUNK, :] for i in range(n)]
        k_state = [kc[i] * jnp.exp(cum_last[i] - cumc[i]) for i in range(n)]
        wqi = [jnp.concatenate([sol[i][:, GDN_DV:GDN_DV + GDN_DK], qc[i] * ecum[i]], axis=0) for i in range(n)]
        for cc in range(npar):
            idx = [cc * nh + h for h in range(nh)]
            s = [state_ref[h] for h in range(nh)]
            wq = [_mm(wqi[i], s[h]) for h, i in enumerate(idx)]
            v_new = [sol[i][:, 0:GDN_DV] - wq[h][0:CHUNK, :] for h, i in enumerate(idx)]
            av = [_mm(attn[i], v_new[h]) for h, i in enumerate(idx)]
            upd = [_mm_tn(k_state[i], v_new[h]) for h, i in enumerate(idx)]
            for h, i in enumerate(idx):
                state_ref[h] = s[h] * jnp.exp(cum_last[i]) + upd[h]
                o_s[pl.ds(r0s[i], CHUNK), h * LANE:(h + 1) * LANE] = wq[h][CHUNK:2 * CHUNK, :] + av[h]
        return carry

    lax.fori_loop(0, nc // npar, group_body, 0)

    for h in range(nh):
        hs = slice(h * LANE, (h + 1) * LANE)
        o = o_s[:, hs]
        oh = o * lax.rsqrt(jnp.mean(o * o, axis=-1, keepdims=True) + RMS_EPS) * nw_ref[...]
        o_ref[:, hs] = oh * _silu(gate_ref[:, hs])


def _gdn4_call(proj, conv_w, par, norm_w, layer, batch, seq, tb=512):
    tb = min(tb, seq)
    nb = seq // tb
    nc = tb // CHUNK
    t = batch * seq
    wq = 2 * GDN_QK + GDN_V
    kern = functools.partial(_gdn4_kernel, tb=tb)
    return pl.pallas_call(
        kern,
        grid=(batch, nb),
        in_specs=[pl.BlockSpec((tb, wq), lambda b, i: (b * nb + i, A_Q // wq)),
                  pl.BlockSpec((tb, LANE), lambda b, i: (b * nb + i, A_S // LANE)),
                  pl.BlockSpec((tb, GDN_V), lambda b, i: (b * nb + i, A_G // GDN_V)),
                  pl.BlockSpec((None, CONV_W, wq), lambda b, i: (layer, 0, 0)),
                  pl.BlockSpec((None, SUBLANE, LANE), lambda b, i: (layer, 0, 0)),
                  pl.BlockSpec((None, 1, GDN_DV), lambda b, i: (layer, 0, 0))],
        out_specs=pl.BlockSpec((tb, GDN_V), lambda b, i: (b * nb + i, 0)),
        out_shape=jax.ShapeDtypeStruct((t, GDN_V), F32),
        scratch_shapes=[pltpu.VMEM((GDN_HEADS, GDN_DK, GDN_DV), F32),
                        pltpu.VMEM((SUBLANE, wq), F32),
                        pltpu.VMEM((tb + SUBLANE, wq), F32),
                        pltpu.VMEM((GDN_HEADS, tb, LANE), F32),
                        pltpu.VMEM((GDN_HEADS, tb, LANE), F32),
                        pltpu.VMEM((GDN_HEADS, tb, LANE), F32),
                        pltpu.VMEM((GDN_HEADS, tb, LANE), F32),
                        pltpu.VMEM((GDN_HEADS, tb, LANE), F32),
                        pltpu.VMEM((nc, SUBLANE, CHUNK), F32),
                        pltpu.VMEM((tb, GDN_V), F32)],
        compiler_params=_cparams(("parallel", "arbitrary")),
        name="gdn",
    )(proj, proj, proj, conv_w, par, norm_w)


def _level_matrix():
    i = np.arange(CHUNK)[:, None]
    j = np.arange(CHUNK)[None, :]
    mats = []
    for b in LEVELS:
        mats.append(((j <= i) & (j // b == i // b)).astype(np.float32))
    for b in LEVELS:
        mats.append(((j > i) & (j // b == i // b)).astype(np.float32))
    return np.concatenate(mats, axis=0)


def _dd_core(q_s, k_s, v_ref, la_s, o_s, state_ref, lm_ref, c8_s, p_s, *, tb, g_heads):
    nchunk = tb // CHUNK
    dkh = LANE // g_heads
    dvp = g_heads * LANE
    row = lax.broadcasted_iota(jnp.int32, (CHUNK, CHUNK), 0)
    col = lax.broadcasted_iota(jnp.int32, (CHUNK, CHUNK), 1)
    level_masks = []
    for sh in (5, 4, 3):
        same2b = jnp.right_shift(row, sh + 1) == jnp.right_shift(col, sh + 1)
        upper = (jnp.right_shift(row, sh) & 1) == 1
        lower = (jnp.right_shift(col, sh) & 1) == 0
        level_masks.append(jnp.where(same2b, jnp.where(upper, jnp.where(lower, 1.0, 0.0), 0.0), 0.0))
    lane128 = lax.broadcasted_iota(jnp.int32, (CHUNK, LANE), 1)
    head_masks = [jnp.where((lane128 >= g * dkh) & (lane128 < (g + 1) * dkh), 1.0, 0.0)
                  for g in range(g_heads)]
    sub = lax.broadcasted_iota(jnp.int32, (SUB, LANE), 0)
    dk_sh = dkh.bit_length() - 1
    lane_sh = LANE.bit_length() - 1
    orow = lax.broadcasted_iota(jnp.int32, (LANE, dvp), 0)
    ocol = lax.broadcasted_iota(jnp.int32, (LANE, dvp), 1)
    ones_bd = jnp.where(jnp.right_shift(orow, dk_sh) == jnp.right_shift(ocol, lane_sh), 1.0, 0.0).astype(BF16)
    srow = lax.broadcasted_iota(jnp.int32, (dvp, LANE), 0)
    scol = lax.broadcasted_iota(jnp.int32, (dvp, LANE), 1)
    state_mask = jnp.where(jnp.right_shift(srow, lane_sh) == jnp.right_shift(scol, dk_sh), 1.0, 0.0)
    lm = lm_ref[...]

    nl = len(LEVELS)
    npar = DD_NPAR

    def group_body(cg, carry):
        rng = range(npar)
        r0s = [pl.multiple_of((cg * npar + i) * CHUNK, CHUNK) for i in rng]
        qc = [q_s[pl.ds(r0, CHUNK), :] for r0 in r0s]
        kc = [k_s[pl.ds(r0, CHUNK), :] for r0 in r0s]
        vc = [v_ref[pl.ds(r0, CHUNK), :] for r0 in r0s]
        parts = [_split3(la_s[pl.ds(r0, CHUNK), :]) for r0 in r0s]
        e = [jnp.dot(lm, parts[i][0], preferred_element_type=F32)
             + jnp.dot(lm, parts[i][1], preferred_element_type=F32)
             + jnp.dot(lm, parts[i][2], preferred_element_type=F32) for i in rng]
        cpre = [[e[i][l * CHUNK:(l + 1) * CHUNK, :] for l in range(nl)] for i in rng]
        csuf = [[e[i][(nl + l) * CHUNK:(nl + l + 1) * CHUNK, :] for l in range(nl)] for i in rng]
        attn = [[None] * g_heads for _ in rng]
        for li in range(3):
            qs = [qc[i] * jnp.exp(cpre[i][li + 1]) for i in rng]
            ks = [kc[i] * jnp.exp(csuf[i][li + 1]) for i in rng]
            for g in range(g_heads):
                for i in rng:
                    qg = qs[i] * head_masks[g] if g_heads > 1 else qs[i]
                    term = _mm_nt(qg, ks[i]) * level_masks[li]
                    attn[i][g] = term if li == 0 else attn[i][g] + term
        o = [jnp.concatenate([_mm(attn[i][g], vc[i][:, g * LANE:(g + 1) * LANE]) for g in range(g_heads)], axis=1)
             if g_heads > 1 else _mm(attn[i][0], vc[i]) for i in rng]
        for i in rng:
            c8 = cpre[i][3]
            c8_s[i] = c8
            for r in range(CHUNK // SUB):
                qr = qc[i][r * SUB:(r + 1) * SUB, :]
                cr = c8[r * SUB:(r + 1) * SUB, :]
                for jj in range(SUB):
                    krow = k_s[pl.ds(r0s[i] + r * SUB + jj, 1), :]
                    crow = c8_s[i, r * SUB + jj:r * SUB + jj + 1, :]
                    m = sub >= jj
                    pr = jnp.where(m, qr * krow * jnp.exp(jnp.where(m, cr - crow, 0.0)), 0.0)
                    p_s[i, (r * SUB + jj) * SUB:(r * SUB + jj + 1) * SUB, :] = pr
        rs = []
        for i in rng:
            pp = p_s[i]
            p_hi = pp.astype(BF16)
            p_lo = (pp - p_hi.astype(F32)).astype(BF16)
            rs.append(jnp.dot(p_hi, ones_bd, preferred_element_type=F32)
                      + jnp.dot(p_lo, ones_bd, preferred_element_type=F32))
        for i in rng:
            od = []
            for r in range(CHUNK // SUB):
                acc = None
                for jj in range(SUB):
                    vrow = v_ref[pl.ds(r0s[i] + r * SUB + jj, 1), :]
                    term = rs[i][(r * SUB + jj) * SUB:(r * SUB + jj + 1) * SUB, :] * vrow
                    acc = term if acc is None else acc + term
                od.append(acc)
            o[i] = o[i] + jnp.concatenate(od, axis=0)
        q_inter = [qc[i] * jnp.exp(cpre[i][0]) for i in rng]
        k_state = [kc[i] * jnp.exp(csuf[i][0]) for i in rng]
        upd = [_mm_tn(vc[i], k_state[i]) for i in rng]
        st = state_ref[...]
        for i in rng:
            o_s[pl.ds(r0s[i], CHUNK), :] = o[i] + _mm_nt(q_inter[i], st)
            decay_last = jnp.exp(cpre[i][0][CHUNK - 1:CHUNK, :])
            st = st * decay_last + (upd[i] * state_mask if g_heads > 1 else upd[i])
        state_ref[...] = st
        return carry

    lax.fori_loop(0, nchunk // npar, group_body, 0)


def _gated_rms_out(o_s, gate_ref, nw_ref, o_ref, g_heads):
    for g in range(g_heads):
        o = o_s[:, g * LANE:(g + 1) * LANE]
        oh = o * lax.rsqrt(jnp.mean(o * o, axis=-1, keepdims=True) + RMS_EPS) * nw_ref[...]
        o_ref[:, g * LANE:(g + 1) * LANE] = oh * _silu(gate_ref[:, g * LANE:(g + 1) * LANE])


def _gla_kernel(q_ref, k_ref, v_ref, lr_ref, gate_ref, w2_ref, b2_ref, nw_ref, lm_ref,
                o_ref, state_ref, q_s, k_s, la_s, o_s, c8_s, p_s, *, tb):
    @pl.when(pl.program_id(2) == 0)
    def _():
        state_ref[...] = jnp.zeros_like(state_ref)

    q_s[...] = q_ref[...] * (GLA_DK ** -0.5)
    k_s[...] = k_ref[...]
    z = _mm(lr_ref[...], w2_ref[...]) + b2_ref[...]
    la_s[...] = _log_sigmoid(z) * (1.0 / GLA_NORMALIZER)
    _dd_core(q_s, k_s, v_ref, la_s, o_s, state_ref, lm_ref, c8_s, p_s, tb=tb, g_heads=2)
    _gated_rms_out(o_s, gate_ref, nw_ref, o_ref, 2)


def _gla_call(proj, w2p, b2, norm_w, lm, layer, batch, seq, tb=512):
    tb = min(tb, seq)
    nb = seq // tb
    t = batch * seq
    npair = GLA_HEADS // 2
    kern = functools.partial(_gla_kernel, tb=tb)
    return pl.pallas_call(
        kern,
        grid=(batch, npair, nb),
        in_specs=[pl.BlockSpec((tb, LANE), lambda b, p, i: (b * nb + i, B_Q // LANE + p)),
                  pl.BlockSpec((tb, LANE), lambda b, p, i: (b * nb + i, B_K // LANE + p)),
                  pl.BlockSpec((tb, 2 * LANE), lambda b, p, i: (b * nb + i, B_V // (2 * LANE) + p)),
                  pl.BlockSpec((tb, LANE), lambda b, p, i: (b * nb + i, B_LR // LANE)),
                  pl.BlockSpec((tb, 2 * LANE), lambda b, p, i: (b * nb + i, B_G // (2 * LANE) + p)),
                  pl.BlockSpec((None, LANE, LANE), lambda b, p, i: (layer, 0, p)),
                  pl.BlockSpec((None, 1, LANE), lambda b, p, i: (layer, 0, p)),
                  pl.BlockSpec((None, 1, GLA_DV), lambda b, p, i: (layer, 0, 0)),
                  pl.BlockSpec((2 * len(LEVELS) * CHUNK, CHUNK), lambda b, p, i: (0, 0))],
        out_specs=pl.BlockSpec((tb, 2 * LANE), lambda b, p, i: (b * nb + i, p)),
        out_shape=jax.ShapeDtypeStruct((t, GLA_V), F32),
        scratch_shapes=[pltpu.VMEM((2 * LANE, LANE), F32),
                        pltpu.VMEM((tb, LANE), F32), pltpu.VMEM((tb, LANE), F32), pltpu.VMEM((tb, LANE), F32),
                        pltpu.VMEM((tb, 2 * LANE), F32),
                        pltpu.VMEM((DD_NPAR, CHUNK, LANE), F32),
                        pltpu.VMEM((DD_NPAR, CHUNK * SUB, LANE), F32)],
        compiler_params=_cparams(("parallel", "parallel", "arbitrary")),
        name="gla",
    )(proj, proj, proj, proj, proj, w2p, b2, norm_w, lm)


def _hgrn_kernel(q_ref, f_ref, v_ref, gate_ref, lbl_ref, nw_ref, lm_ref,
                 o_ref, state_ref, q_s, k_s, la_s, o_s, c8_s, p_s, *, tb, layer):
    @pl.when(pl.program_id(2) == 0)
    def _():
        state_ref[...] = jnp.zeros_like(state_ref)

    logits = lbl_ref[...]
    mx = jnp.max(logits, axis=0, keepdims=True)
    ex = jnp.exp(logits - mx)
    p = ex / jnp.sum(ex, axis=0, keepdims=True)
    acc = p[0:1, :]
    for r in range(1, layer + 1):
        acc = acc + p[r:r + 1, :]
    lb = jnp.clip(acc - p[0:1, :], 0.0, 1.0)
    log_lb = jnp.log(jnp.maximum(lb, LB_FLOOR))
    log_1m = jnp.log1p(-lb)

    cf = f_ref[...]
    second = log_1m + _log_sigmoid(cf)
    mxab = jnp.maximum(log_lb, second)
    la_s[...] = mxab + jnp.log1p(jnp.exp(-jnp.abs(log_lb - second)))
    k_s[...] = (1.0 - lb) * _sigmoid(-cf)
    q_s[...] = _silu(q_ref[...]) * (HGRN_EXPAND ** -0.5)
    _dd_core(q_s, k_s, v_ref, la_s, o_s, state_ref, lm_ref, c8_s, p_s, tb=tb, g_heads=1)
    _gated_rms_out(o_s, gate_ref, nw_ref, o_ref, 1)


def _hgrn_call(proj, lb_logits, norm_w, lm, layer, batch, seq, tb=512):
    tb = min(tb, seq)
    nb = seq // tb
    t = batch * seq

    def tok(colblk):
        return pl.BlockSpec((tb, LANE), lambda b, h, i: (b * nb + i, colblk + h))

    kern = functools.partial(_hgrn_kernel, tb=tb, layer=layer)
    return pl.pallas_call(
        kern,
        grid=(batch, HGRN_HEADS, nb),
        in_specs=[tok(C_Q // LANE), tok(C_F // LANE), tok(C_I // LANE), tok(C_G // LANE),
                  pl.BlockSpec((DEPTH, LANE), lambda b, h, i: (0, h)),
                  pl.BlockSpec((None, 1, HGRN_DV), lambda b, h, i: (layer, 0, 0)),
                  pl.BlockSpec((2 * len(LEVELS) * CHUNK, CHUNK), lambda b, h, i: (0, 0))],
        out_specs=pl.BlockSpec((tb, LANE), lambda b, h, i: (b * nb + i, h)),
        out_shape=jax.ShapeDtypeStruct((t, HGRN_V), F32),
        scratch_shapes=[pltpu.VMEM((LANE, LANE), F32),
                        pltpu.VMEM((tb, LANE), F32), pltpu.VMEM((tb, LANE), F32), pltpu.VMEM((tb, LANE), F32),
                        pltpu.VMEM((tb, LANE), F32),
                        pltpu.VMEM((DD_NPAR, CHUNK, LANE), F32),
                        pltpu.VMEM((DD_NPAR, CHUNK * SUB, LANE), F32)],
        compiler_params=_cparams(("parallel", "parallel", "arbitrary")),
        name="hgrn",
    )(proj, proj, proj, proj, lb_logits, norm_w, lm)


def _merge_kernel(oa_ref, ob_ref, oc_ref, ma_ref, mb_ref, mc_ref, h_ref,
                  wa_ref, wb_ref, wc_ref, wo_ref, g_ref, b_ref, o_ref):
    y = (_sigmoid(ma_ref[...]) * _mm(oa_ref[...], wa_ref[...])
         + _sigmoid(mb_ref[...]) * _mm(ob_ref[...], wb_ref[...])
         + _sigmoid(mc_ref[...]) * _mm(oc_ref[...], wc_ref[...]))
    mix = _mm(y, wo_ref[...])
    o_ref[...] = _layer_norm(ALPHA * h_ref[...] + mix, g_ref[...], b_ref[...])


def _merge_call(o_a, o_b, o_c, proj, h, wa, wb, wc, wo, g, b, layer, tm=256):
    t, d = h.shape
    tm = min(tm, t)
    mblk = D_MODEL

    def row(width):
        return pl.BlockSpec((tm, width), lambda i: (i, 0))

    def wspec(kdim):
        return pl.BlockSpec((None, kdim, d), lambda i: (layer, 0, 0))

    vec = pl.BlockSpec((None, 1, d), lambda i: (layer, 0, 0))
    return pl.pallas_call(
        _merge_kernel,
        grid=(t // tm,),
        in_specs=[row(GDN_V), row(GLA_V), row(HGRN_V),
                  pl.BlockSpec((tm, mblk), lambda i: (i, M_A // mblk)),
                  pl.BlockSpec((tm, mblk), lambda i: (i, M_B // mblk)),
                  pl.BlockSpec((tm, mblk), lambda i: (i, M_C // mblk)),
                  row(d), wspec(GDN_V), wspec(GLA_V), wspec(HGRN_V), wspec(d), vec, vec],
        out_specs=row(d),
        out_shape=jax.ShapeDtypeStruct((t, d), F32),
        compiler_params=_cparams(("parallel",)),
        name="merge",
    )(o_a, o_b, o_c, proj, proj, proj, h, wa, wb, wc, wo, g, b)


def _route(scores_t, bias_ref):
    s = [scores_t[e:e + 1, :] for e in range(N_EXPERTS)]
    sel = [s[e] + bias_ref[e:e + 1, 0:1] for e in range(N_EXPERTS)]
    gscore = []
    for g in range(N_GROUPS):
        a, b, c, d = sel[4 * g:4 * g + 4]
        hi1, lo1 = jnp.maximum(a, b), jnp.minimum(a, b)
        hi2, lo2 = jnp.maximum(c, d), jnp.minimum(c, d)
        top1 = jnp.maximum(hi1, hi2)
        top2 = jnp.maximum(jnp.minimum(hi1, hi2), jnp.maximum(lo1, lo2))
        gscore.append(top1 + top2)
    best = gscore[0]
    gidx = jnp.zeros_like(best, dtype=jnp.int32)
    for g in range(1, N_GROUPS):
        take = gscore[g] > best
        best = jnp.where(take, gscore[g], best)
        gidx = jnp.where(take, g, gidx)
    ing, raw = [], []
    for kk in range(EXPERTS_PER_GROUP):
        vs, vr = sel[kk], s[kk]
        for g in range(1, N_GROUPS):
            pick = gidx == g
            vs = jnp.where(pick, sel[4 * g + kk], vs)
            vr = jnp.where(pick, s[4 * g + kk], vr)
        ing.append(vs)
        raw.append(vr)
    b1 = ing[0]
    i1 = jnp.zeros_like(gidx)
    for kk in range(1, EXPERTS_PER_GROUP):
        take = ing[kk] > b1
        b1 = jnp.where(take, ing[kk], b1)
        i1 = jnp.where(take, kk, i1)
    neg = jnp.full_like(b1, -jnp.inf)
    b2 = neg
    i2 = jnp.zeros_like(gidx)
    for kk in range(EXPERTS_PER_GROUP):
        cand = jnp.where(i1 == kk, neg, ing[kk])
        take = cand > b2
        b2 = jnp.where(take, cand, b2)
        i2 = jnp.where(take, kk, i2)
    w1 = raw[0]
    w2 = raw[0]
    for kk in range(1, EXPERTS_PER_GROUP):
        w1 = jnp.where(i1 == kk, raw[kk], w1)
        w2 = jnp.where(i2 == kk, raw[kk], w2)
    tot = w1 + w2
    w1 = w1 / tot
    w2 = w2 / tot
    e1 = gidx * EXPERTS_PER_GROUP + i1
    e2 = gidx * EXPERTS_PER_GROUP + i2
    rows = [jnp.where(e1 == e, w1, 0.0) + jnp.where(e2 == e, w2, 0.0) for e in range(N_EXPERTS)]
    return jnp.concatenate(rows, axis=0)


def _moe_kernel(h_ref, wr_ref, rb_ref, wg_ref, wu_ref, wd_ref, g_ref, b_ref, o_ref, comb_ref, acc_ref, *, tm):
    e = pl.program_id(1)

    @pl.when(e == 0)
    def _():
        logits_t = _mm_nt_f32(wr_ref[...], h_ref[...])
        comb_t = _route(_sigmoid(logits_t), rb_ref)
        pad = jnp.zeros((LANE - N_EXPERTS, tm), F32)
        comb_ref[...] = jnp.transpose(jnp.concatenate([comb_t, pad], axis=0))
        acc_ref[...] = jnp.zeros_like(acc_ref)

    x = h_ref[...].astype(BF16)
    hg = jnp.dot(x, wg_ref[...], preferred_element_type=F32)
    hu = jnp.dot(x, wu_ref[...], preferred_element_type=F32)
    cw = _lane_col(comb_ref[...], e)
    hid = _silu(hg) * hu * cw
    acc_ref[...] += jnp.dot(hid.astype(BF16), wd_ref[...], preferred_element_type=F32)

    @pl.when(e == N_EXPERTS - 1)
    def _():
        o_ref[...] = _layer_norm(ALPHA * h_ref[...] + acc_ref[...], g_ref[...], b_ref[...])


def _moe_call(h, wr_t, rbias, wg, wu, wd, g, b, layer, tm=1024):
    t, d = h.shape
    tm = min(tm, t)
    kern = functools.partial(_moe_kernel, tm=tm)
    vec = pl.BlockSpec((None, 1, d), lambda i, e: (layer, 0, 0))
    return pl.pallas_call(
        kern,
        grid=(t // tm, N_EXPERTS),
        in_specs=[pl.BlockSpec((tm, d), lambda i, e: (i, 0)),
                  pl.BlockSpec((N_EXPERTS, d), lambda i, e: (0, 0)),
                  pl.BlockSpec((N_EXPERTS, LANE), lambda i, e: (0, 0)),
                  pl.BlockSpec((None, None, d, D_FF), lambda i, e: (layer, e, 0, 0)),
                  pl.BlockSpec((None, None, d, D_FF), lambda i, e: (layer, e, 0, 0)),
                  pl.BlockSpec((None, None, D_FF, d), lambda i, e: (layer, e, 0, 0)),
                  vec, vec],
        out_specs=pl.BlockSpec((tm, d), lambda i, e: (i, 0)),
        out_shape=jax.ShapeDtypeStruct((t, d), F32),
        scratch_shapes=[pltpu.VMEM((tm, LANE), F32), pltpu.VMEM((tm, d), F32)],
        compiler_params=_cparams(("parallel", "arbitrary")),
        name="moe",
    )(h, wr_t, rbias, wg, wu, wd, g, b)


def _pack_w_in(w_in):
    (a_q, a_k, a_v, a_beta, a_dt, a_g, b_q, b_k, b_v, b_lr, b_g,
     c_q, c_f, c_i, c_g, m_a, m_b, m_c) = jnp.split(w_in, SPLIT_POINTS, axis=-1)
    lead = w_in.shape[:-1]
    a_s = jnp.concatenate([a_beta, a_dt, jnp.zeros(lead + (LANE - 2 * GDN_HEADS,), w_in.dtype)], -1)
    b_lrp = jnp.concatenate([b_lr, jnp.zeros(lead + (LANE - GLA_RANK,), w_in.dtype)], -1)
    packed = jnp.concatenate([m_a, m_b, m_c, a_q, a_k, a_v, a_g, a_s, b_q, b_k, b_lrp, b_v, b_g,
                              c_q, c_f, c_i, c_g], -1)
    assert packed.shape[-1] == NP
    return packed.astype(BF16)


def _prepare(w_in, gdn_conv, gdn_a_log, gdn_dt_bias, gdn_norm, gla_w2, gla_b2, gla_norm, hgrn_lb_logits,
             hgrn_norm, w_br_a, w_br_b, w_br_c, w_out, ln1_g, ln1_b, w_router, router_bias, w_gate, w_up,
             w_down, ln2_g, ln2_b):
    depth = w_in.shape[0]
    d = w_out.shape[-1]
    return dict(
        w_packed=_pack_w_in(w_in),
        gdn_conv=gdn_conv,
        gdn_par=jnp.pad(jnp.stack([gdn_a_log, gdn_dt_bias], axis=1),
                        ((0, 0), (0, SUBLANE - 2), (GDN_HEADS, LANE - 2 * GDN_HEADS))),
        gdn_norm=gdn_norm.reshape(depth, 1, GDN_DV),
        w2p=jnp.concatenate([gla_w2, jnp.zeros((depth, LANE - GLA_RANK, GLA_QK), gla_w2.dtype)], axis=1),
        gla_b2=gla_b2.reshape(depth, 1, GLA_QK),
        gla_norm=gla_norm.reshape(depth, 1, GLA_DV),
        lb_logits=hgrn_lb_logits,
        hgrn_norm=hgrn_norm.reshape(depth, 1, HGRN_DV),
        lm=jnp.asarray(_level_matrix(), dtype=BF16),
        wa=w_br_a.astype(BF16), wb=w_br_b.astype(BF16), wc=w_br_c.astype(BF16), wo=w_out.astype(BF16),
        ln1_g=ln1_g.reshape(depth, 1, d), ln1_b=ln1_b.reshape(depth, 1, d),
        wr_t=jnp.transpose(w_router),
        rbias=jnp.broadcast_to(router_bias[:, None], (N_EXPERTS, LANE)),
        wg=w_gate.astype(BF16), wu=w_up.astype(BF16), wd=w_down.astype(BF16),
        ln2_g=ln2_g.reshape(depth, 1, d), ln2_b=ln2_b.reshape(depth, 1, d),
    )


def _mixer_block(h, p, layer, batch, seq):
    proj = _inproj_call(h, p["w_packed"], layer)
    o_a = _gdn4_call(proj, p["gdn_conv"], p["gdn_par"], p["gdn_norm"], layer, batch, seq)
    o_b = _gla_call(proj, p["w2p"], p["gla_b2"], p["gla_norm"], p["lm"], layer, batch, seq)
    o_c = _hgrn_call(proj, p["lb_logits"], p["hgrn_norm"], p["lm"], layer, batch, seq)
    return _merge_call(o_a, o_b, o_c, proj, h, p["wa"], p["wb"], p["wc"], p["wo"], p["ln1_g"], p["ln1_b"], layer)


def _ffn_block(h, p, layer):
    return _moe_call(h, p["wr_t"], p["rbias"], p["wg"], p["wu"], p["wd"], p["ln2_g"], p["ln2_b"], layer)


def kernel(x, ln0_g, ln0_b, w_in, gdn_conv, gdn_a_log, gdn_dt_bias, gdn_norm, gla_w2, gla_b2, gla_norm,
           hgrn_lb_logits, hgrn_norm, w_br_a, w_br_b, w_br_c, w_out, ln1_g, ln1_b, w_router, router_bias,
           w_gate, w_up, w_down, ln2_g, ln2_b):
    batch, seq, d = x.shape
    p = _prepare(w_in, gdn_conv, gdn_a_log, gdn_dt_bias, gdn_norm, gla_w2, gla_b2, gla_norm, hgrn_lb_logits,
                 hgrn_norm, w_br_a, w_br_b, w_br_c, w_out, ln1_g, ln1_b, w_router, router_bias, w_gate, w_up,
                 w_down, ln2_g, ln2_b)
    h = _ln_call(x.reshape(batch * seq, d), ln0_g, ln0_b)
    for layer in range(w_in.shape[0]):
        h = _mixer_block(h, p, layer, batch, seq)
        h = _ffn_block(h, p, layer)
    return h.reshape(batch, seq, d)
```

```python
import functools

import numpy as np
import jax
import jax.numpy as jnp
from jax import lax
from jax.experimental import pallas as pl
from jax.experimental.pallas import tpu as pltpu

F32 = jnp.float32
BF16 = jnp.bfloat16
HIGHEST = lax.Precision.HIGHEST

D_MODEL = 1024
DEPTH = 4
CHUNK = 64
GDN_HEADS, GDN_DK, GDN_DV, CONV_W = 4, 128, 128, 4
GLA_HEADS, GLA_DK, GLA_DV, GLA_RANK, GLA_NORMALIZER = 4, 64, 128, 16, 16.0
HGRN_HEADS, HGRN_EXPAND, HGRN_DV = 4, 128, 128
LB_FLOOR = 1e-30
N_EXPERTS, N_GROUPS, TOP_K, D_FF = 16, 4, 2, 256
EXPERTS_PER_GROUP = N_EXPERTS // N_GROUPS
ALPHA = (2.0 * DEPTH) ** 0.25
LN_EPS = 1e-5
RMS_EPS = 1e-6

GDN_QK = GDN_HEADS * GDN_DK
GDN_V = GDN_HEADS * GDN_DV
GLA_QK = GLA_HEADS * GLA_DK
GLA_V = GLA_HEADS * GLA_DV
HGRN_QK = HGRN_HEADS * HGRN_EXPAND
HGRN_V = HGRN_HEADS * HGRN_DV
SPLIT_SIZES = (GDN_QK, GDN_QK, GDN_V, GDN_HEADS, GDN_HEADS, GDN_V,
               GLA_QK, GLA_QK, GLA_V, GLA_RANK, GLA_V,
               HGRN_QK, HGRN_QK, HGRN_V, HGRN_V,
               D_MODEL, D_MODEL, D_MODEL)
SPLIT_POINTS = tuple(int(v) for v in np.cumsum(SPLIT_SIZES)[:-1])

LANE = 128
SUBLANE = 8
VMEM_LIMIT = 48 * 1024 * 1024

M_A, M_B, M_C = 0, 1024, 2048
A_Q, A_K, A_V, A_G, A_S = 3072, 3584, 4096, 4608, 5120
B_Q, B_K, B_LR, B_V, B_G = 5248, 5504, 5760, 5888, 6400
C_Q, C_F, C_I, C_G = 6912, 7424, 7936, 8448
NP = 8960

SUB = 8
LEVELS = (64, 32, 16, 8)
DD_NPAR = 4


def _cparams(sem):
    return pltpu.CompilerParams(dimension_semantics=sem, vmem_limit_bytes=VMEM_LIMIT)


def _mm(a, b):
    return jnp.dot(a.astype(BF16), b.astype(BF16), preferred_element_type=F32)


def _mm_nt(a, b):
    return lax.dot_general(a.astype(BF16), b.astype(BF16), (((1,), (1,)), ((), ())),
                           preferred_element_type=F32)


def _mm_tn(a, b):
    return lax.dot_general(a.astype(BF16), b.astype(BF16), (((0,), (0,)), ((), ())),
                           preferred_element_type=F32)


def _mm_f32(a, b):
    return jnp.dot(a, b, preferred_element_type=F32, precision=HIGHEST)


def _mm_nt_f32(a, b):
    return lax.dot_general(a, b, (((1,), (1,)), ((), ())), preferred_element_type=F32,
                           precision=HIGHEST)


def _split3(x):
    hi = x.astype(BF16)
    r = x - hi.astype(F32)
    mid = r.astype(BF16)
    lo = (r - mid.astype(F32)).astype(BF16)
    return hi, mid, lo


def _sigmoid(x):
    return 1.0 / (1.0 + jnp.exp(-x))


def _silu(x):
    return x * _sigmoid(x)


def _softplus(x):
    return jnp.maximum(x, 0.0) + jnp.log1p(jnp.exp(-jnp.abs(x)))


def _log_sigmoid(x):
    return -_softplus(-x)


def _layer_norm(x, g, b):
    mu = jnp.mean(x, axis=-1, keepdims=True)
    xc = x - mu
    var = jnp.mean(xc * xc, axis=-1, keepdims=True)
    return xc * lax.rsqrt(var + LN_EPS) * g + b


def _ln_kernel(x_ref, g_ref, b_ref, o_ref):
    o_ref[...] = _layer_norm(x_ref[...], g_ref[...], b_ref[...])


def _ln_call(x, g, b, tm=512):
    t, d = x.shape
    tm = min(tm, t)
    return pl.pallas_call(
        _ln_kernel,
        grid=(t // tm,),
        in_specs=[pl.BlockSpec((tm, d), lambda i: (i, 0)),
                  pl.BlockSpec((1, d), lambda i: (0, 0)),
                  pl.BlockSpec((1, d), lambda i: (0, 0))],
        out_specs=pl.BlockSpec((tm, d), lambda i: (i, 0)),
        out_shape=jax.ShapeDtypeStruct((t, d), F32),
        compiler_params=_cparams(("parallel",)),
        name="ln0",
    )(x, g.reshape(1, d), b.reshape(1, d))


def _inproj_kernel(x_ref, w_ref, o_ref):
    o_ref[...] = jnp.dot(x_ref[...].astype(BF16), w_ref[...], preferred_element_type=F32)


def _inproj_call(h, w_packed, layer, tm=512, tn=1792):
    t, d = h.shape
    tm = min(tm, t)
    return pl.pallas_call(
        _inproj_kernel,
        grid=(NP // tn, t // tm),
        in_specs=[pl.BlockSpec((tm, d), lambda j, i: (i, 0)),
                  pl.BlockSpec((None, d, tn), lambda j, i: (layer, 0, j))],
        out_specs=pl.BlockSpec((tm, tn), lambda j, i: (i, j)),
        out_shape=jax.ShapeDtypeStruct((t, NP), F32),
        compiler_params=_cparams(("parallel", "parallel")),
        name="inproj",
    )(h, w_packed)


def _lane_col(x, idx):
    lane = lax.broadcasted_iota(jnp.int32, x.shape, 1)
    return jnp.sum(jnp.where(lane == idx, x, 0.0), axis=1, keepdims=True)


def _gdn_kernel(q_ref, k_ref, v_ref, s_ref, gate_ref, cwq_ref, cwk_ref, cwv_ref, par_ref, nw_ref,
                o_ref, state_ref, tail_ref, xbuf_ref, q_s, k_s, v_s, b_s, g_s, o_s, *, tb):
    h = pl.program_id(1)
    blk = pl.program_id(2)
    nchunk = tb // CHUNK

    @pl.when(blk == 0)
    def _():
        state_ref[...] = jnp.zeros_like(state_ref)
        tail_ref[...] = jnp.zeros_like(tail_ref)

    def conv(x_ref, cw_ref, slot):
        xbuf_ref[0:SUBLANE, :] = tail_ref[slot]
        xbuf_ref[SUBLANE:SUBLANE + tb, :] = x_ref[...]
        tail_ref[slot] = x_ref[tb - SUBLANE:tb, :]
        y = xbuf_ref[SUBLANE - (CONV_W - 1):SUBLANE - (CONV_W - 1) + tb, :] * cw_ref[0:1, :]
        for kk in range(1, CONV_W):
            off = SUBLANE - (CONV_W - 1) + kk
            y = y + xbuf_ref[off:off + tb, :] * cw_ref[kk:kk + 1, :]
        return _silu(y)

    q = conv(q_ref, cwq_ref, 0)
    k = conv(k_ref, cwk_ref, 1)
    v = conv(v_ref, cwv_ref, 2)
    q = q * lax.rsqrt(jnp.sum(q * q, axis=-1, keepdims=True) + RMS_EPS) * (GDN_DK ** -0.5)
    k = k * lax.rsqrt(jnp.sum(k * k, axis=-1, keepdims=True) + RMS_EPS)
    q_s[...] = q
    k_s[...] = k
    v_s[...] = v

    sc = s_ref[...]
    beta = _sigmoid(_lane_col(sc, h))
    a_log = par_ref[pl.ds(h, 1), :]
    dt_bias = par_ref[pl.ds(GDN_HEADS + h, 1), :]
    g = -jnp.exp(a_log) * _softplus(_lane_col(sc, GDN_HEADS + h) + dt_bias)
    b_s[...] = jnp.broadcast_to(beta, (tb, LANE))
    g_s[...] = g

    row = lax.broadcasted_iota(jnp.int32, (CHUNK, CHUNK), 0)
    col = lax.broadcasted_iota(jnp.int32, (CHUNK, CHUNK), 1)
    incl = col <= row
    strict = col < row
    tri = jnp.where(incl, 1.0, 0.0).astype(F32)
    eye = jnp.where(col == row, 1.0, 0.0).astype(F32)
    lane = lax.broadcasted_iota(jnp.int32, (CHUNK, LANE), 1)

    def chunk_body(c, carry):
        r0 = pl.multiple_of(c * CHUNK, CHUNK)
        qc = q_s[pl.ds(r0, CHUNK), :]
        kc = k_s[pl.ds(r0, CHUNK), :]
        vc = v_s[pl.ds(r0, CHUNK), :]
        bc = b_s[pl.ds(r0, CHUNK), :]
        gc = g_s[pl.ds(r0, CHUNK), :]
        cum = _mm_f32(tri, gc)
        lhs = jnp.where(lane == 0, cum, jnp.where(lane == 1, 1.0, 0.0))
        rhs = jnp.where(lane == 0, 1.0, jnp.where(lane == 1, -cum, 0.0))
        diff = _mm_nt_f32(lhs, rhs)
        decay = jnp.where(incl, jnp.exp(jnp.where(incl, diff, 0.0)), 0.0)
        kkm = _mm_nt(kc, kc)
        a_strict = jnp.where(strict, bc[:, 0:CHUNK] * kkm * decay, 0.0)
        x = eye - a_strict
        p = _mm_f32(a_strict, a_strict)
        for it in range(5):
            x = x + _mm_f32(x, p)
            if it < 4:
                p = _mm_f32(p, p)
        ecum = jnp.exp(cum)
        rhs_cat = jnp.concatenate([vc * bc, kc * (bc * ecum)], axis=1)
        sol = _mm_f32(x, rhs_cat)
        u = sol[:, 0:GDN_DV]
        w = sol[:, GDN_DV:GDN_DV + GDN_DK]
        attn = _mm_nt(qc, kc) * decay
        cum_last = cum[CHUNK - 1:CHUNK, :]
        q_inter = qc * ecum
        k_state = kc * jnp.exp(cum_last - cum)
        last = jnp.exp(cum_last)
        s = state_ref[...]
        v_new = u - _mm(w, s)
        o = _mm(q_inter, s) + _mm(attn, v_new)
        state_ref[...] = s * last + _mm_tn(k_state, v_new)
        o_s[pl.ds(r0, CHUNK), :] = o
        return carry

    lax.fori_loop(0, nchunk, chunk_body, 0)

    o = o_s[...]
    oh = o * lax.rsqrt(jnp.mean(o * o, axis=-1, keepdims=True) + RMS_EPS) * nw_ref[...]
    o_ref[...] = oh * _silu(gate_ref[...])


def _gdn_call(proj, conv_w, par, norm_w, layer, batch, seq, tb=512):
    tb = min(tb, seq)
    nb = seq // tb
    t = batch * seq
    cb = LANE

    def tok(colblk):
        return pl.BlockSpec((tb, cb), lambda b, h, i: (b * nb + i, colblk + h))

    def cw(colblk):
        return pl.BlockSpec((None, CONV_W, cb), lambda b, h, i: (layer, 0, colblk + h))

    kern = functools.partial(_gdn_kernel, tb=tb)
    return pl.pallas_call(
        kern,
        grid=(batch, GDN_HEADS, nb),
        in_specs=[tok(A_Q // cb), tok(A_K // cb), tok(A_V // cb),
                  pl.BlockSpec((tb, cb), lambda b, h, i: (b * nb + i, A_S // cb)),
                  tok(A_G // cb),
                  cw(0), cw(GDN_QK // cb), cw(2 * GDN_QK // cb),
                  pl.BlockSpec((None, SUBLANE, LANE), lambda b, h, i: (layer, 0, 0)),
                  pl.BlockSpec((None, 1, GDN_DV), lambda b, h, i: (layer, 0, 0))],
        out_specs=pl.BlockSpec((tb, cb), lambda b, h, i: (b * nb + i, h)),
        out_shape=jax.ShapeDtypeStruct((t, GDN_V), F32),
        scratch_shapes=[pltpu.VMEM((GDN_DK, GDN_DV), F32),
                        pltpu.VMEM((3, SUBLANE, cb), F32),
                        pltpu.VMEM((tb + SUBLANE, cb), F32),
                        pltpu.VMEM((tb, cb), F32), pltpu.VMEM((tb, cb), F32), pltpu.VMEM((tb, cb), F32),
                        pltpu.VMEM((tb, LANE), F32), pltpu.VMEM((tb, LANE), F32),
                        pltpu.VMEM((tb, cb), F32)],
        compiler_params=_cparams(("parallel", "parallel", "arbitrary")),
        name="gdn",
    )(proj, proj, proj, proj, proj, conv_w, conv_w, conv_w, par, norm_w)


def _gdn4_kernel(qkv_ref, s_ref, gate_ref, cw_ref, par_ref, nw_ref, o_ref,
                 state_ref, tail_ref, xbuf_ref, q_s, k_s, v_s, cumb_s, betab_s, cumrow_s, o_s, *, tb):
    blk = pl.program_id(1)
    nc = tb // CHUNK
    nh = GDN_HEADS
    off0 = SUBLANE - (CONV_W - 1)

    @pl.when(blk == 0)
    def _():
        state_ref[...] = jnp.zeros_like(state_ref)
        tail_ref[...] = jnp.zeros_like(tail_ref)

    xbuf_ref[0:SUBLANE, :] = tail_ref[...]
    xbuf_ref[SUBLANE:SUBLANE + tb, :] = qkv_ref[...]
    tail_ref[...] = qkv_ref[tb - SUBLANE:tb, :]
    for j in range(3 * nh):
        cs = slice(j * LANE, (j + 1) * LANE)
        y = xbuf_ref[off0:off0 + tb, cs] * cw_ref[0:1, cs]
        for kk in range(1, CONV_W):
            y = y + xbuf_ref[off0 + kk:off0 + kk + tb, cs] * cw_ref[kk:kk + 1, cs]
        y = _silu(y)
        if j < nh:
            q_s[j] = y * lax.rsqrt(jnp.sum(y * y, axis=-1, keepdims=True) + RMS_EPS) * (GDN_DK ** -0.5)
        elif j < 2 * nh:
            k_s[j - nh] = y * lax.rsqrt(jnp.sum(y * y, axis=-1, keepdims=True) + RMS_EPS)
        else:
            v_s[j - 2 * nh] = y

    row = lax.broadcasted_iota(jnp.int32, (CHUNK, CHUNK), 0)
    col = lax.broadcasted_iota(jnp.int32, (CHUNK, CHUNK), 1)
    incl = col <= row
    strict = col < row
    tri = jnp.where(incl, 1.0, 0.0).astype(BF16)
    eye = jnp.where(col == row, 1.0, 0.0).astype(F32)

    sc = s_ref[...]
    beta_all = _sigmoid(sc)
    g_all = -jnp.exp(par_ref[0:1, :]) * _softplus(sc + par_ref[1:2, :])
    cums = []
    for c in range(nc):
        hi, mid, lo = _split3(g_all[c * CHUNK:(c + 1) * CHUNK, :])
        cums.append(jnp.dot(tri, hi, preferred_element_type=F32) + jnp.dot(tri, mid, preferred_element_type=F32)
                    + jnp.dot(tri, lo, preferred_element_type=F32))
    cum_all = jnp.concatenate(cums, axis=0)
    cum_t = jnp.transpose(cum_all)
    for c in range(nc):
        cumrow_s[c] = cum_t[0:SUBLANE, c * CHUNK:(c + 1) * CHUNK]
    for h in range(nh):
        cumb_s[h] = jnp.broadcast_to(cum_all[:, nh + h:nh + h + 1], (tb, LANE))
        betab_s[h] = jnp.broadcast_to(beta_all[:, h:h + 1], (tb, LANE))

    npar = 2

    def group_body(cp, carry):
        chains = [(cp * npar + cc, h) for cc in range(npar) for h in range(nh)]
        r0s = [pl.multiple_of(c * CHUNK, CHUNK) for c, _ in chains]
        qc = [q_s[h, pl.ds(r0, CHUNK), :] for (_, h), r0 in zip(chains, r0s)]
        kc = [k_s[h, pl.ds(r0, CHUNK), :] for (_, h), r0 in zip(chains, r0s)]
        vc = [v_s[h, pl.ds(r0, CHUNK), :] for (_, h), r0 in zip(chains, r0s)]
        cumc = [cumb_s[h, pl.ds(r0, CHUNK), :] for (_, h), r0 in zip(chains, r0s)]
        bc = [betab_s[h, pl.ds(r0, CHUNK), :] for (_, h), r0 in zip(chains, r0s)]
        n = len(chains)
        kk = [_mm_nt(kc[i], kc[i]) for i in range(n)]
        qk = [_mm_nt(qc[i], kc[i]) for i in range(n)]
        decay = []
        for i, (c, h) in enumerate(chains):
            diff = cumc[i][:, 0:CHUNK] - cumrow_s[c][nh + h:nh + h + 1, :]
            decay.append(jnp.where(incl, jnp.exp(jnp.where(incl, diff, 0.0)), 0.0))
        a = [jnp.where(strict, bc[i][:, 0:CHUNK] * kk[i] * decay[i], 0.0) for i in range(n)]
        x = [eye - a[i] for i in range(n)]
        p = [_mm(a[i], a[i]) for i in range(n)]
        for it in range(5):
            x = [x[i] + _mm(x[i], p[i]) for i in range(n)]
            if it < 4:
                p = [_mm(p[i], p[i]) for i in range(n)]
        ecum = [jnp.exp(cumc[i]) for i in range(n)]
        sol = [_mm(x[i], jnp.concatenate([vc[i] * bc[i], kc[i] * (bc[i] * ecum[i])], axis=1)) for i in range(n)]
        attn = [qk[i] * decay[i] for i in range(n)]
        cum_last = [cumc[i][CHUNK - 1:CHUNK, :] for i in range(n)]
        k_state = [kc[i] * jnp.exp(cum_last[i] - cumc[i]) for i in range(n)]
        wqi = [jnp.concatenate([sol[i][:, GDN_DV:GDN_DV + GDN_DK], qc[i] * ecum[i]], axis=0) for i in range(n)]
        for cc in range(npar):
            idx = [cc * nh + h for h in range(nh)]
            s = [state_ref[h] for h in range(nh)]
            wq = [_mm(wqi[i], s[h]) for h, i in enumerate(idx)]
            v_new = [sol[i][:, 0:GDN_DV] - wq[h][0:CHUNK, :] for h, i in enumerate(idx)]
            av = [_mm(attn[i], v_new[h]) for h, i in enumerate(idx)]
            upd = [_mm_tn(k_state[i], v_new[h]) for h, i in enumerate(idx)]
            for h, i in enumerate(idx):
                state_ref[h] = s[h] * jnp.exp(cum_last[i]) + upd[h]
                o_s[pl.ds(r0s[i], CHUNK), h * LANE:(h + 1) * LANE] = wq[h][CHUNK:2 * CHUNK, :] + av[h]
        return carry

    lax.fori_loop(0, nc // npar, group_body, 0)

    for h in range(nh):
        hs = slice(h * LANE, (h + 1) * LANE)
        o = o_s[:, hs]
        oh = o * lax.rsqrt(jnp.mean(o * o, axis=-1, keepdims=True) + RMS_EPS) * nw_ref[...]
        o_ref[:, hs] = oh * _silu(gate_ref[:, hs])


def _gdn4_call(proj, conv_w, par, norm_w, layer, batch, seq, tb=512):
    tb = min(tb, seq)
    nb = seq // tb
    nc = tb // CHUNK
    t = batch * seq
    wq = 2 * GDN_QK + GDN_V
    kern = functools.partial(_gdn4_kernel, tb=tb)
    return pl.pallas_call(
        kern,
        grid=(batch, nb),
        in_specs=[pl.BlockSpec((tb, wq), lambda b, i: (b * nb + i, A_Q // wq)),
                  pl.BlockSpec((tb, LANE), lambda b, i: (b * nb + i, A_S // LANE)),
                  pl.BlockSpec((tb, GDN_V), lambda b, i: (b * nb + i, A_G // GDN_V)),
                  pl.BlockSpec((None, CONV_W, wq), lambda b, i: (layer, 0, 0)),
                  pl.BlockSpec((None, SUBLANE, LANE), lambda b, i: (layer, 0, 0)),
                  pl.BlockSpec((None, 1, GDN_DV), lambda b, i: (layer, 0, 0))],
        out_specs=pl.BlockSpec((tb, GDN_V), lambda b, i: (b * nb + i, 0)),
        out_shape=jax.ShapeDtypeStruct((t, GDN_V), F32),
        scratch_shapes=[pltpu.VMEM((GDN_HEADS, GDN_DK, GDN_DV), F32),
                        pltpu.VMEM((SUBLANE, wq), F32),
                        pltpu.VMEM((tb + SUBLANE, wq), F32),
                        pltpu.VMEM((GDN_HEADS, tb, LANE), F32),
                        pltpu.VMEM((GDN_HEADS, tb, LANE), F32),
                        pltpu.VMEM((GDN_HEADS, tb, LANE), F32),
                        pltpu.VMEM((GDN_HEADS, tb, LANE), F32),
                        pltpu.VMEM((GDN_HEADS, tb, LANE), F32),
                        pltpu.VMEM((nc, SUBLANE, CHUNK), F32),
                        pltpu.VMEM((tb, GDN_V), F32)],
        compiler_params=_cparams(("parallel", "arbitrary")),
        name="gdn",
    )(proj, proj, proj, conv_w, par, norm_w)


def _level_matrix():
    i = np.arange(CHUNK)[:, None]
    j = np.arange(CHUNK)[None, :]
    mats = []
    for b in LEVELS:
        mats.append(((j <= i) & (j // b == i // b)).astype(np.float32))
    for b in LEVELS:
        mats.append(((j > i) & (j // b == i // b)).astype(np.float32))
    return np.concatenate(mats, axis=0)


def _dd_core(q_s, k_s, v_ref, la_s, o_s, state_ref, lm_ref, c8_s, p_s, *, tb, g_heads):
    nchunk = tb // CHUNK
    dkh = LANE // g_heads
    dvp = g_heads * LANE
    row = lax.broadcasted_iota(jnp.int32, (CHUNK, CHUNK), 0)
    col = lax.broadcasted_iota(jnp.int32, (CHUNK, CHUNK), 1)
    level_masks = []
    for sh in (5, 4, 3):
        same2b = jnp.right_shift(row, sh + 1) == jnp.right_shift(col, sh + 1)
        upper = (jnp.right_shift(row, sh) & 1) == 1
        lower = (jnp.right_shift(col, sh) & 1) == 0
        level_masks.append(jnp.where(same2b, jnp.where(upper, jnp.where(lower, 1.0, 0.0), 0.0), 0.0))
    lane128 = lax.broadcasted_iota(jnp.int32, (CHUNK, LANE), 1)
    head_masks = [jnp.where((lane128 >= g * dkh) & (lane128 < (g + 1) * dkh), 1.0, 0.0)
                  for g in range(g_heads)]
    sub = lax.broadcasted_iota(jnp.int32, (SUB, LANE), 0)
    dk_sh = dkh.bit_length() - 1
    lane_sh = LANE.bit_length() - 1
    orow = lax.broadcasted_iota(jnp.int32, (LANE, dvp), 0)
    ocol = lax.broadcasted_iota(jnp.int32, (LANE, dvp), 1)
    ones_bd = jnp.where(jnp.right_shift(orow, dk_sh) == jnp.right_shift(ocol, lane_sh), 1.0, 0.0).astype(BF16)
    srow = lax.broadcasted_iota(jnp.int32, (dvp, LANE), 0)
    scol = lax.broadcasted_iota(jnp.int32, (dvp, LANE), 1)
    state_mask = jnp.where(jnp.right_shift(srow, lane_sh) == jnp.right_shift(scol, dk_sh), 1.0, 0.0)
    lm = lm_ref[...]

    nl = len(LEVELS)
    npar = DD_NPAR

    def group_body(cg, carry):
        rng = range(npar)
        r0s = [pl.multiple_of((cg * npar + i) * CHUNK, CHUNK) for i in rng]
        qc = [q_s[pl.ds(r0, CHUNK), :] for r0 in r0s]
        kc = [k_s[pl.ds(r0, CHUNK), :] for r0 in r0s]
        vc = [v_ref[pl.ds(r0, CHUNK), :] for r0 in r0s]
        parts = [_split3(la_s[pl.ds(r0, CHUNK), :]) for r0 in r0s]
        e_all = (jnp.dot(lm, jnp.concatenate([parts[i][0] for i in rng], axis=1), preferred_element_type=F32)
                 + jnp.dot(lm, jnp.concatenate([parts[i][1] for i in rng], axis=1), preferred_element_type=F32)
                 + jnp.dot(lm, jnp.concatenate([parts[i][2] for i in rng], axis=1), preferred_element_type=F32))
        e = [e_all[:, i * LANE:(i + 1) * LANE] for i in rng]
        cpre = [[e[i][l * CHUNK:(l + 1) * CHUNK, :] for l in range(nl)] for i in rng]
        csuf = [[e[i][(nl + l) * CHUNK:(nl + l + 1) * CHUNK, :] for l in range(nl)] for i in rng]
        attn = [[None] * g_heads for _ in rng]
        for li in range(3):
            qs = [qc[i] * jnp.exp(cpre[i][li + 1]) for i in rng]
            ks = [kc[i] * jnp.exp(csuf[i][li + 1]) for i in rng]
            for g in range(g_heads):
                for i in rng:
                    qg = qs[i] * head_masks[g] if g_heads > 1 else qs[i]
                    term = _mm_nt(qg, ks[i]) * level_masks[li]
                    attn[i][g] = term if li == 0 else attn[i][g] + term
        o = [jnp.concatenate([_mm(attn[i][g], vc[i][:, g * LANE:(g + 1) * LANE]) for g in range(g_heads)], axis=1)
             if g_heads > 1 else _mm(attn[i][0], vc[i]) for i in rng]
        for i in rng:
            c8 = cpre[i][3]
            c8_s[i] = c8
            for r in range(CHUNK // SUB):
                qr = qc[i][r * SUB:(r + 1) * SUB, :]
                cr = c8[r * SUB:(r + 1) * SUB, :]
                for jj in range(SUB):
                    krow = k_s[pl.ds(r0s[i] + r * SUB + jj, 1), :]
                    crow = c8_s[i, r * SUB + jj:r * SUB + jj + 1, :]
                    m = sub >= jj
                    pr = jnp.where(m, qr * krow * jnp.exp(jnp.where(m, cr - crow, 0.0)), 0.0)
                    p_s[i, (r * SUB + jj) * SUB:(r * SUB + jj + 1) * SUB, :] = pr
        rs = []
        for i in rng:
            rs.append(jnp.dot(p_s[i].astype(BF16), ones_bd, preferred_element_type=F32))
        for i in rng:
            od = []
            for r in range(CHUNK // SUB):
                acc = None
                for jj in range(SUB):
                    vrow = v_ref[pl.ds(r0s[i] + r * SUB + jj, 1), :]
                    term = rs[i][(r * SUB + jj) * SUB:(r * SUB + jj + 1) * SUB, :] * vrow
                    acc = term if acc is None else acc + term
                od.append(acc)
            o[i] = o[i] + jnp.concatenate(od, axis=0)
        q_inter = [qc[i] * jnp.exp(cpre[i][0]) for i in rng]
        k_state = [kc[i] * jnp.exp(csuf[i][0]) for i in rng]
        upd = [_mm_tn(vc[i], k_state[i]) for i in rng]
        st = state_ref[...]
        for i in rng:
            o_s[pl.ds(r0s[i], CHUNK), :] = o[i] + _mm_nt(q_inter[i], st)
            decay_last = jnp.exp(cpre[i][0][CHUNK - 1:CHUNK, :])
            st = st * decay_last + (upd[i] * state_mask if g_heads > 1 else upd[i])
        state_ref[...] = st
        return carry

    lax.fori_loop(0, nchunk // npar, group_body, 0)


def _gated_rms_out(o_s, gate_ref, nw_ref, o_ref, g_heads):
    for g in range(g_heads):
        o = o_s[:, g * LANE:(g + 1) * LANE]
        oh = o * lax.rsqrt(jnp.mean(o * o, axis=-1, keepdims=True) + RMS_EPS) * nw_ref[...]
        o_ref[:, g * LANE:(g + 1) * LANE] = oh * _silu(gate_ref[:, g * LANE:(g + 1) * LANE])


def _gla_kernel(q_ref, k_ref, v_ref, lr_ref, gate_ref, w2_ref, b2_ref, nw_ref, lm_ref,
                o_ref, state_ref, q_s, k_s, la_s, o_s, c8_s, p_s, *, tb):
    @pl.when(pl.program_id(2) == 0)
    def _():
        state_ref[...] = jnp.zeros_like(state_ref)

    q_s[...] = q_ref[...] * (GLA_DK ** -0.5)
    k_s[...] = k_ref[...]
    z = _mm(lr_ref[...], w2_ref[...]) + b2_ref[...]
    la_s[...] = _log_sigmoid(z) * (1.0 / GLA_NORMALIZER)
    _dd_core(q_s, k_s, v_ref, la_s, o_s, state_ref, lm_ref, c8_s, p_s, tb=tb, g_heads=2)
    _gated_rms_out(o_s, gate_ref, nw_ref, o_ref, 2)


def _gla_call(proj, w2p, b2, norm_w, lm, layer, batch, seq, tb=512):
    tb = min(tb, seq)
    nb = seq // tb
    t = batch * seq
    npair = GLA_HEADS // 2
    kern = functools.partial(_gla_kernel, tb=tb)
    return pl.pallas_call(
        kern,
        grid=(batch, npair, nb),
        in_specs=[pl.BlockSpec((tb, LANE), lambda b, p, i: (b * nb + i, B_Q // LANE + p)),
                  pl.BlockSpec((tb, LANE), lambda b, p, i: (b * nb + i, B_K // LANE + p)),
                  pl.BlockSpec((tb, 2 * LANE), lambda b, p, i: (b * nb + i, B_V // (2 * LANE) + p)),
                  pl.BlockSpec((tb, LANE), lambda b, p, i: (b * nb + i, B_LR // LANE)),
                  pl.BlockSpec((tb, 2 * LANE), lambda b, p, i: (b * nb + i, B_G // (2 * LANE) + p)),
                  pl.BlockSpec((None, LANE, LANE), lambda b, p, i: (layer, 0, p)),
                  pl.BlockSpec((None, 1, LANE), lambda b, p, i: (layer, 0, p)),
                  pl.BlockSpec((None, 1, GLA_DV), lambda b, p, i: (layer, 0, 0)),
                  pl.BlockSpec((2 * len(LEVELS) * CHUNK, CHUNK), lambda b, p, i: (0, 0))],
        out_specs=pl.BlockSpec((tb, 2 * LANE), lambda b, p, i: (b * nb + i, p)),
        out_shape=jax.ShapeDtypeStruct((t, GLA_V), F32),
        scratch_shapes=[pltpu.VMEM((2 * LANE, LANE), F32),
                        pltpu.VMEM((tb, LANE), F32), pltpu.VMEM((tb, LANE), F32), pltpu.VMEM((tb, LANE), F32),
                        pltpu.VMEM((tb, 2 * LANE), F32),
                        pltpu.VMEM((DD_NPAR, CHUNK, LANE), F32),
                        pltpu.VMEM((DD_NPAR, CHUNK * SUB, LANE), F32)],
        compiler_params=_cparams(("parallel", "parallel", "arbitrary")),
        name="gla",
    )(proj, proj, proj, proj, proj, w2p, b2, norm_w, lm)


def _hgrn_kernel(q_ref, f_ref, v_ref, gate_ref, lbl_ref, nw_ref, lm_ref,
                 o_ref, state_ref, q_s, k_s, la_s, o_s, c8_s, p_s, *, tb, layer):
    @pl.when(pl.program_id(2) == 0)
    def _():
        state_ref[...] = jnp.zeros_like(state_ref)

    logits = lbl_ref[...]
    mx = jnp.max(logits, axis=0, keepdims=True)
    ex = jnp.exp(logits - mx)
    p = ex / jnp.sum(ex, axis=0, keepdims=True)
    acc = p[0:1, :]
    for r in range(1, layer + 1):
        acc = acc + p[r:r + 1, :]
    lb = jnp.clip(acc - p[0:1, :], 0.0, 1.0)
    log_lb = jnp.log(jnp.maximum(lb, LB_FLOOR))
    log_1m = jnp.log1p(-lb)

    cf = f_ref[...]
    second = log_1m + _log_sigmoid(cf)
    mxab = jnp.maximum(log_lb, second)
    la_s[...] = mxab + jnp.log1p(jnp.exp(-jnp.abs(log_lb - second)))
    k_s[...] = (1.0 - lb) * _sigmoid(-cf)
    q_s[...] = _silu(q_ref[...]) * (HGRN_EXPAND ** -0.5)
    _dd_core(q_s, k_s, v_ref, la_s, o_s, state_ref, lm_ref, c8_s, p_s, tb=tb, g_heads=1)
    _gated_rms_out(o_s, gate_ref, nw_ref, o_ref, 1)


def _hgrn_call(proj, lb_logits, norm_w, lm, layer, batch, seq, tb=512):
    tb = min(tb, seq)
    nb = seq // tb
    t = batch * seq

    def tok(colblk):
        return pl.BlockSpec((tb, LANE), lambda b, h, i: (b * nb + i, colblk + h))

    kern = functools.partial(_hgrn_kernel, tb=tb, layer=layer)
    return pl.pallas_call(
        kern,
        grid=(batch, HGRN_HEADS, nb),
        in_specs=[tok(C_Q // LANE), tok(C_F // LANE), tok(C_I // LANE), tok(C_G // LANE),
                  pl.BlockSpec((DEPTH, LANE), lambda b, h, i: (0, h)),
                  pl.BlockSpec((None, 1, HGRN_DV), lambda b, h, i: (layer, 0, 0)),
                  pl.BlockSpec((2 * len(LEVELS) * CHUNK, CHUNK), lambda b, h, i: (0, 0))],
        out_specs=pl.BlockSpec((tb, LANE), lambda b, h, i: (b * nb + i, h)),
        out_shape=jax.ShapeDtypeStruct((t, HGRN_V), F32),
        scratch_shapes=[pltpu.VMEM((LANE, LANE), F32),
                        pltpu.VMEM((tb, LANE), F32), pltpu.VMEM((tb, LANE), F32), pltpu.VMEM((tb, LANE), F32),
                        pltpu.VMEM((tb, LANE), F32),
                        pltpu.VMEM((DD_NPAR, CHUNK, LANE), F32),
                        pltpu.VMEM((DD_NPAR, CHUNK * SUB, LANE), F32)],
        compiler_params=_cparams(("parallel", "parallel", "arbitrary")),
        name="hgrn",
    )(proj, proj, proj, proj, lb_logits, norm_w, lm)


def _merge_kernel(oa_ref, ob_ref, oc_ref, ma_ref, mb_ref, mc_ref, h_ref,
                  wa_ref, wb_ref, wc_ref, wo_ref, g_ref, b_ref, o_ref):
    y = (_sigmoid(ma_ref[...]) * _mm(oa_ref[...], wa_ref[...])
         + _sigmoid(mb_ref[...]) * _mm(ob_ref[...], wb_ref[...])
         + _sigmoid(mc_ref[...]) * _mm(oc_ref[...], wc_ref[...]))
    mix = _mm(y, wo_ref[...])
    o_ref[...] = _layer_norm(ALPHA * h_ref[...] + mix, g_ref[...], b_ref[...])


def _merge_call(o_a, o_b, o_c, proj, h, wa, wb, wc, wo, g, b, layer, tm=256):
    t, d = h.shape
    tm = min(tm, t)
    mblk = D_MODEL

    def row(width):
        return pl.BlockSpec((tm, width), lambda i: (i, 0))

    def wspec(kdim):
        return pl.BlockSpec((None, kdim, d), lambda i: (layer, 0, 0))

    vec = pl.BlockSpec((None, 1, d), lambda i: (layer, 0, 0))
    return pl.pallas_call(
        _merge_kernel,
        grid=(t // tm,),
        in_specs=[row(GDN_V), row(GLA_V), row(HGRN_V),
                  pl.BlockSpec((tm, mblk), lambda i: (i, M_A // mblk)),
                  pl.BlockSpec((tm, mblk), lambda i: (i, M_B // mblk)),
                  pl.BlockSpec((tm, mblk), lambda i: (i, M_C // mblk)),
                  row(d), wspec(GDN_V), wspec(GLA_V), wspec(HGRN_V), wspec(d), vec, vec],
        out_specs=row(d),
        out_shape=jax.ShapeDtypeStruct((t, d), F32),
        compiler_params=_cparams(("parallel",)),
        name="merge",
    )(o_a, o_b, o_c, proj, proj, proj, h, wa, wb, wc, wo, g, b)


def _route(scores_t, bias_ref):
    s = [scores_t[e:e + 1, :] for e in range(N_EXPERTS)]
    sel = [s[e] + bias_ref[e:e + 1, 0:1] for e in range(N_EXPERTS)]
    gscore = []
    for g in range(N_GROUPS):
        a, b, c, d = sel[4 * g:4 * g + 4]
        hi1, lo1 = jnp.maximum(a, b), jnp.minimum(a, b)
        hi2, lo2 = jnp.maximum(c, d), jnp.minimum(c, d)
        top1 = jnp.maximum(hi1, hi2)
        top2 = jnp.maximum(jnp.minimum(hi1, hi2), jnp.maximum(lo1, lo2))
        gscore.append(top1 + top2)
    best = gscore[0]
    gidx = jnp.zeros_like(best, dtype=jnp.int32)
    for g in range(1, N_GROUPS):
        take = gscore[g] > best
        best = jnp.where(take, gscore[g], best)
        gidx = jnp.where(take, g, gidx)
    ing, raw = [], []
    for kk in range(EXPERTS_PER_GROUP):
        vs, vr = sel[kk], s[kk]
        for g in range(1, N_GROUPS):
            pick = gidx == g
            vs = jnp.where(pick, sel[4 * g + kk], vs)
            vr = jnp.where(pick, s[4 * g + kk], vr)
        ing.append(vs)
        raw.append(vr)
    b1 = ing[0]
    i1 = jnp.zeros_like(gidx)
    for kk in range(1, EXPERTS_PER_GROUP):
        take = ing[kk] > b1
        b1 = jnp.where(take, ing[kk], b1)
        i1 = jnp.where(take, kk, i1)
    neg = jnp.full_like(b1, -jnp.inf)
    b2 = neg
    i2 = jnp.zeros_like(gidx)
    for kk in range(EXPERTS_PER_GROUP):
        cand = jnp.where(i1 == kk, neg, ing[kk])
        take = cand > b2
        b2 = jnp.where(take, cand, b2)
        i2 = jnp.where(take, kk, i2)
    w1 = raw[0]
    w2 = raw[0]
    for kk in range(1, EXPERTS_PER_GROUP):
        w1 = jnp.where(i1 == kk, raw[kk], w1)
        w2 = jnp.where(i2 == kk, raw[kk], w2)
    tot = w1 + w2
    w1 = w1 / tot
    w2 = w2 / tot
    e1 = gidx * EXPERTS_PER_GROUP + i1
    e2 = gidx * EXPERTS_PER_GROUP + i2
    rows = [jnp.where(e1 == e, w1, 0.0) + jnp.where(e2 == e, w2, 0.0) for e in range(N_EXPERTS)]
    return jnp.concatenate(rows, axis=0)


MOE_EPS = 4


def _moe_kernel(h_ref, wr_ref, rb_ref, wg_ref, wu_ref, wd_ref, g_ref, b_ref, o_ref,
                comb_ref, acc_ref, xb_ref, *, tm):
    eg = pl.program_id(1)

    @pl.when(eg == 0)
    def _():
        logits_t = _mm_nt_f32(wr_ref[...], h_ref[...])
        comb_t = _route(_sigmoid(logits_t), rb_ref)
        pad = jnp.zeros((LANE - N_EXPERTS, tm), F32)
        comb_ref[...] = jnp.transpose(jnp.concatenate([comb_t, pad], axis=0))
        xb_ref[...] = h_ref[...].astype(BF16)

    x = xb_ref[...]
    comb = comb_ref[...]
    y = None
    for kk in range(MOE_EPS):
        hg = jnp.dot(x, wg_ref[kk], preferred_element_type=F32)
        hu = jnp.dot(x, wu_ref[kk], preferred_element_type=F32)
        cw = _lane_col(comb, eg * MOE_EPS + kk)
        hid = _silu(hg) * hu * cw
        term = jnp.dot(hid.astype(BF16), wd_ref[kk], preferred_element_type=F32)
        y = term if y is None else y + term

    @pl.when(eg == 0)
    def _():
        acc_ref[...] = y

    @pl.when(eg > 0)
    def _():
        acc_ref[...] += y

    @pl.when(eg == N_EXPERTS // MOE_EPS - 1)
    def _():
        o_ref[...] = _layer_norm(ALPHA * h_ref[...] + acc_ref[...], g_ref[...], b_ref[...])


def _moe_call(h, wr_t, rbias, wg, wu, wd, g, b, layer, tm=1024):
    t, d = h.shape
    tm = min(tm, t)
    kern = functools.partial(_moe_kernel, tm=tm)
    vec = pl.BlockSpec((None, 1, d), lambda i, e: (layer, 0, 0))
    return pl.pallas_call(
        kern,
        grid=(t // tm, N_EXPERTS // MOE_EPS),
        in_specs=[pl.BlockSpec((tm, d), lambda i, e: (i, 0)),
                  pl.BlockSpec((N_EXPERTS, d), lambda i, e: (0, 0)),
                  pl.BlockSpec((N_EXPERTS, LANE), lambda i, e: (0, 0)),
                  pl.BlockSpec((None, MOE_EPS, d, D_FF), lambda i, e: (layer, e, 0, 0)),
                  pl.BlockSpec((None, MOE_EPS, d, D_FF), lambda i, e: (layer, e, 0, 0)),
                  pl.BlockSpec((None, MOE_EPS, D_FF, d), lambda i, e: (layer, e, 0, 0)),
                  vec, vec],
        out_specs=pl.BlockSpec((tm, d), lambda i, e: (i, 0)),
        out_shape=jax.ShapeDtypeStruct((t, d), F32),
        scratch_shapes=[pltpu.VMEM((tm, LANE), F32), pltpu.VMEM((tm, d), F32), pltpu.VMEM((tm, d), BF16)],
        compiler_params=_cparams(("parallel", "arbitrary")),
        name="moe",
    )(h, wr_t, rbias, wg, wu, wd, g, b)


def _pack_w_in(w_in):
    (a_q, a_k, a_v, a_beta, a_dt, a_g, b_q, b_k, b_v, b_lr, b_g,
     c_q, c_f, c_i, c_g, m_a, m_b, m_c) = jnp.split(w_in, SPLIT_POINTS, axis=-1)
    lead = w_in.shape[:-1]
    a_s = jnp.concatenate([a_beta, a_dt, jnp.zeros(lead + (LANE - 2 * GDN_HEADS,), w_in.dtype)], -1)
    b_lrp = jnp.concatenate([b_lr, jnp.zeros(lead + (LANE - GLA_RANK,), w_in.dtype)], -1)
    packed = jnp.concatenate([m_a, m_b, m_c, a_q, a_k, a_v, a_g, a_s, b_q, b_k, b_lrp, b_v, b_g,
                              c_q, c_f, c_i, c_g], -1)
    assert packed.shape[-1] == NP
    return packed.astype(BF16)


def _prepare(w_in, gdn_conv, gdn_a_log, gdn_dt_bias, gdn_norm, gla_w2, gla_b2, gla_norm, hgrn_lb_logits,
             hgrn_norm, w_br_a, w_br_b, w_br_c, w_out, ln1_g, ln1_b, w_router, router_bias, w_gate, w_up,
             w_down, ln2_g, ln2_b):
    depth = w_in.shape[0]
    d = w_out.shape[-1]
    return dict(
        w_packed=_pack_w_in(w_in),
        gdn_conv=gdn_conv,
        gdn_par=jnp.pad(jnp.stack([gdn_a_log, gdn_dt_bias], axis=1),
                        ((0, 0), (0, SUBLANE - 2), (GDN_HEADS, LANE - 2 * GDN_HEADS))),
        gdn_norm=gdn_norm.reshape(depth, 1, GDN_DV),
        w2p=jnp.concatenate([gla_w2, jnp.zeros((depth, LANE - GLA_RANK, GLA_QK), gla_w2.dtype)], axis=1),
        gla_b2=gla_b2.reshape(depth, 1, GLA_QK),
        gla_norm=gla_norm.reshape(depth, 1, GLA_DV),
        lb_logits=hgrn_lb_logits,
        hgrn_norm=hgrn_norm.reshape(depth, 1, HGRN_DV),
        lm=jnp.asarray(_level_matrix(), dtype=BF16),
        wa=w_br_a.astype(BF16), wb=w_br_b.astype(BF16), wc=w_br_c.astype(BF16), wo=w_out.astype(BF16),
        ln1_g=ln1_g.reshape(depth, 1, d), ln1_b=ln1_b.reshape(depth, 1, d),
        wr_t=jnp.transpose(w_router),
        rbias=jnp.broadcast_to(router_bias[:, None], (N_EXPERTS, LANE)),
        wg=w_gate.astype(BF16), wu=w_up.astype(BF16), wd=w_down.astype(BF16),
        ln2_g=ln2_g.reshape(depth, 1, d), ln2_b=ln2_b.reshape(depth, 1, d),
    )


def _mixer_block(h, p, layer, batch, seq):
    proj = _inproj_call(h, p["w_packed"], layer)
    o_a = _gdn4_call(proj, p["gdn_conv"], p["gdn_par"], p["gdn_norm"], layer, batch, seq)
    o_b = _gla_call(proj, p["w2p"], p["gla_b2"], p["gla_norm"], p["lm"], layer, batch, seq)
    o_c = _hgrn_call(proj, p["lb_logits"], p["hgrn_norm"], p["lm"], layer, batch, seq)
    return _merge_call(o_a, o_b, o_c, proj, h, p["wa"], p["wb"], p["wc"], p["wo"], p["ln1_g"], p["ln1_b"], layer)


def _ffn_block(h, p, layer):
    return _moe_call(h, p["wr_t"], p["rbias"], p["wg"], p["wu"], p["wd"], p["ln2_g"], p["ln2_b"], layer)


def kernel(x, ln0_g, ln0_b, w_in, gdn_conv, gdn_a_log, gdn_dt_bias, gdn_norm, gla_w2, gla_b2, gla_norm,
           hgrn_lb_logits, hgrn_norm, w_br_a, w_br_b, w_br_c, w_out, ln1_g, ln1_b, w_router, router_bias,
           w_gate, w_up, w_down, ln2_g, ln2_b):
    batch, seq, d = x.shape
    p = _prepare(w_in, gdn_conv, gdn_a_log, gdn_dt_bias, gdn_norm, gla_w2, gla_b2, gla_norm, hgrn_lb_logits,
                 hgrn_norm, w_br_a, w_br_b, w_br_c, w_out, ln1_g, ln1_b, w_router, router_bias, w_gate, w_up,
                 w_down, ln2_g, ln2_b)
    h = _ln_call(x.reshape(batch * seq, d), ln0_g, ln0_b)
    for layer in range(w_in.shape[0]):
        h = _mixer_block(h, p, layer, batch, seq)
        h = _ffn_block(h, p, layer)
    return h.reshape(batch, seq, d)
```

```python
import functools

import numpy as np
import jax
import jax.numpy as jnp
from jax import lax
from jax.experimental import pallas as pl
from jax.experimental.pallas import tpu as pltpu

F32 = jnp.float32
BF16 = jnp.bfloat16
HIGHEST = lax.Precision.HIGHEST

D_MODEL = 1024
DEPTH = 4
CHUNK = 64
GDN_HEADS, GDN_DK, GDN_DV, CONV_W = 4, 128, 128, 4
GLA_HEADS, GLA_DK, GLA_DV, GLA_RANK, GLA_NORMALIZER = 4, 64, 128, 16, 16.0
HGRN_HEADS, HGRN_EXPAND, HGRN_DV = 4, 128, 128
LB_FLOOR = 1e-30
N_EXPERTS, N_GROUPS, TOP_K, D_FF = 16, 4, 2, 256
EXPERTS_PER_GROUP = N_EXPERTS // N_GROUPS
ALPHA = (2.0 * DEPTH) ** 0.25
LN_EPS = 1e-5
RMS_EPS = 1e-6

GDN_QK = GDN_HEADS * GDN_DK
GDN_V = GDN_HEADS * GDN_DV
GLA_QK = GLA_HEADS * GLA_DK
GLA_V = GLA_HEADS * GLA_DV
HGRN_QK = HGRN_HEADS * HGRN_EXPAND
HGRN_V = HGRN_HEADS * HGRN_DV
SPLIT_SIZES = (GDN_QK, GDN_QK, GDN_V, GDN_HEADS, GDN_HEADS, GDN_V,
               GLA_QK, GLA_QK, GLA_V, GLA_RANK, GLA_V,
               HGRN_QK, HGRN_QK, HGRN_V, HGRN_V,
               D_MODEL, D_MODEL, D_MODEL)
SPLIT_POINTS = tuple(int(v) for v in np.cumsum(SPLIT_SIZES)[:-1])

LANE = 128
SUBLANE = 8
VMEM_LIMIT = 48 * 1024 * 1024

M_A, M_B, M_C = 0, 1024, 2048
A_Q, A_K, A_V, A_G = 3072, 3584, 4096, 4608
B_Q, B_K, B_V, B_G = 5120, 5376, 5632, 6144
C_Q, C_I, C_G = 6656, 7168, 7680
NPB = 8192
F_AS, F_LR, F_CF = 0, 128, 256
NPF = 768

SUB = 8
DD_NPAR = 4
GDN_NPAR = 2
MOE_EPS = 4


def _cparams(sem):
    return pltpu.CompilerParams(dimension_semantics=sem, vmem_limit_bytes=VMEM_LIMIT)


def _mm(a, b):
    return jnp.dot(a.astype(BF16), b.astype(BF16), preferred_element_type=F32)


def _mm_nt(a, b):
    return lax.dot_general(a.astype(BF16), b.astype(BF16), (((1,), (1,)), ((), ())),
                           preferred_element_type=F32)


def _mm_tn(a, b):
    return lax.dot_general(a.astype(BF16), b.astype(BF16), (((0,), (0,)), ((), ())),
                           preferred_element_type=F32)


def _mm_nt_f32(a, b):
    return lax.dot_general(a, b, (((1,), (1,)), ((), ())), preferred_element_type=F32,
                           precision=HIGHEST)


def _split3(x):
    hi = x.astype(BF16)
    r = x - hi.astype(F32)
    mid = r.astype(BF16)
    lo = (r - mid.astype(F32)).astype(BF16)
    return hi, mid, lo


def _mm_01(m01, x):
    hi, mid, lo = _split3(x)
    return (jnp.dot(m01, hi, preferred_element_type=F32) + jnp.dot(m01, mid, preferred_element_type=F32)
            + jnp.dot(m01, lo, preferred_element_type=F32))


def _sigmoid(x):
    return 1.0 / (1.0 + jnp.exp(-x))


def _sigmoid_t(x):
    return 0.5 * jnp.tanh(0.5 * x) + 0.5


def _silu(x):
    return x * _sigmoid_t(x)


def _softplus(x):
    return jnp.maximum(x, 0.0) + jnp.log(1.0 + jnp.exp(-jnp.abs(x)))


def _log_sigmoid(x):
    return -_softplus(-x)


def _layer_norm(x, g, b):
    mu = jnp.mean(x, axis=-1, keepdims=True)
    xc = x - mu
    var = jnp.mean(xc * xc, axis=-1, keepdims=True)
    return xc * lax.rsqrt(var + LN_EPS) * g + b


def _ln_kernel(x_ref, g_ref, b_ref, o_ref, ob_ref):
    y = _layer_norm(x_ref[...], g_ref[...], b_ref[...])
    o_ref[...] = y
    ob_ref[...] = y.astype(BF16)


def _ln_call(x, g, b, tm=512):
    t, d = x.shape
    tm = min(tm, t)
    return pl.pallas_call(
        _ln_kernel,
        grid=(t // tm,),
        in_specs=[pl.BlockSpec((tm, d), lambda i: (i, 0)),
                  pl.BlockSpec((1, d), lambda i: (0, 0)),
                  pl.BlockSpec((1, d), lambda i: (0, 0))],
        out_specs=[pl.BlockSpec((tm, d), lambda i: (i, 0)), pl.BlockSpec((tm, d), lambda i: (i, 0))],
        out_shape=[jax.ShapeDtypeStruct((t, d), F32), jax.ShapeDtypeStruct((t, d), BF16)],
        compiler_params=_cparams(("parallel",)),
        name="ln0",
    )(x, g.reshape(1, d), b.reshape(1, d))


def _inproj_kernel(x_ref, w_ref, o_ref):
    o_ref[...] = jnp.dot(x_ref[...], w_ref[...], preferred_element_type=F32).astype(o_ref.dtype)


def _inproj_call(hb, w, layer, out_dtype, tm, tn, name):
    t, d = hb.shape
    tm = min(tm, t)
    n = w.shape[-1]
    return pl.pallas_call(
        _inproj_kernel,
        grid=(n // tn, t // tm),
        in_specs=[pl.BlockSpec((tm, d), lambda j, i: (i, 0)),
                  pl.BlockSpec((None, d, tn), lambda j, i: (layer, 0, j))],
        out_specs=pl.BlockSpec((tm, tn), lambda j, i: (i, j)),
        out_shape=jax.ShapeDtypeStruct((t, n), out_dtype),
        compiler_params=_cparams(("parallel", "parallel")),
        name=name,
    )(hb, w)


def _gdn_kernel(qkv_ref, s_ref, gate_ref, cw_ref, par_ref, nw_ref, o_ref,
                state_ref, tail_ref, xbuf_ref, q_s, k_s, v_s, cumb_s, betab_s, cumrow_s, o_s, *, tb):
    blk = pl.program_id(1)
    nc = tb // CHUNK
    nh = GDN_HEADS
    off0 = SUBLANE - (CONV_W - 1)

    @pl.when(blk == 0)
    def _():
        state_ref[...] = jnp.zeros_like(state_ref)
        tail_ref[...] = jnp.zeros_like(tail_ref)

    xbuf_ref[0:SUBLANE, :] = tail_ref[...]
    xbuf_ref[SUBLANE:SUBLANE + tb, :] = qkv_ref[...].astype(F32)
    tail_ref[...] = xbuf_ref[tb:tb + SUBLANE, :]
    for j in range(3 * nh):
        cs = slice(j * LANE, (j + 1) * LANE)
        y = xbuf_ref[off0:off0 + tb, cs] * cw_ref[0:1, cs]
        for kk in range(1, CONV_W):
            y = y + xbuf_ref[off0 + kk:off0 + kk + tb, cs] * cw_ref[kk:kk + 1, cs]
        y = _silu(y)
        if j < nh:
            q_s[j] = y * lax.rsqrt(jnp.sum(y * y, axis=-1, keepdims=True) + RMS_EPS) * (GDN_DK ** -0.5)
        elif j < 2 * nh:
            k_s[j - nh] = y * lax.rsqrt(jnp.sum(y * y, axis=-1, keepdims=True) + RMS_EPS)
        else:
            v_s[j - 2 * nh] = y

    row = lax.broadcasted_iota(jnp.int32, (CHUNK, CHUNK), 0)
    col = lax.broadcasted_iota(jnp.int32, (CHUNK, CHUNK), 1)
    incl = col <= row
    strict = col < row
    tri = jnp.where(incl, 1.0, 0.0).astype(BF16)
    eye = jnp.where(col == row, 1.0, 0.0).astype(F32)

    sc = s_ref[...]
    beta_all = _sigmoid_t(sc)
    g_all = -jnp.exp(par_ref[0:1, :]) * _softplus(sc + par_ref[1:2, :])
    cum_all = jnp.concatenate([_mm_01(tri, g_all[c * CHUNK:(c + 1) * CHUNK, :]) for c in range(nc)], axis=0)
    cum_t = jnp.transpose(cum_all)
    for c in range(nc):
        cumrow_s[c] = cum_t[0:SUBLANE, c * CHUNK:(c + 1) * CHUNK]
    for h in range(nh):
        cumb_s[h] = jnp.broadcast_to(cum_all[:, nh + h:nh + h + 1], (tb, LANE))
        betab_s[h] = jnp.broadcast_to(beta_all[:, h:h + 1], (tb, LANE))

    npar = GDN_NPAR

    def group_body(cp, carry):
        chains = [(cp * npar + cc, h) for cc in range(npar) for h in range(nh)]
        r0s = [pl.multiple_of(c * CHUNK, CHUNK) for c, _ in chains]
        qc = [q_s[h, pl.ds(r0, CHUNK), :] for (_, h), r0 in zip(chains, r0s)]
        kc = [k_s[h, pl.ds(r0, CHUNK), :] for (_, h), r0 in zip(chains, r0s)]
        vc = [v_s[h, pl.ds(r0, CHUNK), :] for (_, h), r0 in zip(chains, r0s)]
        cumc = [cumb_s[h, pl.ds(r0, CHUNK), :] for (_, h), r0 in zip(chains, r0s)]
        bc = [betab_s[h, pl.ds(r0, CHUNK), :] for (_, h), r0 in zip(chains, r0s)]
        n = len(chains)
        kk = [_mm_nt(kc[i], kc[i]) for i in range(n)]
        qk = [_mm_nt(qc[i], kc[i]) for i in range(n)]
        decay = []
        for i, (c, h) in enumerate(chains):
            diff = cumc[i][:, 0:CHUNK] - cumrow_s[c][nh + h:nh + h + 1, :]
            decay.append(jnp.where(incl, jnp.exp(jnp.where(incl, diff, 0.0)), 0.0))
        a = [jnp.where(strict, bc[i][:, 0:CHUNK] * kk[i] * decay[i], 0.0) for i in range(n)]
        x = [eye - a[i] for i in range(n)]
        p = [_mm(a[i], a[i]) for i in range(n)]
        for it in range(5):
            x = [x[i] + _mm(x[i], p[i]) for i in range(n)]
            if it < 4:
                p = [_mm(p[i], p[i]) for i in range(n)]
        ecum = [jnp.exp(cumc[i]) for i in range(n)]
        sol = [_mm(x[i], jnp.concatenate([vc[i] * bc[i], kc[i] * (bc[i] * ecum[i])], axis=1)) for i in range(n)]
        attn = [qk[i] * decay[i] for i in range(n)]
        cum_last = [cumc[i][CHUNK - 1:CHUNK, :] for i in range(n)]
        k_state = [kc[i] * jnp.exp(cum_last[i] - cumc[i]) for i in range(n)]
        wqi = [jnp.concatenate([sol[i][:, GDN_DV:GDN_DV + GDN_DK], qc[i] * ecum[i]], axis=0) for i in range(n)]
        for cc in range(npar):
            idx = [cc * nh + h for h in range(nh)]
            s = [state_ref[h] for h in range(nh)]
            wq = [_mm(wqi[i], s[h]) for h, i in enumerate(idx)]
            v_new = [sol[i][:, 0:GDN_DV] - wq[h][0:CHUNK, :] for h, i in enumerate(idx)]
            av = [_mm(attn[i], v_new[h]) for h, i in enumerate(idx)]
            upd = [_mm_tn(k_state[i], v_new[h]) for h, i in enumerate(idx)]
            for h, i in enumerate(idx):
                state_ref[h] = s[h] * jnp.exp(cum_last[i]) + upd[h]
                o_s[pl.ds(r0s[i], CHUNK), h * LANE:(h + 1) * LANE] = wq[h][CHUNK:2 * CHUNK, :] + av[h]
        return carry

    lax.fori_loop(0, nc // npar, group_body, 0)

    for h in range(nh):
        hs = slice(h * LANE, (h + 1) * LANE)
        o = o_s[:, hs]
        oh = o * lax.rsqrt(jnp.mean(o * o, axis=-1, keepdims=True) + RMS_EPS) * nw_ref[...]
        o_ref[:, hs] = (oh * _silu(gate_ref[:, hs].astype(F32))).astype(o_ref.dtype)


def _gdn_call(pb, pf, conv_w, par, norm_w, layer, batch, seq, tb=512):
    tb = min(tb, seq)
    nb = seq // tb
    nc = tb // CHUNK
    t = batch * seq
    wq = 2 * GDN_QK + GDN_V
    kern = functools.partial(_gdn_kernel, tb=tb)
    return pl.pallas_call(
        kern,
        grid=(batch, nb),
        in_specs=[pl.BlockSpec((tb, wq), lambda b, i: (b * nb + i, A_Q // wq)),
                  pl.BlockSpec((tb, LANE), lambda b, i: (b * nb + i, F_AS // LANE)),
                  pl.BlockSpec((tb, GDN_V), lambda b, i: (b * nb + i, A_G // GDN_V)),
                  pl.BlockSpec((None, CONV_W, wq), lambda b, i: (layer, 0, 0)),
                  pl.BlockSpec((None, SUBLANE, LANE), lambda b, i: (layer, 0, 0)),
                  pl.BlockSpec((None, 1, GDN_DV), lambda b, i: (layer, 0, 0))],
        out_specs=pl.BlockSpec((tb, GDN_V), lambda b, i: (b * nb + i, 0)),
        out_shape=jax.ShapeDtypeStruct((t, GDN_V), BF16),
        scratch_shapes=[pltpu.VMEM((GDN_HEADS, GDN_DK, GDN_DV), F32),
                        pltpu.VMEM((SUBLANE, wq), F32),
                        pltpu.VMEM((tb + SUBLANE, wq), F32),
                        pltpu.VMEM((GDN_HEADS, tb, LANE), F32),
                        pltpu.VMEM((GDN_HEADS, tb, LANE), F32),
                        pltpu.VMEM((GDN_HEADS, tb, LANE), F32),
                        pltpu.VMEM((GDN_HEADS, tb, LANE), F32),
                        pltpu.VMEM((GDN_HEADS, tb, LANE), F32),
                        pltpu.VMEM((nc, SUBLANE, CHUNK), F32),
                        pltpu.VMEM((tb, GDN_V), F32)],
        compiler_params=_cparams(("parallel", "arbitrary")),
        name="gdn",
    )(pb, pf, pb, conv_w, par, norm_w)


def _dd_core(q_s, k_s, v_s, la_s, o_s, state_ref, c8_s, p_s, *, tb, g_heads):
    nchunk = tb // CHUNK
    nblk = CHUNK // SUB
    dkh = LANE // g_heads
    dvp = g_heads * LANE
    row = lax.broadcasted_iota(jnp.int32, (CHUNK, CHUNK), 0)
    col = lax.broadcasted_iota(jnp.int32, (CHUNK, CHUNK), 1)
    level_masks = []
    for sh in (5, 4, 3):
        same2b = jnp.right_shift(row, sh + 1) == jnp.right_shift(col, sh + 1)
        upper = (jnp.right_shift(row, sh) & 1) == 1
        lower = (jnp.right_shift(col, sh) & 1) == 0
        level_masks.append(jnp.where(same2b, jnp.where(upper, jnp.where(lower, 1.0, 0.0), 0.0), 0.0))
    tri8 = jnp.where(jnp.right_shift(row, 3) == jnp.right_shift(col, 3),
                     jnp.where(col <= row, 1.0, 0.0), 0.0).astype(BF16)
    lane128 = lax.broadcasted_iota(jnp.int32, (CHUNK, LANE), 1)
    head_masks = [jnp.where((lane128 >= g * dkh) & (lane128 < (g + 1) * dkh), 1.0, 0.0)
                  for g in range(g_heads)]
    sub = lax.broadcasted_iota(jnp.int32, (SUB, LANE), 0)
    dk_sh = dkh.bit_length() - 1
    lane_sh = LANE.bit_length() - 1
    orow = lax.broadcasted_iota(jnp.int32, (LANE, dvp), 0)
    ocol = lax.broadcasted_iota(jnp.int32, (LANE, dvp), 1)
    ones_bd = jnp.where(jnp.right_shift(orow, dk_sh) == jnp.right_shift(ocol, lane_sh), 1.0, 0.0).astype(BF16)
    srow = lax.broadcasted_iota(jnp.int32, (dvp, LANE), 0)
    scol = lax.broadcasted_iota(jnp.int32, (dvp, LANE), 1)
    state_mask = jnp.where(jnp.right_shift(srow, lane_sh) == jnp.right_shift(scol, dk_sh), 1.0, 0.0)
    npar = DD_NPAR

    def block_sums(i):
        c8 = [c8_s[i, b * SUB:(b + 1) * SUB, :] for b in range(nblk)]
        t8 = [c8_s[i, (b + 1) * SUB - 1:(b + 1) * SUB, :] for b in range(nblk)]
        t16 = [t8[2 * b] + t8[2 * b + 1] for b in range(nblk // 2)]
        t32 = [t16[2 * b] + t16[2 * b + 1] for b in range(nblk // 4)]
        t64 = t32[0] + t32[1]
        c16 = [c8[b] + t8[b - 1] if b % 2 else c8[b] for b in range(nblk)]
        c32 = [c16[b] + t16[b // 2 - 1] if (b // 2) % 2 else c16[b] for b in range(nblk)]
        c64 = [c32[b] + t32[0] if b >= nblk // 2 else c32[b] for b in range(nblk)]
        pre = {8: c8, 16: c16, 32: c32, 64: c64}
        suf = {8: [t8[b] - c8[b] for b in range(nblk)],
               16: [t16[b // 2] - c16[b] for b in range(nblk)],
               32: [t32[b // 4] - c32[b] for b in range(nblk)],
               64: [t64 - c64[b] for b in range(nblk)]}
        return pre, suf

    def cat(pieces):
        return jnp.concatenate(pieces, axis=0)

    def group_body(cg, carry):
        rng = range(npar)
        r0s = [pl.multiple_of((cg * npar + i) * CHUNK, CHUNK) for i in rng]
        qc = [q_s[pl.ds(r0, CHUNK), :] for r0 in r0s]
        kc = [k_s[pl.ds(r0, CHUNK), :] for r0 in r0s]
        vc = [v_s[pl.ds(r0, CHUNK), :] for r0 in r0s]
        c8_all = _mm_01(tri8, jnp.concatenate([la_s[pl.ds(r0, CHUNK), :] for r0 in r0s], axis=1))
        for i in rng:
            c8_s[i] = c8_all[:, i * LANE:(i + 1) * LANE]
        sums = [block_sums(i) for i in rng]
        attn = [[None] * g_heads for _ in rng]
        for li, b in enumerate((32, 16, 8)):
            qs = [qc[i] * jnp.exp(cat(sums[i][0][b])) for i in rng]
            ks = [kc[i] * jnp.exp(cat(sums[i][1][b])) for i in rng]
            for g in range(g_heads):
                for i in rng:
                    qg = qs[i] * head_masks[g] if g_heads > 1 else qs[i]
                    term = _mm_nt(qg, ks[i]) * level_masks[li]
                    attn[i][g] = term if li == 0 else attn[i][g] + term
        o = [jnp.concatenate([_mm(attn[i][g], vc[i][:, g * LANE:(g + 1) * LANE]) for g in range(g_heads)], axis=1)
             if g_heads > 1 else _mm(attn[i][0], vc[i]) for i in rng]
        for i in rng:
            c8 = sums[i][0][8]
            for r in range(nblk):
                qr = qc[i][r * SUB:(r + 1) * SUB, :]
                for jj in range(SUB):
                    krow = k_s[pl.ds(r0s[i] + r * SUB + jj, 1), :]
                    crow = c8_s[i, r * SUB + jj:r * SUB + jj + 1, :]
                    m = sub >= jj
                    pr = jnp.where(m, qr * krow * jnp.exp(jnp.where(m, c8[r] - crow, 0.0)), 0.0)
                    p_s[i, (r * SUB + jj) * SUB:(r * SUB + jj + 1) * SUB, :] = pr
        rs = [jnp.dot(p_s[i].astype(BF16), ones_bd, preferred_element_type=F32) for i in rng]
        for i in rng:
            od = []
            for r in range(nblk):
                acc = None
                for jj in range(SUB):
                    vrow = v_s[pl.ds(r0s[i] + r * SUB + jj, 1), :]
                    term = rs[i][(r * SUB + jj) * SUB:(r * SUB + jj + 1) * SUB, :] * vrow
                    acc = term if acc is None else acc + term
                od.append(acc)
            o[i] = o[i] + cat(od)
        q_inter = [qc[i] * jnp.exp(cat(sums[i][0][64])) for i in rng]
        k_state = [kc[i] * jnp.exp(cat(sums[i][1][64])) for i in rng]
        upd = [_mm_tn(vc[i], k_state[i]) for i in rng]
        st = state_ref[...]
        for i in rng:
            o_s[pl.ds(r0s[i], CHUNK), :] = o[i] + _mm_nt(q_inter[i], st)
            decay_last = jnp.exp(sums[i][0][64][nblk - 1][SUB - 1:SUB, :])
            st = st * decay_last + (upd[i] * state_mask if g_heads > 1 else upd[i])
        state_ref[...] = st
        return carry

    lax.fori_loop(0, nchunk // npar, group_body, 0)


def _gated_rms_out(o_s, gate_ref, nw_ref, o_ref, g_heads):
    for g in range(g_heads):
        hs = slice(g * LANE, (g + 1) * LANE)
        o = o_s[:, hs]
        oh = o * lax.rsqrt(jnp.mean(o * o, axis=-1, keepdims=True) + RMS_EPS) * nw_ref[...]
        o_ref[:, hs] = (oh * _silu(gate_ref[:, hs].astype(F32))).astype(o_ref.dtype)


def _gla_kernel(q_ref, k_ref, v_ref, lr_ref, gate_ref, w2_ref, b2_ref, nw_ref,
                o_ref, state_ref, q_s, k_s, v_s, la_s, o_s, c8_s, p_s, *, tb):
    @pl.when(pl.program_id(2) == 0)
    def _():
        state_ref[...] = jnp.zeros_like(state_ref)

    q_s[...] = q_ref[...].astype(F32) * (GLA_DK ** -0.5)
    k_s[...] = k_ref[...].astype(F32)
    v_s[...] = v_ref[...].astype(F32)
    z = _mm(lr_ref[...], w2_ref[...]) + b2_ref[...]
    la_s[...] = _log_sigmoid(z) * (1.0 / GLA_NORMALIZER)
    _dd_core(q_s, k_s, v_s, la_s, o_s, state_ref, c8_s, p_s, tb=tb, g_heads=2)
    _gated_rms_out(o_s, gate_ref, nw_ref, o_ref, 2)


def _dd_scratch(tb, dvp):
    return [pltpu.VMEM((dvp, LANE), F32),
            pltpu.VMEM((tb, LANE), F32), pltpu.VMEM((tb, LANE), F32),
            pltpu.VMEM((tb, dvp), F32),
            pltpu.VMEM((tb, LANE), F32),
            pltpu.VMEM((tb, dvp), F32),
            pltpu.VMEM((DD_NPAR, CHUNK, LANE), F32),
            pltpu.VMEM((DD_NPAR, CHUNK * SUB, LANE), F32)]


def _gla_call(pb, pf, w2p, b2, norm_w, layer, batch, seq, tb=512):
    tb = min(tb, seq)
    nb = seq // tb
    t = batch * seq
    npair = GLA_HEADS // 2
    kern = functools.partial(_gla_kernel, tb=tb)
    return pl.pallas_call(
        kern,
        grid=(batch, npair, nb),
        in_specs=[pl.BlockSpec((tb, LANE), lambda b, p, i: (b * nb + i, B_Q // LANE + p)),
                  pl.BlockSpec((tb, LANE), lambda b, p, i: (b * nb + i, B_K // LANE + p)),
                  pl.BlockSpec((tb, 2 * LANE), lambda b, p, i: (b * nb + i, B_V // (2 * LANE) + p)),
                  pl.BlockSpec((tb, LANE), lambda b, p, i: (b * nb + i, F_LR // LANE)),
                  pl.BlockSpec((tb, 2 * LANE), lambda b, p, i: (b * nb + i, B_G // (2 * LANE) + p)),
                  pl.BlockSpec((None, LANE, LANE), lambda b, p, i: (layer, 0, p)),
                  pl.BlockSpec((None, 1, LANE), lambda b, p, i: (layer, 0, p)),
                  pl.BlockSpec((None, 1, GLA_DV), lambda b, p, i: (layer, 0, 0))],
        out_specs=pl.BlockSpec((tb, 2 * LANE), lambda b, p, i: (b * nb + i, p)),
        out_shape=jax.ShapeDtypeStruct((t, GLA_V), BF16),
        scratch_shapes=_dd_scratch(tb, 2 * LANE),
        compiler_params=_cparams(("parallel", "parallel", "arbitrary")),
        name="gla",
    )(pb, pb, pb, pf, pb, w2p, b2, norm_w)


def _hgrn_kernel(q_ref, f_ref, v_ref, gate_ref, lbl_ref, nw_ref,
                 o_ref, state_ref, q_s, k_s, v_s, la_s, o_s, c8_s, p_s, *, tb, layer):
    @pl.when(pl.program_id(2) == 0)
    def _():
        state_ref[...] = jnp.zeros_like(state_ref)

    logits = lbl_ref[...]
    mx = jnp.max(logits, axis=0, keepdims=True)
    ex = jnp.exp(logits - mx)
    p = ex / jnp.sum(ex, axis=0, keepdims=True)
    acc = p[0:1, :]
    for r in range(1, layer + 1):
        acc = acc + p[r:r + 1, :]
    lb = jnp.clip(acc - p[0:1, :], 0.0, 1.0)
    log_lb = jnp.log(jnp.maximum(lb, LB_FLOOR))
    log_1m = jnp.log(1.0 - lb)

    cf = f_ref[...]
    second = log_1m + _log_sigmoid(cf)
    la_s[...] = jnp.maximum(log_lb, second) + jnp.log(1.0 + jnp.exp(-jnp.abs(log_lb - second)))
    k_s[...] = (1.0 - lb) * _sigmoid_t(-cf)
    q_s[...] = _silu(q_ref[...].astype(F32)) * (HGRN_EXPAND ** -0.5)
    v_s[...] = v_ref[...].astype(F32)
    _dd_core(q_s, k_s, v_s, la_s, o_s, state_ref, c8_s, p_s, tb=tb, g_heads=1)
    _gated_rms_out(o_s, gate_ref, nw_ref, o_ref, 1)


def _hgrn_call(pb, pf, lb_logits, norm_w, layer, batch, seq, tb=512):
    tb = min(tb, seq)
    nb = seq // tb
    t = batch * seq

    def tok(colblk):
        return pl.BlockSpec((tb, LANE), lambda b, h, i: (b * nb + i, colblk + h))

    kern = functools.partial(_hgrn_kernel, tb=tb, layer=layer)
    return pl.pallas_call(
        kern,
        grid=(batch, HGRN_HEADS, nb),
        in_specs=[tok(C_Q // LANE), tok(F_CF // LANE), tok(C_I // LANE), tok(C_G // LANE),
                  pl.BlockSpec((DEPTH, LANE), lambda b, h, i: (0, h)),
                  pl.BlockSpec((None, 1, HGRN_DV), lambda b, h, i: (layer, 0, 0))],
        out_specs=pl.BlockSpec((tb, LANE), lambda b, h, i: (b * nb + i, h)),
        out_shape=jax.ShapeDtypeStruct((t, HGRN_V), BF16),
        scratch_shapes=_dd_scratch(tb, LANE),
        compiler_params=_cparams(("parallel", "parallel", "arbitrary")),
        name="hgrn",
    )(pb, pf, pb, pb, lb_logits, norm_w)


def _merge_kernel(oa_ref, ob_ref, oc_ref, ma_ref, mb_ref, mc_ref, h_ref,
                  wa_ref, wb_ref, wc_ref, wo_ref, g_ref, b_ref, o_ref):
    y = (_sigmoid_t(ma_ref[...].astype(F32)) * jnp.dot(oa_ref[...], wa_ref[...], preferred_element_type=F32)
         + _sigmoid_t(mb_ref[...].astype(F32)) * jnp.dot(ob_ref[...], wb_ref[...], preferred_element_type=F32)
         + _sigmoid_t(mc_ref[...].astype(F32)) * jnp.dot(oc_ref[...], wc_ref[...], preferred_element_type=F32))
    mix = _mm(y, wo_ref[...])
    o_ref[...] = _layer_norm(ALPHA * h_ref[...] + mix, g_ref[...], b_ref[...])


def _merge_call(o_a, o_b, o_c, pb, h, wa, wb, wc, wo, g, b, layer, tm=512):
    t, d = h.shape
    tm = min(tm, t)

    def row(width):
        return pl.BlockSpec((tm, width), lambda i: (i, 0))

    def wspec(kdim):
        return pl.BlockSpec((None, kdim, d), lambda i: (layer, 0, 0))

    vec = pl.BlockSpec((None, 1, d), lambda i: (layer, 0, 0))
    return pl.pallas_call(
        _merge_kernel,
        grid=(t // tm,),
        in_specs=[row(GDN_V), row(GLA_V), row(HGRN_V),
                  pl.BlockSpec((tm, d), lambda i: (i, M_A // d)),
                  pl.BlockSpec((tm, d), lambda i: (i, M_B // d)),
                  pl.BlockSpec((tm, d), lambda i: (i, M_C // d)),
                  row(d), wspec(GDN_V), wspec(GLA_V), wspec(HGRN_V), wspec(d), vec, vec],
        out_specs=row(d),
        out_shape=jax.ShapeDtypeStruct((t, d), F32),
        compiler_params=_cparams(("parallel",)),
        name="merge",
    )(o_a, o_b, o_c, pb, pb, pb, h, wa, wb, wc, wo, g, b)


def _lane_col(x, idx):
    lane = lax.broadcasted_iota(jnp.int32, x.shape, 1)
    return jnp.sum(jnp.where(lane == idx, x, 0.0), axis=1, keepdims=True)


def _route(scores_t, bias_ref):
    s = [scores_t[e:e + 1, :] for e in range(N_EXPERTS)]
    sel = [s[e] + bias_ref[e:e + 1, 0:1] for e in range(N_EXPERTS)]
    gscore = []
    for g in range(N_GROUPS):
        a, b, c, d = sel[4 * g:4 * g + 4]
        hi1, lo1 = jnp.maximum(a, b), jnp.minimum(a, b)
        hi2, lo2 = jnp.maximum(c, d), jnp.minimum(c, d)
        top1 = jnp.maximum(hi1, hi2)
        top2 = jnp.maximum(jnp.minimum(hi1, hi2), jnp.maximum(lo1, lo2))
        gscore.append(top1 + top2)
    best = gscore[0]
    gidx = jnp.zeros_like(best, dtype=jnp.int32)
    for g in range(1, N_GROUPS):
        take = gscore[g] > best
        best = jnp.where(take, gscore[g], best)
        gidx = jnp.where(take, g, gidx)
    ing, raw = [], []
    for kk in range(EXPERTS_PER_GROUP):
        vs, vr = sel[kk], s[kk]
        for g in range(1, N_GROUPS):
            pick = gidx == g
            vs = jnp.where(pick, sel[4 * g + kk], vs)
            vr = jnp.where(pick, s[4 * g + kk], vr)
        ing.append(vs)
        raw.append(vr)
    b1 = ing[0]
    i1 = jnp.zeros_like(gidx)
    for kk in range(1, EXPERTS_PER_GROUP):
        take = ing[kk] > b1
        b1 = jnp.where(take, ing[kk], b1)
        i1 = jnp.where(take, kk, i1)
    neg = jnp.full_like(b1, -jnp.inf)
    b2 = neg
    i2 = jnp.zeros_like(gidx)
    for kk in range(EXPERTS_PER_GROUP):
        cand = jnp.where(i1 == kk, neg, ing[kk])
        take = cand > b2
        b2 = jnp.where(take, cand, b2)
        i2 = jnp.where(take, kk, i2)
    w1 = raw[0]
    w2 = raw[0]
    for kk in range(1, EXPERTS_PER_GROUP):
        w1 = jnp.where(i1 == kk, raw[kk], w1)
        w2 = jnp.where(i2 == kk, raw[kk], w2)
    tot = w1 + w2
    w1 = w1 / tot
    w2 = w2 / tot
    e1 = gidx * EXPERTS_PER_GROUP + i1
    e2 = gidx * EXPERTS_PER_GROUP + i2
    rows = [jnp.where(e1 == e, w1, 0.0) + jnp.where(e2 == e, w2, 0.0) for e in range(N_EXPERTS)]
    return jnp.concatenate(rows, axis=0)


def _moe_kernel(h_ref, wr_ref, rb_ref, wg_ref, wu_ref, wd_ref, g_ref, b_ref, o_ref, ob_ref,
                comb_ref, xb_ref, *, tm):
    eg = pl.program_id(1)

    @pl.when(eg == 0)
    def _():
        logits_t = _mm_nt_f32(wr_ref[...], h_ref[...])
        comb_t = _route(_sigmoid(logits_t), rb_ref)
        pad = jnp.zeros((LANE - N_EXPERTS, tm), F32)
        comb_ref[...] = jnp.transpose(jnp.concatenate([comb_t, pad], axis=0))
        xb_ref[...] = h_ref[...].astype(BF16)

    x = xb_ref[...]
    comb = comb_ref[...]
    y = None
    for kk in range(MOE_EPS):
        hg = jnp.dot(x, wg_ref[kk], preferred_element_type=F32)
        hu = jnp.dot(x, wu_ref[kk], preferred_element_type=F32)
        cw = _lane_col(comb, eg * MOE_EPS + kk)
        hid = _silu(hg) * hu * cw
        term = jnp.dot(hid.astype(BF16), wd_ref[kk], preferred_element_type=F32)
        y = term if y is None else y + term

    @pl.when(eg == 0)
    def _():
        o_ref[...] = y

    @pl.when(eg > 0)
    def _():
        o_ref[...] += y

    @pl.when(eg == N_EXPERTS // MOE_EPS - 1)
    def _():
        out = _layer_norm(ALPHA * h_ref[...] + o_ref[...], g_ref[...], b_ref[...])
        o_ref[...] = out
        ob_ref[...] = out.astype(BF16)


def _moe_call(h, wr_t, rbias, wg, wu, wd, g, b, layer, tm=1024):
    t, d = h.shape
    tm = min(tm, t)
    kern = functools.partial(_moe_kernel, tm=tm)
    vec = pl.BlockSpec((None, 1, d), lambda i, e: (layer, 0, 0))
    row = pl.BlockSpec((tm, d), lambda i, e: (i, 0))
    return pl.pallas_call(
        kern,
        grid=(t // tm, N_EXPERTS // MOE_EPS),
        in_specs=[row,
                  pl.BlockSpec((N_EXPERTS, d), lambda i, e: (0, 0)),
                  pl.BlockSpec((N_EXPERTS, LANE), lambda i, e: (0, 0)),
                  pl.BlockSpec((None, MOE_EPS, d, D_FF), lambda i, e: (layer, e, 0, 0)),
                  pl.BlockSpec((None, MOE_EPS, d, D_FF), lambda i, e: (layer, e, 0, 0)),
                  pl.BlockSpec((None, MOE_EPS, D_FF, d), lambda i, e: (layer, e, 0, 0)),
                  vec, vec],
        out_specs=[row, row],
        out_shape=[jax.ShapeDtypeStruct((t, d), F32), jax.ShapeDtypeStruct((t, d), BF16)],
        scratch_shapes=[pltpu.VMEM((tm, LANE), F32), pltpu.VMEM((tm, d), BF16)],
        compiler_params=_cparams(("parallel", "arbitrary")),
        name="moe",
    )(h, wr_t, rbias, wg, wu, wd, g, b)


def _pack_w_in(w_in):
    (a_q, a_k, a_v, a_beta, a_dt, a_g, b_q, b_k, b_v, b_lr, b_g,
     c_q, c_f, c_i, c_g, m_a, m_b, m_c) = jnp.split(w_in, SPLIT_POINTS, axis=-1)
    lead = w_in.shape[:-1]
    a_s = jnp.concatenate([a_beta, a_dt, jnp.zeros(lead + (LANE - 2 * GDN_HEADS,), w_in.dtype)], -1)
    b_lrp = jnp.concatenate([b_lr, jnp.zeros(lead + (LANE - GLA_RANK,), w_in.dtype)], -1)
    wb = jnp.concatenate([m_a, m_b, m_c, a_q, a_k, a_v, a_g, b_q, b_k, b_v, b_g, c_q, c_i, c_g], -1).astype(BF16)
    wf = jnp.concatenate([a_s, b_lrp, c_f], -1).astype(BF16)
    assert wb.shape[-1] == NPB and wf.shape[-1] == NPF
    return wb, wf


def _prepare(w_in, gdn_conv, gdn_a_log, gdn_dt_bias, gdn_norm, gla_w2, gla_b2, gla_norm, hgrn_lb_logits,
             hgrn_norm, w_br_a, w_br_b, w_br_c, w_out, ln1_g, ln1_b, w_router, router_bias, w_gate, w_up,
             w_down, ln2_g, ln2_b):
    depth = w_in.shape[0]
    d = w_out.shape[-1]
    w_pb, w_pf = _pack_w_in(w_in)
    return dict(
        w_pb=w_pb, w_pf=w_pf,
        gdn_conv=gdn_conv,
        gdn_par=jnp.pad(jnp.stack([gdn_a_log, gdn_dt_bias], axis=1),
                        ((0, 0), (0, SUBLANE - 2), (GDN_HEADS, LANE - 2 * GDN_HEADS))),
        gdn_norm=gdn_norm.reshape(depth, 1, GDN_DV),
        w2p=jnp.concatenate([gla_w2, jnp.zeros((depth, LANE - GLA_RANK, GLA_QK), gla_w2.dtype)], axis=1),
        gla_b2=gla_b2.reshape(depth, 1, GLA_QK),
        gla_norm=gla_norm.reshape(depth, 1, GLA_DV),
        lb_logits=hgrn_lb_logits,
        hgrn_norm=hgrn_norm.reshape(depth, 1, HGRN_DV),
        wa=w_br_a.astype(BF16), wb=w_br_b.astype(BF16), wc=w_br_c.astype(BF16), wo=w_out.astype(BF16),
        ln1_g=ln1_g.reshape(depth, 1, d), ln1_b=ln1_b.reshape(depth, 1, d),
        wr_t=jnp.transpose(w_router),
        rbias=jnp.broadcast_to(router_bias[:, None], (N_EXPERTS, LANE)),
        wg=w_gate.astype(BF16), wu=w_up.astype(BF16), wd=w_down.astype(BF16),
        ln2_g=ln2_g.reshape(depth, 1, d), ln2_b=ln2_b.reshape(depth, 1, d),
    )


def _mixer_block(h, hb, p, layer, batch, seq):
    pb = _inproj_call(hb, p["w_pb"], layer, BF16, 512, 2048, "inproj_b")
    pf = _inproj_call(hb, p["w_pf"], layer, F32, 1024, NPF, "inproj_f")
    o_a = _gdn_call(pb, pf, p["gdn_conv"], p["gdn_par"], p["gdn_norm"], layer, batch, seq)
    o_b = _gla_call(pb, pf, p["w2p"], p["gla_b2"], p["gla_norm"], layer, batch, seq)
    o_c = _hgrn_call(pb, pf, p["lb_logits"], p["hgrn_norm"], layer, batch, seq)
    return _merge_call(o_a, o_b, o_c, pb, h, p["wa"], p["wb"], p["wc"], p["wo"], p["ln1_g"], p["ln1_b"], layer)


def _ffn_block(h, p, layer):
    return _moe_call(h, p["wr_t"], p["rbias"], p["wg"], p["wu"], p["wd"], p["ln2_g"], p["ln2_b"], layer)


def kernel(x, ln0_g, ln0_b, w_in, gdn_conv, gdn_a_log, gdn_dt_bias, gdn_norm, gla_w2, gla_b2, gla_norm,
           hgrn_lb_logits, hgrn_norm, w_br_a, w_br_b, w_br_c, w_out, ln1_g, ln1_b, w_router, router_bias,
           w_gate, w_up, w_down, ln2_g, ln2_b):
    batch, seq, d = x.shape
    p = _prepare(w_in, gdn_conv, gdn_a_log, gdn_dt_bias, gdn_norm, gla_w2, gla_b2, gla_norm, hgrn_lb_logits,
                 hgrn_norm, w_br_a, w_br_b, w_br_c, w_out, ln1_g, ln1_b, w_router, router_bias, w_gate, w_up,
                 w_down, ln2_g, ln2_b)
    h, hb = _ln_call(x.reshape(batch * seq, d), ln0_g, ln0_b)
    for layer in range(w_in.shape[0]):
        h = _mixer_block(h, hb, p, layer, batch, seq)
        h, hb = _ffn_block(h, p, layer)
    return h.reshape(batch, seq, d)
```

```python
import functools

import numpy as np
import jax
import jax.numpy as jnp
from jax import lax
from jax.experimental import pallas as pl
from jax.experimental.pallas import tpu as pltpu

F32 = jnp.float32
BF16 = jnp.bfloat16
HIGHEST = lax.Precision.HIGHEST

D_MODEL = 1024
DEPTH = 4
CHUNK = 64
GDN_HEADS, GDN_DK, GDN_DV, CONV_W = 4, 128, 128, 4
GLA_HEADS, GLA_DK, GLA_DV, GLA_RANK, GLA_NORMALIZER = 4, 64, 128, 16, 16.0
HGRN_HEADS, HGRN_EXPAND, HGRN_DV = 4, 128, 128
LB_FLOOR = 1e-30
N_EXPERTS, N_GROUPS, TOP_K, D_FF = 16, 4, 2, 256
EXPERTS_PER_GROUP = N_EXPERTS // N_GROUPS
ALPHA = (2.0 * DEPTH) ** 0.25
LN_EPS = 1e-5
RMS_EPS = 1e-6

GDN_QK = GDN_HEADS * GDN_DK
GDN_V = GDN_HEADS * GDN_DV
GLA_QK = GLA_HEADS * GLA_DK
GLA_V = GLA_HEADS * GLA_DV
HGRN_QK = HGRN_HEADS * HGRN_EXPAND
HGRN_V = HGRN_HEADS * HGRN_DV
SPLIT_SIZES = (GDN_QK, GDN_QK, GDN_V, GDN_HEADS, GDN_HEADS, GDN_V,
               GLA_QK, GLA_QK, GLA_V, GLA_RANK, GLA_V,
               HGRN_QK, HGRN_QK, HGRN_V, HGRN_V,
               D_MODEL, D_MODEL, D_MODEL)
SPLIT_POINTS = tuple(int(v) for v in np.cumsum(SPLIT_SIZES)[:-1])

LANE = 128
SUBLANE = 8
VMEM_LIMIT = 48 * 1024 * 1024

M_A, M_B, M_C = 0, 1024, 2048
A_Q, A_K, A_V, A_G = 3072, 3584, 4096, 4608
B_Q, B_K, B_V, B_G = 5120, 5376, 5632, 6144
C_Q, C_I, C_G = 6656, 7168, 7680
NPB = 8192
F_AS, F_LR, F_CF = 0, 128, 256
NPF = 768

SUB = 8
DD_NPAR = 4
GDN_NPAR = 4
MOE_EPS = 4


def _cparams(sem):
    return pltpu.CompilerParams(dimension_semantics=sem, vmem_limit_bytes=VMEM_LIMIT)


def _mm(a, b):
    return jnp.dot(a.astype(BF16), b.astype(BF16), preferred_element_type=F32)


def _mm_nt(a, b):
    return lax.dot_general(a.astype(BF16), b.astype(BF16), (((1,), (1,)), ((), ())),
                           preferred_element_type=F32)


def _mm_tn(a, b):
    return lax.dot_general(a.astype(BF16), b.astype(BF16), (((0,), (0,)), ((), ())),
                           preferred_element_type=F32)


def _mm_nt_f32(a, b):
    return lax.dot_general(a, b, (((1,), (1,)), ((), ())), preferred_element_type=F32,
                           precision=HIGHEST)


def _split3(x):
    hi = x.astype(BF16)
    r = x - hi.astype(F32)
    mid = r.astype(BF16)
    lo = (r - mid.astype(F32)).astype(BF16)
    return hi, mid, lo


def _mm_01(m01, x):
    hi, mid, lo = _split3(x)
    return (jnp.dot(m01, hi, preferred_element_type=F32) + jnp.dot(m01, mid, preferred_element_type=F32)
            + jnp.dot(m01, lo, preferred_element_type=F32))


def _sigmoid(x):
    return 1.0 / (1.0 + jnp.exp(-x))


def _sigmoid_t(x):
    return 0.5 * jnp.tanh(0.5 * x) + 0.5


def _silu(x):
    return x * _sigmoid_t(x)


def _softplus(x):
    return jnp.maximum(x, 0.0) + jnp.log(1.0 + jnp.exp(-jnp.abs(x)))


def _log_sigmoid(x):
    return -_softplus(-x)


def _layer_norm(x, g, b):
    mu = jnp.mean(x, axis=-1, keepdims=True)
    xc = x - mu
    var = jnp.mean(xc * xc, axis=-1, keepdims=True)
    return xc * lax.rsqrt(var + LN_EPS) * g + b


def _ln_kernel(x_ref, g_ref, b_ref, o_ref, ob_ref):
    y = _layer_norm(x_ref[...], g_ref[...], b_ref[...])
    o_ref[...] = y
    ob_ref[...] = y.astype(BF16)


def _ln_call(x, g, b, tm=512):
    t, d = x.shape
    tm = min(tm, t)
    return pl.pallas_call(
        _ln_kernel,
        grid=(t // tm,),
        in_specs=[pl.BlockSpec((tm, d), lambda i: (i, 0)),
                  pl.BlockSpec((1, d), lambda i: (0, 0)),
                  pl.BlockSpec((1, d), lambda i: (0, 0))],
        out_specs=[pl.BlockSpec((tm, d), lambda i: (i, 0)), pl.BlockSpec((tm, d), lambda i: (i, 0))],
        out_shape=[jax.ShapeDtypeStruct((t, d), F32), jax.ShapeDtypeStruct((t, d), BF16)],
        compiler_params=_cparams(("parallel",)),
        name="ln0",
    )(x, g.reshape(1, d), b.reshape(1, d))


def _inproj_kernel(x_ref, w_ref, o_ref):
    o_ref[...] = jnp.dot(x_ref[...], w_ref[...], preferred_element_type=F32).astype(o_ref.dtype)


def _inproj_call(hb, w, layer, out_dtype, tm, tn, name):
    t, d = hb.shape
    tm = min(tm, t)
    n = w.shape[-1]
    return pl.pallas_call(
        _inproj_kernel,
        grid=(n // tn, t // tm),
        in_specs=[pl.BlockSpec((tm, d), lambda j, i: (i, 0)),
                  pl.BlockSpec((None, d, tn), lambda j, i: (layer, 0, j))],
        out_specs=pl.BlockSpec((tm, tn), lambda j, i: (i, j)),
        out_shape=jax.ShapeDtypeStruct((t, n), out_dtype),
        compiler_params=_cparams(("parallel", "parallel")),
        name=name,
    )(hb, w)


def _gdn_kernel(qkv_ref, s_ref, gate_ref, cw_ref, par_ref, nw_ref, o_ref,
                state_ref, tail_ref, xbuf_ref, q_s, k_s, v_s, cumb_s, betab_s, cumrow_s, o_s, *, tb):
    blk = pl.program_id(1)
    nc = tb // CHUNK
    nh = GDN_HEADS
    off0 = SUBLANE - (CONV_W - 1)

    @pl.when(blk == 0)
    def _():
        state_ref[...] = jnp.zeros_like(state_ref)
        tail_ref[...] = jnp.zeros_like(tail_ref)

    xbuf_ref[0:SUBLANE, :] = tail_ref[...]
    xbuf_ref[SUBLANE:SUBLANE + tb, :] = qkv_ref[...].astype(F32)
    tail_ref[...] = xbuf_ref[tb:tb + SUBLANE, :]
    for j in range(3 * nh):
        cs = slice(j * LANE, (j + 1) * LANE)
        y = xbuf_ref[off0:off0 + tb, cs] * cw_ref[0:1, cs]
        for kk in range(1, CONV_W):
            y = y + xbuf_ref[off0 + kk:off0 + kk + tb, cs] * cw_ref[kk:kk + 1, cs]
        y = _silu(y)
        if j < nh:
            q_s[j] = y * lax.rsqrt(jnp.sum(y * y, axis=-1, keepdims=True) + RMS_EPS) * (GDN_DK ** -0.5)
        elif j < 2 * nh:
            k_s[j - nh] = y * lax.rsqrt(jnp.sum(y * y, axis=-1, keepdims=True) + RMS_EPS)
        else:
            v_s[j - 2 * nh] = y

    row = lax.broadcasted_iota(jnp.int32, (CHUNK, CHUNK), 0)
    col = lax.broadcasted_iota(jnp.int32, (CHUNK, CHUNK), 1)
    incl = col <= row
    strict = col < row
    tri = jnp.where(incl, 1.0, 0.0).astype(BF16)
    eye = jnp.where(col == row, 1.0, 0.0).astype(F32)

    sc = s_ref[...]
    beta_all = _sigmoid_t(sc)
    g_all = -jnp.exp(par_ref[0:1, :]) * _softplus(sc + par_ref[1:2, :])
    cum_all = jnp.concatenate([_mm_01(tri, g_all[c * CHUNK:(c + 1) * CHUNK, :]) for c in range(nc)], axis=0)
    cum_t = jnp.transpose(cum_all)
    for c in range(nc):
        cumrow_s[c] = cum_t[0:SUBLANE, c * CHUNK:(c + 1) * CHUNK]
    for h in range(nh):
        cumb_s[h] = jnp.broadcast_to(cum_all[:, nh + h:nh + h + 1], (tb, LANE))
        betab_s[h] = jnp.broadcast_to(beta_all[:, h:h + 1], (tb, LANE))

    npar = GDN_NPAR

    def group_body(cp, carry):
        chains = [(cp * npar + cc, h) for cc in range(npar) for h in range(nh)]
        r0s = [pl.multiple_of(c * CHUNK, CHUNK) for c, _ in chains]
        qc = [q_s[h, pl.ds(r0, CHUNK), :] for (_, h), r0 in zip(chains, r0s)]
        kc = [k_s[h, pl.ds(r0, CHUNK), :] for (_, h), r0 in zip(chains, r0s)]
        vc = [v_s[h, pl.ds(r0, CHUNK), :] for (_, h), r0 in zip(chains, r0s)]
        cumc = [cumb_s[h, pl.ds(r0, CHUNK), :] for (_, h), r0 in zip(chains, r0s)]
        bc = [betab_s[h, pl.ds(r0, CHUNK), :] for (_, h), r0 in zip(chains, r0s)]
        n = len(chains)
        kk = [_mm_nt(kc[i], kc[i]) for i in range(n)]
        qk = [_mm_nt(qc[i], kc[i]) for i in range(n)]
        decay = []
        for i, (c, h) in enumerate(chains):
            diff = cumc[i][:, 0:CHUNK] - cumrow_s[c][nh + h:nh + h + 1, :]
            decay.append(jnp.where(incl, jnp.exp(jnp.where(incl, diff, 0.0)), 0.0))
        a = [jnp.where(strict, bc[i][:, 0:CHUNK] * kk[i] * decay[i], 0.0) for i in range(n)]
        x = [eye - a[i] for i in range(n)]
        p = [_mm(a[i], a[i]) for i in range(n)]
        for it in range(5):
            x = [x[i] + _mm(x[i], p[i]) for i in range(n)]
            if it < 4:
                p = [_mm(p[i], p[i]) for i in range(n)]
        ecum = [jnp.exp(cumc[i]) for i in range(n)]
        sol = [_mm(x[i], jnp.concatenate([vc[i] * bc[i], kc[i] * (bc[i] * ecum[i])], axis=1)) for i in range(n)]
        attn = [qk[i] * decay[i] for i in range(n)]
        cum_last = [cumc[i][CHUNK - 1:CHUNK, :] for i in range(n)]
        k_state = [kc[i] * jnp.exp(cum_last[i] - cumc[i]) for i in range(n)]
        wqi = [jnp.concatenate([sol[i][:, GDN_DV:GDN_DV + GDN_DK], qc[i] * ecum[i]], axis=0) for i in range(n)]
        for cc in range(npar):
            idx = [cc * nh + h for h in range(nh)]
            s = [state_ref[h] for h in range(nh)]
            wq = [_mm(wqi[i], s[h]) for h, i in enumerate(idx)]
            v_new = [sol[i][:, 0:GDN_DV] - wq[h][0:CHUNK, :] for h, i in enumerate(idx)]
            av = [_mm(attn[i], v_new[h]) for h, i in enumerate(idx)]
            upd = [_mm_tn(k_state[i], v_new[h]) for h, i in enumerate(idx)]
            for h, i in enumerate(idx):
                state_ref[h] = s[h] * jnp.exp(cum_last[i]) + upd[h]
                o_s[pl.ds(r0s[i], CHUNK), h * LANE:(h + 1) * LANE] = wq[h][CHUNK:2 * CHUNK, :] + av[h]
        return carry

    lax.fori_loop(0, nc // npar, group_body, 0)

    for h in range(nh):
        hs = slice(h * LANE, (h + 1) * LANE)
        o = o_s[:, hs]
        oh = o * lax.rsqrt(jnp.mean(o * o, axis=-1, keepdims=True) + RMS_EPS) * nw_ref[...]
        o_ref[:, hs] = (oh * _silu(gate_ref[:, hs].astype(F32))).astype(o_ref.dtype)


def _gdn_call(pb, pf, conv_w, par, norm_w, layer, batch, seq, tb=512):
    tb = min(tb, seq)
    nb = seq // tb
    nc = tb // CHUNK
    t = batch * seq
    wq = 2 * GDN_QK + GDN_V
    kern = functools.partial(_gdn_kernel, tb=tb)
    return pl.pallas_call(
        kern,
        grid=(batch, nb),
        in_specs=[pl.BlockSpec((tb, wq), lambda b, i: (b * nb + i, A_Q // wq)),
                  pl.BlockSpec((tb, LANE), lambda b, i: (b * nb + i, F_AS // LANE)),
                  pl.BlockSpec((tb, GDN_V), lambda b, i: (b * nb + i, A_G // GDN_V)),
                  pl.BlockSpec((None, CONV_W, wq), lambda b, i: (layer, 0, 0)),
                  pl.BlockSpec((None, SUBLANE, LANE), lambda b, i: (layer, 0, 0)),
                  pl.BlockSpec((None, 1, GDN_DV), lambda b, i: (layer, 0, 0))],
        out_specs=pl.BlockSpec((tb, GDN_V), lambda b, i: (b * nb + i, 0)),
        out_shape=jax.ShapeDtypeStruct((t, GDN_V), BF16),
        scratch_shapes=[pltpu.VMEM((GDN_HEADS, GDN_DK, GDN_DV), F32),
                        pltpu.VMEM((SUBLANE, wq), F32),
                        pltpu.VMEM((tb + SUBLANE, wq), F32),
                        pltpu.VMEM((GDN_HEADS, tb, LANE), F32),
                        pltpu.VMEM((GDN_HEADS, tb, LANE), F32),
                        pltpu.VMEM((GDN_HEADS, tb, LANE), F32),
                        pltpu.VMEM((GDN_HEADS, tb, LANE), F32),
                        pltpu.VMEM((GDN_HEADS, tb, LANE), F32),
                        pltpu.VMEM((nc, SUBLANE, CHUNK), F32),
                        pltpu.VMEM((tb, GDN_V), F32)],
        compiler_params=_cparams(("parallel", "arbitrary")),
        name="gdn",
    )(pb, pf, pb, conv_w, par, norm_w)


def _dd_core(q_s, k_s, v_s, la_s, o_s, state_ref, c8_s, p_s, *, tb, g_heads):
    nchunk = tb // CHUNK
    nblk = CHUNK // SUB
    dkh = LANE // g_heads
    dvp = g_heads * LANE
    row = lax.broadcasted_iota(jnp.int32, (CHUNK, CHUNK), 0)
    col = lax.broadcasted_iota(jnp.int32, (CHUNK, CHUNK), 1)
    level_masks = []
    for sh in (5, 4, 3):
        same2b = jnp.right_shift(row, sh + 1) == jnp.right_shift(col, sh + 1)
        upper = (jnp.right_shift(row, sh) & 1) == 1
        lower = (jnp.right_shift(col, sh) & 1) == 0
        level_masks.append(jnp.where(same2b, jnp.where(upper, jnp.where(lower, 1.0, 0.0), 0.0), 0.0))
    tri8 = jnp.where(jnp.right_shift(row, 3) == jnp.right_shift(col, 3),
                     jnp.where(col <= row, 1.0, 0.0), 0.0).astype(BF16)
    lane128 = lax.broadcasted_iota(jnp.int32, (CHUNK, LANE), 1)
    head_masks = [jnp.where((lane128 >= g * dkh) & (lane128 < (g + 1) * dkh), 1.0, 0.0)
                  for g in range(g_heads)]
    sub = lax.broadcasted_iota(jnp.int32, (SUB, LANE), 0)
    dk_sh = dkh.bit_length() - 1
    lane_sh = LANE.bit_length() - 1
    orow = lax.broadcasted_iota(jnp.int32, (LANE, dvp), 0)
    ocol = lax.broadcasted_iota(jnp.int32, (LANE, dvp), 1)
    ones_bd = jnp.where(jnp.right_shift(orow, dk_sh) == jnp.right_shift(ocol, lane_sh), 1.0, 0.0).astype(BF16)
    srow = lax.broadcasted_iota(jnp.int32, (dvp, LANE), 0)
    scol = lax.broadcasted_iota(jnp.int32, (dvp, LANE), 1)
    state_mask = jnp.where(jnp.right_shift(srow, lane_sh) == jnp.right_shift(scol, dk_sh), 1.0, 0.0)
    npar = DD_NPAR

    def block_sums(i):
        c8 = [c8_s[i, b * SUB:(b + 1) * SUB, :] for b in range(nblk)]
        t8 = [c8_s[i, (b + 1) * SUB - 1:(b + 1) * SUB, :] for b in range(nblk)]
        t16 = [t8[2 * b] + t8[2 * b + 1] for b in range(nblk // 2)]
        t32 = [t16[2 * b] + t16[2 * b + 1] for b in range(nblk // 4)]
        t64 = t32[0] + t32[1]
        c16 = [c8[b] + t8[b - 1] if b % 2 else c8[b] for b in range(nblk)]
        c32 = [c16[b] + t16[b // 2 - 1] if (b // 2) % 2 else c16[b] for b in range(nblk)]
        c64 = [c32[b] + t32[0] if b >= nblk // 2 else c32[b] for b in range(nblk)]
        pre = {8: c8, 16: c16, 32: c32, 64: c64}
        suf = {8: [t8[b] - c8[b] for b in range(nblk)],
               16: [t16[b // 2] - c16[b] for b in range(nblk)],
               32: [t32[b // 4] - c32[b] for b in range(nblk)],
               64: [t64 - c64[b] for b in range(nblk)]}
        return pre, suf

    def cat(pieces):
        return jnp.concatenate(pieces, axis=0)

    def group_body(cg, carry):
        rng = range(npar)
        r0s = [pl.multiple_of((cg * npar + i) * CHUNK, CHUNK) for i in rng]
        qc = [q_s[pl.ds(r0, CHUNK), :] for r0 in r0s]
        kc = [k_s[pl.ds(r0, CHUNK), :] for r0 in r0s]
        vc = [v_s[pl.ds(r0, CHUNK), :] for r0 in r0s]
        c8_all = _mm_01(tri8, jnp.concatenate([la_s[pl.ds(r0, CHUNK), :] for r0 in r0s], axis=1))
        for i in rng:
            c8_s[i] = c8_all[:, i * LANE:(i + 1) * LANE]
        sums = [block_sums(i) for i in rng]
        attn = [[None] * g_heads for _ in rng]
        for li, b in enumerate((32, 16, 8)):
            qs = [qc[i] * jnp.exp(cat(sums[i][0][b])) for i in rng]
            ks = [kc[i] * jnp.exp(cat(sums[i][1][b])) for i in rng]
            for g in range(g_heads):
                for i in rng:
                    qg = qs[i] * head_masks[g] if g_heads > 1 else qs[i]
                    term = _mm_nt(qg, ks[i]) * level_masks[li]
                    attn[i][g] = term if li == 0 else attn[i][g] + term
        o = [jnp.concatenate([_mm(attn[i][g], vc[i][:, g * LANE:(g + 1) * LANE]) for g in range(g_heads)], axis=1)
             if g_heads > 1 else _mm(attn[i][0], vc[i]) for i in rng]
        for i in rng:
            c8 = sums[i][0][8]
            for r in range(nblk):
                qr = qc[i][r * SUB:(r + 1) * SUB, :]
                for jj in range(SUB):
                    krow = k_s[pl.ds(r0s[i] + r * SUB + jj, 1), :]
                    crow = c8_s[i, r * SUB + jj:r * SUB + jj + 1, :]
                    m = sub >= jj
                    pr = qr * krow * jnp.exp(jnp.where(m, c8[r] - crow, -jnp.inf))
                    p_s[i, (r * SUB + jj) * SUB:(r * SUB + jj + 1) * SUB, :] = pr
        rs = [jnp.dot(p_s[i].astype(BF16), ones_bd, preferred_element_type=F32) for i in rng]
        for i in rng:
            od = []
            for r in range(nblk):
                acc = None
                for jj in range(SUB):
                    vrow = v_s[pl.ds(r0s[i] + r * SUB + jj, 1), :]
                    term = rs[i][(r * SUB + jj) * SUB:(r * SUB + jj + 1) * SUB, :] * vrow
                    acc = term if acc is None else acc + term
                od.append(acc)
            o[i] = o[i] + cat(od)
        q_inter = [qc[i] * jnp.exp(cat(sums[i][0][64])) for i in rng]
        k_state = [kc[i] * jnp.exp(cat(sums[i][1][64])) for i in rng]
        upd = [_mm_tn(vc[i], k_state[i]) for i in rng]
        st = state_ref[...]
        for i in rng:
            o_s[pl.ds(r0s[i], CHUNK), :] = o[i] + _mm_nt(q_inter[i], st)
            decay_last = jnp.exp(sums[i][0][64][nblk - 1][SUB - 1:SUB, :])
            st = st * decay_last + (upd[i] * state_mask if g_heads > 1 else upd[i])
        state_ref[...] = st
        return carry

    lax.fori_loop(0, nchunk // npar, group_body, 0)


def _gated_rms_out(o_s, gate_ref, nw_ref, o_ref, g_heads):
    for g in range(g_heads):
        hs = slice(g * LANE, (g + 1) * LANE)
        o = o_s[:, hs]
        oh = o * lax.rsqrt(jnp.mean(o * o, axis=-1, keepdims=True) + RMS_EPS) * nw_ref[...]
        o_ref[:, hs] = (oh * _silu(gate_ref[:, hs].astype(F32))).astype(o_ref.dtype)


def _gla_kernel(q_ref, k_ref, v_ref, lr_ref, gate_ref, w2_ref, b2_ref, nw_ref,
                o_ref, state_ref, q_s, k_s, v_s, la_s, o_s, c8_s, p_s, *, tb):
    @pl.when(pl.program_id(2) == 0)
    def _():
        state_ref[...] = jnp.zeros_like(state_ref)

    q_s[...] = q_ref[...].astype(F32) * (GLA_DK ** -0.5)
    k_s[...] = k_ref[...].astype(F32)
    v_s[...] = v_ref[...].astype(F32)
    z = _mm(lr_ref[...], w2_ref[...]) + b2_ref[...]
    la_s[...] = _log_sigmoid(z) * (1.0 / GLA_NORMALIZER)
    _dd_core(q_s, k_s, v_s, la_s, o_s, state_ref, c8_s, p_s, tb=tb, g_heads=2)
    _gated_rms_out(o_s, gate_ref, nw_ref, o_ref, 2)


def _dd_scratch(tb, dvp):
    return [pltpu.VMEM((dvp, LANE), F32),
            pltpu.VMEM((tb, LANE), F32), pltpu.VMEM((tb, LANE), F32),
            pltpu.VMEM((tb, dvp), F32),
            pltpu.VMEM((tb, LANE), F32),
            pltpu.VMEM((tb, dvp), F32),
            pltpu.VMEM((DD_NPAR, CHUNK, LANE), F32),
            pltpu.VMEM((DD_NPAR, CHUNK * SUB, LANE), F32)]


def _gla_call(pb, pf, w2p, b2, norm_w, layer, batch, seq, tb=512):
    tb = min(tb, seq)
    nb = seq // tb
    t = batch * seq
    npair = GLA_HEADS // 2
    kern = functools.partial(_gla_kernel, tb=tb)
    return pl.pallas_call(
        kern,
        grid=(batch, npair, nb),
        in_specs=[pl.BlockSpec((tb, LANE), lambda b, p, i: (b * nb + i, B_Q // LANE + p)),
                  pl.BlockSpec((tb, LANE), lambda b, p, i: (b * nb + i, B_K // LANE + p)),
                  pl.BlockSpec((tb, 2 * LANE), lambda b, p, i: (b * nb + i, B_V // (2 * LANE) + p)),
                  pl.BlockSpec((tb, LANE), lambda b, p, i: (b * nb + i, F_LR // LANE)),
                  pl.BlockSpec((tb, 2 * LANE), lambda b, p, i: (b * nb + i, B_G // (2 * LANE) + p)),
                  pl.BlockSpec((None, LANE, LANE), lambda b, p, i: (layer, 0, p)),
                  pl.BlockSpec((None, 1, LANE), lambda b, p, i: (layer, 0, p)),
                  pl.BlockSpec((None, 1, GLA_DV), lambda b, p, i: (layer, 0, 0))],
        out_specs=pl.BlockSpec((tb, 2 * LANE), lambda b, p, i: (b * nb + i, p)),
        out_shape=jax.ShapeDtypeStruct((t, GLA_V), BF16),
        scratch_shapes=_dd_scratch(tb, 2 * LANE),
        compiler_params=_cparams(("parallel", "parallel", "arbitrary")),
        name="gla",
    )(pb, pb, pb, pf, pb, w2p, b2, norm_w)


def _hgrn_kernel(q_ref, f_ref, v_ref, gate_ref, lbl_ref, nw_ref,
                 o_ref, state_ref, q_s, k_s, v_s, la_s, o_s, c8_s, p_s, *, tb, layer):
    @pl.when(pl.program_id(2) == 0)
    def _():
        state_ref[...] = jnp.zeros_like(state_ref)

    logits = lbl_ref[...]
    mx = jnp.max(logits, axis=0, keepdims=True)
    ex = jnp.exp(logits - mx)
    p = ex / jnp.sum(ex, axis=0, keepdims=True)
    acc = p[0:1, :]
    for r in range(1, layer + 1):
        acc = acc + p[r:r + 1, :]
    lb = jnp.clip(acc - p[0:1, :], 0.0, 1.0)
    log_lb = jnp.log(jnp.maximum(lb, LB_FLOOR))
    log_1m = jnp.log(1.0 - lb)

    cf = f_ref[...]
    second = log_1m + _log_sigmoid(cf)
    la_s[...] = jnp.maximum(log_lb, second) + jnp.log(1.0 + jnp.exp(-jnp.abs(log_lb - second)))
    k_s[...] = (1.0 - lb) * _sigmoid_t(-cf)
    q_s[...] = _silu(q_ref[...].astype(F32)) * (HGRN_EXPAND ** -0.5)
    v_s[...] = v_ref[...].astype(F32)
    _dd_core(q_s, k_s, v_s, la_s, o_s, state_ref, c8_s, p_s, tb=tb, g_heads=1)
    _gated_rms_out(o_s, gate_ref, nw_ref, o_ref, 1)


def _hgrn_call(pb, pf, lb_logits, norm_w, layer, batch, seq, tb=512):
    tb = min(tb, seq)
    nb = seq // tb
    t = batch * seq

    def tok(colblk):
        return pl.BlockSpec((tb, LANE), lambda b, h, i: (b * nb + i, colblk + h))

    kern = functools.partial(_hgrn_kernel, tb=tb, layer=layer)
    return pl.pallas_call(
        kern,
        grid=(batch, HGRN_HEADS, nb),
        in_specs=[tok(C_Q // LANE), tok(F_CF // LANE), tok(C_I // LANE), tok(C_G // LANE),
                  pl.BlockSpec((DEPTH, LANE), lambda b, h, i: (0, h)),
                  pl.BlockSpec((None, 1, HGRN_DV), lambda b, h, i: (layer, 0, 0))],
        out_specs=pl.BlockSpec((tb, LANE), lambda b, h, i: (b * nb + i, h)),
        out_shape=jax.ShapeDtypeStruct((t, HGRN_V), BF16),
        scratch_shapes=_dd_scratch(tb, LANE),
        compiler_params=_cparams(("parallel", "parallel", "arbitrary")),
        name="hgrn",
    )(pb, pf, pb, pb, lb_logits, norm_w)


def _merge_kernel(oa_ref, ob_ref, oc_ref, ma_ref, mb_ref, mc_ref, h_ref,
                  wa_ref, wb_ref, wc_ref, wo_ref, g_ref, b_ref, o_ref):
    y = (_sigmoid_t(ma_ref[...].astype(F32)) * jnp.dot(oa_ref[...], wa_ref[...], preferred_element_type=F32)
         + _sigmoid_t(mb_ref[...].astype(F32)) * jnp.dot(ob_ref[...], wb_ref[...], preferred_element_type=F32)
         + _sigmoid_t(mc_ref[...].astype(F32)) * jnp.dot(oc_ref[...], wc_ref[...], preferred_element_type=F32))
    mix = _mm(y, wo_ref[...])
    o_ref[...] = _layer_norm(ALPHA * h_ref[...] + mix, g_ref[...], b_ref[...])


def _merge_call(o_a, o_b, o_c, pb, h, wa, wb, wc, wo, g, b, layer, tm=512):
    t, d = h.shape
    tm = min(tm, t)

    def row(width):
        return pl.BlockSpec((tm, width), lambda i: (i, 0))

    def wspec(kdim):
        return pl.BlockSpec((None, kdim, d), lambda i: (layer, 0, 0))

    vec = pl.BlockSpec((None, 1, d), lambda i: (layer, 0, 0))
    return pl.pallas_call(
        _merge_kernel,
        grid=(t // tm,),
        in_specs=[row(GDN_V), row(GLA_V), row(HGRN_V),
                  pl.BlockSpec((tm, d), lambda i: (i, M_A // d)),
                  pl.BlockSpec((tm, d), lambda i: (i, M_B // d)),
                  pl.BlockSpec((tm, d), lambda i: (i, M_C // d)),
                  row(d), wspec(GDN_V), wspec(GLA_V), wspec(HGRN_V), wspec(d), vec, vec],
        out_specs=row(d),
        out_shape=jax.ShapeDtypeStruct((t, d), F32),
        compiler_params=_cparams(("parallel",)),
        name="merge",
    )(o_a, o_b, o_c, pb, pb, pb, h, wa, wb, wc, wo, g, b)


def _lane_col(x, idx):
    lane = lax.broadcasted_iota(jnp.int32, x.shape, 1)
    return jnp.sum(jnp.where(lane == idx, x, 0.0), axis=1, keepdims=True)


def _route(scores_t, bias_ref):
    s = [scores_t[e:e + 1, :] for e in range(N_EXPERTS)]
    sel = [s[e] + bias_ref[e:e + 1, 0:1] for e in range(N_EXPERTS)]
    gscore = []
    for g in range(N_GROUPS):
        a, b, c, d = sel[4 * g:4 * g + 4]
        hi1, lo1 = jnp.maximum(a, b), jnp.minimum(a, b)
        hi2, lo2 = jnp.maximum(c, d), jnp.minimum(c, d)
        top1 = jnp.maximum(hi1, hi2)
        top2 = jnp.maximum(jnp.minimum(hi1, hi2), jnp.maximum(lo1, lo2))
        gscore.append(top1 + top2)
    best = gscore[0]
    gidx = jnp.zeros_like(best, dtype=jnp.int32)
    for g in range(1, N_GROUPS):
        take = gscore[g] > best
        best = jnp.where(take, gscore[g], best)
        gidx = jnp.where(take, g, gidx)
    ing, raw = [], []
    for kk in range(EXPERTS_PER_GROUP):
        vs, vr = sel[kk], s[kk]
        for g in range(1, N_GROUPS):
            pick = gidx == g
            vs = jnp.where(pick, sel[4 * g + kk], vs)
            vr = jnp.where(pick, s[4 * g + kk], vr)
        ing.append(vs)
        raw.append(vr)
    b1 = ing[0]
    i1 = jnp.zeros_like(gidx)
    for kk in range(1, EXPERTS_PER_GROUP):
        take = ing[kk] > b1
        b1 = jnp.where(take, ing[kk], b1)
        i1 = jnp.where(take, kk, i1)
    neg = jnp.full_like(b1, -jnp.inf)
    b2 = neg
    i2 = jnp.zeros_like(gidx)
    for kk in range(EXPERTS_PER_GROUP):
        cand = jnp.where(i1 == kk, neg, ing[kk])
        take = cand > b2
        b2 = jnp.where(take, cand, b2)
        i2 = jnp.where(take, kk, i2)
    w1 = raw[0]
    w2 = raw[0]
    for kk in range(1, EXPERTS_PER_GROUP):
        w1 = jnp.where(i1 == kk, raw[kk], w1)
        w2 = jnp.where(i2 == kk, raw[kk], w2)
    tot = w1 + w2
    w1 = w1 / tot
    w2 = w2 / tot
    e1 = gidx * EXPERTS_PER_GROUP + i1
    e2 = gidx * EXPERTS_PER_GROUP + i2
    rows = [jnp.where(e1 == e, w1, 0.0) + jnp.where(e2 == e, w2, 0.0) for e in range(N_EXPERTS)]
    return jnp.concatenate(rows, axis=0)


def _moe_kernel(h_ref, wr_ref, rb_ref, wg_ref, wu_ref, wd_ref, g_ref, b_ref, o_ref, ob_ref,
                comb_ref, xb_ref, *, tm):
    eg = pl.program_id(1)

    @pl.when(eg == 0)
    def _():
        logits_t = _mm_nt_f32(wr_ref[...], h_ref[...])
        comb_t = _route(_sigmoid(logits_t), rb_ref)
        pad = jnp.zeros((LANE - N_EXPERTS, tm), F32)
        comb_ref[...] = jnp.transpose(jnp.concatenate([comb_t, pad], axis=0))
        xb_ref[...] = h_ref[...].astype(BF16)

    x = xb_ref[...]
    comb = comb_ref[...]
    y = None
    for kk in range(MOE_EPS):
        hg = jnp.dot(x, wg_ref[kk], preferred_element_type=F32)
        hu = jnp.dot(x, wu_ref[kk], preferred_element_type=F32)
        cw = _lane_col(comb, eg * MOE_EPS + kk)
        hid = _silu(hg) * hu * cw
        term = jnp.dot(hid.astype(BF16), wd_ref[kk], preferred_element_type=F32)
        y = term if y is None else y + term

    @pl.when(eg == 0)
    def _():
        o_ref[...] = y

    @pl.when(eg > 0)
    def _():
        o_ref[...] += y

    @pl.when(eg == N_EXPERTS // MOE_EPS - 1)
    def _():
        out = _layer_norm(ALPHA * h_ref[...] + o_ref[...], g_ref[...], b_ref[...])
        o_ref[...] = out
        ob_ref[...] = out.astype(BF16)


def _moe_call(h, wr_t, rbias, wg, wu, wd, g, b, layer, tm=1024):
    t, d = h.shape
    tm = min(tm, t)
    kern = functools.partial(_moe_kernel, tm=tm)
    vec = pl.BlockSpec((None, 1, d), lambda i, e: (layer, 0, 0))
    row = pl.BlockSpec((tm, d), lambda i, e: (i, 0))
    return pl.pallas_call(
        kern,
        grid=(t // tm, N_EXPERTS // MOE_EPS),
        in_specs=[row,
                  pl.BlockSpec((N_EXPERTS, d), lambda i, e: (0, 0)),
                  pl.BlockSpec((N_EXPERTS, LANE), lambda i, e: (0, 0)),
                  pl.BlockSpec((None, MOE_EPS, d, D_FF), lambda i, e: (layer, e, 0, 0)),
                  pl.BlockSpec((None, MOE_EPS, d, D_FF), lambda i, e: (layer, e, 0, 0)),
                  pl.BlockSpec((None, MOE_EPS, D_FF, d), lambda i, e: (layer, e, 0, 0)),
                  vec, vec],
        out_specs=[row, row],
        out_shape=[jax.ShapeDtypeStruct((t, d), F32), jax.ShapeDtypeStruct((t, d), BF16)],
        scratch_shapes=[pltpu.VMEM((tm, LANE), F32), pltpu.VMEM((tm, d), BF16)],
        compiler_params=_cparams(("parallel", "arbitrary")),
        name="moe",
    )(h, wr_t, rbias, wg, wu, wd, g, b)


def _pack_w_in(w_in):
    (a_q, a_k, a_v, a_beta, a_dt, a_g, b_q, b_k, b_v, b_lr, b_g,
     c_q, c_f, c_i, c_g, m_a, m_b, m_c) = jnp.split(w_in, SPLIT_POINTS, axis=-1)
    lead = w_in.shape[:-1]
    a_s = jnp.concatenate([a_beta, a_dt, jnp.zeros(lead + (LANE - 2 * GDN_HEADS,), w_in.dtype)], -1)
    b_lrp = jnp.concatenate([b_lr, jnp.zeros(lead + (LANE - GLA_RANK,), w_in.dtype)], -1)
    wb = jnp.concatenate([m_a, m_b, m_c, a_q, a_k, a_v, a_g, b_q, b_k, b_v, b_g, c_q, c_i, c_g], -1).astype(BF16)
    wf = jnp.concatenate([a_s, b_lrp, c_f], -1).astype(BF16)
    assert wb.shape[-1] == NPB and wf.shape[-1] == NPF
    return wb, wf


def _prepare(w_in, gdn_conv, gdn_a_log, gdn_dt_bias, gdn_norm, gla_w2, gla_b2, gla_norm, hgrn_lb_logits,
             hgrn_norm, w_br_a, w_br_b, w_br_c, w_out, ln1_g, ln1_b, w_router, router_bias, w_gate, w_up,
             w_down, ln2_g, ln2_b):
    depth = w_in.shape[0]
    d = w_out.shape[-1]
    w_pb, w_pf = _pack_w_in(w_in)
    return dict(
        w_pb=w_pb, w_pf=w_pf,
        gdn_conv=gdn_conv,
        gdn_par=jnp.pad(jnp.stack([gdn_a_log, gdn_dt_bias], axis=1),
                        ((0, 0), (0, SUBLANE - 2), (GDN_HEADS, LANE - 2 * GDN_HEADS))),
        gdn_norm=gdn_norm.reshape(depth, 1, GDN_DV),
        w2p=jnp.concatenate([gla_w2, jnp.zeros((depth, LANE - GLA_RANK, GLA_QK), gla_w2.dtype)], axis=1),
        gla_b2=gla_b2.reshape(depth, 1, GLA_QK),
        gla_norm=gla_norm.reshape(depth, 1, GLA_DV),
        lb_logits=hgrn_lb_logits,
        hgrn_norm=hgrn_norm.reshape(depth, 1, HGRN_DV),
        wa=w_br_a.astype(BF16), wb=w_br_b.astype(BF16), wc=w_br_c.astype(BF16), wo=w_out.astype(BF16),
        ln1_g=ln1_g.reshape(depth, 1, d), ln1_b=ln1_b.reshape(depth, 1, d),
        wr_t=jnp.transpose(w_router),
        rbias=jnp.broadcast_to(router_bias[:, None], (N_EXPERTS, LANE)),
        wg=w_gate.astype(BF16), wu=w_up.astype(BF16), wd=w_down.astype(BF16),
        ln2_g=ln2_g.reshape(depth, 1, d), ln2_b=ln2_b.reshape(depth, 1, d),
    )


def _mixer_block(h, hb, p, layer, batch, seq):
    pb = _inproj_call(hb, p["w_pb"], layer, BF16, 512, 2048, "inproj_b")
    pf = _inproj_call(hb, p["w_pf"], layer, F32, 1024, NPF, "inproj_f")
    o_a = _gdn_call(pb, pf, p["gdn_conv"], p["gdn_par"], p["gdn_norm"], layer, batch, seq)
    o_b = _gla_call(pb, pf, p["w2p"], p["gla_b2"], p["gla_norm"], layer, batch, seq)
    o_c = _hgrn_call(pb, pf, p["lb_logits"], p["hgrn_norm"], layer, batch, seq)
    return _merge_call(o_a, o_b, o_c, pb, h, p["wa"], p["wb"], p["wc"], p["wo"], p["ln1_g"], p["ln1_b"], layer)


def _ffn_block(h, p, layer):
    return _moe_call(h, p["wr_t"], p["rbias"], p["wg"], p["wu"], p["wd"], p["ln2_g"], p["ln2_b"], layer)


def kernel(x, ln0_g, ln0_b, w_in, gdn_conv, gdn_a_log, gdn_dt_bias, gdn_norm, gla_w2, gla_b2, gla_norm,
           hgrn_lb_logits, hgrn_norm, w_br_a, w_br_b, w_br_c, w_out, ln1_g, ln1_b, w_router, router_bias,
           w_gate, w_up, w_down, ln2_g, ln2_b):
    batch, seq, d = x.shape
    p = _prepare(w_in, gdn_conv, gdn_a_log, gdn_dt_bias, gdn_norm, gla_w2, gla_b2, gla_norm, hgrn_lb_logits,
                 hgrn_norm, w_br_a, w_br_b, w_br_c, w_out, ln1_g, ln1_b, w_router, router_bias, w_gate, w_up,
                 w_down, ln2_g, ln2_b)
    h, hb = _ln_call(x.reshape(batch * seq, d), ln0_g, ln0_b)
    for layer in range(w_in.shape[0]):
        h = _mixer_block(h, hb, p, layer, batch, seq)
        h, hb = _ffn_block(h, p, layer)
    return h.reshape(batch, seq, d)
```

```python
import functools

import numpy as np
import jax
import jax.numpy as jnp
from jax import lax
from jax.experimental import pallas as pl
from jax.experimental.pallas import tpu as pltpu

F32 = jnp.float32
BF16 = jnp.bfloat16
HIGHEST = lax.Precision.HIGHEST

D_MODEL = 1024
DEPTH = 4
CHUNK = 64
GDN_HEADS, GDN_DK, GDN_DV, CONV_W = 4, 128, 128, 4
GLA_HEADS, GLA_DK, GLA_DV, GLA_RANK, GLA_NORMALIZER = 4, 64, 128, 16, 16.0
HGRN_HEADS, HGRN_EXPAND, HGRN_DV = 4, 128, 128
LB_FLOOR = 1e-30
N_EXPERTS, N_GROUPS, TOP_K, D_FF = 16, 4, 2, 256
EXPERTS_PER_GROUP = N_EXPERTS // N_GROUPS
ALPHA = (2.0 * DEPTH) ** 0.25
LN_EPS = 1e-5
RMS_EPS = 1e-6

GDN_QK = GDN_HEADS * GDN_DK
GDN_V = GDN_HEADS * GDN_DV
GLA_QK = GLA_HEADS * GLA_DK
GLA_V = GLA_HEADS * GLA_DV
HGRN_QK = HGRN_HEADS * HGRN_EXPAND
HGRN_V = HGRN_HEADS * HGRN_DV
SPLIT_SIZES = (GDN_QK, GDN_QK, GDN_V, GDN_HEADS, GDN_HEADS, GDN_V,
               GLA_QK, GLA_QK, GLA_V, GLA_RANK, GLA_V,
               HGRN_QK, HGRN_QK, HGRN_V, HGRN_V,
               D_MODEL, D_MODEL, D_MODEL)
SPLIT_POINTS = tuple(int(v) for v in np.cumsum(SPLIT_SIZES)[:-1])

LANE = 128
SUBLANE = 8
VMEM_LIMIT = 48 * 1024 * 1024

M_A, M_B, M_C = 0, 1024, 2048
A_Q, A_K, A_V, A_G = 3072, 3584, 4096, 4608
B_Q, B_K, B_V, B_G = 5120, 5376, 5632, 6144
C_Q, C_I, C_G = 6656, 7168, 7680
NPB = 8192
F_AS, F_LR, F_CF = 0, 128, 256
NPF = 768

SUB = 8
DD_NPAR = 4
GDN_NPAR = 4
MOE_EPS = 4


def _cparams(sem):
    return pltpu.CompilerParams(dimension_semantics=sem, vmem_limit_bytes=VMEM_LIMIT)


def _mm(a, b):
    return jnp.dot(a.astype(BF16), b.astype(BF16), preferred_element_type=F32)


def _mm_nt(a, b):
    return lax.dot_general(a.astype(BF16), b.astype(BF16), (((1,), (1,)), ((), ())),
                           preferred_element_type=F32)


def _mm_tn(a, b):
    return lax.dot_general(a.astype(BF16), b.astype(BF16), (((0,), (0,)), ((), ())),
                           preferred_element_type=F32)


def _mm_nt_f32(a, b):
    return lax.dot_general(a, b, (((1,), (1,)), ((), ())), preferred_element_type=F32,
                           precision=HIGHEST)


def _split3(x):
    hi = x.astype(BF16)
    r = x - hi.astype(F32)
    mid = r.astype(BF16)
    lo = (r - mid.astype(F32)).astype(BF16)
    return hi, mid, lo


def _mm_01(m01, x):
    hi, mid, lo = _split3(x)
    return (jnp.dot(m01, hi, preferred_element_type=F32) + jnp.dot(m01, mid, preferred_element_type=F32)
            + jnp.dot(m01, lo, preferred_element_type=F32))


def _sigmoid(x):
    return 1.0 / (1.0 + jnp.exp(-x))


def _sigmoid_t(x):
    return 0.5 * jnp.tanh(0.5 * x) + 0.5


def _silu(x):
    return x * _sigmoid_t(x)


def _softplus(x):
    return jnp.maximum(x, 0.0) + jnp.log(1.0 + jnp.exp(-jnp.abs(x)))


def _log_sigmoid(x):
    return -_softplus(-x)


def _layer_norm(x, g, b):
    mu = jnp.mean(x, axis=-1, keepdims=True)
    xc = x - mu
    var = jnp.mean(xc * xc, axis=-1, keepdims=True)
    return xc * lax.rsqrt(var + LN_EPS) * g + b


def _ln_kernel(x_ref, g_ref, b_ref, o_ref, ob_ref):
    y = _layer_norm(x_ref[...], g_ref[...], b_ref[...])
    o_ref[...] = y
    ob_ref[...] = y.astype(BF16)


def _ln_call(x, g, b, tm=512):
    t, d = x.shape
    tm = min(tm, t)
    return pl.pallas_call(
        _ln_kernel,
        grid=(t // tm,),
        in_specs=[pl.BlockSpec((tm, d), lambda i: (i, 0)),
                  pl.BlockSpec((1, d), lambda i: (0, 0)),
                  pl.BlockSpec((1, d), lambda i: (0, 0))],
        out_specs=[pl.BlockSpec((tm, d), lambda i: (i, 0)), pl.BlockSpec((tm, d), lambda i: (i, 0))],
        out_shape=[jax.ShapeDtypeStruct((t, d), F32), jax.ShapeDtypeStruct((t, d), BF16)],
        compiler_params=_cparams(("parallel",)),
        name="ln0",
    )(x, g.reshape(1, d), b.reshape(1, d))


def _inproj_kernel(x_ref, w_ref, o_ref):
    o_ref[...] = jnp.dot(x_ref[...], w_ref[...], preferred_element_type=F32).astype(o_ref.dtype)


def _inproj_call(hb, w, layer, out_dtype, tm, tn, name):
    t, d = hb.shape
    tm = min(tm, t)
    n = w.shape[-1]
    return pl.pallas_call(
        _inproj_kernel,
        grid=(n // tn, t // tm),
        in_specs=[pl.BlockSpec((tm, d), lambda j, i: (i, 0)),
                  pl.BlockSpec((None, d, tn), lambda j, i: (layer, 0, j))],
        out_specs=pl.BlockSpec((tm, tn), lambda j, i: (i, j)),
        out_shape=jax.ShapeDtypeStruct((t, n), out_dtype),
        compiler_params=_cparams(("parallel", "parallel")),
        name=name,
    )(hb, w)


def _gdn_kernel(qkv_ref, s_ref, gate_ref, cw_ref, par_ref, nw_ref, o_ref,
                state_ref, tail_ref, xbuf_ref, q_s, k_s, v_s, cumb_s, betab_s, cumrow_s, o_s, *, tb):
    blk = pl.program_id(1)
    nc = tb // CHUNK
    nh = GDN_HEADS
    off0 = SUBLANE - (CONV_W - 1)

    @pl.when(blk == 0)
    def _():
        state_ref[...] = jnp.zeros_like(state_ref)
        tail_ref[...] = jnp.zeros_like(tail_ref)

    row = lax.broadcasted_iota(jnp.int32, (CHUNK, CHUNK), 0)
    col = lax.broadcasted_iota(jnp.int32, (CHUNK, CHUNK), 1)
    incl = col <= row
    strict = col < row
    tri = jnp.where(incl, 1.0, 0.0).astype(BF16)
    eye = jnp.where(col == row, 1.0, 0.0).astype(F32)
    npar = GDN_NPAR
    rp = npar * CHUNK
    xbuf_ref[0:SUBLANE, :] = tail_ref[...]

    def prologue(t):
        lo = t * rp
        xbuf_ref[SUBLANE + lo:SUBLANE + lo + rp, :] = qkv_ref[lo:lo + rp, :].astype(F32)
        for j in range(3 * nh):
            cs = slice(j * LANE, (j + 1) * LANE)
            y = xbuf_ref[off0 + lo:off0 + lo + rp, cs] * cw_ref[0:1, cs]
            for kk in range(1, CONV_W):
                y = y + xbuf_ref[off0 + kk + lo:off0 + kk + lo + rp, cs] * cw_ref[kk:kk + 1, cs]
            y = _silu(y)
            if j < nh:
                q_s[j, lo:lo + rp, :] = (y * lax.rsqrt(jnp.sum(y * y, axis=-1, keepdims=True) + RMS_EPS)
                                         * (GDN_DK ** -0.5))
            elif j < 2 * nh:
                k_s[j - nh, lo:lo + rp, :] = y * lax.rsqrt(jnp.sum(y * y, axis=-1, keepdims=True) + RMS_EPS)
            else:
                v_s[j - 2 * nh, lo:lo + rp, :] = y
        sc = s_ref[lo:lo + rp, :]
        beta_all = _sigmoid_t(sc)
        g_all = -jnp.exp(par_ref[0:1, :]) * _softplus(sc + par_ref[1:2, :])
        cum_all = jnp.concatenate([_mm_01(tri, g_all[c * CHUNK:(c + 1) * CHUNK, :]) for c in range(npar)], axis=0)
        cum_t = jnp.transpose(cum_all)
        for c in range(npar):
            cumrow_s[t * npar + c] = cum_t[0:SUBLANE, c * CHUNK:(c + 1) * CHUNK]
        for h in range(nh):
            cumb_s[h, lo:lo + rp, :] = jnp.broadcast_to(cum_all[:, nh + h:nh + h + 1], (rp, LANE))
            betab_s[h, lo:lo + rp, :] = jnp.broadcast_to(beta_all[:, h:h + 1], (rp, LANE))

    def trip(cp):
        chains = [(cp * npar + cc, h) for cc in range(npar) for h in range(nh)]
        r0s = [c * CHUNK for c, _ in chains]
        qc = [q_s[h, pl.ds(r0, CHUNK), :] for (_, h), r0 in zip(chains, r0s)]
        kc = [k_s[h, pl.ds(r0, CHUNK), :] for (_, h), r0 in zip(chains, r0s)]
        vc = [v_s[h, pl.ds(r0, CHUNK), :] for (_, h), r0 in zip(chains, r0s)]
        cumc = [cumb_s[h, pl.ds(r0, CHUNK), :] for (_, h), r0 in zip(chains, r0s)]
        bc = [betab_s[h, pl.ds(r0, CHUNK), :] for (_, h), r0 in zip(chains, r0s)]
        n = len(chains)
        kk = [_mm_nt(kc[i], kc[i]) for i in range(n)]
        qk = [_mm_nt(qc[i], kc[i]) for i in range(n)]
        decay = []
        for i, (c, h) in enumerate(chains):
            diff = cumc[i][:, 0:CHUNK] - cumrow_s[c][nh + h:nh + h + 1, :]
            decay.append(jnp.where(incl, jnp.exp(jnp.where(incl, diff, 0.0)), 0.0))
        a = [jnp.where(strict, bc[i][:, 0:CHUNK] * kk[i] * decay[i], 0.0) for i in range(n)]
        x = [eye - a[i] for i in range(n)]
        p = [_mm(a[i], a[i]) for i in range(n)]
        for it in range(5):
            x = [x[i] + _mm(x[i], p[i]) for i in range(n)]
            if it < 4:
                p = [_mm(p[i], p[i]) for i in range(n)]
        ecum = [jnp.exp(cumc[i]) for i in range(n)]
        sol = [_mm(x[i], jnp.concatenate([vc[i] * bc[i], kc[i] * (bc[i] * ecum[i])], axis=1)) for i in range(n)]
        attn = [qk[i] * decay[i] for i in range(n)]
        cum_last = [cumc[i][CHUNK - 1:CHUNK, :] for i in range(n)]
        k_state = [kc[i] * jnp.exp(cum_last[i] - cumc[i]) for i in range(n)]
        wqi = [jnp.concatenate([sol[i][:, GDN_DV:GDN_DV + GDN_DK], qc[i] * ecum[i]], axis=0) for i in range(n)]
        for cc in range(npar):
            idx = [cc * nh + h for h in range(nh)]
            s = [state_ref[h] for h in range(nh)]
            wq = [_mm(wqi[i], s[h]) for h, i in enumerate(idx)]
            v_new = [sol[i][:, 0:GDN_DV] - wq[h][0:CHUNK, :] for h, i in enumerate(idx)]
            av = [_mm(attn[i], v_new[h]) for h, i in enumerate(idx)]
            upd = [_mm_tn(k_state[i], v_new[h]) for h, i in enumerate(idx)]
            for h, i in enumerate(idx):
                state_ref[h] = s[h] * jnp.exp(cum_last[i]) + upd[h]
                o_s[pl.ds(r0s[i], CHUNK), h * LANE:(h + 1) * LANE] = wq[h][CHUNK:2 * CHUNK, :] + av[h]

    def epilogue(t):
        lo = t * rp
        for h in range(nh):
            hs = slice(h * LANE, (h + 1) * LANE)
            o = o_s[lo:lo + rp, hs]
            oh = o * lax.rsqrt(jnp.mean(o * o, axis=-1, keepdims=True) + RMS_EPS) * nw_ref[...]
            o_ref[lo:lo + rp, hs] = (oh * _silu(gate_ref[lo:lo + rp, hs].astype(F32))).astype(o_ref.dtype)

    nparts = nc // npar
    prologue(0)
    for t in range(nparts):
        if t + 1 < nparts:
            prologue(t + 1)
        trip(t)
        epilogue(t)
    tail_ref[...] = xbuf_ref[tb:tb + SUBLANE, :]


def _gdn_call(pb, pf, conv_w, par, norm_w, layer, batch, seq, tb=512):
    tb = min(tb, seq)
    nb = seq // tb
    nc = tb // CHUNK
    t = batch * seq
    wq = 2 * GDN_QK + GDN_V
    kern = functools.partial(_gdn_kernel, tb=tb)
    return pl.pallas_call(
        kern,
        grid=(batch, nb),
        in_specs=[pl.BlockSpec((tb, wq), lambda b, i: (b * nb + i, A_Q // wq)),
                  pl.BlockSpec((tb, LANE), lambda b, i: (b * nb + i, F_AS // LANE)),
                  pl.BlockSpec((tb, GDN_V), lambda b, i: (b * nb + i, A_G // GDN_V)),
                  pl.BlockSpec((None, CONV_W, wq), lambda b, i: (layer, 0, 0)),
                  pl.BlockSpec((None, SUBLANE, LANE), lambda b, i: (layer, 0, 0)),
                  pl.BlockSpec((None, 1, GDN_DV), lambda b, i: (layer, 0, 0))],
        out_specs=pl.BlockSpec((tb, GDN_V), lambda b, i: (b * nb + i, 0)),
        out_shape=jax.ShapeDtypeStruct((t, GDN_V), BF16),
        scratch_shapes=[pltpu.VMEM((GDN_HEADS, GDN_DK, GDN_DV), F32),
                        pltpu.VMEM((SUBLANE, wq), F32),
                        pltpu.VMEM((tb + SUBLANE, wq), F32),
                        pltpu.VMEM((GDN_HEADS, tb, LANE), F32),
                        pltpu.VMEM((GDN_HEADS, tb, LANE), F32),
                        pltpu.VMEM((GDN_HEADS, tb, LANE), F32),
                        pltpu.VMEM((GDN_HEADS, tb, LANE), F32),
                        pltpu.VMEM((GDN_HEADS, tb, LANE), F32),
                        pltpu.VMEM((nc, SUBLANE, CHUNK), F32),
                        pltpu.VMEM((tb, GDN_V), F32)],
        compiler_params=_cparams(("parallel", "arbitrary")),
        name="gdn",
    )(pb, pf, pb, conv_w, par, norm_w)


def _dd_core(q_s, k_s, v_s, la_s, o_s, state_ref, c8_s, p_s, *, tb, g_heads):
    nchunk = tb // CHUNK
    nblk = CHUNK // SUB
    dkh = LANE // g_heads
    dvp = g_heads * LANE
    row = lax.broadcasted_iota(jnp.int32, (CHUNK, CHUNK), 0)
    col = lax.broadcasted_iota(jnp.int32, (CHUNK, CHUNK), 1)
    level_masks = []
    for sh in (5, 4, 3):
        same2b = jnp.right_shift(row, sh + 1) == jnp.right_shift(col, sh + 1)
        upper = (jnp.right_shift(row, sh) & 1) == 1
        lower = (jnp.right_shift(col, sh) & 1) == 0
        level_masks.append(jnp.where(same2b, jnp.where(upper, jnp.where(lower, 1.0, 0.0), 0.0), 0.0))
    tri8 = jnp.where(jnp.right_shift(row, 3) == jnp.right_shift(col, 3),
                     jnp.where(col <= row, 1.0, 0.0), 0.0).astype(BF16)
    lane128 = lax.broadcasted_iota(jnp.int32, (CHUNK, LANE), 1)
    head_masks = [jnp.where((lane128 >= g * dkh) & (lane128 < (g + 1) * dkh), 1.0, 0.0)
                  for g in range(g_heads)]
    sub = lax.broadcasted_iota(jnp.int32, (SUB, LANE), 0)
    dk_sh = dkh.bit_length() - 1
    lane_sh = LANE.bit_length() - 1
    orow = lax.broadcasted_iota(jnp.int32, (LANE, dvp), 0)
    ocol = lax.broadcasted_iota(jnp.int32, (LANE, dvp), 1)
    ones_bd = jnp.where(jnp.right_shift(orow, dk_sh) == jnp.right_shift(ocol, lane_sh), 1.0, 0.0).astype(BF16)
    srow = lax.broadcasted_iota(jnp.int32, (dvp, LANE), 0)
    scol = lax.broadcasted_iota(jnp.int32, (dvp, LANE), 1)
    state_mask = jnp.where(jnp.right_shift(srow, lane_sh) == jnp.right_shift(scol, dk_sh), 1.0, 0.0)
    npar = DD_NPAR

    def block_sums(i):
        c8 = [c8_s[i, b * SUB:(b + 1) * SUB, :] for b in range(nblk)]
        t8 = [c8_s[i, (b + 1) * SUB - 1:(b + 1) * SUB, :] for b in range(nblk)]
        t16 = [t8[2 * b] + t8[2 * b + 1] for b in range(nblk // 2)]
        t32 = [t16[2 * b] + t16[2 * b + 1] for b in range(nblk // 4)]
        t64 = t32[0] + t32[1]
        c16 = [c8[b] + t8[b - 1] if b % 2 else c8[b] for b in range(nblk)]
        c32 = [c16[b] + t16[b // 2 - 1] if (b // 2) % 2 else c16[b] for b in range(nblk)]
        c64 = [c32[b] + t32[0] if b >= nblk // 2 else c32[b] for b in range(nblk)]
        pre = {8: c8, 16: c16, 32: c32, 64: c64}
        suf = {8: [t8[b] - c8[b] for b in range(nblk)],
               16: [t16[b // 2] - c16[b] for b in range(nblk)],
               32: [t32[b // 4] - c32[b] for b in range(nblk)],
               64: [t64 - c64[b] for b in range(nblk)]}
        return pre, suf

    def cat(pieces):
        return jnp.concatenate(pieces, axis=0)

    def group_body(cg, carry):
        rng = range(npar)
        r0s = [pl.multiple_of((cg * npar + i) * CHUNK, CHUNK) for i in rng]
        qc = [q_s[pl.ds(r0, CHUNK), :] for r0 in r0s]
        kc = [k_s[pl.ds(r0, CHUNK), :] for r0 in r0s]
        vc = [v_s[pl.ds(r0, CHUNK), :] for r0 in r0s]
        c8_all = _mm_01(tri8, jnp.concatenate([la_s[pl.ds(r0, CHUNK), :] for r0 in r0s], axis=1))
        for i in rng:
            c8_s[i] = c8_all[:, i * LANE:(i + 1) * LANE]
        sums = [block_sums(i) for i in rng]
        attn = [[None] * g_heads for _ in rng]
        for li, b in enumerate((32, 16, 8)):
            qs = [qc[i] * jnp.exp(cat(sums[i][0][b])) for i in rng]
            ks = [kc[i] * jnp.exp(cat(sums[i][1][b])) for i in rng]
            for g in range(g_heads):
                for i in rng:
                    qg = qs[i] * head_masks[g] if g_heads > 1 else qs[i]
                    term = _mm_nt(qg, ks[i]) * level_masks[li]
                    attn[i][g] = term if li == 0 else attn[i][g] + term
        o = [jnp.concatenate([_mm(attn[i][g], vc[i][:, g * LANE:(g + 1) * LANE]) for g in range(g_heads)], axis=1)
             if g_heads > 1 else _mm(attn[i][0], vc[i]) for i in rng]
        for i in rng:
            c8 = sums[i][0][8]
            for r in range(nblk):
                qr = qc[i][r * SUB:(r + 1) * SUB, :]
                for jj in range(SUB):
                    krow = k_s[pl.ds(r0s[i] + r * SUB + jj, 1), :]
                    crow = c8_s[i, r * SUB + jj:r * SUB + jj + 1, :]
                    m = sub >= jj
                    pr = qr * krow * jnp.exp(jnp.where(m, c8[r] - crow, -jnp.inf))
                    p_s[i, (r * SUB + jj) * SUB:(r * SUB + jj + 1) * SUB, :] = pr
        rs = [jnp.dot(p_s[i].astype(BF16), ones_bd, preferred_element_type=F32) for i in rng]
        for i in rng:
            od = []
            for r in range(nblk):
                acc = None
                for jj in range(SUB):
                    vrow = v_s[pl.ds(r0s[i] + r * SUB + jj, 1), :]
                    term = rs[i][(r * SUB + jj) * SUB:(r * SUB + jj + 1) * SUB, :] * vrow
                    acc = term if acc is None else acc + term
                od.append(acc)
            o[i] = o[i] + cat(od)
        q_inter = [qc[i] * jnp.exp(cat(sums[i][0][64])) for i in rng]
        k_state = [kc[i] * jnp.exp(cat(sums[i][1][64])) for i in rng]
        upd = [_mm_tn(vc[i], k_state[i]) for i in rng]
        st = state_ref[...]
        for i in rng:
            o_s[pl.ds(r0s[i], CHUNK), :] = o[i] + _mm_nt(q_inter[i], st)
            decay_last = jnp.exp(sums[i][0][64][nblk - 1][SUB - 1:SUB, :])
            st = st * decay_last + (upd[i] * state_mask if g_heads > 1 else upd[i])
        state_ref[...] = st
        return carry

    lax.fori_loop(0, nchunk // npar, group_body, 0)


def _gated_rms_out(o_s, gate_ref, nw_ref, o_ref, g_heads):
    for g in range(g_heads):
        hs = slice(g * LANE, (g + 1) * LANE)
        o = o_s[:, hs]
        oh = o * lax.rsqrt(jnp.mean(o * o, axis=-1, keepdims=True) + RMS_EPS) * nw_ref[...]
        o_ref[:, hs] = (oh * _silu(gate_ref[:, hs].astype(F32))).astype(o_ref.dtype)


def _gla_kernel(q_ref, k_ref, v_ref, lr_ref, gate_ref, w2_ref, b2_ref, nw_ref,
                o_ref, state_ref, q_s, k_s, v_s, la_s, o_s, c8_s, p_s, *, tb):
    @pl.when(pl.program_id(2) == 0)
    def _():
        state_ref[...] = jnp.zeros_like(state_ref)

    q_s[...] = q_ref[...].astype(F32) * (GLA_DK ** -0.5)
    k_s[...] = k_ref[...].astype(F32)
    v_s[...] = v_ref[...].astype(F32)
    z = _mm(lr_ref[...], w2_ref[...]) + b2_ref[...]
    la_s[...] = _log_sigmoid(z) * (1.0 / GLA_NORMALIZER)
    _dd_core(q_s, k_s, v_s, la_s, o_s, state_ref, c8_s, p_s, tb=tb, g_heads=2)
    _gated_rms_out(o_s, gate_ref, nw_ref, o_ref, 2)


def _dd_scratch(tb, dvp):
    return [pltpu.VMEM((dvp, LANE), F32),
            pltpu.VMEM((tb, LANE), F32), pltpu.VMEM((tb, LANE), F32),
            pltpu.VMEM((tb, dvp), F32),
            pltpu.VMEM((tb, LANE), F32),
            pltpu.VMEM((tb, dvp), F32),
            pltpu.VMEM((DD_NPAR, CHUNK, LANE), F32),
            pltpu.VMEM((DD_NPAR, CHUNK * SUB, LANE), F32)]


def _gla_call(pb, pf, w2p, b2, norm_w, layer, batch, seq, tb=512):
    tb = min(tb, seq)
    nb = seq // tb
    t = batch * seq
    npair = GLA_HEADS // 2
    kern = functools.partial(_gla_kernel, tb=tb)
    return pl.pallas_call(
        kern,
        grid=(batch, npair, nb),
        in_specs=[pl.BlockSpec((tb, LANE), lambda b, p, i: (b * nb + i, B_Q // LANE + p)),
                  pl.BlockSpec((tb, LANE), lambda b, p, i: (b * nb + i, B_K // LANE + p)),
                  pl.BlockSpec((tb, 2 * LANE), lambda b, p, i: (b * nb + i, B_V // (2 * LANE) + p)),
                  pl.BlockSpec((tb, LANE), lambda b, p, i: (b * nb + i, F_LR // LANE)),
                  pl.BlockSpec((tb, 2 * LANE), lambda b, p, i: (b * nb + i, B_G // (2 * LANE) + p)),
                  pl.BlockSpec((None, LANE, LANE), lambda b, p, i: (layer, 0, p)),
                  pl.BlockSpec((None, 1, LANE), lambda b, p, i: (layer, 0, p)),
                  pl.BlockSpec((None, 1, GLA_DV), lambda b, p, i: (layer, 0, 0))],
        out_specs=pl.BlockSpec((tb, 2 * LANE), lambda b, p, i: (b * nb + i, p)),
        out_shape=jax.ShapeDtypeStruct((t, GLA_V), BF16),
        scratch_shapes=_dd_scratch(tb, 2 * LANE),
        compiler_params=_cparams(("parallel", "parallel", "arbitrary")),
        name="gla",
    )(pb, pb, pb, pf, pb, w2p, b2, norm_w)


def _hgrn_kernel(q_ref, f_ref, v_ref, gate_ref, lbl_ref, nw_ref,
                 o_ref, state_ref, q_s, k_s, v_s, la_s, o_s, c8_s, p_s, *, tb, layer):
    @pl.when(pl.program_id(2) == 0)
    def _():
        state_ref[...] = jnp.zeros_like(state_ref)

    logits = lbl_ref[...]
    mx = jnp.max(logits, axis=0, keepdims=True)
    ex = jnp.exp(logits - mx)
    p = ex / jnp.sum(ex, axis=0, keepdims=True)
    acc = p[0:1, :]
    for r in range(1, layer + 1):
        acc = acc + p[r:r + 1, :]
    lb = jnp.clip(acc - p[0:1, :], 0.0, 1.0)
    log_lb = jnp.log(jnp.maximum(lb, LB_FLOOR))
    log_1m = jnp.log(1.0 - lb)

    cf = f_ref[...]
    second = log_1m + _log_sigmoid(cf)
    la_s[...] = jnp.maximum(log_lb, second) + jnp.log(1.0 + jnp.exp(-jnp.abs(log_lb - second)))
    k_s[...] = (1.0 - lb) * _sigmoid_t(-cf)
    q_s[...] = _silu(q_ref[...].astype(F32)) * (HGRN_EXPAND ** -0.5)
    v_s[...] = v_ref[...].astype(F32)
    _dd_core(q_s, k_s, v_s, la_s, o_s, state_ref, c8_s, p_s, tb=tb, g_heads=1)
    _gated_rms_out(o_s, gate_ref, nw_ref, o_ref, 1)


def _hgrn_call(pb, pf, lb_logits, norm_w, layer, batch, seq, tb=512):
    tb = min(tb, seq)
    nb = seq // tb
    t = batch * seq

    def tok(colblk):
        return pl.BlockSpec((tb, LANE), lambda b, h, i: (b * nb + i, colblk + h))

    kern = functools.partial(_hgrn_kernel, tb=tb, layer=layer)
    return pl.pallas_call(
        kern,
        grid=(batch, HGRN_HEADS, nb),
        in_specs=[tok(C_Q // LANE), tok(F_CF // LANE), tok(C_I // LANE), tok(C_G // LANE),
                  pl.BlockSpec((DEPTH, LANE), lambda b, h, i: (0, h)),
                  pl.BlockSpec((None, 1, HGRN_DV), lambda b, h, i: (layer, 0, 0))],
        out_specs=pl.BlockSpec((tb, LANE), lambda b, h, i: (b * nb + i, h)),
        out_shape=jax.ShapeDtypeStruct((t, HGRN_V), BF16),
        scratch_shapes=_dd_scratch(tb, LANE),
        compiler_params=_cparams(("parallel", "parallel", "arbitrary")),
        name="hgrn",
    )(pb, pf, pb, pb, lb_logits, norm_w)


def _merge_kernel(oa_ref, ob_ref, oc_ref, ma_ref, mb_ref, mc_ref, h_ref,
                  wa_ref, wb_ref, wc_ref, wo_ref, g_ref, b_ref, o_ref):
    y = (_sigmoid_t(ma_ref[...].astype(F32)) * jnp.dot(oa_ref[...], wa_ref[...], preferred_element_type=F32)
         + _sigmoid_t(mb_ref[...].astype(F32)) * jnp.dot(ob_ref[...], wb_ref[...], preferred_element_type=F32)
         + _sigmoid_t(mc_ref[...].astype(F32)) * jnp.dot(oc_ref[...], wc_ref[...], preferred_element_type=F32))
    mix = _mm(y, wo_ref[...])
    o_ref[...] = _layer_norm(ALPHA * h_ref[...] + mix, g_ref[...], b_ref[...])


def _merge_call(o_a, o_b, o_c, pb, h, wa, wb, wc, wo, g, b, layer, tm=512):
    t, d = h.shape
    tm = min(tm, t)

    def row(width):
        return pl.BlockSpec((tm, width), lambda i: (i, 0))

    def wspec(kdim):
        return pl.BlockSpec((None, kdim, d), lambda i: (layer, 0, 0))

    vec = pl.BlockSpec((None, 1, d), lambda i: (layer, 0, 0))
    return pl.pallas_call(
        _merge_kernel,
        grid=(t // tm,),
        in_specs=[row(GDN_V), row(GLA_V), row(HGRN_V),
                  pl.BlockSpec((tm, d), lambda i: (i, M_A // d)),
                  pl.BlockSpec((tm, d), lambda i: (i, M_B // d)),
                  pl.BlockSpec((tm, d), lambda i: (i, M_C // d)),
                  row(d), wspec(GDN_V), wspec(GLA_V), wspec(HGRN_V), wspec(d), vec, vec],
        out_specs=row(d),
        out_shape=jax.ShapeDtypeStruct((t, d), F32),
        compiler_params=_cparams(("parallel",)),
        name="merge",
    )(o_a, o_b, o_c, pb, pb, pb, h, wa, wb, wc, wo, g, b)


def _lane_col(x, idx):
    lane = lax.broadcasted_iota(jnp.int32, x.shape, 1)
    return jnp.sum(jnp.where(lane == idx, x, 0.0), axis=1, keepdims=True)


def _route(scores_t, bias_ref):
    s = [scores_t[e:e + 1, :] for e in range(N_EXPERTS)]
    sel = [s[e] + bias_ref[e:e + 1, 0:1] for e in range(N_EXPERTS)]
    gscore = []
    for g in range(N_GROUPS):
        a, b, c, d = sel[4 * g:4 * g + 4]
        hi1, lo1 = jnp.maximum(a, b), jnp.minimum(a, b)
        hi2, lo2 = jnp.maximum(c, d), jnp.minimum(c, d)
        top1 = jnp.maximum(hi1, hi2)
        top2 = jnp.maximum(jnp.minimum(hi1, hi2), jnp.maximum(lo1, lo2))
        gscore.append(top1 + top2)
    best = gscore[0]
    gidx = jnp.zeros_like(best, dtype=jnp.int32)
    for g in range(1, N_GROUPS):
        take = gscore[g] > best
        best = jnp.where(take, gscore[g], best)
        gidx = jnp.where(take, g, gidx)
    ing, raw = [], []
    for kk in range(EXPERTS_PER_GROUP):
        vs, vr = sel[kk], s[kk]
        for g in range(1, N_GROUPS):
            pick = gidx == g
            vs = jnp.where(pick, sel[4 * g + kk], vs)
            vr = jnp.where(pick, s[4 * g + kk], vr)
        ing.append(vs)
        raw.append(vr)
    b1 = ing[0]
    i1 = jnp.zeros_like(gidx)
    for kk in range(1, EXPERTS_PER_GROUP):
        take = ing[kk] > b1
        b1 = jnp.where(take, ing[kk], b1)
        i1 = jnp.where(take, kk, i1)
    neg = jnp.full_like(b1, -jnp.inf)
    b2 = neg
    i2 = jnp.zeros_like(gidx)
    for kk in range(EXPERTS_PER_GROUP):
        cand = jnp.where(i1 == kk, neg, ing[kk])
        take = cand > b2
        b2 = jnp.where(take, cand, b2)
        i2 = jnp.where(take, kk, i2)
    w1 = raw[0]
    w2 = raw[0]
    for kk in range(1, EXPERTS_PER_GROUP):
        w1 = jnp.where(i1 == kk, raw[kk], w1)
        w2 = jnp.where(i2 == kk, raw[kk], w2)
    tot = w1 + w2
    w1 = w1 / tot
    w2 = w2 / tot
    e1 = gidx * EXPERTS_PER_GROUP + i1
    e2 = gidx * EXPERTS_PER_GROUP + i2
    rows = [jnp.where(e1 == e, w1, 0.0) + jnp.where(e2 == e, w2, 0.0) for e in range(N_EXPERTS)]
    return jnp.concatenate(rows, axis=0)


def _moe_kernel(h_ref, wr_ref, rb_ref, wg_ref, wu_ref, wd_ref, g_ref, b_ref, o_ref, ob_ref,
                comb_ref, xb_ref, *, tm):
    eg = pl.program_id(1)

    @pl.when(eg == 0)
    def _():
        logits_t = _mm_nt_f32(wr_ref[...], h_ref[...])
        comb_t = _route(_sigmoid(logits_t), rb_ref)
        pad = jnp.zeros((LANE - N_EXPERTS, tm), F32)
        comb_ref[...] = jnp.transpose(jnp.concatenate([comb_t, pad], axis=0))
        xb_ref[...] = h_ref[...].astype(BF16)

    x = xb_ref[...]
    comb = comb_ref[...]
    y = None
    for kk in range(MOE_EPS):
        hg = jnp.dot(x, wg_ref[kk], preferred_element_type=F32)
        hu = jnp.dot(x, wu_ref[kk], preferred_element_type=F32)
        cw = _lane_col(comb, eg * MOE_EPS + kk)
        hid = _silu(hg) * hu * cw
        term = jnp.dot(hid.astype(BF16), wd_ref[kk], preferred_element_type=F32)
        y = term if y is None else y + term

    @pl.when(eg == 0)
    def _():
        o_ref[...] = y

    @pl.when(eg > 0)
    def _():
        o_ref[...] += y

    @pl.when(eg == N_EXPERTS // MOE_EPS - 1)
    def _():
        out = _layer_norm(ALPHA * h_ref[...] + o_ref[...], g_ref[...], b_ref[...])
        o_ref[...] = out
        ob_ref[...] = out.astype(BF16)


def _moe_call(h, wr_t, rbias, wg, wu, wd, g, b, layer, tm=1024):
    t, d = h.shape
    tm = min(tm, t)
    kern = functools.partial(_moe_kernel, tm=tm)
    vec = pl.BlockSpec((None, 1, d), lambda i, e: (layer, 0, 0))
    row = pl.BlockSpec((tm, d), lambda i, e: (i, 0))
    return pl.pallas_call(
        kern,
        grid=(t // tm, N_EXPERTS // MOE_EPS),
        in_specs=[row,
                  pl.BlockSpec((N_EXPERTS, d), lambda i, e: (0, 0)),
                  pl.BlockSpec((N_EXPERTS, LANE), lambda i, e: (0, 0)),
                  pl.BlockSpec((None, MOE_EPS, d, D_FF), lambda i, e: (layer, e, 0, 0)),
                  pl.BlockSpec((None, MOE_EPS, d, D_FF), lambda i, e: (layer, e, 0, 0)),
                  pl.BlockSpec((None, MOE_EPS, D_FF, d), lambda i, e: (layer, e, 0, 0)),
                  vec, vec],
        out_specs=[row, row],
        out_shape=[jax.ShapeDtypeStruct((t, d), F32), jax.ShapeDtypeStruct((t, d), BF16)],
        scratch_shapes=[pltpu.VMEM((tm, LANE), F32), pltpu.VMEM((tm, d), BF16)],
        compiler_params=_cparams(("parallel", "arbitrary")),
        name="moe",
    )(h, wr_t, rbias, wg, wu, wd, g, b)


def _pack_w_in(w_in):
    (a_q, a_k, a_v, a_beta, a_dt, a_g, b_q, b_k, b_v, b_lr, b_g,
     c_q, c_f, c_i, c_g, m_a, m_b, m_c) = jnp.split(w_in, SPLIT_POINTS, axis=-1)
    lead = w_in.shape[:-1]
    a_s = jnp.concatenate([a_beta, a_dt, jnp.zeros(lead + (LANE - 2 * GDN_HEADS,), w_in.dtype)], -1)
    b_lrp = jnp.concatenate([b_lr, jnp.zeros(lead + (LANE - GLA_RANK,), w_in.dtype)], -1)
    wb = jnp.concatenate([m_a, m_b, m_c, a_q, a_k, a_v, a_g, b_q, b_k, b_v, b_g, c_q, c_i, c_g], -1).astype(BF16)
    wf = jnp.concatenate([a_s, b_lrp, c_f], -1).astype(BF16)
    assert wb.shape[-1] == NPB and wf.shape[-1] == NPF
    return wb, wf


def _prepare(w_in, gdn_conv, gdn_a_log, gdn_dt_bias, gdn_norm, gla_w2, gla_b2, gla_norm, hgrn_lb_logits,
             hgrn_norm, w_br_a, w_br_b, w_br_c, w_out, ln1_g, ln1_b, w_router, router_bias, w_gate, w_up,
             w_down, ln2_g, ln2_b):
    depth = w_in.shape[0]
    d = w_out.shape[-1]
    w_pb, w_pf = _pack_w_in(w_in)
    return dict(
        w_pb=w_pb, w_pf=w_pf,
        gdn_conv=gdn_conv,
        gdn_par=jnp.pad(jnp.stack([gdn_a_log, gdn_dt_bias], axis=1),
                        ((0, 0), (0, SUBLANE - 2), (GDN_HEADS, LANE - 2 * GDN_HEADS))),
        gdn_norm=gdn_norm.reshape(depth, 1, GDN_DV),
        w2p=jnp.concatenate([gla_w2, jnp.zeros((depth, LANE - GLA_RANK, GLA_QK), gla_w2.dtype)], axis=1),
        gla_b2=gla_b2.reshape(depth, 1, GLA_QK),
        gla_norm=gla_norm.reshape(depth, 1, GLA_DV),
        lb_logits=hgrn_lb_logits,
        hgrn_norm=hgrn_norm.reshape(depth, 1, HGRN_DV),
        wa=w_br_a.astype(BF16), wb=w_br_b.astype(BF16), wc=w_br_c.astype(BF16), wo=w_out.astype(BF16),
        ln1_g=ln1_g.reshape(depth, 1, d), ln1_b=ln1_b.reshape(depth, 1, d),
        wr_t=jnp.transpose(w_router),
        rbias=jnp.broadcast_to(router_bias[:, None], (N_EXPERTS, LANE)),
        wg=w_gate.astype(BF16), wu=w_up.astype(BF16), wd=w_down.astype(BF16),
        ln2_g=ln2_g.reshape(depth, 1, d), ln2_b=ln2_b.reshape(depth, 1, d),
    )


def _mixer_block(h, hb, p, layer, batch, seq):
    pb = _inproj_call(hb, p["w_pb"], layer, BF16, 512, 2048, "inproj_b")
    pf = _inproj_call(hb, p["w_pf"], layer, F32, 1024, NPF, "inproj_f")
    o_a = _gdn_call(pb, pf, p["gdn_conv"], p["gdn_par"], p["gdn_norm"], layer, batch, seq)
    o_b = _gla_call(pb, pf, p["w2p"], p["gla_b2"], p["gla_norm"], layer, batch, seq)
    o_c = _hgrn_call(pb, pf, p["lb_logits"], p["hgrn_norm"], layer, batch, seq)
    return _merge_call(o_a, o_b, o_c, pb, h, p["wa"], p["wb"], p["wc"], p["wo"], p["ln1_g"], p["ln1_b"], layer)


def _ffn_block(h, p, layer):
    return _moe_call(h, p["wr_t"], p["rbias"], p["wg"], p["wu"], p["wd"], p["ln2_g"], p["ln2_b"], layer)


def kernel(x, ln0_g, ln0_b, w_in, gdn_conv, gdn_a_log, gdn_dt_bias, gdn_norm, gla_w2, gla_b2, gla_norm,
           hgrn_lb_logits, hgrn_norm, w_br_a, w_br_b, w_br_c, w_out, ln1_g, ln1_b, w_router, router_bias,
           w_gate, w_up, w_down, ln2_g, ln2_b):
    batch, seq, d = x.shape
    p = _prepare(w_in, gdn_conv, gdn_a_log, gdn_dt_bias, gdn_norm, gla_w2, gla_b2, gla_norm, hgrn_lb_logits,
                 hgrn_norm, w_br_a, w_br_b, w_br_c, w_out, ln1_g, ln1_b, w_router, router_bias, w_gate, w_up,
                 w_down, ln2_g, ln2_b)
    h, hb = _ln_call(x.reshape(batch * seq, d), ln0_g, ln0_b)
    for layer in range(w_in.shape[0]):
        h = _mixer_block(h, hb, p, layer, batch, seq)
        h, hb = _ffn_block(h, p, layer)
    return h.reshape(batch, seq, d)
```

```python
import functools

import numpy as np
import jax
import jax.numpy as jnp
from jax import lax
from jax.experimental import pallas as pl
from jax.experimental.pallas import tpu as pltpu

F32 = jnp.float32
BF16 = jnp.bfloat16
HIGHEST = lax.Precision.HIGHEST

D_MODEL = 1024
DEPTH = 4
CHUNK = 64
GDN_HEADS, GDN_DK, GDN_DV, CONV_W = 4, 128, 128, 4
GLA_HEADS, GLA_DK, GLA_DV, GLA_RANK, GLA_NORMALIZER = 4, 64, 128, 16, 16.0
HGRN_HEADS, HGRN_EXPAND, HGRN_DV = 4, 128, 128
LB_FLOOR = 1e-30
N_EXPERTS, N_GROUPS, TOP_K, D_FF = 16, 4, 2, 256
EXPERTS_PER_GROUP = N_EXPERTS // N_GROUPS
ALPHA = (2.0 * DEPTH) ** 0.25
LN_EPS = 1e-5
RMS_EPS = 1e-6

GDN_QK = GDN_HEADS * GDN_DK
GDN_V = GDN_HEADS * GDN_DV
GLA_QK = GLA_HEADS * GLA_DK
GLA_V = GLA_HEADS * GLA_DV
HGRN_QK = HGRN_HEADS * HGRN_EXPAND
HGRN_V = HGRN_HEADS * HGRN_DV
SPLIT_SIZES = (GDN_QK, GDN_QK, GDN_V, GDN_HEADS, GDN_HEADS, GDN_V,
               GLA_QK, GLA_QK, GLA_V, GLA_RANK, GLA_V,
               HGRN_QK, HGRN_QK, HGRN_V, HGRN_V,
               D_MODEL, D_MODEL, D_MODEL)
SPLIT_POINTS = tuple(int(v) for v in np.cumsum(SPLIT_SIZES)[:-1])

LANE = 128
SUBLANE = 8
VMEM_LIMIT = 48 * 1024 * 1024

M_A, M_B, M_C = 0, 1024, 2048
A_Q, A_K, A_V, A_G = 3072, 3584, 4096, 4608
B_Q, B_K, B_V, B_G = 5120, 5376, 5632, 6144
C_Q, C_I, C_G = 6656, 7168, 7680
NPB = 8192
F_CF, F_AS, F_LR = 0, 512, 640
NPF = 768

SUB = 8
DD_NPAR = 4
GDN_NPAR = 4
MOE_EPS = 4


def _cparams(sem):
    return pltpu.CompilerParams(dimension_semantics=sem, vmem_limit_bytes=VMEM_LIMIT)


def _mm(a, b):
    return jnp.dot(a.astype(BF16), b.astype(BF16), preferred_element_type=F32)


def _mm_nt(a, b):
    return lax.dot_general(a.astype(BF16), b.astype(BF16), (((1,), (1,)), ((), ())),
                           preferred_element_type=F32)


def _mm_tn(a, b):
    return lax.dot_general(a.astype(BF16), b.astype(BF16), (((0,), (0,)), ((), ())),
                           preferred_element_type=F32)


def _mm_nt_f32(a, b):
    return lax.dot_general(a, b, (((1,), (1,)), ((), ())), preferred_element_type=F32,
                           precision=HIGHEST)


def _split3(x):
    hi = x.astype(BF16)
    r = x - hi.astype(F32)
    mid = r.astype(BF16)
    lo = (r - mid.astype(F32)).astype(BF16)
    return hi, mid, lo


def _mm_01(m01, x):
    hi, mid, lo = _split3(x)
    return (jnp.dot(m01, hi, preferred_element_type=F32) + jnp.dot(m01, mid, preferred_element_type=F32)
            + jnp.dot(m01, lo, preferred_element_type=F32))


def _sigmoid(x):
    return 1.0 / (1.0 + jnp.exp(-x))


def _sigmoid_t(x):
    return 0.5 * jnp.tanh(0.5 * x) + 0.5


def _silu(x):
    return x * _sigmoid_t(x)


def _softplus(x):
    return jnp.maximum(x, 0.0) + jnp.log(1.0 + jnp.exp(-jnp.abs(x)))


def _log_sigmoid(x):
    return -_softplus(-x)


def _layer_norm(x, g, b):
    mu = jnp.mean(x, axis=-1, keepdims=True)
    xc = x - mu
    var = jnp.mean(xc * xc, axis=-1, keepdims=True)
    return xc * lax.rsqrt(var + LN_EPS) * g + b


def _ln_kernel(x_ref, g_ref, b_ref, o_ref, ob_ref):
    y = _layer_norm(x_ref[...], g_ref[...], b_ref[...])
    o_ref[...] = y
    ob_ref[...] = y.astype(BF16)


def _ln_call(x, g, b, tm=512):
    t, d = x.shape
    tm = min(tm, t)
    return pl.pallas_call(
        _ln_kernel,
        grid=(t // tm,),
        in_specs=[pl.BlockSpec((tm, d), lambda i: (i, 0)),
                  pl.BlockSpec((1, d), lambda i: (0, 0)),
                  pl.BlockSpec((1, d), lambda i: (0, 0))],
        out_specs=[pl.BlockSpec((tm, d), lambda i: (i, 0)), pl.BlockSpec((tm, d), lambda i: (i, 0))],
        out_shape=[jax.ShapeDtypeStruct((t, d), F32), jax.ShapeDtypeStruct((t, d), BF16)],
        compiler_params=_cparams(("parallel",)),
        name="ln0",
    )(x, g.reshape(1, d), b.reshape(1, d))


def _inproj_kernel(x_ref, w_ref, o_ref):
    o_ref[...] = jnp.dot(x_ref[...], w_ref[...], preferred_element_type=F32).astype(o_ref.dtype)


def _inproj_call(hb, w, layer, out_dtype, tm, tn, name):
    t, d = hb.shape
    tm = min(tm, t)
    n = w.shape[-1]
    return pl.pallas_call(
        _inproj_kernel,
        grid=(n // tn, t // tm),
        in_specs=[pl.BlockSpec((tm, d), lambda j, i: (i, 0)),
                  pl.BlockSpec((None, d, tn), lambda j, i: (layer, 0, j))],
        out_specs=pl.BlockSpec((tm, tn), lambda j, i: (i, j)),
        out_shape=jax.ShapeDtypeStruct((t, n), out_dtype),
        compiler_params=_cparams(("parallel", "parallel")),
        name=name,
    )(hb, w)


def _gdn_kernel(qkv_ref, s_ref, gate_ref, cw_ref, par_ref, nw_ref, o_ref,
                state_ref, tail_ref, xbuf_ref, q_s, k_s, v_s, cumb_s, betab_s, cumrow_s, o_s, *, tb):
    blk = pl.program_id(1)
    nc = tb // CHUNK
    nh = GDN_HEADS
    off0 = SUBLANE - (CONV_W - 1)

    @pl.when(blk == 0)
    def _():
        state_ref[...] = jnp.zeros_like(state_ref)
        tail_ref[...] = jnp.zeros_like(tail_ref)

    row = lax.broadcasted_iota(jnp.int32, (CHUNK, CHUNK), 0)
    col = lax.broadcasted_iota(jnp.int32, (CHUNK, CHUNK), 1)
    incl = col <= row
    strict = col < row
    tri = jnp.where(incl, 1.0, 0.0).astype(BF16)
    eye = jnp.where(col == row, 1.0, 0.0).astype(F32)
    npar = GDN_NPAR
    rp = npar * CHUNK
    xbuf_ref[0:SUBLANE, :] = tail_ref[...]

    def prologue(t):
        lo = t * rp
        xbuf_ref[SUBLANE + lo:SUBLANE + lo + rp, :] = qkv_ref[lo:lo + rp, :].astype(F32)
        for j in range(3 * nh):
            cs = slice(j * LANE, (j + 1) * LANE)
            y = xbuf_ref[off0 + lo:off0 + lo + rp, cs] * cw_ref[0:1, cs]
            for kk in range(1, CONV_W):
                y = y + xbuf_ref[off0 + kk + lo:off0 + kk + lo + rp, cs] * cw_ref[kk:kk + 1, cs]
            y = _silu(y)
            if j < nh:
                q_s[j, lo:lo + rp, :] = (y * lax.rsqrt(jnp.sum(y * y, axis=-1, keepdims=True) + RMS_EPS)
                                         * (GDN_DK ** -0.5))
            elif j < 2 * nh:
                k_s[j - nh, lo:lo + rp, :] = y * lax.rsqrt(jnp.sum(y * y, axis=-1, keepdims=True) + RMS_EPS)
            else:
                v_s[j - 2 * nh, lo:lo + rp, :] = y
        sc = s_ref[lo:lo + rp, :]
        beta_all = _sigmoid_t(sc)
        g_all = -jnp.exp(par_ref[0:1, :]) * _softplus(sc + par_ref[1:2, :])
        cum_all = jnp.concatenate([_mm_01(tri, g_all[c * CHUNK:(c + 1) * CHUNK, :]) for c in range(npar)], axis=0)
        cum_t = jnp.transpose(cum_all)
        for c in range(npar):
            cumrow_s[t * npar + c] = cum_t[0:SUBLANE, c * CHUNK:(c + 1) * CHUNK]
        for h in range(nh):
            cumb_s[h, lo:lo + rp, :] = jnp.broadcast_to(cum_all[:, nh + h:nh + h + 1], (rp, LANE))
            betab_s[h, lo:lo + rp, :] = jnp.broadcast_to(beta_all[:, h:h + 1], (rp, LANE))

    def trip(cp):
        chains = [(cp * npar + cc, h) for cc in range(npar) for h in range(nh)]
        r0s = [c * CHUNK for c, _ in chains]
        qc = [q_s[h, pl.ds(r0, CHUNK), :] for (_, h), r0 in zip(chains, r0s)]
        kc = [k_s[h, pl.ds(r0, CHUNK), :] for (_, h), r0 in zip(chains, r0s)]
        vc = [v_s[h, pl.ds(r0, CHUNK), :] for (_, h), r0 in zip(chains, r0s)]
        cumc = [cumb_s[h, pl.ds(r0, CHUNK), :] for (_, h), r0 in zip(chains, r0s)]
        bc = [betab_s[h, pl.ds(r0, CHUNK), :] for (_, h), r0 in zip(chains, r0s)]
        n = len(chains)
        kk = [_mm_nt(kc[i], kc[i]) for i in range(n)]
        qk = [_mm_nt(qc[i], kc[i]) for i in range(n)]
        decay = []
        for i, (c, h) in enumerate(chains):
            diff = cumc[i][:, 0:CHUNK] - cumrow_s[c][nh + h:nh + h + 1, :]
            decay.append(jnp.where(incl, jnp.exp(jnp.where(incl, diff, 0.0)), 0.0))
        a = [jnp.where(strict, bc[i][:, 0:CHUNK] * kk[i] * decay[i], 0.0) for i in range(n)]
        x = [eye - a[i] for i in range(n)]
        p = [_mm(a[i], a[i]) for i in range(n)]
        for it in range(5):
            x = [x[i] + _mm(x[i], p[i]) for i in range(n)]
            if it < 4:
                p = [_mm(p[i], p[i]) for i in range(n)]
        ecum = [jnp.exp(cumc[i]) for i in range(n)]
        sol = [_mm(x[i], jnp.concatenate([vc[i] * bc[i], kc[i] * (bc[i] * ecum[i])], axis=1)) for i in range(n)]
        attn = [qk[i] * decay[i] for i in range(n)]
        cum_last = [cumc[i][CHUNK - 1:CHUNK, :] for i in range(n)]
        k_state = [kc[i] * jnp.exp(cum_last[i] - cumc[i]) for i in range(n)]
        wqi = [jnp.concatenate([sol[i][:, GDN_DV:GDN_DV + GDN_DK], qc[i] * ecum[i]], axis=0) for i in range(n)]
        for cc in range(npar):
            idx = [cc * nh + h for h in range(nh)]
            s = [state_ref[h] for h in range(nh)]
            wq = [_mm(wqi[i], s[h]) for h, i in enumerate(idx)]
            v_new = [sol[i][:, 0:GDN_DV] - wq[h][0:CHUNK, :] for h, i in enumerate(idx)]
            av = [_mm(attn[i], v_new[h]) for h, i in enumerate(idx)]
            upd = [_mm_tn(k_state[i], v_new[h]) for h, i in enumerate(idx)]
            for h, i in enumerate(idx):
                state_ref[h] = s[h] * jnp.exp(cum_last[i]) + upd[h]
                o_s[pl.ds(r0s[i], CHUNK), h * LANE:(h + 1) * LANE] = wq[h][CHUNK:2 * CHUNK, :] + av[h]

    def epilogue(t):
        lo = t * rp
        for h in range(nh):
            hs = slice(h * LANE, (h + 1) * LANE)
            o = o_s[lo:lo + rp, hs]
            oh = o * lax.rsqrt(jnp.mean(o * o, axis=-1, keepdims=True) + RMS_EPS) * nw_ref[...]
            o_ref[lo:lo + rp, hs] = (oh * _silu(gate_ref[lo:lo + rp, hs].astype(F32))).astype(o_ref.dtype)

    nparts = nc // npar
    prologue(0)
    for t in range(nparts):
        if t + 1 < nparts:
            prologue(t + 1)
        trip(t)
        epilogue(t)
    tail_ref[...] = xbuf_ref[tb:tb + SUBLANE, :]


def _gdn_call(pb, pf, conv_w, par, norm_w, layer, batch, seq, tb=512):
    tb = min(tb, seq)
    nb = seq // tb
    nc = tb // CHUNK
    t = batch * seq
    wq = 2 * GDN_QK + GDN_V
    kern = functools.partial(_gdn_kernel, tb=tb)
    return pl.pallas_call(
        kern,
        grid=(batch, nb),
        in_specs=[pl.BlockSpec((tb, wq), lambda b, i: (b * nb + i, A_Q // wq)),
                  pl.BlockSpec((tb, LANE), lambda b, i: (b * nb + i, F_AS // LANE)),
                  pl.BlockSpec((tb, GDN_V), lambda b, i: (b * nb + i, A_G // GDN_V)),
                  pl.BlockSpec((None, CONV_W, wq), lambda b, i: (layer, 0, 0)),
                  pl.BlockSpec((None, SUBLANE, LANE), lambda b, i: (layer, 0, 0)),
                  pl.BlockSpec((None, 1, GDN_DV), lambda b, i: (layer, 0, 0))],
        out_specs=pl.BlockSpec((tb, GDN_V), lambda b, i: (b * nb + i, 0)),
        out_shape=jax.ShapeDtypeStruct((t, GDN_V), BF16),
        scratch_shapes=[pltpu.VMEM((GDN_HEADS, GDN_DK, GDN_DV), F32),
                        pltpu.VMEM((SUBLANE, wq), F32),
                        pltpu.VMEM((tb + SUBLANE, wq), F32),
                        pltpu.VMEM((GDN_HEADS, tb, LANE), F32),
                        pltpu.VMEM((GDN_HEADS, tb, LANE), F32),
                        pltpu.VMEM((GDN_HEADS, tb, LANE), F32),
                        pltpu.VMEM((GDN_HEADS, tb, LANE), F32),
                        pltpu.VMEM((GDN_HEADS, tb, LANE), F32),
                        pltpu.VMEM((nc, SUBLANE, CHUNK), F32),
                        pltpu.VMEM((tb, GDN_V), F32)],
        compiler_params=_cparams(("parallel", "arbitrary")),
        name="gdn",
    )(pb, pf, pb, conv_w, par, norm_w)


def _dd_core(q_s, k_s, v_s, la_s, o_s, state_ref, c8_s, p_s, *, tb, g_heads):
    nchunk = tb // CHUNK
    nblk = CHUNK // SUB
    dkh = LANE // g_heads
    dvp = g_heads * LANE
    row = lax.broadcasted_iota(jnp.int32, (CHUNK, CHUNK), 0)
    col = lax.broadcasted_iota(jnp.int32, (CHUNK, CHUNK), 1)
    level_masks = []
    for sh in (5, 4, 3):
        same2b = jnp.right_shift(row, sh + 1) == jnp.right_shift(col, sh + 1)
        upper = (jnp.right_shift(row, sh) & 1) == 1
        lower = (jnp.right_shift(col, sh) & 1) == 0
        level_masks.append(jnp.where(same2b, jnp.where(upper, jnp.where(lower, 1.0, 0.0), 0.0), 0.0))
    tri8 = jnp.where(jnp.right_shift(row, 3) == jnp.right_shift(col, 3),
                     jnp.where(col <= row, 1.0, 0.0), 0.0).astype(BF16)
    lane128 = lax.broadcasted_iota(jnp.int32, (CHUNK, LANE), 1)
    head_masks = [jnp.where((lane128 >= g * dkh) & (lane128 < (g + 1) * dkh), 1.0, 0.0)
                  for g in range(g_heads)]
    sub = lax.broadcasted_iota(jnp.int32, (SUB, LANE), 0)
    dk_sh = dkh.bit_length() - 1
    lane_sh = LANE.bit_length() - 1
    orow = lax.broadcasted_iota(jnp.int32, (LANE, dvp), 0)
    ocol = lax.broadcasted_iota(jnp.int32, (LANE, dvp), 1)
    ones_bd = jnp.where(jnp.right_shift(orow, dk_sh) == jnp.right_shift(ocol, lane_sh), 1.0, 0.0).astype(BF16)
    srow = lax.broadcasted_iota(jnp.int32, (dvp, LANE), 0)
    scol = lax.broadcasted_iota(jnp.int32, (dvp, LANE), 1)
    state_mask = jnp.where(jnp.right_shift(srow, lane_sh) == jnp.right_shift(scol, dk_sh), 1.0, 0.0)
    npar = DD_NPAR

    def block_sums(i):
        c8 = [c8_s[i, b * SUB:(b + 1) * SUB, :] for b in range(nblk)]
        t8 = [c8_s[i, (b + 1) * SUB - 1:(b + 1) * SUB, :] for b in range(nblk)]
        t16 = [t8[2 * b] + t8[2 * b + 1] for b in range(nblk // 2)]
        t32 = [t16[2 * b] + t16[2 * b + 1] for b in range(nblk // 4)]
        t64 = t32[0] + t32[1]
        c16 = [c8[b] + t8[b - 1] if b % 2 else c8[b] for b in range(nblk)]
        c32 = [c16[b] + t16[b // 2 - 1] if (b // 2) % 2 else c16[b] for b in range(nblk)]
        c64 = [c32[b] + t32[0] if b >= nblk // 2 else c32[b] for b in range(nblk)]
        pre = {8: c8, 16: c16, 32: c32, 64: c64}
        suf = {8: [t8[b] - c8[b] for b in range(nblk)],
               16: [t16[b // 2] - c16[b] for b in range(nblk)],
               32: [t32[b // 4] - c32[b] for b in range(nblk)],
               64: [t64 - c64[b] for b in range(nblk)]}
        return pre, suf

    def cat(pieces):
        return jnp.concatenate(pieces, axis=0)

    def group_body(cg, carry):
        rng = range(npar)
        r0s = [pl.multiple_of((cg * npar + i) * CHUNK, CHUNK) for i in rng]
        qc = [q_s[pl.ds(r0, CHUNK), :] for r0 in r0s]
        kc = [k_s[pl.ds(r0, CHUNK), :] for r0 in r0s]
        vc = [v_s[pl.ds(r0, CHUNK), :] for r0 in r0s]
        c8_all = _mm_01(tri8, jnp.concatenate([la_s[pl.ds(r0, CHUNK), :] for r0 in r0s], axis=1))
        for i in rng:
            c8_s[i] = c8_all[:, i * LANE:(i + 1) * LANE]
        sums = [block_sums(i) for i in rng]
        attn = [[None] * g_heads for _ in rng]
        for li, b in enumerate((32, 16, 8)):
            qs = [qc[i] * jnp.exp(cat(sums[i][0][b])) for i in rng]
            ks = [kc[i] * jnp.exp(cat(sums[i][1][b])) for i in rng]
            for g in range(g_heads):
                for i in rng:
                    qg = qs[i] * head_masks[g] if g_heads > 1 else qs[i]
                    term = _mm_nt(qg, ks[i]) * level_masks[li]
                    attn[i][g] = term if li == 0 else attn[i][g] + term
        o = [jnp.concatenate([_mm(attn[i][g], vc[i][:, g * LANE:(g + 1) * LANE]) for g in range(g_heads)], axis=1)
             if g_heads > 1 else _mm(attn[i][0], vc[i]) for i in rng]
        for i in rng:
            c8 = sums[i][0][8]
            for r in range(nblk):
                qr = qc[i][r * SUB:(r + 1) * SUB, :]
                for jj in range(SUB):
                    krow = k_s[pl.ds(r0s[i] + r * SUB + jj, 1), :]
                    crow = c8_s[i, r * SUB + jj:r * SUB + jj + 1, :]
                    m = sub >= jj
                    pr = qr * krow * jnp.exp(jnp.where(m, c8[r] - crow, -jnp.inf))
                    p_s[i, (r * SUB + jj) * SUB:(r * SUB + jj + 1) * SUB, :] = pr
        rs = [jnp.dot(p_s[i].astype(BF16), ones_bd, preferred_element_type=F32) for i in rng]
        for i in rng:
            od = []
            for r in range(nblk):
                acc = None
                for jj in range(SUB):
                    vrow = v_s[pl.ds(r0s[i] + r * SUB + jj, 1), :]
                    term = rs[i][(r * SUB + jj) * SUB:(r * SUB + jj + 1) * SUB, :] * vrow
                    acc = term if acc is None else acc + term
                od.append(acc)
            o[i] = o[i] + cat(od)
        q_inter = [qc[i] * jnp.exp(cat(sums[i][0][64])) for i in rng]
        k_state = [kc[i] * jnp.exp(cat(sums[i][1][64])) for i in rng]
        upd = [_mm_tn(vc[i], k_state[i]) for i in rng]
        st = state_ref[...]
        for i in rng:
            o_s[pl.ds(r0s[i], CHUNK), :] = o[i] + _mm_nt(q_inter[i], st)
            decay_last = jnp.exp(sums[i][0][64][nblk - 1][SUB - 1:SUB, :])
            st = st * decay_last + (upd[i] * state_mask if g_heads > 1 else upd[i])
        state_ref[...] = st
        return carry

    lax.fori_loop(0, nchunk // npar, group_body, 0)


def _gated_rms_out(o_s, gate_ref, nw_ref, o_ref, g_heads):
    for g in range(g_heads):
        hs = slice(g * LANE, (g + 1) * LANE)
        o = o_s[:, hs]
        oh = o * lax.rsqrt(jnp.mean(o * o, axis=-1, keepdims=True) + RMS_EPS) * nw_ref[...]
        o_ref[:, hs] = (oh * _silu(gate_ref[:, hs].astype(F32))).astype(o_ref.dtype)


def _gla_kernel(q_ref, k_ref, v_ref, lr_ref, gate_ref, w2_ref, b2_ref, nw_ref,
                o_ref, state_ref, q_s, k_s, v_s, la_s, o_s, c8_s, p_s, *, tb):
    @pl.when(pl.program_id(2) == 0)
    def _():
        state_ref[...] = jnp.zeros_like(state_ref)

    q_s[...] = q_ref[...].astype(F32) * (GLA_DK ** -0.5)
    k_s[...] = k_ref[...].astype(F32)
    v_s[...] = v_ref[...].astype(F32)
    z = _mm(lr_ref[...], w2_ref[...]) + b2_ref[...]
    la_s[...] = _log_sigmoid(z) * (1.0 / GLA_NORMALIZER)
    _dd_core(q_s, k_s, v_s, la_s, o_s, state_ref, c8_s, p_s, tb=tb, g_heads=2)
    _gated_rms_out(o_s, gate_ref, nw_ref, o_ref, 2)


def _dd_scratch(tb, dvp):
    return [pltpu.VMEM((dvp, LANE), F32),
            pltpu.VMEM((tb, LANE), F32), pltpu.VMEM((tb, LANE), F32),
            pltpu.VMEM((tb, dvp), F32),
            pltpu.VMEM((tb, LANE), F32),
            pltpu.VMEM((tb, dvp), F32),
            pltpu.VMEM((DD_NPAR, CHUNK, LANE), F32),
            pltpu.VMEM((DD_NPAR, CHUNK * SUB, LANE), F32)]


def _gla_call(pb, pf, w2p, b2, norm_w, layer, batch, seq, tb=512):
    tb = min(tb, seq)
    nb = seq // tb
    t = batch * seq
    npair = GLA_HEADS // 2
    kern = functools.partial(_gla_kernel, tb=tb)
    return pl.pallas_call(
        kern,
        grid=(batch, npair, nb),
        in_specs=[pl.BlockSpec((tb, LANE), lambda b, p, i: (b * nb + i, B_Q // LANE + p)),
                  pl.BlockSpec((tb, LANE), lambda b, p, i: (b * nb + i, B_K // LANE + p)),
                  pl.BlockSpec((tb, 2 * LANE), lambda b, p, i: (b * nb + i, B_V // (2 * LANE) + p)),
                  pl.BlockSpec((tb, LANE), lambda b, p, i: (b * nb + i, F_LR // LANE)),
                  pl.BlockSpec((tb, 2 * LANE), lambda b, p, i: (b * nb + i, B_G // (2 * LANE) + p)),
                  pl.BlockSpec((None, LANE, LANE), lambda b, p, i: (layer, 0, p)),
                  pl.BlockSpec((None, 1, LANE), lambda b, p, i: (layer, 0, p)),
                  pl.BlockSpec((None, 1, GLA_DV), lambda b, p, i: (layer, 0, 0))],
        out_specs=pl.BlockSpec((tb, 2 * LANE), lambda b, p, i: (b * nb + i, p)),
        out_shape=jax.ShapeDtypeStruct((t, GLA_V), BF16),
        scratch_shapes=_dd_scratch(tb, 2 * LANE),
        compiler_params=_cparams(("parallel", "parallel", "arbitrary")),
        name="gla",
    )(pb, pb, pb, pf, pb, w2p, b2, norm_w)


def _hgrn_kernel(q_ref, f_ref, v_ref, gate_ref, lbl_ref, nw_ref,
                 o_ref, state_ref, q_s, k_s, v_s, la_s, o_s, c8_s, p_s, *, tb, layer):
    @pl.when(pl.program_id(2) == 0)
    def _():
        state_ref[...] = jnp.zeros_like(state_ref)

    logits = lbl_ref[...]
    mx = jnp.max(logits, axis=0, keepdims=True)
    ex = jnp.exp(logits - mx)
    p = ex / jnp.sum(ex, axis=0, keepdims=True)
    acc = p[0:1, :]
    for r in range(1, layer + 1):
        acc = acc + p[r:r + 1, :]
    lb = jnp.clip(acc - p[0:1, :], 0.0, 1.0)
    log_lb = jnp.log(jnp.maximum(lb, LB_FLOOR))
    log_1m = jnp.log(1.0 - lb)

    cf = f_ref[...]
    second = log_1m + _log_sigmoid(cf)
    la_s[...] = jnp.maximum(log_lb, second) + jnp.log(1.0 + jnp.exp(-jnp.abs(log_lb - second)))
    k_s[...] = (1.0 - lb) * _sigmoid_t(-cf)
    q_s[...] = _silu(q_ref[...].astype(F32)) * (HGRN_EXPAND ** -0.5)
    v_s[...] = v_ref[...].astype(F32)
    _dd_core(q_s, k_s, v_s, la_s, o_s, state_ref, c8_s, p_s, tb=tb, g_heads=1)
    _gated_rms_out(o_s, gate_ref, nw_ref, o_ref, 1)


def _hgrn_call(pb, pf, lb_logits, norm_w, layer, batch, seq, tb=512):
    tb = min(tb, seq)
    nb = seq // tb
    t = batch * seq

    def tok(colblk):
        return pl.BlockSpec((tb, LANE), lambda b, h, i: (b * nb + i, colblk + h))

    kern = functools.partial(_hgrn_kernel, tb=tb, layer=layer)
    return pl.pallas_call(
        kern,
        grid=(batch, HGRN_HEADS, nb),
        in_specs=[tok(C_Q // LANE), tok(F_CF // LANE), tok(C_I // LANE), tok(C_G // LANE),
                  pl.BlockSpec((DEPTH, LANE), lambda b, h, i: (0, h)),
                  pl.BlockSpec((None, 1, HGRN_DV), lambda b, h, i: (layer, 0, 0))],
        out_specs=pl.BlockSpec((tb, LANE), lambda b, h, i: (b * nb + i, h)),
        out_shape=jax.ShapeDtypeStruct((t, HGRN_V), BF16),
        scratch_shapes=_dd_scratch(tb, LANE),
        compiler_params=_cparams(("parallel", "parallel", "arbitrary")),
        name="hgrn",
    )(pb, pf, pb, pb, lb_logits, norm_w)


def _dd_make_trip(q_s, k_s, v_s, la_s, o_s, state_ref, c8_s, p_s, g_heads):
    nblk = CHUNK // SUB
    dkh = LANE // g_heads
    dvp = g_heads * LANE
    npar = DD_NPAR
    row = lax.broadcasted_iota(jnp.int32, (CHUNK, CHUNK), 0)
    col = lax.broadcasted_iota(jnp.int32, (CHUNK, CHUNK), 1)
    level_masks = []
    for sh in (5, 4, 3):
        same2b = jnp.right_shift(row, sh + 1) == jnp.right_shift(col, sh + 1)
        upper = (jnp.right_shift(row, sh) & 1) == 1
        lower = (jnp.right_shift(col, sh) & 1) == 0
        level_masks.append(jnp.where(same2b, jnp.where(upper, jnp.where(lower, 1.0, 0.0), 0.0), 0.0))
    tri8 = jnp.where(jnp.right_shift(row, 3) == jnp.right_shift(col, 3),
                     jnp.where(col <= row, 1.0, 0.0), 0.0).astype(BF16)
    lane128 = lax.broadcasted_iota(jnp.int32, (CHUNK, LANE), 1)
    head_masks = [jnp.where((lane128 >= g * dkh) & (lane128 < (g + 1) * dkh), 1.0, 0.0)
                  for g in range(g_heads)]
    sub = lax.broadcasted_iota(jnp.int32, (SUB, LANE), 0)
    dk_sh = dkh.bit_length() - 1
    lane_sh = LANE.bit_length() - 1
    orow = lax.broadcasted_iota(jnp.int32, (LANE, dvp), 0)
    ocol = lax.broadcasted_iota(jnp.int32, (LANE, dvp), 1)
    ones_bd = jnp.where(jnp.right_shift(orow, dk_sh) == jnp.right_shift(ocol, lane_sh), 1.0, 0.0).astype(BF16)
    srow = lax.broadcasted_iota(jnp.int32, (dvp, LANE), 0)
    scol = lax.broadcasted_iota(jnp.int32, (dvp, LANE), 1)
    state_mask = jnp.where(jnp.right_shift(srow, lane_sh) == jnp.right_shift(scol, dk_sh), 1.0, 0.0)

    def block_sums(ci):
        c8 = [c8_s[ci, b * SUB:(b + 1) * SUB, :] for b in range(nblk)]
        t8 = [c8_s[ci, (b + 1) * SUB - 1:(b + 1) * SUB, :] for b in range(nblk)]
        t16 = [t8[2 * b] + t8[2 * b + 1] for b in range(nblk // 2)]
        t32 = [t16[2 * b] + t16[2 * b + 1] for b in range(nblk // 4)]
        t64 = t32[0] + t32[1]
        c16 = [c8[b] + t8[b - 1] if b % 2 else c8[b] for b in range(nblk)]
        c32 = [c16[b] + t16[b // 2 - 1] if (b // 2) % 2 else c16[b] for b in range(nblk)]
        c64 = [c32[b] + t32[0] if b >= nblk // 2 else c32[b] for b in range(nblk)]
        pre = {8: c8, 16: c16, 32: c32, 64: c64}
        suf = {8: [t8[b] - c8[b] for b in range(nblk)],
               16: [t16[b // 2] - c16[b] for b in range(nblk)],
               32: [t32[b // 4] - c32[b] for b in range(nblk)],
               64: [t64 - c64[b] for b in range(nblk)]}
        return pre, suf

    def cat(pieces):
        return jnp.concatenate(pieces, axis=0)

    def trip(u, t, slot):
        rng = range(npar)
        r0s = [(t * npar + i) * CHUNK for i in rng]
        cis = [slot * npar + i for i in rng]
        qc = [q_s[u, r0:r0 + CHUNK, :] for r0 in r0s]
        kc = [k_s[u, r0:r0 + CHUNK, :] for r0 in r0s]
        vc = [v_s[u, r0:r0 + CHUNK, :] for r0 in r0s]
        c8_all = _mm_01(tri8, jnp.concatenate([la_s[u, r0:r0 + CHUNK, :] for r0 in r0s], axis=1))
        for i in rng:
            c8_s[cis[i]] = c8_all[:, i * LANE:(i + 1) * LANE]
        sums = [block_sums(cis[i]) for i in rng]
        attn = [[None] * g_heads for _ in rng]
        for li, b in enumerate((32, 16, 8)):
            qs = [qc[i] * jnp.exp(cat(sums[i][0][b])) for i in rng]
            ks = [kc[i] * jnp.exp(cat(sums[i][1][b])) for i in rng]
            for g in range(g_heads):
                for i in rng:
                    qg = qs[i] * head_masks[g] if g_heads > 1 else qs[i]
                    term = _mm_nt(qg, ks[i]) * level_masks[li]
                    attn[i][g] = term if li == 0 else attn[i][g] + term
        o = [jnp.concatenate([_mm(attn[i][g], vc[i][:, g * LANE:(g + 1) * LANE]) for g in range(g_heads)], axis=1)
             if g_heads > 1 else _mm(attn[i][0], vc[i]) for i in rng]
        for i in rng:
            c8 = sums[i][0][8]
            for r in range(nblk):
                qr = qc[i][r * SUB:(r + 1) * SUB, :]
                for jj in range(SUB):
                    rr = r0s[i] + r * SUB + jj
                    krow = k_s[u, rr:rr + 1, :]
                    crow = c8_s[cis[i], r * SUB + jj:r * SUB + jj + 1, :]
                    pr = qr * krow * jnp.exp(jnp.where(sub >= jj, c8[r] - crow, -jnp.inf))
                    p_s[cis[i], (r * SUB + jj) * SUB:(r * SUB + jj + 1) * SUB, :] = pr
        rs = [jnp.dot(p_s[cis[i]].astype(BF16), ones_bd, preferred_element_type=F32) for i in rng]
        for i in rng:
            od = []
            for r in range(nblk):
                acc = None
                for jj in range(SUB):
                    rr = r0s[i] + r * SUB + jj
                    term = rs[i][(r * SUB + jj) * SUB:(r * SUB + jj + 1) * SUB, :] * v_s[u, rr:rr + 1, :]
                    acc = term if acc is None else acc + term
                od.append(acc)
            o[i] = o[i] + cat(od)
        q_inter = [qc[i] * jnp.exp(cat(sums[i][0][64])) for i in rng]
        k_state = [kc[i] * jnp.exp(cat(sums[i][1][64])) for i in rng]
        upd = [_mm_tn(vc[i], k_state[i]) for i in rng]
        st = state_ref[u]
        for i in rng:
            o_s[u, r0s[i]:r0s[i] + CHUNK, :] = o[i] + _mm_nt(q_inter[i], st)
            decay_last = jnp.exp(sums[i][0][64][nblk - 1][SUB - 1:SUB, :])
            st = st * decay_last + (upd[i] * state_mask if g_heads > 1 else upd[i])
        state_ref[u] = st

    return trip


def _dd_schedule(units, prologue, trip, epilogue):
    prologue(*units[0])
    for n, (u, t) in enumerate(units):
        if n + 1 < len(units):
            prologue(*units[n + 1])
        trip(u, t, n % 2)
        epilogue(u, t)


def _dd_scratch4(nu, tb, dvp):
    return [pltpu.VMEM((nu, dvp, LANE), F32),
            pltpu.VMEM((nu, tb, LANE), F32), pltpu.VMEM((nu, tb, LANE), F32),
            pltpu.VMEM((nu, tb, dvp), F32),
            pltpu.VMEM((nu, tb, LANE), F32),
            pltpu.VMEM((nu, tb, dvp), F32),
            pltpu.VMEM((2 * DD_NPAR, CHUNK, LANE), F32),
            pltpu.VMEM((2 * DD_NPAR, CHUNK * SUB, LANE), F32)]


def _gla4_kernel(q_ref, k_ref, v_ref, lr_ref, gate_ref, w2_ref, b2_ref, nw_ref,
                 o_ref, state_ref, q_s, k_s, v_s, la_s, o_s, c8_s, p_s, *, tb):
    @pl.when(pl.program_id(1) == 0)
    def _():
        state_ref[...] = jnp.zeros_like(state_ref)

    npair = GLA_HEADS // 2
    rp = DD_NPAR * CHUNK
    trip = _dd_make_trip(q_s, k_s, v_s, la_s, o_s, state_ref, c8_s, p_s, 2)

    def prologue(u, t):
        lo = t * rp
        ls = slice(u * LANE, (u + 1) * LANE)
        vs = slice(u * 2 * LANE, (u + 1) * 2 * LANE)
        q_s[u, lo:lo + rp, :] = q_ref[lo:lo + rp, ls].astype(F32) * (GLA_DK ** -0.5)
        k_s[u, lo:lo + rp, :] = k_ref[lo:lo + rp, ls].astype(F32)
        v_s[u, lo:lo + rp, :] = v_ref[lo:lo + rp, vs].astype(F32)
        z = _mm(lr_ref[lo:lo + rp, :], w2_ref[:, ls]) + b2_ref[:, ls]
        la_s[u, lo:lo + rp, :] = _log_sigmoid(z) * (1.0 / GLA_NORMALIZER)

    def epilogue(u, t):
        lo = t * rp
        for g in range(2):
            hs = slice((2 * u + g) * LANE, (2 * u + g + 1) * LANE)
            o = o_s[u, lo:lo + rp, g * LANE:(g + 1) * LANE]
            oh = o * lax.rsqrt(jnp.mean(o * o, axis=-1, keepdims=True) + RMS_EPS) * nw_ref[...]
            o_ref[lo:lo + rp, hs] = (oh * _silu(gate_ref[lo:lo + rp, hs].astype(F32))).astype(o_ref.dtype)

    _dd_schedule([(u, t) for u in range(npair) for t in range(tb // rp)], prologue, trip, epilogue)


def _gla4_call(pb, pf, w2p, b2, norm_w, layer, batch, seq, tb=512):
    tb = min(tb, seq)
    nb = seq // tb
    t = batch * seq
    kern = functools.partial(_gla4_kernel, tb=tb)
    return pl.pallas_call(
        kern,
        grid=(batch, nb),
        in_specs=[pl.BlockSpec((tb, GLA_QK), lambda b, i: (b * nb + i, B_Q // GLA_QK)),
                  pl.BlockSpec((tb, GLA_QK), lambda b, i: (b * nb + i, B_K // GLA_QK)),
                  pl.BlockSpec((tb, GLA_V), lambda b, i: (b * nb + i, B_V // GLA_V)),
                  pl.BlockSpec((tb, LANE), lambda b, i: (b * nb + i, F_LR // LANE)),
                  pl.BlockSpec((tb, GLA_V), lambda b, i: (b * nb + i, B_G // GLA_V)),
                  pl.BlockSpec((None, LANE, GLA_QK), lambda b, i: (layer, 0, 0)),
                  pl.BlockSpec((None, 1, GLA_QK), lambda b, i: (layer, 0, 0)),
                  pl.BlockSpec((None, 1, GLA_DV), lambda b, i: (layer, 0, 0))],
        out_specs=pl.BlockSpec((tb, GLA_V), lambda b, i: (b * nb + i, 0)),
        out_shape=jax.ShapeDtypeStruct((t, GLA_V), BF16),
        scratch_shapes=_dd_scratch4(GLA_HEADS // 2, tb, 2 * LANE),
        compiler_params=_cparams(("parallel", "arbitrary")),
        name="gla",
    )(pb, pb, pb, pf, pb, w2p, b2, norm_w)


def _hgrn4_kernel(q_ref, f_ref, v_ref, gate_ref, lbl_ref, nw_ref,
                  o_ref, state_ref, q_s, k_s, v_s, la_s, o_s, c8_s, p_s, *, tb, layer):
    @pl.when(pl.program_id(1) == 0)
    def _():
        state_ref[...] = jnp.zeros_like(state_ref)

    logits = lbl_ref[...]
    mx = jnp.max(logits, axis=0, keepdims=True)
    ex = jnp.exp(logits - mx)
    p = ex / jnp.sum(ex, axis=0, keepdims=True)
    acc = p[0:1, :]
    for r in range(1, layer + 1):
        acc = acc + p[r:r + 1, :]
    lb = jnp.clip(acc - p[0:1, :], 0.0, 1.0)
    log_lb = jnp.log(jnp.maximum(lb, LB_FLOOR))
    log_1m = jnp.log(1.0 - lb)

    rp = DD_NPAR * CHUNK
    trip = _dd_make_trip(q_s, k_s, v_s, la_s, o_s, state_ref, c8_s, p_s, 1)

    def prologue(u, t):
        lo = t * rp
        hs = slice(u * LANE, (u + 1) * LANE)
        cf = f_ref[lo:lo + rp, hs]
        second = log_1m[:, hs] + _log_sigmoid(cf)
        llb = log_lb[:, hs]
        la_s[u, lo:lo + rp, :] = jnp.maximum(llb, second) + jnp.log(1.0 + jnp.exp(-jnp.abs(llb - second)))
        k_s[u, lo:lo + rp, :] = (1.0 - lb[:, hs]) * _sigmoid_t(-cf)
        q_s[u, lo:lo + rp, :] = _silu(q_ref[lo:lo + rp, hs].astype(F32)) * (HGRN_EXPAND ** -0.5)
        v_s[u, lo:lo + rp, :] = v_ref[lo:lo + rp, hs].astype(F32)

    def epilogue(u, t):
        lo = t * rp
        hs = slice(u * LANE, (u + 1) * LANE)
        o = o_s[u, lo:lo + rp, :]
        oh = o * lax.rsqrt(jnp.mean(o * o, axis=-1, keepdims=True) + RMS_EPS) * nw_ref[...]
        o_ref[lo:lo + rp, hs] = (oh * _silu(gate_ref[lo:lo + rp, hs].astype(F32))).astype(o_ref.dtype)

    _dd_schedule([(u, t) for u in range(HGRN_HEADS) for t in range(tb // rp)], prologue, trip, epilogue)


def _hgrn4_call(pb, pf, lb_logits, norm_w, layer, batch, seq, tb=512):
    tb = min(tb, seq)
    nb = seq // tb
    t = batch * seq
    kern = functools.partial(_hgrn4_kernel, tb=tb, layer=layer)
    return pl.pallas_call(
        kern,
        grid=(batch, nb),
        in_specs=[pl.BlockSpec((tb, HGRN_QK), lambda b, i: (b * nb + i, C_Q // HGRN_QK)),
                  pl.BlockSpec((tb, HGRN_QK), lambda b, i: (b * nb + i, F_CF // HGRN_QK)),
                  pl.BlockSpec((tb, HGRN_V), lambda b, i: (b * nb + i, C_I // HGRN_V)),
                  pl.BlockSpec((tb, HGRN_V), lambda b, i: (b * nb + i, C_G // HGRN_V)),
                  pl.BlockSpec((DEPTH, HGRN_QK), lambda b, i: (0, 0)),
                  pl.BlockSpec((None, 1, HGRN_DV), lambda b, i: (layer, 0, 0))],
        out_specs=pl.BlockSpec((tb, HGRN_V), lambda b, i: (b * nb + i, 0)),
        out_shape=jax.ShapeDtypeStruct((t, HGRN_V), BF16),
        scratch_shapes=_dd_scratch4(HGRN_HEADS, tb, LANE),
        compiler_params=_cparams(("parallel", "arbitrary")),
        name="hgrn",
    )(pb, pf, pb, pb, lb_logits, norm_w)


def _merge_kernel(oa_ref, ob_ref, oc_ref, ma_ref, mb_ref, mc_ref, h_ref,
                  wa_ref, wb_ref, wc_ref, wo_ref, g_ref, b_ref, o_ref):
    y = (_sigmoid_t(ma_ref[...].astype(F32)) * jnp.dot(oa_ref[...], wa_ref[...], preferred_element_type=F32)
         + _sigmoid_t(mb_ref[...].astype(F32)) * jnp.dot(ob_ref[...], wb_ref[...], preferred_element_type=F32)
         + _sigmoid_t(mc_ref[...].astype(F32)) * jnp.dot(oc_ref[...], wc_ref[...], preferred_element_type=F32))
    mix = _mm(y, wo_ref[...])
    o_ref[...] = _layer_norm(ALPHA * h_ref[...] + mix, g_ref[...], b_ref[...])


def _merge_call(o_a, o_b, o_c, pb, h, wa, wb, wc, wo, g, b, layer, tm=512):
    t, d = h.shape
    tm = min(tm, t)

    def row(width):
        return pl.BlockSpec((tm, width), lambda i: (i, 0))

    def wspec(kdim):
        return pl.BlockSpec((None, kdim, d), lambda i: (layer, 0, 0))

    vec = pl.BlockSpec((None, 1, d), lambda i: (layer, 0, 0))
    return pl.pallas_call(
        _merge_kernel,
        grid=(t // tm,),
        in_specs=[row(GDN_V), row(GLA_V), row(HGRN_V),
                  pl.BlockSpec((tm, d), lambda i: (i, M_A // d)),
                  pl.BlockSpec((tm, d), lambda i: (i, M_B // d)),
                  pl.BlockSpec((tm, d), lambda i: (i, M_C // d)),
                  row(d), wspec(GDN_V), wspec(GLA_V), wspec(HGRN_V), wspec(d), vec, vec],
        out_specs=row(d),
        out_shape=jax.ShapeDtypeStruct((t, d), F32),
        compiler_params=_cparams(("parallel",)),
        name="merge",
    )(o_a, o_b, o_c, pb, pb, pb, h, wa, wb, wc, wo, g, b)


def _lane_col(x, idx):
    lane = lax.broadcasted_iota(jnp.int32, x.shape, 1)
    return jnp.sum(jnp.where(lane == idx, x, 0.0), axis=1, keepdims=True)


def _route(scores_t, bias_ref):
    s = [scores_t[e:e + 1, :] for e in range(N_EXPERTS)]
    sel = [s[e] + bias_ref[e:e + 1, 0:1] for e in range(N_EXPERTS)]
    gscore = []
    for g in range(N_GROUPS):
        a, b, c, d = sel[4 * g:4 * g + 4]
        hi1, lo1 = jnp.maximum(a, b), jnp.minimum(a, b)
        hi2, lo2 = jnp.maximum(c, d), jnp.minimum(c, d)
        top1 = jnp.maximum(hi1, hi2)
        top2 = jnp.maximum(jnp.minimum(hi1, hi2), jnp.maximum(lo1, lo2))
        gscore.append(top1 + top2)
    best = gscore[0]
    gidx = jnp.zeros_like(best, dtype=jnp.int32)
    for g in range(1, N_GROUPS):
        take = gscore[g] > best
        best = jnp.where(take, gscore[g], best)
        gidx = jnp.where(take, g, gidx)
    ing, raw = [], []
    for kk in range(EXPERTS_PER_GROUP):
        vs, vr = sel[kk], s[kk]
        for g in range(1, N_GROUPS):
            pick = gidx == g
            vs = jnp.where(pick, sel[4 * g + kk], vs)
            vr = jnp.where(pick, s[4 * g + kk], vr)
        ing.append(vs)
        raw.append(vr)
    b1 = ing[0]
    i1 = jnp.zeros_like(gidx)
    for kk in range(1, EXPERTS_PER_GROUP):
        take = ing[kk] > b1
        b1 = jnp.where(take, ing[kk], b1)
        i1 = jnp.where(take, kk, i1)
    neg = jnp.full_like(b1, -jnp.inf)
    b2 = neg
    i2 = jnp.zeros_like(gidx)
    for kk in range(EXPERTS_PER_GROUP):
        cand = jnp.where(i1 == kk, neg, ing[kk])
        take = cand > b2
        b2 = jnp.where(take, cand, b2)
        i2 = jnp.where(take, kk, i2)
    w1 = raw[0]
    w2 = raw[0]
    for kk in range(1, EXPERTS_PER_GROUP):
        w1 = jnp.where(i1 == kk, raw[kk], w1)
        w2 = jnp.where(i2 == kk, raw[kk], w2)
    tot = w1 + w2
    w1 = w1 / tot
    w2 = w2 / tot
    e1 = gidx * EXPERTS_PER_GROUP + i1
    e2 = gidx * EXPERTS_PER_GROUP + i2
    rows = [jnp.where(e1 == e, w1, 0.0) + jnp.where(e2 == e, w2, 0.0) for e in range(N_EXPERTS)]
    return jnp.concatenate(rows, axis=0)


def _moe_kernel(h_ref, wr_ref, rb_ref, wg_ref, wu_ref, wd_ref, g_ref, b_ref, o_ref, ob_ref,
                comb_ref, xb_ref, *, tm):
    eg = pl.program_id(1)

    @pl.when(eg == 0)
    def _():
        logits_t = _mm_nt_f32(wr_ref[...], h_ref[...])
        comb_t = _route(_sigmoid(logits_t), rb_ref)
        pad = jnp.zeros((LANE - N_EXPERTS, tm), F32)
        comb_ref[...] = jnp.transpose(jnp.concatenate([comb_t, pad], axis=0))
        xb_ref[...] = h_ref[...].astype(BF16)

    x = xb_ref[...]
    comb = comb_ref[...]
    y = None
    for kk in range(MOE_EPS):
        hg = jnp.dot(x, wg_ref[kk], preferred_element_type=F32)
        hu = jnp.dot(x, wu_ref[kk], preferred_element_type=F32)
        cw = _lane_col(comb, eg * MOE_EPS + kk)
        hid = _silu(hg) * hu * cw
        term = jnp.dot(hid.astype(BF16), wd_ref[kk], preferred_element_type=F32)
        y = term if y is None else y + term

    @pl.when(eg == 0)
    def _():
        o_ref[...] = y

    @pl.when(eg > 0)
    def _():
        o_ref[...] += y

    @pl.when(eg == N_EXPERTS // MOE_EPS - 1)
    def _():
        out = _layer_norm(ALPHA * h_ref[...] + o_ref[...], g_ref[...], b_ref[...])
        o_ref[...] = out
        ob_ref[...] = out.astype(BF16)


def _moe_call(h, wr_t, rbias, wg, wu, wd, g, b, layer, tm=1024):
    t, d = h.shape
    tm = min(tm, t)
    kern = functools.partial(_moe_kernel, tm=tm)
    vec = pl.BlockSpec((None, 1, d), lambda i, e: (layer, 0, 0))
    row = pl.BlockSpec((tm, d), lambda i, e: (i, 0))
    return pl.pallas_call(
        kern,
        grid=(t // tm, N_EXPERTS // MOE_EPS),
        in_specs=[row,
                  pl.BlockSpec((N_EXPERTS, d), lambda i, e: (0, 0)),
                  pl.BlockSpec((N_EXPERTS, LANE), lambda i, e: (0, 0)),
                  pl.BlockSpec((None, MOE_EPS, d, D_FF), lambda i, e: (layer, e, 0, 0)),
                  pl.BlockSpec((None, MOE_EPS, d, D_FF), lambda i, e: (layer, e, 0, 0)),
                  pl.BlockSpec((None, MOE_EPS, D_FF, d), lambda i, e: (layer, e, 0, 0)),
                  vec, vec],
        out_specs=[row, row],
        out_shape=[jax.ShapeDtypeStruct((t, d), F32), jax.ShapeDtypeStruct((t, d), BF16)],
        scratch_shapes=[pltpu.VMEM((tm, LANE), F32), pltpu.VMEM((tm, d), BF16)],
        compiler_params=_cparams(("parallel", "arbitrary")),
        name="moe",
    )(h, wr_t, rbias, wg, wu, wd, g, b)


def _pack_w_in(w_in):
    (a_q, a_k, a_v, a_beta, a_dt, a_g, b_q, b_k, b_v, b_lr, b_g,
     c_q, c_f, c_i, c_g, m_a, m_b, m_c) = jnp.split(w_in, SPLIT_POINTS, axis=-1)
    lead = w_in.shape[:-1]
    a_s = jnp.concatenate([a_beta, a_dt, jnp.zeros(lead + (LANE - 2 * GDN_HEADS,), w_in.dtype)], -1)
    b_lrp = jnp.concatenate([b_lr, jnp.zeros(lead + (LANE - GLA_RANK,), w_in.dtype)], -1)
    wb = jnp.concatenate([m_a, m_b, m_c, a_q, a_k, a_v, a_g, b_q, b_k, b_v, b_g, c_q, c_i, c_g], -1).astype(BF16)
    wf = jnp.concatenate([c_f, a_s, b_lrp], -1).astype(BF16)
    assert wb.shape[-1] == NPB and wf.shape[-1] == NPF
    return wb, wf


def _prepare(w_in, gdn_conv, gdn_a_log, gdn_dt_bias, gdn_norm, gla_w2, gla_b2, gla_norm, hgrn_lb_logits,
             hgrn_norm, w_br_a, w_br_b, w_br_c, w_out, ln1_g, ln1_b, w_router, router_bias, w_gate, w_up,
             w_down, ln2_g, ln2_b):
    depth = w_in.shape[0]
    d = w_out.shape[-1]
    w_pb, w_pf = _pack_w_in(w_in)
    return dict(
        w_pb=w_pb, w_pf=w_pf,
        gdn_conv=gdn_conv,
        gdn_par=jnp.pad(jnp.stack([gdn_a_log, gdn_dt_bias], axis=1),
                        ((0, 0), (0, SUBLANE - 2), (GDN_HEADS, LANE - 2 * GDN_HEADS))),
        gdn_norm=gdn_norm.reshape(depth, 1, GDN_DV),
        w2p=jnp.concatenate([gla_w2, jnp.zeros((depth, LANE - GLA_RANK, GLA_QK), gla_w2.dtype)], axis=1),
        gla_b2=gla_b2.reshape(depth, 1, GLA_QK),
        gla_norm=gla_norm.reshape(depth, 1, GLA_DV),
        lb_logits=hgrn_lb_logits,
        hgrn_norm=hgrn_norm.reshape(depth, 1, HGRN_DV),
        wa=w_br_a.astype(BF16), wb=w_br_b.astype(BF16), wc=w_br_c.astype(BF16), wo=w_out.astype(BF16),
        ln1_g=ln1_g.reshape(depth, 1, d), ln1_b=ln1_b.reshape(depth, 1, d),
        wr_t=jnp.transpose(w_router),
        rbias=jnp.broadcast_to(router_bias[:, None], (N_EXPERTS, LANE)),
        wg=w_gate.astype(BF16), wu=w_up.astype(BF16), wd=w_down.astype(BF16),
        ln2_g=ln2_g.reshape(depth, 1, d), ln2_b=ln2_b.reshape(depth, 1, d),
    )


def _mixer_block(h, hb, p, layer, batch, seq):
    pb = _inproj_call(hb, p["w_pb"], layer, BF16, 512, 2048, "inproj_b")
    pf = _inproj_call(hb, p["w_pf"], layer, F32, 1024, NPF, "inproj_f")
    o_a = _gdn_call(pb, pf, p["gdn_conv"], p["gdn_par"], p["gdn_norm"], layer, batch, seq)
    o_b = _gla4_call(pb, pf, p["w2p"], p["gla_b2"], p["gla_norm"], layer, batch, seq)
    o_c = _hgrn4_call(pb, pf, p["lb_logits"], p["hgrn_norm"], layer, batch, seq)
    return _merge_call(o_a, o_b, o_c, pb, h, p["wa"], p["wb"], p["wc"], p["wo"], p["ln1_g"], p["ln1_b"], layer)


def _ffn_block(h, p, layer):
    return _moe_call(h, p["wr_t"], p["rbias"], p["wg"], p["wu"], p["wd"], p["ln2_g"], p["ln2_b"], layer)


def kernel(x, ln0_g, ln0_b, w_in, gdn_conv, gdn_a_log, gdn_dt_bias, gdn_norm, gla_w2, gla_b2, gla_norm,
           hgrn_lb_logits, hgrn_norm, w_br_a, w_br_b, w_br_c, w_out, ln1_g, ln1_b, w_router, router_bias,
           w_gate, w_up, w_down, ln2_g, ln2_b):
    batch, seq, d = x.shape
    p = _prepare(w_in, gdn_conv, gdn_a_log, gdn_dt_bias, gdn_norm, gla_w2, gla_b2, gla_norm, hgrn_lb_logits,
                 hgrn_norm, w_br_a, w_br_b, w_br_c, w_out, ln1_g, ln1_b, w_router, router_bias, w_gate, w_up,
                 w_down, ln2_g, ln2_b)
    h, hb = _ln_call(x.reshape(batch * seq, d), ln0_g, ln0_b)
    for layer in range(w_in.shape[0]):
        h = _mixer_block(h, hb, p, layer, batch, seq)
        h, hb = _ffn_block(h, p, layer)
    return h.reshape(batch, seq, d)
```

```python
import functools

import numpy as np
import jax
import jax.numpy as jnp
from jax import lax
from jax.experimental import pallas as pl
from jax.experimental.pallas import tpu as pltpu

F32 = jnp.float32
BF16 = jnp.bfloat16
HIGHEST = lax.Precision.HIGHEST

D_MODEL = 1024
DEPTH = 4
CHUNK = 64
GDN_HEADS, GDN_DK, GDN_DV, CONV_W = 4, 128, 128, 4
GLA_HEADS, GLA_DK, GLA_DV, GLA_RANK, GLA_NORMALIZER = 4, 64, 128, 16, 16.0
HGRN_HEADS, HGRN_EXPAND, HGRN_DV = 4, 128, 128
LB_FLOOR = 1e-30
N_EXPERTS, N_GROUPS, TOP_K, D_FF = 16, 4, 2, 256
EXPERTS_PER_GROUP = N_EXPERTS // N_GROUPS
ALPHA = (2.0 * DEPTH) ** 0.25
LN_EPS = 1e-5
RMS_EPS = 1e-6

GDN_QK = GDN_HEADS * GDN_DK
GDN_V = GDN_HEADS * GDN_DV
GLA_QK = GLA_HEADS * GLA_DK
GLA_V = GLA_HEADS * GLA_DV
HGRN_QK = HGRN_HEADS * HGRN_EXPAND
HGRN_V = HGRN_HEADS * HGRN_DV
SPLIT_SIZES = (GDN_QK, GDN_QK, GDN_V, GDN_HEADS, GDN_HEADS, GDN_V,
               GLA_QK, GLA_QK, GLA_V, GLA_RANK, GLA_V,
               HGRN_QK, HGRN_QK, HGRN_V, HGRN_V,
               D_MODEL, D_MODEL, D_MODEL)
SPLIT_POINTS = tuple(int(v) for v in np.cumsum(SPLIT_SIZES)[:-1])

LANE = 128
SUBLANE = 8
VMEM_LIMIT = 48 * 1024 * 1024

M_A, M_B, M_C = 0, 1024, 2048
A_Q, A_K, A_V, A_G = 3072, 3584, 4096, 4608
B_Q, B_K, B_V, B_G = 5120, 5376, 5632, 6144
C_Q, C_I, C_G = 6656, 7168, 7680
NPB = 8192
F_CF, F_AS, F_LR = 0, 512, 640
NPF = 768

SUB = 8
DD_NPAR = 4
GDN_NPAR = 4
MOE_EPS = 4


def _cparams(sem):
    return pltpu.CompilerParams(dimension_semantics=sem, vmem_limit_bytes=VMEM_LIMIT)


def _mm(a, b):
    return jnp.dot(a.astype(BF16), b.astype(BF16), preferred_element_type=F32)


def _mm_nt(a, b):
    return lax.dot_general(a.astype(BF16), b.astype(BF16), (((1,), (1,)), ((), ())),
                           preferred_element_type=F32)


def _mm_tn(a, b):
    return lax.dot_general(a.astype(BF16), b.astype(BF16), (((0,), (0,)), ((), ())),
                           preferred_element_type=F32)


def _mm_nt_f32(a, b):
    return lax.dot_general(a, b, (((1,), (1,)), ((), ())), preferred_element_type=F32,
                           precision=HIGHEST)


def _split3(x):
    hi = x.astype(BF16)
    r = x - hi.astype(F32)
    mid = r.astype(BF16)
    lo = (r - mid.astype(F32)).astype(BF16)
    return hi, mid, lo


def _mm_01(m01, x):
    hi, mid, lo = _split3(x)
    return (jnp.dot(m01, hi, preferred_element_type=F32) + jnp.dot(m01, mid, preferred_element_type=F32)
            + jnp.dot(m01, lo, preferred_element_type=F32))


def _sigmoid(x):
    return 1.0 / (1.0 + jnp.exp(-x))


def _sigmoid_t(x):
    return 0.5 * jnp.tanh(0.5 * x) + 0.5


def _silu(x):
    return x * _sigmoid_t(x)


def _softplus(x):
    return jnp.maximum(x, 0.0) + jnp.log(1.0 + jnp.exp(-jnp.abs(x)))


def _log_sigmoid(x):
    return -_softplus(-x)


def _layer_norm(x, g, b):
    mu = jnp.mean(x, axis=-1, keepdims=True)
    xc = x - mu
    var = jnp.mean(xc * xc, axis=-1, keepdims=True)
    return xc * lax.rsqrt(var + LN_EPS) * g + b


def _ln_kernel(x_ref, g_ref, b_ref, o_ref, ob_ref):
    y = _layer_norm(x_ref[...], g_ref[...], b_ref[...])
    o_ref[...] = y
    ob_ref[...] = y.astype(BF16)


def _ln_call(x, g, b, tm=512):
    t, d = x.shape
    tm = min(tm, t)
    return pl.pallas_call(
        _ln_kernel,
        grid=(t // tm,),
        in_specs=[pl.BlockSpec((tm, d), lambda i: (i, 0)),
                  pl.BlockSpec((1, d), lambda i: (0, 0)),
                  pl.BlockSpec((1, d), lambda i: (0, 0))],
        out_specs=[pl.BlockSpec((tm, d), lambda i: (i, 0)), pl.BlockSpec((tm, d), lambda i: (i, 0))],
        out_shape=[jax.ShapeDtypeStruct((t, d), F32), jax.ShapeDtypeStruct((t, d), BF16)],
        compiler_params=_cparams(("parallel",)),
        name="ln0",
    )(x, g.reshape(1, d), b.reshape(1, d))


def _inproj_kernel(x_ref, w_ref, o_ref):
    o_ref[...] = jnp.dot(x_ref[...], w_ref[...], preferred_element_type=F32).astype(o_ref.dtype)


def _inproj_call(hb, w, layer, out_dtype, tm, tn, name):
    t, d = hb.shape
    tm = min(tm, t)
    n = w.shape[-1]
    return pl.pallas_call(
        _inproj_kernel,
        grid=(n // tn, t // tm),
        in_specs=[pl.BlockSpec((tm, d), lambda j, i: (i, 0)),
                  pl.BlockSpec((None, d, tn), lambda j, i: (layer, 0, j))],
        out_specs=pl.BlockSpec((tm, tn), lambda j, i: (i, j)),
        out_shape=jax.ShapeDtypeStruct((t, n), out_dtype),
        compiler_params=_cparams(("parallel", "parallel")),
        name=name,
    )(hb, w)


def _gdn_kernel(qkv_ref, s_ref, gate_ref, cw_ref, par_ref, nw_ref, o_ref,
                state_ref, tail_ref, xbuf_ref, q_s, k_s, v_s, cumb_s, betab_s, cumrow_s, o_s, *, tb):
    blk = pl.program_id(1)
    nc = tb // CHUNK
    nh = GDN_HEADS
    off0 = SUBLANE - (CONV_W - 1)

    @pl.when(blk == 0)
    def _():
        state_ref[...] = jnp.zeros_like(state_ref)
        tail_ref[...] = jnp.zeros_like(tail_ref)

    row = lax.broadcasted_iota(jnp.int32, (CHUNK, CHUNK), 0)
    col = lax.broadcasted_iota(jnp.int32, (CHUNK, CHUNK), 1)
    incl = col <= row
    strict = col < row
    tri = jnp.where(incl, 1.0, 0.0).astype(BF16)
    eye = jnp.where(col == row, 1.0, 0.0).astype(F32)
    npar = GDN_NPAR
    rp = npar * CHUNK
    xbuf_ref[0:SUBLANE, :] = tail_ref[...]

    def prologue(t):
        lo = t * rp
        xbuf_ref[SUBLANE + lo:SUBLANE + lo + rp, :] = qkv_ref[lo:lo + rp, :].astype(F32)
        for j in range(3 * nh):
            cs = slice(j * LANE, (j + 1) * LANE)
            y = xbuf_ref[off0 + lo:off0 + lo + rp, cs] * cw_ref[0:1, cs]
            for kk in range(1, CONV_W):
                y = y + xbuf_ref[off0 + kk + lo:off0 + kk + lo + rp, cs] * cw_ref[kk:kk + 1, cs]
            y = _silu(y)
            if j < nh:
                q_s[j, lo:lo + rp, :] = (y * lax.rsqrt(jnp.sum(y * y, axis=-1, keepdims=True) + RMS_EPS)
                                         * (GDN_DK ** -0.5))
            elif j < 2 * nh:
                k_s[j - nh, lo:lo + rp, :] = y * lax.rsqrt(jnp.sum(y * y, axis=-1, keepdims=True) + RMS_EPS)
            else:
                v_s[j - 2 * nh, lo:lo + rp, :] = y
        sc = s_ref[lo:lo + rp, :]
        beta_all = _sigmoid_t(sc)
        g_all = -jnp.exp(par_ref[0:1, :]) * _softplus(sc + par_ref[1:2, :])
        cum_all = jnp.concatenate([_mm_01(tri, g_all[c * CHUNK:(c + 1) * CHUNK, :]) for c in range(npar)], axis=0)
        cum_t = jnp.transpose(cum_all)
        for c in range(npar):
            cumrow_s[t * npar + c] = cum_t[0:SUBLANE, c * CHUNK:(c + 1) * CHUNK]
        for h in range(nh):
            cumb_s[h, lo:lo + rp, :] = jnp.broadcast_to(cum_all[:, nh + h:nh + h + 1], (rp, LANE))
            betab_s[h, lo:lo + rp, :] = jnp.broadcast_to(beta_all[:, h:h + 1], (rp, LANE))

    def trip(cp):
        chains = [(cp * npar + cc, h) for cc in range(npar) for h in range(nh)]
        r0s = [c * CHUNK for c, _ in chains]
        qc = [q_s[h, pl.ds(r0, CHUNK), :] for (_, h), r0 in zip(chains, r0s)]
        kc = [k_s[h, pl.ds(r0, CHUNK), :] for (_, h), r0 in zip(chains, r0s)]
        vc = [v_s[h, pl.ds(r0, CHUNK), :] for (_, h), r0 in zip(chains, r0s)]
        cumc = [cumb_s[h, pl.ds(r0, CHUNK), :] for (_, h), r0 in zip(chains, r0s)]
        bc = [betab_s[h, pl.ds(r0, CHUNK), :] for (_, h), r0 in zip(chains, r0s)]
        n = len(chains)
        kk = [_mm_nt(kc[i], kc[i]) for i in range(n)]
        qk = [_mm_nt(qc[i], kc[i]) for i in range(n)]
        decay = []
        for i, (c, h) in enumerate(chains):
            diff = cumc[i][:, 0:CHUNK] - cumrow_s[c][nh + h:nh + h + 1, :]
            decay.append(jnp.where(incl, jnp.exp(jnp.where(incl, diff, 0.0)), 0.0))
        a = [jnp.where(strict, bc[i][:, 0:CHUNK] * kk[i] * decay[i], 0.0) for i in range(n)]
        x = [eye - a[i] for i in range(n)]
        p = [_mm(a[i], a[i]) for i in range(n)]
        for it in range(5):
            x = [x[i] + _mm(x[i], p[i]) for i in range(n)]
            if it < 4:
                p = [_mm(p[i], p[i]) for i in range(n)]
        ecum = [jnp.exp(cumc[i]) for i in range(n)]
        sol = [_mm(x[i], jnp.concatenate([vc[i] * bc[i], kc[i] * (bc[i] * ecum[i])], axis=1)) for i in range(n)]
        attn = [qk[i] * decay[i] for i in range(n)]
        cum_last = [cumc[i][CHUNK - 1:CHUNK, :] for i in range(n)]
        k_state = [kc[i] * jnp.exp(cum_last[i] - cumc[i]) for i in range(n)]
        wqi = [jnp.concatenate([sol[i][:, GDN_DV:GDN_DV + GDN_DK], qc[i] * ecum[i]], axis=0) for i in range(n)]
        for cc in range(npar):
            idx = [cc * nh + h for h in range(nh)]
            s = [state_ref[h] for h in range(nh)]
            wq = [_mm(wqi[i], s[h]) for h, i in enumerate(idx)]
            v_new = [sol[i][:, 0:GDN_DV] - wq[h][0:CHUNK, :] for h, i in enumerate(idx)]
            av = [_mm(attn[i], v_new[h]) for h, i in enumerate(idx)]
            upd = [_mm_tn(k_state[i], v_new[h]) for h, i in enumerate(idx)]
            for h, i in enumerate(idx):
                state_ref[h] = s[h] * jnp.exp(cum_last[i]) + upd[h]
                o_s[pl.ds(r0s[i], CHUNK), h * LANE:(h + 1) * LANE] = wq[h][CHUNK:2 * CHUNK, :] + av[h]

    def epilogue(t):
        lo = t * rp
        for h in range(nh):
            hs = slice(h * LANE, (h + 1) * LANE)
            o = o_s[lo:lo + rp, hs]
            oh = o * lax.rsqrt(jnp.mean(o * o, axis=-1, keepdims=True) + RMS_EPS) * nw_ref[...]
            o_ref[lo:lo + rp, hs] = (oh * _silu(gate_ref[lo:lo + rp, hs].astype(F32))).astype(o_ref.dtype)

    nparts = nc // npar
    prologue(0)
    for t in range(nparts):
        if t + 1 < nparts:
            prologue(t + 1)
        trip(t)
        epilogue(t)
    tail_ref[...] = xbuf_ref[tb:tb + SUBLANE, :]


def _gdn_call(pb, pf, conv_w, par, norm_w, layer, batch, seq, tb=512):
    tb = min(tb, seq)
    nb = seq // tb
    nc = tb // CHUNK
    t = batch * seq
    wq = 2 * GDN_QK + GDN_V
    kern = functools.partial(_gdn_kernel, tb=tb)
    return pl.pallas_call(
        kern,
        grid=(batch, nb),
        in_specs=[pl.BlockSpec((tb, wq), lambda b, i: (b * nb + i, A_Q // wq)),
                  pl.BlockSpec((tb, LANE), lambda b, i: (b * nb + i, F_AS // LANE)),
                  pl.BlockSpec((tb, GDN_V), lambda b, i: (b * nb + i, A_G // GDN_V)),
                  pl.BlockSpec((None, CONV_W, wq), lambda b, i: (layer, 0, 0)),
                  pl.BlockSpec((None, SUBLANE, LANE), lambda b, i: (layer, 0, 0)),
                  pl.BlockSpec((None, 1, GDN_DV), lambda b, i: (layer, 0, 0))],
        out_specs=pl.BlockSpec((tb, GDN_V), lambda b, i: (b * nb + i, 0)),
        out_shape=jax.ShapeDtypeStruct((t, GDN_V), BF16),
        scratch_shapes=[pltpu.VMEM((GDN_HEADS, GDN_DK, GDN_DV), F32),
                        pltpu.VMEM((SUBLANE, wq), F32),
                        pltpu.VMEM((tb + SUBLANE, wq), F32),
                        pltpu.VMEM((GDN_HEADS, tb, LANE), F32),
                        pltpu.VMEM((GDN_HEADS, tb, LANE), F32),
                        pltpu.VMEM((GDN_HEADS, tb, LANE), F32),
                        pltpu.VMEM((GDN_HEADS, tb, LANE), F32),
                        pltpu.VMEM((GDN_HEADS, tb, LANE), F32),
                        pltpu.VMEM((nc, SUBLANE, CHUNK), F32),
                        pltpu.VMEM((tb, GDN_V), F32)],
        compiler_params=_cparams(("parallel", "arbitrary")),
        name="gdn",
    )(pb, pf, pb, conv_w, par, norm_w)


def _dd_core(q_s, k_s, v_s, la_s, o_s, state_ref, c8_s, p_s, *, tb, g_heads):
    nchunk = tb // CHUNK
    nblk = CHUNK // SUB
    dkh = LANE // g_heads
    dvp = g_heads * LANE
    row = lax.broadcasted_iota(jnp.int32, (CHUNK, CHUNK), 0)
    col = lax.broadcasted_iota(jnp.int32, (CHUNK, CHUNK), 1)
    level_masks = []
    for sh in (5, 4, 3):
        same2b = jnp.right_shift(row, sh + 1) == jnp.right_shift(col, sh + 1)
        upper = (jnp.right_shift(row, sh) & 1) == 1
        lower = (jnp.right_shift(col, sh) & 1) == 0
        level_masks.append(jnp.where(same2b, jnp.where(upper, jnp.where(lower, 1.0, 0.0), 0.0), 0.0))
    tri8 = jnp.where(jnp.right_shift(row, 3) == jnp.right_shift(col, 3),
                     jnp.where(col <= row, 1.0, 0.0), 0.0).astype(BF16)
    lane128 = lax.broadcasted_iota(jnp.int32, (CHUNK, LANE), 1)
    head_masks = [jnp.where((lane128 >= g * dkh) & (lane128 < (g + 1) * dkh), 1.0, 0.0)
                  for g in range(g_heads)]
    sub = lax.broadcasted_iota(jnp.int32, (SUB, LANE), 0)
    dk_sh = dkh.bit_length() - 1
    lane_sh = LANE.bit_length() - 1
    orow = lax.broadcasted_iota(jnp.int32, (LANE, dvp), 0)
    ocol = lax.broadcasted_iota(jnp.int32, (LANE, dvp), 1)
    ones_bd = jnp.where(jnp.right_shift(orow, dk_sh) == jnp.right_shift(ocol, lane_sh), 1.0, 0.0).astype(BF16)
    srow = lax.broadcasted_iota(jnp.int32, (dvp, LANE), 0)
    scol = lax.broadcasted_iota(jnp.int32, (dvp, LANE), 1)
    state_mask = jnp.where(jnp.right_shift(srow, lane_sh) == jnp.right_shift(scol, dk_sh), 1.0, 0.0)
    npar = DD_NPAR

    def block_sums(i):
        c8 = [c8_s[i, b * SUB:(b + 1) * SUB, :] for b in range(nblk)]
        t8 = [c8_s[i, (b + 1) * SUB - 1:(b + 1) * SUB, :] for b in range(nblk)]
        t16 = [t8[2 * b] + t8[2 * b + 1] for b in range(nblk // 2)]
        t32 = [t16[2 * b] + t16[2 * b + 1] for b in range(nblk // 4)]
        t64 = t32[0] + t32[1]
        c16 = [c8[b] + t8[b - 1] if b % 2 else c8[b] for b in range(nblk)]
        c32 = [c16[b] + t16[b // 2 - 1] if (b // 2) % 2 else c16[b] for b in range(nblk)]
        c64 = [c32[b] + t32[0] if b >= nblk // 2 else c32[b] for b in range(nblk)]
        pre = {8: c8, 16: c16, 32: c32, 64: c64}
        suf = {8: [t8[b] - c8[b] for b in range(nblk)],
               16: [t16[b // 2] - c16[b] for b in range(nblk)],
               32: [t32[b // 4] - c32[b] for b in range(nblk)],
               64: [t64 - c64[b] for b in range(nblk)]}
        return pre, suf

    def cat(pieces):
        return jnp.concatenate(pieces, axis=0)

    def group_body(cg, carry):
        rng = range(npar)
        r0s = [pl.multiple_of((cg * npar + i) * CHUNK, CHUNK) for i in rng]
        qc = [q_s[pl.ds(r0, CHUNK), :] for r0 in r0s]
        kc = [k_s[pl.ds(r0, CHUNK), :] for r0 in r0s]
        vc = [v_s[pl.ds(r0, CHUNK), :] for r0 in r0s]
        c8_all = _mm_01(tri8, jnp.concatenate([la_s[pl.ds(r0, CHUNK), :] for r0 in r0s], axis=1))
        for i in rng:
            c8_s[i] = c8_all[:, i * LANE:(i + 1) * LANE]
        sums = [block_sums(i) for i in rng]
        attn = [[None] * g_heads for _ in rng]
        for li, b in enumerate((32, 16, 8)):
            qs = [qc[i] * jnp.exp(cat(sums[i][0][b])) for i in rng]
            ks = [kc[i] * jnp.exp(cat(sums[i][1][b])) for i in rng]
            for g in range(g_heads):
                for i in rng:
                    qg = qs[i] * head_masks[g] if g_heads > 1 else qs[i]
                    term = _mm_nt(qg, ks[i]) * level_masks[li]
                    attn[i][g] = term if li == 0 else attn[i][g] + term
        o = [jnp.concatenate([_mm(attn[i][g], vc[i][:, g * LANE:(g + 1) * LANE]) for g in range(g_heads)], axis=1)
             if g_heads > 1 else _mm(attn[i][0], vc[i]) for i in rng]
        for i in rng:
            c8 = sums[i][0][8]
            for r in range(nblk):
                qr = qc[i][r * SUB:(r + 1) * SUB, :]
                for jj in range(SUB):
                    krow = k_s[pl.ds(r0s[i] + r * SUB + jj, 1), :]
                    crow = c8_s[i, r * SUB + jj:r * SUB + jj + 1, :]
                    m = sub >= jj
                    pr = qr * krow * jnp.exp(jnp.where(m, c8[r] - crow, -jnp.inf))
                    p_s[i, (r * SUB + jj) * SUB:(r * SUB + jj + 1) * SUB, :] = pr
        rs = [jnp.dot(p_s[i].astype(BF16), ones_bd, preferred_element_type=F32) for i in rng]
        for i in rng:
            od = []
            for r in range(nblk):
                acc = None
                for jj in range(SUB):
                    vrow = v_s[pl.ds(r0s[i] + r * SUB + jj, 1), :]
                    term = rs[i][(r * SUB + jj) * SUB:(r * SUB + jj + 1) * SUB, :] * vrow
                    acc = term if acc is None else acc + term
                od.append(acc)
            o[i] = o[i] + cat(od)
        q_inter = [qc[i] * jnp.exp(cat(sums[i][0][64])) for i in rng]
        k_state = [kc[i] * jnp.exp(cat(sums[i][1][64])) for i in rng]
        upd = [_mm_tn(vc[i], k_state[i]) for i in rng]
        st = state_ref[...]
        for i in rng:
            o_s[pl.ds(r0s[i], CHUNK), :] = o[i] + _mm_nt(q_inter[i], st)
            decay_last = jnp.exp(sums[i][0][64][nblk - 1][SUB - 1:SUB, :])
            st = st * decay_last + (upd[i] * state_mask if g_heads > 1 else upd[i])
        state_ref[...] = st
        return carry

    lax.fori_loop(0, nchunk // npar, group_body, 0)


def _gated_rms_out(o_s, gate_ref, nw_ref, o_ref, g_heads):
    for g in range(g_heads):
        hs = slice(g * LANE, (g + 1) * LANE)
        o = o_s[:, hs]
        oh = o * lax.rsqrt(jnp.mean(o * o, axis=-1, keepdims=True) + RMS_EPS) * nw_ref[...]
        o_ref[:, hs] = (oh * _silu(gate_ref[:, hs].astype(F32))).astype(o_ref.dtype)


def _gla_kernel(q_ref, k_ref, v_ref, lr_ref, gate_ref, w2_ref, b2_ref, nw_ref,
                o_ref, state_ref, q_s, k_s, v_s, la_s, o_s, c8_s, p_s, *, tb):
    @pl.when(pl.program_id(2) == 0)
    def _():
        state_ref[...] = jnp.zeros_like(state_ref)

    q_s[...] = q_ref[...].astype(F32) * (GLA_DK ** -0.5)
    k_s[...] = k_ref[...].astype(F32)
    v_s[...] = v_ref[...].astype(F32)
    z = _mm(lr_ref[...], w2_ref[...]) + b2_ref[...]
    la_s[...] = _log_sigmoid(z) * (1.0 / GLA_NORMALIZER)
    _dd_core(q_s, k_s, v_s, la_s, o_s, state_ref, c8_s, p_s, tb=tb, g_heads=2)
    _gated_rms_out(o_s, gate_ref, nw_ref, o_ref, 2)


def _dd_scratch(tb, dvp):
    return [pltpu.VMEM((dvp, LANE), F32),
            pltpu.VMEM((tb, LANE), F32), pltpu.VMEM((tb, LANE), F32),
            pltpu.VMEM((tb, dvp), F32),
            pltpu.VMEM((tb, LANE), F32),
            pltpu.VMEM((tb, dvp), F32),
            pltpu.VMEM((DD_NPAR, CHUNK, LANE), F32),
            pltpu.VMEM((DD_NPAR, CHUNK * SUB, LANE), F32)]


def _gla_call(pb, pf, w2p, b2, norm_w, layer, batch, seq, tb=512):
    tb = min(tb, seq)
    nb = seq // tb
    t = batch * seq
    npair = GLA_HEADS // 2
    kern = functools.partial(_gla_kernel, tb=tb)
    return pl.pallas_call(
        kern,
        grid=(batch, npair, nb),
        in_specs=[pl.BlockSpec((tb, LANE), lambda b, p, i: (b * nb + i, B_Q // LANE + p)),
                  pl.BlockSpec((tb, LANE), lambda b, p, i: (b * nb + i, B_K // LANE + p)),
                  pl.BlockSpec((tb, 2 * LANE), lambda b, p, i: (b * nb + i, B_V // (2 * LANE) + p)),
                  pl.BlockSpec((tb, LANE), lambda b, p, i: (b * nb + i, F_LR // LANE)),
                  pl.BlockSpec((tb, 2 * LANE), lambda b, p, i: (b * nb + i, B_G // (2 * LANE) + p)),
                  pl.BlockSpec((None, LANE, LANE), lambda b, p, i: (layer, 0, p)),
                  pl.BlockSpec((None, 1, LANE), lambda b, p, i: (layer, 0, p)),
                  pl.BlockSpec((None, 1, GLA_DV), lambda b, p, i: (layer, 0, 0))],
        out_specs=pl.BlockSpec((tb, 2 * LANE), lambda b, p, i: (b * nb + i, p)),
        out_shape=jax.ShapeDtypeStruct((t, GLA_V), BF16),
        scratch_shapes=_dd_scratch(tb, 2 * LANE),
        compiler_params=_cparams(("parallel", "parallel", "arbitrary")),
        name="gla",
    )(pb, pb, pb, pf, pb, w2p, b2, norm_w)


def _hgrn_kernel(q_ref, f_ref, v_ref, gate_ref, lbl_ref, nw_ref,
                 o_ref, state_ref, q_s, k_s, v_s, la_s, o_s, c8_s, p_s, *, tb, layer):
    @pl.when(pl.program_id(2) == 0)
    def _():
        state_ref[...] = jnp.zeros_like(state_ref)

    logits = lbl_ref[...]
    mx = jnp.max(logits, axis=0, keepdims=True)
    ex = jnp.exp(logits - mx)
    p = ex / jnp.sum(ex, axis=0, keepdims=True)
    acc = p[0:1, :]
    for r in range(1, layer + 1):
        acc = acc + p[r:r + 1, :]
    lb = jnp.clip(acc - p[0:1, :], 0.0, 1.0)
    log_lb = jnp.log(jnp.maximum(lb, LB_FLOOR))
    log_1m = jnp.log(1.0 - lb)

    cf = f_ref[...]
    second = log_1m + _log_sigmoid(cf)
    la_s[...] = jnp.maximum(log_lb, second) + jnp.log(1.0 + jnp.exp(-jnp.abs(log_lb - second)))
    k_s[...] = (1.0 - lb) * _sigmoid_t(-cf)
    q_s[...] = _silu(q_ref[...].astype(F32)) * (HGRN_EXPAND ** -0.5)
    v_s[...] = v_ref[...].astype(F32)
    _dd_core(q_s, k_s, v_s, la_s, o_s, state_ref, c8_s, p_s, tb=tb, g_heads=1)
    _gated_rms_out(o_s, gate_ref, nw_ref, o_ref, 1)


def _hgrn_call(pb, pf, lb_logits, norm_w, layer, batch, seq, tb=512):
    tb = min(tb, seq)
    nb = seq // tb
    t = batch * seq

    def tok(colblk):
        return pl.BlockSpec((tb, LANE), lambda b, h, i: (b * nb + i, colblk + h))

    kern = functools.partial(_hgrn_kernel, tb=tb, layer=layer)
    return pl.pallas_call(
        kern,
        grid=(batch, HGRN_HEADS, nb),
        in_specs=[tok(C_Q // LANE), tok(F_CF // LANE), tok(C_I // LANE), tok(C_G // LANE),
                  pl.BlockSpec((DEPTH, LANE), lambda b, h, i: (0, h)),
                  pl.BlockSpec((None, 1, HGRN_DV), lambda b, h, i: (layer, 0, 0))],
        out_specs=pl.BlockSpec((tb, LANE), lambda b, h, i: (b * nb + i, h)),
        out_shape=jax.ShapeDtypeStruct((t, HGRN_V), BF16),
        scratch_shapes=_dd_scratch(tb, LANE),
        compiler_params=_cparams(("parallel", "parallel", "arbitrary")),
        name="hgrn",
    )(pb, pf, pb, pb, lb_logits, norm_w)


def _dd_make_trip(q_s, k_s, v_s, la_s, o_s, state_ref, c8_s, p_s, g_heads):
    nblk = CHUNK // SUB
    dkh = LANE // g_heads
    dvp = g_heads * LANE
    npar = DD_NPAR
    row = lax.broadcasted_iota(jnp.int32, (CHUNK, CHUNK), 0)
    col = lax.broadcasted_iota(jnp.int32, (CHUNK, CHUNK), 1)
    level_masks = []
    for sh in (5, 4, 3):
        same2b = jnp.right_shift(row, sh + 1) == jnp.right_shift(col, sh + 1)
        upper = (jnp.right_shift(row, sh) & 1) == 1
        lower = (jnp.right_shift(col, sh) & 1) == 0
        level_masks.append(jnp.where(same2b, jnp.where(upper, jnp.where(lower, 1.0, 0.0), 0.0), 0.0))
    tri8 = jnp.where(jnp.right_shift(row, 3) == jnp.right_shift(col, 3),
                     jnp.where(col <= row, 1.0, 0.0), 0.0).astype(BF16)
    lane128 = lax.broadcasted_iota(jnp.int32, (CHUNK, LANE), 1)
    head_masks = [jnp.where((lane128 >= g * dkh) & (lane128 < (g + 1) * dkh), 1.0, 0.0)
                  for g in range(g_heads)]
    sub = lax.broadcasted_iota(jnp.int32, (SUB, LANE), 0)
    dk_sh = dkh.bit_length() - 1
    lane_sh = LANE.bit_length() - 1
    orow = lax.broadcasted_iota(jnp.int32, (LANE, dvp), 0)
    ocol = lax.broadcasted_iota(jnp.int32, (LANE, dvp), 1)
    ones_bd = jnp.where(jnp.right_shift(orow, dk_sh) == jnp.right_shift(ocol, lane_sh), 1.0, 0.0).astype(BF16)
    srow = lax.broadcasted_iota(jnp.int32, (dvp, LANE), 0)
    scol = lax.broadcasted_iota(jnp.int32, (dvp, LANE), 1)
    state_mask = jnp.where(jnp.right_shift(srow, lane_sh) == jnp.right_shift(scol, dk_sh), 1.0, 0.0)

    def block_sums(ci):
        c8 = [c8_s[ci, b * SUB:(b + 1) * SUB, :] for b in range(nblk)]
        t8 = [c8_s[ci, (b + 1) * SUB - 1:(b + 1) * SUB, :] for b in range(nblk)]
        t16 = [t8[2 * b] + t8[2 * b + 1] for b in range(nblk // 2)]
        t32 = [t16[2 * b] + t16[2 * b + 1] for b in range(nblk // 4)]
        t64 = t32[0] + t32[1]
        c16 = [c8[b] + t8[b - 1] if b % 2 else c8[b] for b in range(nblk)]
        c32 = [c16[b] + t16[b // 2 - 1] if (b // 2) % 2 else c16[b] for b in range(nblk)]
        c64 = [c32[b] + t32[0] if b >= nblk // 2 else c32[b] for b in range(nblk)]
        pre = {8: c8, 16: c16, 32: c32, 64: c64}
        suf = {8: [t8[b] - c8[b] for b in range(nblk)],
               16: [t16[b // 2] - c16[b] for b in range(nblk)],
               32: [t32[b // 4] - c32[b] for b in range(nblk)],
               64: [t64 - c64[b] for b in range(nblk)]}
        return pre, suf

    def cat(pieces):
        return jnp.concatenate(pieces, axis=0)

    def trip(u, t, slot):
        rng = range(npar)
        r0s = [(t * npar + i) * CHUNK for i in rng]
        cis = [slot * npar + i for i in rng]
        qc = [q_s[u, r0:r0 + CHUNK, :] for r0 in r0s]
        kc = [k_s[u, r0:r0 + CHUNK, :] for r0 in r0s]
        vc = [v_s[u, r0:r0 + CHUNK, :] for r0 in r0s]
        c8_all = _mm_01(tri8, jnp.concatenate([la_s[u, r0:r0 + CHUNK, :] for r0 in r0s], axis=1))
        for i in rng:
            c8_s[cis[i]] = c8_all[:, i * LANE:(i + 1) * LANE]
        sums = [block_sums(cis[i]) for i in rng]
        attn = [[None] * g_heads for _ in rng]
        for li, b in enumerate((32, 16, 8)):
            qs = [qc[i] * jnp.exp(cat(sums[i][0][b])) for i in rng]
            ks = [kc[i] * jnp.exp(cat(sums[i][1][b])) for i in rng]
            for g in range(g_heads):
                for i in rng:
                    qg = qs[i] * head_masks[g] if g_heads > 1 else qs[i]
                    term = _mm_nt(qg, ks[i]) * level_masks[li]
                    attn[i][g] = term if li == 0 else attn[i][g] + term
        o = [jnp.concatenate([_mm(attn[i][g], vc[i][:, g * LANE:(g + 1) * LANE]) for g in range(g_heads)], axis=1)
             if g_heads > 1 else _mm(attn[i][0], vc[i]) for i in rng]
        for i in rng:
            c8 = sums[i][0][8]
            for r in range(nblk):
                qr = qc[i][r * SUB:(r + 1) * SUB, :]
                for jj in range(SUB):
                    rr = r0s[i] + r * SUB + jj
                    krow = k_s[u, rr:rr + 1, :]
                    crow = c8_s[cis[i], r * SUB + jj:r * SUB + jj + 1, :]
                    pr = qr * krow * jnp.exp(jnp.where(sub >= jj, c8[r] - crow, -jnp.inf))
                    p_s[cis[i], (r * SUB + jj) * SUB:(r * SUB + jj + 1) * SUB, :] = pr
        rs = [jnp.dot(p_s[cis[i]].astype(BF16), ones_bd, preferred_element_type=F32) for i in rng]
        for i in rng:
            od = []
            for r in range(nblk):
                acc = None
                for jj in range(SUB):
                    rr = r0s[i] + r * SUB + jj
                    term = rs[i][(r * SUB + jj) * SUB:(r * SUB + jj + 1) * SUB, :] * v_s[u, rr:rr + 1, :]
                    acc = term if acc is None else acc + term
                od.append(acc)
            o[i] = o[i] + cat(od)
        q_inter = [qc[i] * jnp.exp(cat(sums[i][0][64])) for i in rng]
        k_state = [kc[i] * jnp.exp(cat(sums[i][1][64])) for i in rng]
        upd = [_mm_tn(vc[i], k_state[i]) for i in rng]
        st = state_ref[u]
        for i in rng:
            o_s[u, r0s[i]:r0s[i] + CHUNK, :] = o[i] + _mm_nt(q_inter[i], st)
            decay_last = jnp.exp(sums[i][0][64][nblk - 1][SUB - 1:SUB, :])
            st = st * decay_last + (upd[i] * state_mask if g_heads > 1 else upd[i])
        state_ref[u] = st

    return trip


def _dd_schedule(units, prologue, trip, epilogue):
    prologue(*units[0])
    for n, (u, t) in enumerate(units):
        if n + 1 < len(units):
            prologue(*units[n + 1])
        trip(u, t, n % 2)
        epilogue(u, t)


def _dd_scratch4(nu, tb, dvp):
    return [pltpu.VMEM((nu, dvp, LANE), F32),
            pltpu.VMEM((nu, tb, LANE), F32), pltpu.VMEM((nu, tb, LANE), F32),
            pltpu.VMEM((nu, tb, dvp), F32),
            pltpu.VMEM((nu, tb, LANE), F32),
            pltpu.VMEM((nu, tb, dvp), F32),
            pltpu.VMEM((2 * DD_NPAR, CHUNK, LANE), F32),
            pltpu.VMEM((2 * DD_NPAR, CHUNK * SUB, LANE), F32)]


def _gla4_kernel(q_ref, k_ref, v_ref, lr_ref, gate_ref, w2_ref, b2_ref, nw_ref,
                 o_ref, state_ref, q_s, k_s, v_s, la_s, o_s, c8_s, p_s, *, tb):
    @pl.when(pl.program_id(1) == 0)
    def _():
        state_ref[...] = jnp.zeros_like(state_ref)

    npair = GLA_HEADS // 2
    rp = DD_NPAR * CHUNK
    trip = _dd_make_trip(q_s, k_s, v_s, la_s, o_s, state_ref, c8_s, p_s, 2)

    def prologue(u, t):
        lo = t * rp
        ls = slice(u * LANE, (u + 1) * LANE)
        vs = slice(u * 2 * LANE, (u + 1) * 2 * LANE)
        q_s[u, lo:lo + rp, :] = q_ref[lo:lo + rp, ls].astype(F32) * (GLA_DK ** -0.5)
        k_s[u, lo:lo + rp, :] = k_ref[lo:lo + rp, ls].astype(F32)
        v_s[u, lo:lo + rp, :] = v_ref[lo:lo + rp, vs].astype(F32)
        z = _mm(lr_ref[lo:lo + rp, :], w2_ref[:, ls]) + b2_ref[:, ls]
        la_s[u, lo:lo + rp, :] = _log_sigmoid(z) * (1.0 / GLA_NORMALIZER)

    def epilogue(u, t):
        lo = t * rp
        for g in range(2):
            hs = slice((2 * u + g) * LANE, (2 * u + g + 1) * LANE)
            o = o_s[u, lo:lo + rp, g * LANE:(g + 1) * LANE]
            oh = o * lax.rsqrt(jnp.mean(o * o, axis=-1, keepdims=True) + RMS_EPS) * nw_ref[...]
            o_ref[lo:lo + rp, hs] = (oh * _silu(gate_ref[lo:lo + rp, hs].astype(F32))).astype(o_ref.dtype)

    _dd_schedule([(u, t) for u in range(npair) for t in range(tb // rp)], prologue, trip, epilogue)


def _gla4_call(pb, pf, w2p, b2, norm_w, layer, batch, seq, tb=512):
    tb = min(tb, seq)
    nb = seq // tb
    t = batch * seq
    kern = functools.partial(_gla4_kernel, tb=tb)
    return pl.pallas_call(
        kern,
        grid=(batch, nb),
        in_specs=[pl.BlockSpec((tb, GLA_QK), lambda b, i: (b * nb + i, B_Q // GLA_QK)),
                  pl.BlockSpec((tb, GLA_QK), lambda b, i: (b * nb + i, B_K // GLA_QK)),
                  pl.BlockSpec((tb, GLA_V), lambda b, i: (b * nb + i, B_V // GLA_V)),
                  pl.BlockSpec((tb, LANE), lambda b, i: (b * nb + i, F_LR // LANE)),
                  pl.BlockSpec((tb, GLA_V), lambda b, i: (b * nb + i, B_G // GLA_V)),
                  pl.BlockSpec((None, LANE, GLA_QK), lambda b, i: (layer, 0, 0)),
                  pl.BlockSpec((None, 1, GLA_QK), lambda b, i: (layer, 0, 0)),
                  pl.BlockSpec((None, 1, GLA_DV), lambda b, i: (layer, 0, 0))],
        out_specs=pl.BlockSpec((tb, GLA_V), lambda b, i: (b * nb + i, 0)),
        out_shape=jax.ShapeDtypeStruct((t, GLA_V), BF16),
        scratch_shapes=_dd_scratch4(GLA_HEADS // 2, tb, 2 * LANE),
        compiler_params=_cparams(("parallel", "arbitrary")),
        name="gla",
    )(pb, pb, pb, pf, pb, w2p, b2, norm_w)


def _hgrn4_kernel(q_ref, f_ref, v_ref, gate_ref, lbl_ref, nw_ref,
                  o_ref, state_ref, q_s, k_s, v_s, la_s, o_s, c8_s, p_s, *, tb, layer):
    @pl.when(pl.program_id(1) == 0)
    def _():
        state_ref[...] = jnp.zeros_like(state_ref)

    logits = lbl_ref[...]
    mx = jnp.max(logits, axis=0, keepdims=True)
    ex = jnp.exp(logits - mx)
    p = ex / jnp.sum(ex, axis=0, keepdims=True)
    acc = p[0:1, :]
    for r in range(1, layer + 1):
        acc = acc + p[r:r + 1, :]
    lb = jnp.clip(acc - p[0:1, :], 0.0, 1.0)
    log_lb = jnp.log(jnp.maximum(lb, LB_FLOOR))
    log_1m = jnp.log(1.0 - lb)

    rp = DD_NPAR * CHUNK
    trip = _dd_make_trip(q_s, k_s, v_s, la_s, o_s, state_ref, c8_s, p_s, 1)

    def prologue(u, t):
        lo = t * rp
        hs = slice(u * LANE, (u + 1) * LANE)
        cf = f_ref[lo:lo + rp, hs]
        second = log_1m[:, hs] + _log_sigmoid(cf)
        llb = log_lb[:, hs]
        la_s[u, lo:lo + rp, :] = jnp.maximum(llb, second) + jnp.log(1.0 + jnp.exp(-jnp.abs(llb - second)))
        k_s[u, lo:lo + rp, :] = (1.0 - lb[:, hs]) * _sigmoid_t(-cf)
        q_s[u, lo:lo + rp, :] = _silu(q_ref[lo:lo + rp, hs].astype(F32)) * (HGRN_EXPAND ** -0.5)
        v_s[u, lo:lo + rp, :] = v_ref[lo:lo + rp, hs].astype(F32)

    def epilogue(u, t):
        lo = t * rp
        hs = slice(u * LANE, (u + 1) * LANE)
        o = o_s[u, lo:lo + rp, :]
        oh = o * lax.rsqrt(jnp.mean(o * o, axis=-1, keepdims=True) + RMS_EPS) * nw_ref[...]
        o_ref[lo:lo + rp, hs] = (oh * _silu(gate_ref[lo:lo + rp, hs].astype(F32))).astype(o_ref.dtype)

    _dd_schedule([(u, t) for u in range(HGRN_HEADS) for t in range(tb // rp)], prologue, trip, epilogue)


def _hgrn4_call(pb, pf, lb_logits, norm_w, layer, batch, seq, tb=512):
    tb = min(tb, seq)
    nb = seq // tb
    t = batch * seq
    kern = functools.partial(_hgrn4_kernel, tb=tb, layer=layer)
    return pl.pallas_call(
        kern,
        grid=(batch, nb),
        in_specs=[pl.BlockSpec((tb, HGRN_QK), lambda b, i: (b * nb + i, C_Q // HGRN_QK)),
                  pl.BlockSpec((tb, HGRN_QK), lambda b, i: (b * nb + i, F_CF // HGRN_QK)),
                  pl.BlockSpec((tb, HGRN_V), lambda b, i: (b * nb + i, C_I // HGRN_V)),
                  pl.BlockSpec((tb, HGRN_V), lambda b, i: (b * nb + i, C_G // HGRN_V)),
                  pl.BlockSpec((DEPTH, HGRN_QK), lambda b, i: (0, 0)),
                  pl.BlockSpec((None, 1, HGRN_DV), lambda b, i: (layer, 0, 0))],
        out_specs=pl.BlockSpec((tb, HGRN_V), lambda b, i: (b * nb + i, 0)),
        out_shape=jax.ShapeDtypeStruct((t, HGRN_V), BF16),
        scratch_shapes=_dd_scratch4(HGRN_HEADS, tb, LANE),
        compiler_params=_cparams(("parallel", "arbitrary")),
        name="hgrn",
    )(pb, pf, pb, pb, lb_logits, norm_w)


def _merge_kernel(oa_ref, ob_ref, oc_ref, ma_ref, mb_ref, mc_ref, h_ref,
                  wa_ref, wb_ref, wc_ref, wo_ref, g_ref, b_ref, o_ref):
    y = (_sigmoid_t(ma_ref[...].astype(F32)) * jnp.dot(oa_ref[...], wa_ref[...], preferred_element_type=F32)
         + _sigmoid_t(mb_ref[...].astype(F32)) * jnp.dot(ob_ref[...], wb_ref[...], preferred_element_type=F32)
         + _sigmoid_t(mc_ref[...].astype(F32)) * jnp.dot(oc_ref[...], wc_ref[...], preferred_element_type=F32))
    mix = _mm(y, wo_ref[...])
    o_ref[...] = _layer_norm(ALPHA * h_ref[...] + mix, g_ref[...], b_ref[...])


def _merge_call(o_a, o_b, o_c, pb, h, wa, wb, wc, wo, g, b, layer, tm=512):
    t, d = h.shape
    tm = min(tm, t)

    def row(width):
        return pl.BlockSpec((tm, width), lambda i: (i, 0))

    def wspec(kdim):
        return pl.BlockSpec((None, kdim, d), lambda i: (layer, 0, 0))

    vec = pl.BlockSpec((None, 1, d), lambda i: (layer, 0, 0))
    return pl.pallas_call(
        _merge_kernel,
        grid=(t // tm,),
        in_specs=[row(GDN_V), row(GLA_V), row(HGRN_V),
                  pl.BlockSpec((tm, d), lambda i: (i, M_A // d)),
                  pl.BlockSpec((tm, d), lambda i: (i, M_B // d)),
                  pl.BlockSpec((tm, d), lambda i: (i, M_C // d)),
                  row(d), wspec(GDN_V), wspec(GLA_V), wspec(HGRN_V), wspec(d), vec, vec],
        out_specs=row(d),
        out_shape=jax.ShapeDtypeStruct((t, d), F32),
        compiler_params=_cparams(("parallel",)),
        name="merge",
    )(o_a, o_b, o_c, pb, pb, pb, h, wa, wb, wc, wo, g, b)


def _lane_col(x, idx):
    lane = lax.broadcasted_iota(jnp.int32, x.shape, 1)
    return jnp.sum(jnp.where(lane == idx, x, 0.0), axis=1, keepdims=True)


def _route(scores_t, bias_ref):
    s = [scores_t[e:e + 1, :] for e in range(N_EXPERTS)]
    sel = [s[e] + bias_ref[e:e + 1, 0:1] for e in range(N_EXPERTS)]
    gscore = []
    for g in range(N_GROUPS):
        a, b, c, d = sel[4 * g:4 * g + 4]
        hi1, lo1 = jnp.maximum(a, b), jnp.minimum(a, b)
        hi2, lo2 = jnp.maximum(c, d), jnp.minimum(c, d)
        top1 = jnp.maximum(hi1, hi2)
        top2 = jnp.maximum(jnp.minimum(hi1, hi2), jnp.maximum(lo1, lo2))
        gscore.append(top1 + top2)
    best = gscore[0]
    gidx = jnp.zeros_like(best, dtype=jnp.int32)
    for g in range(1, N_GROUPS):
        take = gscore[g] > best
        best = jnp.where(take, gscore[g], best)
        gidx = jnp.where(take, g, gidx)
    ing, raw = [], []
    for kk in range(EXPERTS_PER_GROUP):
        vs, vr = sel[kk], s[kk]
        for g in range(1, N_GROUPS):
            pick = gidx == g
            vs = jnp.where(pick, sel[4 * g + kk], vs)
            vr = jnp.where(pick, s[4 * g + kk], vr)
        ing.append(vs)
        raw.append(vr)
    b1 = ing[0]
    i1 = jnp.zeros_like(gidx)
    for kk in range(1, EXPERTS_PER_GROUP):
        take = ing[kk] > b1
        b1 = jnp.where(take, ing[kk], b1)
        i1 = jnp.where(take, kk, i1)
    neg = jnp.full_like(b1, -jnp.inf)
    b2 = neg
    i2 = jnp.zeros_like(gidx)
    for kk in range(EXPERTS_PER_GROUP):
        cand = jnp.where(i1 == kk, neg, ing[kk])
        take = cand > b2
        b2 = jnp.where(take, cand, b2)
        i2 = jnp.where(take, kk, i2)
    w1 = raw[0]
    w2 = raw[0]
    for kk in range(1, EXPERTS_PER_GROUP):
        w1 = jnp.where(i1 == kk, raw[kk], w1)
        w2 = jnp.where(i2 == kk, raw[kk], w2)
    tot = w1 + w2
    w1 = w1 / tot
    w2 = w2 / tot
    e1 = gidx * EXPERTS_PER_GROUP + i1
    e2 = gidx * EXPERTS_PER_GROUP + i2
    rows = [jnp.where(e1 == e, w1, 0.0) + jnp.where(e2 == e, w2, 0.0) for e in range(N_EXPERTS)]
    return jnp.concatenate(rows, axis=0)


def _moe_kernel(h_ref, wr_ref, rb_ref, wg_ref, wu_ref, wd_ref, g_ref, b_ref, o_ref, ob_ref,
                comb_ref, xb_ref, *, tm):
    eg = pl.program_id(1)

    @pl.when(eg == 0)
    def _():
        logits_t = _mm_nt_f32(wr_ref[...], h_ref[...])
        comb_t = _route(_sigmoid(logits_t), rb_ref)
        pad = jnp.zeros((LANE - N_EXPERTS, tm), F32)
        comb_ref[...] = jnp.transpose(jnp.concatenate([comb_t, pad], axis=0))
        xb_ref[...] = h_ref[...].astype(BF16)

    x = xb_ref[...]
    comb = comb_ref[...]
    y = None
    for kk in range(MOE_EPS):
        hg = jnp.dot(x, wg_ref[kk], preferred_element_type=F32)
        hu = jnp.dot(x, wu_ref[kk], preferred_element_type=F32)
        cw = _lane_col(comb, eg * MOE_EPS + kk)
        hid = _silu(hg) * hu * cw
        term = jnp.dot(hid.astype(BF16), wd_ref[kk], preferred_element_type=F32)
        y = term if y is None else y + term

    @pl.when(eg == 0)
    def _():
        o_ref[...] = y

    @pl.when(eg > 0)
    def _():
        o_ref[...] += y

    @pl.when(eg == N_EXPERTS // MOE_EPS - 1)
    def _():
        out = _layer_norm(ALPHA * h_ref[...] + o_ref[...], g_ref[...], b_ref[...])
        o_ref[...] = out
        ob_ref[...] = out.astype(BF16)


def _moe_call(h, wr_t, rbias, wg, wu, wd, g, b, layer, tm=1024):
    t, d = h.shape
    tm = min(tm, t)
    kern = functools.partial(_moe_kernel, tm=tm)
    vec = pl.BlockSpec((None, 1, d), lambda i, e: (layer, 0, 0))
    row = pl.BlockSpec((tm, d), lambda i, e: (i, 0))
    return pl.pallas_call(
        kern,
        grid=(t // tm, N_EXPERTS // MOE_EPS),
        in_specs=[row,
                  pl.BlockSpec((N_EXPERTS, d), lambda i, e: (0, 0)),
                  pl.BlockSpec((N_EXPERTS, LANE), lambda i, e: (0, 0)),
                  pl.BlockSpec((None, MOE_EPS, d, D_FF), lambda i, e: (layer, e, 0, 0)),
                  pl.BlockSpec((None, MOE_EPS, d, D_FF), lambda i, e: (layer, e, 0, 0)),
                  pl.BlockSpec((None, MOE_EPS, D_FF, d), lambda i, e: (layer, e, 0, 0)),
                  vec, vec],
        out_specs=[row, row],
        out_shape=[jax.ShapeDtypeStruct((t, d), F32), jax.ShapeDtypeStruct((t, d), BF16)],
        scratch_shapes=[pltpu.VMEM((tm, LANE), F32), pltpu.VMEM((tm, d), BF16)],
        compiler_params=_cparams(("parallel", "arbitrary")),
        name="moe",
    )(h, wr_t, rbias, wg, wu, wd, g, b)


def _pack_w_in(w_in):
    (a_q, a_k, a_v, a_beta, a_dt, a_g, b_q, b_k, b_v, b_lr, b_g,
     c_q, c_f, c_i, c_g, m_a, m_b, m_c) = jnp.split(w_in, SPLIT_POINTS, axis=-1)
    lead = w_in.shape[:-1]
    a_s = jnp.concatenate([a_beta, a_dt, jnp.zeros(lead + (LANE - 2 * GDN_HEADS,), w_in.dtype)], -1)
    b_lrp = jnp.concatenate([b_lr, jnp.zeros(lead + (LANE - GLA_RANK,), w_in.dtype)], -1)
    wb = jnp.concatenate([m_a, m_b, m_c, a_q, a_k, a_v, a_g, b_q, b_k, b_v, b_g, c_q, c_i, c_g], -1).astype(BF16)
    wf = jnp.concatenate([c_f, a_s, b_lrp], -1).astype(BF16)
    assert wb.shape[-1] == NPB and wf.shape[-1] == NPF
    return wb, wf


def _prepare(w_in, gdn_conv, gdn_a_log, gdn_dt_bias, gdn_norm, gla_w2, gla_b2, gla_norm, hgrn_lb_logits,
             hgrn_norm, w_br_a, w_br_b, w_br_c, w_out, ln1_g, ln1_b, w_router, router_bias, w_gate, w_up,
             w_down, ln2_g, ln2_b):
    depth = w_in.shape[0]
    d = w_out.shape[-1]
    w_pb, w_pf = _pack_w_in(w_in)
    return dict(
        w_pb=w_pb, w_pf=w_pf,
        gdn_conv=gdn_conv,
        gdn_par=jnp.pad(jnp.stack([gdn_a_log, gdn_dt_bias], axis=1),
                        ((0, 0), (0, SUBLANE - 2), (GDN_HEADS, LANE - 2 * GDN_HEADS))),
        gdn_norm=gdn_norm.reshape(depth, 1, GDN_DV),
        w2p=jnp.concatenate([gla_w2, jnp.zeros((depth, LANE - GLA_RANK, GLA_QK), gla_w2.dtype)], axis=1),
        gla_b2=gla_b2.reshape(depth, 1, GLA_QK),
        gla_norm=gla_norm.reshape(depth, 1, GLA_DV),
        lb_logits=hgrn_lb_logits,
        hgrn_norm=hgrn_norm.reshape(depth, 1, HGRN_DV),
        wa=w_br_a.astype(BF16), wb=w_br_b.astype(BF16), wc=w_br_c.astype(BF16), wo=w_out.astype(BF16),
        ln1_g=ln1_g.reshape(depth, 1, d), ln1_b=ln1_b.reshape(depth, 1, d),
        wr_t=jnp.transpose(w_router),
        rbias=jnp.broadcast_to(router_bias[:, None], (N_EXPERTS, LANE)),
        wg=w_gate.astype(BF16), wu=w_up.astype(BF16), wd=w_down.astype(BF16),
        ln2_g=ln2_g.reshape(depth, 1, d), ln2_b=ln2_b.reshape(depth, 1, d),
    )


def _mixer_block(h, hb, p, layer, batch, seq):
    pb = _inproj_call(hb, p["w_pb"], layer, BF16, 2048, 2048, "inproj_b")
    pf = _inproj_call(hb, p["w_pf"], layer, F32, 1024, NPF, "inproj_f")
    o_a = _gdn_call(pb, pf, p["gdn_conv"], p["gdn_par"], p["gdn_norm"], layer, batch, seq)
    o_b = _gla4_call(pb, pf, p["w2p"], p["gla_b2"], p["gla_norm"], layer, batch, seq)
    o_c = _hgrn4_call(pb, pf, p["lb_logits"], p["hgrn_norm"], layer, batch, seq)
    return _merge_call(o_a, o_b, o_c, pb, h, p["wa"], p["wb"], p["wc"], p["wo"], p["ln1_g"], p["ln1_b"], layer)


MOE_TR = 256
MOE_NBKT = N_GROUPS * 6
NPIECE = D_MODEL // LANE
SC_CH = 64


def _to_token_major(ref, x, rows):
    for j in range(NPIECE):
        ref[pl.ds(j, rows, stride=NPIECE), :] = x[:, j * LANE:(j + 1) * LANE]


def _from_token_major(ref, rows):
    return jnp.concatenate([ref[pl.ds(j, rows, stride=NPIECE), :] for j in range(NPIECE)], axis=1)


def _route_pairs(scores_t, bias_ref):
    s = [scores_t[e:e + 1, :] for e in range(N_EXPERTS)]
    sel = [s[e] + bias_ref[e:e + 1, 0:1] for e in range(N_EXPERTS)]
    gscore = []
    for g in range(N_GROUPS):
        a, b, c, d = sel[4 * g:4 * g + 4]
        hi1, lo1 = jnp.maximum(a, b), jnp.minimum(a, b)
        hi2, lo2 = jnp.maximum(c, d), jnp.minimum(c, d)
        top1 = jnp.maximum(hi1, hi2)
        top2 = jnp.maximum(jnp.minimum(hi1, hi2), jnp.maximum(lo1, lo2))
        gscore.append(top1 + top2)
    best = gscore[0]
    gidx = jnp.zeros_like(best, dtype=jnp.int32)
    for g in range(1, N_GROUPS):
        take = gscore[g] > best
        best = jnp.where(take, gscore[g], best)
        gidx = jnp.where(take, g, gidx)
    ing, raw = [], []
    for kk in range(EXPERTS_PER_GROUP):
        vs, vr = sel[kk], s[kk]
        for g in range(1, N_GROUPS):
            pick = gidx == g
            vs = jnp.where(pick, sel[4 * g + kk], vs)
            vr = jnp.where(pick, s[4 * g + kk], vr)
        ing.append(vs)
        raw.append(vr)
    b1 = ing[0]
    i1 = jnp.zeros_like(gidx)
    for kk in range(1, EXPERTS_PER_GROUP):
        take = ing[kk] > b1
        b1 = jnp.where(take, ing[kk], b1)
        i1 = jnp.where(take, kk, i1)
    neg = jnp.full_like(b1, -jnp.inf)
    b2 = neg
    i2 = jnp.zeros_like(gidx)
    for kk in range(EXPERTS_PER_GROUP):
        cand = jnp.where(i1 == kk, neg, ing[kk])
        take = cand > b2
        b2 = jnp.where(take, cand, b2)
        i2 = jnp.where(take, kk, i2)
    w1 = raw[0]
    w2 = raw[0]
    for kk in range(1, EXPERTS_PER_GROUP):
        w1 = jnp.where(i1 == kk, raw[kk], w1)
        w2 = jnp.where(i2 == kk, raw[kk], w2)
    tot = w1 + w2
    w1 = w1 / tot
    w2 = w2 / tot
    first_lower = i1 < i2
    lo = jnp.where(first_lower, i1, i2)
    hi = jnp.where(first_lower, i2, i1)
    pidx = jnp.where(lo == 0, hi - 1, jnp.where(lo == 1, hi + 1, 5))
    bkt = gidx * 6 + pidx
    return bkt, jnp.where(first_lower, w1, w2), jnp.where(first_lower, w2, w1)


def _moe_route_kernel(h_ref, wr_ref, rb_ref, bkt_ref, rank_ref, wab_ref, cnt_ref, hx_ref, carry_ref, *, tm):
    @pl.when(pl.program_id(0) == 0)
    def _():
        carry_ref[...] = jnp.zeros_like(carry_ref)

    logits_t = _mm_nt_f32(wr_ref[...], h_ref[...])
    bkt, wa, wb = _route_pairs(_sigmoid(logits_t), rb_ref)
    sub = lax.broadcasted_iota(jnp.int32, (32, tm), 0)
    oh = jnp.where(sub == bkt, 1.0, 0.0)
    r = lax.broadcasted_iota(jnp.int32, (tm, tm), 0)
    c = lax.broadcasted_iota(jnp.int32, (tm, tm), 1)
    earlier = jnp.where(r < c, 1.0, 0.0).astype(BF16)
    before = jnp.dot(oh.astype(BF16), earlier, preferred_element_type=F32)
    carry = carry_ref[...]
    rank = jnp.sum(oh * (before + carry[:, 0:1]), axis=0, keepdims=True)
    carry = carry + jnp.sum(oh, axis=1, keepdims=True)
    carry_ref[...] = carry
    cnt_ref[...] = carry
    bkt_ref[...] = bkt
    rank_ref[...] = rank.astype(jnp.int32)
    pad = jnp.zeros((LANE - 2, tm), F32)
    wab_ref[...] = jnp.transpose(jnp.concatenate([wa, wb, pad], axis=0))
    _to_token_major(hx_ref, h_ref[...], tm)


def _moe_tables_kernel(cnt_ref, bkt_ref, rank_ref, pos_ref, tabs_ref, *, t, tr):
    cnt = cnt_ref[...]
    sz = jnp.floor((cnt + (tr - 1)) * (1.0 / tr)) * tr
    r = lax.broadcasted_iota(jnp.int32, (32, 32), 0)
    c = lax.broadcasted_iota(jnp.int32, (32, 32), 1)
    start = jnp.dot(jnp.where(c < r, 1.0, 0.0), sz, preferred_element_type=F32, precision=HIGHEST)
    end = start + sz
    sub = lax.broadcasted_iota(jnp.int32, (32, t), 0)
    pos = jnp.sum(jnp.where(sub == bkt_ref[...], start[:, 0:1], 0.0), axis=0, keepdims=True)
    pos_ref[...] = pos.astype(jnp.int32) + rank_ref[...]
    brow = lax.broadcasted_iota(jnp.int32, (32, LANE), 0)
    tile0 = lax.broadcasted_iota(jnp.int32, (32, LANE), 1).astype(F32) * tr
    tbk = jnp.sum(jnp.where((brow < MOE_NBKT) & (end <= tile0), 1, 0), axis=0, keepdims=True)
    tbk = jnp.minimum(tbk, MOE_NBKT - 1)
    total = end[MOE_NBKT - 1:MOE_NBKT, :]
    valid = jnp.where(tile0[0:1, :] < total, 1, 0)
    g = jnp.where(tbk >= 6, 1, 0) + jnp.where(tbk >= 12, 1, 0) + jnp.where(tbk >= 18, 1, 0)
    p = tbk - 6 * g
    ge3 = jnp.where(p >= 3, 1, 0)
    ge5 = jnp.where(p >= 5, 1, 0)
    ea = 4 * g + ge3 + ge5
    eb = 4 * g + p + 1 - 2 * ge3 - ge5
    zero = jnp.zeros((SUBLANE - 3, LANE), jnp.int32)
    tabs_ref[...] = jnp.concatenate([ea, eb, valid, zero], axis=0)


def _moe_group_kernel(ea_ref, eb_ref, vd_ref, xs_ref, ws_ref, wga_ref, wua_ref, wda_ref,
                      wgb_ref, wub_ref, wdb_ref, ys_ref, *, tr):
    @pl.when(vd_ref[pl.program_id(0)] > 0)
    def _():
        x = _from_token_major(xs_ref, tr).astype(BF16)
        w = ws_ref[...]

        def ffn(wg_ref, wu_ref, wd_ref, cw):
            hg = jnp.dot(x, wg_ref[...], preferred_element_type=F32)
            hu = jnp.dot(x, wu_ref[...], preferred_element_type=F32)
            hid = _silu(hg) * hu * cw
            return jnp.dot(hid.astype(BF16), wd_ref[...], preferred_element_type=F32)

        y = ffn(wga_ref, wua_ref, wda_ref, w[:, 0:1]) + ffn(wgb_ref, wub_ref, wdb_ref, w[:, 1:2])
        _to_token_major(ys_ref, y, tr)


def _moe_ln_kernel(h_ref, y_ref, g_ref, b_ref, o_ref, ob_ref, *, tm):
    out = _layer_norm(ALPHA * h_ref[...] + _from_token_major(y_ref, tm), g_ref[...], b_ref[...])
    o_ref[...] = out
    ob_ref[...] = out.astype(BF16)


def _sc_scatter_rows(x3, w2, pos, n_rows):
    from jax.experimental.pallas import tpu_sc as plsc
    t = x3.shape[0]
    info = plsc.get_sparse_core_info()
    nc, ns = info.num_cores, info.num_subcores
    per_w = t // (nc * ns)
    mesh = plsc.VectorSubcoreMesh(core_axis_name="c", subcore_axis_name="s")

    @functools.partial(
        pl.kernel, mesh=mesh,
        out_type=[jax.ShapeDtypeStruct((n_rows,) + x3.shape[1:], x3.dtype),
                  jax.ShapeDtypeStruct((n_rows,) + w2.shape[1:], w2.dtype)],
        scratch_types=[pltpu.VMEM((SC_CH,), jnp.int32), pltpu.VMEM((SC_CH,) + x3.shape[1:], x3.dtype),
                       pltpu.VMEM((SC_CH,) + w2.shape[1:], w2.dtype),
                       pltpu.SemaphoreType.DMA, pltpu.SemaphoreType.DMA])
    def k(x_hbm, w_hbm, idx_hbm, ox_hbm, ow_hbm, idx_v, rows_v, wrows_v, sem_x, sem_w):
        base = (lax.axis_index("s") * nc + lax.axis_index("c")) * per_w

        @pl.loop(0, per_w // SC_CH)
        def _(j):
            off = base + j * SC_CH
            pltpu.sync_copy(idx_hbm.at[pl.ds(off, SC_CH)], idx_v)
            pltpu.sync_copy(x_hbm.at[pl.ds(off, SC_CH)], rows_v)
            pltpu.sync_copy(w_hbm.at[pl.ds(off, SC_CH)], wrows_v)
            cx = pltpu.async_copy(rows_v, ox_hbm.at[idx_v], sem_x)
            cw = pltpu.async_copy(wrows_v, ow_hbm.at[idx_v], sem_w)
            cx.wait()
            cw.wait()

    return k(x3, w2, pos)


def _sc_gather_rows(y3, pos):
    from jax.experimental.pallas import tpu_sc as plsc
    t = pos.shape[0]
    info = plsc.get_sparse_core_info()
    nc, ns = info.num_cores, info.num_subcores
    per_w = t // (nc * ns)
    mesh = plsc.VectorSubcoreMesh(core_axis_name="c", subcore_axis_name="s")

    @functools.partial(
        pl.kernel, mesh=mesh,
        out_type=jax.ShapeDtypeStruct((t,) + y3.shape[1:], y3.dtype),
        scratch_types=[pltpu.VMEM((SC_CH,), jnp.int32), pltpu.VMEM((SC_CH,) + y3.shape[1:], y3.dtype),
                       pltpu.SemaphoreType.DMA])
    def k(y_hbm, idx_hbm, o_hbm, idx_v, rows_v, sem):
        base = (lax.axis_index("s") * nc + lax.axis_index("c")) * per_w

        @pl.loop(0, per_w // SC_CH)
        def _(j):
            off = base + j * SC_CH
            pltpu.sync_copy(idx_hbm.at[pl.ds(off, SC_CH)], idx_v)
            pltpu.async_copy(y_hbm.at[idx_v], rows_v, sem).wait()
            pltpu.sync_copy(rows_v, o_hbm.at[pl.ds(off, SC_CH)])

    return k(y3, pos)


def _moe_sparse(h, wr_t, rbias, wg, wu, wd, g, b, layer, tm=1024):
    t, d = h.shape
    tm = min(tm, t)
    tr = MOE_TR
    nt = t // tr + MOE_NBKT
    n_rows = nt * tr
    row1 = pl.BlockSpec((1, tm), lambda i: (0, i))
    bkt, rank, wab, cnt, hx = pl.pallas_call(
        functools.partial(_moe_route_kernel, tm=tm),
        grid=(t // tm,),
        in_specs=[pl.BlockSpec((tm, d), lambda i: (i, 0)),
                  pl.BlockSpec((N_EXPERTS, d), lambda i: (0, 0)),
                  pl.BlockSpec((N_EXPERTS, LANE), lambda i: (0, 0))],
        out_specs=[row1, row1, pl.BlockSpec((tm, LANE), lambda i: (i, 0)),
                   pl.BlockSpec((32, LANE), lambda i: (0, 0)),
                   pl.BlockSpec((tm * NPIECE, LANE), lambda i: (i, 0))],
        out_shape=[jax.ShapeDtypeStruct((1, t), jnp.int32), jax.ShapeDtypeStruct((1, t), jnp.int32),
                   jax.ShapeDtypeStruct((t, LANE), F32), jax.ShapeDtypeStruct((32, LANE), F32),
                   jax.ShapeDtypeStruct((t * NPIECE, LANE), F32)],
        scratch_shapes=[pltpu.VMEM((32, LANE), F32)],
        compiler_params=_cparams(("arbitrary",)),
        name="moe_route",
    )(h, wr_t, rbias)
    pos, tabs = pl.pallas_call(
        functools.partial(_moe_tables_kernel, t=t, tr=tr),
        out_shape=[jax.ShapeDtypeStruct((1, t), jnp.int32), jax.ShapeDtypeStruct((SUBLANE, LANE), jnp.int32)],
        compiler_params=pltpu.CompilerParams(vmem_limit_bytes=VMEM_LIMIT),
        name="moe_tables",
    )(cnt, bkt, rank)
    pos = pos.reshape(t)
    xs3, ws = _sc_scatter_rows(hx.reshape(t, NPIECE, LANE), wab, pos, n_rows)

    def wspec(which, shape):
        if which == 0:
            return pl.BlockSpec((None, None) + shape, lambda j, ea, eb, vd: (layer, ea[j], 0, 0))
        return pl.BlockSpec((None, None) + shape, lambda j, ea, eb, vd: (layer, eb[j], 0, 0))

    ys = pl.pallas_call(
        functools.partial(_moe_group_kernel, tr=tr),
        grid_spec=pltpu.PrefetchScalarGridSpec(
            num_scalar_prefetch=3,
            grid=(nt,),
            in_specs=[pl.BlockSpec((tr * NPIECE, LANE), lambda j, ea, eb, vd: (j, 0)),
                      pl.BlockSpec((tr, LANE), lambda j, ea, eb, vd: (j, 0)),
                      wspec(0, (d, D_FF)), wspec(0, (d, D_FF)), wspec(0, (D_FF, d)),
                      wspec(1, (d, D_FF)), wspec(1, (d, D_FF)), wspec(1, (D_FF, d))],
            out_specs=pl.BlockSpec((tr * NPIECE, LANE), lambda j, ea, eb, vd: (j, 0))),
        out_shape=jax.ShapeDtypeStruct((n_rows * NPIECE, LANE), F32),
        compiler_params=_cparams(("arbitrary",)),
        name="moe_experts",
    )(tabs[0, :nt], tabs[1, :nt], tabs[2, :nt], xs3.reshape(n_rows * NPIECE, LANE), ws,
      wg, wu, wd, wg, wu, wd)
    y3 = _sc_gather_rows(ys.reshape(n_rows, NPIECE, LANE), pos)
    vec = pl.BlockSpec((None, 1, d), lambda i: (layer, 0, 0))
    row = pl.BlockSpec((tm, d), lambda i: (i, 0))
    return pl.pallas_call(
        functools.partial(_moe_ln_kernel, tm=tm),
        grid=(t // tm,),
        in_specs=[row, pl.BlockSpec((tm * NPIECE, LANE), lambda i: (i, 0)), vec, vec],
        out_specs=[row, row],
        out_shape=[jax.ShapeDtypeStruct((t, d), F32), jax.ShapeDtypeStruct((t, d), BF16)],
        compiler_params=_cparams(("parallel",)),
        name="moe_ln",
    )(h, y3.reshape(t * NPIECE, LANE), g, b)


def _ffn_block(h, p, layer):
    return _moe_sparse(h, p["wr_t"], p["rbias"], p["wg"], p["wu"], p["wd"], p["ln2_g"], p["ln2_b"], layer)


def kernel(x, ln0_g, ln0_b, w_in, gdn_conv, gdn_a_log, gdn_dt_bias, gdn_norm, gla_w2, gla_b2, gla_norm,
           hgrn_lb_logits, hgrn_norm, w_br_a, w_br_b, w_br_c, w_out, ln1_g, ln1_b, w_router, router_bias,
           w_gate, w_up, w_down, ln2_g, ln2_b):
    batch, seq, d = x.shape
    p = _prepare(w_in, gdn_conv, gdn_a_log, gdn_dt_bias, gdn_norm, gla_w2, gla_b2, gla_norm, hgrn_lb_logits,
                 hgrn_norm, w_br_a, w_br_b, w_br_c, w_out, ln1_g, ln1_b, w_router, router_bias, w_gate, w_up,
                 w_down, ln2_g, ln2_b)
    h, hb = _ln_call(x.reshape(batch * seq, d), ln0_g, ln0_b)
    for layer in range(w_in.shape[0]):
        h = _mixer_block(h, hb, p, layer, batch, seq)
        h, hb = _ffn_block(h, p, layer)
    return h.reshape(batch, seq, d)
```

```python
import functools

import numpy as np
import jax
import jax.numpy as jnp
from jax import lax
from jax.experimental import pallas as pl
from jax.experimental.pallas import tpu as pltpu

F32 = jnp.float32
BF16 = jnp.bfloat16
HIGHEST = lax.Precision.HIGHEST

D_MODEL = 1024
DEPTH = 4
CHUNK = 64
GDN_HEADS, GDN_DK, GDN_DV, CONV_W = 4, 128, 128, 4
GLA_HEADS, GLA_DK, GLA_DV, GLA_RANK, GLA_NORMALIZER = 4, 64, 128, 16, 16.0
HGRN_HEADS, HGRN_EXPAND, HGRN_DV = 4, 128, 128
LB_FLOOR = 1e-30
N_EXPERTS, N_GROUPS, TOP_K, D_FF = 16, 4, 2, 256
EXPERTS_PER_GROUP = N_EXPERTS // N_GROUPS
ALPHA = (2.0 * DEPTH) ** 0.25
LN_EPS = 1e-5
RMS_EPS = 1e-6

GDN_QK = GDN_HEADS * GDN_DK
GDN_V = GDN_HEADS * GDN_DV
GLA_QK = GLA_HEADS * GLA_DK
GLA_V = GLA_HEADS * GLA_DV
HGRN_QK = HGRN_HEADS * HGRN_EXPAND
HGRN_V = HGRN_HEADS * HGRN_DV
SPLIT_SIZES = (GDN_QK, GDN_QK, GDN_V, GDN_HEADS, GDN_HEADS, GDN_V,
               GLA_QK, GLA_QK, GLA_V, GLA_RANK, GLA_V,
               HGRN_QK, HGRN_QK, HGRN_V, HGRN_V,
               D_MODEL, D_MODEL, D_MODEL)
SPLIT_POINTS = tuple(int(v) for v in np.cumsum(SPLIT_SIZES)[:-1])

LANE = 128
SUBLANE = 8
VMEM_LIMIT = 48 * 1024 * 1024

M_A, M_B, M_C = 0, 1024, 2048
A_Q, A_K, A_V, A_G = 3072, 3584, 4096, 4608
B_Q, B_K, B_V, B_G = 5120, 5376, 5632, 6144
C_Q, C_I, C_G = 6656, 7168, 7680
NPB = 8192
F_CF, F_AS, F_LR = 0, 512, 640
NPF = 768

SUB = 8
DD_NPAR = 4
GDN_NPAR = 4
MOE_EPS = 4


def _cparams(sem):
    return pltpu.CompilerParams(dimension_semantics=sem, vmem_limit_bytes=VMEM_LIMIT)


def _mm(a, b):
    return jnp.dot(a.astype(BF16), b.astype(BF16), preferred_element_type=F32)


def _mm_nt(a, b):
    return lax.dot_general(a.astype(BF16), b.astype(BF16), (((1,), (1,)), ((), ())),
                           preferred_element_type=F32)


def _mm_tn(a, b):
    return lax.dot_general(a.astype(BF16), b.astype(BF16), (((0,), (0,)), ((), ())),
                           preferred_element_type=F32)


def _mm_nt_f32(a, b):
    return lax.dot_general(a, b, (((1,), (1,)), ((), ())), preferred_element_type=F32,
                           precision=HIGHEST)


def _split3(x):
    hi = x.astype(BF16)
    r = x - hi.astype(F32)
    mid = r.astype(BF16)
    lo = (r - mid.astype(F32)).astype(BF16)
    return hi, mid, lo


def _mm_01(m01, x):
    hi, mid, lo = _split3(x)
    return (jnp.dot(m01, hi, preferred_element_type=F32) + jnp.dot(m01, mid, preferred_element_type=F32)
            + jnp.dot(m01, lo, preferred_element_type=F32))


def _sigmoid(x):
    return 1.0 / (1.0 + jnp.exp(-x))


def _sigmoid_t(x):
    return 0.5 * jnp.tanh(0.5 * x) + 0.5


def _silu(x):
    return x * _sigmoid_t(x)


def _softplus(x):
    return jnp.maximum(x, 0.0) + jnp.log(1.0 + jnp.exp(-jnp.abs(x)))


def _log_sigmoid(x):
    return -_softplus(-x)


def _layer_norm(x, g, b):
    mu = jnp.mean(x, axis=-1, keepdims=True)
    xc = x - mu
    var = jnp.mean(xc * xc, axis=-1, keepdims=True)
    return xc * lax.rsqrt(var + LN_EPS) * g + b


def _ln_kernel(x_ref, g_ref, b_ref, o_ref, ob_ref):
    y = _layer_norm(x_ref[...], g_ref[...], b_ref[...])
    o_ref[...] = y
    ob_ref[...] = y.astype(BF16)


def _ln_call(x, g, b, tm=512):
    t, d = x.shape
    tm = min(tm, t)
    return pl.pallas_call(
        _ln_kernel,
        grid=(t // tm,),
        in_specs=[pl.BlockSpec((tm, d), lambda i: (i, 0)),
                  pl.BlockSpec((1, d), lambda i: (0, 0)),
                  pl.BlockSpec((1, d), lambda i: (0, 0))],
        out_specs=[pl.BlockSpec((tm, d), lambda i: (i, 0)), pl.BlockSpec((tm, d), lambda i: (i, 0))],
        out_shape=[jax.ShapeDtypeStruct((t, d), F32), jax.ShapeDtypeStruct((t, d), BF16)],
        compiler_params=_cparams(("parallel",)),
        name="ln0",
    )(x, g.reshape(1, d), b.reshape(1, d))


def _inproj_kernel(x_ref, wt_ref, o_ref):
    o_ref[...] = lax.dot_general(x_ref[...], wt_ref[...], (((1,), (1,)), ((), ())),
                                 preferred_element_type=F32).astype(o_ref.dtype)


def _inproj_call(hb, w, layer, out_dtype, tm, tn, name):
    t, d = hb.shape
    tm = min(tm, t)
    n = w.shape[-2]
    return pl.pallas_call(
        _inproj_kernel,
        grid=(n // tn, t // tm),
        in_specs=[pl.BlockSpec((tm, d), lambda j, i: (i, 0)),
                  pl.BlockSpec((None, tn, d), lambda j, i: (layer, j, 0))],
        out_specs=pl.BlockSpec((tm, tn), lambda j, i: (i, j)),
        out_shape=jax.ShapeDtypeStruct((t, n), out_dtype),
        compiler_params=_cparams(("parallel", "parallel")),
        name=name,
    )(hb, w)


def _gdn_kernel(qkv_ref, s_ref, gate_ref, cw_ref, par_ref, nw_ref, o_ref,
                state_ref, tail_ref, xbuf_ref, q_s, k_s, v_s, cumb_s, betab_s, cumrow_s, o_s, *, tb):
    blk = pl.program_id(1)
    nc = tb // CHUNK
    nh = GDN_HEADS
    off0 = SUBLANE - (CONV_W - 1)

    @pl.when(blk == 0)
    def _():
        state_ref[...] = jnp.zeros_like(state_ref)
        tail_ref[...] = jnp.zeros_like(tail_ref)

    row = lax.broadcasted_iota(jnp.int32, (CHUNK, CHUNK), 0)
    col = lax.broadcasted_iota(jnp.int32, (CHUNK, CHUNK), 1)
    incl = col <= row
    strict = col < row
    tri = jnp.where(incl, 1.0, 0.0).astype(BF16)
    eye = jnp.where(col == row, 1.0, 0.0).astype(F32)
    npar = GDN_NPAR
    rp = npar * CHUNK
    xbuf_ref[0:SUBLANE, :] = tail_ref[...]

    def prologue(t):
        lo = t * rp
        xbuf_ref[SUBLANE + lo:SUBLANE + lo + rp, :] = qkv_ref[lo:lo + rp, :].astype(F32)
        for j in range(3 * nh):
            cs = slice(j * LANE, (j + 1) * LANE)
            y = xbuf_ref[off0 + lo:off0 + lo + rp, cs] * cw_ref[0:1, cs]
            for kk in range(1, CONV_W):
                y = y + xbuf_ref[off0 + kk + lo:off0 + kk + lo + rp, cs] * cw_ref[kk:kk + 1, cs]
            y = _silu(y)
            if j < nh:
                q_s[j, lo:lo + rp, :] = (y * lax.rsqrt(jnp.sum(y * y, axis=-1, keepdims=True) + RMS_EPS)
                                         * (GDN_DK ** -0.5))
            elif j < 2 * nh:
                k_s[j - nh, lo:lo + rp, :] = y * lax.rsqrt(jnp.sum(y * y, axis=-1, keepdims=True) + RMS_EPS)
            else:
                v_s[j - 2 * nh, lo:lo + rp, :] = y
        sc = s_ref[lo:lo + rp, :]
        beta_all = _sigmoid_t(sc)
        g_all = -jnp.exp(par_ref[0:1, :]) * _softplus(sc + par_ref[1:2, :])
        cum_all = jnp.concatenate([_mm_01(tri, g_all[c * CHUNK:(c + 1) * CHUNK, :]) for c in range(npar)], axis=0)
        cum_t = jnp.transpose(cum_all)
        for c in range(npar):
            cumrow_s[t * npar + c] = cum_t[0:SUBLANE, c * CHUNK:(c + 1) * CHUNK]
        for h in range(nh):
            cumb_s[h, lo:lo + rp, :] = jnp.broadcast_to(cum_all[:, nh + h:nh + h + 1], (rp, LANE))
            betab_s[h, lo:lo + rp, :] = jnp.broadcast_to(beta_all[:, h:h + 1], (rp, LANE))

    def trip(cp):
        chains = [(cp * npar + cc, h) for cc in range(npar) for h in range(nh)]
        r0s = [c * CHUNK for c, _ in chains]
        qc = [q_s[h, pl.ds(r0, CHUNK), :] for (_, h), r0 in zip(chains, r0s)]
        kc = [k_s[h, pl.ds(r0, CHUNK), :] for (_, h), r0 in zip(chains, r0s)]
        vc = [v_s[h, pl.ds(r0, CHUNK), :] for (_, h), r0 in zip(chains, r0s)]
        cumc = [cumb_s[h, pl.ds(r0, CHUNK), :] for (_, h), r0 in zip(chains, r0s)]
        bc = [betab_s[h, pl.ds(r0, CHUNK), :] for (_, h), r0 in zip(chains, r0s)]
        n = len(chains)
        kk = [_mm_nt(kc[i], kc[i]) for i in range(n)]
        qk = [_mm_nt(qc[i], kc[i]) for i in range(n)]
        decay = []
        for i, (c, h) in enumerate(chains):
            diff = cumc[i][:, 0:CHUNK] - cumrow_s[c][nh + h:nh + h + 1, :]
            decay.append(jnp.where(incl, jnp.exp(jnp.where(incl, diff, 0.0)), 0.0))
        a = [jnp.where(strict, bc[i][:, 0:CHUNK] * kk[i] * decay[i], 0.0) for i in range(n)]
        x = [eye - a[i] for i in range(n)]
        p = [_mm(a[i], a[i]) for i in range(n)]
        for it in range(5):
            x = [x[i] + _mm(x[i], p[i]) for i in range(n)]
            if it < 4:
                p = [_mm(p[i], p[i]) for i in range(n)]
        ecum = [jnp.exp(cumc[i]) for i in range(n)]
        sol = [_mm(x[i], jnp.concatenate([vc[i] * bc[i], kc[i] * (bc[i] * ecum[i])], axis=1)) for i in range(n)]
        attn = [qk[i] * decay[i] for i in range(n)]
        cum_last = [cumc[i][CHUNK - 1:CHUNK, :] for i in range(n)]
        k_state = [kc[i] * jnp.exp(cum_last[i] - cumc[i]) for i in range(n)]
        wqi = [jnp.concatenate([sol[i][:, GDN_DV:GDN_DV + GDN_DK], qc[i] * ecum[i]], axis=0) for i in range(n)]
        for cc in range(npar):
            idx = [cc * nh + h for h in range(nh)]
            s = [state_ref[h] for h in range(nh)]
            wq = [_mm(wqi[i], s[h]) for h, i in enumerate(idx)]
            v_new = [sol[i][:, 0:GDN_DV] - wq[h][0:CHUNK, :] for h, i in enumerate(idx)]
            av = [_mm(attn[i], v_new[h]) for h, i in enumerate(idx)]
            upd = [_mm_tn(k_state[i], v_new[h]) for h, i in enumerate(idx)]
            for h, i in enumerate(idx):
                state_ref[h] = s[h] * jnp.exp(cum_last[i]) + upd[h]
                o_s[pl.ds(r0s[i], CHUNK), h * LANE:(h + 1) * LANE] = wq[h][CHUNK:2 * CHUNK, :] + av[h]

    def epilogue(t):
        lo = t * rp
        for h in range(nh):
            hs = slice(h * LANE, (h + 1) * LANE)
            o = o_s[lo:lo + rp, hs]
            oh = o * lax.rsqrt(jnp.mean(o * o, axis=-1, keepdims=True) + RMS_EPS) * nw_ref[...]
            o_ref[lo:lo + rp, hs] = (oh * _silu(gate_ref[lo:lo + rp, hs].astype(F32))).astype(o_ref.dtype)

    nparts = nc // npar
    prologue(0)
    for t in range(nparts):
        if t + 1 < nparts:
            prologue(t + 1)
        trip(t)
        epilogue(t)
    tail_ref[...] = xbuf_ref[tb:tb + SUBLANE, :]


def _gdn_call(pb, pf, conv_w, par, norm_w, layer, batch, seq, tb=512):
    tb = min(tb, seq)
    nb = seq // tb
    nc = tb // CHUNK
    t = batch * seq
    wq = 2 * GDN_QK + GDN_V
    kern = functools.partial(_gdn_kernel, tb=tb)
    return pl.pallas_call(
        kern,
        grid=(batch, nb),
        in_specs=[pl.BlockSpec((tb, wq), lambda b, i: (b * nb + i, A_Q // wq)),
                  pl.BlockSpec((tb, LANE), lambda b, i: (b * nb + i, F_AS // LANE)),
                  pl.BlockSpec((tb, GDN_V), lambda b, i: (b * nb + i, A_G // GDN_V)),
                  pl.BlockSpec((None, CONV_W, wq), lambda b, i: (layer, 0, 0)),
                  pl.BlockSpec((None, SUBLANE, LANE), lambda b, i: (layer, 0, 0)),
                  pl.BlockSpec((None, 1, GDN_DV), lambda b, i: (layer, 0, 0))],
        out_specs=pl.BlockSpec((tb, GDN_V), lambda b, i: (b * nb + i, 0)),
        out_shape=jax.ShapeDtypeStruct((t, GDN_V), BF16),
        scratch_shapes=[pltpu.VMEM((GDN_HEADS, GDN_DK, GDN_DV), F32),
                        pltpu.VMEM((SUBLANE, wq), F32),
                        pltpu.VMEM((tb + SUBLANE, wq), F32),
                        pltpu.VMEM((GDN_HEADS, tb, LANE), F32),
                        pltpu.VMEM((GDN_HEADS, tb, LANE), F32),
                        pltpu.VMEM((GDN_HEADS, tb, LANE), F32),
                        pltpu.VMEM((GDN_HEADS, tb, LANE), F32),
                        pltpu.VMEM((GDN_HEADS, tb, LANE), F32),
                        pltpu.VMEM((nc, SUBLANE, CHUNK), F32),
                        pltpu.VMEM((tb, GDN_V), F32)],
        compiler_params=_cparams(("parallel", "arbitrary")),
        name="gdn",
    )(pb, pf, pb, conv_w, par, norm_w)


def _dd_core(q_s, k_s, v_s, la_s, o_s, state_ref, c8_s, p_s, *, tb, g_heads):
    nchunk = tb // CHUNK
    nblk = CHUNK // SUB
    dkh = LANE // g_heads
    dvp = g_heads * LANE
    row = lax.broadcasted_iota(jnp.int32, (CHUNK, CHUNK), 0)
    col = lax.broadcasted_iota(jnp.int32, (CHUNK, CHUNK), 1)
    level_masks = []
    for sh in (5, 4, 3):
        same2b = jnp.right_shift(row, sh + 1) == jnp.right_shift(col, sh + 1)
        upper = (jnp.right_shift(row, sh) & 1) == 1
        lower = (jnp.right_shift(col, sh) & 1) == 0
        level_masks.append(jnp.where(same2b, jnp.where(upper, jnp.where(lower, 1.0, 0.0), 0.0), 0.0))
    tri8 = jnp.where(jnp.right_shift(row, 3) == jnp.right_shift(col, 3),
                     jnp.where(col <= row, 1.0, 0.0), 0.0).astype(BF16)
    lane128 = lax.broadcasted_iota(jnp.int32, (CHUNK, LANE), 1)
    head_masks = [jnp.where((lane128 >= g * dkh) & (lane128 < (g + 1) * dkh), 1.0, 0.0)
                  for g in range(g_heads)]
    sub = lax.broadcasted_iota(jnp.int32, (SUB, LANE), 0)
    dk_sh = dkh.bit_length() - 1
    lane_sh = LANE.bit_length() - 1
    orow = lax.broadcasted_iota(jnp.int32, (LANE, dvp), 0)
    ocol = lax.broadcasted_iota(jnp.int32, (LANE, dvp), 1)
    ones_bd = jnp.where(jnp.right_shift(orow, dk_sh) == jnp.right_shift(ocol, lane_sh), 1.0, 0.0).astype(BF16)
    srow = lax.broadcasted_iota(jnp.int32, (dvp, LANE), 0)
    scol = lax.broadcasted_iota(jnp.int32, (dvp, LANE), 1)
    state_mask = jnp.where(jnp.right_shift(srow, lane_sh) == jnp.right_shift(scol, dk_sh), 1.0, 0.0)
    npar = DD_NPAR

    def block_sums(i):
        c8 = [c8_s[i, b * SUB:(b + 1) * SUB, :] for b in range(nblk)]
        t8 = [c8_s[i, (b + 1) * SUB - 1:(b + 1) * SUB, :] for b in range(nblk)]
        t16 = [t8[2 * b] + t8[2 * b + 1] for b in range(nblk // 2)]
        t32 = [t16[2 * b] + t16[2 * b + 1] for b in range(nblk // 4)]
        t64 = t32[0] + t32[1]
        c16 = [c8[b] + t8[b - 1] if b % 2 else c8[b] for b in range(nblk)]
        c32 = [c16[b] + t16[b // 2 - 1] if (b // 2) % 2 else c16[b] for b in range(nblk)]
        c64 = [c32[b] + t32[0] if b >= nblk // 2 else c32[b] for b in range(nblk)]
        pre = {8: c8, 16: c16, 32: c32, 64: c64}
        suf = {8: [t8[b] - c8[b] for b in range(nblk)],
               16: [t16[b // 2] - c16[b] for b in range(nblk)],
               32: [t32[b // 4] - c32[b] for b in range(nblk)],
               64: [t64 - c64[b] for b in range(nblk)]}
        return pre, suf

    def cat(pieces):
        return jnp.concatenate(pieces, axis=0)

    def group_body(cg, carry):
        rng = range(npar)
        r0s = [pl.multiple_of((cg * npar + i) * CHUNK, CHUNK) for i in rng]
        qc = [q_s[pl.ds(r0, CHUNK), :] for r0 in r0s]
        kc = [k_s[pl.ds(r0, CHUNK), :] for r0 in r0s]
        vc = [v_s[pl.ds(r0, CHUNK), :] for r0 in r0s]
        c8_all = _mm_01(tri8, jnp.concatenate([la_s[pl.ds(r0, CHUNK), :] for r0 in r0s], axis=1))
        for i in rng:
            c8_s[i] = c8_all[:, i * LANE:(i + 1) * LANE]
        sums = [block_sums(i) for i in rng]
        attn = [[None] * g_heads for _ in rng]
        for li, b in enumerate((32, 16, 8)):
            qs = [qc[i] * jnp.exp(cat(sums[i][0][b])) for i in rng]
            ks = [kc[i] * jnp.exp(cat(sums[i][1][b])) for i in rng]
            for g in range(g_heads):
                for i in rng:
                    qg = qs[i] * head_masks[g] if g_heads > 1 else qs[i]
                    term = _mm_nt(qg, ks[i]) * level_masks[li]
                    attn[i][g] = term if li == 0 else attn[i][g] + term
        o = [jnp.concatenate([_mm(attn[i][g], vc[i][:, g * LANE:(g + 1) * LANE]) for g in range(g_heads)], axis=1)
             if g_heads > 1 else _mm(attn[i][0], vc[i]) for i in rng]
        for i in rng:
            c8 = sums[i][0][8]
            for r in range(nblk):
                qr = qc[i][r * SUB:(r + 1) * SUB, :]
                for jj in range(SUB):
                    krow = k_s[pl.ds(r0s[i] + r * SUB + jj, 1), :]
                    crow = c8_s[i, r * SUB + jj:r * SUB + jj + 1, :]
                    m = sub >= jj
                    pr = qr * krow * jnp.exp(jnp.where(m, c8[r] - crow, -jnp.inf))
                    p_s[i, (r * SUB + jj) * SUB:(r * SUB + jj + 1) * SUB, :] = pr
        rs = [jnp.dot(p_s[i].astype(BF16), ones_bd, preferred_element_type=F32) for i in rng]
        for i in rng:
            od = []
            for r in range(nblk):
                acc = None
                for jj in range(SUB):
                    vrow = v_s[pl.ds(r0s[i] + r * SUB + jj, 1), :]
                    term = rs[i][(r * SUB + jj) * SUB:(r * SUB + jj + 1) * SUB, :] * vrow
                    acc = term if acc is None else acc + term
                od.append(acc)
            o[i] = o[i] + cat(od)
        q_inter = [qc[i] * jnp.exp(cat(sums[i][0][64])) for i in rng]
        k_state = [kc[i] * jnp.exp(cat(sums[i][1][64])) for i in rng]
        upd = [_mm_tn(vc[i], k_state[i]) for i in rng]
        st = state_ref[...]
        for i in rng:
            o_s[pl.ds(r0s[i], CHUNK), :] = o[i] + _mm_nt(q_inter[i], st)
            decay_last = jnp.exp(sums[i][0][64][nblk - 1][SUB - 1:SUB, :])
            st = st * decay_last + (upd[i] * state_mask if g_heads > 1 else upd[i])
        state_ref[...] = st
        return carry

    lax.fori_loop(0, nchunk // npar, group_body, 0)


def _gated_rms_out(o_s, gate_ref, nw_ref, o_ref, g_heads):
    for g in range(g_heads):
        hs = slice(g * LANE, (g + 1) * LANE)
        o = o_s[:, hs]
        oh = o * lax.rsqrt(jnp.mean(o * o, axis=-1, keepdims=True) + RMS_EPS) * nw_ref[...]
        o_ref[:, hs] = (oh * _silu(gate_ref[:, hs].astype(F32))).astype(o_ref.dtype)


def _gla_kernel(q_ref, k_ref, v_ref, lr_ref, gate_ref, w2_ref, b2_ref, nw_ref,
                o_ref, state_ref, q_s, k_s, v_s, la_s, o_s, c8_s, p_s, *, tb):
    @pl.when(pl.program_id(2) == 0)
    def _():
        state_ref[...] = jnp.zeros_like(state_ref)

    q_s[...] = q_ref[...].astype(F32) * (GLA_DK ** -0.5)
    k_s[...] = k_ref[...].astype(F32)
    v_s[...] = v_ref[...].astype(F32)
    z = _mm(lr_ref[...], w2_ref[...]) + b2_ref[...]
    la_s[...] = _log_sigmoid(z) * (1.0 / GLA_NORMALIZER)
    _dd_core(q_s, k_s, v_s, la_s, o_s, state_ref, c8_s, p_s, tb=tb, g_heads=2)
    _gated_rms_out(o_s, gate_ref, nw_ref, o_ref, 2)


def _dd_scratch(tb, dvp):
    return [pltpu.VMEM((dvp, LANE), F32),
            pltpu.VMEM((tb, LANE), F32), pltpu.VMEM((tb, LANE), F32),
            pltpu.VMEM((tb, dvp), F32),
            pltpu.VMEM((tb, LANE), F32),
            pltpu.VMEM((tb, dvp), F32),
            pltpu.VMEM((DD_NPAR, CHUNK, LANE), F32),
            pltpu.VMEM((DD_NPAR, CHUNK * SUB, LANE), F32)]


def _gla_call(pb, pf, w2p, b2, norm_w, layer, batch, seq, tb=512):
    tb = min(tb, seq)
    nb = seq // tb
    t = batch * seq
    npair = GLA_HEADS // 2
    kern = functools.partial(_gla_kernel, tb=tb)
    return pl.pallas_call(
        kern,
        grid=(batch, npair, nb),
        in_specs=[pl.BlockSpec((tb, LANE), lambda b, p, i: (b * nb + i, B_Q // LANE + p)),
                  pl.BlockSpec((tb, LANE), lambda b, p, i: (b * nb + i, B_K // LANE + p)),
                  pl.BlockSpec((tb, 2 * LANE), lambda b, p, i: (b * nb + i, B_V // (2 * LANE) + p)),
                  pl.BlockSpec((tb, LANE), lambda b, p, i: (b * nb + i, F_LR // LANE)),
                  pl.BlockSpec((tb, 2 * LANE), lambda b, p, i: (b * nb + i, B_G // (2 * LANE) + p)),
                  pl.BlockSpec((None, LANE, LANE), lambda b, p, i: (layer, 0, p)),
                  pl.BlockSpec((None, 1, LANE), lambda b, p, i: (layer, 0, p)),
                  pl.BlockSpec((None, 1, GLA_DV), lambda b, p, i: (layer, 0, 0))],
        out_specs=pl.BlockSpec((tb, 2 * LANE), lambda b, p, i: (b * nb + i, p)),
        out_shape=jax.ShapeDtypeStruct((t, GLA_V), BF16),
        scratch_shapes=_dd_scratch(tb, 2 * LANE),
        compiler_params=_cparams(("parallel", "parallel", "arbitrary")),
        name="gla",
    )(pb, pb, pb, pf, pb, w2p, b2, norm_w)


def _hgrn_kernel(q_ref, f_ref, v_ref, gate_ref, lbl_ref, nw_ref,
                 o_ref, state_ref, q_s, k_s, v_s, la_s, o_s, c8_s, p_s, *, tb, layer):
    @pl.when(pl.program_id(2) == 0)
    def _():
        state_ref[...] = jnp.zeros_like(state_ref)

    logits = lbl_ref[...]
    mx = jnp.max(logits, axis=0, keepdims=True)
    ex = jnp.exp(logits - mx)
    p = ex / jnp.sum(ex, axis=0, keepdims=True)
    acc = p[0:1, :]
    for r in range(1, layer + 1):
        acc = acc + p[r:r + 1, :]
    lb = jnp.clip(acc - p[0:1, :], 0.0, 1.0)
    log_lb = jnp.log(jnp.maximum(lb, LB_FLOOR))
    log_1m = jnp.log(1.0 - lb)

    cf = f_ref[...]
    second = log_1m + _log_sigmoid(cf)
    la_s[...] = jnp.maximum(log_lb, second) + jnp.log(1.0 + jnp.exp(-jnp.abs(log_lb - second)))
    k_s[...] = (1.0 - lb) * _sigmoid_t(-cf)
    q_s[...] = _silu(q_ref[...].astype(F32)) * (HGRN_EXPAND ** -0.5)
    v_s[...] = v_ref[...].astype(F32)
    _dd_core(q_s, k_s, v_s, la_s, o_s, state_ref, c8_s, p_s, tb=tb, g_heads=1)
    _gated_rms_out(o_s, gate_ref, nw_ref, o_ref, 1)


def _hgrn_call(pb, pf, lb_logits, norm_w, layer, batch, seq, tb=512):
    tb = min(tb, seq)
    nb = seq // tb
    t = batch * seq

    def tok(colblk):
        return pl.BlockSpec((tb, LANE), lambda b, h, i: (b * nb + i, colblk + h))

    kern = functools.partial(_hgrn_kernel, tb=tb, layer=layer)
    return pl.pallas_call(
        kern,
        grid=(batch, HGRN_HEADS, nb),
        in_specs=[tok(C_Q // LANE), tok(F_CF // LANE), tok(C_I // LANE), tok(C_G // LANE),
                  pl.BlockSpec((DEPTH, LANE), lambda b, h, i: (0, h)),
                  pl.BlockSpec((None, 1, HGRN_DV), lambda b, h, i: (layer, 0, 0))],
        out_specs=pl.BlockSpec((tb, LANE), lambda b, h, i: (b * nb + i, h)),
        out_shape=jax.ShapeDtypeStruct((t, HGRN_V), BF16),
        scratch_shapes=_dd_scratch(tb, LANE),
        compiler_params=_cparams(("parallel", "parallel", "arbitrary")),
        name="hgrn",
    )(pb, pf, pb, pb, lb_logits, norm_w)


def _dd_make_trip(q_s, k_s, v_s, la_s, o_s, state_ref, c8_s, p_s, g_heads):
    nblk = CHUNK // SUB
    dkh = LANE // g_heads
    dvp = g_heads * LANE
    npar = DD_NPAR
    row = lax.broadcasted_iota(jnp.int32, (CHUNK, CHUNK), 0)
    col = lax.broadcasted_iota(jnp.int32, (CHUNK, CHUNK), 1)
    level_masks = []
    for sh in (5, 4, 3):
        same2b = jnp.right_shift(row, sh + 1) == jnp.right_shift(col, sh + 1)
        upper = (jnp.right_shift(row, sh) & 1) == 1
        lower = (jnp.right_shift(col, sh) & 1) == 0
        level_masks.append(jnp.where(same2b, jnp.where(upper, jnp.where(lower, 1.0, 0.0), 0.0), 0.0))
    tri8 = jnp.where(jnp.right_shift(row, 3) == jnp.right_shift(col, 3),
                     jnp.where(col <= row, 1.0, 0.0), 0.0).astype(BF16)
    lane128 = lax.broadcasted_iota(jnp.int32, (CHUNK, LANE), 1)
    head_masks = [jnp.where((lane128 >= g * dkh) & (lane128 < (g + 1) * dkh), 1.0, 0.0)
                  for g in range(g_heads)]
    sub = lax.broadcasted_iota(jnp.int32, (SUB, LANE), 0)
    dk_sh = dkh.bit_length() - 1
    lane_sh = LANE.bit_length() - 1
    orow = lax.broadcasted_iota(jnp.int32, (LANE, dvp), 0)
    ocol = lax.broadcasted_iota(jnp.int32, (LANE, dvp), 1)
    ones_bd = jnp.where(jnp.right_shift(orow, dk_sh) == jnp.right_shift(ocol, lane_sh), 1.0, 0.0).astype(BF16)
    srow = lax.broadcasted_iota(jnp.int32, (dvp, LANE), 0)
    scol = lax.broadcasted_iota(jnp.int32, (dvp, LANE), 1)
    state_mask = jnp.where(jnp.right_shift(srow, lane_sh) == jnp.right_shift(scol, dk_sh), 1.0, 0.0)

    def block_sums(ci):
        c8 = [c8_s[ci, b * SUB:(b + 1) * SUB, :] for b in range(nblk)]
        t8 = [c8_s[ci, (b + 1) * SUB - 1:(b + 1) * SUB, :] for b in range(nblk)]
        t16 = [t8[2 * b] + t8[2 * b + 1] for b in range(nblk // 2)]
        t32 = [t16[2 * b] + t16[2 * b + 1] for b in range(nblk // 4)]
        t64 = t32[0] + t32[1]
        c16 = [c8[b] + t8[b - 1] if b % 2 else c8[b] for b in range(nblk)]
        c32 = [c16[b] + t16[b // 2 - 1] if (b // 2) % 2 else c16[b] for b in range(nblk)]
        c64 = [c32[b] + t32[0] if b >= nblk // 2 else c32[b] for b in range(nblk)]
        pre = {8: c8, 16: c16, 32: c32, 64: c64}
        suf = {8: [t8[b] - c8[b] for b in range(nblk)],
               16: [t16[b // 2] - c16[b] for b in range(nblk)],
               32: [t32[b // 4] - c32[b] for b in range(nblk)],
               64: [t64 - c64[b] for b in range(nblk)]}
        return pre, suf

    def cat(pieces):
        return jnp.concatenate(pieces, axis=0)

    def trip(u, t, slot):
        rng = range(npar)
        r0s = [(t * npar + i) * CHUNK for i in rng]
        cis = [slot * npar + i for i in rng]
        qc = [q_s[u, r0:r0 + CHUNK, :] for r0 in r0s]
        kc = [k_s[u, r0:r0 + CHUNK, :] for r0 in r0s]
        vc = [v_s[u, r0:r0 + CHUNK, :] for r0 in r0s]
        c8_all = _mm_01(tri8, jnp.concatenate([la_s[u, r0:r0 + CHUNK, :] for r0 in r0s], axis=1))
        for i in rng:
            c8_s[cis[i]] = c8_all[:, i * LANE:(i + 1) * LANE]
        sums = [block_sums(cis[i]) for i in rng]
        attn = [[None] * g_heads for _ in rng]
        for li, b in enumerate((32, 16, 8)):
            qs = [qc[i] * jnp.exp(cat(sums[i][0][b])) for i in rng]
            ks = [kc[i] * jnp.exp(cat(sums[i][1][b])) for i in rng]
            for g in range(g_heads):
                for i in rng:
                    qg = qs[i] * head_masks[g] if g_heads > 1 else qs[i]
                    term = _mm_nt(qg, ks[i]) * level_masks[li]
                    attn[i][g] = term if li == 0 else attn[i][g] + term
        o = [jnp.concatenate([_mm(attn[i][g], vc[i][:, g * LANE:(g + 1) * LANE]) for g in range(g_heads)], axis=1)
             if g_heads > 1 else _mm(attn[i][0], vc[i]) for i in rng]
        for i in rng:
            c8 = sums[i][0][8]
            for r in range(nblk):
                qr = qc[i][r * SUB:(r + 1) * SUB, :]
                for jj in range(SUB):
                    rr = r0s[i] + r * SUB + jj
                    krow = k_s[u, rr:rr + 1, :]
                    crow = c8_s[cis[i], r * SUB + jj:r * SUB + jj + 1, :]
                    pr = qr * krow * jnp.exp(jnp.where(sub >= jj, c8[r] - crow, -jnp.inf))
                    p_s[cis[i], (r * SUB + jj) * SUB:(r * SUB + jj + 1) * SUB, :] = pr
        rs = [jnp.dot(p_s[cis[i]].astype(BF16), ones_bd, preferred_element_type=F32) for i in rng]
        for i in rng:
            od = []
            for r in range(nblk):
                acc = None
                for jj in range(SUB):
                    rr = r0s[i] + r * SUB + jj
                    term = rs[i][(r * SUB + jj) * SUB:(r * SUB + jj + 1) * SUB, :] * v_s[u, rr:rr + 1, :]
                    acc = term if acc is None else acc + term
                od.append(acc)
            o[i] = o[i] + cat(od)
        q_inter = [qc[i] * jnp.exp(cat(sums[i][0][64])) for i in rng]
        k_state = [kc[i] * jnp.exp(cat(sums[i][1][64])) for i in rng]
        upd = [_mm_tn(vc[i], k_state[i]) for i in rng]
        st = state_ref[u]
        for i in rng:
            o_s[u, r0s[i]:r0s[i] + CHUNK, :] = o[i] + _mm_nt(q_inter[i], st)
            decay_last = jnp.exp(sums[i][0][64][nblk - 1][SUB - 1:SUB, :])
            st = st * decay_last + (upd[i] * state_mask if g_heads > 1 else upd[i])
        state_ref[u] = st

    return trip


def _dd_schedule(units, prologue, trip, epilogue):
    prologue(*units[0])
    for n, (u, t) in enumerate(units):
        if n + 1 < len(units):
            prologue(*units[n + 1])
        trip(u, t, n % 2)
        epilogue(u, t)


def _dd_scratch4(nu, tb, dvp):
    return [pltpu.VMEM((nu, dvp, LANE), F32),
            pltpu.VMEM((nu, tb, LANE), F32), pltpu.VMEM((nu, tb, LANE), F32),
            pltpu.VMEM((nu, tb, dvp), F32),
            pltpu.VMEM((nu, tb, LANE), F32),
            pltpu.VMEM((nu, tb, dvp), F32),
            pltpu.VMEM((2 * DD_NPAR, CHUNK, LANE), F32),
            pltpu.VMEM((2 * DD_NPAR, CHUNK * SUB, LANE), F32)]


def _gla4_kernel(q_ref, k_ref, v_ref, lr_ref, gate_ref, w2_ref, b2_ref, nw_ref,
                 o_ref, state_ref, q_s, k_s, v_s, la_s, o_s, c8_s, p_s, *, tb):
    @pl.when(pl.program_id(1) == 0)
    def _():
        state_ref[...] = jnp.zeros_like(state_ref)

    npair = GLA_HEADS // 2
    rp = DD_NPAR * CHUNK
    trip = _dd_make_trip(q_s, k_s, v_s, la_s, o_s, state_ref, c8_s, p_s, 2)

    def prologue(u, t):
        lo = t * rp
        ls = slice(u * LANE, (u + 1) * LANE)
        vs = slice(u * 2 * LANE, (u + 1) * 2 * LANE)
        q_s[u, lo:lo + rp, :] = q_ref[lo:lo + rp, ls].astype(F32) * (GLA_DK ** -0.5)
        k_s[u, lo:lo + rp, :] = k_ref[lo:lo + rp, ls].astype(F32)
        v_s[u, lo:lo + rp, :] = v_ref[lo:lo + rp, vs].astype(F32)
        z = _mm(lr_ref[lo:lo + rp, :], w2_ref[:, ls]) + b2_ref[:, ls]
        la_s[u, lo:lo + rp, :] = _log_sigmoid(z) * (1.0 / GLA_NORMALIZER)

    def epilogue(u, t):
        lo = t * rp
        for g in range(2):
            hs = slice((2 * u + g) * LANE, (2 * u + g + 1) * LANE)
            o = o_s[u, lo:lo + rp, g * LANE:(g + 1) * LANE]
            oh = o * lax.rsqrt(jnp.mean(o * o, axis=-1, keepdims=True) + RMS_EPS) * nw_ref[...]
            o_ref[lo:lo + rp, hs] = (oh * _silu(gate_ref[lo:lo + rp, hs].astype(F32))).astype(o_ref.dtype)

    _dd_schedule([(u, t) for u in range(npair) for t in range(tb // rp)], prologue, trip, epilogue)


def _gla4_call(pb, pf, w2p, b2, norm_w, layer, batch, seq, tb=512):
    tb = min(tb, seq)
    nb = seq // tb
    t = batch * seq
    kern = functools.partial(_gla4_kernel, tb=tb)
    return pl.pallas_call(
        kern,
        grid=(batch, nb),
        in_specs=[pl.BlockSpec((tb, GLA_QK), lambda b, i: (b * nb + i, B_Q // GLA_QK)),
                  pl.BlockSpec((tb, GLA_QK), lambda b, i: (b * nb + i, B_K // GLA_QK)),
                  pl.BlockSpec((tb, GLA_V), lambda b, i: (b * nb + i, B_V // GLA_V)),
                  pl.BlockSpec((tb, LANE), lambda b, i: (b * nb + i, F_LR // LANE)),
                  pl.BlockSpec((tb, GLA_V), lambda b, i: (b * nb + i, B_G // GLA_V)),
                  pl.BlockSpec((None, LANE, GLA_QK), lambda b, i: (layer, 0, 0)),
                  pl.BlockSpec((None, 1, GLA_QK), lambda b, i: (layer, 0, 0)),
                  pl.BlockSpec((None, 1, GLA_DV), lambda b, i: (layer, 0, 0))],
        out_specs=pl.BlockSpec((tb, GLA_V), lambda b, i: (b * nb + i, 0)),
        out_shape=jax.ShapeDtypeStruct((t, GLA_V), BF16),
        scratch_shapes=_dd_scratch4(GLA_HEADS // 2, tb, 2 * LANE),
        compiler_params=_cparams(("parallel", "arbitrary")),
        name="gla",
    )(pb, pb, pb, pf, pb, w2p, b2, norm_w)


def _hgrn4_kernel(q_ref, f_ref, v_ref, gate_ref, lbl_ref, nw_ref,
                  o_ref, state_ref, q_s, k_s, v_s, la_s, o_s, c8_s, p_s, *, tb, layer):
    @pl.when(pl.program_id(1) == 0)
    def _():
        state_ref[...] = jnp.zeros_like(state_ref)

    logits = lbl_ref[...]
    mx = jnp.max(logits, axis=0, keepdims=True)
    ex = jnp.exp(logits - mx)
    p = ex / jnp.sum(ex, axis=0, keepdims=True)
    acc = p[0:1, :]
    for r in range(1, layer + 1):
        acc = acc + p[r:r + 1, :]
    lb = jnp.clip(acc - p[0:1, :], 0.0, 1.0)
    log_lb = jnp.log(jnp.maximum(lb, LB_FLOOR))
    log_1m = jnp.log(1.0 - lb)

    rp = DD_NPAR * CHUNK
    trip = _dd_make_trip(q_s, k_s, v_s, la_s, o_s, state_ref, c8_s, p_s, 1)

    def prologue(u, t):
        lo = t * rp
        hs = slice(u * LANE, (u + 1) * LANE)
        cf = f_ref[lo:lo + rp, hs]
        second = log_1m[:, hs] + _log_sigmoid(cf)
        llb = log_lb[:, hs]
        la_s[u, lo:lo + rp, :] = jnp.maximum(llb, second) + jnp.log(1.0 + jnp.exp(-jnp.abs(llb - second)))
        k_s[u, lo:lo + rp, :] = (1.0 - lb[:, hs]) * _sigmoid_t(-cf)
        q_s[u, lo:lo + rp, :] = _silu(q_ref[lo:lo + rp, hs].astype(F32)) * (HGRN_EXPAND ** -0.5)
        v_s[u, lo:lo + rp, :] = v_ref[lo:lo + rp, hs].astype(F32)

    def epilogue(u, t):
        lo = t * rp
        hs = slice(u * LANE, (u + 1) * LANE)
        o = o_s[u, lo:lo + rp, :]
        oh = o * lax.rsqrt(jnp.mean(o * o, axis=-1, keepdims=True) + RMS_EPS) * nw_ref[...]
        o_ref[lo:lo + rp, hs] = (oh * _silu(gate_ref[lo:lo + rp, hs].astype(F32))).astype(o_ref.dtype)

    _dd_schedule([(u, t) for u in range(HGRN_HEADS) for t in range(tb // rp)], prologue, trip, epilogue)


def _hgrn4_call(pb, pf, lb_logits, norm_w, layer, batch, seq, tb=512):
    tb = min(tb, seq)
    nb = seq // tb
    t = batch * seq
    kern = functools.partial(_hgrn4_kernel, tb=tb, layer=layer)
    return pl.pallas_call(
        kern,
        grid=(batch, nb),
        in_specs=[pl.BlockSpec((tb, HGRN_QK), lambda b, i: (b * nb + i, C_Q // HGRN_QK)),
                  pl.BlockSpec((tb, HGRN_QK), lambda b, i: (b * nb + i, F_CF // HGRN_QK)),
                  pl.BlockSpec((tb, HGRN_V), lambda b, i: (b * nb + i, C_I // HGRN_V)),
                  pl.BlockSpec((tb, HGRN_V), lambda b, i: (b * nb + i, C_G // HGRN_V)),
                  pl.BlockSpec((DEPTH, HGRN_QK), lambda b, i: (0, 0)),
                  pl.BlockSpec((None, 1, HGRN_DV), lambda b, i: (layer, 0, 0))],
        out_specs=pl.BlockSpec((tb, HGRN_V), lambda b, i: (b * nb + i, 0)),
        out_shape=jax.ShapeDtypeStruct((t, HGRN_V), BF16),
        scratch_shapes=_dd_scratch4(HGRN_HEADS, tb, LANE),
        compiler_params=_cparams(("parallel", "arbitrary")),
        name="hgrn",
    )(pb, pf, pb, pb, lb_logits, norm_w)


def _merge_kernel(oa_ref, ob_ref, oc_ref, ma_ref, mb_ref, mc_ref, h_ref,
                  wa_ref, wb_ref, wc_ref, wo_ref, g_ref, b_ref, o_ref):
    y = (_sigmoid_t(ma_ref[...].astype(F32)) * jnp.dot(oa_ref[...], wa_ref[...], preferred_element_type=F32)
         + _sigmoid_t(mb_ref[...].astype(F32)) * jnp.dot(ob_ref[...], wb_ref[...], preferred_element_type=F32)
         + _sigmoid_t(mc_ref[...].astype(F32)) * jnp.dot(oc_ref[...], wc_ref[...], preferred_element_type=F32))
    mix = _mm(y, wo_ref[...])
    o_ref[...] = _layer_norm(ALPHA * h_ref[...] + mix, g_ref[...], b_ref[...])


def _merge_call(o_a, o_b, o_c, pb, h, wa, wb, wc, wo, g, b, layer, tm=512):
    t, d = h.shape
    tm = min(tm, t)

    def row(width):
        return pl.BlockSpec((tm, width), lambda i: (i, 0))

    def wspec(kdim):
        return pl.BlockSpec((None, kdim, d), lambda i: (layer, 0, 0))

    vec = pl.BlockSpec((None, 1, d), lambda i: (layer, 0, 0))
    return pl.pallas_call(
        _merge_kernel,
        grid=(t // tm,),
        in_specs=[row(GDN_V), row(GLA_V), row(HGRN_V),
                  pl.BlockSpec((tm, d), lambda i: (i, M_A // d)),
                  pl.BlockSpec((tm, d), lambda i: (i, M_B // d)),
                  pl.BlockSpec((tm, d), lambda i: (i, M_C // d)),
                  row(d), wspec(GDN_V), wspec(GLA_V), wspec(HGRN_V), wspec(d), vec, vec],
        out_specs=row(d),
        out_shape=jax.ShapeDtypeStruct((t, d), F32),
        compiler_params=_cparams(("parallel",)),
        name="merge",
    )(o_a, o_b, o_c, pb, pb, pb, h, wa, wb, wc, wo, g, b)


def _lane_col(x, idx):
    lane = lax.broadcasted_iota(jnp.int32, x.shape, 1)
    return jnp.sum(jnp.where(lane == idx, x, 0.0), axis=1, keepdims=True)


def _route(scores_t, bias_ref):
    s = [scores_t[e:e + 1, :] for e in range(N_EXPERTS)]
    sel = [s[e] + bias_ref[e:e + 1, 0:1] for e in range(N_EXPERTS)]
    gscore = []
    for g in range(N_GROUPS):
        a, b, c, d = sel[4 * g:4 * g + 4]
        hi1, lo1 = jnp.maximum(a, b), jnp.minimum(a, b)
        hi2, lo2 = jnp.maximum(c, d), jnp.minimum(c, d)
        top1 = jnp.maximum(hi1, hi2)
        top2 = jnp.maximum(jnp.minimum(hi1, hi2), jnp.maximum(lo1, lo2))
        gscore.append(top1 + top2)
    best = gscore[0]
    gidx = jnp.zeros_like(best, dtype=jnp.int32)
    for g in range(1, N_GROUPS):
        take = gscore[g] > best
        best = jnp.where(take, gscore[g], best)
        gidx = jnp.where(take, g, gidx)
    ing, raw = [], []
    for kk in range(EXPERTS_PER_GROUP):
        vs, vr = sel[kk], s[kk]
        for g in range(1, N_GROUPS):
            pick = gidx == g
            vs = jnp.where(pick, sel[4 * g + kk], vs)
            vr = jnp.where(pick, s[4 * g + kk], vr)
        ing.append(vs)
        raw.append(vr)
    b1 = ing[0]
    i1 = jnp.zeros_like(gidx)
    for kk in range(1, EXPERTS_PER_GROUP):
        take = ing[kk] > b1
        b1 = jnp.where(take, ing[kk], b1)
        i1 = jnp.where(take, kk, i1)
    neg = jnp.full_like(b1, -jnp.inf)
    b2 = neg
    i2 = jnp.zeros_like(gidx)
    for kk in range(EXPERTS_PER_GROUP):
        cand = jnp.where(i1 == kk, neg, ing[kk])
        take = cand > b2
        b2 = jnp.where(take, cand, b2)
        i2 = jnp.where(take, kk, i2)
    w1 = raw[0]
    w2 = raw[0]
    for kk in range(1, EXPERTS_PER_GROUP):
        w1 = jnp.where(i1 == kk, raw[kk], w1)
        w2 = jnp.where(i2 == kk, raw[kk], w2)
    tot = w1 + w2
    w1 = w1 / tot
    w2 = w2 / tot
    e1 = gidx * EXPERTS_PER_GROUP + i1
    e2 = gidx * EXPERTS_PER_GROUP + i2
    rows = [jnp.where(e1 == e, w1, 0.0) + jnp.where(e2 == e, w2, 0.0) for e in range(N_EXPERTS)]
    return jnp.concatenate(rows, axis=0)


def _moe_kernel(h_ref, wr_ref, rb_ref, wg_ref, wu_ref, wd_ref, g_ref, b_ref, o_ref, ob_ref,
                comb_ref, xb_ref, *, tm):
    eg = pl.program_id(1)

    @pl.when(eg == 0)
    def _():
        logits_t = _mm_nt_f32(wr_ref[...], h_ref[...])
        comb_t = _route(_sigmoid(logits_t), rb_ref)
        pad = jnp.zeros((LANE - N_EXPERTS, tm), F32)
        comb_ref[...] = jnp.transpose(jnp.concatenate([comb_t, pad], axis=0))
        xb_ref[...] = h_ref[...].astype(BF16)

    x = xb_ref[...]
    comb = comb_ref[...]
    y = None
    for kk in range(MOE_EPS):
        hg = jnp.dot(x, wg_ref[kk], preferred_element_type=F32)
        hu = jnp.dot(x, wu_ref[kk], preferred_element_type=F32)
        cw = _lane_col(comb, eg * MOE_EPS + kk)
        hid = _silu(hg) * hu * cw
        term = jnp.dot(hid.astype(BF16), wd_ref[kk], preferred_element_type=F32)
        y = term if y is None else y + term

    @pl.when(eg == 0)
    def _():
        o_ref[...] = y

    @pl.when(eg > 0)
    def _():
        o_ref[...] += y

    @pl.when(eg == N_EXPERTS // MOE_EPS - 1)
    def _():
        out = _layer_norm(ALPHA * h_ref[...] + o_ref[...], g_ref[...], b_ref[...])
        o_ref[...] = out
        ob_ref[...] = out.astype(BF16)


def _moe_call(h, wr_t, rbias, wg, wu, wd, g, b, layer, tm=1024):
    t, d = h.shape
    tm = min(tm, t)
    kern = functools.partial(_moe_kernel, tm=tm)
    vec = pl.BlockSpec((None, 1, d), lambda i, e: (layer, 0, 0))
    row = pl.BlockSpec((tm, d), lambda i, e: (i, 0))
    return pl.pallas_call(
        kern,
        grid=(t // tm, N_EXPERTS // MOE_EPS),
        in_specs=[row,
                  pl.BlockSpec((N_EXPERTS, d), lambda i, e: (0, 0)),
                  pl.BlockSpec((N_EXPERTS, LANE), lambda i, e: (0, 0)),
                  pl.BlockSpec((None, MOE_EPS, d, D_FF), lambda i, e: (layer, e, 0, 0)),
                  pl.BlockSpec((None, MOE_EPS, d, D_FF), lambda i, e: (layer, e, 0, 0)),
                  pl.BlockSpec((None, MOE_EPS, D_FF, d), lambda i, e: (layer, e, 0, 0)),
                  vec, vec],
        out_specs=[row, row],
        out_shape=[jax.ShapeDtypeStruct((t, d), F32), jax.ShapeDtypeStruct((t, d), BF16)],
        scratch_shapes=[pltpu.VMEM((tm, LANE), F32), pltpu.VMEM((tm, d), BF16)],
        compiler_params=_cparams(("parallel", "arbitrary")),
        name="moe",
    )(h, wr_t, rbias, wg, wu, wd, g, b)


def _pack_w_in(w_in):
    wt = jnp.swapaxes(w_in, 1, 2)
    (a_q, a_k, a_v, a_beta, a_dt, a_g, b_q, b_k, b_v, b_lr, b_g,
     c_q, c_f, c_i, c_g, m_a, m_b, m_c) = jnp.split(wt, SPLIT_POINTS, axis=1)
    depth, _, d = wt.shape
    a_s = jnp.concatenate([a_beta, a_dt, jnp.zeros((depth, LANE - 2 * GDN_HEADS, d), wt.dtype)], 1)
    b_lrp = jnp.concatenate([b_lr, jnp.zeros((depth, LANE - GLA_RANK, d), wt.dtype)], 1)
    wb = jnp.concatenate([m_a, m_b, m_c, a_q, a_k, a_v, a_g, b_q, b_k, b_v, b_g, c_q, c_i, c_g], 1).astype(BF16)
    wf = jnp.concatenate([c_f, a_s, b_lrp], 1).astype(BF16)
    assert wb.shape[1] == NPB and wf.shape[1] == NPF
    return wb, wf


def _prepare(w_in, gdn_conv, gdn_a_log, gdn_dt_bias, gdn_norm, gla_w2, gla_b2, gla_norm, hgrn_lb_logits,
             hgrn_norm, w_br_a, w_br_b, w_br_c, w_out, ln1_g, ln1_b, w_router, router_bias, w_gate, w_up,
             w_down, ln2_g, ln2_b):
    depth = w_in.shape[0]
    d = w_out.shape[-1]
    w_pb, w_pf = _pack_w_in(w_in)
    return dict(
        w_pb=w_pb, w_pf=w_pf,
        gdn_conv=gdn_conv,
        gdn_par=jnp.pad(jnp.stack([gdn_a_log, gdn_dt_bias], axis=1),
                        ((0, 0), (0, SUBLANE - 2), (GDN_HEADS, LANE - 2 * GDN_HEADS))),
        gdn_norm=gdn_norm.reshape(depth, 1, GDN_DV),
        w2p=jnp.concatenate([gla_w2, jnp.zeros((depth, LANE - GLA_RANK, GLA_QK), gla_w2.dtype)], axis=1),
        gla_b2=gla_b2.reshape(depth, 1, GLA_QK),
        gla_norm=gla_norm.reshape(depth, 1, GLA_DV),
        lb_logits=hgrn_lb_logits,
        hgrn_norm=hgrn_norm.reshape(depth, 1, HGRN_DV),
        wa=w_br_a.astype(BF16), wb=w_br_b.astype(BF16), wc=w_br_c.astype(BF16), wo=w_out.astype(BF16),
        ln1_g=ln1_g.reshape(depth, 1, d), ln1_b=ln1_b.reshape(depth, 1, d),
        wr_t=jnp.transpose(w_router),
        rbias=jnp.broadcast_to(router_bias[:, None], (N_EXPERTS, LANE)),
        wg=w_gate, wu=w_up, wd=w_down,
        ln2_g=ln2_g.reshape(depth, 1, d), ln2_b=ln2_b.reshape(depth, 1, d),
    )


def _mixer_block(h, hb, p, layer, batch, seq):
    pb = _inproj_call(hb, p["w_pb"], layer, BF16, 2048, 2048, "inproj_b")
    pf = _inproj_call(hb, p["w_pf"], layer, F32, 1024, NPF, "inproj_f")
    o_a = _gdn_call(pb, pf, p["gdn_conv"], p["gdn_par"], p["gdn_norm"], layer, batch, seq)
    o_b = _gla4_call(pb, pf, p["w2p"], p["gla_b2"], p["gla_norm"], layer, batch, seq)
    o_c = _hgrn4_call(pb, pf, p["lb_logits"], p["hgrn_norm"], layer, batch, seq)
    return _merge_call(o_a, o_b, o_c, pb, h, p["wa"], p["wb"], p["wc"], p["wo"], p["ln1_g"], p["ln1_b"], layer)


MOE_TR = 256
MOE_NBKT = N_GROUPS * 6
NPIECE = D_MODEL // LANE
SC_CH = 64


def _to_token_major(ref, x, rows):
    for j in range(NPIECE):
        ref[pl.ds(j, rows, stride=NPIECE), :] = x[:, j * LANE:(j + 1) * LANE]


def _from_token_major(ref, rows):
    return jnp.concatenate([ref[pl.ds(j, rows, stride=NPIECE), :] for j in range(NPIECE)], axis=1)


def _route_pairs(scores_t, bias_ref):
    s = [scores_t[e:e + 1, :] for e in range(N_EXPERTS)]
    sel = [s[e] + bias_ref[e:e + 1, 0:1] for e in range(N_EXPERTS)]
    gscore = []
    for g in range(N_GROUPS):
        a, b, c, d = sel[4 * g:4 * g + 4]
        hi1, lo1 = jnp.maximum(a, b), jnp.minimum(a, b)
        hi2, lo2 = jnp.maximum(c, d), jnp.minimum(c, d)
        top1 = jnp.maximum(hi1, hi2)
        top2 = jnp.maximum(jnp.minimum(hi1, hi2), jnp.maximum(lo1, lo2))
        gscore.append(top1 + top2)
    best = gscore[0]
    gidx = jnp.zeros_like(best, dtype=jnp.int32)
    for g in range(1, N_GROUPS):
        take = gscore[g] > best
        best = jnp.where(take, gscore[g], best)
        gidx = jnp.where(take, g, gidx)
    ing, raw = [], []
    for kk in range(EXPERTS_PER_GROUP):
        vs, vr = sel[kk], s[kk]
        for g in range(1, N_GROUPS):
            pick = gidx == g
            vs = jnp.where(pick, sel[4 * g + kk], vs)
            vr = jnp.where(pick, s[4 * g + kk], vr)
        ing.append(vs)
        raw.append(vr)
    b1 = ing[0]
    i1 = jnp.zeros_like(gidx)
    for kk in range(1, EXPERTS_PER_GROUP):
        take = ing[kk] > b1
        b1 = jnp.where(take, ing[kk], b1)
        i1 = jnp.where(take, kk, i1)
    neg = jnp.full_like(b1, -jnp.inf)
    b2 = neg
    i2 = jnp.zeros_like(gidx)
    for kk in range(EXPERTS_PER_GROUP):
        cand = jnp.where(i1 == kk, neg, ing[kk])
        take = cand > b2
        b2 = jnp.where(take, cand, b2)
        i2 = jnp.where(take, kk, i2)
    w1 = raw[0]
    w2 = raw[0]
    for kk in range(1, EXPERTS_PER_GROUP):
        w1 = jnp.where(i1 == kk, raw[kk], w1)
        w2 = jnp.where(i2 == kk, raw[kk], w2)
    tot = w1 + w2
    w1 = w1 / tot
    w2 = w2 / tot
    first_lower = i1 < i2
    lo = jnp.where(first_lower, i1, i2)
    hi = jnp.where(first_lower, i2, i1)
    pidx = jnp.where(lo == 0, hi - 1, jnp.where(lo == 1, hi + 1, 5))
    bkt = gidx * 6 + pidx
    return bkt, jnp.where(first_lower, w1, w2), jnp.where(first_lower, w2, w1)


def _moe_route_kernel(h_ref, wr_ref, rb_ref, bkt_ref, rank_ref, wab_ref, cnt_ref, hx_ref, carry_ref, *, tm):
    @pl.when(pl.program_id(0) == 0)
    def _():
        carry_ref[...] = jnp.zeros_like(carry_ref)

    logits_t = _mm_nt_f32(wr_ref[...], h_ref[...])
    bkt, wa, wb = _route_pairs(_sigmoid(logits_t), rb_ref)
    sub = lax.broadcasted_iota(jnp.int32, (32, tm), 0)
    oh = jnp.where(sub == bkt, 1.0, 0.0)
    r = lax.broadcasted_iota(jnp.int32, (tm, tm), 0)
    c = lax.broadcasted_iota(jnp.int32, (tm, tm), 1)
    earlier = jnp.where(r < c, 1.0, 0.0).astype(BF16)
    before = jnp.dot(oh.astype(BF16), earlier, preferred_element_type=F32)
    carry = carry_ref[...]
    rank = jnp.sum(oh * (before + carry[:, 0:1]), axis=0, keepdims=True)
    carry = carry + jnp.sum(oh, axis=1, keepdims=True)
    carry_ref[...] = carry
    cnt_ref[...] = carry
    bkt_ref[...] = bkt
    rank_ref[...] = rank.astype(jnp.int32)
    pad = jnp.zeros((LANE - 2, tm), F32)
    wab_ref[...] = jnp.transpose(jnp.concatenate([wa, wb, pad], axis=0))
    _to_token_major(hx_ref, h_ref[...], tm)


def _moe_tables_kernel(cnt_ref, bkt_ref, rank_ref, pos_ref, tabs_ref, *, t, tr):
    cnt = cnt_ref[...]
    sz = jnp.floor((cnt + (tr - 1)) * (1.0 / tr)) * tr
    r = lax.broadcasted_iota(jnp.int32, (32, 32), 0)
    c = lax.broadcasted_iota(jnp.int32, (32, 32), 1)
    start = jnp.dot(jnp.where(c < r, 1.0, 0.0), sz, preferred_element_type=F32, precision=HIGHEST)
    end = start + sz
    sub = lax.broadcasted_iota(jnp.int32, (32, t), 0)
    pos = jnp.sum(jnp.where(sub == bkt_ref[...], start[:, 0:1], 0.0), axis=0, keepdims=True)
    pos_ref[...] = pos.astype(jnp.int32) + rank_ref[...]
    brow = lax.broadcasted_iota(jnp.int32, (32, LANE), 0)
    tile0 = lax.broadcasted_iota(jnp.int32, (32, LANE), 1).astype(F32) * tr
    tbk = jnp.sum(jnp.where((brow < MOE_NBKT) & (end <= tile0), 1, 0), axis=0, keepdims=True)
    tbk = jnp.minimum(tbk, MOE_NBKT - 1)
    total = end[MOE_NBKT - 1:MOE_NBKT, :]
    valid = jnp.where(tile0[0:1, :] < total, 1, 0)
    g = jnp.where(tbk >= 6, 1, 0) + jnp.where(tbk >= 12, 1, 0) + jnp.where(tbk >= 18, 1, 0)
    p = tbk - 6 * g
    ge3 = jnp.where(p >= 3, 1, 0)
    ge5 = jnp.where(p >= 5, 1, 0)
    ea = 4 * g + ge3 + ge5
    eb = 4 * g + p + 1 - 2 * ge3 - ge5
    zero = jnp.zeros((SUBLANE - 3, LANE), jnp.int32)
    tabs_ref[...] = jnp.concatenate([ea, eb, valid, zero], axis=0)


def _moe_group_kernel(ea_ref, eb_ref, vd_ref, xs_ref, ws_ref, wga_ref, wua_ref, wda_ref,
                      wgb_ref, wub_ref, wdb_ref, ys_ref, wg_s, wu_s, wd_s, *, tr):
    j = pl.program_id(0)
    prev = jnp.maximum(j - 1, 0)

    @pl.when((j == 0) | (ea_ref[j] != ea_ref[prev]))
    def _():
        wg_s[0] = wga_ref[...].astype(BF16)
        wu_s[0] = wua_ref[...].astype(BF16)
        wd_s[0] = wda_ref[...].astype(BF16)

    @pl.when((j == 0) | (eb_ref[j] != eb_ref[prev]))
    def _():
        wg_s[1] = wgb_ref[...].astype(BF16)
        wu_s[1] = wub_ref[...].astype(BF16)
        wd_s[1] = wdb_ref[...].astype(BF16)

    @pl.when(vd_ref[j] > 0)
    def _():
        x = _from_token_major(xs_ref, tr).astype(BF16)
        w = ws_ref[...]

        def ffn(s, cw):
            hg = jnp.dot(x, wg_s[s], preferred_element_type=F32)
            hu = jnp.dot(x, wu_s[s], preferred_element_type=F32)
            hid = _silu(hg) * hu * cw
            return jnp.dot(hid.astype(BF16), wd_s[s], preferred_element_type=F32)

        y = ffn(0, w[:, 0:1]) + ffn(1, w[:, 1:2])
        _to_token_major(ys_ref, y, tr)


def _moe_ln_kernel(h_ref, y_ref, g_ref, b_ref, o_ref, ob_ref, *, tm):
    out = _layer_norm(ALPHA * h_ref[...] + _from_token_major(y_ref, tm), g_ref[...], b_ref[...])
    o_ref[...] = out
    ob_ref[...] = out.astype(BF16)


def _sc_scatter_rows(x3, w2, pos, n_rows):
    from jax.experimental.pallas import tpu_sc as plsc
    t = x3.shape[0]
    info = plsc.get_sparse_core_info()
    nc, ns = info.num_cores, info.num_subcores
    per_w = t // (nc * ns)
    mesh = plsc.VectorSubcoreMesh(core_axis_name="c", subcore_axis_name="s")

    @functools.partial(
        pl.kernel, mesh=mesh,
        out_type=[jax.ShapeDtypeStruct((n_rows,) + x3.shape[1:], x3.dtype),
                  jax.ShapeDtypeStruct((n_rows,) + w2.shape[1:], w2.dtype)],
        scratch_types=[pltpu.VMEM((SC_CH,), jnp.int32), pltpu.VMEM((SC_CH,) + x3.shape[1:], x3.dtype),
                       pltpu.VMEM((SC_CH,) + w2.shape[1:], w2.dtype),
                       pltpu.SemaphoreType.DMA, pltpu.SemaphoreType.DMA])
    def k(x_hbm, w_hbm, idx_hbm, ox_hbm, ow_hbm, idx_v, rows_v, wrows_v, sem_x, sem_w):
        base = (lax.axis_index("s") * nc + lax.axis_index("c")) * per_w

        @pl.loop(0, per_w // SC_CH)
        def _(j):
            off = base + j * SC_CH
            pltpu.sync_copy(idx_hbm.at[pl.ds(off, SC_CH)], idx_v)
            pltpu.sync_copy(x_hbm.at[pl.ds(off, SC_CH)], rows_v)
            pltpu.sync_copy(w_hbm.at[pl.ds(off, SC_CH)], wrows_v)
            cx = pltpu.async_copy(rows_v, ox_hbm.at[idx_v], sem_x)
            cw = pltpu.async_copy(wrows_v, ow_hbm.at[idx_v], sem_w)
            cx.wait()
            cw.wait()

    return k(x3, w2, pos)


def _sc_gather_rows(y3, pos):
    from jax.experimental.pallas import tpu_sc as plsc
    t = pos.shape[0]
    info = plsc.get_sparse_core_info()
    nc, ns = info.num_cores, info.num_subcores
    per_w = t // (nc * ns)
    mesh = plsc.VectorSubcoreMesh(core_axis_name="c", subcore_axis_name="s")

    @functools.partial(
        pl.kernel, mesh=mesh,
        out_type=jax.ShapeDtypeStruct((t,) + y3.shape[1:], y3.dtype),
        scratch_types=[pltpu.VMEM((SC_CH,), jnp.int32), pltpu.VMEM((SC_CH,) + y3.shape[1:], y3.dtype),
                       pltpu.SemaphoreType.DMA])
    def k(y_hbm, idx_hbm, o_hbm, idx_v, rows_v, sem):
        base = (lax.axis_index("s") * nc + lax.axis_index("c")) * per_w

        @pl.loop(0, per_w // SC_CH)
        def _(j):
            off = base + j * SC_CH
            pltpu.sync_copy(idx_hbm.at[pl.ds(off, SC_CH)], idx_v)
            pltpu.async_copy(y_hbm.at[idx_v], rows_v, sem).wait()
            pltpu.sync_copy(rows_v, o_hbm.at[pl.ds(off, SC_CH)])

    return k(y3, pos)


def _moe_sparse(h, wr_t, rbias, wg, wu, wd, g, b, layer, tm=1024):
    t, d = h.shape
    tm = min(tm, t)
    tr = MOE_TR
    nt = t // tr + MOE_NBKT
    n_rows = nt * tr
    row1 = pl.BlockSpec((1, tm), lambda i: (0, i))
    bkt, rank, wab, cnt, hx = pl.pallas_call(
        functools.partial(_moe_route_kernel, tm=tm),
        grid=(t // tm,),
        in_specs=[pl.BlockSpec((tm, d), lambda i: (i, 0)),
                  pl.BlockSpec((N_EXPERTS, d), lambda i: (0, 0)),
                  pl.BlockSpec((N_EXPERTS, LANE), lambda i: (0, 0))],
        out_specs=[row1, row1, pl.BlockSpec((tm, LANE), lambda i: (i, 0)),
                   pl.BlockSpec((32, LANE), lambda i: (0, 0)),
                   pl.BlockSpec((tm * NPIECE, LANE), lambda i: (i, 0))],
        out_shape=[jax.ShapeDtypeStruct((1, t), jnp.int32), jax.ShapeDtypeStruct((1, t), jnp.int32),
                   jax.ShapeDtypeStruct((t, LANE), F32), jax.ShapeDtypeStruct((32, LANE), F32),
                   jax.ShapeDtypeStruct((t * NPIECE, LANE), F32)],
        scratch_shapes=[pltpu.VMEM((32, LANE), F32)],
        compiler_params=_cparams(("arbitrary",)),
        name="moe_route",
    )(h, wr_t, rbias)
    pos, tabs = pl.pallas_call(
        functools.partial(_moe_tables_kernel, t=t, tr=tr),
        out_shape=[jax.ShapeDtypeStruct((1, t), jnp.int32), jax.ShapeDtypeStruct((SUBLANE, LANE), jnp.int32)],
        compiler_params=pltpu.CompilerParams(vmem_limit_bytes=VMEM_LIMIT),
        name="moe_tables",
    )(cnt, bkt, rank)
    pos = pos.reshape(t)
    xs3, ws = _sc_scatter_rows(hx.reshape(t, NPIECE, LANE), wab, pos, n_rows)

    def wspec(which, shape):
        if which == 0:
            return pl.BlockSpec((None, None) + shape, lambda j, ea, eb, vd: (layer, ea[j], 0, 0))
        return pl.BlockSpec((None, None) + shape, lambda j, ea, eb, vd: (layer, eb[j], 0, 0))

    ys = pl.pallas_call(
        functools.partial(_moe_group_kernel, tr=tr),
        grid_spec=pltpu.PrefetchScalarGridSpec(
            num_scalar_prefetch=3,
            grid=(nt,),
            in_specs=[pl.BlockSpec((tr * NPIECE, LANE), lambda j, ea, eb, vd: (j, 0)),
                      pl.BlockSpec((tr, LANE), lambda j, ea, eb, vd: (j, 0)),
                      wspec(0, (d, D_FF)), wspec(0, (d, D_FF)), wspec(0, (D_FF, d)),
                      wspec(1, (d, D_FF)), wspec(1, (d, D_FF)), wspec(1, (D_FF, d))],
            out_specs=pl.BlockSpec((tr * NPIECE, LANE), lambda j, ea, eb, vd: (j, 0)),
            scratch_shapes=[pltpu.VMEM((2, d, D_FF), BF16), pltpu.VMEM((2, d, D_FF), BF16),
                            pltpu.VMEM((2, D_FF, d), BF16)]),
        out_shape=jax.ShapeDtypeStruct((n_rows * NPIECE, LANE), F32),
        compiler_params=_cparams(("arbitrary",)),
        name="moe_experts",
    )(tabs[0, :nt], tabs[1, :nt], tabs[2, :nt], xs3.reshape(n_rows * NPIECE, LANE), ws,
      wg, wu, wd, wg, wu, wd)
    y3 = _sc_gather_rows(ys.reshape(n_rows, NPIECE, LANE), pos)
    vec = pl.BlockSpec((None, 1, d), lambda i: (layer, 0, 0))
    row = pl.BlockSpec((tm, d), lambda i: (i, 0))
    return pl.pallas_call(
        functools.partial(_moe_ln_kernel, tm=tm),
        grid=(t // tm,),
        in_specs=[row, pl.BlockSpec((tm * NPIECE, LANE), lambda i: (i, 0)), vec, vec],
        out_specs=[row, row],
        out_shape=[jax.ShapeDtypeStruct((t, d), F32), jax.ShapeDtypeStruct((t, d), BF16)],
        compiler_params=_cparams(("parallel",)),
        name="moe_ln",
    )(h, y3.reshape(t * NPIECE, LANE), g, b)


def _ffn_block(h, p, layer):
    return _moe_sparse(h, p["wr_t"], p["rbias"], p["wg"], p["wu"], p["wd"], p["ln2_g"], p["ln2_b"], layer)


def kernel(x, ln0_g, ln0_b, w_in, gdn_conv, gdn_a_log, gdn_dt_bias, gdn_norm, gla_w2, gla_b2, gla_norm,
           hgrn_lb_logits, hgrn_norm, w_br_a, w_br_b, w_br_c, w_out, ln1_g, ln1_b, w_router, router_bias,
           w_gate, w_up, w_down, ln2_g, ln2_b):
    batch, seq, d = x.shape
    p = _prepare(w_in, gdn_conv, gdn_a_log, gdn_dt_bias, gdn_norm, gla_w2, gla_b2, gla_norm, hgrn_lb_logits,
                 hgrn_norm, w_br_a, w_br_b, w_br_c, w_out, ln1_g, ln1_b, w_router, router_bias, w_gate, w_up,
                 w_down, ln2_g, ln2_b)
    h, hb = _ln_call(x.reshape(batch * seq, d), ln0_g, ln0_b)
    for layer in range(w_in.shape[0]):
        h = _mixer_block(h, hb, p, layer, batch, seq)
        h, hb = _ffn_block(h, p, layer)
    return h.reshape(batch, seq, d)
```

```python
import functools

import numpy as np
import jax
import jax.numpy as jnp
from jax import lax
from jax.experimental import pallas as pl
from jax.experimental.pallas import tpu as pltpu

F32 = jnp.float32
BF16 = jnp.bfloat16
HIGHEST = lax.Precision.HIGHEST

D_MODEL = 1024
DEPTH = 4
CHUNK = 64
GDN_HEADS, GDN_DK, GDN_DV, CONV_W = 4, 128, 128, 4
GLA_HEADS, GLA_DK, GLA_DV, GLA_RANK, GLA_NORMALIZER = 4, 64, 128, 16, 16.0
HGRN_HEADS, HGRN_EXPAND, HGRN_DV = 4, 128, 128
LB_FLOOR = 1e-30
N_EXPERTS, N_GROUPS, TOP_K, D_FF = 16, 4, 2, 256
EXPERTS_PER_GROUP = N_EXPERTS // N_GROUPS
ALPHA = (2.0 * DEPTH) ** 0.25
LN_EPS = 1e-5
RMS_EPS = 1e-6

GDN_QK = GDN_HEADS * GDN_DK
GDN_V = GDN_HEADS * GDN_DV
GLA_QK = GLA_HEADS * GLA_DK
GLA_V = GLA_HEADS * GLA_DV
HGRN_QK = HGRN_HEADS * HGRN_EXPAND
HGRN_V = HGRN_HEADS * HGRN_DV
SPLIT_SIZES = (GDN_QK, GDN_QK, GDN_V, GDN_HEADS, GDN_HEADS, GDN_V,
               GLA_QK, GLA_QK, GLA_V, GLA_RANK, GLA_V,
               HGRN_QK, HGRN_QK, HGRN_V, HGRN_V,
               D_MODEL, D_MODEL, D_MODEL)
SPLIT_POINTS = tuple(int(v) for v in np.cumsum(SPLIT_SIZES)[:-1])

LANE = 128
SUBLANE = 8
VMEM_LIMIT = 48 * 1024 * 1024

M_A, M_B, M_C = 0, 1024, 2048
A_Q, A_K, A_V, A_G = 3072, 3584, 4096, 4608
B_Q, B_K, B_V, B_G = 5120, 5376, 5632, 6144
C_Q, C_I, C_G = 6656, 7168, 7680
NPB = 8192
F_CF, F_AS, F_LR = 0, 512, 640
NPF = 768

SUB = 8
DD_NPAR = 4
GDN_NPAR = 4
MOE_EPS = 4


def _cparams(sem):
    return pltpu.CompilerParams(dimension_semantics=sem, vmem_limit_bytes=VMEM_LIMIT)


def _mm(a, b):
    return jnp.dot(a.astype(BF16), b.astype(BF16), preferred_element_type=F32)


def _mm_nt(a, b):
    return lax.dot_general(a.astype(BF16), b.astype(BF16), (((1,), (1,)), ((), ())),
                           preferred_element_type=F32)


def _mm_tn(a, b):
    return lax.dot_general(a.astype(BF16), b.astype(BF16), (((0,), (0,)), ((), ())),
                           preferred_element_type=F32)


def _mm_nt_f32(a, b):
    return lax.dot_general(a, b, (((1,), (1,)), ((), ())), preferred_element_type=F32,
                           precision=HIGHEST)


def _split3(x):
    hi = x.astype(BF16)
    r = x - hi.astype(F32)
    mid = r.astype(BF16)
    lo = (r - mid.astype(F32)).astype(BF16)
    return hi, mid, lo


def _mm_01(m01, x):
    hi, mid, lo = _split3(x)
    return (jnp.dot(m01, hi, preferred_element_type=F32) + jnp.dot(m01, mid, preferred_element_type=F32)
            + jnp.dot(m01, lo, preferred_element_type=F32))


def _sigmoid(x):
    return 1.0 / (1.0 + jnp.exp(-x))


def _sigmoid_t(x):
    return 0.5 * jnp.tanh(0.5 * x) + 0.5


def _silu(x):
    return x * _sigmoid_t(x)


def _softplus(x):
    return jnp.maximum(x, 0.0) + jnp.log(1.0 + jnp.exp(-jnp.abs(x)))


def _log_sigmoid(x):
    return -_softplus(-x)


def _layer_norm(x, g, b):
    mu = jnp.mean(x, axis=-1, keepdims=True)
    xc = x - mu
    var = jnp.mean(xc * xc, axis=-1, keepdims=True)
    return xc * lax.rsqrt(var + LN_EPS) * g + b


def _ln_kernel(x_ref, g_ref, b_ref, o_ref, ob_ref):
    y = _layer_norm(x_ref[...], g_ref[...], b_ref[...])
    o_ref[...] = y
    ob_ref[...] = y.astype(BF16)


def _ln_call(x, g, b, tm=512):
    t, d = x.shape
    tm = min(tm, t)
    return pl.pallas_call(
        _ln_kernel,
        grid=(t // tm,),
        in_specs=[pl.BlockSpec((tm, d), lambda i: (i, 0)),
                  pl.BlockSpec((1, d), lambda i: (0, 0)),
                  pl.BlockSpec((1, d), lambda i: (0, 0))],
        out_specs=[pl.BlockSpec((tm, d), lambda i: (i, 0)), pl.BlockSpec((tm, d), lambda i: (i, 0))],
        out_shape=[jax.ShapeDtypeStruct((t, d), F32), jax.ShapeDtypeStruct((t, d), BF16)],
        compiler_params=_cparams(("parallel",)),
        name="ln0",
    )(x, g.reshape(1, d), b.reshape(1, d))


def _inproj_kernel(x_ref, wt_ref, o_ref):
    o_ref[...] = lax.dot_general(x_ref[...], wt_ref[...], (((1,), (1,)), ((), ())),
                                 preferred_element_type=F32).astype(o_ref.dtype)


def _inproj_call(hb, w, layer, out_dtype, tm, tn, name):
    t, d = hb.shape
    tm = min(tm, t)
    n = w.shape[-2]
    return pl.pallas_call(
        _inproj_kernel,
        grid=(n // tn, t // tm),
        in_specs=[pl.BlockSpec((tm, d), lambda j, i: (i, 0)),
                  pl.BlockSpec((None, tn, d), lambda j, i: (layer, j, 0))],
        out_specs=pl.BlockSpec((tm, tn), lambda j, i: (i, j)),
        out_shape=jax.ShapeDtypeStruct((t, n), out_dtype),
        compiler_params=_cparams(("parallel", "parallel")),
        name=name,
    )(hb, w)


def _gdn_kernel(qkv_ref, s_ref, gate_ref, cw_ref, par_ref, nw_ref, o_ref,
                state_ref, tail_ref, xbuf_ref, q_s, k_s, v_s, cumb_s, betab_s, cumrow_s, o_s, *, tb):
    blk = pl.program_id(1)
    nc = tb // CHUNK
    nh = GDN_HEADS
    off0 = SUBLANE - (CONV_W - 1)

    @pl.when(blk == 0)
    def _():
        state_ref[...] = jnp.zeros_like(state_ref)
        tail_ref[...] = jnp.zeros_like(tail_ref)

    row = lax.broadcasted_iota(jnp.int32, (CHUNK, CHUNK), 0)
    col = lax.broadcasted_iota(jnp.int32, (CHUNK, CHUNK), 1)
    incl = col <= row
    strict = col < row
    tri = jnp.where(incl, 1.0, 0.0).astype(BF16)
    eye = jnp.where(col == row, 1.0, 0.0).astype(F32)
    npar = GDN_NPAR
    rp = npar * CHUNK
    xbuf_ref[0:SUBLANE, :] = tail_ref[...]

    def prologue(t):
        lo = t * rp
        xbuf_ref[SUBLANE + lo:SUBLANE + lo + rp, :] = qkv_ref[lo:lo + rp, :].astype(F32)
        for j in range(3 * nh):
            cs = slice(j * LANE, (j + 1) * LANE)
            y = xbuf_ref[off0 + lo:off0 + lo + rp, cs] * cw_ref[0:1, cs]
            for kk in range(1, CONV_W):
                y = y + xbuf_ref[off0 + kk + lo:off0 + kk + lo + rp, cs] * cw_ref[kk:kk + 1, cs]
            y = _silu(y)
            if j < nh:
                q_s[j, lo:lo + rp, :] = (y * lax.rsqrt(jnp.sum(y * y, axis=-1, keepdims=True) + RMS_EPS)
                                         * (GDN_DK ** -0.5))
            elif j < 2 * nh:
                k_s[j - nh, lo:lo + rp, :] = y * lax.rsqrt(jnp.sum(y * y, axis=-1, keepdims=True) + RMS_EPS)
            else:
                v_s[j - 2 * nh, lo:lo + rp, :] = y
        sc = s_ref[lo:lo + rp, :]
        beta_all = _sigmoid_t(sc)
        g_all = -jnp.exp(par_ref[0:1, :]) * _softplus(sc + par_ref[1:2, :])
        cum_all = jnp.concatenate([_mm_01(tri, g_all[c * CHUNK:(c + 1) * CHUNK, :]) for c in range(npar)], axis=0)
        cum_t = jnp.transpose(cum_all)
        for c in range(npar):
            cumrow_s[t * npar + c] = cum_t[0:SUBLANE, c * CHUNK:(c + 1) * CHUNK]
        for h in range(nh):
            cumb_s[h, lo:lo + rp, :] = jnp.broadcast_to(cum_all[:, nh + h:nh + h + 1], (rp, LANE))
            betab_s[h, lo:lo + rp, :] = jnp.broadcast_to(beta_all[:, h:h + 1], (rp, LANE))

    def trip(cp):
        chains = [(cp * npar + cc, h) for cc in range(npar) for h in range(nh)]
        r0s = [c * CHUNK for c, _ in chains]
        qc = [q_s[h, pl.ds(r0, CHUNK), :] for (_, h), r0 in zip(chains, r0s)]
        kc = [k_s[h, pl.ds(r0, CHUNK), :] for (_, h), r0 in zip(chains, r0s)]
        vc = [v_s[h, pl.ds(r0, CHUNK), :] for (_, h), r0 in zip(chains, r0s)]
        cumc = [cumb_s[h, pl.ds(r0, CHUNK), :] for (_, h), r0 in zip(chains, r0s)]
        bc = [betab_s[h, pl.ds(r0, CHUNK), :] for (_, h), r0 in zip(chains, r0s)]
        n = len(chains)
        kk = [_mm_nt(kc[i], kc[i]) for i in range(n)]
        qk = [_mm_nt(qc[i], kc[i]) for i in range(n)]
        decay = []
        for i, (c, h) in enumerate(chains):
            diff = cumc[i][:, 0:CHUNK] - cumrow_s[c][nh + h:nh + h + 1, :]
            decay.append(jnp.where(incl, jnp.exp(jnp.where(incl, diff, 0.0)), 0.0))
        a = [jnp.where(strict, bc[i][:, 0:CHUNK] * kk[i] * decay[i], 0.0) for i in range(n)]
        x = [eye - a[i] for i in range(n)]
        p = [_mm(a[i], a[i]) for i in range(n)]
        for it in range(5):
            x = [x[i] + _mm(x[i], p[i]) for i in range(n)]
            if it < 4:
                p = [_mm(p[i], p[i]) for i in range(n)]
        ecum = [jnp.exp(cumc[i]) for i in range(n)]
        sol = [_mm(x[i], jnp.concatenate([vc[i] * bc[i], kc[i] * (bc[i] * ecum[i])], axis=1)) for i in range(n)]
        attn = [qk[i] * decay[i] for i in range(n)]
        cum_last = [cumc[i][CHUNK - 1:CHUNK, :] for i in range(n)]
        k_state = [kc[i] * jnp.exp(cum_last[i] - cumc[i]) for i in range(n)]
        wqi = [jnp.concatenate([sol[i][:, GDN_DV:GDN_DV + GDN_DK], qc[i] * ecum[i]], axis=0) for i in range(n)]
        for cc in range(npar):
            idx = [cc * nh + h for h in range(nh)]
            s = [state_ref[h] for h in range(nh)]
            wq = [_mm(wqi[i], s[h]) for h, i in enumerate(idx)]
            v_new = [sol[i][:, 0:GDN_DV] - wq[h][0:CHUNK, :] for h, i in enumerate(idx)]
            av = [_mm(attn[i], v_new[h]) for h, i in enumerate(idx)]
            upd = [_mm_tn(k_state[i], v_new[h]) for h, i in enumerate(idx)]
            for h, i in enumerate(idx):
                state_ref[h] = s[h] * jnp.exp(cum_last[i]) + upd[h]
                o_s[pl.ds(r0s[i], CHUNK), h * LANE:(h + 1) * LANE] = wq[h][CHUNK:2 * CHUNK, :] + av[h]

    def epilogue(t):
        lo = t * rp
        for h in range(nh):
            hs = slice(h * LANE, (h + 1) * LANE)
            o = o_s[lo:lo + rp, hs]
            oh = o * lax.rsqrt(jnp.mean(o * o, axis=-1, keepdims=True) + RMS_EPS) * nw_ref[...]
            o_ref[lo:lo + rp, hs] = (oh * _silu(gate_ref[lo:lo + rp, hs].astype(F32))).astype(o_ref.dtype)

    nparts = nc // npar
    prologue(0)
    for t in range(nparts):
        if t + 1 < nparts:
            prologue(t + 1)
        trip(t)
        epilogue(t)
    tail_ref[...] = xbuf_ref[tb:tb + SUBLANE, :]


def _gdn_call(pb, pf, conv_w, par, norm_w, layer, batch, seq, tb=1024):
    tb = min(tb, seq)
    nb = seq // tb
    nc = tb // CHUNK
    t = batch * seq
    wq = 2 * GDN_QK + GDN_V
    kern = functools.partial(_gdn_kernel, tb=tb)
    return pl.pallas_call(
        kern,
        grid=(batch, nb),
        in_specs=[pl.BlockSpec((tb, wq), lambda b, i: (b * nb + i, A_Q // wq)),
                  pl.BlockSpec((tb, LANE), lambda b, i: (b * nb + i, F_AS // LANE)),
                  pl.BlockSpec((tb, GDN_V), lambda b, i: (b * nb + i, A_G // GDN_V)),
                  pl.BlockSpec((None, CONV_W, wq), lambda b, i: (layer, 0, 0)),
                  pl.BlockSpec((None, SUBLANE, LANE), lambda b, i: (layer, 0, 0)),
                  pl.BlockSpec((None, 1, GDN_DV), lambda b, i: (layer, 0, 0))],
        out_specs=pl.BlockSpec((tb, GDN_V), lambda b, i: (b * nb + i, 0)),
        out_shape=jax.ShapeDtypeStruct((t, GDN_V), BF16),
        scratch_shapes=[pltpu.VMEM((GDN_HEADS, GDN_DK, GDN_DV), F32),
                        pltpu.VMEM((SUBLANE, wq), F32),
                        pltpu.VMEM((tb + SUBLANE, wq), F32),
                        pltpu.VMEM((GDN_HEADS, tb, LANE), F32),
                        pltpu.VMEM((GDN_HEADS, tb, LANE), F32),
                        pltpu.VMEM((GDN_HEADS, tb, LANE), F32),
                        pltpu.VMEM((GDN_HEADS, tb, LANE), F32),
                        pltpu.VMEM((GDN_HEADS, tb, LANE), F32),
                        pltpu.VMEM((nc, SUBLANE, CHUNK), F32),
                        pltpu.VMEM((tb, GDN_V), F32)],
        compiler_params=_cparams(("parallel", "arbitrary")),
        name="gdn",
    )(pb, pf, pb, conv_w, par, norm_w)


def _dd_core(q_s, k_s, v_s, la_s, o_s, state_ref, c8_s, p_s, *, tb, g_heads):
    nchunk = tb // CHUNK
    nblk = CHUNK // SUB
    dkh = LANE // g_heads
    dvp = g_heads * LANE
    row = lax.broadcasted_iota(jnp.int32, (CHUNK, CHUNK), 0)
    col = lax.broadcasted_iota(jnp.int32, (CHUNK, CHUNK), 1)
    level_masks = []
    for sh in (5, 4, 3):
        same2b = jnp.right_shift(row, sh + 1) == jnp.right_shift(col, sh + 1)
        upper = (jnp.right_shift(row, sh) & 1) == 1
        lower = (jnp.right_shift(col, sh) & 1) == 0
        level_masks.append(jnp.where(same2b, jnp.where(upper, jnp.where(lower, 1.0, 0.0), 0.0), 0.0))
    tri8 = jnp.where(jnp.right_shift(row, 3) == jnp.right_shift(col, 3),
                     jnp.where(col <= row, 1.0, 0.0), 0.0).astype(BF16)
    lane128 = lax.broadcasted_iota(jnp.int32, (CHUNK, LANE), 1)
    head_masks = [jnp.where((lane128 >= g * dkh) & (lane128 < (g + 1) * dkh), 1.0, 0.0)
                  for g in range(g_heads)]
    sub = lax.broadcasted_iota(jnp.int32, (SUB, LANE), 0)
    dk_sh = dkh.bit_length() - 1
    lane_sh = LANE.bit_length() - 1
    orow = lax.broadcasted_iota(jnp.int32, (LANE, dvp), 0)
    ocol = lax.broadcasted_iota(jnp.int32, (LANE, dvp), 1)
    ones_bd = jnp.where(jnp.right_shift(orow, dk_sh) == jnp.right_shift(ocol, lane_sh), 1.0, 0.0).astype(BF16)
    srow = lax.broadcasted_iota(jnp.int32, (dvp, LANE), 0)
    scol = lax.broadcasted_iota(jnp.int32, (dvp, LANE), 1)
    state_mask = jnp.where(jnp.right_shift(srow, lane_sh) == jnp.right_shift(scol, dk_sh), 1.0, 0.0)
    npar = DD_NPAR

    def block_sums(i):
        c8 = [c8_s[i, b * SUB:(b + 1) * SUB, :] for b in range(nblk)]
        t8 = [c8_s[i, (b + 1) * SUB - 1:(b + 1) * SUB, :] for b in range(nblk)]
        t16 = [t8[2 * b] + t8[2 * b + 1] for b in range(nblk // 2)]
        t32 = [t16[2 * b] + t16[2 * b + 1] for b in range(nblk // 4)]
        t64 = t32[0] + t32[1]
        c16 = [c8[b] + t8[b - 1] if b % 2 else c8[b] for b in range(nblk)]
        c32 = [c16[b] + t16[b // 2 - 1] if (b // 2) % 2 else c16[b] for b in range(nblk)]
        c64 = [c32[b] + t32[0] if b >= nblk // 2 else c32[b] for b in range(nblk)]
        pre = {8: c8, 16: c16, 32: c32, 64: c64}
        suf = {8: [t8[b] - c8[b] for b in range(nblk)],
               16: [t16[b // 2] - c16[b] for b in range(nblk)],
               32: [t32[b // 4] - c32[b] for b in range(nblk)],
               64: [t64 - c64[b] for b in range(nblk)]}
        return pre, suf

    def cat(pieces):
        return jnp.concatenate(pieces, axis=0)

    def group_body(cg, carry):
        rng = range(npar)
        r0s = [pl.multiple_of((cg * npar + i) * CHUNK, CHUNK) for i in rng]
        qc = [q_s[pl.ds(r0, CHUNK), :] for r0 in r0s]
        kc = [k_s[pl.ds(r0, CHUNK), :] for r0 in r0s]
        vc = [v_s[pl.ds(r0, CHUNK), :] for r0 in r0s]
        c8_all = _mm_01(tri8, jnp.concatenate([la_s[pl.ds(r0, CHUNK), :] for r0 in r0s], axis=1))
        for i in rng:
            c8_s[i] = c8_all[:, i * LANE:(i + 1) * LANE]
        sums = [block_sums(i) for i in rng]
        attn = [[None] * g_heads for _ in rng]
        for li, b in enumerate((32, 16, 8)):
            qs = [qc[i] * jnp.exp(cat(sums[i][0][b])) for i in rng]
            ks = [kc[i] * jnp.exp(cat(sums[i][1][b])) for i in rng]
            for g in range(g_heads):
                for i in rng:
                    qg = qs[i] * head_masks[g] if g_heads > 1 else qs[i]
                    term = _mm_nt(qg, ks[i]) * level_masks[li]
                    attn[i][g] = term if li == 0 else attn[i][g] + term
        o = [jnp.concatenate([_mm(attn[i][g], vc[i][:, g * LANE:(g + 1) * LANE]) for g in range(g_heads)], axis=1)
             if g_heads > 1 else _mm(attn[i][0], vc[i]) for i in rng]
        for i in rng:
            c8 = sums[i][0][8]
            for r in range(nblk):
                qr = qc[i][r * SUB:(r + 1) * SUB, :]
                for jj in range(SUB):
                    krow = k_s[pl.ds(r0s[i] + r * SUB + jj, 1), :]
                    crow = c8_s[i, r * SUB + jj:r * SUB + jj + 1, :]
                    m = sub >= jj
                    pr = qr * krow * jnp.exp(jnp.where(m, c8[r] - crow, -jnp.inf))
                    p_s[i, (r * SUB + jj) * SUB:(r * SUB + jj + 1) * SUB, :] = pr
        rs = [jnp.dot(p_s[i].astype(BF16), ones_bd, preferred_element_type=F32) for i in rng]
        for i in rng:
            od = []
            for r in range(nblk):
                acc = None
                for jj in range(SUB):
                    vrow = v_s[pl.ds(r0s[i] + r * SUB + jj, 1), :]
                    term = rs[i][(r * SUB + jj) * SUB:(r * SUB + jj + 1) * SUB, :] * vrow
                    acc = term if acc is None else acc + term
                od.append(acc)
            o[i] = o[i] + cat(od)
        q_inter = [qc[i] * jnp.exp(cat(sums[i][0][64])) for i in rng]
        k_state = [kc[i] * jnp.exp(cat(sums[i][1][64])) for i in rng]
        upd = [_mm_tn(vc[i], k_state[i]) for i in rng]
        st = state_ref[...]
        for i in rng:
            o_s[pl.ds(r0s[i], CHUNK), :] = o[i] + _mm_nt(q_inter[i], st)
            decay_last = jnp.exp(sums[i][0][64][nblk - 1][SUB - 1:SUB, :])
            st = st * decay_last + (upd[i] * state_mask if g_heads > 1 else upd[i])
        state_ref[...] = st
        return carry

    lax.fori_loop(0, nchunk // npar, group_body, 0)


def _gated_rms_out(o_s, gate_ref, nw_ref, o_ref, g_heads):
    for g in range(g_heads):
        hs = slice(g * LANE, (g + 1) * LANE)
        o = o_s[:, hs]
        oh = o * lax.rsqrt(jnp.mean(o * o, axis=-1, keepdims=True) + RMS_EPS) * nw_ref[...]
        o_ref[:, hs] = (oh * _silu(gate_ref[:, hs].astype(F32))).astype(o_ref.dtype)


def _gla_kernel(q_ref, k_ref, v_ref, lr_ref, gate_ref, w2_ref, b2_ref, nw_ref,
                o_ref, state_ref, q_s, k_s, v_s, la_s, o_s, c8_s, p_s, *, tb):
    @pl.when(pl.program_id(2) == 0)
    def _():
        state_ref[...] = jnp.zeros_like(state_ref)

    q_s[...] = q_ref[...].astype(F32) * (GLA_DK ** -0.5)
    k_s[...] = k_ref[...].astype(F32)
    v_s[...] = v_ref[...].astype(F32)
    z = _mm(lr_ref[...], w2_ref[...]) + b2_ref[...]
    la_s[...] = _log_sigmoid(z) * (1.0 / GLA_NORMALIZER)
    _dd_core(q_s, k_s, v_s, la_s, o_s, state_ref, c8_s, p_s, tb=tb, g_heads=2)
    _gated_rms_out(o_s, gate_ref, nw_ref, o_ref, 2)


def _dd_scratch(tb, dvp):
    return [pltpu.VMEM((dvp, LANE), F32),
            pltpu.VMEM((tb, LANE), F32), pltpu.VMEM((tb, LANE), F32),
            pltpu.VMEM((tb, dvp), F32),
            pltpu.VMEM((tb, LANE), F32),
            pltpu.VMEM((tb, dvp), F32),
            pltpu.VMEM((DD_NPAR, CHUNK, LANE), F32),
            pltpu.VMEM((DD_NPAR, CHUNK * SUB, LANE), F32)]


def _gla_call(pb, pf, w2p, b2, norm_w, layer, batch, seq, tb=512):
    tb = min(tb, seq)
    nb = seq // tb
    t = batch * seq
    npair = GLA_HEADS // 2
    kern = functools.partial(_gla_kernel, tb=tb)
    return pl.pallas_call(
        kern,
        grid=(batch, npair, nb),
        in_specs=[pl.BlockSpec((tb, LANE), lambda b, p, i: (b * nb + i, B_Q // LANE + p)),
                  pl.BlockSpec((tb, LANE), lambda b, p, i: (b * nb + i, B_K // LANE + p)),
                  pl.BlockSpec((tb, 2 * LANE), lambda b, p, i: (b * nb + i, B_V // (2 * LANE) + p)),
                  pl.BlockSpec((tb, LANE), lambda b, p, i: (b * nb + i, F_LR // LANE)),
                  pl.BlockSpec((tb, 2 * LANE), lambda b, p, i: (b * nb + i, B_G // (2 * LANE) + p)),
                  pl.BlockSpec((None, LANE, LANE), lambda b, p, i: (layer, 0, p)),
                  pl.BlockSpec((None, 1, LANE), lambda b, p, i: (layer, 0, p)),
                  pl.BlockSpec((None, 1, GLA_DV), lambda b, p, i: (layer, 0, 0))],
        out_specs=pl.BlockSpec((tb, 2 * LANE), lambda b, p, i: (b * nb + i, p)),
        out_shape=jax.ShapeDtypeStruct((t, GLA_V), BF16),
        scratch_shapes=_dd_scratch(tb, 2 * LANE),
        compiler_params=_cparams(("parallel", "parallel", "arbitrary")),
        name="gla",
    )(pb, pb, pb, pf, pb, w2p, b2, norm_w)


def _hgrn_kernel(q_ref, f_ref, v_ref, gate_ref, lbl_ref, nw_ref,
                 o_ref, state_ref, q_s, k_s, v_s, la_s, o_s, c8_s, p_s, *, tb, layer):
    @pl.when(pl.program_id(2) == 0)
    def _():
        state_ref[...] = jnp.zeros_like(state_ref)

    logits = lbl_ref[...]
    mx = jnp.max(logits, axis=0, keepdims=True)
    ex = jnp.exp(logits - mx)
    p = ex / jnp.sum(ex, axis=0, keepdims=True)
    acc = p[0:1, :]
    for r in range(1, layer + 1):
        acc = acc + p[r:r + 1, :]
    lb = jnp.clip(acc - p[0:1, :], 0.0, 1.0)
    log_lb = jnp.log(jnp.maximum(lb, LB_FLOOR))
    log_1m = jnp.log(1.0 - lb)

    cf = f_ref[...]
    second = log_1m + _log_sigmoid(cf)
    la_s[...] = jnp.maximum(log_lb, second) + jnp.log(1.0 + jnp.exp(-jnp.abs(log_lb - second)))
    k_s[...] = (1.0 - lb) * _sigmoid_t(-cf)
    q_s[...] = _silu(q_ref[...].astype(F32)) * (HGRN_EXPAND ** -0.5)
    v_s[...] = v_ref[...].astype(F32)
    _dd_core(q_s, k_s, v_s, la_s, o_s, state_ref, c8_s, p_s, tb=tb, g_heads=1)
    _gated_rms_out(o_s, gate_ref, nw_ref, o_ref, 1)


def _hgrn_call(pb, pf, lb_logits, norm_w, layer, batch, seq, tb=512):
    tb = min(tb, seq)
    nb = seq // tb
    t = batch * seq

    def tok(colblk):
        return pl.BlockSpec((tb, LANE), lambda b, h, i: (b * nb + i, colblk + h))

    kern = functools.partial(_hgrn_kernel, tb=tb, layer=layer)
    return pl.pallas_call(
        kern,
        grid=(batch, HGRN_HEADS, nb),
        in_specs=[tok(C_Q // LANE), tok(F_CF // LANE), tok(C_I // LANE), tok(C_G // LANE),
                  pl.BlockSpec((DEPTH, LANE), lambda b, h, i: (0, h)),
                  pl.BlockSpec((None, 1, HGRN_DV), lambda b, h, i: (layer, 0, 0))],
        out_specs=pl.BlockSpec((tb, LANE), lambda b, h, i: (b * nb + i, h)),
        out_shape=jax.ShapeDtypeStruct((t, HGRN_V), BF16),
        scratch_shapes=_dd_scratch(tb, LANE),
        compiler_params=_cparams(("parallel", "parallel", "arbitrary")),
        name="hgrn",
    )(pb, pf, pb, pb, lb_logits, norm_w)


def _dd_make_trip(q_s, k_s, v_s, la_s, o_s, state_ref, c8_s, p_s, g_heads):
    nblk = CHUNK // SUB
    dkh = LANE // g_heads
    dvp = g_heads * LANE
    npar = DD_NPAR
    row = lax.broadcasted_iota(jnp.int32, (CHUNK, CHUNK), 0)
    col = lax.broadcasted_iota(jnp.int32, (CHUNK, CHUNK), 1)
    level_masks = []
    for sh in (5, 4, 3):
        same2b = jnp.right_shift(row, sh + 1) == jnp.right_shift(col, sh + 1)
        upper = (jnp.right_shift(row, sh) & 1) == 1
        lower = (jnp.right_shift(col, sh) & 1) == 0
        level_masks.append(jnp.where(same2b, jnp.where(upper, jnp.where(lower, 1.0, 0.0), 0.0), 0.0))
    tri8 = jnp.where(jnp.right_shift(row, 3) == jnp.right_shift(col, 3),
                     jnp.where(col <= row, 1.0, 0.0), 0.0).astype(BF16)
    lane128 = lax.broadcasted_iota(jnp.int32, (CHUNK, LANE), 1)
    head_masks = [jnp.where((lane128 >= g * dkh) & (lane128 < (g + 1) * dkh), 1.0, 0.0)
                  for g in range(g_heads)]
    sub = lax.broadcasted_iota(jnp.int32, (SUB, LANE), 0)
    dk_sh = dkh.bit_length() - 1
    lane_sh = LANE.bit_length() - 1
    orow = lax.broadcasted_iota(jnp.int32, (LANE, dvp), 0)
    ocol = lax.broadcasted_iota(jnp.int32, (LANE, dvp), 1)
    ones_bd = jnp.where(jnp.right_shift(orow, dk_sh) == jnp.right_shift(ocol, lane_sh), 1.0, 0.0).astype(BF16)
    srow = lax.broadcasted_iota(jnp.int32, (dvp, LANE), 0)
    scol = lax.broadcasted_iota(jnp.int32, (dvp, LANE), 1)
    state_mask = jnp.where(jnp.right_shift(srow, lane_sh) == jnp.right_shift(scol, dk_sh), 1.0, 0.0)

    def block_sums(ci):
        c8 = [c8_s[ci, b * SUB:(b + 1) * SUB, :] for b in range(nblk)]
        t8 = [c8_s[ci, (b + 1) * SUB - 1:(b + 1) * SUB, :] for b in range(nblk)]
        t16 = [t8[2 * b] + t8[2 * b + 1] for b in range(nblk // 2)]
        t32 = [t16[2 * b] + t16[2 * b + 1] for b in range(nblk // 4)]
        t64 = t32[0] + t32[1]
        c16 = [c8[b] + t8[b - 1] if b % 2 else c8[b] for b in range(nblk)]
        c32 = [c16[b] + t16[b // 2 - 1] if (b // 2) % 2 else c16[b] for b in range(nblk)]
        c64 = [c32[b] + t32[0] if b >= nblk // 2 else c32[b] for b in range(nblk)]
        pre = {8: c8, 16: c16, 32: c32, 64: c64}
        suf = {8: [t8[b] - c8[b] for b in range(nblk)],
               16: [t16[b // 2] - c16[b] for b in range(nblk)],
               32: [t32[b // 4] - c32[b] for b in range(nblk)],
               64: [t64 - c64[b] for b in range(nblk)]}
        return pre, suf

    def cat(pieces):
        return jnp.concatenate(pieces, axis=0)

    def trip(u, t, slot):
        rng = range(npar)
        r0s = [(t * npar + i) * CHUNK for i in rng]
        cis = [slot * npar + i for i in rng]
        qc = [q_s[u, r0:r0 + CHUNK, :] for r0 in r0s]
        kc = [k_s[u, r0:r0 + CHUNK, :] for r0 in r0s]
        vc = [v_s[u, r0:r0 + CHUNK, :] for r0 in r0s]
        c8_all = _mm_01(tri8, jnp.concatenate([la_s[u, r0:r0 + CHUNK, :] for r0 in r0s], axis=1))
        for i in rng:
            c8_s[cis[i]] = c8_all[:, i * LANE:(i + 1) * LANE]
        sums = [block_sums(cis[i]) for i in rng]
        attn = [[None] * g_heads for _ in rng]
        for li, b in enumerate((32, 16, 8)):
            qs = [qc[i] * jnp.exp(cat(sums[i][0][b])) for i in rng]
            ks = [kc[i] * jnp.exp(cat(sums[i][1][b])) for i in rng]
            for g in range(g_heads):
                for i in rng:
                    qg = qs[i] * head_masks[g] if g_heads > 1 else qs[i]
                    term = _mm_nt(qg, ks[i]) * level_masks[li]
                    attn[i][g] = term if li == 0 else attn[i][g] + term
        o = [jnp.concatenate([_mm(attn[i][g], vc[i][:, g * LANE:(g + 1) * LANE]) for g in range(g_heads)], axis=1)
             if g_heads > 1 else _mm(attn[i][0], vc[i]) for i in rng]
        for i in rng:
            c8 = sums[i][0][8]
            for r in range(nblk):
                qr = qc[i][r * SUB:(r + 1) * SUB, :]
                for jj in range(SUB):
                    rr = r0s[i] + r * SUB + jj
                    krow = k_s[u, rr:rr + 1, :]
                    crow = c8_s[cis[i], r * SUB + jj:r * SUB + jj + 1, :]
                    pr = qr * krow * jnp.exp(jnp.where(sub >= jj, c8[r] - crow, -jnp.inf))
                    p_s[cis[i], (r * SUB + jj) * SUB:(r * SUB + jj + 1) * SUB, :] = pr
        rs = [jnp.dot(p_s[cis[i]].astype(BF16), ones_bd, preferred_element_type=F32) for i in rng]
        for i in rng:
            od = []
            for r in range(nblk):
                acc = None
                for jj in range(SUB):
                    rr = r0s[i] + r * SUB + jj
                    term = rs[i][(r * SUB + jj) * SUB:(r * SUB + jj + 1) * SUB, :] * v_s[u, rr:rr + 1, :]
                    acc = term if acc is None else acc + term
                od.append(acc)
            o[i] = o[i] + cat(od)
        q_inter = [qc[i] * jnp.exp(cat(sums[i][0][64])) for i in rng]
        k_state = [kc[i] * jnp.exp(cat(sums[i][1][64])) for i in rng]
        upd = [_mm_tn(vc[i], k_state[i]) for i in rng]
        st = state_ref[u]
        for i in rng:
            o_s[u, r0s[i]:r0s[i] + CHUNK, :] = o[i] + _mm_nt(q_inter[i], st)
            decay_last = jnp.exp(sums[i][0][64][nblk - 1][SUB - 1:SUB, :])
            st = st * decay_last + (upd[i] * state_mask if g_heads > 1 else upd[i])
        state_ref[u] = st

    return trip


def _dd_schedule(units, prologue, trip, epilogue):
    prologue(*units[0])
    for n, (u, t) in enumerate(units):
        if n + 1 < len(units):
            prologue(*units[n + 1])
        trip(u, t, n % 2)
        epilogue(u, t)


def _dd_scratch4(nu, tb, dvp):
    return [pltpu.VMEM((nu, dvp, LANE), F32),
            pltpu.VMEM((nu, tb, LANE), F32), pltpu.VMEM((nu, tb, LANE), F32),
            pltpu.VMEM((nu, tb, dvp), F32),
            pltpu.VMEM((nu, tb, LANE), F32),
            pltpu.VMEM((nu, tb, dvp), F32),
            pltpu.VMEM((2 * DD_NPAR, CHUNK, LANE), F32),
            pltpu.VMEM((2 * DD_NPAR, CHUNK * SUB, LANE), F32)]


def _gla4_kernel(q_ref, k_ref, v_ref, lr_ref, gate_ref, w2_ref, b2_ref, nw_ref,
                 o_ref, state_ref, q_s, k_s, v_s, la_s, o_s, c8_s, p_s, *, tb):
    @pl.when(pl.program_id(1) == 0)
    def _():
        state_ref[...] = jnp.zeros_like(state_ref)

    npair = GLA_HEADS // 2
    rp = DD_NPAR * CHUNK
    trip = _dd_make_trip(q_s, k_s, v_s, la_s, o_s, state_ref, c8_s, p_s, 2)

    def prologue(u, t):
        lo = t * rp
        ls = slice(u * LANE, (u + 1) * LANE)
        vs = slice(u * 2 * LANE, (u + 1) * 2 * LANE)
        q_s[u, lo:lo + rp, :] = q_ref[lo:lo + rp, ls].astype(F32) * (GLA_DK ** -0.5)
        k_s[u, lo:lo + rp, :] = k_ref[lo:lo + rp, ls].astype(F32)
        v_s[u, lo:lo + rp, :] = v_ref[lo:lo + rp, vs].astype(F32)
        z = _mm(lr_ref[lo:lo + rp, :], w2_ref[:, ls]) + b2_ref[:, ls]
        la_s[u, lo:lo + rp, :] = _log_sigmoid(z) * (1.0 / GLA_NORMALIZER)

    def epilogue(u, t):
        lo = t * rp
        for g in range(2):
            hs = slice((2 * u + g) * LANE, (2 * u + g + 1) * LANE)
            o = o_s[u, lo:lo + rp, g * LANE:(g + 1) * LANE]
            oh = o * lax.rsqrt(jnp.mean(o * o, axis=-1, keepdims=True) + RMS_EPS) * nw_ref[...]
            o_ref[lo:lo + rp, hs] = (oh * _silu(gate_ref[lo:lo + rp, hs].astype(F32))).astype(o_ref.dtype)

    _dd_schedule([(u, t) for u in range(npair) for t in range(tb // rp)], prologue, trip, epilogue)


def _gla4_call(pb, pf, w2p, b2, norm_w, layer, batch, seq, tb=1024):
    tb = min(tb, seq)
    nb = seq // tb
    t = batch * seq
    kern = functools.partial(_gla4_kernel, tb=tb)
    return pl.pallas_call(
        kern,
        grid=(batch, nb),
        in_specs=[pl.BlockSpec((tb, GLA_QK), lambda b, i: (b * nb + i, B_Q // GLA_QK)),
                  pl.BlockSpec((tb, GLA_QK), lambda b, i: (b * nb + i, B_K // GLA_QK)),
                  pl.BlockSpec((tb, GLA_V), lambda b, i: (b * nb + i, B_V // GLA_V)),
                  pl.BlockSpec((tb, LANE), lambda b, i: (b * nb + i, F_LR // LANE)),
                  pl.BlockSpec((tb, GLA_V), lambda b, i: (b * nb + i, B_G // GLA_V)),
                  pl.BlockSpec((None, LANE, GLA_QK), lambda b, i: (layer, 0, 0)),
                  pl.BlockSpec((None, 1, GLA_QK), lambda b, i: (layer, 0, 0)),
                  pl.BlockSpec((None, 1, GLA_DV), lambda b, i: (layer, 0, 0))],
        out_specs=pl.BlockSpec((tb, GLA_V), lambda b, i: (b * nb + i, 0)),
        out_shape=jax.ShapeDtypeStruct((t, GLA_V), BF16),
        scratch_shapes=_dd_scratch4(GLA_HEADS // 2, tb, 2 * LANE),
        compiler_params=_cparams(("parallel", "arbitrary")),
        name="gla",
    )(pb, pb, pb, pf, pb, w2p, b2, norm_w)


def _hgrn4_kernel(q_ref, f_ref, v_ref, gate_ref, lbl_ref, nw_ref,
                  o_ref, state_ref, q_s, k_s, v_s, la_s, o_s, c8_s, p_s, *, tb, layer):
    @pl.when(pl.program_id(1) == 0)
    def _():
        state_ref[...] = jnp.zeros_like(state_ref)

    logits = lbl_ref[...]
    mx = jnp.max(logits, axis=0, keepdims=True)
    ex = jnp.exp(logits - mx)
    p = ex / jnp.sum(ex, axis=0, keepdims=True)
    acc = p[0:1, :]
    for r in range(1, layer + 1):
        acc = acc + p[r:r + 1, :]
    lb = jnp.clip(acc - p[0:1, :], 0.0, 1.0)
    log_lb = jnp.log(jnp.maximum(lb, LB_FLOOR))
    log_1m = jnp.log(1.0 - lb)

    rp = DD_NPAR * CHUNK
    trip = _dd_make_trip(q_s, k_s, v_s, la_s, o_s, state_ref, c8_s, p_s, 1)

    def prologue(u, t):
        lo = t * rp
        hs = slice(u * LANE, (u + 1) * LANE)
        cf = f_ref[lo:lo + rp, hs]
        second = log_1m[:, hs] + _log_sigmoid(cf)
        llb = log_lb[:, hs]
        la_s[u, lo:lo + rp, :] = jnp.maximum(llb, second) + jnp.log(1.0 + jnp.exp(-jnp.abs(llb - second)))
        k_s[u, lo:lo + rp, :] = (1.0 - lb[:, hs]) * _sigmoid_t(-cf)
        q_s[u, lo:lo + rp, :] = _silu(q_ref[lo:lo + rp, hs].astype(F32)) * (HGRN_EXPAND ** -0.5)
        v_s[u, lo:lo + rp, :] = v_ref[lo:lo + rp, hs].astype(F32)

    def epilogue(u, t):
        lo = t * rp
        hs = slice(u * LANE, (u + 1) * LANE)
        o = o_s[u, lo:lo + rp, :]
        oh = o * lax.rsqrt(jnp.mean(o * o, axis=-1, keepdims=True) + RMS_EPS) * nw_ref[...]
        o_ref[lo:lo + rp, hs] = (oh * _silu(gate_ref[lo:lo + rp, hs].astype(F32))).astype(o_ref.dtype)

    _dd_schedule([(u, t) for u in range(HGRN_HEADS) for t in range(tb // rp)], prologue, trip, epilogue)


def _hgrn4_call(pb, pf, lb_logits, norm_w, layer, batch, seq, tb=1024):
    tb = min(tb, seq)
    nb = seq // tb
    t = batch * seq
    kern = functools.partial(_hgrn4_kernel, tb=tb, layer=layer)
    return pl.pallas_call(
        kern,
        grid=(batch, nb),
        in_specs=[pl.BlockSpec((tb, HGRN_QK), lambda b, i: (b * nb + i, C_Q // HGRN_QK)),
                  pl.BlockSpec((tb, HGRN_QK), lambda b, i: (b * nb + i, F_CF // HGRN_QK)),
                  pl.BlockSpec((tb, HGRN_V), lambda b, i: (b * nb + i, C_I // HGRN_V)),
                  pl.BlockSpec((tb, HGRN_V), lambda b, i: (b * nb + i, C_G // HGRN_V)),
                  pl.BlockSpec((DEPTH, HGRN_QK), lambda b, i: (0, 0)),
                  pl.BlockSpec((None, 1, HGRN_DV), lambda b, i: (layer, 0, 0))],
        out_specs=pl.BlockSpec((tb, HGRN_V), lambda b, i: (b * nb + i, 0)),
        out_shape=jax.ShapeDtypeStruct((t, HGRN_V), BF16),
        scratch_shapes=_dd_scratch4(HGRN_HEADS, tb, LANE),
        compiler_params=_cparams(("parallel", "arbitrary")),
        name="hgrn",
    )(pb, pf, pb, pb, lb_logits, norm_w)


def _merge_kernel(oa_ref, ob_ref, oc_ref, ma_ref, mb_ref, mc_ref, h_ref,
                  wa_ref, wb_ref, wc_ref, wo_ref, g_ref, b_ref, o_ref):
    y = (_sigmoid_t(ma_ref[...].astype(F32)) * jnp.dot(oa_ref[...], wa_ref[...], preferred_element_type=F32)
         + _sigmoid_t(mb_ref[...].astype(F32)) * jnp.dot(ob_ref[...], wb_ref[...], preferred_element_type=F32)
         + _sigmoid_t(mc_ref[...].astype(F32)) * jnp.dot(oc_ref[...], wc_ref[...], preferred_element_type=F32))
    mix = _mm(y, wo_ref[...])
    o_ref[...] = _layer_norm(ALPHA * h_ref[...] + mix, g_ref[...], b_ref[...])


def _merge_call(o_a, o_b, o_c, pb, h, wa, wb, wc, wo, g, b, layer, tm=512):
    t, d = h.shape
    tm = min(tm, t)

    def row(width):
        return pl.BlockSpec((tm, width), lambda i: (i, 0))

    def wspec(kdim):
        return pl.BlockSpec((None, kdim, d), lambda i: (layer, 0, 0))

    vec = pl.BlockSpec((None, 1, d), lambda i: (layer, 0, 0))
    return pl.pallas_call(
        _merge_kernel,
        grid=(t // tm,),
        in_specs=[row(GDN_V), row(GLA_V), row(HGRN_V),
                  pl.BlockSpec((tm, d), lambda i: (i, M_A // d)),
                  pl.BlockSpec((tm, d), lambda i: (i, M_B // d)),
                  pl.BlockSpec((tm, d), lambda i: (i, M_C // d)),
                  row(d), wspec(GDN_V), wspec(GLA_V), wspec(HGRN_V), wspec(d), vec, vec],
        out_specs=row(d),
        out_shape=jax.ShapeDtypeStruct((t, d), F32),
        compiler_params=_cparams(("parallel",)),
        name="merge",
    )(o_a, o_b, o_c, pb, pb, pb, h, wa, wb, wc, wo, g, b)


def _lane_col(x, idx):
    lane = lax.broadcasted_iota(jnp.int32, x.shape, 1)
    return jnp.sum(jnp.where(lane == idx, x, 0.0), axis=1, keepdims=True)


def _route(scores_t, bias_ref):
    s = [scores_t[e:e + 1, :] for e in range(N_EXPERTS)]
    sel = [s[e] + bias_ref[e:e + 1, 0:1] for e in range(N_EXPERTS)]
    gscore = []
    for g in range(N_GROUPS):
        a, b, c, d = sel[4 * g:4 * g + 4]
        hi1, lo1 = jnp.maximum(a, b), jnp.minimum(a, b)
        hi2, lo2 = jnp.maximum(c, d), jnp.minimum(c, d)
        top1 = jnp.maximum(hi1, hi2)
        top2 = jnp.maximum(jnp.minimum(hi1, hi2), jnp.maximum(lo1, lo2))
        gscore.append(top1 + top2)
    best = gscore[0]
    gidx = jnp.zeros_like(best, dtype=jnp.int32)
    for g in range(1, N_GROUPS):
        take = gscore[g] > best
        best = jnp.where(take, gscore[g], best)
        gidx = jnp.where(take, g, gidx)
    ing, raw = [], []
    for kk in range(EXPERTS_PER_GROUP):
        vs, vr = sel[kk], s[kk]
        for g in range(1, N_GROUPS):
            pick = gidx == g
            vs = jnp.where(pick, sel[4 * g + kk], vs)
            vr = jnp.where(pick, s[4 * g + kk], vr)
        ing.append(vs)
        raw.append(vr)
    b1 = ing[0]
    i1 = jnp.zeros_like(gidx)
    for kk in range(1, EXPERTS_PER_GROUP):
        take = ing[kk] > b1
        b1 = jnp.where(take, ing[kk], b1)
        i1 = jnp.where(take, kk, i1)
    neg = jnp.full_like(b1, -jnp.inf)
    b2 = neg
    i2 = jnp.zeros_like(gidx)
    for kk in range(EXPERTS_PER_GROUP):
        cand = jnp.where(i1 == kk, neg, ing[kk])
        take = cand > b2
        b2 = jnp.where(take, cand, b2)
        i2 = jnp.where(take, kk, i2)
    w1 = raw[0]
    w2 = raw[0]
    for kk in range(1, EXPERTS_PER_GROUP):
        w1 = jnp.where(i1 == kk, raw[kk], w1)
        w2 = jnp.where(i2 == kk, raw[kk], w2)
    tot = w1 + w2
    w1 = w1 / tot
    w2 = w2 / tot
    e1 = gidx * EXPERTS_PER_GROUP + i1
    e2 = gidx * EXPERTS_PER_GROUP + i2
    rows = [jnp.where(e1 == e, w1, 0.0) + jnp.where(e2 == e, w2, 0.0) for e in range(N_EXPERTS)]
    return jnp.concatenate(rows, axis=0)


def _moe_kernel(h_ref, wr_ref, rb_ref, wg_ref, wu_ref, wd_ref, g_ref, b_ref, o_ref, ob_ref,
                comb_ref, xb_ref, *, tm):
    eg = pl.program_id(1)

    @pl.when(eg == 0)
    def _():
        logits_t = _mm_nt_f32(wr_ref[...], h_ref[...])
        comb_t = _route(_sigmoid(logits_t), rb_ref)
        pad = jnp.zeros((LANE - N_EXPERTS, tm), F32)
        comb_ref[...] = jnp.transpose(jnp.concatenate([comb_t, pad], axis=0))
        xb_ref[...] = h_ref[...].astype(BF16)

    x = xb_ref[...]
    comb = comb_ref[...]
    y = None
    for kk in range(MOE_EPS):
        hg = jnp.dot(x, wg_ref[kk], preferred_element_type=F32)
        hu = jnp.dot(x, wu_ref[kk], preferred_element_type=F32)
        cw = _lane_col(comb, eg * MOE_EPS + kk)
        hid = _silu(hg) * hu * cw
        term = jnp.dot(hid.astype(BF16), wd_ref[kk], preferred_element_type=F32)
        y = term if y is None else y + term

    @pl.when(eg == 0)
    def _():
        o_ref[...] = y

    @pl.when(eg > 0)
    def _():
        o_ref[...] += y

    @pl.when(eg == N_EXPERTS // MOE_EPS - 1)
    def _():
        out = _layer_norm(ALPHA * h_ref[...] + o_ref[...], g_ref[...], b_ref[...])
        o_ref[...] = out
        ob_ref[...] = out.astype(BF16)


def _moe_call(h, wr_t, rbias, wg, wu, wd, g, b, layer, tm=1024):
    t, d = h.shape
    tm = min(tm, t)
    kern = functools.partial(_moe_kernel, tm=tm)
    vec = pl.BlockSpec((None, 1, d), lambda i, e: (layer, 0, 0))
    row = pl.BlockSpec((tm, d), lambda i, e: (i, 0))
    return pl.pallas_call(
        kern,
        grid=(t // tm, N_EXPERTS // MOE_EPS),
        in_specs=[row,
                  pl.BlockSpec((N_EXPERTS, d), lambda i, e: (0, 0)),
                  pl.BlockSpec((N_EXPERTS, LANE), lambda i, e: (0, 0)),
                  pl.BlockSpec((None, MOE_EPS, d, D_FF), lambda i, e: (layer, e, 0, 0)),
                  pl.BlockSpec((None, MOE_EPS, d, D_FF), lambda i, e: (layer, e, 0, 0)),
                  pl.BlockSpec((None, MOE_EPS, D_FF, d), lambda i, e: (layer, e, 0, 0)),
                  vec, vec],
        out_specs=[row, row],
        out_shape=[jax.ShapeDtypeStruct((t, d), F32), jax.ShapeDtypeStruct((t, d), BF16)],
        scratch_shapes=[pltpu.VMEM((tm, LANE), F32), pltpu.VMEM((tm, d), BF16)],
        compiler_params=_cparams(("parallel", "arbitrary")),
        name="moe",
    )(h, wr_t, rbias, wg, wu, wd, g, b)


def _pack_w_in(w_in):
    wt = jnp.swapaxes(w_in, 1, 2)
    (a_q, a_k, a_v, a_beta, a_dt, a_g, b_q, b_k, b_v, b_lr, b_g,
     c_q, c_f, c_i, c_g, m_a, m_b, m_c) = jnp.split(wt, SPLIT_POINTS, axis=1)
    depth, _, d = wt.shape
    a_s = jnp.concatenate([a_beta, a_dt, jnp.zeros((depth, LANE - 2 * GDN_HEADS, d), wt.dtype)], 1)
    b_lrp = jnp.concatenate([b_lr, jnp.zeros((depth, LANE - GLA_RANK, d), wt.dtype)], 1)
    wb = jnp.concatenate([m_a, m_b, m_c, a_q, a_k, a_v, a_g, b_q, b_k, b_v, b_g, c_q, c_i, c_g], 1).astype(BF16)
    wf = jnp.concatenate([c_f, a_s, b_lrp], 1).astype(BF16)
    assert wb.shape[1] == NPB and wf.shape[1] == NPF
    return wb, wf


def _prepare(w_in, gdn_conv, gdn_a_log, gdn_dt_bias, gdn_norm, gla_w2, gla_b2, gla_norm, hgrn_lb_logits,
             hgrn_norm, w_br_a, w_br_b, w_br_c, w_out, ln1_g, ln1_b, w_router, router_bias, w_gate, w_up,
             w_down, ln2_g, ln2_b):
    depth = w_in.shape[0]
    d = w_out.shape[-1]
    w_pb, w_pf = _pack_w_in(w_in)
    return dict(
        w_pb=w_pb, w_pf=w_pf,
        gdn_conv=gdn_conv,
        gdn_par=jnp.pad(jnp.stack([gdn_a_log, gdn_dt_bias], axis=1),
                        ((0, 0), (0, SUBLANE - 2), (GDN_HEADS, LANE - 2 * GDN_HEADS))),
        gdn_norm=gdn_norm.reshape(depth, 1, GDN_DV),
        w2p=jnp.concatenate([gla_w2, jnp.zeros((depth, LANE - GLA_RANK, GLA_QK), gla_w2.dtype)], axis=1),
        gla_b2=gla_b2.reshape(depth, 1, GLA_QK),
        gla_norm=gla_norm.reshape(depth, 1, GLA_DV),
        lb_logits=hgrn_lb_logits,
        hgrn_norm=hgrn_norm.reshape(depth, 1, HGRN_DV),
        wa=w_br_a.astype(BF16), wb=w_br_b.astype(BF16), wc=w_br_c.astype(BF16), wo=w_out.astype(BF16),
        ln1_g=ln1_g.reshape(depth, 1, d), ln1_b=ln1_b.reshape(depth, 1, d),
        wr_t=jnp.transpose(w_router),
        rbias=jnp.broadcast_to(router_bias[:, None], (N_EXPERTS, LANE)),
        wg=w_gate, wu=w_up, wd=w_down,
        ln2_g=ln2_g.reshape(depth, 1, d), ln2_b=ln2_b.reshape(depth, 1, d),
    )


def _mixer_block(h, hb, p, layer, batch, seq):
    pb = _inproj_call(hb, p["w_pb"], layer, BF16, 2048, 2048, "inproj_b")
    pf = _inproj_call(hb, p["w_pf"], layer, F32, 1024, NPF, "inproj_f")
    o_a = _gdn_call(pb, pf, p["gdn_conv"], p["gdn_par"], p["gdn_norm"], layer, batch, seq)
    o_b = _gla4_call(pb, pf, p["w2p"], p["gla_b2"], p["gla_norm"], layer, batch, seq)
    o_c = _hgrn4_call(pb, pf, p["lb_logits"], p["hgrn_norm"], layer, batch, seq)
    return _merge_call(o_a, o_b, o_c, pb, h, p["wa"], p["wb"], p["wc"], p["wo"], p["ln1_g"], p["ln1_b"], layer)


MOE_TR = 256
MOE_NBKT = N_GROUPS * 6
NPIECE = D_MODEL // LANE
SC_CH = 64


def _to_token_major(ref, x, rows):
    for j in range(NPIECE):
        ref[pl.ds(j, rows, stride=NPIECE), :] = x[:, j * LANE:(j + 1) * LANE]


def _from_token_major(ref, rows):
    return jnp.concatenate([ref[pl.ds(j, rows, stride=NPIECE), :] for j in range(NPIECE)], axis=1)


def _route_pairs(scores_t, bias_ref):
    s = [scores_t[e:e + 1, :] for e in range(N_EXPERTS)]
    sel = [s[e] + bias_ref[e:e + 1, 0:1] for e in range(N_EXPERTS)]
    gscore = []
    for g in range(N_GROUPS):
        a, b, c, d = sel[4 * g:4 * g + 4]
        hi1, lo1 = jnp.maximum(a, b), jnp.minimum(a, b)
        hi2, lo2 = jnp.maximum(c, d), jnp.minimum(c, d)
        top1 = jnp.maximum(hi1, hi2)
        top2 = jnp.maximum(jnp.minimum(hi1, hi2), jnp.maximum(lo1, lo2))
        gscore.append(top1 + top2)
    best = gscore[0]
    gidx = jnp.zeros_like(best, dtype=jnp.int32)
    for g in range(1, N_GROUPS):
        take = gscore[g] > best
        best = jnp.where(take, gscore[g], best)
        gidx = jnp.where(take, g, gidx)
    ing, raw = [], []
    for kk in range(EXPERTS_PER_GROUP):
        vs, vr = sel[kk], s[kk]
        for g in range(1, N_GROUPS):
            pick = gidx == g
            vs = jnp.where(pick, sel[4 * g + kk], vs)
            vr = jnp.where(pick, s[4 * g + kk], vr)
        ing.append(vs)
        raw.append(vr)
    b1 = ing[0]
    i1 = jnp.zeros_like(gidx)
    for kk in range(1, EXPERTS_PER_GROUP):
        take = ing[kk] > b1
        b1 = jnp.where(take, ing[kk], b1)
        i1 = jnp.where(take, kk, i1)
    neg = jnp.full_like(b1, -jnp.inf)
    b2 = neg
    i2 = jnp.zeros_like(gidx)
    for kk in range(EXPERTS_PER_GROUP):
        cand = jnp.where(i1 == kk, neg, ing[kk])
        take = cand > b2
        b2 = jnp.where(take, cand, b2)
        i2 = jnp.where(take, kk, i2)
    w1 = raw[0]
    w2 = raw[0]
    for kk in range(1, EXPERTS_PER_GROUP):
        w1 = jnp.where(i1 == kk, raw[kk], w1)
        w2 = jnp.where(i2 == kk, raw[kk], w2)
    tot = w1 + w2
    w1 = w1 / tot
    w2 = w2 / tot
    first_lower = i1 < i2
    lo = jnp.where(first_lower, i1, i2)
    hi = jnp.where(first_lower, i2, i1)
    pidx = jnp.where(lo == 0, hi - 1, jnp.where(lo == 1, hi + 1, 5))
    bkt = gidx * 6 + pidx
    return bkt, jnp.where(first_lower, w1, w2), jnp.where(first_lower, w2, w1)


def _moe_route_kernel(h_ref, wr_ref, rb_ref, bkt_ref, rank_ref, wab_ref, cnt_ref, hx_ref, carry_ref, *, tm):
    @pl.when(pl.program_id(0) == 0)
    def _():
        carry_ref[...] = jnp.zeros_like(carry_ref)

    logits_t = _mm_nt_f32(wr_ref[...], h_ref[...])
    bkt, wa, wb = _route_pairs(_sigmoid(logits_t), rb_ref)
    sub = lax.broadcasted_iota(jnp.int32, (32, tm), 0)
    oh = jnp.where(sub == bkt, 1.0, 0.0)
    r = lax.broadcasted_iota(jnp.int32, (tm, tm), 0)
    c = lax.broadcasted_iota(jnp.int32, (tm, tm), 1)
    earlier = jnp.where(r < c, 1.0, 0.0).astype(BF16)
    before = jnp.dot(oh.astype(BF16), earlier, preferred_element_type=F32)
    carry = carry_ref[...]
    rank = jnp.sum(oh * (before + carry[:, 0:1]), axis=0, keepdims=True)
    carry = carry + jnp.sum(oh, axis=1, keepdims=True)
    carry_ref[...] = carry
    cnt_ref[...] = carry
    bkt_ref[...] = bkt
    rank_ref[...] = rank.astype(jnp.int32)
    pad = jnp.zeros((LANE - 2, tm), F32)
    wab_ref[...] = jnp.transpose(jnp.concatenate([wa, wb, pad], axis=0))
    _to_token_major(hx_ref, h_ref[...], tm)


def _moe_tables_kernel(cnt_ref, bkt_ref, rank_ref, pos_ref, tabs_ref, *, t, tr):
    cnt = cnt_ref[...]
    sz = jnp.floor((cnt + (tr - 1)) * (1.0 / tr)) * tr
    r = lax.broadcasted_iota(jnp.int32, (32, 32), 0)
    c = lax.broadcasted_iota(jnp.int32, (32, 32), 1)
    start = jnp.dot(jnp.where(c < r, 1.0, 0.0), sz, preferred_element_type=F32, precision=HIGHEST)
    end = start + sz
    sub = lax.broadcasted_iota(jnp.int32, (32, t), 0)
    pos = jnp.sum(jnp.where(sub == bkt_ref[...], start[:, 0:1], 0.0), axis=0, keepdims=True)
    pos_ref[...] = pos.astype(jnp.int32) + rank_ref[...]
    brow = lax.broadcasted_iota(jnp.int32, (32, LANE), 0)
    tile0 = lax.broadcasted_iota(jnp.int32, (32, LANE), 1).astype(F32) * tr
    tbk = jnp.sum(jnp.where((brow < MOE_NBKT) & (end <= tile0), 1, 0), axis=0, keepdims=True)
    tbk = jnp.minimum(tbk, MOE_NBKT - 1)
    total = end[MOE_NBKT - 1:MOE_NBKT, :]
    valid = jnp.where(tile0[0:1, :] < total, 1, 0)
    g = jnp.where(tbk >= 6, 1, 0) + jnp.where(tbk >= 12, 1, 0) + jnp.where(tbk >= 18, 1, 0)
    p = tbk - 6 * g
    ge3 = jnp.where(p >= 3, 1, 0)
    ge5 = jnp.where(p >= 5, 1, 0)
    ea = 4 * g + ge3 + ge5
    eb = 4 * g + p + 1 - 2 * ge3 - ge5
    zero = jnp.zeros((SUBLANE - 3, LANE), jnp.int32)
    tabs_ref[...] = jnp.concatenate([ea, eb, valid, zero], axis=0)


def _moe_group_kernel(ea_ref, eb_ref, vd_ref, xs_ref, ws_ref, wga_ref, wua_ref, wda_ref,
                      wgb_ref, wub_ref, wdb_ref, ys_ref, wg_s, wu_s, wd_s, *, tr):
    j = pl.program_id(0)
    prev = jnp.maximum(j - 1, 0)

    @pl.when((j == 0) | (ea_ref[j] != ea_ref[prev]))
    def _():
        wg_s[0] = wga_ref[...].astype(BF16)
        wu_s[0] = wua_ref[...].astype(BF16)
        wd_s[0] = wda_ref[...].astype(BF16)

    @pl.when((j == 0) | (eb_ref[j] != eb_ref[prev]))
    def _():
        wg_s[1] = wgb_ref[...].astype(BF16)
        wu_s[1] = wub_ref[...].astype(BF16)
        wd_s[1] = wdb_ref[...].astype(BF16)

    @pl.when(vd_ref[j] > 0)
    def _():
        x = _from_token_major(xs_ref, tr).astype(BF16)
        w = ws_ref[...]

        def ffn(s, cw):
            hg = jnp.dot(x, wg_s[s], preferred_element_type=F32)
            hu = jnp.dot(x, wu_s[s], preferred_element_type=F32)
            hid = _silu(hg) * hu * cw
            return jnp.dot(hid.astype(BF16), wd_s[s], preferred_element_type=F32)

        y = ffn(0, w[:, 0:1]) + ffn(1, w[:, 1:2])
        _to_token_major(ys_ref, y, tr)


def _moe_ln_kernel(h_ref, y_ref, g_ref, b_ref, o_ref, ob_ref, *, tm):
    out = _layer_norm(ALPHA * h_ref[...] + _from_token_major(y_ref, tm), g_ref[...], b_ref[...])
    o_ref[...] = out
    ob_ref[...] = out.astype(BF16)


def _sc_scatter_rows(x3, w2, pos, n_rows):
    from jax.experimental.pallas import tpu_sc as plsc
    t = x3.shape[0]
    info = plsc.get_sparse_core_info()
    nc, ns = info.num_cores, info.num_subcores
    per_w = t // (nc * ns)
    mesh = plsc.VectorSubcoreMesh(core_axis_name="c", subcore_axis_name="s")

    @functools.partial(
        pl.kernel, mesh=mesh,
        out_type=[jax.ShapeDtypeStruct((n_rows,) + x3.shape[1:], x3.dtype),
                  jax.ShapeDtypeStruct((n_rows,) + w2.shape[1:], w2.dtype)],
        scratch_types=[pltpu.VMEM((SC_CH,), jnp.int32), pltpu.VMEM((SC_CH,) + x3.shape[1:], x3.dtype),
                       pltpu.VMEM((SC_CH,) + w2.shape[1:], w2.dtype),
                       pltpu.SemaphoreType.DMA, pltpu.SemaphoreType.DMA])
    def k(x_hbm, w_hbm, idx_hbm, ox_hbm, ow_hbm, idx_v, rows_v, wrows_v, sem_x, sem_w):
        base = (lax.axis_index("s") * nc + lax.axis_index("c")) * per_w

        @pl.loop(0, per_w // SC_CH)
        def _(j):
            off = base + j * SC_CH
            pltpu.sync_copy(idx_hbm.at[pl.ds(off, SC_CH)], idx_v)
            pltpu.sync_copy(x_hbm.at[pl.ds(off, SC_CH)], rows_v)
            pltpu.sync_copy(w_hbm.at[pl.ds(off, SC_CH)], wrows_v)
            cx = pltpu.async_copy(rows_v, ox_hbm.at[idx_v], sem_x)
            cw = pltpu.async_copy(wrows_v, ow_hbm.at[idx_v], sem_w)
            cx.wait()
            cw.wait()

    return k(x3, w2, pos)


def _sc_gather_rows(y3, pos):
    from jax.experimental.pallas import tpu_sc as plsc
    t = pos.shape[0]
    info = plsc.get_sparse_core_info()
    nc, ns = info.num_cores, info.num_subcores
    per_w = t // (nc * ns)
    mesh = plsc.VectorSubcoreMesh(core_axis_name="c", subcore_axis_name="s")

    @functools.partial(
        pl.kernel, mesh=mesh,
        out_type=jax.ShapeDtypeStruct((t,) + y3.shape[1:], y3.dtype),
        scratch_types=[pltpu.VMEM((SC_CH,), jnp.int32), pltpu.VMEM((SC_CH,) + y3.shape[1:], y3.dtype),
                       pltpu.SemaphoreType.DMA])
    def k(y_hbm, idx_hbm, o_hbm, idx_v, rows_v, sem):
        base = (lax.axis_index("s") * nc + lax.axis_index("c")) * per_w

        @pl.loop(0, per_w // SC_CH)
        def _(j):
            off = base + j * SC_CH
            pltpu.sync_copy(idx_hbm.at[pl.ds(off, SC_CH)], idx_v)
            pltpu.async_copy(y_hbm.at[idx_v], rows_v, sem).wait()
            pltpu.sync_copy(rows_v, o_hbm.at[pl.ds(off, SC_CH)])

    return k(y3, pos)


def _moe_sparse(h, wr_t, rbias, wg, wu, wd, g, b, layer, tm=1024):
    t, d = h.shape
    tm = min(tm, t)
    tr = MOE_TR
    nt = t // tr + MOE_NBKT
    n_rows = nt * tr
    row1 = pl.BlockSpec((1, tm), lambda i: (0, i))
    bkt, rank, wab, cnt, hx = pl.pallas_call(
        functools.partial(_moe_route_kernel, tm=tm),
        grid=(t // tm,),
        in_specs=[pl.BlockSpec((tm, d), lambda i: (i, 0)),
                  pl.BlockSpec((N_EXPERTS, d), lambda i: (0, 0)),
                  pl.BlockSpec((N_EXPERTS, LANE), lambda i: (0, 0))],
        out_specs=[row1, row1, pl.BlockSpec((tm, LANE), lambda i: (i, 0)),
                   pl.BlockSpec((32, LANE), lambda i: (0, 0)),
                   pl.BlockSpec((tm * NPIECE, LANE), lambda i: (i, 0))],
        out_shape=[jax.ShapeDtypeStruct((1, t), jnp.int32), jax.ShapeDtypeStruct((1, t), jnp.int32),
                   jax.ShapeDtypeStruct((t, LANE), F32), jax.ShapeDtypeStruct((32, LANE), F32),
                   jax.ShapeDtypeStruct((t * NPIECE, LANE), F32)],
        scratch_shapes=[pltpu.VMEM((32, LANE), F32)],
        compiler_params=_cparams(("arbitrary",)),
        name="moe_route",
    )(h, wr_t, rbias)
    pos, tabs = pl.pallas_call(
        functools.partial(_moe_tables_kernel, t=t, tr=tr),
        out_shape=[jax.ShapeDtypeStruct((1, t), jnp.int32), jax.ShapeDtypeStruct((SUBLANE, LANE), jnp.int32)],
        compiler_params=pltpu.CompilerParams(vmem_limit_bytes=VMEM_LIMIT),
        name="moe_tables",
    )(cnt, bkt, rank)
    pos = pos.reshape(t)
    xs3, ws = _sc_scatter_rows(hx.reshape(t, NPIECE, LANE), wab, pos, n_rows)

    def wspec(which, shape):
        if which == 0:
            return pl.BlockSpec((None, None) + shape, lambda j, ea, eb, vd: (layer, ea[j], 0, 0))
        return pl.BlockSpec((None, None) + shape, lambda j, ea, eb, vd: (layer, eb[j], 0, 0))

    ys = pl.pallas_call(
        functools.partial(_moe_group_kernel, tr=tr),
        grid_spec=pltpu.PrefetchScalarGridSpec(
            num_scalar_prefetch=3,
            grid=(nt,),
            in_specs=[pl.BlockSpec((tr * NPIECE, LANE), lambda j, ea, eb, vd: (j, 0)),
                      pl.BlockSpec((tr, LANE), lambda j, ea, eb, vd: (j, 0)),
                      wspec(0, (d, D_FF)), wspec(0, (d, D_FF)), wspec(0, (D_FF, d)),
                      wspec(1, (d, D_FF)), wspec(1, (d, D_FF)), wspec(1, (D_FF, d))],
            out_specs=pl.BlockSpec((tr * NPIECE, LANE), lambda j, ea, eb, vd: (j, 0)),
            scratch_shapes=[pltpu.VMEM((2, d, D_FF), BF16), pltpu.VMEM((2, d, D_FF), BF16),
                            pltpu.VMEM((2, D_FF, d), BF16)]),
        out_shape=jax.ShapeDtypeStruct((n_rows * NPIECE, LANE), F32),
        compiler_params=_cparams(("arbitrary",)),
        name="moe_experts",
    )(tabs[0, :nt], tabs[1, :nt], tabs[2, :nt], xs3.reshape(n_rows * NPIECE, LANE), ws,
      wg, wu, wd, wg, wu, wd)
    y3 = _sc_gather_rows(ys.reshape(n_rows, NPIECE, LANE), pos)
    vec = pl.BlockSpec((None, 1, d), lambda i: (layer, 0, 0))
    row = pl.BlockSpec((tm, d), lambda i: (i, 0))
    return pl.pallas_call(
        functools.partial(_moe_ln_kernel, tm=tm),
        grid=(t // tm,),
        in_specs=[row, pl.BlockSpec((tm * NPIECE, LANE), lambda i: (i, 0)), vec, vec],
        out_specs=[row, row],
        out_shape=[jax.ShapeDtypeStruct((t, d), F32), jax.ShapeDtypeStruct((t, d), BF16)],
        compiler_params=_cparams(("parallel",)),
        name="moe_ln",
    )(h, y3.reshape(t * NPIECE, LANE), g, b)


def _ffn_block(h, p, layer):
    return _moe_sparse(h, p["wr_t"], p["rbias"], p["wg"], p["wu"], p["wd"], p["ln2_g"], p["ln2_b"], layer)


def kernel(x, ln0_g, ln0_b, w_in, gdn_conv, gdn_a_log, gdn_dt_bias, gdn_norm, gla_w2, gla_b2, gla_norm,
           hgrn_lb_logits, hgrn_norm, w_br_a, w_br_b, w_br_c, w_out, ln1_g, ln1_b, w_router, router_bias,
           w_gate, w_up, w_down, ln2_g, ln2_b):
    batch, seq, d = x.shape
    p = _prepare(w_in, gdn_conv, gdn_a_log, gdn_dt_bias, gdn_norm, gla_w2, gla_b2, gla_norm, hgrn_lb_logits,
                 hgrn_norm, w_br_a, w_br_b, w_br_c, w_out, ln1_g, ln1_b, w_router, router_bias, w_gate, w_up,
                 w_down, ln2_g, ln2_b)
    h, hb = _ln_call(x.reshape(batch * seq, d), ln0_g, ln0_b)
    for layer in range(w_in.shape[0]):
        h = _mixer_block(h, hb, p, layer, batch, seq)
        h, hb = _ffn_block(h, p, layer)
    return h.reshape(batch, seq, d)
```

```python
import functools

import numpy as np
import jax
import jax.numpy as jnp
from jax import lax
from jax.experimental import pallas as pl
from jax.experimental.pallas import tpu as pltpu

F32 = jnp.float32
BF16 = jnp.bfloat16
HIGHEST = lax.Precision.HIGHEST

D_MODEL = 1024
DEPTH = 4
CHUNK = 64
GDN_HEADS, GDN_DK, GDN_DV, CONV_W = 4, 128, 128, 4
GLA_HEADS, GLA_DK, GLA_DV, GLA_RANK, GLA_NORMALIZER = 4, 64, 128, 16, 16.0
HGRN_HEADS, HGRN_EXPAND, HGRN_DV = 4, 128, 128
LB_FLOOR = 1e-30
N_EXPERTS, N_GROUPS, TOP_K, D_FF = 16, 4, 2, 256
EXPERTS_PER_GROUP = N_EXPERTS // N_GROUPS
ALPHA = (2.0 * DEPTH) ** 0.25
LN_EPS = 1e-5
RMS_EPS = 1e-6

GDN_QK = GDN_HEADS * GDN_DK
GDN_V = GDN_HEADS * GDN_DV
GLA_QK = GLA_HEADS * GLA_DK
GLA_V = GLA_HEADS * GLA_DV
HGRN_QK = HGRN_HEADS * HGRN_EXPAND
HGRN_V = HGRN_HEADS * HGRN_DV
SPLIT_SIZES = (GDN_QK, GDN_QK, GDN_V, GDN_HEADS, GDN_HEADS, GDN_V,
               GLA_QK, GLA_QK, GLA_V, GLA_RANK, GLA_V,
               HGRN_QK, HGRN_QK, HGRN_V, HGRN_V,
               D_MODEL, D_MODEL, D_MODEL)
SPLIT_POINTS = tuple(int(v) for v in np.cumsum(SPLIT_SIZES)[:-1])

LANE = 128
SUBLANE = 8
VMEM_LIMIT = 48 * 1024 * 1024

M_A, M_B, M_C = 0, 1024, 2048
A_Q, A_K, A_V, A_G = 3072, 3584, 4096, 4608
B_Q, B_K, B_V, B_G = 5120, 5376, 5632, 6144
C_Q, C_I, C_G = 6656, 7168, 7680
NPB = 8192
F_CF, F_AS, F_LR = 0, 512, 640
NPF = 768

SUB = 8
DD_NPAR = 4
GDN_NPAR = 4
MOE_EPS = 4


def _cparams(sem):
    return pltpu.CompilerParams(dimension_semantics=sem, vmem_limit_bytes=VMEM_LIMIT)


def _mm(a, b):
    return jnp.dot(a.astype(BF16), b.astype(BF16), preferred_element_type=F32)


def _mm_nt(a, b):
    return lax.dot_general(a.astype(BF16), b.astype(BF16), (((1,), (1,)), ((), ())),
                           preferred_element_type=F32)


def _mm_tn(a, b):
    return lax.dot_general(a.astype(BF16), b.astype(BF16), (((0,), (0,)), ((), ())),
                           preferred_element_type=F32)


def _mm_nt_f32(a, b):
    return lax.dot_general(a, b, (((1,), (1,)), ((), ())), preferred_element_type=F32,
                           precision=HIGHEST)


def _split3(x):
    hi = x.astype(BF16)
    r = x - hi.astype(F32)
    mid = r.astype(BF16)
    lo = (r - mid.astype(F32)).astype(BF16)
    return hi, mid, lo


def _mm_01(m01, x):
    hi, mid, lo = _split3(x)
    return (jnp.dot(m01, hi, preferred_element_type=F32) + jnp.dot(m01, mid, preferred_element_type=F32)
            + jnp.dot(m01, lo, preferred_element_type=F32))


def _sigmoid(x):
    return 1.0 / (1.0 + jnp.exp(-x))


def _sigmoid_t(x):
    return 0.5 * jnp.tanh(0.5 * x) + 0.5


def _silu(x):
    return x * _sigmoid_t(x)


def _softplus(x):
    return jnp.maximum(x, 0.0) + jnp.log(1.0 + jnp.exp(-jnp.abs(x)))


def _log_sigmoid(x):
    return -_softplus(-x)


def _layer_norm(x, g, b):
    mu = jnp.mean(x, axis=-1, keepdims=True)
    xc = x - mu
    var = jnp.mean(xc * xc, axis=-1, keepdims=True)
    return xc * lax.rsqrt(var + LN_EPS) * g + b


def _ln_kernel(x_ref, g_ref, b_ref, o_ref, ob_ref):
    y = _layer_norm(x_ref[...], g_ref[...], b_ref[...])
    o_ref[...] = y
    ob_ref[...] = y.astype(BF16)


def _ln_call(x, g, b, tm=512):
    t, d = x.shape
    tm = min(tm, t)
    return pl.pallas_call(
        _ln_kernel,
        grid=(t // tm,),
        in_specs=[pl.BlockSpec((tm, d), lambda i: (i, 0)),
                  pl.BlockSpec((1, d), lambda i: (0, 0)),
                  pl.BlockSpec((1, d), lambda i: (0, 0))],
        out_specs=[pl.BlockSpec((tm, d), lambda i: (i, 0)), pl.BlockSpec((tm, d), lambda i: (i, 0))],
        out_shape=[jax.ShapeDtypeStruct((t, d), F32), jax.ShapeDtypeStruct((t, d), BF16)],
        compiler_params=_cparams(("parallel",)),
        name="ln0",
    )(x, g.reshape(1, d), b.reshape(1, d))


def _inproj_kernel(x_ref, wt_ref, o_ref):
    o_ref[...] = lax.dot_general(x_ref[...], wt_ref[...], (((1,), (1,)), ((), ())),
                                 preferred_element_type=F32).astype(o_ref.dtype)


def _inproj_call(hb, w, layer, out_dtype, tm, tn, name):
    t, d = hb.shape
    tm = min(tm, t)
    n = w.shape[-2]
    return pl.pallas_call(
        _inproj_kernel,
        grid=(n // tn, t // tm),
        in_specs=[pl.BlockSpec((tm, d), lambda j, i: (i, 0)),
                  pl.BlockSpec((None, tn, d), lambda j, i: (layer, j, 0))],
        out_specs=pl.BlockSpec((tm, tn), lambda j, i: (i, j)),
        out_shape=jax.ShapeDtypeStruct((t, n), out_dtype),
        compiler_params=_cparams(("parallel", "parallel")),
        name=name,
    )(hb, w)


def _gdn_kernel(qkv_ref, s_ref, gate_ref, cw_ref, par_ref, nw_ref, o_ref,
                state_ref, tail_ref, xbuf_ref, q_s, k_s, v_s, cumb_s, betab_s, cumrow_s, o_s, *, tb):
    blk = pl.program_id(1)
    nc = tb // CHUNK
    nh = GDN_HEADS
    off0 = SUBLANE - (CONV_W - 1)

    @pl.when(blk == 0)
    def _():
        state_ref[...] = jnp.zeros_like(state_ref)
        tail_ref[...] = jnp.zeros_like(tail_ref)

    row = lax.broadcasted_iota(jnp.int32, (CHUNK, CHUNK), 0)
    col = lax.broadcasted_iota(jnp.int32, (CHUNK, CHUNK), 1)
    incl = col <= row
    strict = col < row
    tri = jnp.where(incl, 1.0, 0.0).astype(BF16)
    eye = jnp.where(col == row, 1.0, 0.0).astype(F32)
    npar = GDN_NPAR
    rp = npar * CHUNK
    xbuf_ref[0:SUBLANE, :] = tail_ref[...]

    def prologue(t):
        lo = t * rp
        xbuf_ref[SUBLANE + lo:SUBLANE + lo + rp, :] = qkv_ref[lo:lo + rp, :].astype(F32)
        for j in range(3 * nh):
            cs = slice(j * LANE, (j + 1) * LANE)
            y = xbuf_ref[off0 + lo:off0 + lo + rp, cs] * cw_ref[0:1, cs]
            for kk in range(1, CONV_W):
                y = y + xbuf_ref[off0 + kk + lo:off0 + kk + lo + rp, cs] * cw_ref[kk:kk + 1, cs]
            y = _silu(y)
            if j < nh:
                q_s[j, lo:lo + rp, :] = (y * lax.rsqrt(jnp.sum(y * y, axis=-1, keepdims=True) + RMS_EPS)
                                         * (GDN_DK ** -0.5))
            elif j < 2 * nh:
                k_s[j - nh, lo:lo + rp, :] = y * lax.rsqrt(jnp.sum(y * y, axis=-1, keepdims=True) + RMS_EPS)
            else:
                v_s[j - 2 * nh, lo:lo + rp, :] = y
        sc = s_ref[lo:lo + rp, :]
        beta_all = _sigmoid_t(sc)
        g_all = -jnp.exp(par_ref[0:1, :]) * _softplus(sc + par_ref[1:2, :])
        cum_all = jnp.concatenate([_mm_01(tri, g_all[c * CHUNK:(c + 1) * CHUNK, :]) for c in range(npar)], axis=0)
        cum_t = jnp.transpose(cum_all)
        for c in range(npar):
            cumrow_s[t * npar + c] = cum_t[0:SUBLANE, c * CHUNK:(c + 1) * CHUNK]
        for h in range(nh):
            cumb_s[h, lo:lo + rp, :] = jnp.broadcast_to(cum_all[:, nh + h:nh + h + 1], (rp, LANE))
            betab_s[h, lo:lo + rp, :] = jnp.broadcast_to(beta_all[:, h:h + 1], (rp, LANE))

    def trip(cp):
        chains = [(cp * npar + cc, h) for cc in range(npar) for h in range(nh)]
        r0s = [c * CHUNK for c, _ in chains]
        qc = [q_s[h, pl.ds(r0, CHUNK), :] for (_, h), r0 in zip(chains, r0s)]
        kc = [k_s[h, pl.ds(r0, CHUNK), :] for (_, h), r0 in zip(chains, r0s)]
        vc = [v_s[h, pl.ds(r0, CHUNK), :] for (_, h), r0 in zip(chains, r0s)]
        cumc = [cumb_s[h, pl.ds(r0, CHUNK), :] for (_, h), r0 in zip(chains, r0s)]
        bc = [betab_s[h, pl.ds(r0, CHUNK), :] for (_, h), r0 in zip(chains, r0s)]
        n = len(chains)
        kk = [_mm_nt(kc[i], kc[i]) for i in range(n)]
        qk = [_mm_nt(qc[i], kc[i]) for i in range(n)]
        decay = []
        for i, (c, h) in enumerate(chains):
            diff = cumc[i][:, 0:CHUNK] - cumrow_s[c][nh + h:nh + h + 1, :]
            decay.append(jnp.where(incl, jnp.exp(jnp.where(incl, diff, 0.0)), 0.0))
        a = [jnp.where(strict, bc[i][:, 0:CHUNK] * kk[i] * decay[i], 0.0) for i in range(n)]
        x = [eye - a[i] for i in range(n)]
        p = [_mm(a[i], a[i]) for i in range(n)]
        for it in range(5):
            x = [x[i] + _mm(x[i], p[i]) for i in range(n)]
            if it < 4:
                p = [_mm(p[i], p[i]) for i in range(n)]
        ecum = [jnp.exp(cumc[i]) for i in range(n)]
        sol = [_mm(x[i], jnp.concatenate([vc[i] * bc[i], kc[i] * (bc[i] * ecum[i])], axis=1)) for i in range(n)]
        attn = [qk[i] * decay[i] for i in range(n)]
        cum_last = [cumc[i][CHUNK - 1:CHUNK, :] for i in range(n)]
        k_state = [kc[i] * jnp.exp(cum_last[i] - cumc[i]) for i in range(n)]
        wqi = [jnp.concatenate([sol[i][:, GDN_DV:GDN_DV + GDN_DK], qc[i] * ecum[i]], axis=0) for i in range(n)]
        for cc in range(npar):
            idx = [cc * nh + h for h in range(nh)]
            s = [state_ref[h] for h in range(nh)]
            wq = [_mm(wqi[i], s[h]) for h, i in enumerate(idx)]
            v_new = [sol[i][:, 0:GDN_DV] - wq[h][0:CHUNK, :] for h, i in enumerate(idx)]
            av = [_mm(attn[i], v_new[h]) for h, i in enumerate(idx)]
            upd = [_mm_tn(k_state[i], v_new[h]) for h, i in enumerate(idx)]
            for h, i in enumerate(idx):
                state_ref[h] = s[h] * jnp.exp(cum_last[i]) + upd[h]
                o_s[pl.ds(r0s[i], CHUNK), h * LANE:(h + 1) * LANE] = wq[h][CHUNK:2 * CHUNK, :] + av[h]

    def epilogue(t):
        lo = t * rp
        for h in range(nh):
            hs = slice(h * LANE, (h + 1) * LANE)
            o = o_s[lo:lo + rp, hs]
            oh = o * lax.rsqrt(jnp.mean(o * o, axis=-1, keepdims=True) + RMS_EPS) * nw_ref[...]
            o_ref[lo:lo + rp, hs] = (oh * _silu(gate_ref[lo:lo + rp, hs].astype(F32))).astype(o_ref.dtype)

    nparts = nc // npar
    prologue(0)
    for t in range(nparts):
        if t + 1 < nparts:
            prologue(t + 1)
        trip(t)
        epilogue(t)
    tail_ref[...] = xbuf_ref[tb:tb + SUBLANE, :]


def _gdn_call(pb, pf, conv_w, par, norm_w, layer, batch, seq, tb=1024):
    tb = min(tb, seq)
    nb = seq // tb
    nc = tb // CHUNK
    t = batch * seq
    wq = 2 * GDN_QK + GDN_V
    kern = functools.partial(_gdn_kernel, tb=tb)
    return pl.pallas_call(
        kern,
        grid=(batch, nb),
        in_specs=[pl.BlockSpec((tb, wq), lambda b, i: (b * nb + i, A_Q // wq)),
                  pl.BlockSpec((tb, LANE), lambda b, i: (b * nb + i, F_AS // LANE)),
                  pl.BlockSpec((tb, GDN_V), lambda b, i: (b * nb + i, A_G // GDN_V)),
                  pl.BlockSpec((None, CONV_W, wq), lambda b, i: (layer, 0, 0)),
                  pl.BlockSpec((None, SUBLANE, LANE), lambda b, i: (layer, 0, 0)),
                  pl.BlockSpec((None, 1, GDN_DV), lambda b, i: (layer, 0, 0))],
        out_specs=pl.BlockSpec((tb, GDN_V), lambda b, i: (b * nb + i, 0)),
        out_shape=jax.ShapeDtypeStruct((t, GDN_V), BF16),
        scratch_shapes=[pltpu.VMEM((GDN_HEADS, GDN_DK, GDN_DV), F32),
                        pltpu.VMEM((SUBLANE, wq), F32),
                        pltpu.VMEM((tb + SUBLANE, wq), F32),
                        pltpu.VMEM((GDN_HEADS, tb, LANE), F32),
                        pltpu.VMEM((GDN_HEADS, tb, LANE), F32),
                        pltpu.VMEM((GDN_HEADS, tb, LANE), F32),
                        pltpu.VMEM((GDN_HEADS, tb, LANE), F32),
                        pltpu.VMEM((GDN_HEADS, tb, LANE), F32),
                        pltpu.VMEM((nc, SUBLANE, CHUNK), F32),
                        pltpu.VMEM((tb, GDN_V), F32)],
        compiler_params=_cparams(("parallel", "arbitrary")),
        name="gdn",
    )(pb, pf, pb, conv_w, par, norm_w)


def _dd_core(q_s, k_s, v_s, la_s, o_s, state_ref, c8_s, p_s, *, tb, g_heads):
    nchunk = tb // CHUNK
    nblk = CHUNK // SUB
    dkh = LANE // g_heads
    dvp = g_heads * LANE
    row = lax.broadcasted_iota(jnp.int32, (CHUNK, CHUNK), 0)
    col = lax.broadcasted_iota(jnp.int32, (CHUNK, CHUNK), 1)
    level_masks = []
    for sh in (5, 4, 3):
        same2b = jnp.right_shift(row, sh + 1) == jnp.right_shift(col, sh + 1)
        upper = (jnp.right_shift(row, sh) & 1) == 1
        lower = (jnp.right_shift(col, sh) & 1) == 0
        level_masks.append(jnp.where(same2b, jnp.where(upper, jnp.where(lower, 1.0, 0.0), 0.0), 0.0))
    tri8 = jnp.where(jnp.right_shift(row, 3) == jnp.right_shift(col, 3),
                     jnp.where(col <= row, 1.0, 0.0), 0.0).astype(BF16)
    lane128 = lax.broadcasted_iota(jnp.int32, (CHUNK, LANE), 1)
    head_masks = [jnp.where((lane128 >= g * dkh) & (lane128 < (g + 1) * dkh), 1.0, 0.0)
                  for g in range(g_heads)]
    sub = lax.broadcasted_iota(jnp.int32, (SUB, LANE), 0)
    dk_sh = dkh.bit_length() - 1
    lane_sh = LANE.bit_length() - 1
    orow = lax.broadcasted_iota(jnp.int32, (LANE, dvp), 0)
    ocol = lax.broadcasted_iota(jnp.int32, (LANE, dvp), 1)
    ones_bd = jnp.where(jnp.right_shift(orow, dk_sh) == jnp.right_shift(ocol, lane_sh), 1.0, 0.0).astype(BF16)
    srow = lax.broadcasted_iota(jnp.int32, (dvp, LANE), 0)
    scol = lax.broadcasted_iota(jnp.int32, (dvp, LANE), 1)
    state_mask = jnp.where(jnp.right_shift(srow, lane_sh) == jnp.right_shift(scol, dk_sh), 1.0, 0.0)
    npar = DD_NPAR

    def block_sums(i):
        c8 = [c8_s[i, b * SUB:(b + 1) * SUB, :] for b in range(nblk)]
        t8 = [c8_s[i, (b + 1) * SUB - 1:(b + 1) * SUB, :] for b in range(nblk)]
        t16 = [t8[2 * b] + t8[2 * b + 1] for b in range(nblk // 2)]
        t32 = [t16[2 * b] + t16[2 * b + 1] for b in range(nblk // 4)]
        t64 = t32[0] + t32[1]
        c16 = [c8[b] + t8[b - 1] if b % 2 else c8[b] for b in range(nblk)]
        c32 = [c16[b] + t16[b // 2 - 1] if (b // 2) % 2 else c16[b] for b in range(nblk)]
        c64 = [c32[b] + t32[0] if b >= nblk // 2 else c32[b] for b in range(nblk)]
        pre = {8: c8, 16: c16, 32: c32, 64: c64}
        suf = {8: [t8[b] - c8[b] for b in range(nblk)],
               16: [t16[b // 2] - c16[b] for b in range(nblk)],
               32: [t32[b // 4] - c32[b] for b in range(nblk)],
               64: [t64 - c64[b] for b in range(nblk)]}
        return pre, suf

    def cat(pieces):
        return jnp.concatenate(pieces, axis=0)

    def group_body(cg, carry):
        rng = range(npar)
        r0s = [pl.multiple_of((cg * npar + i) * CHUNK, CHUNK) for i in rng]
        qc = [q_s[pl.ds(r0, CHUNK), :] for r0 in r0s]
        kc = [k_s[pl.ds(r0, CHUNK), :] for r0 in r0s]
        vc = [v_s[pl.ds(r0, CHUNK), :] for r0 in r0s]
        c8_all = _mm_01(tri8, jnp.concatenate([la_s[pl.ds(r0, CHUNK), :] for r0 in r0s], axis=1))
        for i in rng:
            c8_s[i] = c8_all[:, i * LANE:(i + 1) * LANE]
        sums = [block_sums(i) for i in rng]
        attn = [[None] * g_heads for _ in rng]
        for li, b in enumerate((32, 16, 8)):
            qs = [qc[i] * jnp.exp(cat(sums[i][0][b])) for i in rng]
            ks = [kc[i] * jnp.exp(cat(sums[i][1][b])) for i in rng]
            for g in range(g_heads):
                for i in rng:
                    qg = qs[i] * head_masks[g] if g_heads > 1 else qs[i]
                    term = _mm_nt(qg, ks[i]) * level_masks[li]
                    attn[i][g] = term if li == 0 else attn[i][g] + term
        o = [jnp.concatenate([_mm(attn[i][g], vc[i][:, g * LANE:(g + 1) * LANE]) for g in range(g_heads)], axis=1)
             if g_heads > 1 else _mm(attn[i][0], vc[i]) for i in rng]
        for i in rng:
            c8 = sums[i][0][8]
            for r in range(nblk):
                qr = qc[i][r * SUB:(r + 1) * SUB, :]
                for jj in range(SUB):
                    krow = k_s[pl.ds(r0s[i] + r * SUB + jj, 1), :]
                    crow = c8_s[i, r * SUB + jj:r * SUB + jj + 1, :]
                    m = sub >= jj
                    pr = qr * krow * jnp.exp(jnp.where(m, c8[r] - crow, -jnp.inf))
                    p_s[i, (r * SUB + jj) * SUB:(r * SUB + jj + 1) * SUB, :] = pr
        rs = [jnp.dot(p_s[i].astype(BF16), ones_bd, preferred_element_type=F32) for i in rng]
        for i in rng:
            od = []
            for r in range(nblk):
                acc = None
                for jj in range(SUB):
                    vrow = v_s[pl.ds(r0s[i] + r * SUB + jj, 1), :]
                    term = rs[i][(r * SUB + jj) * SUB:(r * SUB + jj + 1) * SUB, :] * vrow
                    acc = term if acc is None else acc + term
                od.append(acc)
            o[i] = o[i] + cat(od)
        q_inter = [qc[i] * jnp.exp(cat(sums[i][0][64])) for i in rng]
        k_state = [kc[i] * jnp.exp(cat(sums[i][1][64])) for i in rng]
        upd = [_mm_tn(vc[i], k_state[i]) for i in rng]
        st = state_ref[...]
        for i in rng:
            o_s[pl.ds(r0s[i], CHUNK), :] = o[i] + _mm_nt(q_inter[i], st)
            decay_last = jnp.exp(sums[i][0][64][nblk - 1][SUB - 1:SUB, :])
            st = st * decay_last + (upd[i] * state_mask if g_heads > 1 else upd[i])
        state_ref[...] = st
        return carry

    lax.fori_loop(0, nchunk // npar, group_body, 0)


def _gated_rms_out(o_s, gate_ref, nw_ref, o_ref, g_heads):
    for g in range(g_heads):
        hs = slice(g * LANE, (g + 1) * LANE)
        o = o_s[:, hs]
        oh = o * lax.rsqrt(jnp.mean(o * o, axis=-1, keepdims=True) + RMS_EPS) * nw_ref[...]
        o_ref[:, hs] = (oh * _silu(gate_ref[:, hs].astype(F32))).astype(o_ref.dtype)


def _gla_kernel(q_ref, k_ref, v_ref, lr_ref, gate_ref, w2_ref, b2_ref, nw_ref,
                o_ref, state_ref, q_s, k_s, v_s, la_s, o_s, c8_s, p_s, *, tb):
    @pl.when(pl.program_id(2) == 0)
    def _():
        state_ref[...] = jnp.zeros_like(state_ref)

    q_s[...] = q_ref[...].astype(F32) * (GLA_DK ** -0.5)
    k_s[...] = k_ref[...].astype(F32)
    v_s[...] = v_ref[...].astype(F32)
    z = _mm(lr_ref[...], w2_ref[...]) + b2_ref[...]
    la_s[...] = _log_sigmoid(z) * (1.0 / GLA_NORMALIZER)
    _dd_core(q_s, k_s, v_s, la_s, o_s, state_ref, c8_s, p_s, tb=tb, g_heads=2)
    _gated_rms_out(o_s, gate_ref, nw_ref, o_ref, 2)


def _dd_scratch(tb, dvp):
    return [pltpu.VMEM((dvp, LANE), F32),
            pltpu.VMEM((tb, LANE), F32), pltpu.VMEM((tb, LANE), F32),
            pltpu.VMEM((tb, dvp), F32),
            pltpu.VMEM((tb, LANE), F32),
            pltpu.VMEM((tb, dvp), F32),
            pltpu.VMEM((DD_NPAR, CHUNK, LANE), F32),
            pltpu.VMEM((DD_NPAR, CHUNK * SUB, LANE), F32)]


def _gla_call(pb, pf, w2p, b2, norm_w, layer, batch, seq, tb=512):
    tb = min(tb, seq)
    nb = seq // tb
    t = batch * seq
    npair = GLA_HEADS // 2
    kern = functools.partial(_gla_kernel, tb=tb)
    return pl.pallas_call(
        kern,
        grid=(batch, npair, nb),
        in_specs=[pl.BlockSpec((tb, LANE), lambda b, p, i: (b * nb + i, B_Q // LANE + p)),
                  pl.BlockSpec((tb, LANE), lambda b, p, i: (b * nb + i, B_K // LANE + p)),
                  pl.BlockSpec((tb, 2 * LANE), lambda b, p, i: (b * nb + i, B_V // (2 * LANE) + p)),
                  pl.BlockSpec((tb, LANE), lambda b, p, i: (b * nb + i, F_LR // LANE)),
                  pl.BlockSpec((tb, 2 * LANE), lambda b, p, i: (b * nb + i, B_G // (2 * LANE) + p)),
                  pl.BlockSpec((None, LANE, LANE), lambda b, p, i: (layer, 0, p)),
                  pl.BlockSpec((None, 1, LANE), lambda b, p, i: (layer, 0, p)),
                  pl.BlockSpec((None, 1, GLA_DV), lambda b, p, i: (layer, 0, 0))],
        out_specs=pl.BlockSpec((tb, 2 * LANE), lambda b, p, i: (b * nb + i, p)),
        out_shape=jax.ShapeDtypeStruct((t, GLA_V), BF16),
        scratch_shapes=_dd_scratch(tb, 2 * LANE),
        compiler_params=_cparams(("parallel", "parallel", "arbitrary")),
        name="gla",
    )(pb, pb, pb, pf, pb, w2p, b2, norm_w)


def _hgrn_kernel(q_ref, f_ref, v_ref, gate_ref, lbl_ref, nw_ref,
                 o_ref, state_ref, q_s, k_s, v_s, la_s, o_s, c8_s, p_s, *, tb, layer):
    @pl.when(pl.program_id(2) == 0)
    def _():
        state_ref[...] = jnp.zeros_like(state_ref)

    logits = lbl_ref[...]
    mx = jnp.max(logits, axis=0, keepdims=True)
    ex = jnp.exp(logits - mx)
    p = ex / jnp.sum(ex, axis=0, keepdims=True)
    acc = p[0:1, :]
    for r in range(1, layer + 1):
        acc = acc + p[r:r + 1, :]
    lb = jnp.clip(acc - p[0:1, :], 0.0, 1.0)
    log_lb = jnp.log(jnp.maximum(lb, LB_FLOOR))
    log_1m = jnp.log(1.0 - lb)

    cf = f_ref[...]
    second = log_1m + _log_sigmoid(cf)
    la_s[...] = jnp.maximum(log_lb, second) + jnp.log(1.0 + jnp.exp(-jnp.abs(log_lb - second)))
    k_s[...] = (1.0 - lb) * _sigmoid_t(-cf)
    q_s[...] = _silu(q_ref[...].astype(F32)) * (HGRN_EXPAND ** -0.5)
    v_s[...] = v_ref[...].astype(F32)
    _dd_core(q_s, k_s, v_s, la_s, o_s, state_ref, c8_s, p_s, tb=tb, g_heads=1)
    _gated_rms_out(o_s, gate_ref, nw_ref, o_ref, 1)


def _hgrn_call(pb, pf, lb_logits, norm_w, layer, batch, seq, tb=512):
    tb = min(tb, seq)
    nb = seq // tb
    t = batch * seq

    def tok(colblk):
        return pl.BlockSpec((tb, LANE), lambda b, h, i: (b * nb + i, colblk + h))

    kern = functools.partial(_hgrn_kernel, tb=tb, layer=layer)
    return pl.pallas_call(
        kern,
        grid=(batch, HGRN_HEADS, nb),
        in_specs=[tok(C_Q // LANE), tok(F_CF // LANE), tok(C_I // LANE), tok(C_G // LANE),
                  pl.BlockSpec((DEPTH, LANE), lambda b, h, i: (0, h)),
                  pl.BlockSpec((None, 1, HGRN_DV), lambda b, h, i: (layer, 0, 0))],
        out_specs=pl.BlockSpec((tb, LANE), lambda b, h, i: (b * nb + i, h)),
        out_shape=jax.ShapeDtypeStruct((t, HGRN_V), BF16),
        scratch_shapes=_dd_scratch(tb, LANE),
        compiler_params=_cparams(("parallel", "parallel", "arbitrary")),
        name="hgrn",
    )(pb, pf, pb, pb, lb_logits, norm_w)


def _dd_make_trip(q_s, k_s, v_s, la_s, o_s, state_ref, c8_s, p_s, g_heads):
    nblk = CHUNK // SUB
    dkh = LANE // g_heads
    dvp = g_heads * LANE
    npar = DD_NPAR
    row = lax.broadcasted_iota(jnp.int32, (CHUNK, CHUNK), 0)
    col = lax.broadcasted_iota(jnp.int32, (CHUNK, CHUNK), 1)
    level_masks = []
    for sh in (5, 4, 3):
        same2b = jnp.right_shift(row, sh + 1) == jnp.right_shift(col, sh + 1)
        upper = (jnp.right_shift(row, sh) & 1) == 1
        lower = (jnp.right_shift(col, sh) & 1) == 0
        level_masks.append(jnp.where(same2b, jnp.where(upper, jnp.where(lower, 1.0, 0.0), 0.0), 0.0))
    tri8 = jnp.where(jnp.right_shift(row, 3) == jnp.right_shift(col, 3),
                     jnp.where(col <= row, 1.0, 0.0), 0.0).astype(BF16)
    lane128 = lax.broadcasted_iota(jnp.int32, (CHUNK, LANE), 1)
    head_masks = [jnp.where((lane128 >= g * dkh) & (lane128 < (g + 1) * dkh), 1.0, 0.0)
                  for g in range(g_heads)]
    sub = lax.broadcasted_iota(jnp.int32, (SUB, LANE), 0)
    dk_sh = dkh.bit_length() - 1
    lane_sh = LANE.bit_length() - 1
    orow = lax.broadcasted_iota(jnp.int32, (LANE, dvp), 0)
    ocol = lax.broadcasted_iota(jnp.int32, (LANE, dvp), 1)
    ones_bd = jnp.where(jnp.right_shift(orow, dk_sh) == jnp.right_shift(ocol, lane_sh), 1.0, 0.0).astype(BF16)
    srow = lax.broadcasted_iota(jnp.int32, (dvp, LANE), 0)
    scol = lax.broadcasted_iota(jnp.int32, (dvp, LANE), 1)
    state_mask = jnp.where(jnp.right_shift(srow, lane_sh) == jnp.right_shift(scol, dk_sh), 1.0, 0.0)

    def block_sums(ci):
        c8 = [c8_s[ci, b * SUB:(b + 1) * SUB, :] for b in range(nblk)]
        t8 = [c8_s[ci, (b + 1) * SUB - 1:(b + 1) * SUB, :] for b in range(nblk)]
        t16 = [t8[2 * b] + t8[2 * b + 1] for b in range(nblk // 2)]
        t32 = [t16[2 * b] + t16[2 * b + 1] for b in range(nblk // 4)]
        t64 = t32[0] + t32[1]
        c16 = [c8[b] + t8[b - 1] if b % 2 else c8[b] for b in range(nblk)]
        c32 = [c16[b] + t16[b // 2 - 1] if (b // 2) % 2 else c16[b] for b in range(nblk)]
        c64 = [c32[b] + t32[0] if b >= nblk // 2 else c32[b] for b in range(nblk)]
        pre = {8: c8, 16: c16, 32: c32, 64: c64}
        suf = {8: [t8[b] - c8[b] for b in range(nblk)],
               16: [t16[b // 2] - c16[b] for b in range(nblk)],
               32: [t32[b // 4] - c32[b] for b in range(nblk)],
               64: [t64 - c64[b] for b in range(nblk)]}
        return pre, suf

    def cat(pieces):
        return jnp.concatenate(pieces, axis=0)

    def trip(u, t, slot):
        rng = range(npar)
        r0s = [(t * npar + i) * CHUNK for i in rng]
        cis = [slot * npar + i for i in rng]
        qc = [q_s[u, r0:r0 + CHUNK, :] for r0 in r0s]
        kc = [k_s[u, r0:r0 + CHUNK, :] for r0 in r0s]
        vc = [v_s[u, r0:r0 + CHUNK, :] for r0 in r0s]
        c8_all = _mm_01(tri8, jnp.concatenate([la_s[u, r0:r0 + CHUNK, :] for r0 in r0s], axis=1))
        for i in rng:
            c8_s[cis[i]] = c8_all[:, i * LANE:(i + 1) * LANE]
        sums = [block_sums(cis[i]) for i in rng]
        attn = [[None] * g_heads for _ in rng]
        for li, b in enumerate((32, 16, 8)):
            qs = [qc[i] * jnp.exp(cat(sums[i][0][b])) for i in rng]
            ks = [kc[i] * jnp.exp(cat(sums[i][1][b])) for i in rng]
            for g in range(g_heads):
                for i in rng:
                    qg = qs[i] * head_masks[g] if g_heads > 1 else qs[i]
                    term = _mm_nt(qg, ks[i]) * level_masks[li]
                    attn[i][g] = term if li == 0 else attn[i][g] + term
        o = [jnp.concatenate([_mm(attn[i][g], vc[i][:, g * LANE:(g + 1) * LANE]) for g in range(g_heads)], axis=1)
             if g_heads > 1 else _mm(attn[i][0], vc[i]) for i in rng]
        for i in rng:
            c8 = sums[i][0][8]
            for r in range(nblk):
                qr = qc[i][r * SUB:(r + 1) * SUB, :]
                for jj in range(SUB):
                    rr = r0s[i] + r * SUB + jj
                    krow = k_s[u, rr:rr + 1, :]
                    crow = c8_s[cis[i], r * SUB + jj:r * SUB + jj + 1, :]
                    pr = qr * krow * jnp.exp(jnp.where(sub >= jj, c8[r] - crow, -jnp.inf))
                    p_s[cis[i], (r * SUB + jj) * SUB:(r * SUB + jj + 1) * SUB, :] = pr
        rs = [jnp.dot(p_s[cis[i]].astype(BF16), ones_bd, preferred_element_type=F32) for i in rng]
        for i in rng:
            od = []
            for r in range(nblk):
                acc = None
                for jj in range(SUB):
                    rr = r0s[i] + r * SUB + jj
                    term = rs[i][(r * SUB + jj) * SUB:(r * SUB + jj + 1) * SUB, :] * v_s[u, rr:rr + 1, :]
                    acc = term if acc is None else acc + term
                od.append(acc)
            o[i] = o[i] + cat(od)
        q_inter = [qc[i] * jnp.exp(cat(sums[i][0][64])) for i in rng]
        k_state = [kc[i] * jnp.exp(cat(sums[i][1][64])) for i in rng]
        upd = [_mm_tn(vc[i], k_state[i]) for i in rng]
        st = state_ref[u]
        for i in rng:
            o_s[u, r0s[i]:r0s[i] + CHUNK, :] = o[i] + _mm_nt(q_inter[i], st)
            decay_last = jnp.exp(sums[i][0][64][nblk - 1][SUB - 1:SUB, :])
            st = st * decay_last + (upd[i] * state_mask if g_heads > 1 else upd[i])
        state_ref[u] = st

    return trip


def _dd_schedule(units, prologue, trip, epilogue):
    prologue(*units[0])
    for n, (u, t) in enumerate(units):
        if n + 1 < len(units):
            prologue(*units[n + 1])
        trip(u, t, n % 2)
        epilogue(u, t)


def _dd_scratch4(nu, tb, dvp):
    return [pltpu.VMEM((nu, dvp, LANE), F32),
            pltpu.VMEM((nu, tb, LANE), F32), pltpu.VMEM((nu, tb, LANE), F32),
            pltpu.VMEM((nu, tb, dvp), F32),
            pltpu.VMEM((nu, tb, LANE), F32),
            pltpu.VMEM((nu, tb, dvp), F32),
            pltpu.VMEM((2 * DD_NPAR, CHUNK, LANE), F32),
            pltpu.VMEM((2 * DD_NPAR, CHUNK * SUB, LANE), F32)]


def _gla4_kernel(q_ref, k_ref, v_ref, lr_ref, gate_ref, w2_ref, b2_ref, nw_ref,
                 o_ref, state_ref, q_s, k_s, v_s, la_s, o_s, c8_s, p_s, *, tb):
    @pl.when(pl.program_id(1) == 0)
    def _():
        state_ref[...] = jnp.zeros_like(state_ref)

    npair = GLA_HEADS // 2
    rp = DD_NPAR * CHUNK
    trip = _dd_make_trip(q_s, k_s, v_s, la_s, o_s, state_ref, c8_s, p_s, 2)

    def prologue(u, t):
        lo = t * rp
        ls = slice(u * LANE, (u + 1) * LANE)
        vs = slice(u * 2 * LANE, (u + 1) * 2 * LANE)
        q_s[u, lo:lo + rp, :] = q_ref[lo:lo + rp, ls].astype(F32) * (GLA_DK ** -0.5)
        k_s[u, lo:lo + rp, :] = k_ref[lo:lo + rp, ls].astype(F32)
        v_s[u, lo:lo + rp, :] = v_ref[lo:lo + rp, vs].astype(F32)
        z = _mm(lr_ref[lo:lo + rp, :], w2_ref[:, ls]) + b2_ref[:, ls]
        la_s[u, lo:lo + rp, :] = _log_sigmoid(z) * (1.0 / GLA_NORMALIZER)

    def epilogue(u, t):
        lo = t * rp
        for g in range(2):
            hs = slice((2 * u + g) * LANE, (2 * u + g + 1) * LANE)
            o = o_s[u, lo:lo + rp, g * LANE:(g + 1) * LANE]
            oh = o * lax.rsqrt(jnp.mean(o * o, axis=-1, keepdims=True) + RMS_EPS) * nw_ref[...]
            o_ref[lo:lo + rp, hs] = (oh * _silu(gate_ref[lo:lo + rp, hs].astype(F32))).astype(o_ref.dtype)

    _dd_schedule([(u, t) for u in range(npair) for t in range(tb // rp)], prologue, trip, epilogue)


def _gla4_call(pb, pf, w2p, b2, norm_w, layer, batch, seq, tb=1024):
    tb = min(tb, seq)
    nb = seq // tb
    t = batch * seq
    kern = functools.partial(_gla4_kernel, tb=tb)
    return pl.pallas_call(
        kern,
        grid=(batch, nb),
        in_specs=[pl.BlockSpec((tb, GLA_QK), lambda b, i: (b * nb + i, B_Q // GLA_QK)),
                  pl.BlockSpec((tb, GLA_QK), lambda b, i: (b * nb + i, B_K // GLA_QK)),
                  pl.BlockSpec((tb, GLA_V), lambda b, i: (b * nb + i, B_V // GLA_V)),
                  pl.BlockSpec((tb, LANE), lambda b, i: (b * nb + i, F_LR // LANE)),
                  pl.BlockSpec((tb, GLA_V), lambda b, i: (b * nb + i, B_G // GLA_V)),
                  pl.BlockSpec((None, LANE, GLA_QK), lambda b, i: (layer, 0, 0)),
                  pl.BlockSpec((None, 1, GLA_QK), lambda b, i: (layer, 0, 0)),
                  pl.BlockSpec((None, 1, GLA_DV), lambda b, i: (layer, 0, 0))],
        out_specs=pl.BlockSpec((tb, GLA_V), lambda b, i: (b * nb + i, 0)),
        out_shape=jax.ShapeDtypeStruct((t, GLA_V), BF16),
        scratch_shapes=_dd_scratch4(GLA_HEADS // 2, tb, 2 * LANE),
        compiler_params=_cparams(("parallel", "arbitrary")),
        name="gla",
    )(pb, pb, pb, pf, pb, w2p, b2, norm_w)


def _hgrn4_kernel(q_ref, f_ref, v_ref, gate_ref, lbl_ref, nw_ref,
                  o_ref, state_ref, q_s, k_s, v_s, la_s, o_s, c8_s, p_s, *, tb, layer):
    @pl.when(pl.program_id(1) == 0)
    def _():
        state_ref[...] = jnp.zeros_like(state_ref)

    logits = lbl_ref[...]
    mx = jnp.max(logits, axis=0, keepdims=True)
    ex = jnp.exp(logits - mx)
    p = ex / jnp.sum(ex, axis=0, keepdims=True)
    acc = p[0:1, :]
    for r in range(1, layer + 1):
        acc = acc + p[r:r + 1, :]
    lb = jnp.clip(acc - p[0:1, :], 0.0, 1.0)
    log_lb = jnp.log(jnp.maximum(lb, LB_FLOOR))
    log_1m = jnp.log(1.0 - lb)

    rp = DD_NPAR * CHUNK
    trip = _dd_make_trip(q_s, k_s, v_s, la_s, o_s, state_ref, c8_s, p_s, 1)

    def prologue(u, t):
        lo = t * rp
        hs = slice(u * LANE, (u + 1) * LANE)
        cf = f_ref[lo:lo + rp, hs]
        second = log_1m[:, hs] + _log_sigmoid(cf)
        llb = log_lb[:, hs]
        la_s[u, lo:lo + rp, :] = jnp.maximum(llb, second) + jnp.log(1.0 + jnp.exp(-jnp.abs(llb - second)))
        k_s[u, lo:lo + rp, :] = (1.0 - lb[:, hs]) * _sigmoid_t(-cf)
        q_s[u, lo:lo + rp, :] = _silu(q_ref[lo:lo + rp, hs].astype(F32)) * (HGRN_EXPAND ** -0.5)
        v_s[u, lo:lo + rp, :] = v_ref[lo:lo + rp, hs].astype(F32)

    def epilogue(u, t):
        lo = t * rp
        hs = slice(u * LANE, (u + 1) * LANE)
        o = o_s[u, lo:lo + rp, :]
        oh = o * lax.rsqrt(jnp.mean(o * o, axis=-1, keepdims=True) + RMS_EPS) * nw_ref[...]
        o_ref[lo:lo + rp, hs] = (oh * _silu(gate_ref[lo:lo + rp, hs].astype(F32))).astype(o_ref.dtype)

    _dd_schedule([(u, t) for u in range(HGRN_HEADS) for t in range(tb // rp)], prologue, trip, epilogue)


def _hgrn4_call(pb, pf, lb_logits, norm_w, layer, batch, seq, tb=1024):
    tb = min(tb, seq)
    nb = seq // tb
    t = batch * seq
    kern = functools.partial(_hgrn4_kernel, tb=tb, layer=layer)
    return pl.pallas_call(
        kern,
        grid=(batch, nb),
        in_specs=[pl.BlockSpec((tb, HGRN_QK), lambda b, i: (b * nb + i, C_Q // HGRN_QK)),
                  pl.BlockSpec((tb, HGRN_QK), lambda b, i: (b * nb + i, F_CF // HGRN_QK)),
                  pl.BlockSpec((tb, HGRN_V), lambda b, i: (b * nb + i, C_I // HGRN_V)),
                  pl.BlockSpec((tb, HGRN_V), lambda b, i: (b * nb + i, C_G // HGRN_V)),
                  pl.BlockSpec((DEPTH, HGRN_QK), lambda b, i: (0, 0)),
                  pl.BlockSpec((None, 1, HGRN_DV), lambda b, i: (layer, 0, 0))],
        out_specs=pl.BlockSpec((tb, HGRN_V), lambda b, i: (b * nb + i, 0)),
        out_shape=jax.ShapeDtypeStruct((t, HGRN_V), BF16),
        scratch_shapes=_dd_scratch4(HGRN_HEADS, tb, LANE),
        compiler_params=_cparams(("parallel", "arbitrary")),
        name="hgrn",
    )(pb, pf, pb, pb, lb_logits, norm_w)


def _merge_kernel(oa_ref, ob_ref, oc_ref, ma_ref, mb_ref, mc_ref, h_ref,
                  wa_ref, wb_ref, wc_ref, wo_ref, g_ref, b_ref, o_ref):
    y = (_sigmoid_t(ma_ref[...].astype(F32)) * jnp.dot(oa_ref[...], wa_ref[...], preferred_element_type=F32)
         + _sigmoid_t(mb_ref[...].astype(F32)) * jnp.dot(ob_ref[...], wb_ref[...], preferred_element_type=F32)
         + _sigmoid_t(mc_ref[...].astype(F32)) * jnp.dot(oc_ref[...], wc_ref[...], preferred_element_type=F32))
    mix = _mm(y, wo_ref[...])
    o_ref[...] = _layer_norm(ALPHA * h_ref[...] + mix, g_ref[...], b_ref[...])


def _merge_call(o_a, o_b, o_c, pb, h, wa, wb, wc, wo, g, b, layer, tm=512):
    t, d = h.shape
    tm = min(tm, t)

    def row(width):
        return pl.BlockSpec((tm, width), lambda i: (i, 0))

    def wspec(kdim):
        return pl.BlockSpec((None, kdim, d), lambda i: (layer, 0, 0))

    vec = pl.BlockSpec((None, 1, d), lambda i: (layer, 0, 0))
    return pl.pallas_call(
        _merge_kernel,
        grid=(t // tm,),
        in_specs=[row(GDN_V), row(GLA_V), row(HGRN_V),
                  pl.BlockSpec((tm, d), lambda i: (i, M_A // d)),
                  pl.BlockSpec((tm, d), lambda i: (i, M_B // d)),
                  pl.BlockSpec((tm, d), lambda i: (i, M_C // d)),
                  row(d), wspec(GDN_V), wspec(GLA_V), wspec(HGRN_V), wspec(d), vec, vec],
        out_specs=row(d),
        out_shape=jax.ShapeDtypeStruct((t, d), F32),
        compiler_params=_cparams(("parallel",)),
        name="merge",
    )(o_a, o_b, o_c, pb, pb, pb, h, wa, wb, wc, wo, g, b)


def _lane_col(x, idx):
    lane = lax.broadcasted_iota(jnp.int32, x.shape, 1)
    return jnp.sum(jnp.where(lane == idx, x, 0.0), axis=1, keepdims=True)


def _route(scores_t, bias_ref):
    s = [scores_t[e:e + 1, :] for e in range(N_EXPERTS)]
    sel = [s[e] + bias_ref[e:e + 1, 0:1] for e in range(N_EXPERTS)]
    gscore = []
    for g in range(N_GROUPS):
        a, b, c, d = sel[4 * g:4 * g + 4]
        hi1, lo1 = jnp.maximum(a, b), jnp.minimum(a, b)
        hi2, lo2 = jnp.maximum(c, d), jnp.minimum(c, d)
        top1 = jnp.maximum(hi1, hi2)
        top2 = jnp.maximum(jnp.minimum(hi1, hi2), jnp.maximum(lo1, lo2))
        gscore.append(top1 + top2)
    best = gscore[0]
    gidx = jnp.zeros_like(best, dtype=jnp.int32)
    for g in range(1, N_GROUPS):
        take = gscore[g] > best
        best = jnp.where(take, gscore[g], best)
        gidx = jnp.where(take, g, gidx)
    ing, raw = [], []
    for kk in range(EXPERTS_PER_GROUP):
        vs, vr = sel[kk], s[kk]
        for g in range(1, N_GROUPS):
            pick = gidx == g
            vs = jnp.where(pick, sel[4 * g + kk], vs)
            vr = jnp.where(pick, s[4 * g + kk], vr)
        ing.append(vs)
        raw.append(vr)
    b1 = ing[0]
    i1 = jnp.zeros_like(gidx)
    for kk in range(1, EXPERTS_PER_GROUP):
        take = ing[kk] > b1
        b1 = jnp.where(take, ing[kk], b1)
        i1 = jnp.where(take, kk, i1)
    neg = jnp.full_like(b1, -jnp.inf)
    b2 = neg
    i2 = jnp.zeros_like(gidx)
    for kk in range(EXPERTS_PER_GROUP):
        cand = jnp.where(i1 == kk, neg, ing[kk])
        take = cand > b2
        b2 = jnp.where(take, cand, b2)
        i2 = jnp.where(take, kk, i2)
    w1 = raw[0]
    w2 = raw[0]
    for kk in range(1, EXPERTS_PER_GROUP):
        w1 = jnp.where(i1 == kk, raw[kk], w1)
        w2 = jnp.where(i2 == kk, raw[kk], w2)
    tot = w1 + w2
    w1 = w1 / tot
    w2 = w2 / tot
    e1 = gidx * EXPERTS_PER_GROUP + i1
    e2 = gidx * EXPERTS_PER_GROUP + i2
    rows = [jnp.where(e1 == e, w1, 0.0) + jnp.where(e2 == e, w2, 0.0) for e in range(N_EXPERTS)]
    return jnp.concatenate(rows, axis=0)


def _moe_kernel(h_ref, wr_ref, rb_ref, wg_ref, wu_ref, wd_ref, g_ref, b_ref, o_ref, ob_ref,
                comb_ref, xb_ref, *, tm):
    eg = pl.program_id(1)

    @pl.when(eg == 0)
    def _():
        logits_t = _mm_nt_f32(wr_ref[...], h_ref[...])
        comb_t = _route(_sigmoid(logits_t), rb_ref)
        pad = jnp.zeros((LANE - N_EXPERTS, tm), F32)
        comb_ref[...] = jnp.transpose(jnp.concatenate([comb_t, pad], axis=0))
        xb_ref[...] = h_ref[...].astype(BF16)

    x = xb_ref[...]
    comb = comb_ref[...]
    y = None
    for kk in range(MOE_EPS):
        hg = jnp.dot(x, wg_ref[kk], preferred_element_type=F32)
        hu = jnp.dot(x, wu_ref[kk], preferred_element_type=F32)
        cw = _lane_col(comb, eg * MOE_EPS + kk)
        hid = _silu(hg) * hu * cw
        term = jnp.dot(hid.astype(BF16), wd_ref[kk], preferred_element_type=F32)
        y = term if y is None else y + term

    @pl.when(eg == 0)
    def _():
        o_ref[...] = y

    @pl.when(eg > 0)
    def _():
        o_ref[...] += y

    @pl.when(eg == N_EXPERTS // MOE_EPS - 1)
    def _():
        out = _layer_norm(ALPHA * h_ref[...] + o_ref[...], g_ref[...], b_ref[...])
        o_ref[...] = out
        ob_ref[...] = out.astype(BF16)


def _moe_call(h, wr_t, rbias, wg, wu, wd, g, b, layer, tm=1024):
    t, d = h.shape
    tm = min(tm, t)
    kern = functools.partial(_moe_kernel, tm=tm)
    vec = pl.BlockSpec((None, 1, d), lambda i, e: (layer, 0, 0))
    row = pl.BlockSpec((tm, d), lambda i, e: (i, 0))
    return pl.pallas_call(
        kern,
        grid=(t // tm, N_EXPERTS // MOE_EPS),
        in_specs=[row,
                  pl.BlockSpec((N_EXPERTS, d), lambda i, e: (0, 0)),
                  pl.BlockSpec((N_EXPERTS, LANE), lambda i, e: (0, 0)),
                  pl.BlockSpec((None, MOE_EPS, d, D_FF), lambda i, e: (layer, e, 0, 0)),
                  pl.BlockSpec((None, MOE_EPS, d, D_FF), lambda i, e: (layer, e, 0, 0)),
                  pl.BlockSpec((None, MOE_EPS, D_FF, d), lambda i, e: (layer, e, 0, 0)),
                  vec, vec],
        out_specs=[row, row],
        out_shape=[jax.ShapeDtypeStruct((t, d), F32), jax.ShapeDtypeStruct((t, d), BF16)],
        scratch_shapes=[pltpu.VMEM((tm, LANE), F32), pltpu.VMEM((tm, d), BF16)],
        compiler_params=_cparams(("parallel", "arbitrary")),
        name="moe",
    )(h, wr_t, rbias, wg, wu, wd, g, b)


def _pack_w_in(w_in):
    wt = jnp.swapaxes(w_in, 1, 2)
    (a_q, a_k, a_v, a_beta, a_dt, a_g, b_q, b_k, b_v, b_lr, b_g,
     c_q, c_f, c_i, c_g, m_a, m_b, m_c) = jnp.split(wt, SPLIT_POINTS, axis=1)
    depth, _, d = wt.shape
    a_s = jnp.concatenate([a_beta, a_dt, jnp.zeros((depth, LANE - 2 * GDN_HEADS, d), wt.dtype)], 1)
    b_lrp = jnp.concatenate([b_lr, jnp.zeros((depth, LANE - GLA_RANK, d), wt.dtype)], 1)
    wb = jnp.concatenate([m_a, m_b, m_c, a_q, a_k, a_v, a_g, b_q, b_k, b_v, b_g, c_q, c_i, c_g], 1).astype(BF16)
    wf = jnp.concatenate([c_f, a_s, b_lrp], 1).astype(BF16)
    assert wb.shape[1] == NPB and wf.shape[1] == NPF
    return wb, wf


def _prepare(w_in, gdn_conv, gdn_a_log, gdn_dt_bias, gdn_norm, gla_w2, gla_b2, gla_norm, hgrn_lb_logits,
             hgrn_norm, w_br_a, w_br_b, w_br_c, w_out, ln1_g, ln1_b, w_router, router_bias, w_gate, w_up,
             w_down, ln2_g, ln2_b):
    depth = w_in.shape[0]
    d = w_out.shape[-1]
    w_pb, w_pf = _pack_w_in(w_in)
    return dict(
        w_pb=w_pb, w_pf=w_pf,
        gdn_conv=gdn_conv,
        gdn_par=jnp.pad(jnp.stack([gdn_a_log, gdn_dt_bias], axis=1),
                        ((0, 0), (0, SUBLANE - 2), (GDN_HEADS, LANE - 2 * GDN_HEADS))),
        gdn_norm=gdn_norm.reshape(depth, 1, GDN_DV),
        w2p=jnp.concatenate([gla_w2, jnp.zeros((depth, LANE - GLA_RANK, GLA_QK), gla_w2.dtype)], axis=1),
        gla_b2=gla_b2.reshape(depth, 1, GLA_QK),
        gla_norm=gla_norm.reshape(depth, 1, GLA_DV),
        lb_logits=hgrn_lb_logits,
        hgrn_norm=hgrn_norm.reshape(depth, 1, HGRN_DV),
        wa=w_br_a.astype(BF16), wb=w_br_b.astype(BF16), wc=w_br_c.astype(BF16), wo=w_out.astype(BF16),
        ln1_g=ln1_g.reshape(depth, 1, d), ln1_b=ln1_b.reshape(depth, 1, d),
        wr_t=jnp.transpose(w_router),
        rbias=jnp.broadcast_to(router_bias[:, None], (N_EXPERTS, LANE)),
        wg=w_gate, wu=w_up, wd=w_down,
        ln2_g=ln2_g.reshape(depth, 1, d), ln2_b=ln2_b.reshape(depth, 1, d),
    )


def _mixer_block(h, hb, p, layer, batch, seq):
    pb = _inproj_call(hb, p["w_pb"], layer, BF16, 2048, 2048, "inproj_b")
    pf = _inproj_call(hb, p["w_pf"], layer, F32, 1024, NPF, "inproj_f")
    o_a = _gdn_call(pb, pf, p["gdn_conv"], p["gdn_par"], p["gdn_norm"], layer, batch, seq)
    o_b = _gla4_call(pb, pf, p["w2p"], p["gla_b2"], p["gla_norm"], layer, batch, seq)
    o_c = _hgrn4_call(pb, pf, p["lb_logits"], p["hgrn_norm"], layer, batch, seq)
    return _merge_call(o_a, o_b, o_c, pb, h, p["wa"], p["wb"], p["wc"], p["wo"], p["ln1_g"], p["ln1_b"], layer)


MOE_TR = 512
MOE_NBKT = N_GROUPS * 6
NPIECE = D_MODEL // LANE
SC_CH = 64


def _to_token_major(ref, x, rows):
    for j in range(NPIECE):
        ref[pl.ds(j, rows, stride=NPIECE), :] = x[:, j * LANE:(j + 1) * LANE]


def _from_token_major(ref, rows):
    return jnp.concatenate([ref[pl.ds(j, rows, stride=NPIECE), :] for j in range(NPIECE)], axis=1)


def _route_pairs(scores_t, bias_ref):
    s = [scores_t[e:e + 1, :] for e in range(N_EXPERTS)]
    sel = [s[e] + bias_ref[e:e + 1, 0:1] for e in range(N_EXPERTS)]
    gscore = []
    for g in range(N_GROUPS):
        a, b, c, d = sel[4 * g:4 * g + 4]
        hi1, lo1 = jnp.maximum(a, b), jnp.minimum(a, b)
        hi2, lo2 = jnp.maximum(c, d), jnp.minimum(c, d)
        top1 = jnp.maximum(hi1, hi2)
        top2 = jnp.maximum(jnp.minimum(hi1, hi2), jnp.maximum(lo1, lo2))
        gscore.append(top1 + top2)
    best = gscore[0]
    gidx = jnp.zeros_like(best, dtype=jnp.int32)
    for g in range(1, N_GROUPS):
        take = gscore[g] > best
        best = jnp.where(take, gscore[g], best)
        gidx = jnp.where(take, g, gidx)
    ing, raw = [], []
    for kk in range(EXPERTS_PER_GROUP):
        vs, vr = sel[kk], s[kk]
        for g in range(1, N_GROUPS):
            pick = gidx == g
            vs = jnp.where(pick, sel[4 * g + kk], vs)
            vr = jnp.where(pick, s[4 * g + kk], vr)
        ing.append(vs)
        raw.append(vr)
    b1 = ing[0]
    i1 = jnp.zeros_like(gidx)
    for kk in range(1, EXPERTS_PER_GROUP):
        take = ing[kk] > b1
        b1 = jnp.where(take, ing[kk], b1)
        i1 = jnp.where(take, kk, i1)
    neg = jnp.full_like(b1, -jnp.inf)
    b2 = neg
    i2 = jnp.zeros_like(gidx)
    for kk in range(EXPERTS_PER_GROUP):
        cand = jnp.where(i1 == kk, neg, ing[kk])
        take = cand > b2
        b2 = jnp.where(take, cand, b2)
        i2 = jnp.where(take, kk, i2)
    w1 = raw[0]
    w2 = raw[0]
    for kk in range(1, EXPERTS_PER_GROUP):
        w1 = jnp.where(i1 == kk, raw[kk], w1)
        w2 = jnp.where(i2 == kk, raw[kk], w2)
    tot = w1 + w2
    w1 = w1 / tot
    w2 = w2 / tot
    first_lower = i1 < i2
    lo = jnp.where(first_lower, i1, i2)
    hi = jnp.where(first_lower, i2, i1)
    pidx = jnp.where(lo == 0, hi - 1, jnp.where(lo == 1, hi + 1, 5))
    bkt = gidx * 6 + pidx
    return bkt, jnp.where(first_lower, w1, w2), jnp.where(first_lower, w2, w1)


def _moe_route_kernel(h_ref, wr_ref, rb_ref, bkt_ref, rank_ref, wab_ref, cnt_ref, hx_ref, carry_ref, *, tm):
    @pl.when(pl.program_id(0) == 0)
    def _():
        carry_ref[...] = jnp.zeros_like(carry_ref)

    logits_t = _mm_nt_f32(wr_ref[...], h_ref[...])
    bkt, wa, wb = _route_pairs(_sigmoid(logits_t), rb_ref)
    sub = lax.broadcasted_iota(jnp.int32, (32, tm), 0)
    oh = jnp.where(sub == bkt, 1.0, 0.0)
    r = lax.broadcasted_iota(jnp.int32, (tm, tm), 0)
    c = lax.broadcasted_iota(jnp.int32, (tm, tm), 1)
    earlier = jnp.where(r < c, 1.0, 0.0).astype(BF16)
    before = jnp.dot(oh.astype(BF16), earlier, preferred_element_type=F32)
    carry = carry_ref[...]
    rank = jnp.sum(oh * (before + carry[:, 0:1]), axis=0, keepdims=True)
    carry = carry + jnp.sum(oh, axis=1, keepdims=True)
    carry_ref[...] = carry
    cnt_ref[...] = carry
    bkt_ref[...] = bkt
    rank_ref[...] = rank.astype(jnp.int32)
    pad = jnp.zeros((LANE - 2, tm), F32)
    wab_ref[...] = jnp.transpose(jnp.concatenate([wa, wb, pad], axis=0))
    _to_token_major(hx_ref, h_ref[...], tm)


def _moe_tables_kernel(cnt_ref, bkt_ref, rank_ref, pos_ref, tabs_ref, *, t, tr):
    cnt = cnt_ref[...]
    sz = jnp.floor((cnt + (tr - 1)) * (1.0 / tr)) * tr
    r = lax.broadcasted_iota(jnp.int32, (32, 32), 0)
    c = lax.broadcasted_iota(jnp.int32, (32, 32), 1)
    start = jnp.dot(jnp.where(c < r, 1.0, 0.0), sz, preferred_element_type=F32, precision=HIGHEST)
    end = start + sz
    sub = lax.broadcasted_iota(jnp.int32, (32, t), 0)
    pos = jnp.sum(jnp.where(sub == bkt_ref[...], start[:, 0:1], 0.0), axis=0, keepdims=True)
    pos_ref[...] = pos.astype(jnp.int32) + rank_ref[...]
    brow = lax.broadcasted_iota(jnp.int32, (32, LANE), 0)
    tile0 = lax.broadcasted_iota(jnp.int32, (32, LANE), 1).astype(F32) * tr
    tbk = jnp.sum(jnp.where((brow < MOE_NBKT) & (end <= tile0), 1, 0), axis=0, keepdims=True)
    tbk = jnp.minimum(tbk, MOE_NBKT - 1)
    total = end[MOE_NBKT - 1:MOE_NBKT, :]
    valid = jnp.where(tile0[0:1, :] < total, 1, 0)
    g = jnp.where(tbk >= 6, 1, 0) + jnp.where(tbk >= 12, 1, 0) + jnp.where(tbk >= 18, 1, 0)
    p = tbk - 6 * g
    ge3 = jnp.where(p >= 3, 1, 0)
    ge5 = jnp.where(p >= 5, 1, 0)
    ea = 4 * g + ge3 + ge5
    eb = 4 * g + p + 1 - 2 * ge3 - ge5
    zero = jnp.zeros((SUBLANE - 3, LANE), jnp.int32)
    tabs_ref[...] = jnp.concatenate([ea, eb, valid, zero], axis=0)


def _moe_group_kernel(ea_ref, eb_ref, vd_ref, xs_ref, ws_ref, wga_ref, wua_ref, wda_ref,
                      wgb_ref, wub_ref, wdb_ref, ys_ref, wg_s, wu_s, wd_s, *, tr):
    j = pl.program_id(0)
    prev = jnp.maximum(j - 1, 0)

    @pl.when((j == 0) | (ea_ref[j] != ea_ref[prev]))
    def _():
        wg_s[0] = wga_ref[...].astype(BF16)
        wu_s[0] = wua_ref[...].astype(BF16)
        wd_s[0] = wda_ref[...].astype(BF16)

    @pl.when((j == 0) | (eb_ref[j] != eb_ref[prev]))
    def _():
        wg_s[1] = wgb_ref[...].astype(BF16)
        wu_s[1] = wub_ref[...].astype(BF16)
        wd_s[1] = wdb_ref[...].astype(BF16)

    @pl.when(vd_ref[j] > 0)
    def _():
        x = _from_token_major(xs_ref, tr).astype(BF16)
        w = ws_ref[...]

        def ffn(s, cw):
            hg = jnp.dot(x, wg_s[s], preferred_element_type=F32)
            hu = jnp.dot(x, wu_s[s], preferred_element_type=F32)
            hid = _silu(hg) * hu * cw
            return jnp.dot(hid.astype(BF16), wd_s[s], preferred_element_type=F32)

        y = ffn(0, w[:, 0:1]) + ffn(1, w[:, 1:2])
        _to_token_major(ys_ref, y, tr)


def _moe_ln_kernel(h_ref, y_ref, g_ref, b_ref, o_ref, ob_ref, *, tm):
    out = _layer_norm(ALPHA * h_ref[...] + _from_token_major(y_ref, tm), g_ref[...], b_ref[...])
    o_ref[...] = out
    ob_ref[...] = out.astype(BF16)


def _sc_scatter_rows(x3, w2, pos, n_rows):
    from jax.experimental.pallas import tpu_sc as plsc
    t = x3.shape[0]
    info = plsc.get_sparse_core_info()
    nc, ns = info.num_cores, info.num_subcores
    per_w = t // (nc * ns)
    mesh = plsc.VectorSubcoreMesh(core_axis_name="c", subcore_axis_name="s")

    @functools.partial(
        pl.kernel, mesh=mesh,
        out_type=[jax.ShapeDtypeStruct((n_rows,) + x3.shape[1:], x3.dtype),
                  jax.ShapeDtypeStruct((n_rows,) + w2.shape[1:], w2.dtype)],
        scratch_types=[pltpu.VMEM((SC_CH,), jnp.int32), pltpu.VMEM((SC_CH,) + x3.shape[1:], x3.dtype),
                       pltpu.VMEM((SC_CH,) + w2.shape[1:], w2.dtype),
                       pltpu.SemaphoreType.DMA, pltpu.SemaphoreType.DMA])
    def k(x_hbm, w_hbm, idx_hbm, ox_hbm, ow_hbm, idx_v, rows_v, wrows_v, sem_x, sem_w):
        base = (lax.axis_index("s") * nc + lax.axis_index("c")) * per_w

        @pl.loop(0, per_w // SC_CH)
        def _(j):
            off = base + j * SC_CH
            pltpu.sync_copy(idx_hbm.at[pl.ds(off, SC_CH)], idx_v)
            pltpu.sync_copy(x_hbm.at[pl.ds(off, SC_CH)], rows_v)
            pltpu.sync_copy(w_hbm.at[pl.ds(off, SC_CH)], wrows_v)
            cx = pltpu.async_copy(rows_v, ox_hbm.at[idx_v], sem_x)
            cw = pltpu.async_copy(wrows_v, ow_hbm.at[idx_v], sem_w)
            cx.wait()
            cw.wait()

    return k(x3, w2, pos)


def _sc_gather_rows(y3, pos):
    from jax.experimental.pallas import tpu_sc as plsc
    t = pos.shape[0]
    info = plsc.get_sparse_core_info()
    nc, ns = info.num_cores, info.num_subcores
    per_w = t // (nc * ns)
    mesh = plsc.VectorSubcoreMesh(core_axis_name="c", subcore_axis_name="s")

    @functools.partial(
        pl.kernel, mesh=mesh,
        out_type=jax.ShapeDtypeStruct((t,) + y3.shape[1:], y3.dtype),
        scratch_types=[pltpu.VMEM((SC_CH,), jnp.int32), pltpu.VMEM((SC_CH,) + y3.shape[1:], y3.dtype),
                       pltpu.SemaphoreType.DMA])
    def k(y_hbm, idx_hbm, o_hbm, idx_v, rows_v, sem):
        base = (lax.axis_index("s") * nc + lax.axis_index("c")) * per_w

        @pl.loop(0, per_w // SC_CH)
        def _(j):
            off = base + j * SC_CH
            pltpu.sync_copy(idx_hbm.at[pl.ds(off, SC_CH)], idx_v)
            pltpu.async_copy(y_hbm.at[idx_v], rows_v, sem).wait()
            pltpu.sync_copy(rows_v, o_hbm.at[pl.ds(off, SC_CH)])

    return k(y3, pos)


def _moe_sparse(h, wr_t, rbias, wg, wu, wd, g, b, layer, tm=1024):
    t, d = h.shape
    tm = min(tm, t)
    tr = MOE_TR
    nt = t // tr + MOE_NBKT
    n_rows = nt * tr
    row1 = pl.BlockSpec((1, tm), lambda i: (0, i))
    bkt, rank, wab, cnt, hx = pl.pallas_call(
        functools.partial(_moe_route_kernel, tm=tm),
        grid=(t // tm,),
        in_specs=[pl.BlockSpec((tm, d), lambda i: (i, 0)),
                  pl.BlockSpec((N_EXPERTS, d), lambda i: (0, 0)),
                  pl.BlockSpec((N_EXPERTS, LANE), lambda i: (0, 0))],
        out_specs=[row1, row1, pl.BlockSpec((tm, LANE), lambda i: (i, 0)),
                   pl.BlockSpec((32, LANE), lambda i: (0, 0)),
                   pl.BlockSpec((tm * NPIECE, LANE), lambda i: (i, 0))],
        out_shape=[jax.ShapeDtypeStruct((1, t), jnp.int32), jax.ShapeDtypeStruct((1, t), jnp.int32),
                   jax.ShapeDtypeStruct((t, LANE), F32), jax.ShapeDtypeStruct((32, LANE), F32),
                   jax.ShapeDtypeStruct((t * NPIECE, LANE), F32)],
        scratch_shapes=[pltpu.VMEM((32, LANE), F32)],
        compiler_params=_cparams(("arbitrary",)),
        name="moe_route",
    )(h, wr_t, rbias)
    pos, tabs = pl.pallas_call(
        functools.partial(_moe_tables_kernel, t=t, tr=tr),
        out_shape=[jax.ShapeDtypeStruct((1, t), jnp.int32), jax.ShapeDtypeStruct((SUBLANE, LANE), jnp.int32)],
        compiler_params=pltpu.CompilerParams(vmem_limit_bytes=VMEM_LIMIT),
        name="moe_tables",
    )(cnt, bkt, rank)
    pos = pos.reshape(t)
    xs3, ws = _sc_scatter_rows(hx.reshape(t, NPIECE, LANE), wab, pos, n_rows)

    def wspec(which, shape):
        if which == 0:
            return pl.BlockSpec((None, None) + shape, lambda j, ea, eb, vd: (layer, ea[j], 0, 0))
        return pl.BlockSpec((None, None) + shape, lambda j, ea, eb, vd: (layer, eb[j], 0, 0))

    ys = pl.pallas_call(
        functools.partial(_moe_group_kernel, tr=tr),
        grid_spec=pltpu.PrefetchScalarGridSpec(
            num_scalar_prefetch=3,
            grid=(nt,),
            in_specs=[pl.BlockSpec((tr * NPIECE, LANE), lambda j, ea, eb, vd: (j, 0)),
                      pl.BlockSpec((tr, LANE), lambda j, ea, eb, vd: (j, 0)),
                      wspec(0, (d, D_FF)), wspec(0, (d, D_FF)), wspec(0, (D_FF, d)),
                      wspec(1, (d, D_FF)), wspec(1, (d, D_FF)), wspec(1, (D_FF, d))],
            out_specs=pl.BlockSpec((tr * NPIECE, LANE), lambda j, ea, eb, vd: (j, 0)),
            scratch_shapes=[pltpu.VMEM((2, d, D_FF), BF16), pltpu.VMEM((2, d, D_FF), BF16),
                            pltpu.VMEM((2, D_FF, d), BF16)]),
        out_shape=jax.ShapeDtypeStruct((n_rows * NPIECE, LANE), F32),
        compiler_params=_cparams(("arbitrary",)),
        name="moe_experts",
    )(tabs[0, :nt], tabs[1, :nt], tabs[2, :nt], xs3.reshape(n_rows * NPIECE, LANE), ws,
      wg, wu, wd, wg, wu, wd)
    y3 = _sc_gather_rows(ys.reshape(n_rows, NPIECE, LANE), pos)
    vec = pl.BlockSpec((None, 1, d), lambda i: (layer, 0, 0))
    row = pl.BlockSpec((tm, d), lambda i: (i, 0))
    return pl.pallas_call(
        functools.partial(_moe_ln_kernel, tm=tm),
        grid=(t // tm,),
        in_specs=[row, pl.BlockSpec((tm * NPIECE, LANE), lambda i: (i, 0)), vec, vec],
        out_specs=[row, row],
        out_shape=[jax.ShapeDtypeStruct((t, d), F32), jax.ShapeDtypeStruct((t, d), BF16)],
        compiler_params=_cparams(("parallel",)),
        name="moe_ln",
    )(h, y3.reshape(t * NPIECE, LANE), g, b)


def _ffn_block(h, p, layer):
    return _moe_sparse(h, p["wr_t"], p["rbias"], p["wg"], p["wu"], p["wd"], p["ln2_g"], p["ln2_b"], layer)


def kernel(x, ln0_g, ln0_b, w_in, gdn_conv, gdn_a_log, gdn_dt_bias, gdn_norm, gla_w2, gla_b2, gla_norm,
           hgrn_lb_logits, hgrn_norm, w_br_a, w_br_b, w_br_c, w_out, ln1_g, ln1_b, w_router, router_bias,
           w_gate, w_up, w_down, ln2_g, ln2_b):
    batch, seq, d = x.shape
    p = _prepare(w_in, gdn_conv, gdn_a_log, gdn_dt_bias, gdn_norm, gla_w2, gla_b2, gla_norm, hgrn_lb_logits,
                 hgrn_norm, w_br_a, w_br_b, w_br_c, w_out, ln1_g, ln1_b, w_router, router_bias, w_gate, w_up,
                 w_down, ln2_g, ln2_b)
    h, hb = _ln_call(x.reshape(batch * seq, d), ln0_g, ln0_b)
    for layer in range(w_in.shape[0]):
        h = _mixer_block(h, hb, p, layer, batch, seq)
        h, hb = _ffn_block(h, p, layer)
    return h.reshape(batch, seq, d)
```

```python
import functools

import numpy as np
import jax
import jax.numpy as jnp
from jax import lax
from jax.experimental import pallas as pl
from jax.experimental.pallas import tpu as pltpu
from jax.experimental.pallas import tpu_sc as plsc

F32 = jnp.float32
BF16 = jnp.bfloat16
HIGHEST = lax.Precision.HIGHEST

D_MODEL = 1024
DEPTH = 4
CHUNK = 64
GDN_HEADS, GDN_DK, GDN_DV, CONV_W = 4, 128, 128, 4
GLA_HEADS, GLA_DK, GLA_DV, GLA_RANK, GLA_NORMALIZER = 4, 64, 128, 16, 16.0
HGRN_HEADS, HGRN_EXPAND, HGRN_DV = 4, 128, 128
LB_FLOOR = 1e-30
N_EXPERTS, N_GROUPS, TOP_K, D_FF = 16, 4, 2, 256
EXPERTS_PER_GROUP = N_EXPERTS // N_GROUPS
ALPHA = (2.0 * DEPTH) ** 0.25
LN_EPS = 1e-5
RMS_EPS = 1e-6

GDN_QK = GDN_HEADS * GDN_DK
GDN_V = GDN_HEADS * GDN_DV
GLA_QK = GLA_HEADS * GLA_DK
GLA_V = GLA_HEADS * GLA_DV
HGRN_QK = HGRN_HEADS * HGRN_EXPAND
HGRN_V = HGRN_HEADS * HGRN_DV
SPLIT_SIZES = (GDN_QK, GDN_QK, GDN_V, GDN_HEADS, GDN_HEADS, GDN_V,
               GLA_QK, GLA_QK, GLA_V, GLA_RANK, GLA_V,
               HGRN_QK, HGRN_QK, HGRN_V, HGRN_V,
               D_MODEL, D_MODEL, D_MODEL)
SPLIT_POINTS = tuple(int(v) for v in np.cumsum(SPLIT_SIZES)[:-1])

LANE = 128
SUBLANE = 8
VMEM_LIMIT = 48 * 1024 * 1024

M_A, M_B, M_C = 0, 1024, 2048
A_Q, A_K, A_V, A_G = 3072, 3584, 4096, 4608
B_Q, B_K, B_V, B_G = 5120, 5376, 5632, 6144
C_Q, C_I, C_G = 6656, 7168, 7680
NPB = 8192
F_CF, F_AS, F_LR = 0, 512, 640
NPF = 768

TB_MIX = 1024
TM_ROWS = 512
TM_MOE = 1024
TM_PROJ, TN_PROJ = 2048, 2048
SUB = 8
DD_NPAR = 4
GDN_NPAR = 4
MOE_TR = 512
MOE_NBKT = N_GROUPS * 6
NPIECE = D_MODEL // LANE
SC_CH = 64


def _cparams(sem):
    return pltpu.CompilerParams(dimension_semantics=sem, vmem_limit_bytes=VMEM_LIMIT)


def _mm(a, b):
    return jnp.dot(a.astype(BF16), b.astype(BF16), preferred_element_type=F32)


def _mm_nt(a, b):
    return lax.dot_general(a.astype(BF16), b.astype(BF16), (((1,), (1,)), ((), ())),
                           preferred_element_type=F32)


def _mm_tn(a, b):
    return lax.dot_general(a.astype(BF16), b.astype(BF16), (((0,), (0,)), ((), ())),
                           preferred_element_type=F32)


def _mm_nt_f32(a, b):
    return lax.dot_general(a, b, (((1,), (1,)), ((), ())), preferred_element_type=F32,
                           precision=HIGHEST)


def _split3(x):
    hi = x.astype(BF16)
    r = x - hi.astype(F32)
    mid = r.astype(BF16)
    lo = (r - mid.astype(F32)).astype(BF16)
    return hi, mid, lo


def _mm_01(m01, x):
    hi, mid, lo = _split3(x)
    return (jnp.dot(m01, hi, preferred_element_type=F32) + jnp.dot(m01, mid, preferred_element_type=F32)
            + jnp.dot(m01, lo, preferred_element_type=F32))


def _sigmoid(x):
    return 1.0 / (1.0 + jnp.exp(-x))


def _sigmoid_t(x):
    return 0.5 * jnp.tanh(0.5 * x) + 0.5


def _silu(x):
    return x * _sigmoid_t(x)


def _softplus(x):
    return jnp.maximum(x, 0.0) + jnp.log(1.0 + jnp.exp(-jnp.abs(x)))


def _log_sigmoid(x):
    return -_softplus(-x)


def _layer_norm(x, g, b):
    mu = jnp.mean(x, axis=-1, keepdims=True)
    xc = x - mu
    var = jnp.mean(xc * xc, axis=-1, keepdims=True)
    return xc * lax.rsqrt(var + LN_EPS) * g + b


def _to_token_major(ref, x, rows):
    for j in range(NPIECE):
        ref[pl.ds(j, rows, stride=NPIECE), :] = x[:, j * LANE:(j + 1) * LANE]


def _from_token_major(ref, rows):
    return jnp.concatenate([ref[pl.ds(j, rows, stride=NPIECE), :] for j in range(NPIECE)], axis=1)


def _ln_kernel(x_ref, g_ref, b_ref, o_ref, ob_ref):
    y = _layer_norm(x_ref[...], g_ref[...], b_ref[...])
    o_ref[...] = y
    ob_ref[...] = y.astype(BF16)


def _ln_call(x, g, b):
    t, d = x.shape
    tm = min(TM_ROWS, t)
    return pl.pallas_call(
        _ln_kernel,
        grid=(t // tm,),
        in_specs=[pl.BlockSpec((tm, d), lambda i: (i, 0)),
                  pl.BlockSpec((1, d), lambda i: (0, 0)),
                  pl.BlockSpec((1, d), lambda i: (0, 0))],
        out_specs=[pl.BlockSpec((tm, d), lambda i: (i, 0)), pl.BlockSpec((tm, d), lambda i: (i, 0))],
        out_shape=[jax.ShapeDtypeStruct((t, d), F32), jax.ShapeDtypeStruct((t, d), BF16)],
        compiler_params=_cparams(("parallel",)),
        name="ln0",
    )(x, g.reshape(1, d), b.reshape(1, d))


def _inproj_kernel(x_ref, wt_ref, o_ref):
    o_ref[...] = lax.dot_general(x_ref[...], wt_ref[...], (((1,), (1,)), ((), ())),
                                 preferred_element_type=F32).astype(o_ref.dtype)


def _inproj_call(hb, w, layer, out_dtype, tm, tn, name):
    t, d = hb.shape
    tm = min(tm, t)
    n = w.shape[-2]
    return pl.pallas_call(
        _inproj_kernel,
        grid=(n // tn, t // tm),
        in_specs=[pl.BlockSpec((tm, d), lambda j, i: (i, 0)),
                  pl.BlockSpec((None, tn, d), lambda j, i: (layer, j, 0))],
        out_specs=pl.BlockSpec((tm, tn), lambda j, i: (i, j)),
        out_shape=jax.ShapeDtypeStruct((t, n), out_dtype),
        compiler_params=_cparams(("parallel", "parallel")),
        name=name,
    )(hb, w)


def _schedule(units, prologue, trip, epilogue):
    prologue(*units[0])
    for n, u in enumerate(units):
        if n + 1 < len(units):
            prologue(*units[n + 1])
        trip(*u, n % 2)
        epilogue(*u)


def _gdn_kernel(qkv_ref, s_ref, gate_ref, cw_ref, par_ref, nw_ref, o_ref,
                state_ref, tail_ref, xbuf_ref, q_s, k_s, v_s, cumb_s, betab_s, cumrow_s, o_s, *, tb):
    nc = tb // CHUNK
    nh = GDN_HEADS
    off0 = SUBLANE - (CONV_W - 1)

    @pl.when(pl.program_id(1) == 0)
    def _():
        state_ref[...] = jnp.zeros_like(state_ref)
        tail_ref[...] = jnp.zeros_like(tail_ref)

    row = lax.broadcasted_iota(jnp.int32, (CHUNK, CHUNK), 0)
    col = lax.broadcasted_iota(jnp.int32, (CHUNK, CHUNK), 1)
    incl = col <= row
    strict = col < row
    tri = jnp.where(incl, 1.0, 0.0).astype(BF16)
    eye = jnp.where(col == row, 1.0, 0.0).astype(F32)
    npar = GDN_NPAR
    rp = npar * CHUNK
    xbuf_ref[0:SUBLANE, :] = tail_ref[...]

    def prologue(t):
        lo = t * rp
        xbuf_ref[SUBLANE + lo:SUBLANE + lo + rp, :] = qkv_ref[lo:lo + rp, :].astype(F32)
        for j in range(3 * nh):
            cs = slice(j * LANE, (j + 1) * LANE)
            y = xbuf_ref[off0 + lo:off0 + lo + rp, cs] * cw_ref[0:1, cs]
            for kk in range(1, CONV_W):
                y = y + xbuf_ref[off0 + kk + lo:off0 + kk + lo + rp, cs] * cw_ref[kk:kk + 1, cs]
            y = _silu(y)
            if j < nh:
                q_s[j, lo:lo + rp, :] = (y * lax.rsqrt(jnp.sum(y * y, axis=-1, keepdims=True) + RMS_EPS)
                                         * (GDN_DK ** -0.5))
            elif j < 2 * nh:
                k_s[j - nh, lo:lo + rp, :] = y * lax.rsqrt(jnp.sum(y * y, axis=-1, keepdims=True) + RMS_EPS)
            else:
                v_s[j - 2 * nh, lo:lo + rp, :] = y
        sc = s_ref[lo:lo + rp, :]
        beta_all = _sigmoid_t(sc)
        g_all = -jnp.exp(par_ref[0:1, :]) * _softplus(sc + par_ref[1:2, :])
        cum_all = jnp.concatenate([_mm_01(tri, g_all[c * CHUNK:(c + 1) * CHUNK, :]) for c in range(npar)], axis=0)
        cum_t = jnp.transpose(cum_all)
        for c in range(npar):
            cumrow_s[t * npar + c] = cum_t[0:SUBLANE, c * CHUNK:(c + 1) * CHUNK]
        for h in range(nh):
            cumb_s[h, lo:lo + rp, :] = jnp.broadcast_to(cum_all[:, nh + h:nh + h + 1], (rp, LANE))
            betab_s[h, lo:lo + rp, :] = jnp.broadcast_to(beta_all[:, h:h + 1], (rp, LANE))

    def trip(cp, _slot):
        chains = [(cp * npar + cc, h) for cc in range(npar) for h in range(nh)]
        r0s = [c * CHUNK for c, _ in chains]
        qc = [q_s[h, r0:r0 + CHUNK, :] for (_, h), r0 in zip(chains, r0s)]
        kc = [k_s[h, r0:r0 + CHUNK, :] for (_, h), r0 in zip(chains, r0s)]
        vc = [v_s[h, r0:r0 + CHUNK, :] for (_, h), r0 in zip(chains, r0s)]
        cumc = [cumb_s[h, r0:r0 + CHUNK, :] for (_, h), r0 in zip(chains, r0s)]
        bc = [betab_s[h, r0:r0 + CHUNK, :] for (_, h), r0 in zip(chains, r0s)]
        n = len(chains)
        kk = [_mm_nt(kc[i], kc[i]) for i in range(n)]
        qk = [_mm_nt(qc[i], kc[i]) for i in range(n)]
        decay = []
        for i, (c, h) in enumerate(chains):
            diff = cumc[i][:, 0:CHUNK] - cumrow_s[c][nh + h:nh + h + 1, :]
            decay.append(jnp.where(incl, jnp.exp(jnp.where(incl, diff, 0.0)), 0.0))
        a = [jnp.where(strict, bc[i][:, 0:CHUNK] * kk[i] * decay[i], 0.0) for i in range(n)]
        x = [eye - a[i] for i in range(n)]
        p = [_mm(a[i], a[i]) for i in range(n)]
        for it in range(5):
            x = [x[i] + _mm(x[i], p[i]) for i in range(n)]
            if it < 4:
                p = [_mm(p[i], p[i]) for i in range(n)]
        ecum = [jnp.exp(cumc[i]) for i in range(n)]
        sol = [_mm(x[i], jnp.concatenate([vc[i] * bc[i], kc[i] * (bc[i] * ecum[i])], axis=1)) for i in range(n)]
        attn = [qk[i] * decay[i] for i in range(n)]
        cum_last = [cumc[i][CHUNK - 1:CHUNK, :] for i in range(n)]
        k_state = [kc[i] * jnp.exp(cum_last[i] - cumc[i]) for i in range(n)]
        wqi = [jnp.concatenate([sol[i][:, GDN_DV:GDN_DV + GDN_DK], qc[i] * ecum[i]], axis=0) for i in range(n)]
        for cc in range(npar):
            idx = [cc * nh + h for h in range(nh)]
            s = [state_ref[h] for h in range(nh)]
            wq = [_mm(wqi[i], s[h]) for h, i in enumerate(idx)]
            v_new = [sol[i][:, 0:GDN_DV] - wq[h][0:CHUNK, :] for h, i in enumerate(idx)]
            av = [_mm(attn[i], v_new[h]) for h, i in enumerate(idx)]
            upd = [_mm_tn(k_state[i], v_new[h]) for h, i in enumerate(idx)]
            for h, i in enumerate(idx):
                state_ref[h] = s[h] * jnp.exp(cum_last[i]) + upd[h]
                o_s[r0s[i]:r0s[i] + CHUNK, h * LANE:(h + 1) * LANE] = wq[h][CHUNK:2 * CHUNK, :] + av[h]

    def epilogue(t):
        lo = t * rp
        for h in range(nh):
            hs = slice(h * LANE, (h + 1) * LANE)
            o = o_s[lo:lo + rp, hs]
            oh = o * lax.rsqrt(jnp.mean(o * o, axis=-1, keepdims=True) + RMS_EPS) * nw_ref[...]
            o_ref[lo:lo + rp, hs] = (oh * _silu(gate_ref[lo:lo + rp, hs].astype(F32))).astype(o_ref.dtype)

    _schedule([(t,) for t in range(nc // npar)], prologue, trip, epilogue)
    tail_ref[...] = xbuf_ref[tb:tb + SUBLANE, :]


def _gdn_call(pb, pf, conv_w, par, norm_w, layer, batch, seq):
    tb = min(TB_MIX, seq)
    nb = seq // tb
    nc = tb // CHUNK
    t = batch * seq
    wq = 2 * GDN_QK + GDN_V
    kern = functools.partial(_gdn_kernel, tb=tb)
    return pl.pallas_call(
        kern,
        grid=(batch, nb),
        in_specs=[pl.BlockSpec((tb, wq), lambda b, i: (b * nb + i, A_Q // wq)),
                  pl.BlockSpec((tb, LANE), lambda b, i: (b * nb + i, F_AS // LANE)),
                  pl.BlockSpec((tb, GDN_V), lambda b, i: (b * nb + i, A_G // GDN_V)),
                  pl.BlockSpec((None, CONV_W, wq), lambda b, i: (layer, 0, 0)),
                  pl.BlockSpec((None, SUBLANE, LANE), lambda b, i: (layer, 0, 0)),
                  pl.BlockSpec((None, 1, GDN_DV), lambda b, i: (layer, 0, 0))],
        out_specs=pl.BlockSpec((tb, GDN_V), lambda b, i: (b * nb + i, 0)),
        out_shape=jax.ShapeDtypeStruct((t, GDN_V), BF16),
        scratch_shapes=[pltpu.VMEM((GDN_HEADS, GDN_DK, GDN_DV), F32),
                        pltpu.VMEM((SUBLANE, wq), F32),
                        pltpu.VMEM((tb + SUBLANE, wq), F32),
                        pltpu.VMEM((GDN_HEADS, tb, LANE), F32),
                        pltpu.VMEM((GDN_HEADS, tb, LANE), F32),
                        pltpu.VMEM((GDN_HEADS, tb, LANE), F32),
                        pltpu.VMEM((GDN_HEADS, tb, LANE), F32),
                        pltpu.VMEM((GDN_HEADS, tb, LANE), F32),
                        pltpu.VMEM((nc, SUBLANE, CHUNK), F32),
                        pltpu.VMEM((tb, GDN_V), F32)],
        compiler_params=_cparams(("parallel", "arbitrary")),
        name="gdn",
    )(pb, pf, pb, conv_w, par, norm_w)


def _dd_make_trip(q_s, k_s, v_s, la_s, o_s, state_ref, c8_s, p_s, g_heads):
    nblk = CHUNK // SUB
    dkh = LANE // g_heads
    dvp = g_heads * LANE
    npar = DD_NPAR
    row = lax.broadcasted_iota(jnp.int32, (CHUNK, CHUNK), 0)
    col = lax.broadcasted_iota(jnp.int32, (CHUNK, CHUNK), 1)
    level_masks = []
    for sh in (5, 4, 3):
        same2b = jnp.right_shift(row, sh + 1) == jnp.right_shift(col, sh + 1)
        upper = (jnp.right_shift(row, sh) & 1) == 1
        lower = (jnp.right_shift(col, sh) & 1) == 0
        level_masks.append(jnp.where(same2b, jnp.where(upper, jnp.where(lower, 1.0, 0.0), 0.0), 0.0))
    tri8 = jnp.where(jnp.right_shift(row, 3) == jnp.right_shift(col, 3),
                     jnp.where(col <= row, 1.0, 0.0), 0.0).astype(BF16)
    lane128 = lax.broadcasted_iota(jnp.int32, (CHUNK, LANE), 1)
    head_masks = [jnp.where((lane128 >= g * dkh) & (lane128 < (g + 1) * dkh), 1.0, 0.0)
                  for g in range(g_heads)]
    sub = lax.broadcasted_iota(jnp.int32, (SUB, LANE), 0)
    dk_sh = dkh.bit_length() - 1
    lane_sh = LANE.bit_length() - 1
    orow = lax.broadcasted_iota(jnp.int32, (LANE, dvp), 0)
    ocol = lax.broadcasted_iota(jnp.int32, (LANE, dvp), 1)
    ones_bd = jnp.where(jnp.right_shift(orow, dk_sh) == jnp.right_shift(ocol, lane_sh), 1.0, 0.0).astype(BF16)
    srow = lax.broadcasted_iota(jnp.int32, (dvp, LANE), 0)
    scol = lax.broadcasted_iota(jnp.int32, (dvp, LANE), 1)
    state_mask = jnp.where(jnp.right_shift(srow, lane_sh) == jnp.right_shift(scol, dk_sh), 1.0, 0.0)

    def block_sums(ci):
        c8 = [c8_s[ci, b * SUB:(b + 1) * SUB, :] for b in range(nblk)]
        t8 = [c8_s[ci, (b + 1) * SUB - 1:(b + 1) * SUB, :] for b in range(nblk)]
        t16 = [t8[2 * b] + t8[2 * b + 1] for b in range(nblk // 2)]
        t32 = [t16[2 * b] + t16[2 * b + 1] for b in range(nblk // 4)]
        t64 = t32[0] + t32[1]
        c16 = [c8[b] + t8[b - 1] if b % 2 else c8[b] for b in range(nblk)]
        c32 = [c16[b] + t16[b // 2 - 1] if (b // 2) % 2 else c16[b] for b in range(nblk)]
        c64 = [c32[b] + t32[0] if b >= nblk // 2 else c32[b] for b in range(nblk)]
        pre = {8: c8, 16: c16, 32: c32, 64: c64}
        suf = {8: [t8[b] - c8[b] for b in range(nblk)],
               16: [t16[b // 2] - c16[b] for b in range(nblk)],
               32: [t32[b // 4] - c32[b] for b in range(nblk)],
               64: [t64 - c64[b] for b in range(nblk)]}
        return pre, suf

    def cat(pieces):
        return jnp.concatenate(pieces, axis=0)

    def trip(u, t, slot):
        rng = range(npar)
        r0s = [(t * npar + i) * CHUNK for i in rng]
        cis = [slot * npar + i for i in rng]
        qc = [q_s[u, r0:r0 + CHUNK, :] for r0 in r0s]
        kc = [k_s[u, r0:r0 + CHUNK, :] for r0 in r0s]
        vc = [v_s[u, r0:r0 + CHUNK, :] for r0 in r0s]
        c8_all = _mm_01(tri8, jnp.concatenate([la_s[u, r0:r0 + CHUNK, :] for r0 in r0s], axis=1))
        for i in rng:
            c8_s[cis[i]] = c8_all[:, i * LANE:(i + 1) * LANE]
        sums = [block_sums(cis[i]) for i in rng]
        attn = [[None] * g_heads for _ in rng]
        for li, b in enumerate((32, 16, 8)):
            qs = [qc[i] * jnp.exp(cat(sums[i][0][b])) for i in rng]
            ks = [kc[i] * jnp.exp(cat(sums[i][1][b])) for i in rng]
            for g in range(g_heads):
                for i in rng:
                    qg = qs[i] * head_masks[g] if g_heads > 1 else qs[i]
                    term = _mm_nt(qg, ks[i]) * level_masks[li]
                    attn[i][g] = term if li == 0 else attn[i][g] + term
        o = [jnp.concatenate([_mm(attn[i][g], vc[i][:, g * LANE:(g + 1) * LANE]) for g in range(g_heads)], axis=1)
             if g_heads > 1 else _mm(attn[i][0], vc[i]) for i in rng]
        for i in rng:
            c8 = sums[i][0][8]
            for r in range(nblk):
                qr = qc[i][r * SUB:(r + 1) * SUB, :]
                for jj in range(SUB):
                    rr = r0s[i] + r * SUB + jj
                    krow = k_s[u, rr:rr + 1, :]
                    crow = c8_s[cis[i], r * SUB + jj:r * SUB + jj + 1, :]
                    pr = qr * krow * jnp.exp(jnp.where(sub >= jj, c8[r] - crow, -jnp.inf))
                    p_s[cis[i], (r * SUB + jj) * SUB:(r * SUB + jj + 1) * SUB, :] = pr
        rs = [jnp.dot(p_s[cis[i]].astype(BF16), ones_bd, preferred_element_type=F32) for i in rng]
        for i in rng:
            od = []
            for r in range(nblk):
                acc = None
                for jj in range(SUB):
                    rr = r0s[i] + r * SUB + jj
                    term = rs[i][(r * SUB + jj) * SUB:(r * SUB + jj + 1) * SUB, :] * v_s[u, rr:rr + 1, :]
                    acc = term if acc is None else acc + term
                od.append(acc)
            o[i] = o[i] + cat(od)
        q_inter = [qc[i] * jnp.exp(cat(sums[i][0][64])) for i in rng]
        k_state = [kc[i] * jnp.exp(cat(sums[i][1][64])) for i in rng]
        upd = [_mm_tn(vc[i], k_state[i]) for i in rng]
        st = state_ref[u]
        for i in rng:
            o_s[u, r0s[i]:r0s[i] + CHUNK, :] = o[i] + _mm_nt(q_inter[i], st)
            decay_last = jnp.exp(sums[i][0][64][nblk - 1][SUB - 1:SUB, :])
            st = st * decay_last + (upd[i] * state_mask if g_heads > 1 else upd[i])
        state_ref[u] = st

    return trip


def _dd_scratch(nu, tb, dvp):
    return [pltpu.VMEM((nu, dvp, LANE), F32),
            pltpu.VMEM((nu, tb, LANE), F32), pltpu.VMEM((nu, tb, LANE), F32),
            pltpu.VMEM((nu, tb, dvp), F32),
            pltpu.VMEM((nu, tb, LANE), F32),
            pltpu.VMEM((nu, tb, dvp), F32),
            pltpu.VMEM((2 * DD_NPAR, CHUNK, LANE), F32),
            pltpu.VMEM((2 * DD_NPAR, CHUNK * SUB, LANE), F32)]


def _gla_kernel(q_ref, k_ref, v_ref, lr_ref, gate_ref, w2_ref, b2_ref, nw_ref,
                o_ref, state_ref, q_s, k_s, v_s, la_s, o_s, c8_s, p_s, *, tb):
    @pl.when(pl.program_id(1) == 0)
    def _():
        state_ref[...] = jnp.zeros_like(state_ref)

    npair = GLA_HEADS // 2
    rp = DD_NPAR * CHUNK
    trip = _dd_make_trip(q_s, k_s, v_s, la_s, o_s, state_ref, c8_s, p_s, 2)

    def prologue(u, t):
        lo = t * rp
        ls = slice(u * LANE, (u + 1) * LANE)
        vs = slice(u * 2 * LANE, (u + 1) * 2 * LANE)
        q_s[u, lo:lo + rp, :] = q_ref[lo:lo + rp, ls].astype(F32) * (GLA_DK ** -0.5)
        k_s[u, lo:lo + rp, :] = k_ref[lo:lo + rp, ls].astype(F32)
        v_s[u, lo:lo + rp, :] = v_ref[lo:lo + rp, vs].astype(F32)
        z = _mm(lr_ref[lo:lo + rp, :], w2_ref[:, ls]) + b2_ref[:, ls]
        la_s[u, lo:lo + rp, :] = _log_sigmoid(z) * (1.0 / GLA_NORMALIZER)

    def epilogue(u, t):
        lo = t * rp
        for g in range(2):
            hs = slice((2 * u + g) * LANE, (2 * u + g + 1) * LANE)
            o = o_s[u, lo:lo + rp, g * LANE:(g + 1) * LANE]
            oh = o * lax.rsqrt(jnp.mean(o * o, axis=-1, keepdims=True) + RMS_EPS) * nw_ref[...]
            o_ref[lo:lo + rp, hs] = (oh * _silu(gate_ref[lo:lo + rp, hs].astype(F32))).astype(o_ref.dtype)

    _schedule([(u, t) for u in range(npair) for t in range(tb // rp)], prologue, trip, epilogue)


def _gla_call(pb, pf, w2p, b2, norm_w, layer, batch, seq):
    tb = min(TB_MIX, seq)
    nb = seq // tb
    t = batch * seq
    kern = functools.partial(_gla_kernel, tb=tb)
    return pl.pallas_call(
        kern,
        grid=(batch, nb),
        in_specs=[pl.BlockSpec((tb, GLA_QK), lambda b, i: (b * nb + i, B_Q // GLA_QK)),
                  pl.BlockSpec((tb, GLA_QK), lambda b, i: (b * nb + i, B_K // GLA_QK)),
                  pl.BlockSpec((tb, GLA_V), lambda b, i: (b * nb + i, B_V // GLA_V)),
                  pl.BlockSpec((tb, LANE), lambda b, i: (b * nb + i, F_LR // LANE)),
                  pl.BlockSpec((tb, GLA_V), lambda b, i: (b * nb + i, B_G // GLA_V)),
                  pl.BlockSpec((None, LANE, GLA_QK), lambda b, i: (layer, 0, 0)),
                  pl.BlockSpec((None, 1, GLA_QK), lambda b, i: (layer, 0, 0)),
                  pl.BlockSpec((None, 1, GLA_DV), lambda b, i: (layer, 0, 0))],
        out_specs=pl.BlockSpec((tb, GLA_V), lambda b, i: (b * nb + i, 0)),
        out_shape=jax.ShapeDtypeStruct((t, GLA_V), BF16),
        scratch_shapes=_dd_scratch(GLA_HEADS // 2, tb, 2 * LANE),
        compiler_params=_cparams(("parallel", "arbitrary")),
        name="gla",
    )(pb, pb, pb, pf, pb, w2p, b2, norm_w)


def _hgrn_kernel(q_ref, f_ref, v_ref, gate_ref, lbl_ref, nw_ref,
                 o_ref, state_ref, q_s, k_s, v_s, la_s, o_s, c8_s, p_s, *, tb, layer):
    @pl.when(pl.program_id(1) == 0)
    def _():
        state_ref[...] = jnp.zeros_like(state_ref)

    logits = lbl_ref[...]
    mx = jnp.max(logits, axis=0, keepdims=True)
    ex = jnp.exp(logits - mx)
    p = ex / jnp.sum(ex, axis=0, keepdims=True)
    acc = p[0:1, :]
    for r in range(1, layer + 1):
        acc = acc + p[r:r + 1, :]
    lb = jnp.clip(acc - p[0:1, :], 0.0, 1.0)
    log_lb = jnp.log(jnp.maximum(lb, LB_FLOOR))
    log_1m = jnp.log(1.0 - lb)

    rp = DD_NPAR * CHUNK
    trip = _dd_make_trip(q_s, k_s, v_s, la_s, o_s, state_ref, c8_s, p_s, 1)

    def prologue(u, t):
        lo = t * rp
        hs = slice(u * LANE, (u + 1) * LANE)
        cf = f_ref[lo:lo + rp, hs]
        second = log_1m[:, hs] + _log_sigmoid(cf)
        llb = log_lb[:, hs]
        la_s[u, lo:lo + rp, :] = jnp.maximum(llb, second) + jnp.log(1.0 + jnp.exp(-jnp.abs(llb - second)))
        k_s[u, lo:lo + rp, :] = (1.0 - lb[:, hs]) * _sigmoid_t(-cf)
        q_s[u, lo:lo + rp, :] = _silu(q_ref[lo:lo + rp, hs].astype(F32)) * (HGRN_EXPAND ** -0.5)
        v_s[u, lo:lo + rp, :] = v_ref[lo:lo + rp, hs].astype(F32)

    def epilogue(u, t):
        lo = t * rp
        hs = slice(u * LANE, (u + 1) * LANE)
        o = o_s[u, lo:lo + rp, :]
        oh = o * lax.rsqrt(jnp.mean(o * o, axis=-1, keepdims=True) + RMS_EPS) * nw_ref[...]
        o_ref[lo:lo + rp, hs] = (oh * _silu(gate_ref[lo:lo + rp, hs].astype(F32))).astype(o_ref.dtype)

    _schedule([(u, t) for u in range(HGRN_HEADS) for t in range(tb // rp)], prologue, trip, epilogue)


def _hgrn_call(pb, pf, lb_logits, norm_w, layer, batch, seq):
    tb = min(TB_MIX, seq)
    nb = seq // tb
    t = batch * seq
    kern = functools.partial(_hgrn_kernel, tb=tb, layer=layer)
    return pl.pallas_call(
        kern,
        grid=(batch, nb),
        in_specs=[pl.BlockSpec((tb, HGRN_QK), lambda b, i: (b * nb + i, C_Q // HGRN_QK)),
                  pl.BlockSpec((tb, HGRN_QK), lambda b, i: (b * nb + i, F_CF // HGRN_QK)),
                  pl.BlockSpec((tb, HGRN_V), lambda b, i: (b * nb + i, C_I // HGRN_V)),
                  pl.BlockSpec((tb, HGRN_V), lambda b, i: (b * nb + i, C_G // HGRN_V)),
                  pl.BlockSpec((DEPTH, HGRN_QK), lambda b, i: (0, 0)),
                  pl.BlockSpec((None, 1, HGRN_DV), lambda b, i: (layer, 0, 0))],
        out_specs=pl.BlockSpec((tb, HGRN_V), lambda b, i: (b * nb + i, 0)),
        out_shape=jax.ShapeDtypeStruct((t, HGRN_V), BF16),
        scratch_shapes=_dd_scratch(HGRN_HEADS, tb, LANE),
        compiler_params=_cparams(("parallel", "arbitrary")),
        name="hgrn",
    )(pb, pf, pb, pb, lb_logits, norm_w)


def _merge_kernel(oa_ref, ob_ref, oc_ref, ma_ref, mb_ref, mc_ref, h_ref,
                  wa_ref, wb_ref, wc_ref, wo_ref, g_ref, b_ref, ox_ref, *, tm):
    y = (_sigmoid_t(ma_ref[...].astype(F32)) * jnp.dot(oa_ref[...], wa_ref[...], preferred_element_type=F32)
         + _sigmoid_t(mb_ref[...].astype(F32)) * jnp.dot(ob_ref[...], wb_ref[...], preferred_element_type=F32)
         + _sigmoid_t(mc_ref[...].astype(F32)) * jnp.dot(oc_ref[...], wc_ref[...], preferred_element_type=F32))
    mix = _mm(y, wo_ref[...])
    _to_token_major(ox_ref, _layer_norm(ALPHA * h_ref[...] + mix, g_ref[...], b_ref[...]), tm)


def _merge_call(o_a, o_b, o_c, pb, h, wa, wb, wc, wo, g, b, layer):
    t, d = h.shape
    tm = min(TM_ROWS, t)

    def row(width):
        return pl.BlockSpec((tm, width), lambda i: (i, 0))

    def wspec(kdim):
        return pl.BlockSpec((None, kdim, d), lambda i: (layer, 0, 0))

    vec = pl.BlockSpec((None, 1, d), lambda i: (layer, 0, 0))
    return pl.pallas_call(
        functools.partial(_merge_kernel, tm=tm),
        grid=(t // tm,),
        in_specs=[row(GDN_V), row(GLA_V), row(HGRN_V),
                  pl.BlockSpec((tm, d), lambda i: (i, M_A // d)),
                  pl.BlockSpec((tm, d), lambda i: (i, M_B // d)),
                  pl.BlockSpec((tm, d), lambda i: (i, M_C // d)),
                  row(d), wspec(GDN_V), wspec(GLA_V), wspec(HGRN_V), wspec(d), vec, vec],
        out_specs=pl.BlockSpec((tm * NPIECE, LANE), lambda i: (i, 0)),
        out_shape=jax.ShapeDtypeStruct((t * NPIECE, LANE), F32),
        compiler_params=_cparams(("parallel",)),
        name="merge",
    )(o_a, o_b, o_c, pb, pb, pb, h, wa, wb, wc, wo, g, b)


def _route_pairs(scores_t, bias_ref):
    s = [scores_t[e:e + 1, :] for e in range(N_EXPERTS)]
    sel = [s[e] + bias_ref[e:e + 1, 0:1] for e in range(N_EXPERTS)]
    gscore = []
    for g in range(N_GROUPS):
        a, b, c, d = sel[4 * g:4 * g + 4]
        hi1, lo1 = jnp.maximum(a, b), jnp.minimum(a, b)
        hi2, lo2 = jnp.maximum(c, d), jnp.minimum(c, d)
        top1 = jnp.maximum(hi1, hi2)
        top2 = jnp.maximum(jnp.minimum(hi1, hi2), jnp.maximum(lo1, lo2))
        gscore.append(top1 + top2)
    best = gscore[0]
    gidx = jnp.zeros_like(best, dtype=jnp.int32)
    for g in range(1, N_GROUPS):
        take = gscore[g] > best
        best = jnp.where(take, gscore[g], best)
        gidx = jnp.where(take, g, gidx)
    ing, raw = [], []
    for kk in range(EXPERTS_PER_GROUP):
        vs, vr = sel[kk], s[kk]
        for g in range(1, N_GROUPS):
            pick = gidx == g
            vs = jnp.where(pick, sel[4 * g + kk], vs)
            vr = jnp.where(pick, s[4 * g + kk], vr)
        ing.append(vs)
        raw.append(vr)
    b1 = ing[0]
    i1 = jnp.zeros_like(gidx)
    for kk in range(1, EXPERTS_PER_GROUP):
        take = ing[kk] > b1
        b1 = jnp.where(take, ing[kk], b1)
        i1 = jnp.where(take, kk, i1)
    neg = jnp.full_like(b1, -jnp.inf)
    b2 = neg
    i2 = jnp.zeros_like(gidx)
    for kk in range(EXPERTS_PER_GROUP):
        cand = jnp.where(i1 == kk, neg, ing[kk])
        take = cand > b2
        b2 = jnp.where(take, cand, b2)
        i2 = jnp.where(take, kk, i2)
    w1 = raw[0]
    w2 = raw[0]
    for kk in range(1, EXPERTS_PER_GROUP):
        w1 = jnp.where(i1 == kk, raw[kk], w1)
        w2 = jnp.where(i2 == kk, raw[kk], w2)
    tot = w1 + w2
    w1 = w1 / tot
    w2 = w2 / tot
    first_lower = i1 < i2
    lo = jnp.where(first_lower, i1, i2)
    hi = jnp.where(first_lower, i2, i1)
    pidx = jnp.where(lo == 0, hi - 1, jnp.where(lo == 1, hi + 1, 5))
    bkt = gidx * 6 + pidx
    return bkt, jnp.where(first_lower, w1, w2), jnp.where(first_lower, w2, w1)


def _moe_route_kernel(hx_ref, wr_ref, rb_ref, bkt_ref, rank_ref, wab_ref, cnt_ref, carry_ref, *, tm):
    @pl.when(pl.program_id(0) == 0)
    def _():
        carry_ref[...] = jnp.zeros_like(carry_ref)

    logits_t = _mm_nt_f32(wr_ref[...], _from_token_major(hx_ref, tm))
    bkt, wa, wb = _route_pairs(_sigmoid(logits_t), rb_ref)
    sub = lax.broadcasted_iota(jnp.int32, (32, tm), 0)
    oh = jnp.where(sub == bkt, 1.0, 0.0)
    r = lax.broadcasted_iota(jnp.int32, (tm, tm), 0)
    c = lax.broadcasted_iota(jnp.int32, (tm, tm), 1)
    earlier = jnp.where(r < c, 1.0, 0.0).astype(BF16)
    before = jnp.dot(oh.astype(BF16), earlier, preferred_element_type=F32)
    carry = carry_ref[...]
    rank = jnp.sum(oh * (before + carry[:, 0:1]), axis=0, keepdims=True)
    carry = carry + jnp.sum(oh, axis=1, keepdims=True)
    carry_ref[...] = carry
    cnt_ref[...] = carry
    bkt_ref[...] = bkt
    rank_ref[...] = rank.astype(jnp.int32)
    pad = jnp.zeros((LANE - 2, tm), F32)
    wab_ref[...] = jnp.transpose(jnp.concatenate([wa, wb, pad], axis=0))


def _moe_tables_kernel(cnt_ref, bkt_ref, rank_ref, pos_ref, tabs_ref, *, t, tr):
    cnt = cnt_ref[...]
    sz = jnp.floor((cnt + (tr - 1)) * (1.0 / tr)) * tr
    r = lax.broadcasted_iota(jnp.int32, (32, 32), 0)
    c = lax.broadcasted_iota(jnp.int32, (32, 32), 1)
    start = jnp.dot(jnp.where(c < r, 1.0, 0.0), sz, preferred_element_type=F32, precision=HIGHEST)
    end = start + sz
    sub = lax.broadcasted_iota(jnp.int32, (32, t), 0)
    pos = jnp.sum(jnp.where(sub == bkt_ref[...], start[:, 0:1], 0.0), axis=0, keepdims=True)
    pos_ref[...] = pos.astype(jnp.int32) + rank_ref[...]
    brow = lax.broadcasted_iota(jnp.int32, (32, LANE), 0)
    tile0 = lax.broadcasted_iota(jnp.int32, (32, LANE), 1).astype(F32) * tr
    tbk = jnp.sum(jnp.where((brow < MOE_NBKT) & (end <= tile0), 1, 0), axis=0, keepdims=True)
    tbk = jnp.minimum(tbk, MOE_NBKT - 1)
    total = end[MOE_NBKT - 1:MOE_NBKT, :]
    valid = jnp.where(tile0[0:1, :] < total, 1, 0)
    g = jnp.where(tbk >= 6, 1, 0) + jnp.where(tbk >= 12, 1, 0) + jnp.where(tbk >= 18, 1, 0)
    p = tbk - 6 * g
    ge3 = jnp.where(p >= 3, 1, 0)
    ge5 = jnp.where(p >= 5, 1, 0)
    ea = 4 * g + ge3 + ge5
    eb = 4 * g + p + 1 - 2 * ge3 - ge5
    zero = jnp.zeros((SUBLANE - 3, LANE), jnp.int32)
    tabs_ref[...] = jnp.concatenate([ea, eb, valid, zero], axis=0)


def _moe_group_kernel(ea_ref, eb_ref, vd_ref, xs_ref, ws_ref, wga_ref, wua_ref, wda_ref,
                      wgb_ref, wub_ref, wdb_ref, ys_ref, wg_s, wu_s, wd_s, *, tr):
    j = pl.program_id(0)
    prev = jnp.maximum(j - 1, 0)

    @pl.when((j == 0) | (ea_ref[j] != ea_ref[prev]))
    def _():
        wg_s[0] = wga_ref[...].astype(BF16)
        wu_s[0] = wua_ref[...].astype(BF16)
        wd_s[0] = wda_ref[...].astype(BF16)

    @pl.when((j == 0) | (eb_ref[j] != eb_ref[prev]))
    def _():
        wg_s[1] = wgb_ref[...].astype(BF16)
        wu_s[1] = wub_ref[...].astype(BF16)
        wd_s[1] = wdb_ref[...].astype(BF16)

    @pl.when(vd_ref[j] > 0)
    def _():
        x = _from_token_major(xs_ref, tr).astype(BF16)
        w = ws_ref[...]

        def ffn(s, cw):
            hg = jnp.dot(x, wg_s[s], preferred_element_type=F32)
            hu = jnp.dot(x, wu_s[s], preferred_element_type=F32)
            hid = _silu(hg) * hu * cw
            return jnp.dot(hid.astype(BF16), wd_s[s], preferred_element_type=F32)

        y = ffn(0, w[:, 0:1]) + ffn(1, w[:, 1:2])
        _to_token_major(ys_ref, y, tr)


def _moe_ln_kernel(hx_ref, y_ref, g_ref, b_ref, o_ref, ob_ref, *, tm):
    out = _layer_norm(ALPHA * _from_token_major(hx_ref, tm) + _from_token_major(y_ref, tm),
                      g_ref[...], b_ref[...])
    o_ref[...] = out
    ob_ref[...] = out.astype(BF16)


def _sc_mesh_info():
    info = plsc.get_sparse_core_info()
    mesh = plsc.VectorSubcoreMesh(core_axis_name="c", subcore_axis_name="s")
    return mesh, info.num_cores, info.num_subcores


def _sc_scatter_rows(x3, w2, pos, n_rows):
    t = x3.shape[0]
    mesh, nc, ns = _sc_mesh_info()
    per_w = t // (nc * ns)

    @functools.partial(
        pl.kernel, mesh=mesh,
        out_type=[jax.ShapeDtypeStruct((n_rows,) + x3.shape[1:], x3.dtype),
                  jax.ShapeDtypeStruct((n_rows,) + w2.shape[1:], w2.dtype)],
        scratch_types=[pltpu.VMEM((SC_CH,), jnp.int32), pltpu.VMEM((SC_CH,) + x3.shape[1:], x3.dtype),
                       pltpu.VMEM((SC_CH,) + w2.shape[1:], w2.dtype),
                       pltpu.SemaphoreType.DMA, pltpu.SemaphoreType.DMA])
    def k(x_hbm, w_hbm, idx_hbm, ox_hbm, ow_hbm, idx_v, rows_v, wrows_v, sem_x, sem_w):
        base = (lax.axis_index("s") * nc + lax.axis_index("c")) * per_w

        @pl.loop(0, per_w // SC_CH)
        def _(j):
            off = base + j * SC_CH
            pltpu.sync_copy(idx_hbm.at[pl.ds(off, SC_CH)], idx_v)
            pltpu.sync_copy(x_hbm.at[pl.ds(off, SC_CH)], rows_v)
            pltpu.sync_copy(w_hbm.at[pl.ds(off, SC_CH)], wrows_v)
            cx = pltpu.async_copy(rows_v, ox_hbm.at[idx_v], sem_x)
            cw = pltpu.async_copy(wrows_v, ow_hbm.at[idx_v], sem_w)
            cx.wait()
            cw.wait()

    return k(x3, w2, pos)


def _sc_gather_rows(y3, pos):
    t = pos.shape[0]
    mesh, nc, ns = _sc_mesh_info()
    per_w = t // (nc * ns)

    @functools.partial(
        pl.kernel, mesh=mesh,
        out_type=jax.ShapeDtypeStruct((t,) + y3.shape[1:], y3.dtype),
        scratch_types=[pltpu.VMEM((SC_CH,), jnp.int32), pltpu.VMEM((SC_CH,) + y3.shape[1:], y3.dtype),
                       pltpu.SemaphoreType.DMA])
    def k(y_hbm, idx_hbm, o_hbm, idx_v, rows_v, sem):
        base = (lax.axis_index("s") * nc + lax.axis_index("c")) * per_w

        @pl.loop(0, per_w // SC_CH)
        def _(j):
            off = base + j * SC_CH
            pltpu.sync_copy(idx_hbm.at[pl.ds(off, SC_CH)], idx_v)
            pltpu.async_copy(y_hbm.at[idx_v], rows_v, sem).wait()
            pltpu.sync_copy(rows_v, o_hbm.at[pl.ds(off, SC_CH)])

    return k(y3, pos)


def _moe_sparse(hx, wr_t, rbias, wg, wu, wd, g, b, layer):
    t = hx.shape[0] // NPIECE
    d = D_MODEL
    tm = min(TM_MOE, t)
    tr = MOE_TR
    nt = t // tr + MOE_NBKT
    n_rows = nt * tr
    row1 = pl.BlockSpec((1, tm), lambda i: (0, i))
    rows_tm = pl.BlockSpec((tm * NPIECE, LANE), lambda i: (i, 0))
    bkt, rank, wab, cnt = pl.pallas_call(
        functools.partial(_moe_route_kernel, tm=tm),
        grid=(t // tm,),
        in_specs=[rows_tm,
                  pl.BlockSpec((N_EXPERTS, d), lambda i: (0, 0)),
                  pl.BlockSpec((N_EXPERTS, LANE), lambda i: (0, 0))],
        out_specs=[row1, row1, pl.BlockSpec((tm, LANE), lambda i: (i, 0)),
                   pl.BlockSpec((32, LANE), lambda i: (0, 0))],
        out_shape=[jax.ShapeDtypeStruct((1, t), jnp.int32), jax.ShapeDtypeStruct((1, t), jnp.int32),
                   jax.ShapeDtypeStruct((t, LANE), F32), jax.ShapeDtypeStruct((32, LANE), F32)],
        scratch_shapes=[pltpu.VMEM((32, LANE), F32)],
        compiler_params=_cparams(("arbitrary",)),
        name="moe_route",
    )(hx, wr_t, rbias)
    pos, tabs = pl.pallas_call(
        functools.partial(_moe_tables_kernel, t=t, tr=tr),
        out_shape=[jax.ShapeDtypeStruct((1, t), jnp.int32), jax.ShapeDtypeStruct((SUBLANE, LANE), jnp.int32)],
        compiler_params=pltpu.CompilerParams(vmem_limit_bytes=VMEM_LIMIT),
        name="moe_tables",
    )(cnt, bkt, rank)
    pos = pos.reshape(t)
    xs3, ws = _sc_scatter_rows(hx.reshape(t, NPIECE, LANE), wab, pos, n_rows)

    def wspec(which, shape):
        if which == 0:
            return pl.BlockSpec((None, None) + shape, lambda j, ea, eb, vd: (layer, ea[j], 0, 0))
        return pl.BlockSpec((None, None) + shape, lambda j, ea, eb, vd: (layer, eb[j], 0, 0))

    ys = pl.pallas_call(
        functools.partial(_moe_group_kernel, tr=tr),
        grid_spec=pltpu.PrefetchScalarGridSpec(
            num_scalar_prefetch=3,
            grid=(nt,),
            in_specs=[pl.BlockSpec((tr * NPIECE, LANE), lambda j, ea, eb, vd: (j, 0)),
                      pl.BlockSpec((tr, LANE), lambda j, ea, eb, vd: (j, 0)),
                      wspec(0, (d, D_FF)), wspec(0, (d, D_FF)), wspec(0, (D_FF, d)),
                      wspec(1, (d, D_FF)), wspec(1, (d, D_FF)), wspec(1, (D_FF, d))],
            out_specs=pl.BlockSpec((tr * NPIECE, LANE), lambda j, ea, eb, vd: (j, 0)),
            scratch_shapes=[pltpu.VMEM((2, d, D_FF), BF16), pltpu.VMEM((2, d, D_FF), BF16),
                            pltpu.VMEM((2, D_FF, d), BF16)]),
        out_shape=jax.ShapeDtypeStruct((n_rows * NPIECE, LANE), F32),
        compiler_params=_cparams(("arbitrary",)),
        name="moe_experts",
    )(tabs[0, :nt], tabs[1, :nt], tabs[2, :nt], xs3.reshape(n_rows * NPIECE, LANE), ws,
      wg, wu, wd, wg, wu, wd)
    y3 = _sc_gather_rows(ys.reshape(n_rows, NPIECE, LANE), pos)
    vec = pl.BlockSpec((None, 1, d), lambda i: (layer, 0, 0))
    row = pl.BlockSpec((tm, d), lambda i: (i, 0))
    return pl.pallas_call(
        functools.partial(_moe_ln_kernel, tm=tm),
        grid=(t // tm,),
        in_specs=[rows_tm, rows_tm, vec, vec],
        out_specs=[row, row],
        out_shape=[jax.ShapeDtypeStruct((t, d), F32), jax.ShapeDtypeStruct((t, d), BF16)],
        compiler_params=_cparams(("parallel",)),
        name="moe_ln",
    )(hx, y3.reshape(t * NPIECE, LANE), g, b)


def _pack_w_in(w_in):
    wt = jnp.swapaxes(w_in, 1, 2)
    (a_q, a_k, a_v, a_beta, a_dt, a_g, b_q, b_k, b_v, b_lr, b_g,
     c_q, c_f, c_i, c_g, m_a, m_b, m_c) = jnp.split(wt, SPLIT_POINTS, axis=1)
    depth, _, d = wt.shape
    a_s = jnp.concatenate([a_beta, a_dt, jnp.zeros((depth, LANE - 2 * GDN_HEADS, d), wt.dtype)], 1)
    b_lrp = jnp.concatenate([b_lr, jnp.zeros((depth, LANE - GLA_RANK, d), wt.dtype)], 1)
    wb = jnp.concatenate([m_a, m_b, m_c, a_q, a_k, a_v, a_g, b_q, b_k, b_v, b_g, c_q, c_i, c_g], 1).astype(BF16)
    wf = jnp.concatenate([c_f, a_s, b_lrp], 1).astype(BF16)
    assert wb.shape[1] == NPB and wf.shape[1] == NPF
    return wb, wf


def _prepare(w_in, gdn_conv, gdn_a_log, gdn_dt_bias, gdn_norm, gla_w2, gla_b2, gla_norm, hgrn_lb_logits,
             hgrn_norm, w_br_a, w_br_b, w_br_c, w_out, ln1_g, ln1_b, w_router, router_bias, w_gate, w_up,
             w_down, ln2_g, ln2_b):
    depth = w_in.shape[0]
    d = w_out.shape[-1]
    w_pb, w_pf = _pack_w_in(w_in)
    return dict(
        w_pb=w_pb, w_pf=w_pf,
        gdn_conv=gdn_conv,
        gdn_par=jnp.pad(jnp.stack([gdn_a_log, gdn_dt_bias], axis=1),
                        ((0, 0), (0, SUBLANE - 2), (GDN_HEADS, LANE - 2 * GDN_HEADS))),
        gdn_norm=gdn_norm.reshape(depth, 1, GDN_DV),
        w2p=jnp.concatenate([gla_w2, jnp.zeros((depth, LANE - GLA_RANK, GLA_QK), gla_w2.dtype)], axis=1),
        gla_b2=gla_b2.reshape(depth, 1, GLA_QK),
        gla_norm=gla_norm.reshape(depth, 1, GLA_DV),
        lb_logits=hgrn_lb_logits,
        hgrn_norm=hgrn_norm.reshape(depth, 1, HGRN_DV),
        wa=w_br_a.astype(BF16), wb=w_br_b.astype(BF16), wc=w_br_c.astype(BF16), wo=w_out.astype(BF16),
        ln1_g=ln1_g.reshape(depth, 1, d), ln1_b=ln1_b.reshape(depth, 1, d),
        wr_t=jnp.transpose(w_router),
        rbias=jnp.broadcast_to(router_bias[:, None], (N_EXPERTS, LANE)),
        wg=w_gate, wu=w_up, wd=w_down,
        ln2_g=ln2_g.reshape(depth, 1, d), ln2_b=ln2_b.reshape(depth, 1, d),
    )


def _mixer_block(h, hb, p, layer, batch, seq):
    pb = _inproj_call(hb, p["w_pb"], layer, BF16, TM_PROJ, TN_PROJ, "inproj_b")
    pf = _inproj_call(hb, p["w_pf"], layer, F32, TM_PROJ // 2, NPF, "inproj_f")
    o_a = _gdn_call(pb, pf, p["gdn_conv"], p["gdn_par"], p["gdn_norm"], layer, batch, seq)
    o_b = _gla_call(pb, pf, p["w2p"], p["gla_b2"], p["gla_norm"], layer, batch, seq)
    o_c = _hgrn_call(pb, pf, p["lb_logits"], p["hgrn_norm"], layer, batch, seq)
    return _merge_call(o_a, o_b, o_c, pb, h, p["wa"], p["wb"], p["wc"], p["wo"], p["ln1_g"], p["ln1_b"], layer)


def _ffn_block(hx, p, layer):
    return _moe_sparse(hx, p["wr_t"], p["rbias"], p["wg"], p["wu"], p["wd"], p["ln2_g"], p["ln2_b"], layer)


def kernel(x, ln0_g, ln0_b, w_in, gdn_conv, gdn_a_log, gdn_dt_bias, gdn_norm, gla_w2, gla_b2, gla_norm,
           hgrn_lb_logits, hgrn_norm, w_br_a, w_br_b, w_br_c, w_out, ln1_g, ln1_b, w_router, router_bias,
           w_gate, w_up, w_down, ln2_g, ln2_b):
    batch, seq, d = x.shape
    p = _prepare(w_in, gdn_conv, gdn_a_log, gdn_dt_bias, gdn_norm, gla_w2, gla_b2, gla_norm, hgrn_lb_logits,
                 hgrn_norm, w_br_a, w_br_b, w_br_c, w_out, ln1_g, ln1_b, w_router, router_bias, w_gate, w_up,
                 w_down, ln2_g, ln2_b)
    h, hb = _ln_call(x.reshape(batch * seq, d), ln0_g, ln0_b)
    for layer in range(w_in.shape[0]):
        hx = _mixer_block(h, hb, p, layer, batch, seq)
        h, hb = _ffn_block(hx, p, layer)
    return h.reshape(batch, seq, d)
```

```python
import functools

import numpy as np
import jax
import jax.numpy as jnp
from jax import lax
from jax.experimental import pallas as pl
from jax.experimental.pallas import tpu as pltpu
from jax.experimental.pallas import tpu_sc as plsc

F32 = jnp.float32
BF16 = jnp.bfloat16
HIGHEST = lax.Precision.HIGHEST

D_MODEL = 1024
DEPTH = 4
CHUNK = 64
GDN_HEADS, GDN_DK, GDN_DV, CONV_W = 4, 128, 128, 4
GLA_HEADS, GLA_DK, GLA_DV, GLA_RANK, GLA_NORMALIZER = 4, 64, 128, 16, 16.0
HGRN_HEADS, HGRN_EXPAND, HGRN_DV = 4, 128, 128
LB_FLOOR = 1e-30
N_EXPERTS, N_GROUPS, TOP_K, D_FF = 16, 4, 2, 256
EXPERTS_PER_GROUP = N_EXPERTS // N_GROUPS
ALPHA = (2.0 * DEPTH) ** 0.25
LN_EPS = 1e-5
RMS_EPS = 1e-6

GDN_QK = GDN_HEADS * GDN_DK
GDN_V = GDN_HEADS * GDN_DV
GLA_QK = GLA_HEADS * GLA_DK
GLA_V = GLA_HEADS * GLA_DV
HGRN_QK = HGRN_HEADS * HGRN_EXPAND
HGRN_V = HGRN_HEADS * HGRN_DV
SPLIT_SIZES = (GDN_QK, GDN_QK, GDN_V, GDN_HEADS, GDN_HEADS, GDN_V,
               GLA_QK, GLA_QK, GLA_V, GLA_RANK, GLA_V,
               HGRN_QK, HGRN_QK, HGRN_V, HGRN_V,
               D_MODEL, D_MODEL, D_MODEL)
SPLIT_POINTS = tuple(int(v) for v in np.cumsum(SPLIT_SIZES)[:-1])

LANE = 128
SUBLANE = 8
VMEM_LIMIT = 48 * 1024 * 1024

M_A, M_B, M_C = 0, 1024, 2048
A_Q, A_K, A_V, A_G = 3072, 3584, 4096, 4608
B_Q, B_K, B_V, B_G = 5120, 5376, 5632, 6144
C_Q, C_I, C_G = 6656, 7168, 7680
NPB = 8192
F_CF, F_AS, F_LR = 0, 512, 640
NPF = 768

TB_MIX = 1024
TM_ROWS = 512
TM_MOE = 1024
TM_PROJ, TN_PROJ = 2048, 2048
SUB = 8
DD_NPAR = 4
DD_SAFE_EXP = 60.0
GDN_NPAR = 4
MOE_TR = 512
MOE_NBKT = N_GROUPS * 6
NPIECE = D_MODEL // LANE
SC_CH = 64


def _cparams(sem):
    return pltpu.CompilerParams(dimension_semantics=sem, vmem_limit_bytes=VMEM_LIMIT)


def _mm(a, b):
    return jnp.dot(a.astype(BF16), b.astype(BF16), preferred_element_type=F32)


def _mm_nt(a, b):
    return lax.dot_general(a.astype(BF16), b.astype(BF16), (((1,), (1,)), ((), ())),
                           preferred_element_type=F32)


def _mm_tn(a, b):
    return lax.dot_general(a.astype(BF16), b.astype(BF16), (((0,), (0,)), ((), ())),
                           preferred_element_type=F32)


def _mm_nt_f32(a, b):
    return lax.dot_general(a, b, (((1,), (1,)), ((), ())), preferred_element_type=F32,
                           precision=HIGHEST)


def _split3(x):
    hi = x.astype(BF16)
    r = x - hi.astype(F32)
    mid = r.astype(BF16)
    lo = (r - mid.astype(F32)).astype(BF16)
    return hi, mid, lo


def _mm_01(m01, x):
    hi, mid, lo = _split3(x)
    return (jnp.dot(m01, hi, preferred_element_type=F32) + jnp.dot(m01, mid, preferred_element_type=F32)
            + jnp.dot(m01, lo, preferred_element_type=F32))


def _sigmoid(x):
    return 1.0 / (1.0 + jnp.exp(-x))


def _sigmoid_t(x):
    return 0.5 * jnp.tanh(0.5 * x) + 0.5


def _silu(x):
    return x * _sigmoid_t(x)


def _softplus(x):
    return jnp.maximum(x, 0.0) + jnp.log(1.0 + jnp.exp(-jnp.abs(x)))


def _log_sigmoid(x):
    return -_softplus(-x)


def _layer_norm(x, g, b):
    mu = jnp.mean(x, axis=-1, keepdims=True)
    xc = x - mu
    var = jnp.mean(xc * xc, axis=-1, keepdims=True)
    return xc * lax.rsqrt(var + LN_EPS) * g + b


def _to_token_major(ref, x, rows):
    for j in range(NPIECE):
        ref[pl.ds(j, rows, stride=NPIECE), :] = x[:, j * LANE:(j + 1) * LANE]


def _from_token_major(ref, rows):
    return jnp.concatenate([ref[pl.ds(j, rows, stride=NPIECE), :] for j in range(NPIECE)], axis=1)


def _ln_kernel(x_ref, g_ref, b_ref, o_ref, ob_ref):
    y = _layer_norm(x_ref[...], g_ref[...], b_ref[...])
    o_ref[...] = y
    ob_ref[...] = y.astype(BF16)


def _ln_call(x, g, b):
    t, d = x.shape
    tm = min(TM_ROWS, t)
    return pl.pallas_call(
        _ln_kernel,
        grid=(t // tm,),
        in_specs=[pl.BlockSpec((tm, d), lambda i: (i, 0)),
                  pl.BlockSpec((1, d), lambda i: (0, 0)),
                  pl.BlockSpec((1, d), lambda i: (0, 0))],
        out_specs=[pl.BlockSpec((tm, d), lambda i: (i, 0)), pl.BlockSpec((tm, d), lambda i: (i, 0))],
        out_shape=[jax.ShapeDtypeStruct((t, d), F32), jax.ShapeDtypeStruct((t, d), BF16)],
        compiler_params=_cparams(("parallel",)),
        name="ln0",
    )(x, g.reshape(1, d), b.reshape(1, d))


def _inproj_kernel(x_ref, wt_ref, o_ref):
    o_ref[...] = lax.dot_general(x_ref[...], wt_ref[...], (((1,), (1,)), ((), ())),
                                 preferred_element_type=F32).astype(o_ref.dtype)


def _inproj_call(hb, w, layer, out_dtype, tm, tn, name):
    t, d = hb.shape
    tm = min(tm, t)
    n = w.shape[-2]
    return pl.pallas_call(
        _inproj_kernel,
        grid=(n // tn, t // tm),
        in_specs=[pl.BlockSpec((tm, d), lambda j, i: (i, 0)),
                  pl.BlockSpec((None, tn, d), lambda j, i: (layer, j, 0))],
        out_specs=pl.BlockSpec((tm, tn), lambda j, i: (i, j)),
        out_shape=jax.ShapeDtypeStruct((t, n), out_dtype),
        compiler_params=_cparams(("parallel", "parallel")),
        name=name,
    )(hb, w)


def _schedule(units, prologue, trip, epilogue):
    prologue(*units[0])
    for n, u in enumerate(units):
        if n + 1 < len(units):
            prologue(*units[n + 1])
        trip(*u, n % 2)
        epilogue(*u)


def _gdn_kernel(qkv_ref, s_ref, gate_ref, cw_ref, par_ref, nw_ref, o_ref,
                state_ref, tail_ref, xbuf_ref, q_s, k_s, v_s, cumb_s, betab_s, cumrow_s, o_s, *, tb):
    nc = tb // CHUNK
    nh = GDN_HEADS
    off0 = SUBLANE - (CONV_W - 1)

    @pl.when(pl.program_id(1) == 0)
    def _():
        state_ref[...] = jnp.zeros_like(state_ref)
        tail_ref[...] = jnp.zeros_like(tail_ref)

    row = lax.broadcasted_iota(jnp.int32, (CHUNK, CHUNK), 0)
    col = lax.broadcasted_iota(jnp.int32, (CHUNK, CHUNK), 1)
    incl = col <= row
    strict = col < row
    tri = jnp.where(incl, 1.0, 0.0).astype(BF16)
    eye = jnp.where(col == row, 1.0, 0.0).astype(F32)
    npar = GDN_NPAR
    rp = npar * CHUNK
    xbuf_ref[0:SUBLANE, :] = tail_ref[...]

    def prologue(t):
        lo = t * rp
        xbuf_ref[SUBLANE + lo:SUBLANE + lo + rp, :] = qkv_ref[lo:lo + rp, :].astype(F32)
        for j in range(3 * nh):
            cs = slice(j * LANE, (j + 1) * LANE)
            y = xbuf_ref[off0 + lo:off0 + lo + rp, cs] * cw_ref[0:1, cs]
            for kk in range(1, CONV_W):
                y = y + xbuf_ref[off0 + kk + lo:off0 + kk + lo + rp, cs] * cw_ref[kk:kk + 1, cs]
            y = _silu(y)
            if j < nh:
                q_s[j, lo:lo + rp, :] = (y * lax.rsqrt(jnp.sum(y * y, axis=-1, keepdims=True) + RMS_EPS)
                                         * (GDN_DK ** -0.5))
            elif j < 2 * nh:
                k_s[j - nh, lo:lo + rp, :] = y * lax.rsqrt(jnp.sum(y * y, axis=-1, keepdims=True) + RMS_EPS)
            else:
                v_s[j - 2 * nh, lo:lo + rp, :] = y
        sc = s_ref[lo:lo + rp, :]
        beta_all = _sigmoid_t(sc)
        g_all = -jnp.exp(par_ref[0:1, :]) * _softplus(sc + par_ref[1:2, :])
        cum_all = jnp.concatenate([_mm_01(tri, g_all[c * CHUNK:(c + 1) * CHUNK, :]) for c in range(npar)], axis=0)
        cum_t = jnp.transpose(cum_all)
        for c in range(npar):
            cumrow_s[t * npar + c] = cum_t[0:SUBLANE, c * CHUNK:(c + 1) * CHUNK]
        for h in range(nh):
            cumb_s[h, lo:lo + rp, :] = jnp.broadcast_to(cum_all[:, nh + h:nh + h + 1], (rp, LANE))
            betab_s[h, lo:lo + rp, :] = jnp.broadcast_to(beta_all[:, h:h + 1], (rp, LANE))

    def trip(cp, _slot):
        chains = [(cp * npar + cc, h) for cc in range(npar) for h in range(nh)]
        r0s = [c * CHUNK for c, _ in chains]
        qc = [q_s[h, r0:r0 + CHUNK, :] for (_, h), r0 in zip(chains, r0s)]
        kc = [k_s[h, r0:r0 + CHUNK, :] for (_, h), r0 in zip(chains, r0s)]
        vc = [v_s[h, r0:r0 + CHUNK, :] for (_, h), r0 in zip(chains, r0s)]
        cumc = [cumb_s[h, r0:r0 + CHUNK, :] for (_, h), r0 in zip(chains, r0s)]
        bc = [betab_s[h, r0:r0 + CHUNK, :] for (_, h), r0 in zip(chains, r0s)]
        n = len(chains)
        kk = [_mm_nt(kc[i], kc[i]) for i in range(n)]
        qk = [_mm_nt(qc[i], kc[i]) for i in range(n)]
        decay = []
        for i, (c, h) in enumerate(chains):
            diff = cumc[i][:, 0:CHUNK] - cumrow_s[c][nh + h:nh + h + 1, :]
            decay.append(jnp.where(incl, jnp.exp(jnp.where(incl, diff, 0.0)), 0.0))
        a = [jnp.where(strict, bc[i][:, 0:CHUNK] * kk[i] * decay[i], 0.0) for i in range(n)]
        x = [eye - a[i] for i in range(n)]
        p = [_mm(a[i], a[i]) for i in range(n)]
        for it in range(5):
            x = [x[i] + _mm(x[i], p[i]) for i in range(n)]
            if it < 4:
                p = [_mm(p[i], p[i]) for i in range(n)]
        ecum = [jnp.exp(cumc[i]) for i in range(n)]
        sol = [_mm(x[i], jnp.concatenate([vc[i] * bc[i], kc[i] * (bc[i] * ecum[i])], axis=1)) for i in range(n)]
        attn = [qk[i] * decay[i] for i in range(n)]
        cum_last = [cumc[i][CHUNK - 1:CHUNK, :] for i in range(n)]
        k_state = [kc[i] * jnp.exp(cum_last[i] - cumc[i]) for i in range(n)]
        wqi = [jnp.concatenate([sol[i][:, GDN_DV:GDN_DV + GDN_DK], qc[i] * ecum[i]], axis=0) for i in range(n)]
        for cc in range(npar):
            idx = [cc * nh + h for h in range(nh)]
            s = [state_ref[h] for h in range(nh)]
            wq = [_mm(wqi[i], s[h]) for h, i in enumerate(idx)]
            v_new = [sol[i][:, 0:GDN_DV] - wq[h][0:CHUNK, :] for h, i in enumerate(idx)]
            av = [_mm(attn[i], v_new[h]) for h, i in enumerate(idx)]
            upd = [_mm_tn(k_state[i], v_new[h]) for h, i in enumerate(idx)]
            for h, i in enumerate(idx):
                state_ref[h] = s[h] * jnp.exp(cum_last[i]) + upd[h]
                o_s[r0s[i]:r0s[i] + CHUNK, h * LANE:(h + 1) * LANE] = wq[h][CHUNK:2 * CHUNK, :] + av[h]

    def epilogue(t):
        lo = t * rp
        for h in range(nh):
            hs = slice(h * LANE, (h + 1) * LANE)
            o = o_s[lo:lo + rp, hs]
            oh = o * lax.rsqrt(jnp.mean(o * o, axis=-1, keepdims=True) + RMS_EPS) * nw_ref[...]
            o_ref[lo:lo + rp, hs] = (oh * _silu(gate_ref[lo:lo + rp, hs].astype(F32))).astype(o_ref.dtype)

    _schedule([(t,) for t in range(nc // npar)], prologue, trip, epilogue)
    tail_ref[...] = xbuf_ref[tb:tb + SUBLANE, :]


def _gdn_call(pb, pf, conv_w, par, norm_w, layer, batch, seq):
    tb = min(TB_MIX, seq)
    nb = seq // tb
    nc = tb // CHUNK
    t = batch * seq
    wq = 2 * GDN_QK + GDN_V
    kern = functools.partial(_gdn_kernel, tb=tb)
    return pl.pallas_call(
        kern,
        grid=(batch, nb),
        in_specs=[pl.BlockSpec((tb, wq), lambda b, i: (b * nb + i, A_Q // wq)),
                  pl.BlockSpec((tb, LANE), lambda b, i: (b * nb + i, F_AS // LANE)),
                  pl.BlockSpec((tb, GDN_V), lambda b, i: (b * nb + i, A_G // GDN_V)),
                  pl.BlockSpec((None, CONV_W, wq), lambda b, i: (layer, 0, 0)),
                  pl.BlockSpec((None, SUBLANE, LANE), lambda b, i: (layer, 0, 0)),
                  pl.BlockSpec((None, 1, GDN_DV), lambda b, i: (layer, 0, 0))],
        out_specs=pl.BlockSpec((tb, GDN_V), lambda b, i: (b * nb + i, 0)),
        out_shape=jax.ShapeDtypeStruct((t, GDN_V), BF16),
        scratch_shapes=[pltpu.VMEM((GDN_HEADS, GDN_DK, GDN_DV), F32),
                        pltpu.VMEM((SUBLANE, wq), F32),
                        pltpu.VMEM((tb + SUBLANE, wq), F32),
                        pltpu.VMEM((GDN_HEADS, tb, LANE), F32),
                        pltpu.VMEM((GDN_HEADS, tb, LANE), F32),
                        pltpu.VMEM((GDN_HEADS, tb, LANE), F32),
                        pltpu.VMEM((GDN_HEADS, tb, LANE), F32),
                        pltpu.VMEM((GDN_HEADS, tb, LANE), F32),
                        pltpu.VMEM((nc, SUBLANE, CHUNK), F32),
                        pltpu.VMEM((tb, GDN_V), F32)],
        compiler_params=_cparams(("parallel", "arbitrary")),
        name="gdn",
    )(pb, pf, pb, conv_w, par, norm_w)


def _dd_make_trip(q_s, k_s, v_s, la_s, o_s, state_ref, c8_s, p_s, g_heads):
    nblk = CHUNK // SUB
    dkh = LANE // g_heads
    dvp = g_heads * LANE
    npar = DD_NPAR
    row = lax.broadcasted_iota(jnp.int32, (CHUNK, CHUNK), 0)
    col = lax.broadcasted_iota(jnp.int32, (CHUNK, CHUNK), 1)
    level_masks = []
    for sh in (5, 4, 3):
        same2b = jnp.right_shift(row, sh + 1) == jnp.right_shift(col, sh + 1)
        upper = (jnp.right_shift(row, sh) & 1) == 1
        lower = (jnp.right_shift(col, sh) & 1) == 0
        level_masks.append(jnp.where(same2b, jnp.where(upper, jnp.where(lower, 1.0, 0.0), 0.0), 0.0))
    tri8 = jnp.where(jnp.right_shift(row, 3) == jnp.right_shift(col, 3),
                     jnp.where(col <= row, 1.0, 0.0), 0.0).astype(BF16)
    diag_mask = tri8.astype(F32)
    lane128 = lax.broadcasted_iota(jnp.int32, (CHUNK, LANE), 1)
    head_masks = [jnp.where((lane128 >= g * dkh) & (lane128 < (g + 1) * dkh), 1.0, 0.0)
                  for g in range(g_heads)]
    sub = lax.broadcasted_iota(jnp.int32, (SUB, LANE), 0)
    dk_sh = dkh.bit_length() - 1
    lane_sh = LANE.bit_length() - 1
    orow = lax.broadcasted_iota(jnp.int32, (LANE, dvp), 0)
    ocol = lax.broadcasted_iota(jnp.int32, (LANE, dvp), 1)
    ones_bd = jnp.where(jnp.right_shift(orow, dk_sh) == jnp.right_shift(ocol, lane_sh), 1.0, 0.0).astype(BF16)
    srow = lax.broadcasted_iota(jnp.int32, (dvp, LANE), 0)
    scol = lax.broadcasted_iota(jnp.int32, (dvp, LANE), 1)
    state_mask = jnp.where(jnp.right_shift(srow, lane_sh) == jnp.right_shift(scol, dk_sh), 1.0, 0.0)

    def block_sums(u, r0):
        c8 = [c8_s[u, r0 + b * SUB:r0 + (b + 1) * SUB, :] for b in range(nblk)]
        t8 = [c8_s[u, r0 + (b + 1) * SUB - 1:r0 + (b + 1) * SUB, :] for b in range(nblk)]
        t16 = [t8[2 * b] + t8[2 * b + 1] for b in range(nblk // 2)]
        t32 = [t16[2 * b] + t16[2 * b + 1] for b in range(nblk // 4)]
        t64 = t32[0] + t32[1]
        c16 = [c8[b] + t8[b - 1] if b % 2 else c8[b] for b in range(nblk)]
        c32 = [c16[b] + t16[b // 2 - 1] if (b // 2) % 2 else c16[b] for b in range(nblk)]
        c64 = [c32[b] + t32[0] if b >= nblk // 2 else c32[b] for b in range(nblk)]
        pre = {8: c8, 16: c16, 32: c32, 64: c64}
        suf = {8: [t8[b] - c8[b] for b in range(nblk)],
               16: [t16[b // 2] - c16[b] for b in range(nblk)],
               32: [t32[b // 4] - c32[b] for b in range(nblk)],
               64: [t64 - c64[b] for b in range(nblk)]}
        return pre, suf

    def cat(pieces):
        return jnp.concatenate(pieces, axis=0)

    def trip(u, t, slot):
        rng = range(npar)
        r0s = [(t * npar + i) * CHUNK for i in rng]
        qc = [q_s[u, r0:r0 + CHUNK, :] for r0 in r0s]
        kc = [k_s[u, r0:r0 + CHUNK, :] for r0 in r0s]
        vc = [v_s[u, r0:r0 + CHUNK, :] for r0 in r0s]
        c8_all = _mm_01(tri8, jnp.concatenate([la_s[u, r0:r0 + CHUNK, :] for r0 in r0s], axis=1))
        for i in rng:
            c8_s[u, r0s[i]:r0s[i] + CHUNK, :] = c8_all[:, i * LANE:(i + 1) * LANE]
        sums = [block_sums(u, r0s[i]) for i in rng]
        attn = [[None] * g_heads for _ in rng]
        for li, b in enumerate((32, 16, 8)):
            qs = [qc[i] * jnp.exp(cat(sums[i][0][b])) for i in rng]
            ks = [kc[i] * jnp.exp(cat(sums[i][1][b])) for i in rng]
            for g in range(g_heads):
                for i in rng:
                    qg = qs[i] * head_masks[g] if g_heads > 1 else qs[i]
                    term = _mm_nt(qg, ks[i]) * level_masks[li]
                    attn[i][g] = term if li == 0 else attn[i][g] + term
        for i in rng:
            dg = diag_terms(qc[i], kc[i], cat(sums[i][0][8]))
            for g in range(g_heads):
                attn[i][g] = attn[i][g] + dg[g]
        o = [apply(attn[i], vc[i]) for i in rng]
        q_inter = [qc[i] * jnp.exp(cat(sums[i][0][64])) for i in rng]
        k_state = [kc[i] * jnp.exp(cat(sums[i][1][64])) for i in rng]
        upd = [_mm_tn(vc[i], k_state[i]) for i in rng]
        st = state_ref[u]
        for i in rng:
            o_s[u, r0s[i]:r0s[i] + CHUNK, :] = o[i] + _mm_nt(q_inter[i], st)
            decay_last = jnp.exp(sums[i][0][64][nblk - 1][SUB - 1:SUB, :])
            st = st * decay_last + (upd[i] * state_mask if g_heads > 1 else upd[i])
        state_ref[u] = st

    def apply(att, v):
        if g_heads > 1:
            return jnp.concatenate([_mm(att[g], v[:, g * LANE:(g + 1) * LANE]) for g in range(g_heads)], axis=1)
        return _mm(att[0], v)

    def diag_terms(q, k, c8):
        qd = q * jnp.exp(c8)
        kd = k * jnp.exp(jnp.minimum(-c8, DD_SAFE_EXP))
        return [_mm_nt(qd * head_masks[g] if g_heads > 1 else qd, kd) * diag_mask for g in range(g_heads)]

    def fix_unit(u, nchunk):
        def body(c, carry):
            r0 = pl.multiple_of(c * CHUNK, CHUNK)
            rows = pl.ds(r0, CHUNK)
            fast = apply(diag_terms(q_s[u, rows, :], k_s[u, rows, :], c8_s[u, rows, :]), v_s[u, rows, :])
            for r in range(nblk):
                qr = q_s[u, pl.ds(r0 + r * SUB, SUB), :]
                cr = c8_s[u, pl.ds(r0 + r * SUB, SUB), :]
                for jj in range(SUB):
                    krow = k_s[u, pl.ds(r0 + r * SUB + jj, 1), :]
                    crow = c8_s[u, pl.ds(r0 + r * SUB + jj, 1), :]
                    pr = qr * krow * jnp.exp(jnp.where(sub >= jj, cr - crow, -jnp.inf))
                    p_s[(r * SUB + jj) * SUB:(r * SUB + jj + 1) * SUB, :] = pr
            rs = jnp.dot(p_s[...].astype(BF16), ones_bd, preferred_element_type=F32)
            od = []
            for r in range(nblk):
                acc = None
                for jj in range(SUB):
                    term = (rs[(r * SUB + jj) * SUB:(r * SUB + jj + 1) * SUB, :]
                            * v_s[u, pl.ds(r0 + r * SUB + jj, 1), :])
                    acc = term if acc is None else acc + term
                od.append(acc)
            o_s[u, rows, :] = o_s[u, rows, :] + (cat(od) - fast)
            return carry

        lax.fori_loop(0, nchunk, body, 0)

    return trip, fix_unit


def _dd_fix_if_unsafe(c8_s, fix_unit, nu, tb, units, epilogue):
    worst = None
    for u in range(nu):
        tot = -c8_s[u, pl.ds(SUB - 1, tb // SUB, stride=SUB), :]
        worst = tot if worst is None else jnp.maximum(worst, tot)

    @pl.when(jnp.max(worst) > DD_SAFE_EXP)
    def _():
        for u in range(nu):
            fix_unit(u, tb // CHUNK)
        for u, t in units:
            epilogue(u, t)


def _dd_scratch(nu, tb, dvp):
    return [pltpu.VMEM((nu, dvp, LANE), F32),
            pltpu.VMEM((nu, tb, LANE), F32), pltpu.VMEM((nu, tb, LANE), F32),
            pltpu.VMEM((nu, tb, dvp), F32),
            pltpu.VMEM((nu, tb, LANE), F32),
            pltpu.VMEM((nu, tb, dvp), F32),
            pltpu.VMEM((nu, tb, LANE), F32),
            pltpu.VMEM((CHUNK * SUB, LANE), F32)]


def _gla_kernel(q_ref, k_ref, v_ref, lr_ref, gate_ref, w2_ref, b2_ref, nw_ref,
                o_ref, state_ref, q_s, k_s, v_s, la_s, o_s, c8_s, p_s, *, tb):
    @pl.when(pl.program_id(1) == 0)
    def _():
        state_ref[...] = jnp.zeros_like(state_ref)

    npair = GLA_HEADS // 2
    rp = DD_NPAR * CHUNK
    trip, fix_unit = _dd_make_trip(q_s, k_s, v_s, la_s, o_s, state_ref, c8_s, p_s, 2)

    def prologue(u, t):
        lo = t * rp
        ls = slice(u * LANE, (u + 1) * LANE)
        vs = slice(u * 2 * LANE, (u + 1) * 2 * LANE)
        q_s[u, lo:lo + rp, :] = q_ref[lo:lo + rp, ls].astype(F32) * (GLA_DK ** -0.5)
        k_s[u, lo:lo + rp, :] = k_ref[lo:lo + rp, ls].astype(F32)
        v_s[u, lo:lo + rp, :] = v_ref[lo:lo + rp, vs].astype(F32)
        z = _mm(lr_ref[lo:lo + rp, :], w2_ref[:, ls]) + b2_ref[:, ls]
        la_s[u, lo:lo + rp, :] = _log_sigmoid(z) * (1.0 / GLA_NORMALIZER)

    def epilogue(u, t):
        lo = t * rp
        for g in range(2):
            hs = slice((2 * u + g) * LANE, (2 * u + g + 1) * LANE)
            o = o_s[u, lo:lo + rp, g * LANE:(g + 1) * LANE]
            oh = o * lax.rsqrt(jnp.mean(o * o, axis=-1, keepdims=True) + RMS_EPS) * nw_ref[...]
            o_ref[lo:lo + rp, hs] = (oh * _silu(gate_ref[lo:lo + rp, hs].astype(F32))).astype(o_ref.dtype)

    units = [(u, t) for u in range(npair) for t in range(tb // rp)]
    _schedule(units, prologue, trip, epilogue)
    _dd_fix_if_unsafe(c8_s, fix_unit, npair, tb, units, epilogue)


def _gla_call(pb, pf, w2p, b2, norm_w, layer, batch, seq):
    tb = min(TB_MIX, seq)
    nb = seq // tb
    t = batch * seq
    kern = functools.partial(_gla_kernel, tb=tb)
    return pl.pallas_call(
        kern,
        grid=(batch, nb),
        in_specs=[pl.BlockSpec((tb, GLA_QK), lambda b, i: (b * nb + i, B_Q // GLA_QK)),
                  pl.BlockSpec((tb, GLA_QK), lambda b, i: (b * nb + i, B_K // GLA_QK)),
                  pl.BlockSpec((tb, GLA_V), lambda b, i: (b * nb + i, B_V // GLA_V)),
                  pl.BlockSpec((tb, LANE), lambda b, i: (b * nb + i, F_LR // LANE)),
                  pl.BlockSpec((tb, GLA_V), lambda b, i: (b * nb + i, B_G // GLA_V)),
                  pl.BlockSpec((None, LANE, GLA_QK), lambda b, i: (layer, 0, 0)),
                  pl.BlockSpec((None, 1, GLA_QK), lambda b, i: (layer, 0, 0)),
                  pl.BlockSpec((None, 1, GLA_DV), lambda b, i: (layer, 0, 0))],
        out_specs=pl.BlockSpec((tb, GLA_V), lambda b, i: (b * nb + i, 0)),
        out_shape=jax.ShapeDtypeStruct((t, GLA_V), BF16),
        scratch_shapes=_dd_scratch(GLA_HEADS // 2, tb, 2 * LANE),
        compiler_params=_cparams(("parallel", "arbitrary")),
        name="gla",
    )(pb, pb, pb, pf, pb, w2p, b2, norm_w)


def _hgrn_kernel(q_ref, f_ref, v_ref, gate_ref, lbl_ref, nw_ref,
                 o_ref, state_ref, q_s, k_s, v_s, la_s, o_s, c8_s, p_s, *, tb, layer):
    @pl.when(pl.program_id(1) == 0)
    def _():
        state_ref[...] = jnp.zeros_like(state_ref)

    logits = lbl_ref[...]
    mx = jnp.max(logits, axis=0, keepdims=True)
    ex = jnp.exp(logits - mx)
    p = ex / jnp.sum(ex, axis=0, keepdims=True)
    acc = p[0:1, :]
    for r in range(1, layer + 1):
        acc = acc + p[r:r + 1, :]
    lb = jnp.clip(acc - p[0:1, :], 0.0, 1.0)
    log_lb = jnp.log(jnp.maximum(lb, LB_FLOOR))
    log_1m = jnp.log(1.0 - lb)

    rp = DD_NPAR * CHUNK
    trip, fix_unit = _dd_make_trip(q_s, k_s, v_s, la_s, o_s, state_ref, c8_s, p_s, 1)

    def prologue(u, t):
        lo = t * rp
        hs = slice(u * LANE, (u + 1) * LANE)
        cf = f_ref[lo:lo + rp, hs]
        second = log_1m[:, hs] + _log_sigmoid(cf)
        llb = log_lb[:, hs]
        la_s[u, lo:lo + rp, :] = jnp.maximum(llb, second) + jnp.log(1.0 + jnp.exp(-jnp.abs(llb - second)))
        k_s[u, lo:lo + rp, :] = (1.0 - lb[:, hs]) * _sigmoid_t(-cf)
        q_s[u, lo:lo + rp, :] = _silu(q_ref[lo:lo + rp, hs].astype(F32)) * (HGRN_EXPAND ** -0.5)
        v_s[u, lo:lo + rp, :] = v_ref[lo:lo + rp, hs].astype(F32)

    def epilogue(u, t):
        lo = t * rp
        hs = slice(u * LANE, (u + 1) * LANE)
        o = o_s[u, lo:lo + rp, :]
        oh = o * lax.rsqrt(jnp.mean(o * o, axis=-1, keepdims=True) + RMS_EPS) * nw_ref[...]
        o_ref[lo:lo + rp, hs] = (oh * _silu(gate_ref[lo:lo + rp, hs].astype(F32))).astype(o_ref.dtype)

    units = [(u, t) for u in range(HGRN_HEADS) for t in range(tb // rp)]
    _schedule(units, prologue, trip, epilogue)
    _dd_fix_if_unsafe(c8_s, fix_unit, HGRN_HEADS, tb, units, epilogue)


def _hgrn_call(pb, pf, lb_logits, norm_w, layer, batch, seq):
    tb = min(TB_MIX, seq)
    nb = seq // tb
    t = batch * seq
    kern = functools.partial(_hgrn_kernel, tb=tb, layer=layer)
    return pl.pallas_call(
        kern,
        grid=(batch, nb),
        in_specs=[pl.BlockSpec((tb, HGRN_QK), lambda b, i: (b * nb + i, C_Q // HGRN_QK)),
                  pl.BlockSpec((tb, HGRN_QK), lambda b, i: (b * nb + i, F_CF // HGRN_QK)),
                  pl.BlockSpec((tb, HGRN_V), lambda b, i: (b * nb + i, C_I // HGRN_V)),
                  pl.BlockSpec((tb, HGRN_V), lambda b, i: (b * nb + i, C_G // HGRN_V)),
                  pl.BlockSpec((DEPTH, HGRN_QK), lambda b, i: (0, 0)),
                  pl.BlockSpec((None, 1, HGRN_DV), lambda b, i: (layer, 0, 0))],
        out_specs=pl.BlockSpec((tb, HGRN_V), lambda b, i: (b * nb + i, 0)),
        out_shape=jax.ShapeDtypeStruct((t, HGRN_V), BF16),
        scratch_shapes=_dd_scratch(HGRN_HEADS, tb, LANE),
        compiler_params=_cparams(("parallel", "arbitrary")),
        name="hgrn",
    )(pb, pf, pb, pb, lb_logits, norm_w)


def _merge_kernel(oa_ref, ob_ref, oc_ref, ma_ref, mb_ref, mc_ref, h_ref,
                  wa_ref, wb_ref, wc_ref, wo_ref, g_ref, b_ref, ox_ref, *, tm):
    y = (_sigmoid_t(ma_ref[...].astype(F32)) * jnp.dot(oa_ref[...], wa_ref[...], preferred_element_type=F32)
         + _sigmoid_t(mb_ref[...].astype(F32)) * jnp.dot(ob_ref[...], wb_ref[...], preferred_element_type=F32)
         + _sigmoid_t(mc_ref[...].astype(F32)) * jnp.dot(oc_ref[...], wc_ref[...], preferred_element_type=F32))
    mix = _mm(y, wo_ref[...])
    _to_token_major(ox_ref, _layer_norm(ALPHA * h_ref[...] + mix, g_ref[...], b_ref[...]), tm)


def _merge_call(o_a, o_b, o_c, pb, h, wa, wb, wc, wo, g, b, layer):
    t, d = h.shape
    tm = min(TM_ROWS, t)

    def row(width):
        return pl.BlockSpec((tm, width), lambda i: (i, 0))

    def wspec(kdim):
        return pl.BlockSpec((None, kdim, d), lambda i: (layer, 0, 0))

    vec = pl.BlockSpec((None, 1, d), lambda i: (layer, 0, 0))
    return pl.pallas_call(
        functools.partial(_merge_kernel, tm=tm),
        grid=(t // tm,),
        in_specs=[row(GDN_V), row(GLA_V), row(HGRN_V),
                  pl.BlockSpec((tm, d), lambda i: (i, M_A // d)),
                  pl.BlockSpec((tm, d), lambda i: (i, M_B // d)),
                  pl.BlockSpec((tm, d), lambda i: (i, M_C // d)),
                  row(d), wspec(GDN_V), wspec(GLA_V), wspec(HGRN_V), wspec(d), vec, vec],
        out_specs=pl.BlockSpec((tm * NPIECE, LANE), lambda i: (i, 0)),
        out_shape=jax.ShapeDtypeStruct((t * NPIECE, LANE), F32),
        compiler_params=_cparams(("parallel",)),
        name="merge",
    )(o_a, o_b, o_c, pb, pb, pb, h, wa, wb, wc, wo, g, b)


def _route_pairs(scores_t, bias_ref):
    s = [scores_t[e:e + 1, :] for e in range(N_EXPERTS)]
    sel = [s[e] + bias_ref[e:e + 1, 0:1] for e in range(N_EXPERTS)]
    gscore = []
    for g in range(N_GROUPS):
        a, b, c, d = sel[4 * g:4 * g + 4]
        hi1, lo1 = jnp.maximum(a, b), jnp.minimum(a, b)
        hi2, lo2 = jnp.maximum(c, d), jnp.minimum(c, d)
        top1 = jnp.maximum(hi1, hi2)
        top2 = jnp.maximum(jnp.minimum(hi1, hi2), jnp.maximum(lo1, lo2))
        gscore.append(top1 + top2)
    best = gscore[0]
    gidx = jnp.zeros_like(best, dtype=jnp.int32)
    for g in range(1, N_GROUPS):
        take = gscore[g] > best
        best = jnp.where(take, gscore[g], best)
        gidx = jnp.where(take, g, gidx)
    ing, raw = [], []
    for kk in range(EXPERTS_PER_GROUP):
        vs, vr = sel[kk], s[kk]
        for g in range(1, N_GROUPS):
            pick = gidx == g
            vs = jnp.where(pick, sel[4 * g + kk], vs)
            vr = jnp.where(pick, s[4 * g + kk], vr)
        ing.append(vs)
        raw.append(vr)
    b1 = ing[0]
    i1 = jnp.zeros_like(gidx)
    for kk in range(1, EXPERTS_PER_GROUP):
        take = ing[kk] > b1
        b1 = jnp.where(take, ing[kk], b1)
        i1 = jnp.where(take, kk, i1)
    neg = jnp.full_like(b1, -jnp.inf)
    b2 = neg
    i2 = jnp.zeros_like(gidx)
    for kk in range(EXPERTS_PER_GROUP):
        cand = jnp.where(i1 == kk, neg, ing[kk])
        take = cand > b2
        b2 = jnp.where(take, cand, b2)
        i2 = jnp.where(take, kk, i2)
    w1 = raw[0]
    w2 = raw[0]
    for kk in range(1, EXPERTS_PER_GROUP):
        w1 = jnp.where(i1 == kk, raw[kk], w1)
        w2 = jnp.where(i2 == kk, raw[kk], w2)
    tot = w1 + w2
    w1 = w1 / tot
    w2 = w2 / tot
    first_lower = i1 < i2
    lo = jnp.where(first_lower, i1, i2)
    hi = jnp.where(first_lower, i2, i1)
    pidx = jnp.where(lo == 0, hi - 1, jnp.where(lo == 1, hi + 1, 5))
    bkt = gidx * 6 + pidx
    return bkt, jnp.where(first_lower, w1, w2), jnp.where(first_lower, w2, w1)


def _moe_route_kernel(hx_ref, wr_ref, rb_ref, bkt_ref, rank_ref, wab_ref, cnt_ref, carry_ref, *, tm):
    @pl.when(pl.program_id(0) == 0)
    def _():
        carry_ref[...] = jnp.zeros_like(carry_ref)

    logits_t = _mm_nt_f32(wr_ref[...], _from_token_major(hx_ref, tm))
    bkt, wa, wb = _route_pairs(_sigmoid(logits_t), rb_ref)
    sub = lax.broadcasted_iota(jnp.int32, (32, tm), 0)
    oh = jnp.where(sub == bkt, 1.0, 0.0)
    r = lax.broadcasted_iota(jnp.int32, (tm, tm), 0)
    c = lax.broadcasted_iota(jnp.int32, (tm, tm), 1)
    earlier = jnp.where(r < c, 1.0, 0.0).astype(BF16)
    before = jnp.dot(oh.astype(BF16), earlier, preferred_element_type=F32)
    carry = carry_ref[...]
    rank = jnp.sum(oh * (before + carry[:, 0:1]), axis=0, keepdims=True)
    carry = carry + jnp.sum(oh, axis=1, keepdims=True)
    carry_ref[...] = carry
    cnt_ref[...] = carry
    bkt_ref[...] = bkt
    rank_ref[...] = rank.astype(jnp.int32)
    pad = jnp.zeros((LANE - 2, tm), F32)
    wab_ref[...] = jnp.transpose(jnp.concatenate([wa, wb, pad], axis=0))


def _moe_tables_kernel(cnt_ref, bkt_ref, rank_ref, pos_ref, tabs_ref, *, t, tr):
    cnt = cnt_ref[...]
    sz = jnp.floor((cnt + (tr - 1)) * (1.0 / tr)) * tr
    r = lax.broadcasted_iota(jnp.int32, (32, 32), 0)
    c = lax.broadcasted_iota(jnp.int32, (32, 32), 1)
    start = jnp.dot(jnp.where(c < r, 1.0, 0.0), sz, preferred_element_type=F32, precision=HIGHEST)
    end = start + sz
    sub = lax.broadcasted_iota(jnp.int32, (32, t), 0)
    pos = jnp.sum(jnp.where(sub == bkt_ref[...], start[:, 0:1], 0.0), axis=0, keepdims=True)
    pos_ref[...] = pos.astype(jnp.int32) + rank_ref[...]
    brow = lax.broadcasted_iota(jnp.int32, (32, LANE), 0)
    tile0 = lax.broadcasted_iota(jnp.int32, (32, LANE), 1).astype(F32) * tr
    tbk = jnp.sum(jnp.where((brow < MOE_NBKT) & (end <= tile0), 1, 0), axis=0, keepdims=True)
    tbk = jnp.minimum(tbk, MOE_NBKT - 1)
    total = end[MOE_NBKT - 1:MOE_NBKT, :]
    valid = jnp.where(tile0[0:1, :] < total, 1, 0)
    g = jnp.where(tbk >= 6, 1, 0) + jnp.where(tbk >= 12, 1, 0) + jnp.where(tbk >= 18, 1, 0)
    p = tbk - 6 * g
    ge3 = jnp.where(p >= 3, 1, 0)
    ge5 = jnp.where(p >= 5, 1, 0)
    ea = 4 * g + ge3 + ge5
    eb = 4 * g + p + 1 - 2 * ge3 - ge5
    zero = jnp.zeros((SUBLANE - 3, LANE), jnp.int32)
    tabs_ref[...] = jnp.concatenate([ea, eb, valid, zero], axis=0)


def _moe_group_kernel(ea_ref, eb_ref, vd_ref, xs_ref, ws_ref, wga_ref, wua_ref, wda_ref,
                      wgb_ref, wub_ref, wdb_ref, ys_ref, wg_s, wu_s, wd_s, *, tr):
    j = pl.program_id(0)
    prev = jnp.maximum(j - 1, 0)

    @pl.when((j == 0) | (ea_ref[j] != ea_ref[prev]))
    def _():
        wg_s[0] = wga_ref[...].astype(BF16)
        wu_s[0] = wua_ref[...].astype(BF16)
        wd_s[0] = wda_ref[...].astype(BF16)

    @pl.when((j == 0) | (eb_ref[j] != eb_ref[prev]))
    def _():
        wg_s[1] = wgb_ref[...].astype(BF16)
        wu_s[1] = wub_ref[...].astype(BF16)
        wd_s[1] = wdb_ref[...].astype(BF16)

    @pl.when(vd_ref[j] > 0)
    def _():
        x = _from_token_major(xs_ref, tr).astype(BF16)
        w = ws_ref[...]

        def ffn(s, cw):
            hg = jnp.dot(x, wg_s[s], preferred_element_type=F32)
            hu = jnp.dot(x, wu_s[s], preferred_element_type=F32)
            hid = _silu(hg) * hu * cw
            return jnp.dot(hid.astype(BF16), wd_s[s], preferred_element_type=F32)

        y = ffn(0, w[:, 0:1]) + ffn(1, w[:, 1:2])
        _to_token_major(ys_ref, y, tr)


def _moe_ln_kernel(hx_ref, y_ref, g_ref, b_ref, o_ref, ob_ref, *, tm):
    out = _layer_norm(ALPHA * _from_token_major(hx_ref, tm) + _from_token_major(y_ref, tm),
                      g_ref[...], b_ref[...])
    o_ref[...] = out
    ob_ref[...] = out.astype(BF16)


def _sc_mesh_info():
    info = plsc.get_sparse_core_info()
    mesh = plsc.VectorSubcoreMesh(core_axis_name="c", subcore_axis_name="s")
    return mesh, info.num_cores, info.num_subcores


def _sc_scatter_rows(x3, w2, pos, n_rows):
    t = x3.shape[0]
    mesh, nc, ns = _sc_mesh_info()
    per_w = t // (nc * ns)

    @functools.partial(
        pl.kernel, mesh=mesh,
        out_type=[jax.ShapeDtypeStruct((n_rows,) + x3.shape[1:], x3.dtype),
                  jax.ShapeDtypeStruct((n_rows,) + w2.shape[1:], w2.dtype)],
        scratch_types=[pltpu.VMEM((SC_CH,), jnp.int32), pltpu.VMEM((SC_CH,) + x3.shape[1:], x3.dtype),
                       pltpu.VMEM((SC_CH,) + w2.shape[1:], w2.dtype),
                       pltpu.SemaphoreType.DMA, pltpu.SemaphoreType.DMA])
    def k(x_hbm, w_hbm, idx_hbm, ox_hbm, ow_hbm, idx_v, rows_v, wrows_v, sem_x, sem_w):
        base = (lax.axis_index("s") * nc + lax.axis_index("c")) * per_w

        @pl.loop(0, per_w // SC_CH)
        def _(j):
            off = base + j * SC_CH
            pltpu.sync_copy(idx_hbm.at[pl.ds(off, SC_CH)], idx_v)
            pltpu.sync_copy(x_hbm.at[pl.ds(off, SC_CH)], rows_v)
            pltpu.sync_copy(w_hbm.at[pl.ds(off, SC_CH)], wrows_v)
            cx = pltpu.async_copy(rows_v, ox_hbm.at[idx_v], sem_x)
            cw = pltpu.async_copy(wrows_v, ow_hbm.at[idx_v], sem_w)
            cx.wait()
            cw.wait()

    return k(x3, w2, pos)


def _sc_gather_rows(y3, pos):
    t = pos.shape[0]
    mesh, nc, ns = _sc_mesh_info()
    per_w = t // (nc * ns)

    @functools.partial(
        pl.kernel, mesh=mesh,
        out_type=jax.ShapeDtypeStruct((t,) + y3.shape[1:], y3.dtype),
        scratch_types=[pltpu.VMEM((SC_CH,), jnp.int32), pltpu.VMEM((SC_CH,) + y3.shape[1:], y3.dtype),
                       pltpu.SemaphoreType.DMA])
    def k(y_hbm, idx_hbm, o_hbm, idx_v, rows_v, sem):
        base = (lax.axis_index("s") * nc + lax.axis_index("c")) * per_w

        @pl.loop(0, per_w // SC_CH)
        def _(j):
            off = base + j * SC_CH
            pltpu.sync_copy(idx_hbm.at[pl.ds(off, SC_CH)], idx_v)
            pltpu.async_copy(y_hbm.at[idx_v], rows_v, sem).wait()
            pltpu.sync_copy(rows_v, o_hbm.at[pl.ds(off, SC_CH)])

    return k(y3, pos)


def _moe_sparse(hx, wr_t, rbias, wg, wu, wd, g, b, layer):
    t = hx.shape[0] // NPIECE
    d = D_MODEL
    tm = min(TM_MOE, t)
    tr = MOE_TR
    nt = t // tr + MOE_NBKT
    n_rows = nt * tr
    row1 = pl.BlockSpec((1, tm), lambda i: (0, i))
    rows_tm = pl.BlockSpec((tm * NPIECE, LANE), lambda i: (i, 0))
    bkt, rank, wab, cnt = pl.pallas_call(
        functools.partial(_moe_route_kernel, tm=tm),
        grid=(t // tm,),
        in_specs=[rows_tm,
                  pl.BlockSpec((N_EXPERTS, d), lambda i: (0, 0)),
                  pl.BlockSpec((N_EXPERTS, LANE), lambda i: (0, 0))],
        out_specs=[row1, row1, pl.BlockSpec((tm, LANE), lambda i: (i, 0)),
                   pl.BlockSpec((32, LANE), lambda i: (0, 0))],
        out_shape=[jax.ShapeDtypeStruct((1, t), jnp.int32), jax.ShapeDtypeStruct((1, t), jnp.int32),
                   jax.ShapeDtypeStruct((t, LANE), F32), jax.ShapeDtypeStruct((32, LANE), F32)],
        scratch_shapes=[pltpu.VMEM((32, LANE), F32)],
        compiler_params=_cparams(("arbitrary",)),
        name="moe_route",
    )(hx, wr_t, rbias)
    pos, tabs = pl.pallas_call(
        functools.partial(_moe_tables_kernel, t=t, tr=tr),
        out_shape=[jax.ShapeDtypeStruct((1, t), jnp.int32), jax.ShapeDtypeStruct((SUBLANE, LANE), jnp.int32)],
        compiler_params=pltpu.CompilerParams(vmem_limit_bytes=VMEM_LIMIT),
        name="moe_tables",
    )(cnt, bkt, rank)
    pos = pos.reshape(t)
    xs3, ws = _sc_scatter_rows(hx.reshape(t, NPIECE, LANE), wab, pos, n_rows)

    def wspec(which, shape):
        if which == 0:
            return pl.BlockSpec((None, None) + shape, lambda j, ea, eb, vd: (layer, ea[j], 0, 0))
        return pl.BlockSpec((None, None) + shape, lambda j, ea, eb, vd: (layer, eb[j], 0, 0))

    ys = pl.pallas_call(
        functools.partial(_moe_group_kernel, tr=tr),
        grid_spec=pltpu.PrefetchScalarGridSpec(
            num_scalar_prefetch=3,
            grid=(nt,),
            in_specs=[pl.BlockSpec((tr * NPIECE, LANE), lambda j, ea, eb, vd: (j, 0)),
                      pl.BlockSpec((tr, LANE), lambda j, ea, eb, vd: (j, 0)),
                      wspec(0, (d, D_FF)), wspec(0, (d, D_FF)), wspec(0, (D_FF, d)),
                      wspec(1, (d, D_FF)), wspec(1, (d, D_FF)), wspec(1, (D_FF, d))],
            out_specs=pl.BlockSpec((tr * NPIECE, LANE), lambda j, ea, eb, vd: (j, 0)),
            scratch_shapes=[pltpu.VMEM((2, d, D_FF), BF16), pltpu.VMEM((2, d, D_FF), BF16),
                            pltpu.VMEM((2, D_FF, d), BF16)]),
        out_shape=jax.ShapeDtypeStruct((n_rows * NPIECE, LANE), F32),
        compiler_params=_cparams(("arbitrary",)),
        name="moe_experts",
    )(tabs[0, :nt], tabs[1, :nt], tabs[2, :nt], xs3.reshape(n_rows * NPIECE, LANE), ws,
      wg, wu, wd, wg, wu, wd)
    y3 = _sc_gather_rows(ys.reshape(n_rows, NPIECE, LANE), pos)
    vec = pl.BlockSpec((None, 1, d), lambda i: (layer, 0, 0))
    row = pl.BlockSpec((tm, d), lambda i: (i, 0))
    return pl.pallas_call(
        functools.partial(_moe_ln_kernel, tm=tm),
        grid=(t // tm,),
        in_specs=[rows_tm, rows_tm, vec, vec],
        out_specs=[row, row],
        out_shape=[jax.ShapeDtypeStruct((t, d), F32), jax.ShapeDtypeStruct((t, d), BF16)],
        compiler_params=_cparams(("parallel",)),
        name="moe_ln",
    )(hx, y3.reshape(t * NPIECE, LANE), g, b)


def _pack_w_in(w_in):
    wt = jnp.swapaxes(w_in, 1, 2)
    (a_q, a_k, a_v, a_beta, a_dt, a_g, b_q, b_k, b_v, b_lr, b_g,
     c_q, c_f, c_i, c_g, m_a, m_b, m_c) = jnp.split(wt, SPLIT_POINTS, axis=1)
    depth, _, d = wt.shape
    a_s = jnp.concatenate([a_beta, a_dt, jnp.zeros((depth, LANE - 2 * GDN_HEADS, d), wt.dtype)], 1)
    b_lrp = jnp.concatenate([b_lr, jnp.zeros((depth, LANE - GLA_RANK, d), wt.dtype)], 1)
    wb = jnp.concatenate([m_a, m_b, m_c, a_q, a_k, a_v, a_g, b_q, b_k, b_v, b_g, c_q, c_i, c_g], 1).astype(BF16)
    wf = jnp.concatenate([c_f, a_s, b_lrp], 1).astype(BF16)
    assert wb.shape[1] == NPB and wf.shape[1] == NPF
    return wb, wf


def _prepare(w_in, gdn_conv, gdn_a_log, gdn_dt_bias, gdn_norm, gla_w2, gla_b2, gla_norm, hgrn_lb_logits,
             hgrn_norm, w_br_a, w_br_b, w_br_c, w_out, ln1_g, ln1_b, w_router, router_bias, w_gate, w_up,
             w_down, ln2_g, ln2_b):
    depth = w_in.shape[0]
    d = w_out.shape[-1]
    w_pb, w_pf = _pack_w_in(w_in)
    return dict(
        w_pb=w_pb, w_pf=w_pf,
        gdn_conv=gdn_conv,
        gdn_par=jnp.pad(jnp.stack([gdn_a_log, gdn_dt_bias], axis=1),
                        ((0, 0), (0, SUBLANE - 2), (GDN_HEADS, LANE - 2 * GDN_HEADS))),
        gdn_norm=gdn_norm.reshape(depth, 1, GDN_DV),
        w2p=jnp.concatenate([gla_w2, jnp.zeros((depth, LANE - GLA_RANK, GLA_QK), gla_w2.dtype)], axis=1),
        gla_b2=gla_b2.reshape(depth, 1, GLA_QK),
        gla_norm=gla_norm.reshape(depth, 1, GLA_DV),
        lb_logits=hgrn_lb_logits,
        hgrn_norm=hgrn_norm.reshape(depth, 1, HGRN_DV),
        wa=w_br_a.astype(BF16), wb=w_br_b.astype(BF16), wc=w_br_c.astype(BF16), wo=w_out.astype(BF16),
        ln1_g=ln1_g.reshape(depth, 1, d), ln1_b=ln1_b.reshape(depth, 1, d),
        wr_t=jnp.transpose(w_router),
        rbias=jnp.broadcast_to(router_bias[:, None], (N_EXPERTS, LANE)),
        wg=w_gate, wu=w_up, wd=w_down,
        ln2_g=ln2_g.reshape(depth, 1, d), ln2_b=ln2_b.reshape(depth, 1, d),
    )


def _mixer_block(h, hb, p, layer, batch, seq):
    pb = _inproj_call(hb, p["w_pb"], layer, BF16, TM_PROJ, TN_PROJ, "inproj_b")
    pf = _inproj_call(hb, p["w_pf"], layer, F32, TM_PROJ // 2, NPF, "inproj_f")
    o_a = _gdn_call(pb, pf, p["gdn_conv"], p["gdn_par"], p["gdn_norm"], layer, batch, seq)
    o_b = _gla_call(pb, pf, p["w2p"], p["gla_b2"], p["gla_norm"], layer, batch, seq)
    o_c = _hgrn_call(pb, pf, p["lb_logits"], p["hgrn_norm"], layer, batch, seq)
    return _merge_call(o_a, o_b, o_c, pb, h, p["wa"], p["wb"], p["wc"], p["wo"], p["ln1_g"], p["ln1_b"], layer)


def _ffn_block(hx, p, layer):
    return _moe_sparse(hx, p["wr_t"], p["rbias"], p["wg"], p["wu"], p["wd"], p["ln2_g"], p["ln2_b"], layer)


def kernel(x, ln0_g, ln0_b, w_in, gdn_conv, gdn_a_log, gdn_dt_bias, gdn_norm, gla_w2, gla_b2, gla_norm,
           hgrn_lb_logits, hgrn_norm, w_br_a, w_br_b, w_br_c, w_out, ln1_g, ln1_b, w_router, router_bias,
           w_gate, w_up, w_down, ln2_g, ln2_b):
    batch, seq, d = x.shape
    p = _prepare(w_in, gdn_conv, gdn_a_log, gdn_dt_bias, gdn_norm, gla_w2, gla_b2, gla_norm, hgrn_lb_logits,
                 hgrn_norm, w_br_a, w_br_b, w_br_c, w_out, ln1_g, ln1_b, w_router, router_bias, w_gate, w_up,
                 w_down, ln2_g, ln2_b)
    h, hb = _ln_call(x.reshape(batch * seq, d), ln0_g, ln0_b)
    for layer in range(w_in.shape[0]):
        hx = _mixer_block(h, hb, p, layer, batch, seq)
        h, hb = _ffn_block(hx, p, layer)
    return h.reshape(batch, seq, d)
```

```python
import functools

import numpy as np
import jax
import jax.numpy as jnp
from jax import lax
from jax.experimental import pallas as pl
from jax.experimental.pallas import tpu as pltpu
from jax.experimental.pallas import tpu_sc as plsc

F32 = jnp.float32
BF16 = jnp.bfloat16
HIGHEST = lax.Precision.HIGHEST

D_MODEL = 1024
DEPTH = 4
CHUNK = 64
GDN_HEADS, GDN_DK, GDN_DV, CONV_W = 4, 128, 128, 4
GLA_HEADS, GLA_DK, GLA_DV, GLA_RANK, GLA_NORMALIZER = 4, 64, 128, 16, 16.0
HGRN_HEADS, HGRN_EXPAND, HGRN_DV = 4, 128, 128
LB_FLOOR = 1e-30
N_EXPERTS, N_GROUPS, TOP_K, D_FF = 16, 4, 2, 256
EXPERTS_PER_GROUP = N_EXPERTS // N_GROUPS
ALPHA = (2.0 * DEPTH) ** 0.25
LN_EPS = 1e-5
RMS_EPS = 1e-6

GDN_QK = GDN_HEADS * GDN_DK
GDN_V = GDN_HEADS * GDN_DV
GLA_QK = GLA_HEADS * GLA_DK
GLA_V = GLA_HEADS * GLA_DV
HGRN_QK = HGRN_HEADS * HGRN_EXPAND
HGRN_V = HGRN_HEADS * HGRN_DV
SPLIT_SIZES = (GDN_QK, GDN_QK, GDN_V, GDN_HEADS, GDN_HEADS, GDN_V,
               GLA_QK, GLA_QK, GLA_V, GLA_RANK, GLA_V,
               HGRN_QK, HGRN_QK, HGRN_V, HGRN_V,
               D_MODEL, D_MODEL, D_MODEL)
SPLIT_POINTS = tuple(int(v) for v in np.cumsum(SPLIT_SIZES)[:-1])

LANE = 128
SUBLANE = 8
VMEM_LIMIT = 48 * 1024 * 1024

M_A, M_B, M_C = 0, 1024, 2048
A_Q, A_K, A_V, A_G = 3072, 3584, 4096, 4608
B_Q, B_K, B_V, B_G = 5120, 5376, 5632, 6144
C_Q, C_I, C_G = 6656, 7168, 7680
NPB = 8192
F_CF, F_AS, F_LR = 0, 512, 640
NPF = 768

TB_MIX = 1024
TM_ROWS = 512
TM_MOE = 1024
TM_PROJ, TN_PROJ = 2048, 2048
SUB = 8
DD_NPAR = 4
DD_SAFE_EXP = 60.0
GDN_NPAR = 8
MOE_TR = 512
MOE_NBKT = N_GROUPS * 6
NPIECE = D_MODEL // LANE
SC_CH = 64


def _cparams(sem):
    return pltpu.CompilerParams(dimension_semantics=sem, vmem_limit_bytes=VMEM_LIMIT)


def _mm(a, b):
    return jnp.dot(a.astype(BF16), b.astype(BF16), preferred_element_type=F32)


def _mm_nt(a, b):
    return lax.dot_general(a.astype(BF16), b.astype(BF16), (((1,), (1,)), ((), ())),
                           preferred_element_type=F32)


def _mm_tn(a, b):
    return lax.dot_general(a.astype(BF16), b.astype(BF16), (((0,), (0,)), ((), ())),
                           preferred_element_type=F32)


def _mm_nt_f32(a, b):
    a_hi = a.astype(BF16)
    a_mid = (a - a_hi.astype(F32)).astype(BF16)
    b_hi = b.astype(BF16)
    b_mid = (b - b_hi.astype(F32)).astype(BF16)
    dn = (((1,), (1,)), ((), ()))
    return (lax.dot_general(a_hi, b_hi, dn, preferred_element_type=F32)
            + lax.dot_general(a_hi, b_mid, dn, preferred_element_type=F32)
            + lax.dot_general(a_mid, b_hi, dn, preferred_element_type=F32))


def _split3(x):
    hi = x.astype(BF16)
    r = x - hi.astype(F32)
    mid = r.astype(BF16)
    lo = (r - mid.astype(F32)).astype(BF16)
    return hi, mid, lo


def _mm_01(m01, x):
    hi, mid, lo = _split3(x)
    return (jnp.dot(m01, hi, preferred_element_type=F32) + jnp.dot(m01, mid, preferred_element_type=F32)
            + jnp.dot(m01, lo, preferred_element_type=F32))


def _sigmoid(x):
    return 1.0 / (1.0 + jnp.exp(-x))


def _sigmoid_t(x):
    return 0.5 * jnp.tanh(0.5 * x) + 0.5


def _silu(x):
    return x * _sigmoid_t(x)


def _softplus(x):
    return jnp.maximum(x, 0.0) + jnp.log(1.0 + jnp.exp(-jnp.abs(x)))


def _log_sigmoid(x):
    return -_softplus(-x)


def _layer_norm(x, g, b):
    mu = jnp.mean(x, axis=-1, keepdims=True)
    xc = x - mu
    var = jnp.mean(xc * xc, axis=-1, keepdims=True)
    return xc * lax.rsqrt(var + LN_EPS) * g + b


def _to_token_major(ref, x, rows):
    for j in range(NPIECE):
        ref[pl.ds(j, rows, stride=NPIECE), :] = x[:, j * LANE:(j + 1) * LANE]


def _from_token_major(ref, rows):
    return jnp.concatenate([ref[pl.ds(j, rows, stride=NPIECE), :] for j in range(NPIECE)], axis=1)


def _ln_kernel(x_ref, g_ref, b_ref, o_ref, ob_ref):
    y = _layer_norm(x_ref[...], g_ref[...], b_ref[...])
    o_ref[...] = y
    ob_ref[...] = y.astype(BF16)


def _ln_call(x, g, b):
    t, d = x.shape
    tm = min(TM_ROWS, t)
    return pl.pallas_call(
        _ln_kernel,
        grid=(t // tm,),
        in_specs=[pl.BlockSpec((tm, d), lambda i: (i, 0)),
                  pl.BlockSpec((1, d), lambda i: (0, 0)),
                  pl.BlockSpec((1, d), lambda i: (0, 0))],
        out_specs=[pl.BlockSpec((tm, d), lambda i: (i, 0)), pl.BlockSpec((tm, d), lambda i: (i, 0))],
        out_shape=[jax.ShapeDtypeStruct((t, d), F32), jax.ShapeDtypeStruct((t, d), BF16)],
        compiler_params=_cparams(("parallel",)),
        name="ln0",
    )(x, g.reshape(1, d), b.reshape(1, d))


def _inproj_kernel(x_ref, wt_ref, o_ref):
    o_ref[...] = lax.dot_general(x_ref[...], wt_ref[...], (((1,), (1,)), ((), ())),
                                 preferred_element_type=F32).astype(o_ref.dtype)


def _inproj_call(hb, w, layer, out_dtype, tm, tn, name):
    t, d = hb.shape
    tm = min(tm, t)
    n = w.shape[-2]
    return pl.pallas_call(
        _inproj_kernel,
        grid=(n // tn, t // tm),
        in_specs=[pl.BlockSpec((tm, d), lambda j, i: (i, 0)),
                  pl.BlockSpec((None, tn, d), lambda j, i: (layer, j, 0))],
        out_specs=pl.BlockSpec((tm, tn), lambda j, i: (i, j)),
        out_shape=jax.ShapeDtypeStruct((t, n), out_dtype),
        compiler_params=_cparams(("parallel", "parallel")),
        name=name,
    )(hb, w)


def _schedule(units, prologue, trip, epilogue):
    prologue(*units[0])
    for n, u in enumerate(units):
        if n + 1 < len(units):
            prologue(*units[n + 1])
        trip(*u, n % 2)
        epilogue(*u)


def _gdn_kernel(qkv_ref, s_ref, gate_ref, cw_ref, par_ref, nw_ref, o_ref,
                state_ref, tail_ref, xbuf_ref, q_s, k_s, v_s, cumb_s, betab_s, cumrow_s, o_s, *, tb):
    nc = tb // CHUNK
    nh = GDN_HEADS
    off0 = SUBLANE - (CONV_W - 1)

    @pl.when(pl.program_id(1) == 0)
    def _():
        state_ref[...] = jnp.zeros_like(state_ref)
        tail_ref[...] = jnp.zeros_like(tail_ref)

    row = lax.broadcasted_iota(jnp.int32, (CHUNK, CHUNK), 0)
    col = lax.broadcasted_iota(jnp.int32, (CHUNK, CHUNK), 1)
    incl = col <= row
    strict = col < row
    tri = jnp.where(incl, 1.0, 0.0).astype(BF16)
    eye = jnp.where(col == row, 1.0, 0.0).astype(F32)
    npar = GDN_NPAR
    rp = npar * CHUNK
    xbuf_ref[0:SUBLANE, :] = tail_ref[...]

    def prologue(t):
        lo = t * rp
        xbuf_ref[SUBLANE + lo:SUBLANE + lo + rp, :] = qkv_ref[lo:lo + rp, :].astype(F32)
        for j in range(3 * nh):
            cs = slice(j * LANE, (j + 1) * LANE)
            y = xbuf_ref[off0 + lo:off0 + lo + rp, cs] * cw_ref[0:1, cs]
            for kk in range(1, CONV_W):
                y = y + xbuf_ref[off0 + kk + lo:off0 + kk + lo + rp, cs] * cw_ref[kk:kk + 1, cs]
            y = _silu(y)
            if j < nh:
                q_s[j, lo:lo + rp, :] = (y * lax.rsqrt(jnp.sum(y * y, axis=-1, keepdims=True) + RMS_EPS)
                                         * (GDN_DK ** -0.5))
            elif j < 2 * nh:
                k_s[j - nh, lo:lo + rp, :] = y * lax.rsqrt(jnp.sum(y * y, axis=-1, keepdims=True) + RMS_EPS)
            else:
                v_s[j - 2 * nh, lo:lo + rp, :] = y
        sc = s_ref[lo:lo + rp, :]
        beta_all = _sigmoid_t(sc)
        g_all = -jnp.exp(par_ref[0:1, :]) * _softplus(sc + par_ref[1:2, :])
        cum_all = jnp.concatenate([_mm_01(tri, g_all[c * CHUNK:(c + 1) * CHUNK, :]) for c in range(npar)], axis=0)
        cum_t = jnp.transpose(cum_all)
        for c in range(npar):
            cumrow_s[t * npar + c] = cum_t[0:SUBLANE, c * CHUNK:(c + 1) * CHUNK]
        for h in range(nh):
            cumb_s[h, lo:lo + rp, :] = jnp.broadcast_to(cum_all[:, nh + h:nh + h + 1], (rp, LANE))
            betab_s[h, lo:lo + rp, :] = jnp.broadcast_to(beta_all[:, h:h + 1], (rp, LANE))

    def trip(cp, _slot):
        chains = [(cp * npar + cc, h) for cc in range(npar) for h in range(nh)]
        r0s = [c * CHUNK for c, _ in chains]
        qc = [q_s[h, r0:r0 + CHUNK, :] for (_, h), r0 in zip(chains, r0s)]
        kc = [k_s[h, r0:r0 + CHUNK, :] for (_, h), r0 in zip(chains, r0s)]
        vc = [v_s[h, r0:r0 + CHUNK, :] for (_, h), r0 in zip(chains, r0s)]
        cumc = [cumb_s[h, r0:r0 + CHUNK, :] for (_, h), r0 in zip(chains, r0s)]
        bc = [betab_s[h, r0:r0 + CHUNK, :] for (_, h), r0 in zip(chains, r0s)]
        n = len(chains)
        kk = [_mm_nt(kc[i], kc[i]) for i in range(n)]
        qk = [_mm_nt(qc[i], kc[i]) for i in range(n)]
        decay = []
        for i, (c, h) in enumerate(chains):
            diff = cumc[i][:, 0:CHUNK] - cumrow_s[c][nh + h:nh + h + 1, :]
            decay.append(jnp.where(incl, jnp.exp(jnp.where(incl, diff, 0.0)), 0.0))
        a = [jnp.where(strict, bc[i][:, 0:CHUNK] * kk[i] * decay[i], 0.0) for i in range(n)]
        x = [eye - a[i] for i in range(n)]
        p = [_mm(a[i], a[i]) for i in range(n)]
        for it in range(5):
            x = [x[i] + _mm(x[i], p[i]) for i in range(n)]
            if it < 4:
                p = [_mm(p[i], p[i]) for i in range(n)]
        ecum = [jnp.exp(cumc[i]) for i in range(n)]
        sol = [_mm(x[i], jnp.concatenate([vc[i] * bc[i], kc[i] * (bc[i] * ecum[i])], axis=1)) for i in range(n)]
        attn = [qk[i] * decay[i] for i in range(n)]
        cum_last = [cumc[i][CHUNK - 1:CHUNK, :] for i in range(n)]
        k_state = [kc[i] * jnp.exp(cum_last[i] - cumc[i]) for i in range(n)]
        wqi = [jnp.concatenate([sol[i][:, GDN_DV:GDN_DV + GDN_DK], qc[i] * ecum[i]], axis=0) for i in range(n)]
        for cc in range(npar):
            idx = [cc * nh + h for h in range(nh)]
            s = [state_ref[h] for h in range(nh)]
            wq = [_mm(wqi[i], s[h]) for h, i in enumerate(idx)]
            v_new = [sol[i][:, 0:GDN_DV] - wq[h][0:CHUNK, :] for h, i in enumerate(idx)]
            av = [_mm(attn[i], v_new[h]) for h, i in enumerate(idx)]
            upd = [_mm_tn(k_state[i], v_new[h]) for h, i in enumerate(idx)]
            for h, i in enumerate(idx):
                state_ref[h] = s[h] * jnp.exp(cum_last[i]) + upd[h]
                o_s[r0s[i]:r0s[i] + CHUNK, h * LANE:(h + 1) * LANE] = wq[h][CHUNK:2 * CHUNK, :] + av[h]

    def epilogue(t):
        lo = t * rp
        for h in range(nh):
            hs = slice(h * LANE, (h + 1) * LANE)
            o = o_s[lo:lo + rp, hs]
            oh = o * lax.rsqrt(jnp.mean(o * o, axis=-1, keepdims=True) + RMS_EPS) * nw_ref[...]
            o_ref[lo:lo + rp, hs] = (oh * _silu(gate_ref[lo:lo + rp, hs].astype(F32))).astype(o_ref.dtype)

    _schedule([(t,) for t in range(nc // npar)], prologue, trip, epilogue)
    tail_ref[...] = xbuf_ref[tb:tb + SUBLANE, :]


def _gdn_call(pb, pf, conv_w, par, norm_w, layer, batch, seq):
    tb = min(TB_MIX, seq)
    nb = seq // tb
    nc = tb // CHUNK
    t = batch * seq
    wq = 2 * GDN_QK + GDN_V
    kern = functools.partial(_gdn_kernel, tb=tb)
    return pl.pallas_call(
        kern,
        grid=(batch, nb),
        in_specs=[pl.BlockSpec((tb, wq), lambda b, i: (b * nb + i, A_Q // wq)),
                  pl.BlockSpec((tb, LANE), lambda b, i: (b * nb + i, F_AS // LANE)),
                  pl.BlockSpec((tb, GDN_V), lambda b, i: (b * nb + i, A_G // GDN_V)),
                  pl.BlockSpec((None, CONV_W, wq), lambda b, i: (layer, 0, 0)),
                  pl.BlockSpec((None, SUBLANE, LANE), lambda b, i: (layer, 0, 0)),
                  pl.BlockSpec((None, 1, GDN_DV), lambda b, i: (layer, 0, 0))],
        out_specs=pl.BlockSpec((tb, GDN_V), lambda b, i: (b * nb + i, 0)),
        out_shape=jax.ShapeDtypeStruct((t, GDN_V), BF16),
        scratch_shapes=[pltpu.VMEM((GDN_HEADS, GDN_DK, GDN_DV), F32),
                        pltpu.VMEM((SUBLANE, wq), F32),
                        pltpu.VMEM((tb + SUBLANE, wq), F32),
                        pltpu.VMEM((GDN_HEADS, tb, LANE), F32),
                        pltpu.VMEM((GDN_HEADS, tb, LANE), F32),
                        pltpu.VMEM((GDN_HEADS, tb, LANE), F32),
                        pltpu.VMEM((GDN_HEADS, tb, LANE), F32),
                        pltpu.VMEM((GDN_HEADS, tb, LANE), F32),
                        pltpu.VMEM((nc, SUBLANE, CHUNK), F32),
                        pltpu.VMEM((tb, GDN_V), F32)],
        compiler_params=_cparams(("parallel", "arbitrary")),
        name="gdn",
    )(pb, pf, pb, conv_w, par, norm_w)


def _dd_make_trip(q_s, k_s, v_s, la_s, o_s, state_ref, c8_s, p_s, g_heads):
    nblk = CHUNK // SUB
    dkh = LANE // g_heads
    dvp = g_heads * LANE
    npar = DD_NPAR
    row = lax.broadcasted_iota(jnp.int32, (CHUNK, CHUNK), 0)
    col = lax.broadcasted_iota(jnp.int32, (CHUNK, CHUNK), 1)
    level_masks = []
    for sh in (5, 4, 3):
        same2b = jnp.right_shift(row, sh + 1) == jnp.right_shift(col, sh + 1)
        upper = (jnp.right_shift(row, sh) & 1) == 1
        lower = (jnp.right_shift(col, sh) & 1) == 0
        level_masks.append(jnp.where(same2b, jnp.where(upper, jnp.where(lower, 1.0, 0.0), 0.0), 0.0))
    tri8 = jnp.where(jnp.right_shift(row, 3) == jnp.right_shift(col, 3),
                     jnp.where(col <= row, 1.0, 0.0), 0.0).astype(BF16)
    diag_mask = tri8.astype(F32)
    lane128 = lax.broadcasted_iota(jnp.int32, (CHUNK, LANE), 1)
    head_masks = [jnp.where((lane128 >= g * dkh) & (lane128 < (g + 1) * dkh), 1.0, 0.0)
                  for g in range(g_heads)]
    sub = lax.broadcasted_iota(jnp.int32, (SUB, LANE), 0)
    dk_sh = dkh.bit_length() - 1
    lane_sh = LANE.bit_length() - 1
    orow = lax.broadcasted_iota(jnp.int32, (LANE, dvp), 0)
    ocol = lax.broadcasted_iota(jnp.int32, (LANE, dvp), 1)
    ones_bd = jnp.where(jnp.right_shift(orow, dk_sh) == jnp.right_shift(ocol, lane_sh), 1.0, 0.0).astype(BF16)
    srow = lax.broadcasted_iota(jnp.int32, (dvp, LANE), 0)
    scol = lax.broadcasted_iota(jnp.int32, (dvp, LANE), 1)
    state_mask = jnp.where(jnp.right_shift(srow, lane_sh) == jnp.right_shift(scol, dk_sh), 1.0, 0.0)

    def block_sums(u, r0):
        c8 = [c8_s[u, r0 + b * SUB:r0 + (b + 1) * SUB, :] for b in range(nblk)]
        t8 = [c8_s[u, r0 + (b + 1) * SUB - 1:r0 + (b + 1) * SUB, :] for b in range(nblk)]
        t16 = [t8[2 * b] + t8[2 * b + 1] for b in range(nblk // 2)]
        t32 = [t16[2 * b] + t16[2 * b + 1] for b in range(nblk // 4)]
        t64 = t32[0] + t32[1]
        c16 = [c8[b] + t8[b - 1] if b % 2 else c8[b] for b in range(nblk)]
        c32 = [c16[b] + t16[b // 2 - 1] if (b // 2) % 2 else c16[b] for b in range(nblk)]
        c64 = [c32[b] + t32[0] if b >= nblk // 2 else c32[b] for b in range(nblk)]
        pre = {8: c8, 16: c16, 32: c32, 64: c64}
        suf = {8: [t8[b] - c8[b] for b in range(nblk)],
               16: [t16[b // 2] - c16[b] for b in range(nblk)],
               32: [t32[b // 4] - c32[b] for b in range(nblk)],
               64: [t64 - c64[b] for b in range(nblk)]}
        return pre, suf

    def cat(pieces):
        return jnp.concatenate(pieces, axis=0)

    def trip(u, t, slot):
        rng = range(npar)
        r0s = [(t * npar + i) * CHUNK for i in rng]
        qc = [q_s[u, r0:r0 + CHUNK, :] for r0 in r0s]
        kc = [k_s[u, r0:r0 + CHUNK, :] for r0 in r0s]
        vc = [v_s[u, r0:r0 + CHUNK, :] for r0 in r0s]
        c8_all = _mm_01(tri8, jnp.concatenate([la_s[u, r0:r0 + CHUNK, :] for r0 in r0s], axis=1))
        for i in rng:
            c8_s[u, r0s[i]:r0s[i] + CHUNK, :] = c8_all[:, i * LANE:(i + 1) * LANE]
        sums = [block_sums(u, r0s[i]) for i in rng]
        attn = [[None] * g_heads for _ in rng]
        for li, b in enumerate((32, 16, 8)):
            qs = [qc[i] * jnp.exp(cat(sums[i][0][b])) for i in rng]
            ks = [kc[i] * jnp.exp(cat(sums[i][1][b])) for i in rng]
            for g in range(g_heads):
                for i in rng:
                    qg = qs[i] * head_masks[g] if g_heads > 1 else qs[i]
                    term = _mm_nt(qg, ks[i]) * level_masks[li]
                    attn[i][g] = term if li == 0 else attn[i][g] + term
        for i in rng:
            dg = diag_terms(qc[i], kc[i], cat(sums[i][0][8]))
            for g in range(g_heads):
                attn[i][g] = attn[i][g] + dg[g]
        o = [apply(attn[i], vc[i]) for i in rng]
        q_inter = [qc[i] * jnp.exp(cat(sums[i][0][64])) for i in rng]
        k_state = [kc[i] * jnp.exp(cat(sums[i][1][64])) for i in rng]
        upd = [_mm_tn(vc[i], k_state[i]) for i in rng]
        st = state_ref[u]
        for i in rng:
            o_s[u, r0s[i]:r0s[i] + CHUNK, :] = o[i] + _mm_nt(q_inter[i], st)
            decay_last = jnp.exp(sums[i][0][64][nblk - 1][SUB - 1:SUB, :])
            st = st * decay_last + (upd[i] * state_mask if g_heads > 1 else upd[i])
        state_ref[u] = st

    def apply(att, v):
        if g_heads > 1:
            return jnp.concatenate([_mm(att[g], v[:, g * LANE:(g + 1) * LANE]) for g in range(g_heads)], axis=1)
        return _mm(att[0], v)

    def diag_terms(q, k, c8):
        qd = q * jnp.exp(c8)
        kd = k * jnp.exp(jnp.minimum(-c8, DD_SAFE_EXP))
        return [_mm_nt(qd * head_masks[g] if g_heads > 1 else qd, kd) * diag_mask for g in range(g_heads)]

    def fix_unit(u, nchunk):
        def body(c, carry):
            r0 = pl.multiple_of(c * CHUNK, CHUNK)
            rows = pl.ds(r0, CHUNK)
            fast = apply(diag_terms(q_s[u, rows, :], k_s[u, rows, :], c8_s[u, rows, :]), v_s[u, rows, :])
            for r in range(nblk):
                qr = q_s[u, pl.ds(r0 + r * SUB, SUB), :]
                cr = c8_s[u, pl.ds(r0 + r * SUB, SUB), :]
                for jj in range(SUB):
                    krow = k_s[u, pl.ds(r0 + r * SUB + jj, 1), :]
                    crow = c8_s[u, pl.ds(r0 + r * SUB + jj, 1), :]
                    pr = qr * krow * jnp.exp(jnp.where(sub >= jj, cr - crow, -jnp.inf))
                    p_s[(r * SUB + jj) * SUB:(r * SUB + jj + 1) * SUB, :] = pr
            rs = jnp.dot(p_s[...].astype(BF16), ones_bd, preferred_element_type=F32)
            od = []
            for r in range(nblk):
                acc = None
                for jj in range(SUB):
                    term = (rs[(r * SUB + jj) * SUB:(r * SUB + jj + 1) * SUB, :]
                            * v_s[u, pl.ds(r0 + r * SUB + jj, 1), :])
                    acc = term if acc is None else acc + term
                od.append(acc)
            o_s[u, rows, :] = o_s[u, rows, :] + (cat(od) - fast)
            return carry

        lax.fori_loop(0, nchunk, body, 0)

    return trip, fix_unit


def _dd_fix_if_unsafe(c8_s, fix_unit, nu, tb, units, epilogue):
    worst = None
    for u in range(nu):
        tot = -c8_s[u, pl.ds(SUB - 1, tb // SUB, stride=SUB), :]
        worst = tot if worst is None else jnp.maximum(worst, tot)

    @pl.when(jnp.max(worst) > DD_SAFE_EXP)
    def _():
        for u in range(nu):
            fix_unit(u, tb // CHUNK)
        for u, t in units:
            epilogue(u, t)


def _dd_scratch(nu, tb, dvp):
    return [pltpu.VMEM((nu, dvp, LANE), F32),
            pltpu.VMEM((nu, tb, LANE), F32), pltpu.VMEM((nu, tb, LANE), F32),
            pltpu.VMEM((nu, tb, dvp), F32),
            pltpu.VMEM((nu, tb, LANE), F32),
            pltpu.VMEM((nu, tb, dvp), F32),
            pltpu.VMEM((nu, tb, LANE), F32),
            pltpu.VMEM((CHUNK * SUB, LANE), F32)]


def _gla_kernel(q_ref, k_ref, v_ref, lr_ref, gate_ref, w2_ref, b2_ref, nw_ref,
                o_ref, state_ref, q_s, k_s, v_s, la_s, o_s, c8_s, p_s, *, tb):
    @pl.when(pl.program_id(1) == 0)
    def _():
        state_ref[...] = jnp.zeros_like(state_ref)

    npair = GLA_HEADS // 2
    rp = DD_NPAR * CHUNK
    trip, fix_unit = _dd_make_trip(q_s, k_s, v_s, la_s, o_s, state_ref, c8_s, p_s, 2)

    def prologue(u, t):
        lo = t * rp
        ls = slice(u * LANE, (u + 1) * LANE)
        vs = slice(u * 2 * LANE, (u + 1) * 2 * LANE)
        q_s[u, lo:lo + rp, :] = q_ref[lo:lo + rp, ls].astype(F32) * (GLA_DK ** -0.5)
        k_s[u, lo:lo + rp, :] = k_ref[lo:lo + rp, ls].astype(F32)
        v_s[u, lo:lo + rp, :] = v_ref[lo:lo + rp, vs].astype(F32)
        z = _mm(lr_ref[lo:lo + rp, :], w2_ref[:, ls]) + b2_ref[:, ls]
        la_s[u, lo:lo + rp, :] = _log_sigmoid(z) * (1.0 / GLA_NORMALIZER)

    def epilogue(u, t):
        lo = t * rp
        for g in range(2):
            hs = slice((2 * u + g) * LANE, (2 * u + g + 1) * LANE)
            o = o_s[u, lo:lo + rp, g * LANE:(g + 1) * LANE]
            oh = o * lax.rsqrt(jnp.mean(o * o, axis=-1, keepdims=True) + RMS_EPS) * nw_ref[...]
            o_ref[lo:lo + rp, hs] = (oh * _silu(gate_ref[lo:lo + rp, hs].astype(F32))).astype(o_ref.dtype)

    units = [(u, t) for u in range(npair) for t in range(tb // rp)]
    _schedule(units, prologue, trip, epilogue)
    _dd_fix_if_unsafe(c8_s, fix_unit, npair, tb, units, epilogue)


def _gla_call(pb, pf, w2p, b2, norm_w, layer, batch, seq):
    tb = min(TB_MIX, seq)
    nb = seq // tb
    t = batch * seq
    kern = functools.partial(_gla_kernel, tb=tb)
    return pl.pallas_call(
        kern,
        grid=(batch, nb),
        in_specs=[pl.BlockSpec((tb, GLA_QK), lambda b, i: (b * nb + i, B_Q // GLA_QK)),
                  pl.BlockSpec((tb, GLA_QK), lambda b, i: (b * nb + i, B_K // GLA_QK)),
                  pl.BlockSpec((tb, GLA_V), lambda b, i: (b * nb + i, B_V // GLA_V)),
                  pl.BlockSpec((tb, LANE), lambda b, i: (b * nb + i, F_LR // LANE)),
                  pl.BlockSpec((tb, GLA_V), lambda b, i: (b * nb + i, B_G // GLA_V)),
                  pl.BlockSpec((None, LANE, GLA_QK), lambda b, i: (layer, 0, 0)),
                  pl.BlockSpec((None, 1, GLA_QK), lambda b, i: (layer, 0, 0)),
                  pl.BlockSpec((None, 1, GLA_DV), lambda b, i: (layer, 0, 0))],
        out_specs=pl.BlockSpec((tb, GLA_V), lambda b, i: (b * nb + i, 0)),
        out_shape=jax.ShapeDtypeStruct((t, GLA_V), BF16),
        scratch_shapes=_dd_scratch(GLA_HEADS // 2, tb, 2 * LANE),
        compiler_params=_cparams(("parallel", "arbitrary")),
        name="gla",
    )(pb, pb, pb, pf, pb, w2p, b2, norm_w)


def _hgrn_kernel(q_ref, f_ref, v_ref, gate_ref, lbl_ref, nw_ref,
                 o_ref, state_ref, q_s, k_s, v_s, la_s, o_s, c8_s, p_s, *, tb, layer):
    @pl.when(pl.program_id(1) == 0)
    def _():
        state_ref[...] = jnp.zeros_like(state_ref)

    logits = lbl_ref[...]
    mx = jnp.max(logits, axis=0, keepdims=True)
    ex = jnp.exp(logits - mx)
    p = ex / jnp.sum(ex, axis=0, keepdims=True)
    acc = p[0:1, :]
    for r in range(1, layer + 1):
        acc = acc + p[r:r + 1, :]
    lb = jnp.clip(acc - p[0:1, :], 0.0, 1.0)
    log_lb = jnp.log(jnp.maximum(lb, LB_FLOOR))
    log_1m = jnp.log(1.0 - lb)

    rp = DD_NPAR * CHUNK
    trip, fix_unit = _dd_make_trip(q_s, k_s, v_s, la_s, o_s, state_ref, c8_s, p_s, 1)

    def prologue(u, t):
        lo = t * rp
        hs = slice(u * LANE, (u + 1) * LANE)
        cf = f_ref[lo:lo + rp, hs]
        second = log_1m[:, hs] + _log_sigmoid(cf)
        llb = log_lb[:, hs]
        la_s[u, lo:lo + rp, :] = jnp.maximum(llb, second) + jnp.log(1.0 + jnp.exp(-jnp.abs(llb - second)))
        k_s[u, lo:lo + rp, :] = (1.0 - lb[:, hs]) * _sigmoid_t(-cf)
        q_s[u, lo:lo + rp, :] = _silu(q_ref[lo:lo + rp, hs].astype(F32)) * (HGRN_EXPAND ** -0.5)
        v_s[u, lo:lo + rp, :] = v_ref[lo:lo + rp, hs].astype(F32)

    def epilogue(u, t):
        lo = t * rp
        hs = slice(u * LANE, (u + 1) * LANE)
        o = o_s[u, lo:lo + rp, :]
        oh = o * lax.rsqrt(jnp.mean(o * o, axis=-1, keepdims=True) + RMS_EPS) * nw_ref[...]
        o_ref[lo:lo + rp, hs] = (oh * _silu(gate_ref[lo:lo + rp, hs].astype(F32))).astype(o_ref.dtype)

    units = [(u, t) for u in range(HGRN_HEADS) for t in range(tb // rp)]
    _schedule(units, prologue, trip, epilogue)
    _dd_fix_if_unsafe(c8_s, fix_unit, HGRN_HEADS, tb, units, epilogue)


def _hgrn_call(pb, pf, lb_logits, norm_w, layer, batch, seq):
    tb = min(TB_MIX, seq)
    nb = seq // tb
    t = batch * seq
    kern = functools.partial(_hgrn_kernel, tb=tb, layer=layer)
    return pl.pallas_call(
        kern,
        grid=(batch, nb),
        in_specs=[pl.BlockSpec((tb, HGRN_QK), lambda b, i: (b * nb + i, C_Q // HGRN_QK)),
                  pl.BlockSpec((tb, HGRN_QK), lambda b, i: (b * nb + i, F_CF // HGRN_QK)),
                  pl.BlockSpec((tb, HGRN_V), lambda b, i: (b * nb + i, C_I // HGRN_V)),
                  pl.BlockSpec((tb, HGRN_V), lambda b, i: (b * nb + i, C_G // HGRN_V)),
                  pl.BlockSpec((DEPTH, HGRN_QK), lambda b, i: (0, 0)),
                  pl.BlockSpec((None, 1, HGRN_DV), lambda b, i: (layer, 0, 0))],
        out_specs=pl.BlockSpec((tb, HGRN_V), lambda b, i: (b * nb + i, 0)),
        out_shape=jax.ShapeDtypeStruct((t, HGRN_V), BF16),
        scratch_shapes=_dd_scratch(HGRN_HEADS, tb, LANE),
        compiler_params=_cparams(("parallel", "arbitrary")),
        name="hgrn",
    )(pb, pf, pb, pb, lb_logits, norm_w)


def _merge_kernel(oa_ref, ob_ref, oc_ref, ma_ref, mb_ref, mc_ref, h_ref,
                  wa_ref, wb_ref, wc_ref, wo_ref, g_ref, b_ref, ox_ref, *, tm):
    y = (_sigmoid_t(ma_ref[...].astype(F32)) * jnp.dot(oa_ref[...], wa_ref[...], preferred_element_type=F32)
         + _sigmoid_t(mb_ref[...].astype(F32)) * jnp.dot(ob_ref[...], wb_ref[...], preferred_element_type=F32)
         + _sigmoid_t(mc_ref[...].astype(F32)) * jnp.dot(oc_ref[...], wc_ref[...], preferred_element_type=F32))
    mix = _mm(y, wo_ref[...])
    _to_token_major(ox_ref, _layer_norm(ALPHA * h_ref[...] + mix, g_ref[...], b_ref[...]), tm)


def _merge_call(o_a, o_b, o_c, pb, h, wa, wb, wc, wo, g, b, layer):
    t, d = h.shape
    tm = min(TM_ROWS, t)

    def row(width):
        return pl.BlockSpec((tm, width), lambda i: (i, 0))

    def wspec(kdim):
        return pl.BlockSpec((None, kdim, d), lambda i: (layer, 0, 0))

    vec = pl.BlockSpec((None, 1, d), lambda i: (layer, 0, 0))
    return pl.pallas_call(
        functools.partial(_merge_kernel, tm=tm),
        grid=(t // tm,),
        in_specs=[row(GDN_V), row(GLA_V), row(HGRN_V),
                  pl.BlockSpec((tm, d), lambda i: (i, M_A // d)),
                  pl.BlockSpec((tm, d), lambda i: (i, M_B // d)),
                  pl.BlockSpec((tm, d), lambda i: (i, M_C // d)),
                  row(d), wspec(GDN_V), wspec(GLA_V), wspec(HGRN_V), wspec(d), vec, vec],
        out_specs=pl.BlockSpec((tm * NPIECE, LANE), lambda i: (i, 0)),
        out_shape=jax.ShapeDtypeStruct((t * NPIECE, LANE), F32),
        compiler_params=_cparams(("parallel",)),
        name="merge",
    )(o_a, o_b, o_c, pb, pb, pb, h, wa, wb, wc, wo, g, b)


def _route_pairs(scores_t, bias_ref):
    s = [scores_t[e:e + 1, :] for e in range(N_EXPERTS)]
    sel = [s[e] + bias_ref[e:e + 1, 0:1] for e in range(N_EXPERTS)]
    gscore = []
    for g in range(N_GROUPS):
        a, b, c, d = sel[4 * g:4 * g + 4]
        hi1, lo1 = jnp.maximum(a, b), jnp.minimum(a, b)
        hi2, lo2 = jnp.maximum(c, d), jnp.minimum(c, d)
        top1 = jnp.maximum(hi1, hi2)
        top2 = jnp.maximum(jnp.minimum(hi1, hi2), jnp.maximum(lo1, lo2))
        gscore.append(top1 + top2)
    best = gscore[0]
    gidx = jnp.zeros_like(best, dtype=jnp.int32)
    for g in range(1, N_GROUPS):
        take = gscore[g] > best
        best = jnp.where(take, gscore[g], best)
        gidx = jnp.where(take, g, gidx)
    ing, raw = [], []
    for kk in range(EXPERTS_PER_GROUP):
        vs, vr = sel[kk], s[kk]
        for g in range(1, N_GROUPS):
            pick = gidx == g
            vs = jnp.where(pick, sel[4 * g + kk], vs)
            vr = jnp.where(pick, s[4 * g + kk], vr)
        ing.append(vs)
        raw.append(vr)
    b1 = ing[0]
    i1 = jnp.zeros_like(gidx)
    for kk in range(1, EXPERTS_PER_GROUP):
        take = ing[kk] > b1
        b1 = jnp.where(take, ing[kk], b1)
        i1 = jnp.where(take, kk, i1)
    neg = jnp.full_like(b1, -jnp.inf)
    b2 = neg
    i2 = jnp.zeros_like(gidx)
    for kk in range(EXPERTS_PER_GROUP):
        cand = jnp.where(i1 == kk, neg, ing[kk])
        take = cand > b2
        b2 = jnp.where(take, cand, b2)
        i2 = jnp.where(take, kk, i2)
    w1 = raw[0]
    w2 = raw[0]
    for kk in range(1, EXPERTS_PER_GROUP):
        w1 = jnp.where(i1 == kk, raw[kk], w1)
        w2 = jnp.where(i2 == kk, raw[kk], w2)
    tot = w1 + w2
    w1 = w1 / tot
    w2 = w2 / tot
    first_lower = i1 < i2
    lo = jnp.where(first_lower, i1, i2)
    hi = jnp.where(first_lower, i2, i1)
    pidx = jnp.where(lo == 0, hi - 1, jnp.where(lo == 1, hi + 1, 5))
    bkt = gidx * 6 + pidx
    return bkt, jnp.where(first_lower, w1, w2), jnp.where(first_lower, w2, w1)


def _moe_route_kernel(hx_ref, wr_ref, rb_ref, bkt_ref, rank_ref, wab_ref, cnt_ref, carry_ref, *, tm):
    @pl.when(pl.program_id(0) == 0)
    def _():
        carry_ref[...] = jnp.zeros_like(carry_ref)

    logits_t = _mm_nt_f32(wr_ref[...], _from_token_major(hx_ref, tm))
    bkt, wa, wb = _route_pairs(_sigmoid(logits_t), rb_ref)
    sub = lax.broadcasted_iota(jnp.int32, (32, tm), 0)
    oh = jnp.where(sub == bkt, 1.0, 0.0)
    r = lax.broadcasted_iota(jnp.int32, (tm, tm), 0)
    c = lax.broadcasted_iota(jnp.int32, (tm, tm), 1)
    earlier = jnp.where(r < c, 1.0, 0.0).astype(BF16)
    before = jnp.dot(oh.astype(BF16), earlier, preferred_element_type=F32)
    carry = carry_ref[...]
    rank = jnp.sum(oh * (before + carry[:, 0:1]), axis=0, keepdims=True)
    carry = carry + jnp.sum(oh, axis=1, keepdims=True)
    carry_ref[...] = carry
    cnt_ref[...] = carry
    bkt_ref[...] = bkt
    rank_ref[...] = rank.astype(jnp.int32)
    pad = jnp.zeros((LANE - 2, tm), F32)
    wab_ref[...] = jnp.transpose(jnp.concatenate([wa, wb, pad], axis=0))


def _moe_tables_kernel(cnt_ref, bkt_ref, rank_ref, pos_ref, tabs_ref, *, t, tr):
    cnt = cnt_ref[...]
    sz = jnp.floor((cnt + (tr - 1)) * (1.0 / tr)) * tr
    r = lax.broadcasted_iota(jnp.int32, (32, 32), 0)
    c = lax.broadcasted_iota(jnp.int32, (32, 32), 1)
    start = jnp.dot(jnp.where(c < r, 1.0, 0.0), sz, preferred_element_type=F32, precision=HIGHEST)
    end = start + sz
    sub = lax.broadcasted_iota(jnp.int32, (32, t), 0)
    pos = jnp.sum(jnp.where(sub == bkt_ref[...], start[:, 0:1], 0.0), axis=0, keepdims=True)
    pos_ref[...] = pos.astype(jnp.int32) + rank_ref[...]
    brow = lax.broadcasted_iota(jnp.int32, (32, LANE), 0)
    tile0 = lax.broadcasted_iota(jnp.int32, (32, LANE), 1).astype(F32) * tr
    tbk = jnp.sum(jnp.where((brow < MOE_NBKT) & (end <= tile0), 1, 0), axis=0, keepdims=True)
    tbk = jnp.minimum(tbk, MOE_NBKT - 1)
    total = end[MOE_NBKT - 1:MOE_NBKT, :]
    valid = jnp.where(tile0[0:1, :] < total, 1, 0)
    g = jnp.where(tbk >= 6, 1, 0) + jnp.where(tbk >= 12, 1, 0) + jnp.where(tbk >= 18, 1, 0)
    p = tbk - 6 * g
    ge3 = jnp.where(p >= 3, 1, 0)
    ge5 = jnp.where(p >= 5, 1, 0)
    ea = 4 * g + ge3 + ge5
    eb = 4 * g + p + 1 - 2 * ge3 - ge5
    zero = jnp.zeros((SUBLANE - 3, LANE), jnp.int32)
    tabs_ref[...] = jnp.concatenate([ea, eb, valid, zero], axis=0)


def _moe_group_kernel(ea_ref, eb_ref, vd_ref, xs_ref, ws_ref, wga_ref, wua_ref, wda_ref,
                      wgb_ref, wub_ref, wdb_ref, ys_ref, wg_s, wu_s, wd_s, *, tr):
    j = pl.program_id(0)
    prev = jnp.maximum(j - 1, 0)

    @pl.when((j == 0) | (ea_ref[j] != ea_ref[prev]))
    def _():
        wg_s[0] = wga_ref[...].astype(BF16)
        wu_s[0] = wua_ref[...].astype(BF16)
        wd_s[0] = wda_ref[...].astype(BF16)

    @pl.when((j == 0) | (eb_ref[j] != eb_ref[prev]))
    def _():
        wg_s[1] = wgb_ref[...].astype(BF16)
        wu_s[1] = wub_ref[...].astype(BF16)
        wd_s[1] = wdb_ref[...].astype(BF16)

    @pl.when(vd_ref[j] > 0)
    def _():
        x = _from_token_major(xs_ref, tr).astype(BF16)
        w = ws_ref[...]

        def ffn(s, cw):
            hg = jnp.dot(x, wg_s[s], preferred_element_type=F32)
            hu = jnp.dot(x, wu_s[s], preferred_element_type=F32)
            hid = _silu(hg) * hu * cw
            return jnp.dot(hid.astype(BF16), wd_s[s], preferred_element_type=F32)

        y = ffn(0, w[:, 0:1]) + ffn(1, w[:, 1:2])
        _to_token_major(ys_ref, y, tr)


def _moe_ln_kernel(hx_ref, y_ref, g_ref, b_ref, o_ref, ob_ref, *, tm):
    out = _layer_norm(ALPHA * _from_token_major(hx_ref, tm) + _from_token_major(y_ref, tm),
                      g_ref[...], b_ref[...])
    o_ref[...] = out
    ob_ref[...] = out.astype(BF16)


def _sc_mesh_info():
    info = plsc.get_sparse_core_info()
    mesh = plsc.VectorSubcoreMesh(core_axis_name="c", subcore_axis_name="s")
    return mesh, info.num_cores, info.num_subcores


def _sc_scatter_rows(x3, w2, pos, n_rows):
    t = x3.shape[0]
    mesh, nc, ns = _sc_mesh_info()
    per_w = t // (nc * ns)

    @functools.partial(
        pl.kernel, mesh=mesh,
        out_type=[jax.ShapeDtypeStruct((n_rows,) + x3.shape[1:], x3.dtype),
                  jax.ShapeDtypeStruct((n_rows,) + w2.shape[1:], w2.dtype)],
        scratch_types=[pltpu.VMEM((SC_CH,), jnp.int32), pltpu.VMEM((SC_CH,) + x3.shape[1:], x3.dtype),
                       pltpu.VMEM((SC_CH,) + w2.shape[1:], w2.dtype),
                       pltpu.SemaphoreType.DMA, pltpu.SemaphoreType.DMA])
    def k(x_hbm, w_hbm, idx_hbm, ox_hbm, ow_hbm, idx_v, rows_v, wrows_v, sem_x, sem_w):
        base = (lax.axis_index("s") * nc + lax.axis_index("c")) * per_w

        @pl.loop(0, per_w // SC_CH)
        def _(j):
            off = base + j * SC_CH
            pltpu.sync_copy(idx_hbm.at[pl.ds(off, SC_CH)], idx_v)
            pltpu.sync_copy(x_hbm.at[pl.ds(off, SC_CH)], rows_v)
            pltpu.sync_copy(w_hbm.at[pl.ds(off, SC_CH)], wrows_v)
            cx = pltpu.async_copy(rows_v, ox_hbm.at[idx_v], sem_x)
            cw = pltpu.async_copy(wrows_v, ow_hbm.at[idx_v], sem_w)
            cx.wait()
            cw.wait()

    return k(x3, w2, pos)


def _sc_gather_rows(y3, pos):
    t = pos.shape[0]
    mesh, nc, ns = _sc_mesh_info()
    per_w = t // (nc * ns)

    @functools.partial(
        pl.kernel, mesh=mesh,
        out_type=jax.ShapeDtypeStruct((t,) + y3.shape[1:], y3.dtype),
        scratch_types=[pltpu.VMEM((SC_CH,), jnp.int32), pltpu.VMEM((SC_CH,) + y3.shape[1:], y3.dtype),
                       pltpu.SemaphoreType.DMA])
    def k(y_hbm, idx_hbm, o_hbm, idx_v, rows_v, sem):
        base = (lax.axis_index("s") * nc + lax.axis_index("c")) * per_w

        @pl.loop(0, per_w // SC_CH)
        def _(j):
            off = base + j * SC_CH
            pltpu.sync_copy(idx_hbm.at[pl.ds(off, SC_CH)], idx_v)
            pltpu.async_copy(y_hbm.at[idx_v], rows_v, sem).wait()
            pltpu.sync_copy(rows_v, o_hbm.at[pl.ds(off, SC_CH)])

    return k(y3, pos)


def _moe_sparse(hx, wr_t, rbias, wg, wu, wd, g, b, layer):
    t = hx.shape[0] // NPIECE
    d = D_MODEL
    tm = min(TM_MOE, t)
    tr = MOE_TR
    nt = t // tr + MOE_NBKT
    n_rows = nt * tr
    row1 = pl.BlockSpec((1, tm), lambda i: (0, i))
    rows_tm = pl.BlockSpec((tm * NPIECE, LANE), lambda i: (i, 0))
    bkt, rank, wab, cnt = pl.pallas_call(
        functools.partial(_moe_route_kernel, tm=tm),
        grid=(t // tm,),
        in_specs=[rows_tm,
                  pl.BlockSpec((N_EXPERTS, d), lambda i: (0, 0)),
                  pl.BlockSpec((N_EXPERTS, LANE), lambda i: (0, 0))],
        out_specs=[row1, row1, pl.BlockSpec((tm, LANE), lambda i: (i, 0)),
                   pl.BlockSpec((32, LANE), lambda i: (0, 0))],
        out_shape=[jax.ShapeDtypeStruct((1, t), jnp.int32), jax.ShapeDtypeStruct((1, t), jnp.int32),
                   jax.ShapeDtypeStruct((t, LANE), F32), jax.ShapeDtypeStruct((32, LANE), F32)],
        scratch_shapes=[pltpu.VMEM((32, LANE), F32)],
        compiler_params=_cparams(("arbitrary",)),
        name="moe_route",
    )(hx, wr_t, rbias)
    pos, tabs = pl.pallas_call(
        functools.partial(_moe_tables_kernel, t=t, tr=tr),
        out_shape=[jax.ShapeDtypeStruct((1, t), jnp.int32), jax.ShapeDtypeStruct((SUBLANE, LANE), jnp.int32)],
        compiler_params=pltpu.CompilerParams(vmem_limit_bytes=VMEM_LIMIT),
        name="moe_tables",
    )(cnt, bkt, rank)
    pos = pos.reshape(t)
    xs3, ws = _sc_scatter_rows(hx.reshape(t, NPIECE, LANE), wab, pos, n_rows)

    def wspec(which, shape):
        if which == 0:
            return pl.BlockSpec((None, None) + shape, lambda j, ea, eb, vd: (layer, ea[j], 0, 0))
        return pl.BlockSpec((None, None) + shape, lambda j, ea, eb, vd: (layer, eb[j], 0, 0))

    ys = pl.pallas_call(
        functools.partial(_moe_group_kernel, tr=tr),
        grid_spec=pltpu.PrefetchScalarGridSpec(
            num_scalar_prefetch=3,
            grid=(nt,),
            in_specs=[pl.BlockSpec((tr * NPIECE, LANE), lambda j, ea, eb, vd: (j, 0)),
                      pl.BlockSpec((tr, LANE), lambda j, ea, eb, vd: (j, 0)),
                      wspec(0, (d, D_FF)), wspec(0, (d, D_FF)), wspec(0, (D_FF, d)),
                      wspec(1, (d, D_FF)), wspec(1, (d, D_FF)), wspec(1, (D_FF, d))],
            out_specs=pl.BlockSpec((tr * NPIECE, LANE), lambda j, ea, eb, vd: (j, 0)),
            scratch_shapes=[pltpu.VMEM((2, d, D_FF), BF16), pltpu.VMEM((2, d, D_FF), BF16),
                            pltpu.VMEM((2, D_FF, d), BF16)]),
        out_shape=jax.ShapeDtypeStruct((n_rows * NPIECE, LANE), F32),
        compiler_params=_cparams(("arbitrary",)),
        name="moe_experts",
    )(tabs[0, :nt], tabs[1, :nt], tabs[2, :nt], xs3.reshape(n_rows * NPIECE, LANE), ws,
      wg, wu, wd, wg, wu, wd)
    y3 = _sc_gather_rows(ys.reshape(n_rows, NPIECE, LANE), pos)
    vec = pl.BlockSpec((None, 1, d), lambda i: (layer, 0, 0))
    row = pl.BlockSpec((tm, d), lambda i: (i, 0))
    return pl.pallas_call(
        functools.partial(_moe_ln_kernel, tm=tm),
        grid=(t // tm,),
        in_specs=[rows_tm, rows_tm, vec, vec],
        out_specs=[row, row],
        out_shape=[jax.ShapeDtypeStruct((t, d), F32), jax.ShapeDtypeStruct((t, d), BF16)],
        compiler_params=_cparams(("parallel",)),
        name="moe_ln",
    )(hx, y3.reshape(t * NPIECE, LANE), g, b)


def _pack_w_in(w_in):
    wt = jnp.swapaxes(w_in, 1, 2)
    (a_q, a_k, a_v, a_beta, a_dt, a_g, b_q, b_k, b_v, b_lr, b_g,
     c_q, c_f, c_i, c_g, m_a, m_b, m_c) = jnp.split(wt, SPLIT_POINTS, axis=1)
    depth, _, d = wt.shape
    a_s = jnp.concatenate([a_beta, a_dt, jnp.zeros((depth, LANE - 2 * GDN_HEADS, d), wt.dtype)], 1)
    b_lrp = jnp.concatenate([b_lr, jnp.zeros((depth, LANE - GLA_RANK, d), wt.dtype)], 1)
    wb = jnp.concatenate([m_a, m_b, m_c, a_q, a_k, a_v, a_g, b_q, b_k, b_v, b_g, c_q, c_i, c_g], 1).astype(BF16)
    wf = jnp.concatenate([c_f, a_s, b_lrp], 1).astype(BF16)
    assert wb.shape[1] == NPB and wf.shape[1] == NPF
    return wb, wf


def _prepare(w_in, gdn_conv, gdn_a_log, gdn_dt_bias, gdn_norm, gla_w2, gla_b2, gla_norm, hgrn_lb_logits,
             hgrn_norm, w_br_a, w_br_b, w_br_c, w_out, ln1_g, ln1_b, w_router, router_bias, w_gate, w_up,
             w_down, ln2_g, ln2_b):
    depth = w_in.shape[0]
    d = w_out.shape[-1]
    w_pb, w_pf = _pack_w_in(w_in)
    return dict(
        w_pb=w_pb, w_pf=w_pf,
        gdn_conv=gdn_conv,
        gdn_par=jnp.pad(jnp.stack([gdn_a_log, gdn_dt_bias], axis=1),
                        ((0, 0), (0, SUBLANE - 2), (GDN_HEADS, LANE - 2 * GDN_HEADS))),
        gdn_norm=gdn_norm.reshape(depth, 1, GDN_DV),
        w2p=jnp.concatenate([gla_w2, jnp.zeros((depth, LANE - GLA_RANK, GLA_QK), gla_w2.dtype)], axis=1),
        gla_b2=gla_b2.reshape(depth, 1, GLA_QK),
        gla_norm=gla_norm.reshape(depth, 1, GLA_DV),
        lb_logits=hgrn_lb_logits,
        hgrn_norm=hgrn_norm.reshape(depth, 1, HGRN_DV),
        wa=w_br_a.astype(BF16), wb=w_br_b.astype(BF16), wc=w_br_c.astype(BF16), wo=w_out.astype(BF16),
        ln1_g=ln1_g.reshape(depth, 1, d), ln1_b=ln1_b.reshape(depth, 1, d),
        wr_t=jnp.transpose(w_router),
        rbias=jnp.broadcast_to(router_bias[:, None], (N_EXPERTS, LANE)),
        wg=w_gate, wu=w_up, wd=w_down,
        ln2_g=ln2_g.reshape(depth, 1, d), ln2_b=ln2_b.reshape(depth, 1, d),
    )


def _mixer_block(h, hb, p, layer, batch, seq):
    pb = _inproj_call(hb, p["w_pb"], layer, BF16, TM_PROJ, TN_PROJ, "inproj_b")
    pf = _inproj_call(hb, p["w_pf"], layer, F32, TM_PROJ // 2, NPF, "inproj_f")
    o_a = _gdn_call(pb, pf, p["gdn_conv"], p["gdn_par"], p["gdn_norm"], layer, batch, seq)
    o_b = _gla_call(pb, pf, p["w2p"], p["gla_b2"], p["gla_norm"], layer, batch, seq)
    o_c = _hgrn_call(pb, pf, p["lb_logits"], p["hgrn_norm"], layer, batch, seq)
    return _merge_call(o_a, o_b, o_c, pb, h, p["wa"], p["wb"], p["wc"], p["wo"], p["ln1_g"], p["ln1_b"], layer)


def _ffn_block(hx, p, layer):
    return _moe_sparse(hx, p["wr_t"], p["rbias"], p["wg"], p["wu"], p["wd"], p["ln2_g"], p["ln2_b"], layer)


def kernel(x, ln0_g, ln0_b, w_in, gdn_conv, gdn_a_log, gdn_dt_bias, gdn_norm, gla_w2, gla_b2, gla_norm,
           hgrn_lb_logits, hgrn_norm, w_br_a, w_br_b, w_br_c, w_out, ln1_g, ln1_b, w_router, router_bias,
           w_gate, w_up, w_down, ln2_g, ln2_b):
    batch, seq, d = x.shape
    p = _prepare(w_in, gdn_conv, gdn_a_log, gdn_dt_bias, gdn_norm, gla_w2, gla_b2, gla_norm, hgrn_lb_logits,
                 hgrn_norm, w_br_a, w_br_b, w_br_c, w_out, ln1_g, ln1_b, w_router, router_bias, w_gate, w_up,
                 w_down, ln2_g, ln2_b)
    h, hb = _ln_call(x.reshape(batch * seq, d), ln0_g, ln0_b)
    for layer in range(w_in.shape[0]):
        hx = _mixer_block(h, hb, p, layer, batch, seq)
        h, hb = _ffn_block(hx, p, layer)
    return h.reshape(batch, seq, d)
```

```python
import functools

import numpy as np
import jax
import jax.numpy as jnp
from jax import lax
from jax.experimental import pallas as pl
from jax.experimental.pallas import tpu as pltpu
from jax.experimental.pallas import tpu_sc as plsc

F32 = jnp.float32
BF16 = jnp.bfloat16
HIGHEST = lax.Precision.HIGHEST

D_MODEL = 1024
DEPTH = 4
CHUNK = 64
GDN_HEADS, GDN_DK, GDN_DV, CONV_W = 4, 128, 128, 4
GLA_HEADS, GLA_DK, GLA_DV, GLA_RANK, GLA_NORMALIZER = 4, 64, 128, 16, 16.0
HGRN_HEADS, HGRN_EXPAND, HGRN_DV = 4, 128, 128
LB_FLOOR = 1e-30
N_EXPERTS, N_GROUPS, TOP_K, D_FF = 16, 4, 2, 256
EXPERTS_PER_GROUP = N_EXPERTS // N_GROUPS
ALPHA = (2.0 * DEPTH) ** 0.25
LN_EPS = 1e-5
RMS_EPS = 1e-6

GDN_QK = GDN_HEADS * GDN_DK
GDN_V = GDN_HEADS * GDN_DV
GLA_QK = GLA_HEADS * GLA_DK
GLA_V = GLA_HEADS * GLA_DV
HGRN_QK = HGRN_HEADS * HGRN_EXPAND
HGRN_V = HGRN_HEADS * HGRN_DV
SPLIT_SIZES = (GDN_QK, GDN_QK, GDN_V, GDN_HEADS, GDN_HEADS, GDN_V,
               GLA_QK, GLA_QK, GLA_V, GLA_RANK, GLA_V,
               HGRN_QK, HGRN_QK, HGRN_V, HGRN_V,
               D_MODEL, D_MODEL, D_MODEL)
SPLIT_POINTS = tuple(int(v) for v in np.cumsum(SPLIT_SIZES)[:-1])

LANE = 128
SUBLANE = 8
VMEM_LIMIT = 48 * 1024 * 1024

M_A, M_B, M_C = 0, 1024, 2048
A_Q, A_K, A_V, A_G = 3072, 3584, 4096, 4608
B_Q, B_K, B_V, B_G = 5120, 5376, 5632, 6144
C_Q, C_I, C_G = 6656, 7168, 7680
NPB = 8192
F_CF, F_AS, F_LR = 0, 512, 640
NPF = 768

TB_MIX = 1024
TM_ROWS = 512
TM_MOE = 1024
TM_PROJ, TN_PROJ = 2048, 2048
SUB = 8
DD_NPAR = 8
DD_SAFE_EXP = 60.0
GDN_NPAR = 8
MOE_TR = 512
MOE_NBKT = N_GROUPS * 6
NPIECE = D_MODEL // LANE
SC_CH = 64


def _cparams(sem):
    return pltpu.CompilerParams(dimension_semantics=sem, vmem_limit_bytes=VMEM_LIMIT)


def _mm(a, b):
    return jnp.dot(a.astype(BF16), b.astype(BF16), preferred_element_type=F32)


def _mm_nt(a, b):
    return lax.dot_general(a.astype(BF16), b.astype(BF16), (((1,), (1,)), ((), ())),
                           preferred_element_type=F32)


def _mm_tn(a, b):
    return lax.dot_general(a.astype(BF16), b.astype(BF16), (((0,), (0,)), ((), ())),
                           preferred_element_type=F32)


def _mm_nt_f32(a, b):
    a_hi = a.astype(BF16)
    a_mid = (a - a_hi.astype(F32)).astype(BF16)
    b_hi = b.astype(BF16)
    b_mid = (b - b_hi.astype(F32)).astype(BF16)
    dn = (((1,), (1,)), ((), ()))
    return (lax.dot_general(a_hi, b_hi, dn, preferred_element_type=F32)
            + lax.dot_general(a_hi, b_mid, dn, preferred_element_type=F32)
            + lax.dot_general(a_mid, b_hi, dn, preferred_element_type=F32))


def _split3(x):
    hi = x.astype(BF16)
    r = x - hi.astype(F32)
    mid = r.astype(BF16)
    lo = (r - mid.astype(F32)).astype(BF16)
    return hi, mid, lo


def _mm_01(m01, x):
    hi, mid, lo = _split3(x)
    return (jnp.dot(m01, hi, preferred_element_type=F32) + jnp.dot(m01, mid, preferred_element_type=F32)
            + jnp.dot(m01, lo, preferred_element_type=F32))


def _sigmoid(x):
    return 1.0 / (1.0 + jnp.exp(-x))


def _sigmoid_t(x):
    return 0.5 * jnp.tanh(0.5 * x) + 0.5


def _silu(x):
    h = 0.5 * x
    return h + h * jnp.tanh(h)


def _softplus(x):
    return jnp.maximum(x, 0.0) + jnp.log(1.0 + jnp.exp(-jnp.abs(x)))


def _log_sigmoid(x):
    return -_softplus(-x)


def _layer_norm(x, g, b):
    mu = jnp.mean(x, axis=-1, keepdims=True)
    xc = x - mu
    var = jnp.mean(xc * xc, axis=-1, keepdims=True)
    return xc * lax.rsqrt(var + LN_EPS) * g + b


def _to_token_major(ref, x, rows):
    for j in range(NPIECE):
        ref[pl.ds(j, rows, stride=NPIECE), :] = x[:, j * LANE:(j + 1) * LANE]


def _from_token_major(ref, rows):
    return jnp.concatenate([ref[pl.ds(j, rows, stride=NPIECE), :] for j in range(NPIECE)], axis=1)


def _ln_kernel(x_ref, g_ref, b_ref, o_ref, ob_ref):
    y = _layer_norm(x_ref[...], g_ref[...], b_ref[...])
    o_ref[...] = y
    ob_ref[...] = y.astype(BF16)


def _ln_call(x, g, b):
    t, d = x.shape
    tm = min(TM_ROWS, t)
    return pl.pallas_call(
        _ln_kernel,
        grid=(t // tm,),
        in_specs=[pl.BlockSpec((tm, d), lambda i: (i, 0)),
                  pl.BlockSpec((1, d), lambda i: (0, 0)),
                  pl.BlockSpec((1, d), lambda i: (0, 0))],
        out_specs=[pl.BlockSpec((tm, d), lambda i: (i, 0)), pl.BlockSpec((tm, d), lambda i: (i, 0))],
        out_shape=[jax.ShapeDtypeStruct((t, d), F32), jax.ShapeDtypeStruct((t, d), BF16)],
        compiler_params=_cparams(("parallel",)),
        name="ln0",
    )(x, g.reshape(1, d), b.reshape(1, d))


def _inproj_kernel(x_ref, wt_ref, o_ref):
    o_ref[...] = lax.dot_general(x_ref[...], wt_ref[...], (((1,), (1,)), ((), ())),
                                 preferred_element_type=F32).astype(o_ref.dtype)


def _inproj_call(hb, w, layer, out_dtype, tm, tn, name):
    t, d = hb.shape
    tm = min(tm, t)
    n = w.shape[-2]
    return pl.pallas_call(
        _inproj_kernel,
        grid=(n // tn, t // tm),
        in_specs=[pl.BlockSpec((tm, d), lambda j, i: (i, 0)),
                  pl.BlockSpec((None, tn, d), lambda j, i: (layer, j, 0))],
        out_specs=pl.BlockSpec((tm, tn), lambda j, i: (i, j)),
        out_shape=jax.ShapeDtypeStruct((t, n), out_dtype),
        compiler_params=_cparams(("parallel", "parallel")),
        name=name,
    )(hb, w)


def _schedule(units, prologue, trip, epilogue):
    prologue(*units[0])
    for n, u in enumerate(units):
        if n + 1 < len(units):
            prologue(*units[n + 1])
        trip(*u, n % 2)
        epilogue(*u)


def _gdn_kernel(qkv_ref, s_ref, gate_ref, cw_ref, par_ref, nw_ref, o_ref,
                state_ref, tail_ref, xbuf_ref, q_s, k_s, v_s, cumb_s, betab_s, cumrow_s, o_s, *, tb):
    nc = tb // CHUNK
    nh = GDN_HEADS
    off0 = SUBLANE - (CONV_W - 1)

    @pl.when(pl.program_id(1) == 0)
    def _():
        state_ref[...] = jnp.zeros_like(state_ref)
        tail_ref[...] = jnp.zeros_like(tail_ref)

    row = lax.broadcasted_iota(jnp.int32, (CHUNK, CHUNK), 0)
    col = lax.broadcasted_iota(jnp.int32, (CHUNK, CHUNK), 1)
    incl = col <= row
    strict = col < row
    tri = jnp.where(incl, 1.0, 0.0).astype(BF16)
    eye = jnp.where(col == row, 1.0, 0.0).astype(F32)
    npar = GDN_NPAR
    rp = npar * CHUNK
    xbuf_ref[0:SUBLANE, :] = tail_ref[...]

    def prologue(t):
        lo = t * rp
        xbuf_ref[SUBLANE + lo:SUBLANE + lo + rp, :] = qkv_ref[lo:lo + rp, :].astype(F32)
        for j in range(3 * nh):
            cs = slice(j * LANE, (j + 1) * LANE)
            y = xbuf_ref[off0 + lo:off0 + lo + rp, cs] * cw_ref[0:1, cs]
            for kk in range(1, CONV_W):
                y = y + xbuf_ref[off0 + kk + lo:off0 + kk + lo + rp, cs] * cw_ref[kk:kk + 1, cs]
            y = _silu(y)
            if j < nh:
                q_s[j, lo:lo + rp, :] = (y * lax.rsqrt(jnp.sum(y * y, axis=-1, keepdims=True) + RMS_EPS)
                                         * (GDN_DK ** -0.5))
            elif j < 2 * nh:
                k_s[j - nh, lo:lo + rp, :] = y * lax.rsqrt(jnp.sum(y * y, axis=-1, keepdims=True) + RMS_EPS)
            else:
                v_s[j - 2 * nh, lo:lo + rp, :] = y
        sc = s_ref[lo:lo + rp, :]
        beta_all = _sigmoid_t(sc)
        g_all = -jnp.exp(par_ref[0:1, :]) * _softplus(sc + par_ref[1:2, :])
        cum_all = jnp.concatenate([_mm_01(tri, g_all[c * CHUNK:(c + 1) * CHUNK, :]) for c in range(npar)], axis=0)
        cum_t = jnp.transpose(cum_all)
        for c in range(npar):
            cumrow_s[t * npar + c] = cum_t[0:SUBLANE, c * CHUNK:(c + 1) * CHUNK]
        for h in range(nh):
            cumb_s[h, lo:lo + rp, :] = jnp.broadcast_to(cum_all[:, nh + h:nh + h + 1], (rp, LANE))
            betab_s[h, lo:lo + rp, :] = jnp.broadcast_to(beta_all[:, h:h + 1], (rp, LANE))

    def trip(cp, _slot):
        chains = [(cp * npar + cc, h) for cc in range(npar) for h in range(nh)]
        r0s = [c * CHUNK for c, _ in chains]
        qc = [q_s[h, r0:r0 + CHUNK, :] for (_, h), r0 in zip(chains, r0s)]
        kc = [k_s[h, r0:r0 + CHUNK, :] for (_, h), r0 in zip(chains, r0s)]
        vc = [v_s[h, r0:r0 + CHUNK, :] for (_, h), r0 in zip(chains, r0s)]
        cumc = [cumb_s[h, r0:r0 + CHUNK, :] for (_, h), r0 in zip(chains, r0s)]
        bc = [betab_s[h, r0:r0 + CHUNK, :] for (_, h), r0 in zip(chains, r0s)]
        n = len(chains)
        kk = [_mm_nt(kc[i], kc[i]) for i in range(n)]
        qk = [_mm_nt(qc[i], kc[i]) for i in range(n)]
        decay = []
        for i, (c, h) in enumerate(chains):
            diff = cumc[i][:, 0:CHUNK] - cumrow_s[c][nh + h:nh + h + 1, :]
            decay.append(jnp.where(incl, jnp.exp(jnp.where(incl, diff, 0.0)), 0.0))
        a = [jnp.where(strict, bc[i][:, 0:CHUNK] * kk[i] * decay[i], 0.0) for i in range(n)]
        x = [eye - a[i] for i in range(n)]
        p = [_mm(a[i], a[i]) for i in range(n)]
        for it in range(5):
            x = [x[i] + _mm(x[i], p[i]) for i in range(n)]
            if it < 4:
                p = [_mm(p[i], p[i]) for i in range(n)]
        ecum = [jnp.exp(cumc[i]) for i in range(n)]
        sol = [_mm(x[i], jnp.concatenate([vc[i] * bc[i], kc[i] * (bc[i] * ecum[i])], axis=1)) for i in range(n)]
        attn = [qk[i] * decay[i] for i in range(n)]
        cum_last = [cumc[i][CHUNK - 1:CHUNK, :] for i in range(n)]
        k_state = [kc[i] * jnp.exp(cum_last[i] - cumc[i]) for i in range(n)]
        wqi = [jnp.concatenate([sol[i][:, GDN_DV:GDN_DV + GDN_DK], qc[i] * ecum[i]], axis=0) for i in range(n)]
        for cc in range(npar):
            idx = [cc * nh + h for h in range(nh)]
            s = [state_ref[h] for h in range(nh)]
            wq = [_mm(wqi[i], s[h]) for h, i in enumerate(idx)]
            v_new = [sol[i][:, 0:GDN_DV] - wq[h][0:CHUNK, :] for h, i in enumerate(idx)]
            av = [_mm(attn[i], v_new[h]) for h, i in enumerate(idx)]
            upd = [_mm_tn(k_state[i], v_new[h]) for h, i in enumerate(idx)]
            for h, i in enumerate(idx):
                state_ref[h] = s[h] * jnp.exp(cum_last[i]) + upd[h]
                o_s[r0s[i]:r0s[i] + CHUNK, h * LANE:(h + 1) * LANE] = wq[h][CHUNK:2 * CHUNK, :] + av[h]

    def epilogue(t):
        lo = t * rp
        for h in range(nh):
            hs = slice(h * LANE, (h + 1) * LANE)
            o = o_s[lo:lo + rp, hs]
            oh = o * lax.rsqrt(jnp.mean(o * o, axis=-1, keepdims=True) + RMS_EPS) * nw_ref[...]
            o_ref[lo:lo + rp, hs] = (oh * _silu(gate_ref[lo:lo + rp, hs].astype(F32))).astype(o_ref.dtype)

    _schedule([(t,) for t in range(nc // npar)], prologue, trip, epilogue)
    tail_ref[...] = xbuf_ref[tb:tb + SUBLANE, :]


def _gdn_call(pb, pf, conv_w, par, norm_w, layer, batch, seq):
    tb = min(TB_MIX, seq)
    nb = seq // tb
    nc = tb // CHUNK
    t = batch * seq
    wq = 2 * GDN_QK + GDN_V
    kern = functools.partial(_gdn_kernel, tb=tb)
    return pl.pallas_call(
        kern,
        grid=(batch, nb),
        in_specs=[pl.BlockSpec((tb, wq), lambda b, i: (b * nb + i, A_Q // wq)),
                  pl.BlockSpec((tb, LANE), lambda b, i: (b * nb + i, F_AS // LANE)),
                  pl.BlockSpec((tb, GDN_V), lambda b, i: (b * nb + i, A_G // GDN_V)),
                  pl.BlockSpec((None, CONV_W, wq), lambda b, i: (layer, 0, 0)),
                  pl.BlockSpec((None, SUBLANE, LANE), lambda b, i: (layer, 0, 0)),
                  pl.BlockSpec((None, 1, GDN_DV), lambda b, i: (layer, 0, 0))],
        out_specs=pl.BlockSpec((tb, GDN_V), lambda b, i: (b * nb + i, 0)),
        out_shape=jax.ShapeDtypeStruct((t, GDN_V), BF16),
        scratch_shapes=[pltpu.VMEM((GDN_HEADS, GDN_DK, GDN_DV), F32),
                        pltpu.VMEM((SUBLANE, wq), F32),
                        pltpu.VMEM((tb + SUBLANE, wq), F32),
                        pltpu.VMEM((GDN_HEADS, tb, LANE), F32),
                        pltpu.VMEM((GDN_HEADS, tb, LANE), F32),
                        pltpu.VMEM((GDN_HEADS, tb, LANE), F32),
                        pltpu.VMEM((GDN_HEADS, tb, LANE), F32),
                        pltpu.VMEM((GDN_HEADS, tb, LANE), F32),
                        pltpu.VMEM((nc, SUBLANE, CHUNK), F32),
                        pltpu.VMEM((tb, GDN_V), F32)],
        compiler_params=_cparams(("parallel", "arbitrary")),
        name="gdn",
    )(pb, pf, pb, conv_w, par, norm_w)


def _dd_make_trip(q_s, k_s, v_s, la_s, o_s, state_ref, c8_s, p_s, g_heads):
    nblk = CHUNK // SUB
    dkh = LANE // g_heads
    dvp = g_heads * LANE
    npar = DD_NPAR
    row = lax.broadcasted_iota(jnp.int32, (CHUNK, CHUNK), 0)
    col = lax.broadcasted_iota(jnp.int32, (CHUNK, CHUNK), 1)
    level_masks = []
    for sh in (5, 4, 3):
        same2b = jnp.right_shift(row, sh + 1) == jnp.right_shift(col, sh + 1)
        upper = (jnp.right_shift(row, sh) & 1) == 1
        lower = (jnp.right_shift(col, sh) & 1) == 0
        level_masks.append(jnp.where(same2b, jnp.where(upper, jnp.where(lower, 1.0, 0.0), 0.0), 0.0))
    tri8 = jnp.where(jnp.right_shift(row, 3) == jnp.right_shift(col, 3),
                     jnp.where(col <= row, 1.0, 0.0), 0.0).astype(BF16)
    diag_mask = tri8.astype(F32)
    lane128 = lax.broadcasted_iota(jnp.int32, (CHUNK, LANE), 1)
    head_masks = [jnp.where((lane128 >= g * dkh) & (lane128 < (g + 1) * dkh), 1.0, 0.0)
                  for g in range(g_heads)]
    sub = lax.broadcasted_iota(jnp.int32, (SUB, LANE), 0)
    dk_sh = dkh.bit_length() - 1
    lane_sh = LANE.bit_length() - 1
    orow = lax.broadcasted_iota(jnp.int32, (LANE, dvp), 0)
    ocol = lax.broadcasted_iota(jnp.int32, (LANE, dvp), 1)
    ones_bd = jnp.where(jnp.right_shift(orow, dk_sh) == jnp.right_shift(ocol, lane_sh), 1.0, 0.0).astype(BF16)
    srow = lax.broadcasted_iota(jnp.int32, (dvp, LANE), 0)
    scol = lax.broadcasted_iota(jnp.int32, (dvp, LANE), 1)
    state_mask = jnp.where(jnp.right_shift(srow, lane_sh) == jnp.right_shift(scol, dk_sh), 1.0, 0.0)

    def block_sums(u, r0):
        c8 = [c8_s[u, r0 + b * SUB:r0 + (b + 1) * SUB, :] for b in range(nblk)]
        t8 = [c8_s[u, r0 + (b + 1) * SUB - 1:r0 + (b + 1) * SUB, :] for b in range(nblk)]
        t16 = [t8[2 * b] + t8[2 * b + 1] for b in range(nblk // 2)]
        t32 = [t16[2 * b] + t16[2 * b + 1] for b in range(nblk // 4)]
        t64 = t32[0] + t32[1]
        c16 = [c8[b] + t8[b - 1] if b % 2 else c8[b] for b in range(nblk)]
        c32 = [c16[b] + t16[b // 2 - 1] if (b // 2) % 2 else c16[b] for b in range(nblk)]
        c64 = [c32[b] + t32[0] if b >= nblk // 2 else c32[b] for b in range(nblk)]
        pre = {8: c8, 16: c16, 32: c32, 64: c64}
        suf = {8: [t8[b] - c8[b] for b in range(nblk)],
               16: [t16[b // 2] - c16[b] for b in range(nblk)],
               32: [t32[b // 4] - c32[b] for b in range(nblk)],
               64: [t64 - c64[b] for b in range(nblk)]}
        return pre, suf

    def cat(pieces):
        return jnp.concatenate(pieces, axis=0)

    def trip(u, t, slot):
        rng = range(npar)
        r0s = [(t * npar + i) * CHUNK for i in rng]
        qc = [q_s[u, r0:r0 + CHUNK, :] for r0 in r0s]
        kc = [k_s[u, r0:r0 + CHUNK, :] for r0 in r0s]
        vc = [v_s[u, r0:r0 + CHUNK, :] for r0 in r0s]
        c8_all = _mm_01(tri8, jnp.concatenate([la_s[u, r0:r0 + CHUNK, :] for r0 in r0s], axis=1))
        for i in rng:
            c8_s[u, r0s[i]:r0s[i] + CHUNK, :] = c8_all[:, i * LANE:(i + 1) * LANE]
        sums = [block_sums(u, r0s[i]) for i in rng]
        attn = [[None] * g_heads for _ in rng]
        for li, b in enumerate((32, 16, 8)):
            qs = [qc[i] * jnp.exp(cat(sums[i][0][b])) for i in rng]
            ks = [kc[i] * jnp.exp(cat(sums[i][1][b])) for i in rng]
            for g in range(g_heads):
                for i in rng:
                    qg = qs[i] * head_masks[g] if g_heads > 1 else qs[i]
                    term = _mm_nt(qg, ks[i]) * level_masks[li]
                    attn[i][g] = term if li == 0 else attn[i][g] + term
        for i in rng:
            dg = diag_terms(qc[i], kc[i], cat(sums[i][0][8]))
            for g in range(g_heads):
                attn[i][g] = attn[i][g] + dg[g]
        o = [apply(attn[i], vc[i]) for i in rng]
        q_inter = [qc[i] * jnp.exp(cat(sums[i][0][64])) for i in rng]
        k_state = [kc[i] * jnp.exp(cat(sums[i][1][64])) for i in rng]
        upd = [_mm_tn(vc[i], k_state[i]) for i in rng]
        st = state_ref[u]
        for i in rng:
            o_s[u, r0s[i]:r0s[i] + CHUNK, :] = o[i] + _mm_nt(q_inter[i], st)
            decay_last = jnp.exp(sums[i][0][64][nblk - 1][SUB - 1:SUB, :])
            st = st * decay_last + (upd[i] * state_mask if g_heads > 1 else upd[i])
        state_ref[u] = st

    def apply(att, v):
        if g_heads > 1:
            return jnp.concatenate([_mm(att[g], v[:, g * LANE:(g + 1) * LANE]) for g in range(g_heads)], axis=1)
        return _mm(att[0], v)

    def diag_terms(q, k, c8):
        qd = q * jnp.exp(c8)
        kd = k * jnp.exp(jnp.minimum(-c8, DD_SAFE_EXP))
        return [_mm_nt(qd * head_masks[g] if g_heads > 1 else qd, kd) * diag_mask for g in range(g_heads)]

    def fix_unit(u, nchunk):
        def body(c, carry):
            r0 = pl.multiple_of(c * CHUNK, CHUNK)
            rows = pl.ds(r0, CHUNK)
            fast = apply(diag_terms(q_s[u, rows, :], k_s[u, rows, :], c8_s[u, rows, :]), v_s[u, rows, :])
            for r in range(nblk):
                qr = q_s[u, pl.ds(r0 + r * SUB, SUB), :]
                cr = c8_s[u, pl.ds(r0 + r * SUB, SUB), :]
                for jj in range(SUB):
                    krow = k_s[u, pl.ds(r0 + r * SUB + jj, 1), :]
                    crow = c8_s[u, pl.ds(r0 + r * SUB + jj, 1), :]
                    pr = qr * krow * jnp.exp(jnp.where(sub >= jj, cr - crow, -jnp.inf))
                    p_s[(r * SUB + jj) * SUB:(r * SUB + jj + 1) * SUB, :] = pr
            rs = jnp.dot(p_s[...].astype(BF16), ones_bd, preferred_element_type=F32)
            od = []
            for r in range(nblk):
                acc = None
                for jj in range(SUB):
                    term = (rs[(r * SUB + jj) * SUB:(r * SUB + jj + 1) * SUB, :]
                            * v_s[u, pl.ds(r0 + r * SUB + jj, 1), :])
                    acc = term if acc is None else acc + term
                od.append(acc)
            o_s[u, rows, :] = o_s[u, rows, :] + (cat(od) - fast)
            return carry

        lax.fori_loop(0, nchunk, body, 0)

    return trip, fix_unit


def _dd_fix_if_unsafe(c8_s, fix_unit, nu, tb, units, epilogue):
    worst = None
    for u in range(nu):
        tot = -c8_s[u, pl.ds(SUB - 1, tb // SUB, stride=SUB), :]
        worst = tot if worst is None else jnp.maximum(worst, tot)

    @pl.when(jnp.max(worst) > DD_SAFE_EXP)
    def _():
        for u in range(nu):
            fix_unit(u, tb // CHUNK)
        for u, t in units:
            epilogue(u, t)


def _dd_scratch(nu, tb, dvp):
    return [pltpu.VMEM((nu, dvp, LANE), F32),
            pltpu.VMEM((nu, tb, LANE), F32), pltpu.VMEM((nu, tb, LANE), F32),
            pltpu.VMEM((nu, tb, dvp), F32),
            pltpu.VMEM((nu, tb, LANE), F32),
            pltpu.VMEM((nu, tb, dvp), F32),
            pltpu.VMEM((nu, tb, LANE), F32),
            pltpu.VMEM((CHUNK * SUB, LANE), F32)]


def _gla_kernel(q_ref, k_ref, v_ref, lr_ref, gate_ref, w2_ref, b2_ref, nw_ref,
                o_ref, state_ref, q_s, k_s, v_s, la_s, o_s, c8_s, p_s, *, tb):
    @pl.when(pl.program_id(1) == 0)
    def _():
        state_ref[...] = jnp.zeros_like(state_ref)

    npair = GLA_HEADS // 2
    rp = DD_NPAR * CHUNK
    trip, fix_unit = _dd_make_trip(q_s, k_s, v_s, la_s, o_s, state_ref, c8_s, p_s, 2)

    def prologue(u, t):
        lo = t * rp
        ls = slice(u * LANE, (u + 1) * LANE)
        vs = slice(u * 2 * LANE, (u + 1) * 2 * LANE)
        q_s[u, lo:lo + rp, :] = q_ref[lo:lo + rp, ls].astype(F32) * (GLA_DK ** -0.5)
        k_s[u, lo:lo + rp, :] = k_ref[lo:lo + rp, ls].astype(F32)
        v_s[u, lo:lo + rp, :] = v_ref[lo:lo + rp, vs].astype(F32)
        z = _mm(lr_ref[lo:lo + rp, :], w2_ref[:, ls]) + b2_ref[:, ls]
        la_s[u, lo:lo + rp, :] = _log_sigmoid(z) * (1.0 / GLA_NORMALIZER)

    def epilogue(u, t):
        lo = t * rp
        for g in range(2):
            hs = slice((2 * u + g) * LANE, (2 * u + g + 1) * LANE)
            o = o_s[u, lo:lo + rp, g * LANE:(g + 1) * LANE]
            oh = o * lax.rsqrt(jnp.mean(o * o, axis=-1, keepdims=True) + RMS_EPS) * nw_ref[...]
            o_ref[lo:lo + rp, hs] = (oh * _silu(gate_ref[lo:lo + rp, hs].astype(F32))).astype(o_ref.dtype)

    units = [(u, t) for u in range(npair) for t in range(tb // rp)]
    _schedule(units, prologue, trip, epilogue)
    _dd_fix_if_unsafe(c8_s, fix_unit, npair, tb, units, epilogue)


def _gla_call(pb, pf, w2p, b2, norm_w, layer, batch, seq):
    tb = min(TB_MIX, seq)
    nb = seq // tb
    t = batch * seq
    kern = functools.partial(_gla_kernel, tb=tb)
    return pl.pallas_call(
        kern,
        grid=(batch, nb),
        in_specs=[pl.BlockSpec((tb, GLA_QK), lambda b, i: (b * nb + i, B_Q // GLA_QK)),
                  pl.BlockSpec((tb, GLA_QK), lambda b, i: (b * nb + i, B_K // GLA_QK)),
                  pl.BlockSpec((tb, GLA_V), lambda b, i: (b * nb + i, B_V // GLA_V)),
                  pl.BlockSpec((tb, LANE), lambda b, i: (b * nb + i, F_LR // LANE)),
                  pl.BlockSpec((tb, GLA_V), lambda b, i: (b * nb + i, B_G // GLA_V)),
                  pl.BlockSpec((None, LANE, GLA_QK), lambda b, i: (layer, 0, 0)),
                  pl.BlockSpec((None, 1, GLA_QK), lambda b, i: (layer, 0, 0)),
                  pl.BlockSpec((None, 1, GLA_DV), lambda b, i: (layer, 0, 0))],
        out_specs=pl.BlockSpec((tb, GLA_V), lambda b, i: (b * nb + i, 0)),
        out_shape=jax.ShapeDtypeStruct((t, GLA_V), BF16),
        scratch_shapes=_dd_scratch(GLA_HEADS // 2, tb, 2 * LANE),
        compiler_params=_cparams(("parallel", "arbitrary")),
        name="gla",
    )(pb, pb, pb, pf, pb, w2p, b2, norm_w)


def _hgrn_kernel(q_ref, f_ref, v_ref, gate_ref, lbl_ref, nw_ref,
                 o_ref, state_ref, q_s, k_s, v_s, la_s, o_s, c8_s, p_s, *, tb, layer):
    @pl.when(pl.program_id(1) == 0)
    def _():
        state_ref[...] = jnp.zeros_like(state_ref)

    logits = lbl_ref[...]
    mx = jnp.max(logits, axis=0, keepdims=True)
    ex = jnp.exp(logits - mx)
    p = ex / jnp.sum(ex, axis=0, keepdims=True)
    acc = p[0:1, :]
    for r in range(1, layer + 1):
        acc = acc + p[r:r + 1, :]
    lb = jnp.clip(acc - p[0:1, :], 0.0, 1.0)
    log_lb = jnp.log(jnp.maximum(lb, LB_FLOOR))
    log_1m = jnp.log(1.0 - lb)

    rp = DD_NPAR * CHUNK
    trip, fix_unit = _dd_make_trip(q_s, k_s, v_s, la_s, o_s, state_ref, c8_s, p_s, 1)

    def prologue(u, t):
        lo = t * rp
        hs = slice(u * LANE, (u + 1) * LANE)
        cf = f_ref[lo:lo + rp, hs]
        second = log_1m[:, hs] + _log_sigmoid(cf)
        llb = log_lb[:, hs]
        la_s[u, lo:lo + rp, :] = jnp.maximum(llb, second) + jnp.log(1.0 + jnp.exp(-jnp.abs(llb - second)))
        k_s[u, lo:lo + rp, :] = (1.0 - lb[:, hs]) * _sigmoid_t(-cf)
        q_s[u, lo:lo + rp, :] = _silu(q_ref[lo:lo + rp, hs].astype(F32)) * (HGRN_EXPAND ** -0.5)
        v_s[u, lo:lo + rp, :] = v_ref[lo:lo + rp, hs].astype(F32)

    def epilogue(u, t):
        lo = t * rp
        hs = slice(u * LANE, (u + 1) * LANE)
        o = o_s[u, lo:lo + rp, :]
        oh = o * lax.rsqrt(jnp.mean(o * o, axis=-1, keepdims=True) + RMS_EPS) * nw_ref[...]
        o_ref[lo:lo + rp, hs] = (oh * _silu(gate_ref[lo:lo + rp, hs].astype(F32))).astype(o_ref.dtype)

    units = [(u, t) for u in range(HGRN_HEADS) for t in range(tb // rp)]
    _schedule(units, prologue, trip, epilogue)
    _dd_fix_if_unsafe(c8_s, fix_unit, HGRN_HEADS, tb, units, epilogue)


def _hgrn_call(pb, pf, lb_logits, norm_w, layer, batch, seq):
    tb = min(TB_MIX, seq)
    nb = seq // tb
    t = batch * seq
    kern = functools.partial(_hgrn_kernel, tb=tb, layer=layer)
    return pl.pallas_call(
        kern,
        grid=(batch, nb),
        in_specs=[pl.BlockSpec((tb, HGRN_QK), lambda b, i: (b * nb + i, C_Q // HGRN_QK)),
                  pl.BlockSpec((tb, HGRN_QK), lambda b, i: (b * nb + i, F_CF // HGRN_QK)),
                  pl.BlockSpec((tb, HGRN_V), lambda b, i: (b * nb + i, C_I // HGRN_V)),
                  pl.BlockSpec((tb, HGRN_V), lambda b, i: (b * nb + i, C_G // HGRN_V)),
                  pl.BlockSpec((DEPTH, HGRN_QK), lambda b, i: (0, 0)),
                  pl.BlockSpec((None, 1, HGRN_DV), lambda b, i: (layer, 0, 0))],
        out_specs=pl.BlockSpec((tb, HGRN_V), lambda b, i: (b * nb + i, 0)),
        out_shape=jax.ShapeDtypeStruct((t, HGRN_V), BF16),
        scratch_shapes=_dd_scratch(HGRN_HEADS, tb, LANE),
        compiler_params=_cparams(("parallel", "arbitrary")),
        name="hgrn",
    )(pb, pf, pb, pb, lb_logits, norm_w)


def _merge_kernel(oa_ref, ob_ref, oc_ref, ma_ref, mb_ref, mc_ref, h_ref,
                  wa_ref, wb_ref, wc_ref, wo_ref, g_ref, b_ref, ox_ref, *, tm):
    y = (_sigmoid_t(ma_ref[...].astype(F32)) * jnp.dot(oa_ref[...], wa_ref[...], preferred_element_type=F32)
         + _sigmoid_t(mb_ref[...].astype(F32)) * jnp.dot(ob_ref[...], wb_ref[...], preferred_element_type=F32)
         + _sigmoid_t(mc_ref[...].astype(F32)) * jnp.dot(oc_ref[...], wc_ref[...], preferred_element_type=F32))
    mix = _mm(y, wo_ref[...])
    _to_token_major(ox_ref, _layer_norm(ALPHA * h_ref[...] + mix, g_ref[...], b_ref[...]), tm)


def _merge_call(o_a, o_b, o_c, pb, h, wa, wb, wc, wo, g, b, layer):
    t, d = h.shape
    tm = min(TM_ROWS, t)

    def row(width):
        return pl.BlockSpec((tm, width), lambda i: (i, 0))

    def wspec(kdim):
        return pl.BlockSpec((None, kdim, d), lambda i: (layer, 0, 0))

    vec = pl.BlockSpec((None, 1, d), lambda i: (layer, 0, 0))
    return pl.pallas_call(
        functools.partial(_merge_kernel, tm=tm),
        grid=(t // tm,),
        in_specs=[row(GDN_V), row(GLA_V), row(HGRN_V),
                  pl.BlockSpec((tm, d), lambda i: (i, M_A // d)),
                  pl.BlockSpec((tm, d), lambda i: (i, M_B // d)),
                  pl.BlockSpec((tm, d), lambda i: (i, M_C // d)),
                  row(d), wspec(GDN_V), wspec(GLA_V), wspec(HGRN_V), wspec(d), vec, vec],
        out_specs=pl.BlockSpec((tm * NPIECE, LANE), lambda i: (i, 0)),
        out_shape=jax.ShapeDtypeStruct((t * NPIECE, LANE), F32),
        compiler_params=_cparams(("parallel",)),
        name="merge",
    )(o_a, o_b, o_c, pb, pb, pb, h, wa, wb, wc, wo, g, b)


def _route_pairs(scores_t, bias_ref):
    s = [scores_t[e:e + 1, :] for e in range(N_EXPERTS)]
    sel = [s[e] + bias_ref[e:e + 1, 0:1] for e in range(N_EXPERTS)]
    gscore = []
    for g in range(N_GROUPS):
        a, b, c, d = sel[4 * g:4 * g + 4]
        hi1, lo1 = jnp.maximum(a, b), jnp.minimum(a, b)
        hi2, lo2 = jnp.maximum(c, d), jnp.minimum(c, d)
        top1 = jnp.maximum(hi1, hi2)
        top2 = jnp.maximum(jnp.minimum(hi1, hi2), jnp.maximum(lo1, lo2))
        gscore.append(top1 + top2)
    best = gscore[0]
    gidx = jnp.zeros_like(best, dtype=jnp.int32)
    for g in range(1, N_GROUPS):
        take = gscore[g] > best
        best = jnp.where(take, gscore[g], best)
        gidx = jnp.where(take, g, gidx)
    ing, raw = [], []
    for kk in range(EXPERTS_PER_GROUP):
        vs, vr = sel[kk], s[kk]
        for g in range(1, N_GROUPS):
            pick = gidx == g
            vs = jnp.where(pick, sel[4 * g + kk], vs)
            vr = jnp.where(pick, s[4 * g + kk], vr)
        ing.append(vs)
        raw.append(vr)
    b1 = ing[0]
    i1 = jnp.zeros_like(gidx)
    for kk in range(1, EXPERTS_PER_GROUP):
        take = ing[kk] > b1
        b1 = jnp.where(take, ing[kk], b1)
        i1 = jnp.where(take, kk, i1)
    neg = jnp.full_like(b1, -jnp.inf)
    b2 = neg
    i2 = jnp.zeros_like(gidx)
    for kk in range(EXPERTS_PER_GROUP):
        cand = jnp.where(i1 == kk, neg, ing[kk])
        take = cand > b2
        b2 = jnp.where(take, cand, b2)
        i2 = jnp.where(take, kk, i2)
    w1 = raw[0]
    w2 = raw[0]
    for kk in range(1, EXPERTS_PER_GROUP):
        w1 = jnp.where(i1 == kk, raw[kk], w1)
        w2 = jnp.where(i2 == kk, raw[kk], w2)
    tot = w1 + w2
    w1 = w1 / tot
    w2 = w2 / tot
    first_lower = i1 < i2
    lo = jnp.where(first_lower, i1, i2)
    hi = jnp.where(first_lower, i2, i1)
    pidx = jnp.where(lo == 0, hi - 1, jnp.where(lo == 1, hi + 1, 5))
    bkt = gidx * 6 + pidx
    return bkt, jnp.where(first_lower, w1, w2), jnp.where(first_lower, w2, w1)


def _moe_route_kernel(hx_ref, wr_ref, rb_ref, bkt_ref, rank_ref, wab_ref, cnt_ref, carry_ref, *, tm):
    @pl.when(pl.program_id(0) == 0)
    def _():
        carry_ref[...] = jnp.zeros_like(carry_ref)

    logits_t = _mm_nt_f32(wr_ref[...], _from_token_major(hx_ref, tm))
    bkt, wa, wb = _route_pairs(_sigmoid(logits_t), rb_ref)
    sub = lax.broadcasted_iota(jnp.int32, (32, tm), 0)
    oh = jnp.where(sub == bkt, 1.0, 0.0)
    r = lax.broadcasted_iota(jnp.int32, (tm, tm), 0)
    c = lax.broadcasted_iota(jnp.int32, (tm, tm), 1)
    earlier = jnp.where(r < c, 1.0, 0.0).astype(BF16)
    before = jnp.dot(oh.astype(BF16), earlier, preferred_element_type=F32)
    carry = carry_ref[...]
    rank = jnp.sum(oh * (before + carry[:, 0:1]), axis=0, keepdims=True)
    carry = carry + jnp.sum(oh, axis=1, keepdims=True)
    carry_ref[...] = carry
    cnt_ref[...] = carry
    bkt_ref[...] = bkt
    rank_ref[...] = rank.astype(jnp.int32)
    pad = jnp.zeros((LANE - 2, tm), F32)
    wab_ref[...] = jnp.transpose(jnp.concatenate([wa, wb, pad], axis=0))


def _moe_tables_kernel(cnt_ref, bkt_ref, rank_ref, pos_ref, tabs_ref, *, t, tr):
    cnt = cnt_ref[...]
    sz = jnp.floor((cnt + (tr - 1)) * (1.0 / tr)) * tr
    r = lax.broadcasted_iota(jnp.int32, (32, 32), 0)
    c = lax.broadcasted_iota(jnp.int32, (32, 32), 1)
    start = jnp.dot(jnp.where(c < r, 1.0, 0.0), sz, preferred_element_type=F32, precision=HIGHEST)
    end = start + sz
    sub = lax.broadcasted_iota(jnp.int32, (32, t), 0)
    pos = jnp.sum(jnp.where(sub == bkt_ref[...], start[:, 0:1], 0.0), axis=0, keepdims=True)
    pos_ref[...] = pos.astype(jnp.int32) + rank_ref[...]
    brow = lax.broadcasted_iota(jnp.int32, (32, LANE), 0)
    tile0 = lax.broadcasted_iota(jnp.int32, (32, LANE), 1).astype(F32) * tr
    tbk = jnp.sum(jnp.where((brow < MOE_NBKT) & (end <= tile0), 1, 0), axis=0, keepdims=True)
    tbk = jnp.minimum(tbk, MOE_NBKT - 1)
    total = end[MOE_NBKT - 1:MOE_NBKT, :]
    valid = jnp.where(tile0[0:1, :] < total, 1, 0)
    g = jnp.where(tbk >= 6, 1, 0) + jnp.where(tbk >= 12, 1, 0) + jnp.where(tbk >= 18, 1, 0)
    p = tbk - 6 * g
    ge3 = jnp.where(p >= 3, 1, 0)
    ge5 = jnp.where(p >= 5, 1, 0)
    ea = 4 * g + ge3 + ge5
    eb = 4 * g + p + 1 - 2 * ge3 - ge5
    zero = jnp.zeros((SUBLANE - 3, LANE), jnp.int32)
    tabs_ref[...] = jnp.concatenate([ea, eb, valid, zero], axis=0)


def _moe_group_kernel(ea_ref, eb_ref, vd_ref, xs_ref, ws_ref, wga_ref, wua_ref, wda_ref,
                      wgb_ref, wub_ref, wdb_ref, ys_ref, wg_s, wu_s, wd_s, *, tr):
    j = pl.program_id(0)
    prev = jnp.maximum(j - 1, 0)

    @pl.when((j == 0) | (ea_ref[j] != ea_ref[prev]))
    def _():
        wg_s[0] = wga_ref[...].astype(BF16)
        wu_s[0] = wua_ref[...].astype(BF16)
        wd_s[0] = wda_ref[...].astype(BF16)

    @pl.when((j == 0) | (eb_ref[j] != eb_ref[prev]))
    def _():
        wg_s[1] = wgb_ref[...].astype(BF16)
        wu_s[1] = wub_ref[...].astype(BF16)
        wd_s[1] = wdb_ref[...].astype(BF16)

    @pl.when(vd_ref[j] > 0)
    def _():
        x = _from_token_major(xs_ref, tr).astype(BF16)
        w = ws_ref[...]

        def ffn(s, cw):
            hg = jnp.dot(x, wg_s[s], preferred_element_type=F32)
            hu = jnp.dot(x, wu_s[s], preferred_element_type=F32)
            hid = _silu(hg) * hu * cw
            return jnp.dot(hid.astype(BF16), wd_s[s], preferred_element_type=F32)

        y = ffn(0, w[:, 0:1]) + ffn(1, w[:, 1:2])
        _to_token_major(ys_ref, y, tr)


def _moe_ln_kernel(hx_ref, y_ref, g_ref, b_ref, o_ref, ob_ref, *, tm):
    out = _layer_norm(ALPHA * _from_token_major(hx_ref, tm) + _from_token_major(y_ref, tm),
                      g_ref[...], b_ref[...])
    o_ref[...] = out
    ob_ref[...] = out.astype(BF16)


def _sc_mesh_info():
    info = plsc.get_sparse_core_info()
    mesh = plsc.VectorSubcoreMesh(core_axis_name="c", subcore_axis_name="s")
    return mesh, info.num_cores, info.num_subcores


def _sc_scatter_rows(x3, w2, pos, n_rows):
    t = x3.shape[0]
    mesh, nc, ns = _sc_mesh_info()
    per_w = t // (nc * ns)

    @functools.partial(
        pl.kernel, mesh=mesh,
        out_type=[jax.ShapeDtypeStruct((n_rows,) + x3.shape[1:], x3.dtype),
                  jax.ShapeDtypeStruct((n_rows,) + w2.shape[1:], w2.dtype)],
        scratch_types=[pltpu.VMEM((SC_CH,), jnp.int32), pltpu.VMEM((SC_CH,) + x3.shape[1:], x3.dtype),
                       pltpu.VMEM((SC_CH,) + w2.shape[1:], w2.dtype),
                       pltpu.SemaphoreType.DMA, pltpu.SemaphoreType.DMA])
    def k(x_hbm, w_hbm, idx_hbm, ox_hbm, ow_hbm, idx_v, rows_v, wrows_v, sem_x, sem_w):
        base = (lax.axis_index("s") * nc + lax.axis_index("c")) * per_w

        @pl.loop(0, per_w // SC_CH)
        def _(j):
            off = base + j * SC_CH
            pltpu.sync_copy(idx_hbm.at[pl.ds(off, SC_CH)], idx_v)
            pltpu.sync_copy(x_hbm.at[pl.ds(off, SC_CH)], rows_v)
            pltpu.sync_copy(w_hbm.at[pl.ds(off, SC_CH)], wrows_v)
            cx = pltpu.async_copy(rows_v, ox_hbm.at[idx_v], sem_x)
            cw = pltpu.async_copy(wrows_v, ow_hbm.at[idx_v], sem_w)
            cx.wait()
            cw.wait()

    return k(x3, w2, pos)


def _sc_gather_rows(y3, pos):
    t = pos.shape[0]
    mesh, nc, ns = _sc_mesh_info()
    per_w = t // (nc * ns)

    @functools.partial(
        pl.kernel, mesh=mesh,
        out_type=jax.ShapeDtypeStruct((t,) + y3.shape[1:], y3.dtype),
        scratch_types=[pltpu.VMEM((SC_CH,), jnp.int32), pltpu.VMEM((SC_CH,) + y3.shape[1:], y3.dtype),
                       pltpu.SemaphoreType.DMA])
    def k(y_hbm, idx_hbm, o_hbm, idx_v, rows_v, sem):
        base = (lax.axis_index("s") * nc + lax.axis_index("c")) * per_w

        @pl.loop(0, per_w // SC_CH)
        def _(j):
            off = base + j * SC_CH
            pltpu.sync_copy(idx_hbm.at[pl.ds(off, SC_CH)], idx_v)
            pltpu.async_copy(y_hbm.at[idx_v], rows_v, sem).wait()
            pltpu.sync_copy(rows_v, o_hbm.at[pl.ds(off, SC_CH)])

    return k(y3, pos)


def _moe_sparse(hx, wr_t, rbias, wg, wu, wd, g, b, layer):
    t = hx.shape[0] // NPIECE
    d = D_MODEL
    tm = min(TM_MOE, t)
    tr = MOE_TR
    nt = t // tr + MOE_NBKT
    n_rows = nt * tr
    row1 = pl.BlockSpec((1, tm), lambda i: (0, i))
    rows_tm = pl.BlockSpec((tm * NPIECE, LANE), lambda i: (i, 0))
    bkt, rank, wab, cnt = pl.pallas_call(
        functools.partial(_moe_route_kernel, tm=tm),
        grid=(t // tm,),
        in_specs=[rows_tm,
                  pl.BlockSpec((N_EXPERTS, d), lambda i: (0, 0)),
                  pl.BlockSpec((N_EXPERTS, LANE), lambda i: (0, 0))],
        out_specs=[row1, row1, pl.BlockSpec((tm, LANE), lambda i: (i, 0)),
                   pl.BlockSpec((32, LANE), lambda i: (0, 0))],
        out_shape=[jax.ShapeDtypeStruct((1, t), jnp.int32), jax.ShapeDtypeStruct((1, t), jnp.int32),
                   jax.ShapeDtypeStruct((t, LANE), F32), jax.ShapeDtypeStruct((32, LANE), F32)],
        scratch_shapes=[pltpu.VMEM((32, LANE), F32)],
        compiler_params=_cparams(("arbitrary",)),
        name="moe_route",
    )(hx, wr_t, rbias)
    pos, tabs = pl.pallas_call(
        functools.partial(_moe_tables_kernel, t=t, tr=tr),
        out_shape=[jax.ShapeDtypeStruct((1, t), jnp.int32), jax.ShapeDtypeStruct((SUBLANE, LANE), jnp.int32)],
        compiler_params=pltpu.CompilerParams(vmem_limit_bytes=VMEM_LIMIT),
        name="moe_tables",
    )(cnt, bkt, rank)
    pos = pos.reshape(t)
    xs3, ws = _sc_scatter_rows(hx.reshape(t, NPIECE, LANE), wab, pos, n_rows)

    def wspec(which, shape):
        if which == 0:
            return pl.BlockSpec((None, None) + shape, lambda j, ea, eb, vd: (layer, ea[j], 0, 0))
        return pl.BlockSpec((None, None) + shape, lambda j, ea, eb, vd: (layer, eb[j], 0, 0))

    ys = pl.pallas_call(
        functools.partial(_moe_group_kernel, tr=tr),
        grid_spec=pltpu.PrefetchScalarGridSpec(
            num_scalar_prefetch=3,
            grid=(nt,),
            in_specs=[pl.BlockSpec((tr * NPIECE, LANE), lambda j, ea, eb, vd: (j, 0)),
                      pl.BlockSpec((tr, LANE), lambda j, ea, eb, vd: (j, 0)),
                      wspec(0, (d, D_FF)), wspec(0, (d, D_FF)), wspec(0, (D_FF, d)),
                      wspec(1, (d, D_FF)), wspec(1, (d, D_FF)), wspec(1, (D_FF, d))],
            out_specs=pl.BlockSpec((tr * NPIECE, LANE), lambda j, ea, eb, vd: (j, 0)),
            scratch_shapes=[pltpu.VMEM((2, d, D_FF), BF16), pltpu.VMEM((2, d, D_FF), BF16),
                            pltpu.VMEM((2, D_FF, d), BF16)]),
        out_shape=jax.ShapeDtypeStruct((n_rows * NPIECE, LANE), F32),
        compiler_params=_cparams(("arbitrary",)),
        name="moe_experts",
    )(tabs[0, :nt], tabs[1, :nt], tabs[2, :nt], xs3.reshape(n_rows * NPIECE, LANE), ws,
      wg, wu, wd, wg, wu, wd)
    y3 = _sc_gather_rows(ys.reshape(n_rows, NPIECE, LANE), pos)
    vec = pl.BlockSpec((None, 1, d), lambda i: (layer, 0, 0))
    row = pl.BlockSpec((tm, d), lambda i: (i, 0))
    return pl.pallas_call(
        functools.partial(_moe_ln_kernel, tm=tm),
        grid=(t // tm,),
        in_specs=[rows_tm, rows_tm, vec, vec],
        out_specs=[row, row],
        out_shape=[jax.ShapeDtypeStruct((t, d), F32), jax.ShapeDtypeStruct((t, d), BF16)],
        compiler_params=_cparams(("parallel",)),
        name="moe_ln",
    )(hx, y3.reshape(t * NPIECE, LANE), g, b)


def _pack_w_in(w_in):
    wt = jnp.swapaxes(w_in, 1, 2)
    (a_q, a_k, a_v, a_beta, a_dt, a_g, b_q, b_k, b_v, b_lr, b_g,
     c_q, c_f, c_i, c_g, m_a, m_b, m_c) = jnp.split(wt, SPLIT_POINTS, axis=1)
    depth, _, d = wt.shape
    a_s = jnp.concatenate([a_beta, a_dt, jnp.zeros((depth, LANE - 2 * GDN_HEADS, d), wt.dtype)], 1)
    b_lrp = jnp.concatenate([b_lr, jnp.zeros((depth, LANE - GLA_RANK, d), wt.dtype)], 1)
    wb = jnp.concatenate([m_a, m_b, m_c, a_q, a_k, a_v, a_g, b_q, b_k, b_v, b_g, c_q, c_i, c_g], 1).astype(BF16)
    wf = jnp.concatenate([c_f, a_s, b_lrp], 1).astype(BF16)
    assert wb.shape[1] == NPB and wf.shape[1] == NPF
    return wb, wf


def _prepare(w_in, gdn_conv, gdn_a_log, gdn_dt_bias, gdn_norm, gla_w2, gla_b2, gla_norm, hgrn_lb_logits,
             hgrn_norm, w_br_a, w_br_b, w_br_c, w_out, ln1_g, ln1_b, w_router, router_bias, w_gate, w_up,
             w_down, ln2_g, ln2_b):
    depth = w_in.shape[0]
    d = w_out.shape[-1]
    w_pb, w_pf = _pack_w_in(w_in)
    return dict(
        w_pb=w_pb, w_pf=w_pf,
        gdn_conv=gdn_conv,
        gdn_par=jnp.pad(jnp.stack([gdn_a_log, gdn_dt_bias], axis=1),
                        ((0, 0), (0, SUBLANE - 2), (GDN_HEADS, LANE - 2 * GDN_HEADS))),
        gdn_norm=gdn_norm.reshape(depth, 1, GDN_DV),
        w2p=jnp.concatenate([gla_w2, jnp.zeros((depth, LANE - GLA_RANK, GLA_QK), gla_w2.dtype)], axis=1),
        gla_b2=gla_b2.reshape(depth, 1, GLA_QK),
        gla_norm=gla_norm.reshape(depth, 1, GLA_DV),
        lb_logits=hgrn_lb_logits,
        hgrn_norm=hgrn_norm.reshape(depth, 1, HGRN_DV),
        wa=w_br_a.astype(BF16), wb=w_br_b.astype(BF16), wc=w_br_c.astype(BF16), wo=w_out.astype(BF16),
        ln1_g=ln1_g.reshape(depth, 1, d), ln1_b=ln1_b.reshape(depth, 1, d),
        wr_t=jnp.transpose(w_router),
        rbias=jnp.broadcast_to(router_bias[:, None], (N_EXPERTS, LANE)),
        wg=w_gate, wu=w_up, wd=w_down,
        ln2_g=ln2_g.reshape(depth, 1, d), ln2_b=ln2_b.reshape(depth, 1, d),
    )


def _mixer_block(h, hb, p, layer, batch, seq):
    pb = _inproj_call(hb, p["w_pb"], layer, BF16, TM_PROJ, TN_PROJ, "inproj_b")
    pf = _inproj_call(hb, p["w_pf"], layer, F32, TM_PROJ // 2, NPF, "inproj_f")
    o_a = _gdn_call(pb, pf, p["gdn_conv"], p["gdn_par"], p["gdn_norm"], layer, batch, seq)
    o_b = _gla_call(pb, pf, p["w2p"], p["gla_b2"], p["gla_norm"], layer, batch, seq)
    o_c = _hgrn_call(pb, pf, p["lb_logits"], p["hgrn_norm"], layer, batch, seq)
    return _merge_call(o_a, o_b, o_c, pb, h, p["wa"], p["wb"], p["wc"], p["wo"], p["ln1_g"], p["ln1_b"], layer)


def _ffn_block(hx, p, layer):
    return _moe_sparse(hx, p["wr_t"], p["rbias"], p["wg"], p["wu"], p["wd"], p["ln2_g"], p["ln2_b"], layer)


def kernel(x, ln0_g, ln0_b, w_in, gdn_conv, gdn_a_log, gdn_dt_bias, gdn_norm, gla_w2, gla_b2, gla_norm,
           hgrn_lb_logits, hgrn_norm, w_br_a, w_br_b, w_br_c, w_out, ln1_g, ln1_b, w_router, router_bias,
           w_gate, w_up, w_down, ln2_g, ln2_b):
    batch, seq, d = x.shape
    p = _prepare(w_in, gdn_conv, gdn_a_log, gdn_dt_bias, gdn_norm, gla_w2, gla_b2, gla_norm, hgrn_lb_logits,
                 hgrn_norm, w_br_a, w_br_b, w_br_c, w_out, ln1_g, ln1_b, w_router, router_bias, w_gate, w_up,
                 w_down, ln2_g, ln2_b)
    h, hb = _ln_call(x.reshape(batch * seq, d), ln0_g, ln0_b)
    for layer in range(w_in.shape[0]):
        hx = _mixer_block(h, hb, p, layer, batch, seq)
        h, hb = _ffn_block(hx, p, layer)
    return h.reshape(batch, seq, d)
```

```python
import functools

import numpy as np
import jax
import jax.numpy as jnp
from jax import lax
from jax.experimental import pallas as pl
from jax.experimental.pallas import tpu as pltpu
from jax.experimental.pallas import tpu_sc as plsc

F32 = jnp.float32
BF16 = jnp.bfloat16
HIGHEST = lax.Precision.HIGHEST

D_MODEL = 1024
DEPTH = 4
CHUNK = 64
GDN_HEADS, GDN_DK, GDN_DV, CONV_W = 4, 128, 128, 4
GLA_HEADS, GLA_DK, GLA_DV, GLA_RANK, GLA_NORMALIZER = 4, 64, 128, 16, 16.0
HGRN_HEADS, HGRN_EXPAND, HGRN_DV = 4, 128, 128
LB_FLOOR = 1e-30
N_EXPERTS, N_GROUPS, TOP_K, D_FF = 16, 4, 2, 256
EXPERTS_PER_GROUP = N_EXPERTS // N_GROUPS
ALPHA = (2.0 * DEPTH) ** 0.25
LN_EPS = 1e-5
RMS_EPS = 1e-6

GDN_QK = GDN_HEADS * GDN_DK
GDN_V = GDN_HEADS * GDN_DV
GLA_QK = GLA_HEADS * GLA_DK
GLA_V = GLA_HEADS * GLA_DV
HGRN_QK = HGRN_HEADS * HGRN_EXPAND
HGRN_V = HGRN_HEADS * HGRN_DV
SPLIT_SIZES = (GDN_QK, GDN_QK, GDN_V, GDN_HEADS, GDN_HEADS, GDN_V,
               GLA_QK, GLA_QK, GLA_V, GLA_RANK, GLA_V,
               HGRN_QK, HGRN_QK, HGRN_V, HGRN_V,
               D_MODEL, D_MODEL, D_MODEL)
SPLIT_POINTS = tuple(int(v) for v in np.cumsum(SPLIT_SIZES)[:-1])

LANE = 128
SUBLANE = 8
VMEM_LIMIT = 48 * 1024 * 1024

M_A, M_B, M_C = 0, 1024, 2048
A_Q, A_K, A_V, A_G = 3072, 3584, 4096, 4608
B_Q, B_K, B_V, B_G = 5120, 5376, 5632, 6144
C_Q, C_I, C_G = 6656, 7168, 7680
NPB = 8192
F_CF, F_AS, F_LR = 0, 512, 640
NPF = 768

TB_MIX = 1024
TM_ROWS = 512
TM_MOE = 1024
TM_PROJ, TN_PROJ = 2048, 2048
SUB = 8
DD_NPAR = 16
DD_SAFE_EXP = 60.0
GDN_NPAR = 8
MOE_TR = 512
MOE_NBKT = N_GROUPS * 6
NPIECE = D_MODEL // LANE
SC_CH = 64


def _cparams(sem):
    return pltpu.CompilerParams(dimension_semantics=sem, vmem_limit_bytes=VMEM_LIMIT)


def _mm(a, b):
    return jnp.dot(a.astype(BF16), b.astype(BF16), preferred_element_type=F32)


def _mm_nt(a, b):
    return lax.dot_general(a.astype(BF16), b.astype(BF16), (((1,), (1,)), ((), ())),
                           preferred_element_type=F32)


def _mm_tn(a, b):
    return lax.dot_general(a.astype(BF16), b.astype(BF16), (((0,), (0,)), ((), ())),
                           preferred_element_type=F32)


def _mm_nt_f32(a, b):
    a_hi = a.astype(BF16)
    a_mid = (a - a_hi.astype(F32)).astype(BF16)
    b_hi = b.astype(BF16)
    b_mid = (b - b_hi.astype(F32)).astype(BF16)
    dn = (((1,), (1,)), ((), ()))
    return (lax.dot_general(a_hi, b_hi, dn, preferred_element_type=F32)
            + lax.dot_general(a_hi, b_mid, dn, preferred_element_type=F32)
            + lax.dot_general(a_mid, b_hi, dn, preferred_element_type=F32))


def _split3(x):
    hi = x.astype(BF16)
    r = x - hi.astype(F32)
    mid = r.astype(BF16)
    lo = (r - mid.astype(F32)).astype(BF16)
    return hi, mid, lo


def _mm_01(m01, x):
    hi, mid, lo = _split3(x)
    return (jnp.dot(m01, hi, preferred_element_type=F32) + jnp.dot(m01, mid, preferred_element_type=F32)
            + jnp.dot(m01, lo, preferred_element_type=F32))


def _sigmoid(x):
    return 1.0 / (1.0 + jnp.exp(-x))


def _sigmoid_t(x):
    return 0.5 * jnp.tanh(0.5 * x) + 0.5


def _silu(x):
    h = 0.5 * x
    return h + h * jnp.tanh(h)


def _softplus(x):
    return jnp.maximum(x, 0.0) + jnp.log(1.0 + jnp.exp(-jnp.abs(x)))


def _log_sigmoid(x):
    return -_softplus(-x)


def _layer_norm(x, g, b):
    mu = jnp.mean(x, axis=-1, keepdims=True)
    xc = x - mu
    var = jnp.mean(xc * xc, axis=-1, keepdims=True)
    return xc * lax.rsqrt(var + LN_EPS) * g + b


def _to_token_major(ref, x, rows):
    for j in range(NPIECE):
        ref[pl.ds(j, rows, stride=NPIECE), :] = x[:, j * LANE:(j + 1) * LANE]


def _from_token_major(ref, rows):
    return jnp.concatenate([ref[pl.ds(j, rows, stride=NPIECE), :] for j in range(NPIECE)], axis=1)


def _ln_kernel(x_ref, g_ref, b_ref, o_ref, ob_ref):
    y = _layer_norm(x_ref[...], g_ref[...], b_ref[...])
    o_ref[...] = y
    ob_ref[...] = y.astype(BF16)


def _ln_call(x, g, b):
    t, d = x.shape
    tm = min(TM_ROWS, t)
    return pl.pallas_call(
        _ln_kernel,
        grid=(t // tm,),
        in_specs=[pl.BlockSpec((tm, d), lambda i: (i, 0)),
                  pl.BlockSpec((1, d), lambda i: (0, 0)),
                  pl.BlockSpec((1, d), lambda i: (0, 0))],
        out_specs=[pl.BlockSpec((tm, d), lambda i: (i, 0)), pl.BlockSpec((tm, d), lambda i: (i, 0))],
        out_shape=[jax.ShapeDtypeStruct((t, d), F32), jax.ShapeDtypeStruct((t, d), BF16)],
        compiler_params=_cparams(("parallel",)),
        name="ln0",
    )(x, g.reshape(1, d), b.reshape(1, d))


def _inproj_kernel(x_ref, wt_ref, o_ref):
    o_ref[...] = lax.dot_general(x_ref[...], wt_ref[...], (((1,), (1,)), ((), ())),
                                 preferred_element_type=F32).astype(o_ref.dtype)


def _inproj_call(hb, w, layer, out_dtype, tm, tn, name):
    t, d = hb.shape
    tm = min(tm, t)
    n = w.shape[-2]
    return pl.pallas_call(
        _inproj_kernel,
        grid=(n // tn, t // tm),
        in_specs=[pl.BlockSpec((tm, d), lambda j, i: (i, 0)),
                  pl.BlockSpec((None, tn, d), lambda j, i: (layer, j, 0))],
        out_specs=pl.BlockSpec((tm, tn), lambda j, i: (i, j)),
        out_shape=jax.ShapeDtypeStruct((t, n), out_dtype),
        compiler_params=_cparams(("parallel", "parallel")),
        name=name,
    )(hb, w)


def _schedule(units, prologue, trip, epilogue):
    prologue(*units[0])
    for n, u in enumerate(units):
        if n + 1 < len(units):
            prologue(*units[n + 1])
        trip(*u, n % 2)
        epilogue(*u)


def _gdn_kernel(qkv_ref, s_ref, gate_ref, cw_ref, par_ref, nw_ref, o_ref,
                state_ref, tail_ref, xbuf_ref, q_s, k_s, v_s, cumb_s, betab_s, cumrow_s, o_s, *, tb):
    nc = tb // CHUNK
    nh = GDN_HEADS
    off0 = SUBLANE - (CONV_W - 1)

    @pl.when(pl.program_id(1) == 0)
    def _():
        state_ref[...] = jnp.zeros_like(state_ref)
        tail_ref[...] = jnp.zeros_like(tail_ref)

    row = lax.broadcasted_iota(jnp.int32, (CHUNK, CHUNK), 0)
    col = lax.broadcasted_iota(jnp.int32, (CHUNK, CHUNK), 1)
    incl = col <= row
    strict = col < row
    tri = jnp.where(incl, 1.0, 0.0).astype(BF16)
    eye = jnp.where(col == row, 1.0, 0.0).astype(F32)
    npar = GDN_NPAR
    rp = npar * CHUNK
    xbuf_ref[0:SUBLANE, :] = tail_ref[...]

    def prologue(t):
        lo = t * rp
        xbuf_ref[SUBLANE + lo:SUBLANE + lo + rp, :] = qkv_ref[lo:lo + rp, :].astype(F32)
        for j in range(3 * nh):
            cs = slice(j * LANE, (j + 1) * LANE)
            y = xbuf_ref[off0 + lo:off0 + lo + rp, cs] * cw_ref[0:1, cs]
            for kk in range(1, CONV_W):
                y = y + xbuf_ref[off0 + kk + lo:off0 + kk + lo + rp, cs] * cw_ref[kk:kk + 1, cs]
            y = _silu(y)
            if j < nh:
                q_s[j, lo:lo + rp, :] = (y * lax.rsqrt(jnp.sum(y * y, axis=-1, keepdims=True) + RMS_EPS)
                                         * (GDN_DK ** -0.5))
            elif j < 2 * nh:
                k_s[j - nh, lo:lo + rp, :] = y * lax.rsqrt(jnp.sum(y * y, axis=-1, keepdims=True) + RMS_EPS)
            else:
                v_s[j - 2 * nh, lo:lo + rp, :] = y
        sc = s_ref[lo:lo + rp, :]
        beta_all = _sigmoid_t(sc)
        g_all = -jnp.exp(par_ref[0:1, :]) * _softplus(sc + par_ref[1:2, :])
        cum_all = jnp.concatenate([_mm_01(tri, g_all[c * CHUNK:(c + 1) * CHUNK, :]) for c in range(npar)], axis=0)
        cum_t = jnp.transpose(cum_all)
        for c in range(npar):
            cumrow_s[t * npar + c] = cum_t[0:SUBLANE, c * CHUNK:(c + 1) * CHUNK]
        for h in range(nh):
            cumb_s[h, lo:lo + rp, :] = jnp.broadcast_to(cum_all[:, nh + h:nh + h + 1], (rp, LANE))
            betab_s[h, lo:lo + rp, :] = jnp.broadcast_to(beta_all[:, h:h + 1], (rp, LANE))

    def trip(cp, _slot):
        chains = [(cp * npar + cc, h) for cc in range(npar) for h in range(nh)]
        r0s = [c * CHUNK for c, _ in chains]
        qc = [q_s[h, r0:r0 + CHUNK, :] for (_, h), r0 in zip(chains, r0s)]
        kc = [k_s[h, r0:r0 + CHUNK, :] for (_, h), r0 in zip(chains, r0s)]
        vc = [v_s[h, r0:r0 + CHUNK, :] for (_, h), r0 in zip(chains, r0s)]
        cumc = [cumb_s[h, r0:r0 + CHUNK, :] for (_, h), r0 in zip(chains, r0s)]
        bc = [betab_s[h, r0:r0 + CHUNK, :] for (_, h), r0 in zip(chains, r0s)]
        n = len(chains)
        kk = [_mm_nt(kc[i], kc[i]) for i in range(n)]
        qk = [_mm_nt(qc[i], kc[i]) for i in range(n)]
        decay = []
        for i, (c, h) in enumerate(chains):
            diff = cumc[i][:, 0:CHUNK] - cumrow_s[c][nh + h:nh + h + 1, :]
            decay.append(jnp.where(incl, jnp.exp(jnp.where(incl, diff, 0.0)), 0.0))
        a = [jnp.where(strict, bc[i][:, 0:CHUNK] * kk[i] * decay[i], 0.0) for i in range(n)]
        x = [eye - a[i] for i in range(n)]
        p = [_mm(a[i], a[i]) for i in range(n)]
        for it in range(5):
            x = [x[i] + _mm(x[i], p[i]) for i in range(n)]
            if it < 4:
                p = [_mm(p[i], p[i]) for i in range(n)]
        ecum = [jnp.exp(cumc[i]) for i in range(n)]
        sol = [_mm(x[i], jnp.concatenate([vc[i] * bc[i], kc[i] * (bc[i] * ecum[i])], axis=1)) for i in range(n)]
        attn = [qk[i] * decay[i] for i in range(n)]
        cum_last = [cumc[i][CHUNK - 1:CHUNK, :] for i in range(n)]
        k_state = [kc[i] * jnp.exp(cum_last[i] - cumc[i]) for i in range(n)]
        wqi = [jnp.concatenate([sol[i][:, GDN_DV:GDN_DV + GDN_DK], qc[i] * ecum[i]], axis=0) for i in range(n)]
        for cc in range(npar):
            idx = [cc * nh + h for h in range(nh)]
            s = [state_ref[h] for h in range(nh)]
            wq = [_mm(wqi[i], s[h]) for h, i in enumerate(idx)]
            v_new = [sol[i][:, 0:GDN_DV] - wq[h][0:CHUNK, :] for h, i in enumerate(idx)]
            av = [_mm(attn[i], v_new[h]) for h, i in enumerate(idx)]
            upd = [_mm_tn(k_state[i], v_new[h]) for h, i in enumerate(idx)]
            for h, i in enumerate(idx):
                state_ref[h] = s[h] * jnp.exp(cum_last[i]) + upd[h]
                o_s[r0s[i]:r0s[i] + CHUNK, h * LANE:(h + 1) * LANE] = wq[h][CHUNK:2 * CHUNK, :] + av[h]

    def epilogue(t):
        lo = t * rp
        for h in range(nh):
            hs = slice(h * LANE, (h + 1) * LANE)
            o = o_s[lo:lo + rp, hs]
            oh = o * lax.rsqrt(jnp.mean(o * o, axis=-1, keepdims=True) + RMS_EPS) * nw_ref[...]
            o_ref[lo:lo + rp, hs] = (oh * _silu(gate_ref[lo:lo + rp, hs].astype(F32))).astype(o_ref.dtype)

    _schedule([(t,) for t in range(nc // npar)], prologue, trip, epilogue)
    tail_ref[...] = xbuf_ref[tb:tb + SUBLANE, :]


def _gdn_call(pb, pf, conv_w, par, norm_w, layer, batch, seq):
    tb = min(TB_MIX, seq)
    nb = seq // tb
    nc = tb // CHUNK
    t = batch * seq
    wq = 2 * GDN_QK + GDN_V
    kern = functools.partial(_gdn_kernel, tb=tb)
    return pl.pallas_call(
        kern,
        grid=(batch, nb),
        in_specs=[pl.BlockSpec((tb, wq), lambda b, i: (b * nb + i, A_Q // wq)),
                  pl.BlockSpec((tb, LANE), lambda b, i: (b * nb + i, F_AS // LANE)),
                  pl.BlockSpec((tb, GDN_V), lambda b, i: (b * nb + i, A_G // GDN_V)),
                  pl.BlockSpec((None, CONV_W, wq), lambda b, i: (layer, 0, 0)),
                  pl.BlockSpec((None, SUBLANE, LANE), lambda b, i: (layer, 0, 0)),
                  pl.BlockSpec((None, 1, GDN_DV), lambda b, i: (layer, 0, 0))],
        out_specs=pl.BlockSpec((tb, GDN_V), lambda b, i: (b * nb + i, 0)),
        out_shape=jax.ShapeDtypeStruct((t, GDN_V), BF16),
        scratch_shapes=[pltpu.VMEM((GDN_HEADS, GDN_DK, GDN_DV), F32),
                        pltpu.VMEM((SUBLANE, wq), F32),
                        pltpu.VMEM((tb + SUBLANE, wq), F32),
                        pltpu.VMEM((GDN_HEADS, tb, LANE), F32),
                        pltpu.VMEM((GDN_HEADS, tb, LANE), F32),
                        pltpu.VMEM((GDN_HEADS, tb, LANE), F32),
                        pltpu.VMEM((GDN_HEADS, tb, LANE), F32),
                        pltpu.VMEM((GDN_HEADS, tb, LANE), F32),
                        pltpu.VMEM((nc, SUBLANE, CHUNK), F32),
                        pltpu.VMEM((tb, GDN_V), F32)],
        compiler_params=_cparams(("parallel", "arbitrary")),
        name="gdn",
    )(pb, pf, pb, conv_w, par, norm_w)


def _dd_make_trip(q_s, k_s, v_s, la_s, o_s, state_ref, c8_s, p_s, g_heads):
    nblk = CHUNK // SUB
    dkh = LANE // g_heads
    dvp = g_heads * LANE
    npar = DD_NPAR
    row = lax.broadcasted_iota(jnp.int32, (CHUNK, CHUNK), 0)
    col = lax.broadcasted_iota(jnp.int32, (CHUNK, CHUNK), 1)
    level_masks = []
    for sh in (5, 4, 3):
        same2b = jnp.right_shift(row, sh + 1) == jnp.right_shift(col, sh + 1)
        upper = (jnp.right_shift(row, sh) & 1) == 1
        lower = (jnp.right_shift(col, sh) & 1) == 0
        level_masks.append(jnp.where(same2b, jnp.where(upper, jnp.where(lower, 1.0, 0.0), 0.0), 0.0))
    tri8 = jnp.where(jnp.right_shift(row, 3) == jnp.right_shift(col, 3),
                     jnp.where(col <= row, 1.0, 0.0), 0.0).astype(BF16)
    diag_mask = tri8.astype(F32)
    lane128 = lax.broadcasted_iota(jnp.int32, (CHUNK, LANE), 1)
    head_masks = [jnp.where((lane128 >= g * dkh) & (lane128 < (g + 1) * dkh), 1.0, 0.0)
                  for g in range(g_heads)]
    sub = lax.broadcasted_iota(jnp.int32, (SUB, LANE), 0)
    dk_sh = dkh.bit_length() - 1
    lane_sh = LANE.bit_length() - 1
    orow = lax.broadcasted_iota(jnp.int32, (LANE, dvp), 0)
    ocol = lax.broadcasted_iota(jnp.int32, (LANE, dvp), 1)
    ones_bd = jnp.where(jnp.right_shift(orow, dk_sh) == jnp.right_shift(ocol, lane_sh), 1.0, 0.0).astype(BF16)
    srow = lax.broadcasted_iota(jnp.int32, (dvp, LANE), 0)
    scol = lax.broadcasted_iota(jnp.int32, (dvp, LANE), 1)
    state_mask = jnp.where(jnp.right_shift(srow, lane_sh) == jnp.right_shift(scol, dk_sh), 1.0, 0.0)

    def block_sums(u, r0):
        c8 = [c8_s[u, r0 + b * SUB:r0 + (b + 1) * SUB, :] for b in range(nblk)]
        t8 = [c8_s[u, r0 + (b + 1) * SUB - 1:r0 + (b + 1) * SUB, :] for b in range(nblk)]
        t16 = [t8[2 * b] + t8[2 * b + 1] for b in range(nblk // 2)]
        t32 = [t16[2 * b] + t16[2 * b + 1] for b in range(nblk // 4)]
        t64 = t32[0] + t32[1]
        c16 = [c8[b] + t8[b - 1] if b % 2 else c8[b] for b in range(nblk)]
        c32 = [c16[b] + t16[b // 2 - 1] if (b // 2) % 2 else c16[b] for b in range(nblk)]
        c64 = [c32[b] + t32[0] if b >= nblk // 2 else c32[b] for b in range(nblk)]
        pre = {8: c8, 16: c16, 32: c32, 64: c64}
        suf = {8: [t8[b] - c8[b] for b in range(nblk)],
               16: [t16[b // 2] - c16[b] for b in range(nblk)],
               32: [t32[b // 4] - c32[b] for b in range(nblk)],
               64: [t64 - c64[b] for b in range(nblk)]}
        return pre, suf

    def cat(pieces):
        return jnp.concatenate(pieces, axis=0)

    def trip(u, t, slot):
        rng = range(npar)
        r0s = [(t * npar + i) * CHUNK for i in rng]
        qc = [q_s[u, r0:r0 + CHUNK, :] for r0 in r0s]
        kc = [k_s[u, r0:r0 + CHUNK, :] for r0 in r0s]
        vc = [v_s[u, r0:r0 + CHUNK, :] for r0 in r0s]
        c8_all = _mm_01(tri8, jnp.concatenate([la_s[u, r0:r0 + CHUNK, :] for r0 in r0s], axis=1))
        for i in rng:
            c8_s[u, r0s[i]:r0s[i] + CHUNK, :] = c8_all[:, i * LANE:(i + 1) * LANE]
        sums = [block_sums(u, r0s[i]) for i in rng]
        attn = [[None] * g_heads for _ in rng]
        for li, b in enumerate((32, 16, 8)):
            qs = [qc[i] * jnp.exp(cat(sums[i][0][b])) for i in rng]
            ks = [kc[i] * jnp.exp(cat(sums[i][1][b])) for i in rng]
            for g in range(g_heads):
                for i in rng:
                    qg = qs[i] * head_masks[g] if g_heads > 1 else qs[i]
                    term = _mm_nt(qg, ks[i]) * level_masks[li]
                    attn[i][g] = term if li == 0 else attn[i][g] + term
        for i in rng:
            dg = diag_terms(qc[i], kc[i], cat(sums[i][0][8]))
            for g in range(g_heads):
                attn[i][g] = attn[i][g] + dg[g]
        o = [apply(attn[i], vc[i]) for i in rng]
        q_inter = [qc[i] * jnp.exp(cat(sums[i][0][64])) for i in rng]
        k_state = [kc[i] * jnp.exp(cat(sums[i][1][64])) for i in rng]
        upd = [_mm_tn(vc[i], k_state[i]) for i in rng]
        st = state_ref[u]
        for i in rng:
            o_s[u, r0s[i]:r0s[i] + CHUNK, :] = o[i] + _mm_nt(q_inter[i], st)
            decay_last = jnp.exp(sums[i][0][64][nblk - 1][SUB - 1:SUB, :])
            st = st * decay_last + (upd[i] * state_mask if g_heads > 1 else upd[i])
        state_ref[u] = st

    def apply(att, v):
        if g_heads > 1:
            return jnp.concatenate([_mm(att[g], v[:, g * LANE:(g + 1) * LANE]) for g in range(g_heads)], axis=1)
        return _mm(att[0], v)

    def diag_terms(q, k, c8):
        qd = q * jnp.exp(c8)
        kd = k * jnp.exp(jnp.minimum(-c8, DD_SAFE_EXP))
        return [_mm_nt(qd * head_masks[g] if g_heads > 1 else qd, kd) * diag_mask for g in range(g_heads)]

    def fix_unit(u, nchunk):
        def body(c, carry):
            r0 = pl.multiple_of(c * CHUNK, CHUNK)
            rows = pl.ds(r0, CHUNK)
            fast = apply(diag_terms(q_s[u, rows, :], k_s[u, rows, :], c8_s[u, rows, :]), v_s[u, rows, :])
            for r in range(nblk):
                qr = q_s[u, pl.ds(r0 + r * SUB, SUB), :]
                cr = c8_s[u, pl.ds(r0 + r * SUB, SUB), :]
                for jj in range(SUB):
                    krow = k_s[u, pl.ds(r0 + r * SUB + jj, 1), :]
                    crow = c8_s[u, pl.ds(r0 + r * SUB + jj, 1), :]
                    pr = qr * krow * jnp.exp(jnp.where(sub >= jj, cr - crow, -jnp.inf))
                    p_s[(r * SUB + jj) * SUB:(r * SUB + jj + 1) * SUB, :] = pr
            rs = jnp.dot(p_s[...].astype(BF16), ones_bd, preferred_element_type=F32)
            od = []
            for r in range(nblk):
                acc = None
                for jj in range(SUB):
                    term = (rs[(r * SUB + jj) * SUB:(r * SUB + jj + 1) * SUB, :]
                            * v_s[u, pl.ds(r0 + r * SUB + jj, 1), :])
                    acc = term if acc is None else acc + term
                od.append(acc)
            o_s[u, rows, :] = o_s[u, rows, :] + (cat(od) - fast)
            return carry

        lax.fori_loop(0, nchunk, body, 0)

    return trip, fix_unit


def _dd_fix_if_unsafe(c8_s, fix_unit, nu, tb, units, epilogue):
    worst = None
    for u in range(nu):
        tot = -c8_s[u, pl.ds(SUB - 1, tb // SUB, stride=SUB), :]
        worst = tot if worst is None else jnp.maximum(worst, tot)

    @pl.when(jnp.max(worst) > DD_SAFE_EXP)
    def _():
        for u in range(nu):
            fix_unit(u, tb // CHUNK)
        for u, t in units:
            epilogue(u, t)


def _dd_scratch(nu, tb, dvp):
    return [pltpu.VMEM((nu, dvp, LANE), F32),
            pltpu.VMEM((nu, tb, LANE), F32), pltpu.VMEM((nu, tb, LANE), F32),
            pltpu.VMEM((nu, tb, dvp), F32),
            pltpu.VMEM((nu, tb, LANE), F32),
            pltpu.VMEM((nu, tb, dvp), F32),
            pltpu.VMEM((nu, tb, LANE), F32),
            pltpu.VMEM((CHUNK * SUB, LANE), F32)]


def _gla_kernel(q_ref, k_ref, v_ref, lr_ref, gate_ref, w2_ref, b2_ref, nw_ref,
                o_ref, state_ref, q_s, k_s, v_s, la_s, o_s, c8_s, p_s, *, tb):
    @pl.when(pl.program_id(1) == 0)
    def _():
        state_ref[...] = jnp.zeros_like(state_ref)

    npair = GLA_HEADS // 2
    rp = DD_NPAR * CHUNK
    trip, fix_unit = _dd_make_trip(q_s, k_s, v_s, la_s, o_s, state_ref, c8_s, p_s, 2)

    def prologue(u, t):
        lo = t * rp
        ls = slice(u * LANE, (u + 1) * LANE)
        vs = slice(u * 2 * LANE, (u + 1) * 2 * LANE)
        q_s[u, lo:lo + rp, :] = q_ref[lo:lo + rp, ls].astype(F32) * (GLA_DK ** -0.5)
        k_s[u, lo:lo + rp, :] = k_ref[lo:lo + rp, ls].astype(F32)
        v_s[u, lo:lo + rp, :] = v_ref[lo:lo + rp, vs].astype(F32)
        z = _mm(lr_ref[lo:lo + rp, :], w2_ref[:, ls]) + b2_ref[:, ls]
        la_s[u, lo:lo + rp, :] = _log_sigmoid(z) * (1.0 / GLA_NORMALIZER)

    def epilogue(u, t):
        lo = t * rp
        for g in range(2):
            hs = slice((2 * u + g) * LANE, (2 * u + g + 1) * LANE)
            o = o_s[u, lo:lo + rp, g * LANE:(g + 1) * LANE]
            oh = o * lax.rsqrt(jnp.mean(o * o, axis=-1, keepdims=True) + RMS_EPS) * nw_ref[...]
            o_ref[lo:lo + rp, hs] = (oh * _silu(gate_ref[lo:lo + rp, hs].astype(F32))).astype(o_ref.dtype)

    units = [(u, t) for u in range(npair) for t in range(tb // rp)]
    _schedule(units, prologue, trip, epilogue)
    _dd_fix_if_unsafe(c8_s, fix_unit, npair, tb, units, epilogue)


def _gla_call(pb, pf, w2p, b2, norm_w, layer, batch, seq):
    tb = min(TB_MIX, seq)
    nb = seq // tb
    t = batch * seq
    kern = functools.partial(_gla_kernel, tb=tb)
    return pl.pallas_call(
        kern,
        grid=(batch, nb),
        in_specs=[pl.BlockSpec((tb, GLA_QK), lambda b, i: (b * nb + i, B_Q // GLA_QK)),
                  pl.BlockSpec((tb, GLA_QK), lambda b, i: (b * nb + i, B_K // GLA_QK)),
                  pl.BlockSpec((tb, GLA_V), lambda b, i: (b * nb + i, B_V // GLA_V)),
                  pl.BlockSpec((tb, LANE), lambda b, i: (b * nb + i, F_LR // LANE)),
                  pl.BlockSpec((tb, GLA_V), lambda b, i: (b * nb + i, B_G // GLA_V)),
                  pl.BlockSpec((None, LANE, GLA_QK), lambda b, i: (layer, 0, 0)),
                  pl.BlockSpec((None, 1, GLA_QK), lambda b, i: (layer, 0, 0)),
                  pl.BlockSpec((None, 1, GLA_DV), lambda b, i: (layer, 0, 0))],
        out_specs=pl.BlockSpec((tb, GLA_V), lambda b, i: (b * nb + i, 0)),
        out_shape=jax.ShapeDtypeStruct((t, GLA_V), BF16),
        scratch_shapes=_dd_scratch(GLA_HEADS // 2, tb, 2 * LANE),
        compiler_params=_cparams(("parallel", "arbitrary")),
        name="gla",
    )(pb, pb, pb, pf, pb, w2p, b2, norm_w)


def _hgrn_kernel(q_ref, f_ref, v_ref, gate_ref, lbl_ref, nw_ref,
                 o_ref, state_ref, q_s, k_s, v_s, la_s, o_s, c8_s, p_s, *, tb, layer):
    @pl.when(pl.program_id(1) == 0)
    def _():
        state_ref[...] = jnp.zeros_like(state_ref)

    logits = lbl_ref[...]
    mx = jnp.max(logits, axis=0, keepdims=True)
    ex = jnp.exp(logits - mx)
    p = ex / jnp.sum(ex, axis=0, keepdims=True)
    acc = p[0:1, :]
    for r in range(1, layer + 1):
        acc = acc + p[r:r + 1, :]
    lb = jnp.clip(acc - p[0:1, :], 0.0, 1.0)
    log_lb = jnp.log(jnp.maximum(lb, LB_FLOOR))
    log_1m = jnp.log(1.0 - lb)

    rp = DD_NPAR * CHUNK
    trip, fix_unit = _dd_make_trip(q_s, k_s, v_s, la_s, o_s, state_ref, c8_s, p_s, 1)

    def prologue(u, t):
        lo = t * rp
        hs = slice(u * LANE, (u + 1) * LANE)
        cf = f_ref[lo:lo + rp, hs]
        second = log_1m[:, hs] + _log_sigmoid(cf)
        llb = log_lb[:, hs]
        la_s[u, lo:lo + rp, :] = jnp.maximum(llb, second) + jnp.log(1.0 + jnp.exp(-jnp.abs(llb - second)))
        k_s[u, lo:lo + rp, :] = (1.0 - lb[:, hs]) * _sigmoid_t(-cf)
        q_s[u, lo:lo + rp, :] = _silu(q_ref[lo:lo + rp, hs].astype(F32)) * (HGRN_EXPAND ** -0.5)
        v_s[u, lo:lo + rp, :] = v_ref[lo:lo + rp, hs].astype(F32)

    def epilogue(u, t):
        lo = t * rp
        hs = slice(u * LANE, (u + 1) * LANE)
        o = o_s[u, lo:lo + rp, :]
        oh = o * lax.rsqrt(jnp.mean(o * o, axis=-1, keepdims=True) + RMS_EPS) * nw_ref[...]
        o_ref[lo:lo + rp, hs] = (oh * _silu(gate_ref[lo:lo + rp, hs].astype(F32))).astype(o_ref.dtype)

    units = [(u, t) for u in range(HGRN_HEADS) for t in range(tb // rp)]
    _schedule(units, prologue, trip, epilogue)
    _dd_fix_if_unsafe(c8_s, fix_unit, HGRN_HEADS, tb, units, epilogue)


def _hgrn_call(pb, pf, lb_logits, norm_w, layer, batch, seq):
    tb = min(TB_MIX, seq)
    nb = seq // tb
    t = batch * seq
    kern = functools.partial(_hgrn_kernel, tb=tb, layer=layer)
    return pl.pallas_call(
        kern,
        grid=(batch, nb),
        in_specs=[pl.BlockSpec((tb, HGRN_QK), lambda b, i: (b * nb + i, C_Q // HGRN_QK)),
                  pl.BlockSpec((tb, HGRN_QK), lambda b, i: (b * nb + i, F_CF // HGRN_QK)),
                  pl.BlockSpec((tb, HGRN_V), lambda b, i: (b * nb + i, C_I // HGRN_V)),
                  pl.BlockSpec((tb, HGRN_V), lambda b, i: (b * nb + i, C_G // HGRN_V)),
                  pl.BlockSpec((DEPTH, HGRN_QK), lambda b, i: (0, 0)),
                  pl.BlockSpec((None, 1, HGRN_DV), lambda b, i: (layer, 0, 0))],
        out_specs=pl.BlockSpec((tb, HGRN_V), lambda b, i: (b * nb + i, 0)),
        out_shape=jax.ShapeDtypeStruct((t, HGRN_V), BF16),
        scratch_shapes=_dd_scratch(HGRN_HEADS, tb, LANE),
        compiler_params=_cparams(("parallel", "arbitrary")),
        name="hgrn",
    )(pb, pf, pb, pb, lb_logits, norm_w)


def _merge_kernel(oa_ref, ob_ref, oc_ref, ma_ref, mb_ref, mc_ref, h_ref,
                  wa_ref, wb_ref, wc_ref, wo_ref, g_ref, b_ref, ox_ref, *, tm):
    y = (_sigmoid_t(ma_ref[...].astype(F32)) * jnp.dot(oa_ref[...], wa_ref[...], preferred_element_type=F32)
         + _sigmoid_t(mb_ref[...].astype(F32)) * jnp.dot(ob_ref[...], wb_ref[...], preferred_element_type=F32)
         + _sigmoid_t(mc_ref[...].astype(F32)) * jnp.dot(oc_ref[...], wc_ref[...], preferred_element_type=F32))
    mix = _mm(y, wo_ref[...])
    _to_token_major(ox_ref, _layer_norm(ALPHA * h_ref[...] + mix, g_ref[...], b_ref[...]), tm)


def _merge_call(o_a, o_b, o_c, pb, h, wa, wb, wc, wo, g, b, layer):
    t, d = h.shape
    tm = min(TM_ROWS, t)

    def row(width):
        return pl.BlockSpec((tm, width), lambda i: (i, 0))

    def wspec(kdim):
        return pl.BlockSpec((None, kdim, d), lambda i: (layer, 0, 0))

    vec = pl.BlockSpec((None, 1, d), lambda i: (layer, 0, 0))
    return pl.pallas_call(
        functools.partial(_merge_kernel, tm=tm),
        grid=(t // tm,),
        in_specs=[row(GDN_V), row(GLA_V), row(HGRN_V),
                  pl.BlockSpec((tm, d), lambda i: (i, M_A // d)),
                  pl.BlockSpec((tm, d), lambda i: (i, M_B // d)),
                  pl.BlockSpec((tm, d), lambda i: (i, M_C // d)),
                  row(d), wspec(GDN_V), wspec(GLA_V), wspec(HGRN_V), wspec(d), vec, vec],
        out_specs=pl.BlockSpec((tm * NPIECE, LANE), lambda i: (i, 0)),
        out_shape=jax.ShapeDtypeStruct((t * NPIECE, LANE), F32),
        compiler_params=_cparams(("parallel",)),
        name="merge",
    )(o_a, o_b, o_c, pb, pb, pb, h, wa, wb, wc, wo, g, b)


def _route_pairs(scores_t, bias_ref):
    s = [scores_t[e:e + 1, :] for e in range(N_EXPERTS)]
    sel = [s[e] + bias_ref[e:e + 1, 0:1] for e in range(N_EXPERTS)]
    gscore = []
    for g in range(N_GROUPS):
        a, b, c, d = sel[4 * g:4 * g + 4]
        hi1, lo1 = jnp.maximum(a, b), jnp.minimum(a, b)
        hi2, lo2 = jnp.maximum(c, d), jnp.minimum(c, d)
        top1 = jnp.maximum(hi1, hi2)
        top2 = jnp.maximum(jnp.minimum(hi1, hi2), jnp.maximum(lo1, lo2))
        gscore.append(top1 + top2)
    best = gscore[0]
    gidx = jnp.zeros_like(best, dtype=jnp.int32)
    for g in range(1, N_GROUPS):
        take = gscore[g] > best
        best = jnp.where(take, gscore[g], best)
        gidx = jnp.where(take, g, gidx)
    ing, raw = [], []
    for kk in range(EXPERTS_PER_GROUP):
        vs, vr = sel[kk], s[kk]
        for g in range(1, N_GROUPS):
            pick = gidx == g
            vs = jnp.where(pick, sel[4 * g + kk], vs)
            vr = jnp.where(pick, s[4 * g + kk], vr)
        ing.append(vs)
        raw.append(vr)
    b1 = ing[0]
    i1 = jnp.zeros_like(gidx)
    for kk in range(1, EXPERTS_PER_GROUP):
        take = ing[kk] > b1
        b1 = jnp.where(take, ing[kk], b1)
        i1 = jnp.where(take, kk, i1)
    neg = jnp.full_like(b1, -jnp.inf)
    b2 = neg
    i2 = jnp.zeros_like(gidx)
    for kk in range(EXPERTS_PER_GROUP):
        cand = jnp.where(i1 == kk, neg, ing[kk])
        take = cand > b2
        b2 = jnp.where(take, cand, b2)
        i2 = jnp.where(take, kk, i2)
    w1 = raw[0]
    w2 = raw[0]
    for kk in range(1, EXPERTS_PER_GROUP):
        w1 = jnp.where(i1 == kk, raw[kk], w1)
        w2 = jnp.where(i2 == kk, raw[kk], w2)
    tot = w1 + w2
    w1 = w1 / tot
    w2 = w2 / tot
    first_lower = i1 < i2
    lo = jnp.where(first_lower, i1, i2)
    hi = jnp.where(first_lower, i2, i1)
    pidx = jnp.where(lo == 0, hi - 1, jnp.where(lo == 1, hi + 1, 5))
    bkt = gidx * 6 + pidx
    return bkt, jnp.where(first_lower, w1, w2), jnp.where(first_lower, w2, w1)


def _moe_route_kernel(hx_ref, wr_ref, rb_ref, bkt_ref, rank_ref, wab_ref, cnt_ref, carry_ref, *, tm):
    @pl.when(pl.program_id(0) == 0)
    def _():
        carry_ref[...] = jnp.zeros_like(carry_ref)

    logits_t = _mm_nt_f32(wr_ref[...], _from_token_major(hx_ref, tm))
    bkt, wa, wb = _route_pairs(_sigmoid(logits_t), rb_ref)
    sub = lax.broadcasted_iota(jnp.int32, (32, tm), 0)
    oh = jnp.where(sub == bkt, 1.0, 0.0)
    r = lax.broadcasted_iota(jnp.int32, (tm, tm), 0)
    c = lax.broadcasted_iota(jnp.int32, (tm, tm), 1)
    earlier = jnp.where(r < c, 1.0, 0.0).astype(BF16)
    before = jnp.dot(oh.astype(BF16), earlier, preferred_element_type=F32)
    carry = carry_ref[...]
    rank = jnp.sum(oh * (before + carry[:, 0:1]), axis=0, keepdims=True)
    carry = carry + jnp.sum(oh, axis=1, keepdims=True)
    carry_ref[...] = carry
    cnt_ref[...] = carry
    bkt_ref[...] = bkt
    rank_ref[...] = rank.astype(jnp.int32)
    pad = jnp.zeros((LANE - 2, tm), F32)
    wab_ref[...] = jnp.transpose(jnp.concatenate([wa, wb, pad], axis=0))


def _moe_tables_kernel(cnt_ref, bkt_ref, rank_ref, pos_ref, tabs_ref, *, t, tr):
    cnt = cnt_ref[...]
    sz = jnp.floor((cnt + (tr - 1)) * (1.0 / tr)) * tr
    r = lax.broadcasted_iota(jnp.int32, (32, 32), 0)
    c = lax.broadcasted_iota(jnp.int32, (32, 32), 1)
    start = jnp.dot(jnp.where(c < r, 1.0, 0.0), sz, preferred_element_type=F32, precision=HIGHEST)
    end = start + sz
    sub = lax.broadcasted_iota(jnp.int32, (32, t), 0)
    pos = jnp.sum(jnp.where(sub == bkt_ref[...], start[:, 0:1], 0.0), axis=0, keepdims=True)
    pos_ref[...] = pos.astype(jnp.int32) + rank_ref[...]
    brow = lax.broadcasted_iota(jnp.int32, (32, LANE), 0)
    tile0 = lax.broadcasted_iota(jnp.int32, (32, LANE), 1).astype(F32) * tr
    tbk = jnp.sum(jnp.where((brow < MOE_NBKT) & (end <= tile0), 1, 0), axis=0, keepdims=True)
    tbk = jnp.minimum(tbk, MOE_NBKT - 1)
    total = end[MOE_NBKT - 1:MOE_NBKT, :]
    valid = jnp.where(tile0[0:1, :] < total, 1, 0)
    g = jnp.where(tbk >= 6, 1, 0) + jnp.where(tbk >= 12, 1, 0) + jnp.where(tbk >= 18, 1, 0)
    p = tbk - 6 * g
    ge3 = jnp.where(p >= 3, 1, 0)
    ge5 = jnp.where(p >= 5, 1, 0)
    ea = 4 * g + ge3 + ge5
    eb = 4 * g + p + 1 - 2 * ge3 - ge5
    zero = jnp.zeros((SUBLANE - 3, LANE), jnp.int32)
    tabs_ref[...] = jnp.concatenate([ea, eb, valid, zero], axis=0)


def _moe_group_kernel(ea_ref, eb_ref, vd_ref, xs_ref, ws_ref, wga_ref, wua_ref, wda_ref,
                      wgb_ref, wub_ref, wdb_ref, ys_ref, wg_s, wu_s, wd_s, *, tr):
    j = pl.program_id(0)
    prev = jnp.maximum(j - 1, 0)

    @pl.when((j == 0) | (ea_ref[j] != ea_ref[prev]))
    def _():
        wg_s[0] = wga_ref[...].astype(BF16)
        wu_s[0] = wua_ref[...].astype(BF16)
        wd_s[0] = wda_ref[...].astype(BF16)

    @pl.when((j == 0) | (eb_ref[j] != eb_ref[prev]))
    def _():
        wg_s[1] = wgb_ref[...].astype(BF16)
        wu_s[1] = wub_ref[...].astype(BF16)
        wd_s[1] = wdb_ref[...].astype(BF16)

    @pl.when(vd_ref[j] > 0)
    def _():
        x = _from_token_major(xs_ref, tr).astype(BF16)
        w = ws_ref[...]

        def ffn(s, cw):
            hg = jnp.dot(x, wg_s[s], preferred_element_type=F32)
            hu = jnp.dot(x, wu_s[s], preferred_element_type=F32)
            hid = _silu(hg) * hu * cw
            return jnp.dot(hid.astype(BF16), wd_s[s], preferred_element_type=F32)

        y = ffn(0, w[:, 0:1]) + ffn(1, w[:, 1:2])
        _to_token_major(ys_ref, y, tr)


def _moe_ln_kernel(hx_ref, y_ref, g_ref, b_ref, o_ref, ob_ref, *, tm):
    out = _layer_norm(ALPHA * _from_token_major(hx_ref, tm) + _from_token_major(y_ref, tm),
                      g_ref[...], b_ref[...])
    o_ref[...] = out
    ob_ref[...] = out.astype(BF16)


def _sc_mesh_info():
    info = plsc.get_sparse_core_info()
    mesh = plsc.VectorSubcoreMesh(core_axis_name="c", subcore_axis_name="s")
    return mesh, info.num_cores, info.num_subcores


def _sc_scatter_rows(x3, w2, pos, n_rows):
    t = x3.shape[0]
    mesh, nc, ns = _sc_mesh_info()
    per_w = t // (nc * ns)

    @functools.partial(
        pl.kernel, mesh=mesh,
        out_type=[jax.ShapeDtypeStruct((n_rows,) + x3.shape[1:], x3.dtype),
                  jax.ShapeDtypeStruct((n_rows,) + w2.shape[1:], w2.dtype)],
        scratch_types=[pltpu.VMEM((SC_CH,), jnp.int32), pltpu.VMEM((SC_CH,) + x3.shape[1:], x3.dtype),
                       pltpu.VMEM((SC_CH,) + w2.shape[1:], w2.dtype),
                       pltpu.SemaphoreType.DMA, pltpu.SemaphoreType.DMA])
    def k(x_hbm, w_hbm, idx_hbm, ox_hbm, ow_hbm, idx_v, rows_v, wrows_v, sem_x, sem_w):
        base = (lax.axis_index("s") * nc + lax.axis_index("c")) * per_w

        @pl.loop(0, per_w // SC_CH)
        def _(j):
            off = base + j * SC_CH
            pltpu.sync_copy(idx_hbm.at[pl.ds(off, SC_CH)], idx_v)
            pltpu.sync_copy(x_hbm.at[pl.ds(off, SC_CH)], rows_v)
            pltpu.sync_copy(w_hbm.at[pl.ds(off, SC_CH)], wrows_v)
            cx = pltpu.async_copy(rows_v, ox_hbm.at[idx_v], sem_x)
            cw = pltpu.async_copy(wrows_v, ow_hbm.at[idx_v], sem_w)
            cx.wait()
            cw.wait()

    return k(x3, w2, pos)


def _sc_gather_rows(y3, pos):
    t = pos.shape[0]
    mesh, nc, ns = _sc_mesh_info()
    per_w = t // (nc * ns)
    ch = SC_CH // 2

    @functools.partial(
        pl.kernel, mesh=mesh,
        out_type=jax.ShapeDtypeStruct((t,) + y3.shape[1:], y3.dtype),
        scratch_types=[pltpu.VMEM((ch,), jnp.int32), pltpu.VMEM((ch,), jnp.int32),
                       pltpu.VMEM((ch,) + y3.shape[1:], y3.dtype), pltpu.VMEM((ch,) + y3.shape[1:], y3.dtype)]
        + [pltpu.SemaphoreType.DMA] * 4)
    def k(y_hbm, idx_hbm, o_hbm, idx_a, idx_b, rows_a, rows_b, sg_a, sg_b, sw_a, sw_b):
        base = (lax.axis_index("s") * nc + lax.axis_index("c")) * per_w

        @pl.loop(0, per_w // (2 * ch))
        def _(j):
            off_a = base + j * 2 * ch
            off_b = off_a + ch
            pltpu.sync_copy(idx_hbm.at[pl.ds(off_a, ch)], idx_a)
            ga = pltpu.async_copy(y_hbm.at[idx_a], rows_a, sg_a)
            pltpu.sync_copy(idx_hbm.at[pl.ds(off_b, ch)], idx_b)
            gb = pltpu.async_copy(y_hbm.at[idx_b], rows_b, sg_b)
            ga.wait()
            wa = pltpu.async_copy(rows_a, o_hbm.at[pl.ds(off_a, ch)], sw_a)
            gb.wait()
            wb = pltpu.async_copy(rows_b, o_hbm.at[pl.ds(off_b, ch)], sw_b)
            wa.wait()
            wb.wait()

    return k(y3, pos)


def _moe_sparse(hx, wr_t, rbias, wg, wu, wd, g, b, layer):
    t = hx.shape[0] // NPIECE
    d = D_MODEL
    tm = min(TM_MOE, t)
    tr = MOE_TR
    nt = t // tr + MOE_NBKT
    n_rows = nt * tr
    row1 = pl.BlockSpec((1, tm), lambda i: (0, i))
    rows_tm = pl.BlockSpec((tm * NPIECE, LANE), lambda i: (i, 0))
    bkt, rank, wab, cnt = pl.pallas_call(
        functools.partial(_moe_route_kernel, tm=tm),
        grid=(t // tm,),
        in_specs=[rows_tm,
                  pl.BlockSpec((N_EXPERTS, d), lambda i: (0, 0)),
                  pl.BlockSpec((N_EXPERTS, LANE), lambda i: (0, 0))],
        out_specs=[row1, row1, pl.BlockSpec((tm, LANE), lambda i: (i, 0)),
                   pl.BlockSpec((32, LANE), lambda i: (0, 0))],
        out_shape=[jax.ShapeDtypeStruct((1, t), jnp.int32), jax.ShapeDtypeStruct((1, t), jnp.int32),
                   jax.ShapeDtypeStruct((t, LANE), F32), jax.ShapeDtypeStruct((32, LANE), F32)],
        scratch_shapes=[pltpu.VMEM((32, LANE), F32)],
        compiler_params=_cparams(("arbitrary",)),
        name="moe_route",
    )(hx, wr_t, rbias)
    pos, tabs = pl.pallas_call(
        functools.partial(_moe_tables_kernel, t=t, tr=tr),
        out_shape=[jax.ShapeDtypeStruct((1, t), jnp.int32), jax.ShapeDtypeStruct((SUBLANE, LANE), jnp.int32)],
        compiler_params=pltpu.CompilerParams(vmem_limit_bytes=VMEM_LIMIT),
        name="moe_tables",
    )(cnt, bkt, rank)
    pos = pos.reshape(t)
    xs3, ws = _sc_scatter_rows(hx.reshape(t, NPIECE, LANE), wab, pos, n_rows)

    def wspec(which, shape):
        if which == 0:
            return pl.BlockSpec((None, None) + shape, lambda j, ea, eb, vd: (layer, ea[j], 0, 0))
        return pl.BlockSpec((None, None) + shape, lambda j, ea, eb, vd: (layer, eb[j], 0, 0))

    ys = pl.pallas_call(
        functools.partial(_moe_group_kernel, tr=tr),
        grid_spec=pltpu.PrefetchScalarGridSpec(
            num_scalar_prefetch=3,
            grid=(nt,),
            in_specs=[pl.BlockSpec((tr * NPIECE, LANE), lambda j, ea, eb, vd: (j, 0)),
                      pl.BlockSpec((tr, LANE), lambda j, ea, eb, vd: (j, 0)),
                      wspec(0, (d, D_FF)), wspec(0, (d, D_FF)), wspec(0, (D_FF, d)),
                      wspec(1, (d, D_FF)), wspec(1, (d, D_FF)), wspec(1, (D_FF, d))],
            out_specs=pl.BlockSpec((tr * NPIECE, LANE), lambda j, ea, eb, vd: (j, 0)),
            scratch_shapes=[pltpu.VMEM((2, d, D_FF), BF16), pltpu.VMEM((2, d, D_FF), BF16),
                            pltpu.VMEM((2, D_FF, d), BF16)]),
        out_shape=jax.ShapeDtypeStruct((n_rows * NPIECE, LANE), F32),
        compiler_params=_cparams(("arbitrary",)),
        name="moe_experts",
    )(tabs[0, :nt], tabs[1, :nt], tabs[2, :nt], xs3.reshape(n_rows * NPIECE, LANE), ws,
      wg, wu, wd, wg, wu, wd)
    y3 = _sc_gather_rows(ys.reshape(n_rows, NPIECE, LANE), pos)
    vec = pl.BlockSpec((None, 1, d), lambda i: (layer, 0, 0))
    row = pl.BlockSpec((tm, d), lambda i: (i, 0))
    return pl.pallas_call(
        functools.partial(_moe_ln_kernel, tm=tm),
        grid=(t // tm,),
        in_specs=[rows_tm, rows_tm, vec, vec],
        out_specs=[row, row],
        out_shape=[jax.ShapeDtypeStruct((t, d), F32), jax.ShapeDtypeStruct((t, d), BF16)],
        compiler_params=_cparams(("parallel",)),
        name="moe_ln",
    )(hx, y3.reshape(t * NPIECE, LANE), g, b)


def _pack_w_in(w_in):
    wt = jnp.swapaxes(w_in, 1, 2)
    (a_q, a_k, a_v, a_beta, a_dt, a_g, b_q, b_k, b_v, b_lr, b_g,
     c_q, c_f, c_i, c_g, m_a, m_b, m_c) = jnp.split(wt, SPLIT_POINTS, axis=1)
    depth, _, d = wt.shape
    a_s = jnp.concatenate([a_beta, a_dt, jnp.zeros((depth, LANE - 2 * GDN_HEADS, d), wt.dtype)], 1)
    b_lrp = jnp.concatenate([b_lr, jnp.zeros((depth, LANE - GLA_RANK, d), wt.dtype)], 1)
    wb = jnp.concatenate([m_a, m_b, m_c, a_q, a_k, a_v, a_g, b_q, b_k, b_v, b_g, c_q, c_i, c_g], 1).astype(BF16)
    wf = jnp.concatenate([c_f, a_s, b_lrp], 1).astype(BF16)
    assert wb.shape[1] == NPB and wf.shape[1] == NPF
    return wb, wf


def _prepare(w_in, gdn_conv, gdn_a_log, gdn_dt_bias, gdn_norm, gla_w2, gla_b2, gla_norm, hgrn_lb_logits,
             hgrn_norm, w_br_a, w_br_b, w_br_c, w_out, ln1_g, ln1_b, w_router, router_bias, w_gate, w_up,
             w_down, ln2_g, ln2_b):
    depth = w_in.shape[0]
    d = w_out.shape[-1]
    w_pb, w_pf = _pack_w_in(w_in)
    return dict(
        w_pb=w_pb, w_pf=w_pf,
        gdn_conv=gdn_conv,
        gdn_par=jnp.pad(jnp.stack([gdn_a_log, gdn_dt_bias], axis=1),
                        ((0, 0), (0, SUBLANE - 2), (GDN_HEADS, LANE - 2 * GDN_HEADS))),
        gdn_norm=gdn_norm.reshape(depth, 1, GDN_DV),
        w2p=jnp.concatenate([gla_w2, jnp.zeros((depth, LANE - GLA_RANK, GLA_QK), gla_w2.dtype)], axis=1),
        gla_b2=gla_b2.reshape(depth, 1, GLA_QK),
        gla_norm=gla_norm.reshape(depth, 1, GLA_DV),
        lb_logits=hgrn_lb_logits,
        hgrn_norm=hgrn_norm.reshape(depth, 1, HGRN_DV),
        wa=w_br_a.astype(BF16), wb=w_br_b.astype(BF16), wc=w_br_c.astype(BF16), wo=w_out.astype(BF16),
        ln1_g=ln1_g.reshape(depth, 1, d), ln1_b=ln1_b.reshape(depth, 1, d),
        wr_t=jnp.transpose(w_router),
        rbias=jnp.broadcast_to(router_bias[:, None], (N_EXPERTS, LANE)),
        wg=w_gate, wu=w_up, wd=w_down,
        ln2_g=ln2_g.reshape(depth, 1, d), ln2_b=ln2_b.reshape(depth, 1, d),
    )


def _mixer_block(h, hb, p, layer, batch, seq):
    pb = _inproj_call(hb, p["w_pb"], layer, BF16, TM_PROJ, TN_PROJ, "inproj_b")
    pf = _inproj_call(hb, p["w_pf"], layer, F32, TM_PROJ // 2, NPF, "inproj_f")
    o_a = _gdn_call(pb, pf, p["gdn_conv"], p["gdn_par"], p["gdn_norm"], layer, batch, seq)
    o_b = _gla_call(pb, pf, p["w2p"], p["gla_b2"], p["gla_norm"], layer, batch, seq)
    o_c = _hgrn_call(pb, pf, p["lb_logits"], p["hgrn_norm"], layer, batch, seq)
    return _merge_call(o_a, o_b, o_c, pb, h, p["wa"], p["wb"], p["wc"], p["wo"], p["ln1_g"], p["ln1_b"], layer)


def _ffn_block(hx, p, layer):
    return _moe_sparse(hx, p["wr_t"], p["rbias"], p["wg"], p["wu"], p["wd"], p["ln2_g"], p["ln2_b"], layer)


def kernel(x, ln0_g, ln0_b, w_in, gdn_conv, gdn_a_log, gdn_dt_bias, gdn_norm, gla_w2, gla_b2, gla_norm,
           hgrn_lb_logits, hgrn_norm, w_br_a, w_br_b, w_br_c, w_out, ln1_g, ln1_b, w_router, router_bias,
           w_gate, w_up, w_down, ln2_g, ln2_b):
    batch, seq, d = x.shape
    p = _prepare(w_in, gdn_conv, gdn_a_log, gdn_dt_bias, gdn_norm, gla_w2, gla_b2, gla_norm, hgrn_lb_logits,
                 hgrn_norm, w_br_a, w_br_b, w_br_c, w_out, ln1_g, ln1_b, w_router, router_bias, w_gate, w_up,
                 w_down, ln2_g, ln2_b)
    h, hb = _ln_call(x.reshape(batch * seq, d), ln0_g, ln0_b)
    for layer in range(w_in.shape[0]):
        hx = _mixer_block(h, hb, p, layer, batch, seq)
        h, hb = _ffn_block(hx, p, layer)
    return h.reshape(batch, seq, d)
```

```python
import functools

import numpy as np
import jax
import jax.numpy as jnp
from jax import lax
from jax.experimental import pallas as pl
from jax.experimental.pallas import tpu as pltpu
from jax.experimental.pallas import tpu_sc as plsc

F32 = jnp.float32
BF16 = jnp.bfloat16
HIGHEST = lax.Precision.HIGHEST

D_MODEL = 1024
DEPTH = 4
CHUNK = 64
GDN_HEADS, GDN_DK, GDN_DV, CONV_W = 4, 128, 128, 4
GLA_HEADS, GLA_DK, GLA_DV, GLA_RANK, GLA_NORMALIZER = 4, 64, 128, 16, 16.0
HGRN_HEADS, HGRN_EXPAND, HGRN_DV = 4, 128, 128
LB_FLOOR = 1e-30
N_EXPERTS, N_GROUPS, TOP_K, D_FF = 16, 4, 2, 256
EXPERTS_PER_GROUP = N_EXPERTS // N_GROUPS
ALPHA = (2.0 * DEPTH) ** 0.25
LN_EPS = 1e-5
RMS_EPS = 1e-6

GDN_QK = GDN_HEADS * GDN_DK
GDN_V = GDN_HEADS * GDN_DV
GLA_QK = GLA_HEADS * GLA_DK
GLA_V = GLA_HEADS * GLA_DV
HGRN_QK = HGRN_HEADS * HGRN_EXPAND
HGRN_V = HGRN_HEADS * HGRN_DV
SPLIT_SIZES = (GDN_QK, GDN_QK, GDN_V, GDN_HEADS, GDN_HEADS, GDN_V,
               GLA_QK, GLA_QK, GLA_V, GLA_RANK, GLA_V,
               HGRN_QK, HGRN_QK, HGRN_V, HGRN_V,
               D_MODEL, D_MODEL, D_MODEL)
SPLIT_POINTS = tuple(int(v) for v in np.cumsum(SPLIT_SIZES)[:-1])

LANE = 128
SUBLANE = 8
VMEM_LIMIT = 48 * 1024 * 1024

M_A, M_B, M_C = 0, 1024, 2048
A_Q, A_K, A_V, A_G = 3072, 3584, 4096, 4608
B_Q, B_K, B_V, B_G = 5120, 5376, 5632, 6144
C_Q, C_I, C_G = 6656, 7168, 7680
NPB = 8192
F_CF, F_AS, F_LR = 0, 512, 640
NPF = 768

TB_MIX = 1024
TM_ROWS = 512
TM_MOE = 1024
TM_PROJ, TN_PROJ = 2048, 2048
SUB = 8
DD_NPAR = 16
DD_SAFE_EXP = 60.0
GDN_NPAR = 8
MOE_TR = 512
MOE_NBKT = N_GROUPS * 6
NPIECE = D_MODEL // LANE
SC_CH = 64


def _cparams(sem):
    return pltpu.CompilerParams(dimension_semantics=sem, vmem_limit_bytes=VMEM_LIMIT)


def _mm(a, b):
    return jnp.dot(a.astype(BF16), b.astype(BF16), preferred_element_type=F32)


def _mm_nt(a, b):
    return lax.dot_general(a.astype(BF16), b.astype(BF16), (((1,), (1,)), ((), ())),
                           preferred_element_type=F32)


def _mm_tn(a, b):
    return lax.dot_general(a.astype(BF16), b.astype(BF16), (((0,), (0,)), ((), ())),
                           preferred_element_type=F32)


def _mm_nt_f32(a, b):
    a_hi = a.astype(BF16)
    a_mid = (a - a_hi.astype(F32)).astype(BF16)
    b_hi = b.astype(BF16)
    b_mid = (b - b_hi.astype(F32)).astype(BF16)
    dn = (((1,), (1,)), ((), ()))
    return (lax.dot_general(a_hi, b_hi, dn, preferred_element_type=F32)
            + lax.dot_general(a_hi, b_mid, dn, preferred_element_type=F32)
            + lax.dot_general(a_mid, b_hi, dn, preferred_element_type=F32))


def _split3(x):
    hi = x.astype(BF16)
    r = x - hi.astype(F32)
    mid = r.astype(BF16)
    lo = (r - mid.astype(F32)).astype(BF16)
    return hi, mid, lo


def _mm_01(m01, x):
    hi, mid, lo = _split3(x)
    return (jnp.dot(m01, hi, preferred_element_type=F32) + jnp.dot(m01, mid, preferred_element_type=F32)
            + jnp.dot(m01, lo, preferred_element_type=F32))


def _sigmoid(x):
    return 1.0 / (1.0 + jnp.exp(-x))


def _sigmoid_t(x):
    return 0.5 * jnp.tanh(0.5 * x) + 0.5


def _silu(x):
    h = 0.5 * x
    return h + h * jnp.tanh(h)


def _softplus(x):
    return jnp.maximum(x, 0.0) + jnp.log(1.0 + jnp.exp(-jnp.abs(x)))


def _log_sigmoid(x):
    return -_softplus(-x)


def _layer_norm(x, g, b):
    mu = jnp.mean(x, axis=-1, keepdims=True)
    xc = x - mu
    var = jnp.mean(xc * xc, axis=-1, keepdims=True)
    return xc * lax.rsqrt(var + LN_EPS) * g + b


def _to_token_major(ref, x, rows):
    for j in range(NPIECE):
        ref[pl.ds(j, rows, stride=NPIECE), :] = x[:, j * LANE:(j + 1) * LANE]


def _from_token_major(ref, rows):
    return jnp.concatenate([ref[pl.ds(j, rows, stride=NPIECE), :] for j in range(NPIECE)], axis=1)


def _ln_kernel(x_ref, g_ref, b_ref, o_ref, ob_ref):
    y = _layer_norm(x_ref[...], g_ref[...], b_ref[...])
    o_ref[...] = y
    ob_ref[...] = y.astype(BF16)


def _ln_call(x, g, b):
    t, d = x.shape
    tm = min(TM_ROWS, t)
    return pl.pallas_call(
        _ln_kernel,
        grid=(t // tm,),
        in_specs=[pl.BlockSpec((tm, d), lambda i: (i, 0)),
                  pl.BlockSpec((1, d), lambda i: (0, 0)),
                  pl.BlockSpec((1, d), lambda i: (0, 0))],
        out_specs=[pl.BlockSpec((tm, d), lambda i: (i, 0)), pl.BlockSpec((tm, d), lambda i: (i, 0))],
        out_shape=[jax.ShapeDtypeStruct((t, d), F32), jax.ShapeDtypeStruct((t, d), BF16)],
        compiler_params=_cparams(("parallel",)),
        name="ln0",
    )(x, g.reshape(1, d), b.reshape(1, d))


def _inproj_kernel(x_ref, wt_ref, o_ref):
    o_ref[...] = lax.dot_general(x_ref[...], wt_ref[...], (((1,), (1,)), ((), ())),
                                 preferred_element_type=F32).astype(o_ref.dtype)


def _inproj_call(hb, w, layer, out_dtype, tm, tn, name):
    t, d = hb.shape
    tm = min(tm, t)
    n = w.shape[-2]
    return pl.pallas_call(
        _inproj_kernel,
        grid=(n // tn, t // tm),
        in_specs=[pl.BlockSpec((tm, d), lambda j, i: (i, 0)),
                  pl.BlockSpec((None, tn, d), lambda j, i: (layer, j, 0))],
        out_specs=pl.BlockSpec((tm, tn), lambda j, i: (i, j)),
        out_shape=jax.ShapeDtypeStruct((t, n), out_dtype),
        compiler_params=_cparams(("parallel", "parallel")),
        name=name,
    )(hb, w)


def _schedule(units, prologue, trip, epilogue):
    prologue(*units[0])
    for n, u in enumerate(units):
        if n + 1 < len(units):
            prologue(*units[n + 1])
        trip(*u, n % 2)
        epilogue(*u)


def _gdn_kernel(qkv_ref, s_ref, gate_ref, cw_ref, par_ref, nw_ref, o_ref,
                state_ref, tail_ref, xbuf_ref, q_s, k_s, v_s, cumb_s, betab_s, cumrow_s, o_s, *, tb):
    nc = tb // CHUNK
    nh = GDN_HEADS
    off0 = SUBLANE - (CONV_W - 1)

    @pl.when(pl.program_id(1) == 0)
    def _():
        state_ref[...] = jnp.zeros_like(state_ref)
        tail_ref[...] = jnp.zeros_like(tail_ref)

    row = lax.broadcasted_iota(jnp.int32, (CHUNK, CHUNK), 0)
    col = lax.broadcasted_iota(jnp.int32, (CHUNK, CHUNK), 1)
    incl = col <= row
    strict = col < row
    tri = jnp.where(incl, 1.0, 0.0).astype(BF16)
    eye = jnp.where(col == row, 1.0, 0.0).astype(F32)
    npar = GDN_NPAR
    rp = npar * CHUNK
    xbuf_ref[0:SUBLANE, :] = tail_ref[...]

    def prologue(t):
        lo = t * rp
        xbuf_ref[SUBLANE + lo:SUBLANE + lo + rp, :] = qkv_ref[lo:lo + rp, :].astype(F32)
        for j in range(3 * nh):
            cs = slice(j * LANE, (j + 1) * LANE)
            y = xbuf_ref[off0 + lo:off0 + lo + rp, cs] * cw_ref[0:1, cs]
            for kk in range(1, CONV_W):
                y = y + xbuf_ref[off0 + kk + lo:off0 + kk + lo + rp, cs] * cw_ref[kk:kk + 1, cs]
            y = _silu(y)
            if j < nh:
                q_s[j, lo:lo + rp, :] = (y * lax.rsqrt(jnp.sum(y * y, axis=-1, keepdims=True) + RMS_EPS)
                                         * (GDN_DK ** -0.5))
            elif j < 2 * nh:
                k_s[j - nh, lo:lo + rp, :] = y * lax.rsqrt(jnp.sum(y * y, axis=-1, keepdims=True) + RMS_EPS)
            else:
                v_s[j - 2 * nh, lo:lo + rp, :] = y
        sc = s_ref[lo:lo + rp, :]
        beta_all = _sigmoid_t(sc)
        g_all = -jnp.exp(par_ref[0:1, :]) * _softplus(sc + par_ref[1:2, :])
        cum_all = jnp.concatenate([_mm_01(tri, g_all[c * CHUNK:(c + 1) * CHUNK, :]) for c in range(npar)], axis=0)
        cum_t = jnp.transpose(cum_all)
        for c in range(npar):
            cumrow_s[t * npar + c] = cum_t[0:SUBLANE, c * CHUNK:(c + 1) * CHUNK]
        for h in range(nh):
            cumb_s[h, lo:lo + rp, :] = jnp.broadcast_to(cum_all[:, nh + h:nh + h + 1], (rp, LANE))
            betab_s[h, lo:lo + rp, :] = jnp.broadcast_to(beta_all[:, h:h + 1], (rp, LANE))

    def trip(cp, _slot):
        chains = [(cp * npar + cc, h) for cc in range(npar) for h in range(nh)]
        r0s = [c * CHUNK for c, _ in chains]
        qc = [q_s[h, r0:r0 + CHUNK, :] for (_, h), r0 in zip(chains, r0s)]
        kc = [k_s[h, r0:r0 + CHUNK, :] for (_, h), r0 in zip(chains, r0s)]
        vc = [v_s[h, r0:r0 + CHUNK, :] for (_, h), r0 in zip(chains, r0s)]
        cumc = [cumb_s[h, r0:r0 + CHUNK, :] for (_, h), r0 in zip(chains, r0s)]
        bc = [betab_s[h, r0:r0 + CHUNK, :] for (_, h), r0 in zip(chains, r0s)]
        n = len(chains)
        kk = [_mm_nt(kc[i], kc[i]) for i in range(n)]
        qk = [_mm_nt(qc[i], kc[i]) for i in range(n)]
        decay = []
        for i, (c, h) in enumerate(chains):
            diff = cumc[i][:, 0:CHUNK] - cumrow_s[c][nh + h:nh + h + 1, :]
            decay.append(jnp.where(incl, jnp.exp(jnp.where(incl, diff, 0.0)), 0.0))
        a = [jnp.where(strict, bc[i][:, 0:CHUNK] * kk[i] * decay[i], 0.0) for i in range(n)]
        x = [eye - a[i] for i in range(n)]
        p = [_mm(a[i], a[i]) for i in range(n)]
        for it in range(5):
            x = [x[i] + _mm(x[i], p[i]) for i in range(n)]
            if it < 4:
                p = [_mm(p[i], p[i]) for i in range(n)]
        ecum = [jnp.exp(cumc[i]) for i in range(n)]
        sol = [_mm(x[i], jnp.concatenate([vc[i] * bc[i], kc[i] * (bc[i] * ecum[i])], axis=1)) for i in range(n)]
        attn = [qk[i] * decay[i] for i in range(n)]
        cum_last = [cumc[i][CHUNK - 1:CHUNK, :] for i in range(n)]
        k_state = [kc[i] * jnp.exp(cum_last[i] - cumc[i]) for i in range(n)]
        wqi = [jnp.concatenate([sol[i][:, GDN_DV:GDN_DV + GDN_DK], qc[i] * ecum[i]], axis=0) for i in range(n)]
        for cc in range(npar):
            idx = [cc * nh + h for h in range(nh)]
            s = [state_ref[h] for h in range(nh)]
            wq = [_mm(wqi[i], s[h]) for h, i in enumerate(idx)]
            v_new = [sol[i][:, 0:GDN_DV] - wq[h][0:CHUNK, :] for h, i in enumerate(idx)]
            av = [_mm(attn[i], v_new[h]) for h, i in enumerate(idx)]
            upd = [_mm_tn(k_state[i], v_new[h]) for h, i in enumerate(idx)]
            for h, i in enumerate(idx):
                state_ref[h] = s[h] * jnp.exp(cum_last[i]) + upd[h]
                o_s[r0s[i]:r0s[i] + CHUNK, h * LANE:(h + 1) * LANE] = wq[h][CHUNK:2 * CHUNK, :] + av[h]

    def epilogue(t):
        lo = t * rp
        for h in range(nh):
            hs = slice(h * LANE, (h + 1) * LANE)
            o = o_s[lo:lo + rp, hs]
            oh = o * lax.rsqrt(jnp.mean(o * o, axis=-1, keepdims=True) + RMS_EPS) * nw_ref[...]
            o_ref[lo:lo + rp, hs] = (oh * _silu(gate_ref[lo:lo + rp, hs].astype(F32))).astype(o_ref.dtype)

    _schedule([(t,) for t in range(nc // npar)], prologue, trip, epilogue)
    tail_ref[...] = xbuf_ref[tb:tb + SUBLANE, :]


def _gdn_call(pb, pf, conv_w, par, norm_w, layer, batch, seq):
    tb = min(TB_MIX, seq)
    nb = seq // tb
    nc = tb // CHUNK
    t = batch * seq
    wq = 2 * GDN_QK + GDN_V
    kern = functools.partial(_gdn_kernel, tb=tb)
    return pl.pallas_call(
        kern,
        grid=(batch, nb),
        in_specs=[pl.BlockSpec((tb, wq), lambda b, i: (b * nb + i, A_Q // wq)),
                  pl.BlockSpec((tb, LANE), lambda b, i: (b * nb + i, F_AS // LANE)),
                  pl.BlockSpec((tb, GDN_V), lambda b, i: (b * nb + i, A_G // GDN_V)),
                  pl.BlockSpec((None, CONV_W, wq), lambda b, i: (layer, 0, 0)),
                  pl.BlockSpec((None, SUBLANE, LANE), lambda b, i: (layer, 0, 0)),
                  pl.BlockSpec((None, 1, GDN_DV), lambda b, i: (layer, 0, 0))],
        out_specs=pl.BlockSpec((tb, GDN_V), lambda b, i: (b * nb + i, 0)),
        out_shape=jax.ShapeDtypeStruct((t, GDN_V), BF16),
        scratch_shapes=[pltpu.VMEM((GDN_HEADS, GDN_DK, GDN_DV), F32),
                        pltpu.VMEM((SUBLANE, wq), F32),
                        pltpu.VMEM((tb + SUBLANE, wq), F32),
                        pltpu.VMEM((GDN_HEADS, tb, LANE), F32),
                        pltpu.VMEM((GDN_HEADS, tb, LANE), F32),
                        pltpu.VMEM((GDN_HEADS, tb, LANE), F32),
                        pltpu.VMEM((GDN_HEADS, tb, LANE), F32),
                        pltpu.VMEM((GDN_HEADS, tb, LANE), F32),
                        pltpu.VMEM((nc, SUBLANE, CHUNK), F32),
                        pltpu.VMEM((tb, GDN_V), F32)],
        compiler_params=_cparams(("parallel", "arbitrary")),
        name="gdn",
    )(pb, pf, pb, conv_w, par, norm_w)


def _dd_make_trip(q_s, k_s, v_s, la_s, o_s, state_ref, c8_s, p_s, g_heads):
    nblk = CHUNK // SUB
    dkh = LANE // g_heads
    dvp = g_heads * LANE
    npar = DD_NPAR
    row = lax.broadcasted_iota(jnp.int32, (CHUNK, CHUNK), 0)
    col = lax.broadcasted_iota(jnp.int32, (CHUNK, CHUNK), 1)
    level_masks = []
    for sh in (5, 4, 3):
        same2b = jnp.right_shift(row, sh + 1) == jnp.right_shift(col, sh + 1)
        upper = (jnp.right_shift(row, sh) & 1) == 1
        lower = (jnp.right_shift(col, sh) & 1) == 0
        level_masks.append(jnp.where(same2b, jnp.where(upper, jnp.where(lower, 1.0, 0.0), 0.0), 0.0))
    tri8 = jnp.where(jnp.right_shift(row, 3) == jnp.right_shift(col, 3),
                     jnp.where(col <= row, 1.0, 0.0), 0.0).astype(BF16)
    diag_mask = tri8.astype(F32)
    lane128 = lax.broadcasted_iota(jnp.int32, (CHUNK, LANE), 1)
    head_masks = [jnp.where((lane128 >= g * dkh) & (lane128 < (g + 1) * dkh), 1.0, 0.0)
                  for g in range(g_heads)]
    sub = lax.broadcasted_iota(jnp.int32, (SUB, LANE), 0)
    dk_sh = dkh.bit_length() - 1
    lane_sh = LANE.bit_length() - 1
    orow = lax.broadcasted_iota(jnp.int32, (LANE, dvp), 0)
    ocol = lax.broadcasted_iota(jnp.int32, (LANE, dvp), 1)
    ones_bd = jnp.where(jnp.right_shift(orow, dk_sh) == jnp.right_shift(ocol, lane_sh), 1.0, 0.0).astype(BF16)
    srow = lax.broadcasted_iota(jnp.int32, (dvp, LANE), 0)
    scol = lax.broadcasted_iota(jnp.int32, (dvp, LANE), 1)
    state_mask = jnp.where(jnp.right_shift(srow, lane_sh) == jnp.right_shift(scol, dk_sh), 1.0, 0.0)

    def block_sums(u, r0):
        c8 = [c8_s[u, r0 + b * SUB:r0 + (b + 1) * SUB, :] for b in range(nblk)]
        t8 = [c8_s[u, r0 + (b + 1) * SUB - 1:r0 + (b + 1) * SUB, :] for b in range(nblk)]
        t16 = [t8[2 * b] + t8[2 * b + 1] for b in range(nblk // 2)]
        t32 = [t16[2 * b] + t16[2 * b + 1] for b in range(nblk // 4)]
        t64 = t32[0] + t32[1]
        c16 = [c8[b] + t8[b - 1] if b % 2 else c8[b] for b in range(nblk)]
        c32 = [c16[b] + t16[b // 2 - 1] if (b // 2) % 2 else c16[b] for b in range(nblk)]
        c64 = [c32[b] + t32[0] if b >= nblk // 2 else c32[b] for b in range(nblk)]
        pre = {8: c8, 16: c16, 32: c32, 64: c64}
        suf = {8: [t8[b] - c8[b] for b in range(nblk)],
               16: [t16[b // 2] - c16[b] for b in range(nblk)],
               32: [t32[b // 4] - c32[b] for b in range(nblk)],
               64: [t64 - c64[b] for b in range(nblk)]}
        return pre, suf

    def cat(pieces):
        return jnp.concatenate(pieces, axis=0)

    def trip(u, t, slot):
        rng = range(npar)
        r0s = [(t * npar + i) * CHUNK for i in rng]
        qc = [q_s[u, r0:r0 + CHUNK, :] for r0 in r0s]
        kc = [k_s[u, r0:r0 + CHUNK, :] for r0 in r0s]
        vc = [v_s[u, r0:r0 + CHUNK, :] for r0 in r0s]
        c8_all = _mm_01(tri8, jnp.concatenate([la_s[u, r0:r0 + CHUNK, :] for r0 in r0s], axis=1))
        for i in rng:
            c8_s[u, r0s[i]:r0s[i] + CHUNK, :] = c8_all[:, i * LANE:(i + 1) * LANE]
        sums = [block_sums(u, r0s[i]) for i in rng]
        attn = [[None] * g_heads for _ in rng]
        for li, b in enumerate((32, 16, 8)):
            qs = [qc[i] * jnp.exp(cat(sums[i][0][b])) for i in rng]
            ks = [kc[i] * jnp.exp(cat(sums[i][1][b])) for i in rng]
            for g in range(g_heads):
                for i in rng:
                    qg = qs[i] * head_masks[g] if g_heads > 1 else qs[i]
                    term = _mm_nt(qg, ks[i]) * level_masks[li]
                    attn[i][g] = term if li == 0 else attn[i][g] + term
        for i in rng:
            dg = diag_terms(qc[i], kc[i], cat(sums[i][0][8]))
            for g in range(g_heads):
                attn[i][g] = attn[i][g] + dg[g]
        o = [apply(attn[i], vc[i]) for i in rng]
        q_inter = [qc[i] * jnp.exp(cat(sums[i][0][64])) for i in rng]
        k_state = [kc[i] * jnp.exp(cat(sums[i][1][64])) for i in rng]
        upd = [_mm_tn(vc[i], k_state[i]) for i in rng]
        st = state_ref[u]
        for i in rng:
            o_s[u, r0s[i]:r0s[i] + CHUNK, :] = o[i] + _mm_nt(q_inter[i], st)
            decay_last = jnp.exp(sums[i][0][64][nblk - 1][SUB - 1:SUB, :])
            st = st * decay_last + (upd[i] * state_mask if g_heads > 1 else upd[i])
        state_ref[u] = st

    def apply(att, v):
        if g_heads > 1:
            return jnp.concatenate([_mm(att[g], v[:, g * LANE:(g + 1) * LANE]) for g in range(g_heads)], axis=1)
        return _mm(att[0], v)

    def diag_terms(q, k, c8):
        qd = q * jnp.exp(c8)
        kd = k * jnp.exp(jnp.minimum(-c8, DD_SAFE_EXP))
        return [_mm_nt(qd * head_masks[g] if g_heads > 1 else qd, kd) * diag_mask for g in range(g_heads)]

    def fix_unit(u, nchunk):
        def body(c, carry):
            r0 = pl.multiple_of(c * CHUNK, CHUNK)
            rows = pl.ds(r0, CHUNK)
            fast = apply(diag_terms(q_s[u, rows, :], k_s[u, rows, :], c8_s[u, rows, :]), v_s[u, rows, :])
            for r in range(nblk):
                qr = q_s[u, pl.ds(r0 + r * SUB, SUB), :]
                cr = c8_s[u, pl.ds(r0 + r * SUB, SUB), :]
                for jj in range(SUB):
                    krow = k_s[u, pl.ds(r0 + r * SUB + jj, 1), :]
                    crow = c8_s[u, pl.ds(r0 + r * SUB + jj, 1), :]
                    pr = qr * krow * jnp.exp(jnp.where(sub >= jj, cr - crow, -jnp.inf))
                    p_s[(r * SUB + jj) * SUB:(r * SUB + jj + 1) * SUB, :] = pr
            rs = jnp.dot(p_s[...].astype(BF16), ones_bd, preferred_element_type=F32)
            od = []
            for r in range(nblk):
                acc = None
                for jj in range(SUB):
                    term = (rs[(r * SUB + jj) * SUB:(r * SUB + jj + 1) * SUB, :]
                            * v_s[u, pl.ds(r0 + r * SUB + jj, 1), :])
                    acc = term if acc is None else acc + term
                od.append(acc)
            o_s[u, rows, :] = o_s[u, rows, :] + (cat(od) - fast)
            return carry

        lax.fori_loop(0, nchunk, body, 0)

    return trip, fix_unit


def _dd_fix_if_unsafe(c8_s, fix_unit, nu, tb, units, epilogue):
    worst = None
    for u in range(nu):
        tot = -c8_s[u, pl.ds(SUB - 1, tb // SUB, stride=SUB), :]
        worst = tot if worst is None else jnp.maximum(worst, tot)

    @pl.when(jnp.max(worst) > DD_SAFE_EXP)
    def _():
        for u in range(nu):
            fix_unit(u, tb // CHUNK)
        for u, t in units:
            epilogue(u, t)


def _dd_scratch(nu, tb, dvp):
    return [pltpu.VMEM((nu, dvp, LANE), F32),
            pltpu.VMEM((nu, tb, LANE), F32), pltpu.VMEM((nu, tb, LANE), F32),
            pltpu.VMEM((nu, tb, dvp), F32),
            pltpu.VMEM((nu, tb, LANE), F32),
            pltpu.VMEM((nu, tb, dvp), F32),
            pltpu.VMEM((nu, tb, LANE), F32),
            pltpu.VMEM((CHUNK * SUB, LANE), F32)]


def _gla_kernel(q_ref, k_ref, v_ref, lr_ref, gate_ref, w2_ref, b2_ref, nw_ref,
                o_ref, state_ref, q_s, k_s, v_s, la_s, o_s, c8_s, p_s, *, tb):
    @pl.when(pl.program_id(1) == 0)
    def _():
        state_ref[...] = jnp.zeros_like(state_ref)

    npair = GLA_HEADS // 2
    rp = DD_NPAR * CHUNK
    trip, fix_unit = _dd_make_trip(q_s, k_s, v_s, la_s, o_s, state_ref, c8_s, p_s, 2)

    def prologue(u, t):
        lo = t * rp
        ls = slice(u * LANE, (u + 1) * LANE)
        vs = slice(u * 2 * LANE, (u + 1) * 2 * LANE)
        q_s[u, lo:lo + rp, :] = q_ref[lo:lo + rp, ls].astype(F32) * (GLA_DK ** -0.5)
        k_s[u, lo:lo + rp, :] = k_ref[lo:lo + rp, ls].astype(F32)
        v_s[u, lo:lo + rp, :] = v_ref[lo:lo + rp, vs].astype(F32)
        z = _mm(lr_ref[lo:lo + rp, :], w2_ref[:, ls]) + b2_ref[:, ls]
        la_s[u, lo:lo + rp, :] = _log_sigmoid(z) * (1.0 / GLA_NORMALIZER)

    def epilogue(u, t):
        lo = t * rp
        for g in range(2):
            hs = slice((2 * u + g) * LANE, (2 * u + g + 1) * LANE)
            o = o_s[u, lo:lo + rp, g * LANE:(g + 1) * LANE]
            oh = o * lax.rsqrt(jnp.mean(o * o, axis=-1, keepdims=True) + RMS_EPS) * nw_ref[...]
            o_ref[lo:lo + rp, hs] = (oh * _silu(gate_ref[lo:lo + rp, hs].astype(F32))).astype(o_ref.dtype)

    units = [(u, t) for u in range(npair) for t in range(tb // rp)]
    _schedule(units, prologue, trip, epilogue)
    _dd_fix_if_unsafe(c8_s, fix_unit, npair, tb, units, epilogue)


def _gla_call(pb, pf, w2p, b2, norm_w, layer, batch, seq):
    tb = min(TB_MIX, seq)
    nb = seq // tb
    t = batch * seq
    kern = functools.partial(_gla_kernel, tb=tb)
    return pl.pallas_call(
        kern,
        grid=(batch, nb),
        in_specs=[pl.BlockSpec((tb, GLA_QK), lambda b, i: (b * nb + i, B_Q // GLA_QK)),
                  pl.BlockSpec((tb, GLA_QK), lambda b, i: (b * nb + i, B_K // GLA_QK)),
                  pl.BlockSpec((tb, GLA_V), lambda b, i: (b * nb + i, B_V // GLA_V)),
                  pl.BlockSpec((tb, LANE), lambda b, i: (b * nb + i, F_LR // LANE)),
                  pl.BlockSpec((tb, GLA_V), lambda b, i: (b * nb + i, B_G // GLA_V)),
                  pl.BlockSpec((None, LANE, GLA_QK), lambda b, i: (layer, 0, 0)),
                  pl.BlockSpec((None, 1, GLA_QK), lambda b, i: (layer, 0, 0)),
                  pl.BlockSpec((None, 1, GLA_DV), lambda b, i: (layer, 0, 0))],
        out_specs=pl.BlockSpec((tb, GLA_V), lambda b, i: (b * nb + i, 0)),
        out_shape=jax.ShapeDtypeStruct((t, GLA_V), BF16),
        scratch_shapes=_dd_scratch(GLA_HEADS // 2, tb, 2 * LANE),
        compiler_params=_cparams(("parallel", "arbitrary")),
        name="gla",
    )(pb, pb, pb, pf, pb, w2p, b2, norm_w)


def _hgrn_kernel(q_ref, f_ref, v_ref, gate_ref, lbl_ref, nw_ref,
                 o_ref, state_ref, q_s, k_s, v_s, la_s, o_s, c8_s, p_s, *, tb, layer):
    @pl.when(pl.program_id(1) == 0)
    def _():
        state_ref[...] = jnp.zeros_like(state_ref)

    logits = lbl_ref[...]
    mx = jnp.max(logits, axis=0, keepdims=True)
    ex = jnp.exp(logits - mx)
    p = ex / jnp.sum(ex, axis=0, keepdims=True)
    acc = p[0:1, :]
    for r in range(1, layer + 1):
        acc = acc + p[r:r + 1, :]
    lb = jnp.clip(acc - p[0:1, :], 0.0, 1.0)
    log_lb = jnp.log(jnp.maximum(lb, LB_FLOOR))
    log_1m = jnp.log(1.0 - lb)

    rp = DD_NPAR * CHUNK
    trip, fix_unit = _dd_make_trip(q_s, k_s, v_s, la_s, o_s, state_ref, c8_s, p_s, 1)

    def prologue(u, t):
        lo = t * rp
        hs = slice(u * LANE, (u + 1) * LANE)
        cf = f_ref[lo:lo + rp, hs]
        second = log_1m[:, hs] + _log_sigmoid(cf)
        llb = log_lb[:, hs]
        la_s[u, lo:lo + rp, :] = jnp.maximum(llb, second) + jnp.log(1.0 + jnp.exp(-jnp.abs(llb - second)))
        k_s[u, lo:lo + rp, :] = (1.0 - lb[:, hs]) * _sigmoid_t(-cf)
        q_s[u, lo:lo + rp, :] = _silu(q_ref[lo:lo + rp, hs].astype(F32)) * (HGRN_EXPAND ** -0.5)
        v_s[u, lo:lo + rp, :] = v_ref[lo:lo + rp, hs].astype(F32)

    def epilogue(u, t):
        lo = t * rp
        hs = slice(u * LANE, (u + 1) * LANE)
        o = o_s[u, lo:lo + rp, :]
        oh = o * lax.rsqrt(jnp.mean(o * o, axis=-1, keepdims=True) + RMS_EPS) * nw_ref[...]
        o_ref[lo:lo + rp, hs] = (oh * _silu(gate_ref[lo:lo + rp, hs].astype(F32))).astype(o_ref.dtype)

    units = [(u, t) for u in range(HGRN_HEADS) for t in range(tb // rp)]
    _schedule(units, prologue, trip, epilogue)
    _dd_fix_if_unsafe(c8_s, fix_unit, HGRN_HEADS, tb, units, epilogue)


def _hgrn_call(pb, pf, lb_logits, norm_w, layer, batch, seq):
    tb = min(TB_MIX, seq)
    nb = seq // tb
    t = batch * seq
    kern = functools.partial(_hgrn_kernel, tb=tb, layer=layer)
    return pl.pallas_call(
        kern,
        grid=(batch, nb),
        in_specs=[pl.BlockSpec((tb, HGRN_QK), lambda b, i: (b * nb + i, C_Q // HGRN_QK)),
                  pl.BlockSpec((tb, HGRN_QK), lambda b, i: (b * nb + i, F_CF // HGRN_QK)),
                  pl.BlockSpec((tb, HGRN_V), lambda b, i: (b * nb + i, C_I // HGRN_V)),
                  pl.BlockSpec((tb, HGRN_V), lambda b, i: (b * nb + i, C_G // HGRN_V)),
                  pl.BlockSpec((DEPTH, HGRN_QK), lambda b, i: (0, 0)),
                  pl.BlockSpec((None, 1, HGRN_DV), lambda b, i: (layer, 0, 0))],
        out_specs=pl.BlockSpec((tb, HGRN_V), lambda b, i: (b * nb + i, 0)),
        out_shape=jax.ShapeDtypeStruct((t, HGRN_V), BF16),
        scratch_shapes=_dd_scratch(HGRN_HEADS, tb, LANE),
        compiler_params=_cparams(("parallel", "arbitrary")),
        name="hgrn",
    )(pb, pf, pb, pb, lb_logits, norm_w)


def _merge_kernel(oa_ref, ob_ref, oc_ref, ma_ref, mb_ref, mc_ref, h_ref,
                  wa_ref, wb_ref, wc_ref, wo_ref, g_ref, b_ref, ox_ref, *, tm):
    y = (_sigmoid_t(ma_ref[...].astype(F32)) * jnp.dot(oa_ref[...], wa_ref[...], preferred_element_type=F32)
         + _sigmoid_t(mb_ref[...].astype(F32)) * jnp.dot(ob_ref[...], wb_ref[...], preferred_element_type=F32)
         + _sigmoid_t(mc_ref[...].astype(F32)) * jnp.dot(oc_ref[...], wc_ref[...], preferred_element_type=F32))
    mix = _mm(y, wo_ref[...])
    _to_token_major(ox_ref, _layer_norm(ALPHA * h_ref[...] + mix, g_ref[...], b_ref[...]), tm)


def _merge_call(o_a, o_b, o_c, pb, h, wa, wb, wc, wo, g, b, layer):
    t, d = h.shape
    tm = min(TM_ROWS, t)

    def row(width):
        return pl.BlockSpec((tm, width), lambda i: (i, 0))

    def wspec(kdim):
        return pl.BlockSpec((None, kdim, d), lambda i: (layer, 0, 0))

    vec = pl.BlockSpec((None, 1, d), lambda i: (layer, 0, 0))
    return pl.pallas_call(
        functools.partial(_merge_kernel, tm=tm),
        grid=(t // tm,),
        in_specs=[row(GDN_V), row(GLA_V), row(HGRN_V),
                  pl.BlockSpec((tm, d), lambda i: (i, M_A // d)),
                  pl.BlockSpec((tm, d), lambda i: (i, M_B // d)),
                  pl.BlockSpec((tm, d), lambda i: (i, M_C // d)),
                  row(d), wspec(GDN_V), wspec(GLA_V), wspec(HGRN_V), wspec(d), vec, vec],
        out_specs=pl.BlockSpec((tm * NPIECE, LANE), lambda i: (i, 0)),
        out_shape=jax.ShapeDtypeStruct((t * NPIECE, LANE), F32),
        compiler_params=_cparams(("parallel",)),
        name="merge",
    )(o_a, o_b, o_c, pb, pb, pb, h, wa, wb, wc, wo, g, b)


def _route_pairs(scores_t, bias_ref):
    s = [scores_t[e:e + 1, :] for e in range(N_EXPERTS)]
    sel = [s[e] + bias_ref[e:e + 1, 0:1] for e in range(N_EXPERTS)]
    gscore = []
    for g in range(N_GROUPS):
        a, b, c, d = sel[4 * g:4 * g + 4]
        hi1, lo1 = jnp.maximum(a, b), jnp.minimum(a, b)
        hi2, lo2 = jnp.maximum(c, d), jnp.minimum(c, d)
        top1 = jnp.maximum(hi1, hi2)
        top2 = jnp.maximum(jnp.minimum(hi1, hi2), jnp.maximum(lo1, lo2))
        gscore.append(top1 + top2)
    best = gscore[0]
    gidx = jnp.zeros_like(best, dtype=jnp.int32)
    for g in range(1, N_GROUPS):
        take = gscore[g] > best
        best = jnp.where(take, gscore[g], best)
        gidx = jnp.where(take, g, gidx)
    ing, raw = [], []
    for kk in range(EXPERTS_PER_GROUP):
        vs, vr = sel[kk], s[kk]
        for g in range(1, N_GROUPS):
            pick = gidx == g
            vs = jnp.where(pick, sel[4 * g + kk], vs)
            vr = jnp.where(pick, s[4 * g + kk], vr)
        ing.append(vs)
        raw.append(vr)
    b1 = ing[0]
    i1 = jnp.zeros_like(gidx)
    for kk in range(1, EXPERTS_PER_GROUP):
        take = ing[kk] > b1
        b1 = jnp.where(take, ing[kk], b1)
        i1 = jnp.where(take, kk, i1)
    neg = jnp.full_like(b1, -jnp.inf)
    b2 = neg
    i2 = jnp.zeros_like(gidx)
    for kk in range(EXPERTS_PER_GROUP):
        cand = jnp.where(i1 == kk, neg, ing[kk])
        take = cand > b2
        b2 = jnp.where(take, cand, b2)
        i2 = jnp.where(take, kk, i2)
    w1 = raw[0]
    w2 = raw[0]
    for kk in range(1, EXPERTS_PER_GROUP):
        w1 = jnp.where(i1 == kk, raw[kk], w1)
        w2 = jnp.where(i2 == kk, raw[kk], w2)
    tot = w1 + w2
    w1 = w1 / tot
    w2 = w2 / tot
    first_lower = i1 < i2
    lo = jnp.where(first_lower, i1, i2)
    hi = jnp.where(first_lower, i2, i1)
    pidx = jnp.where(lo == 0, hi - 1, jnp.where(lo == 1, hi + 1, 5))
    bkt = gidx * 6 + pidx
    return bkt, jnp.where(first_lower, w1, w2), jnp.where(first_lower, w2, w1)


def _moe_route_kernel(hx_ref, wr_ref, rb_ref, bkt_ref, rank_ref, wab_ref, cnt_ref, carry_ref, *, tm):
    @pl.when(pl.program_id(0) == 0)
    def _():
        carry_ref[...] = jnp.zeros_like(carry_ref)

    logits_t = _mm_nt_f32(wr_ref[...], _from_token_major(hx_ref, tm))
    bkt, wa, wb = _route_pairs(_sigmoid(logits_t), rb_ref)
    sub = lax.broadcasted_iota(jnp.int32, (32, tm), 0)
    oh = jnp.where(sub == bkt, 1.0, 0.0)
    r = lax.broadcasted_iota(jnp.int32, (tm, tm), 0)
    c = lax.broadcasted_iota(jnp.int32, (tm, tm), 1)
    earlier = jnp.where(r < c, 1.0, 0.0).astype(BF16)
    before = jnp.dot(oh.astype(BF16), earlier, preferred_element_type=F32)
    carry = carry_ref[...]
    rank = jnp.sum(oh * (before + carry[:, 0:1]), axis=0, keepdims=True)
    carry = carry + jnp.sum(oh, axis=1, keepdims=True)
    carry_ref[...] = carry
    cnt_ref[...] = carry
    bkt_ref[...] = bkt
    rank_ref[...] = rank.astype(jnp.int32)
    pad = jnp.zeros((LANE - 2, tm), F32)
    wab_ref[...] = jnp.transpose(jnp.concatenate([wa, wb, pad], axis=0))


def _moe_tables_kernel(cnt_ref, bkt_ref, rank_ref, pos_ref, tabs_ref, *, t, tr):
    cnt = cnt_ref[...]
    sz = jnp.floor((cnt + (tr - 1)) * (1.0 / tr)) * tr
    r = lax.broadcasted_iota(jnp.int32, (32, 32), 0)
    c = lax.broadcasted_iota(jnp.int32, (32, 32), 1)
    start = jnp.dot(jnp.where(c < r, 1.0, 0.0), sz, preferred_element_type=F32, precision=HIGHEST)
    end = start + sz
    sub = lax.broadcasted_iota(jnp.int32, (32, t), 0)
    pos = jnp.sum(jnp.where(sub == bkt_ref[...], start[:, 0:1], 0.0), axis=0, keepdims=True)
    pos_ref[...] = pos.astype(jnp.int32) + rank_ref[...]
    brow = lax.broadcasted_iota(jnp.int32, (32, LANE), 0)
    tile0 = lax.broadcasted_iota(jnp.int32, (32, LANE), 1).astype(F32) * tr
    tbk = jnp.sum(jnp.where((brow < MOE_NBKT) & (end <= tile0), 1, 0), axis=0, keepdims=True)
    tbk = jnp.minimum(tbk, MOE_NBKT - 1)
    total = end[MOE_NBKT - 1:MOE_NBKT, :]
    valid = jnp.where(tile0[0:1, :] < total, 1, 0)
    g = jnp.where(tbk >= 6, 1, 0) + jnp.where(tbk >= 12, 1, 0) + jnp.where(tbk >= 18, 1, 0)
    p = tbk - 6 * g
    ge3 = jnp.where(p >= 3, 1, 0)
    ge5 = jnp.where(p >= 5, 1, 0)
    ea = 4 * g + ge3 + ge5
    eb = 4 * g + p + 1 - 2 * ge3 - ge5
    lane = lax.broadcasted_iota(jnp.int32, (1, LANE), 1)
    last_tile = (total * (1.0 / tr)).astype(jnp.int32) - 1
    src = jnp.where(valid > 0, lane, jnp.maximum(last_tile, 0))
    zero = jnp.zeros((SUBLANE - 3, LANE), jnp.int32)
    tabs_ref[...] = jnp.concatenate([ea, eb, src, zero], axis=0)


def _moe_group_kernel(ea_ref, eb_ref, vd_ref, xs_ref, ws_ref, wga_ref, wua_ref, wda_ref,
                      wgb_ref, wub_ref, wdb_ref, ys_ref, wg_s, wu_s, wd_s, *, tr):
    j = pl.program_id(0)
    prev = jnp.maximum(j - 1, 0)

    @pl.when((j == 0) | (ea_ref[j] != ea_ref[prev]))
    def _():
        wg_s[0] = wga_ref[...].astype(BF16)
        wu_s[0] = wua_ref[...].astype(BF16)
        wd_s[0] = wda_ref[...].astype(BF16)

    @pl.when((j == 0) | (eb_ref[j] != eb_ref[prev]))
    def _():
        wg_s[1] = wgb_ref[...].astype(BF16)
        wu_s[1] = wub_ref[...].astype(BF16)
        wd_s[1] = wdb_ref[...].astype(BF16)

    @pl.when(vd_ref[j] == j)
    def _():
        x = _from_token_major(xs_ref, tr).astype(BF16)
        w = ws_ref[...]

        def ffn(s, cw):
            hg = jnp.dot(x, wg_s[s], preferred_element_type=F32)
            hu = jnp.dot(x, wu_s[s], preferred_element_type=F32)
            hid = _silu(hg) * hu * cw
            return jnp.dot(hid.astype(BF16), wd_s[s], preferred_element_type=F32)

        y = ffn(0, w[:, 0:1]) + ffn(1, w[:, 1:2])
        _to_token_major(ys_ref, y, tr)


def _moe_ln_kernel(hx_ref, y_ref, g_ref, b_ref, o_ref, ob_ref, *, tm):
    out = _layer_norm(ALPHA * _from_token_major(hx_ref, tm) + _from_token_major(y_ref, tm),
                      g_ref[...], b_ref[...])
    o_ref[...] = out
    ob_ref[...] = out.astype(BF16)


def _sc_mesh_info():
    info = plsc.get_sparse_core_info()
    mesh = plsc.VectorSubcoreMesh(core_axis_name="c", subcore_axis_name="s")
    return mesh, info.num_cores, info.num_subcores


def _sc_scatter_rows(x3, w2, pos, n_rows):
    t = x3.shape[0]
    mesh, nc, ns = _sc_mesh_info()
    per_w = t // (nc * ns)

    @functools.partial(
        pl.kernel, mesh=mesh,
        out_type=[jax.ShapeDtypeStruct((n_rows,) + x3.shape[1:], x3.dtype),
                  jax.ShapeDtypeStruct((n_rows,) + w2.shape[1:], w2.dtype)],
        scratch_types=[pltpu.VMEM((SC_CH,), jnp.int32), pltpu.VMEM((SC_CH,) + x3.shape[1:], x3.dtype),
                       pltpu.VMEM((SC_CH,) + w2.shape[1:], w2.dtype),
                       pltpu.SemaphoreType.DMA, pltpu.SemaphoreType.DMA])
    def k(x_hbm, w_hbm, idx_hbm, ox_hbm, ow_hbm, idx_v, rows_v, wrows_v, sem_x, sem_w):
        base = (lax.axis_index("s") * nc + lax.axis_index("c")) * per_w

        @pl.loop(0, per_w // SC_CH)
        def _(j):
            off = base + j * SC_CH
            pltpu.sync_copy(idx_hbm.at[pl.ds(off, SC_CH)], idx_v)
            pltpu.sync_copy(x_hbm.at[pl.ds(off, SC_CH)], rows_v)
            pltpu.sync_copy(w_hbm.at[pl.ds(off, SC_CH)], wrows_v)
            cx = pltpu.async_copy(rows_v, ox_hbm.at[idx_v], sem_x)
            cw = pltpu.async_copy(wrows_v, ow_hbm.at[idx_v], sem_w)
            cx.wait()
            cw.wait()

    return k(x3, w2, pos)


def _sc_gather_rows(y3, pos):
    t = pos.shape[0]
    mesh, nc, ns = _sc_mesh_info()
    per_w = t // (nc * ns)
    ch = SC_CH // 2

    @functools.partial(
        pl.kernel, mesh=mesh,
        out_type=jax.ShapeDtypeStruct((t,) + y3.shape[1:], y3.dtype),
        scratch_types=[pltpu.VMEM((ch,), jnp.int32), pltpu.VMEM((ch,), jnp.int32),
                       pltpu.VMEM((ch,) + y3.shape[1:], y3.dtype), pltpu.VMEM((ch,) + y3.shape[1:], y3.dtype)]
        + [pltpu.SemaphoreType.DMA] * 4)
    def k(y_hbm, idx_hbm, o_hbm, idx_a, idx_b, rows_a, rows_b, sg_a, sg_b, sw_a, sw_b):
        base = (lax.axis_index("s") * nc + lax.axis_index("c")) * per_w

        @pl.loop(0, per_w // (2 * ch))
        def _(j):
            off_a = base + j * 2 * ch
            off_b = off_a + ch
            pltpu.sync_copy(idx_hbm.at[pl.ds(off_a, ch)], idx_a)
            ga = pltpu.async_copy(y_hbm.at[idx_a], rows_a, sg_a)
            pltpu.sync_copy(idx_hbm.at[pl.ds(off_b, ch)], idx_b)
            gb = pltpu.async_copy(y_hbm.at[idx_b], rows_b, sg_b)
            ga.wait()
            wa = pltpu.async_copy(rows_a, o_hbm.at[pl.ds(off_a, ch)], sw_a)
            gb.wait()
            wb = pltpu.async_copy(rows_b, o_hbm.at[pl.ds(off_b, ch)], sw_b)
            wa.wait()
            wb.wait()

    return k(y3, pos)


def _moe_sparse(hx, wr_t, rbias, wg, wu, wd, g, b, layer):
    t = hx.shape[0] // NPIECE
    d = D_MODEL
    tm = min(TM_MOE, t)
    tr = MOE_TR
    nt = t // tr + MOE_NBKT
    n_rows = nt * tr
    row1 = pl.BlockSpec((1, tm), lambda i: (0, i))
    rows_tm = pl.BlockSpec((tm * NPIECE, LANE), lambda i: (i, 0))
    bkt, rank, wab, cnt = pl.pallas_call(
        functools.partial(_moe_route_kernel, tm=tm),
        grid=(t // tm,),
        in_specs=[rows_tm,
                  pl.BlockSpec((N_EXPERTS, d), lambda i: (0, 0)),
                  pl.BlockSpec((N_EXPERTS, LANE), lambda i: (0, 0))],
        out_specs=[row1, row1, pl.BlockSpec((tm, LANE), lambda i: (i, 0)),
                   pl.BlockSpec((32, LANE), lambda i: (0, 0))],
        out_shape=[jax.ShapeDtypeStruct((1, t), jnp.int32), jax.ShapeDtypeStruct((1, t), jnp.int32),
                   jax.ShapeDtypeStruct((t, LANE), F32), jax.ShapeDtypeStruct((32, LANE), F32)],
        scratch_shapes=[pltpu.VMEM((32, LANE), F32)],
        compiler_params=_cparams(("arbitrary",)),
        name="moe_route",
    )(hx, wr_t, rbias)
    pos, tabs = pl.pallas_call(
        functools.partial(_moe_tables_kernel, t=t, tr=tr),
        out_shape=[jax.ShapeDtypeStruct((1, t), jnp.int32), jax.ShapeDtypeStruct((SUBLANE, LANE), jnp.int32)],
        compiler_params=pltpu.CompilerParams(vmem_limit_bytes=VMEM_LIMIT),
        name="moe_tables",
    )(cnt, bkt, rank)
    pos = pos.reshape(t)
    xs3, ws = _sc_scatter_rows(hx.reshape(t, NPIECE, LANE), wab, pos, n_rows)

    def wspec(which, shape):
        if which == 0:
            return pl.BlockSpec((None, None) + shape, lambda j, ea, eb, vd: (layer, ea[j], 0, 0))
        return pl.BlockSpec((None, None) + shape, lambda j, ea, eb, vd: (layer, eb[j], 0, 0))

    ys = pl.pallas_call(
        functools.partial(_moe_group_kernel, tr=tr),
        grid_spec=pltpu.PrefetchScalarGridSpec(
            num_scalar_prefetch=3,
            grid=(nt,),
            in_specs=[pl.BlockSpec((tr * NPIECE, LANE), lambda j, ea, eb, vd: (vd[j], 0)),
                      pl.BlockSpec((tr, LANE), lambda j, ea, eb, vd: (vd[j], 0)),
                      wspec(0, (d, D_FF)), wspec(0, (d, D_FF)), wspec(0, (D_FF, d)),
                      wspec(1, (d, D_FF)), wspec(1, (d, D_FF)), wspec(1, (D_FF, d))],
            out_specs=pl.BlockSpec((tr * NPIECE, LANE), lambda j, ea, eb, vd: (vd[j], 0)),
            scratch_shapes=[pltpu.VMEM((2, d, D_FF), BF16), pltpu.VMEM((2, d, D_FF), BF16),
                            pltpu.VMEM((2, D_FF, d), BF16)]),
        out_shape=jax.ShapeDtypeStruct((n_rows * NPIECE, LANE), F32),
        compiler_params=_cparams(("arbitrary",)),
        name="moe_experts",
    )(tabs[0, :nt], tabs[1, :nt], tabs[2, :nt], xs3.reshape(n_rows * NPIECE, LANE), ws,
      wg, wu, wd, wg, wu, wd)
    y3 = _sc_gather_rows(ys.reshape(n_rows, NPIECE, LANE), pos)
    vec = pl.BlockSpec((None, 1, d), lambda i: (layer, 0, 0))
    row = pl.BlockSpec((tm, d), lambda i: (i, 0))
    return pl.pallas_call(
        functools.partial(_moe_ln_kernel, tm=tm),
        grid=(t // tm,),
        in_specs=[rows_tm, rows_tm, vec, vec],
        out_specs=[row, row],
        out_shape=[jax.ShapeDtypeStruct((t, d), F32), jax.ShapeDtypeStruct((t, d), BF16)],
        compiler_params=_cparams(("parallel",)),
        name="moe_ln",
    )(hx, y3.reshape(t * NPIECE, LANE), g, b)


def _pack_w_in(w_in):
    wt = jnp.swapaxes(w_in, 1, 2)
    (a_q, a_k, a_v, a_beta, a_dt, a_g, b_q, b_k, b_v, b_lr, b_g,
     c_q, c_f, c_i, c_g, m_a, m_b, m_c) = jnp.split(wt, SPLIT_POINTS, axis=1)
    depth, _, d = wt.shape
    a_s = jnp.concatenate([a_beta, a_dt, jnp.zeros((depth, LANE - 2 * GDN_HEADS, d), wt.dtype)], 1)
    b_lrp = jnp.concatenate([b_lr, jnp.zeros((depth, LANE - GLA_RANK, d), wt.dtype)], 1)
    wb = jnp.concatenate([m_a, m_b, m_c, a_q, a_k, a_v, a_g, b_q, b_k, b_v, b_g, c_q, c_i, c_g], 1).astype(BF16)
    wf = jnp.concatenate([c_f, a_s, b_lrp], 1).astype(BF16)
    assert wb.shape[1] == NPB and wf.shape[1] == NPF
    return wb, wf


def _prepare(w_in, gdn_conv, gdn_a_log, gdn_dt_bias, gdn_norm, gla_w2, gla_b2, gla_norm, hgrn_lb_logits,
             hgrn_norm, w_br_a, w_br_b, w_br_c, w_out, ln1_g, ln1_b, w_router, router_bias, w_gate, w_up,
             w_down, ln2_g, ln2_b):
    depth = w_in.shape[0]
    d = w_out.shape[-1]
    w_pb, w_pf = _pack_w_in(w_in)
    return dict(
        w_pb=w_pb, w_pf=w_pf,
        gdn_conv=gdn_conv,
        gdn_par=jnp.pad(jnp.stack([gdn_a_log, gdn_dt_bias], axis=1),
                        ((0, 0), (0, SUBLANE - 2), (GDN_HEADS, LANE - 2 * GDN_HEADS))),
        gdn_norm=gdn_norm.reshape(depth, 1, GDN_DV),
        w2p=jnp.concatenate([gla_w2, jnp.zeros((depth, LANE - GLA_RANK, GLA_QK), gla_w2.dtype)], axis=1),
        gla_b2=gla_b2.reshape(depth, 1, GLA_QK),
        gla_norm=gla_norm.reshape(depth, 1, GLA_DV),
        lb_logits=hgrn_lb_logits,
        hgrn_norm=hgrn_norm.reshape(depth, 1, HGRN_DV),
        wa=w_br_a.astype(BF16), wb=w_br_b.astype(BF16), wc=w_br_c.astype(BF16), wo=w_out.astype(BF16),
        ln1_g=ln1_g.reshape(depth, 1, d), ln1_b=ln1_b.reshape(depth, 1, d),
        wr_t=jnp.transpose(w_router),
        rbias=jnp.broadcast_to(router_bias[:, None], (N_EXPERTS, LANE)),
        wg=w_gate, wu=w_up, wd=w_down,
        ln2_g=ln2_g.reshape(depth, 1, d), ln2_b=ln2_b.reshape(depth, 1, d),
    )


def _mixer_block(h, hb, p, layer, batch, seq):
    pb = _inproj_call(hb, p["w_pb"], layer, BF16, TM_PROJ, TN_PROJ, "inproj_b")
    pf = _inproj_call(hb, p["w_pf"], layer, F32, TM_PROJ // 2, NPF, "inproj_f")
    o_a = _gdn_call(pb, pf, p["gdn_conv"], p["gdn_par"], p["gdn_norm"], layer, batch, seq)
    o_b = _gla_call(pb, pf, p["w2p"], p["gla_b2"], p["gla_norm"], layer, batch, seq)
    o_c = _hgrn_call(pb, pf, p["lb_logits"], p["hgrn_norm"], layer, batch, seq)
    return _merge_call(o_a, o_b, o_c, pb, h, p["wa"], p["wb"], p["wc"], p["wo"], p["ln1_g"], p["ln1_b"], layer)


def _ffn_block(hx, p, layer):
    return _moe_sparse(hx, p["wr_t"], p["rbias"], p["wg"], p["wu"], p["wd"], p["ln2_g"], p["ln2_b"], layer)


def kernel(x, ln0_g, ln0_b, w_in, gdn_conv, gdn_a_log, gdn_dt_bias, gdn_norm, gla_w2, gla_b2, gla_norm,
           hgrn_lb_logits, hgrn_norm, w_br_a, w_br_b, w_br_c, w_out, ln1_g, ln1_b, w_router, router_bias,
           w_gate, w_up, w_down, ln2_g, ln2_b):
    batch, seq, d = x.shape
    p = _prepare(w_in, gdn_conv, gdn_a_log, gdn_dt_bias, gdn_norm, gla_w2, gla_b2, gla_norm, hgrn_lb_logits,
                 hgrn_norm, w_br_a, w_br_b, w_br_c, w_out, ln1_g, ln1_b, w_router, router_bias, w_gate, w_up,
                 w_down, ln2_g, ln2_b)
    h, hb = _ln_call(x.reshape(batch * seq, d), ln0_g, ln0_b)
    for layer in range(w_in.shape[0]):
        hx = _mixer_block(h, hb, p, layer, batch, seq)
        h, hb = _ffn_block(hx, p, layer)
    return h.reshape(batch, seq, d)
```

```python
import functools

import numpy as np
import jax
import jax.numpy as jnp
from jax import lax
from jax.experimental import pallas as pl
from jax.experimental.pallas import tpu as pltpu
from jax.experimental.pallas import tpu_sc as plsc

F32 = jnp.float32
BF16 = jnp.bfloat16
HIGHEST = lax.Precision.HIGHEST

D_MODEL = 1024
DEPTH = 4
CHUNK = 64
GDN_HEADS, GDN_DK, GDN_DV, CONV_W = 4, 128, 128, 4
GLA_HEADS, GLA_DK, GLA_DV, GLA_RANK, GLA_NORMALIZER = 4, 64, 128, 16, 16.0
HGRN_HEADS, HGRN_EXPAND, HGRN_DV = 4, 128, 128
LB_FLOOR = 1e-30
N_EXPERTS, N_GROUPS, TOP_K, D_FF = 16, 4, 2, 256
EXPERTS_PER_GROUP = N_EXPERTS // N_GROUPS
ALPHA = (2.0 * DEPTH) ** 0.25
LN_EPS = 1e-5
RMS_EPS = 1e-6

GDN_QK = GDN_HEADS * GDN_DK
GDN_V = GDN_HEADS * GDN_DV
GLA_QK = GLA_HEADS * GLA_DK
GLA_V = GLA_HEADS * GLA_DV
HGRN_QK = HGRN_HEADS * HGRN_EXPAND
HGRN_V = HGRN_HEADS * HGRN_DV
SPLIT_SIZES = (GDN_QK, GDN_QK, GDN_V, GDN_HEADS, GDN_HEADS, GDN_V,
               GLA_QK, GLA_QK, GLA_V, GLA_RANK, GLA_V,
               HGRN_QK, HGRN_QK, HGRN_V, HGRN_V,
               D_MODEL, D_MODEL, D_MODEL)
SPLIT_POINTS = tuple(int(v) for v in np.cumsum(SPLIT_SIZES)[:-1])

LANE = 128
SUBLANE = 8
VMEM_LIMIT = 48 * 1024 * 1024

M_A, M_B, M_C = 0, 1024, 2048
A_Q, A_K, A_V, A_G = 3072, 3584, 4096, 4608
B_Q, B_K, B_V, B_G = 5120, 5376, 5632, 6144
C_Q, C_I, C_G = 6656, 7168, 7680
NPB = 8192
F_CF, F_AS, F_LR = 0, 512, 640
NPF = 768

TB_MIX = 1024
TM_ROWS = 512
TM_MOE = 1024
TM_PROJ, TN_PROJ = 1024, 4096
SUB = 8
DD_NPAR = 16
DD_SAFE_EXP = 60.0
GDN_NPAR = 8
MOE_TR = 512
MOE_NBKT = N_GROUPS * 6
NPIECE = D_MODEL // LANE
SC_CH = 64


def _cparams(sem):
    return pltpu.CompilerParams(dimension_semantics=sem, vmem_limit_bytes=VMEM_LIMIT)


def _mm(a, b):
    return jnp.dot(a.astype(BF16), b.astype(BF16), preferred_element_type=F32)


def _mm_nt(a, b):
    return lax.dot_general(a.astype(BF16), b.astype(BF16), (((1,), (1,)), ((), ())),
                           preferred_element_type=F32)


def _mm_tn(a, b):
    return lax.dot_general(a.astype(BF16), b.astype(BF16), (((0,), (0,)), ((), ())),
                           preferred_element_type=F32)


def _mm_nt_f32(a, b):
    a_hi = a.astype(BF16)
    a_mid = (a - a_hi.astype(F32)).astype(BF16)
    b_hi = b.astype(BF16)
    b_mid = (b - b_hi.astype(F32)).astype(BF16)
    dn = (((1,), (1,)), ((), ()))
    return (lax.dot_general(a_hi, b_hi, dn, preferred_element_type=F32)
            + lax.dot_general(a_hi, b_mid, dn, preferred_element_type=F32)
            + lax.dot_general(a_mid, b_hi, dn, preferred_element_type=F32))


def _split3(x):
    hi = x.astype(BF16)
    r = x - hi.astype(F32)
    mid = r.astype(BF16)
    lo = (r - mid.astype(F32)).astype(BF16)
    return hi, mid, lo


def _mm_01(m01, x):
    hi, mid, lo = _split3(x)
    return (jnp.dot(m01, hi, preferred_element_type=F32) + jnp.dot(m01, mid, preferred_element_type=F32)
            + jnp.dot(m01, lo, preferred_element_type=F32))


def _sigmoid(x):
    return 1.0 / (1.0 + jnp.exp(-x))


def _sigmoid_t(x):
    return 0.5 * jnp.tanh(0.5 * x) + 0.5


def _silu(x):
    h = 0.5 * x
    return h + h * jnp.tanh(h)


def _softplus(x):
    return jnp.maximum(x, 0.0) + jnp.log(1.0 + jnp.exp(-jnp.abs(x)))


def _log_sigmoid(x):
    return -_softplus(-x)


def _layer_norm(x, g, b):
    mu = jnp.mean(x, axis=-1, keepdims=True)
    xc = x - mu
    var = jnp.mean(xc * xc, axis=-1, keepdims=True)
    return xc * lax.rsqrt(var + LN_EPS) * g + b


def _to_token_major(ref, x, rows):
    for j in range(NPIECE):
        ref[pl.ds(j, rows, stride=NPIECE), :] = x[:, j * LANE:(j + 1) * LANE]


def _from_token_major(ref, rows):
    return jnp.concatenate([ref[pl.ds(j, rows, stride=NPIECE), :] for j in range(NPIECE)], axis=1)


def _ln_kernel(x_ref, g_ref, b_ref, o_ref, ob_ref):
    y = _layer_norm(x_ref[...], g_ref[...], b_ref[...])
    o_ref[...] = y
    ob_ref[...] = y.astype(BF16)


def _ln_call(x, g, b):
    t, d = x.shape
    tm = min(TM_ROWS, t)
    return pl.pallas_call(
        _ln_kernel,
        grid=(t // tm,),
        in_specs=[pl.BlockSpec((tm, d), lambda i: (i, 0)),
                  pl.BlockSpec((1, d), lambda i: (0, 0)),
                  pl.BlockSpec((1, d), lambda i: (0, 0))],
        out_specs=[pl.BlockSpec((tm, d), lambda i: (i, 0)), pl.BlockSpec((tm, d), lambda i: (i, 0))],
        out_shape=[jax.ShapeDtypeStruct((t, d), F32), jax.ShapeDtypeStruct((t, d), BF16)],
        compiler_params=_cparams(("parallel",)),
        name="ln0",
    )(x, g.reshape(1, d), b.reshape(1, d))


def _inproj_kernel(x_ref, wt_ref, o_ref):
    o_ref[...] = lax.dot_general(x_ref[...], wt_ref[...], (((1,), (1,)), ((), ())),
                                 preferred_element_type=F32).astype(o_ref.dtype)


def _inproj_call(hb, w, layer, out_dtype, tm, tn, name):
    t, d = hb.shape
    tm = min(tm, t)
    n = w.shape[-2]
    return pl.pallas_call(
        _inproj_kernel,
        grid=(n // tn, t // tm),
        in_specs=[pl.BlockSpec((tm, d), lambda j, i: (i, 0)),
                  pl.BlockSpec((None, tn, d), lambda j, i: (layer, j, 0))],
        out_specs=pl.BlockSpec((tm, tn), lambda j, i: (i, j)),
        out_shape=jax.ShapeDtypeStruct((t, n), out_dtype),
        compiler_params=_cparams(("parallel", "parallel")),
        name=name,
    )(hb, w)


def _schedule(units, prologue, trip, epilogue):
    prologue(*units[0])
    for n, u in enumerate(units):
        if n + 1 < len(units):
            prologue(*units[n + 1])
        trip(*u, n % 2)
        epilogue(*u)


def _gdn_kernel(qkv_ref, s_ref, gate_ref, cw_ref, par_ref, nw_ref, o_ref,
                state_ref, tail_ref, xbuf_ref, q_s, k_s, v_s, cumb_s, betab_s, cumrow_s, o_s, *, tb):
    nc = tb // CHUNK
    nh = GDN_HEADS
    off0 = SUBLANE - (CONV_W - 1)

    @pl.when(pl.program_id(1) == 0)
    def _():
        state_ref[...] = jnp.zeros_like(state_ref)
        tail_ref[...] = jnp.zeros_like(tail_ref)

    row = lax.broadcasted_iota(jnp.int32, (CHUNK, CHUNK), 0)
    col = lax.broadcasted_iota(jnp.int32, (CHUNK, CHUNK), 1)
    incl = col <= row
    strict = col < row
    tri = jnp.where(incl, 1.0, 0.0).astype(BF16)
    eye = jnp.where(col == row, 1.0, 0.0).astype(F32)
    npar = GDN_NPAR
    rp = npar * CHUNK
    xbuf_ref[0:SUBLANE, :] = tail_ref[...]

    def prologue(t):
        lo = t * rp
        xbuf_ref[SUBLANE + lo:SUBLANE + lo + rp, :] = qkv_ref[lo:lo + rp, :].astype(F32)
        for j in range(3 * nh):
            cs = slice(j * LANE, (j + 1) * LANE)
            y = xbuf_ref[off0 + lo:off0 + lo + rp, cs] * cw_ref[0:1, cs]
            for kk in range(1, CONV_W):
                y = y + xbuf_ref[off0 + kk + lo:off0 + kk + lo + rp, cs] * cw_ref[kk:kk + 1, cs]
            y = _silu(y)
            if j < nh:
                q_s[j, lo:lo + rp, :] = (y * lax.rsqrt(jnp.sum(y * y, axis=-1, keepdims=True) + RMS_EPS)
                                         * (GDN_DK ** -0.5))
            elif j < 2 * nh:
                k_s[j - nh, lo:lo + rp, :] = y * lax.rsqrt(jnp.sum(y * y, axis=-1, keepdims=True) + RMS_EPS)
            else:
                v_s[j - 2 * nh, lo:lo + rp, :] = y
        sc = s_ref[lo:lo + rp, :]
        beta_all = _sigmoid_t(sc)
        g_all = -jnp.exp(par_ref[0:1, :]) * _softplus(sc + par_ref[1:2, :])
        cum_all = jnp.concatenate([_mm_01(tri, g_all[c * CHUNK:(c + 1) * CHUNK, :]) for c in range(npar)], axis=0)
        cum_t = jnp.transpose(cum_all)
        for c in range(npar):
            cumrow_s[t * npar + c] = cum_t[0:SUBLANE, c * CHUNK:(c + 1) * CHUNK]
        for h in range(nh):
            cumb_s[h, lo:lo + rp, :] = jnp.broadcast_to(cum_all[:, nh + h:nh + h + 1], (rp, LANE))
            betab_s[h, lo:lo + rp, :] = jnp.broadcast_to(beta_all[:, h:h + 1], (rp, LANE))

    def trip(cp, _slot):
        chains = [(cp * npar + cc, h) for cc in range(npar) for h in range(nh)]
        r0s = [c * CHUNK for c, _ in chains]
        qc = [q_s[h, r0:r0 + CHUNK, :] for (_, h), r0 in zip(chains, r0s)]
        kc = [k_s[h, r0:r0 + CHUNK, :] for (_, h), r0 in zip(chains, r0s)]
        vc = [v_s[h, r0:r0 + CHUNK, :] for (_, h), r0 in zip(chains, r0s)]
        cumc = [cumb_s[h, r0:r0 + CHUNK, :] for (_, h), r0 in zip(chains, r0s)]
        bc = [betab_s[h, r0:r0 + CHUNK, :] for (_, h), r0 in zip(chains, r0s)]
        n = len(chains)
        kk = [_mm_nt(kc[i], kc[i]) for i in range(n)]
        qk = [_mm_nt(qc[i], kc[i]) for i in range(n)]
        decay = []
        for i, (c, h) in enumerate(chains):
            diff = cumc[i][:, 0:CHUNK] - cumrow_s[c][nh + h:nh + h + 1, :]
            decay.append(jnp.where(incl, jnp.exp(jnp.where(incl, diff, 0.0)), 0.0))
        a = [jnp.where(strict, bc[i][:, 0:CHUNK] * kk[i] * decay[i], 0.0) for i in range(n)]
        x = [eye - a[i] for i in range(n)]
        p = [_mm(a[i], a[i]) for i in range(n)]
        for it in range(5):
            x = [x[i] + _mm(x[i], p[i]) for i in range(n)]
            if it < 4:
                p = [_mm(p[i], p[i]) for i in range(n)]
        ecum = [jnp.exp(cumc[i]) for i in range(n)]
        sol = [_mm(x[i], jnp.concatenate([vc[i] * bc[i], kc[i] * (bc[i] * ecum[i])], axis=1)) for i in range(n)]
        attn = [qk[i] * decay[i] for i in range(n)]
        cum_last = [cumc[i][CHUNK - 1:CHUNK, :] for i in range(n)]
        k_state = [kc[i] * jnp.exp(cum_last[i] - cumc[i]) for i in range(n)]
        wqi = [jnp.concatenate([sol[i][:, GDN_DV:GDN_DV + GDN_DK], qc[i] * ecum[i]], axis=0) for i in range(n)]
        for cc in range(npar):
            idx = [cc * nh + h for h in range(nh)]
            s = [state_ref[h] for h in range(nh)]
            wq = [_mm(wqi[i], s[h]) for h, i in enumerate(idx)]
            v_new = [sol[i][:, 0:GDN_DV] - wq[h][0:CHUNK, :] for h, i in enumerate(idx)]
            av = [_mm(attn[i], v_new[h]) for h, i in enumerate(idx)]
            upd = [_mm_tn(k_state[i], v_new[h]) for h, i in enumerate(idx)]
            for h, i in enumerate(idx):
                state_ref[h] = s[h] * jnp.exp(cum_last[i]) + upd[h]
                o_s[r0s[i]:r0s[i] + CHUNK, h * LANE:(h + 1) * LANE] = wq[h][CHUNK:2 * CHUNK, :] + av[h]

    def epilogue(t):
        lo = t * rp
        for h in range(nh):
            hs = slice(h * LANE, (h + 1) * LANE)
            o = o_s[lo:lo + rp, hs]
            oh = o * lax.rsqrt(jnp.mean(o * o, axis=-1, keepdims=True) + RMS_EPS) * nw_ref[...]
            o_ref[lo:lo + rp, hs] = (oh * _silu(gate_ref[lo:lo + rp, hs].astype(F32))).astype(o_ref.dtype)

    _schedule([(t,) for t in range(nc // npar)], prologue, trip, epilogue)
    tail_ref[...] = xbuf_ref[tb:tb + SUBLANE, :]


def _gdn_call(pb, pf, conv_w, par, norm_w, layer, batch, seq):
    tb = min(TB_MIX, seq)
    nb = seq // tb
    nc = tb // CHUNK
    t = batch * seq
    wq = 2 * GDN_QK + GDN_V
    kern = functools.partial(_gdn_kernel, tb=tb)
    return pl.pallas_call(
        kern,
        grid=(batch, nb),
        in_specs=[pl.BlockSpec((tb, wq), lambda b, i: (b * nb + i, A_Q // wq)),
                  pl.BlockSpec((tb, LANE), lambda b, i: (b * nb + i, F_AS // LANE)),
                  pl.BlockSpec((tb, GDN_V), lambda b, i: (b * nb + i, A_G // GDN_V)),
                  pl.BlockSpec((None, CONV_W, wq), lambda b, i: (layer, 0, 0)),
                  pl.BlockSpec((None, SUBLANE, LANE), lambda b, i: (layer, 0, 0)),
                  pl.BlockSpec((None, 1, GDN_DV), lambda b, i: (layer, 0, 0))],
        out_specs=pl.BlockSpec((tb, GDN_V), lambda b, i: (b * nb + i, 0)),
        out_shape=jax.ShapeDtypeStruct((t, GDN_V), BF16),
        scratch_shapes=[pltpu.VMEM((GDN_HEADS, GDN_DK, GDN_DV), F32),
                        pltpu.VMEM((SUBLANE, wq), F32),
                        pltpu.VMEM((tb + SUBLANE, wq), F32),
                        pltpu.VMEM((GDN_HEADS, tb, LANE), F32),
                        pltpu.VMEM((GDN_HEADS, tb, LANE), F32),
                        pltpu.VMEM((GDN_HEADS, tb, LANE), F32),
                        pltpu.VMEM((GDN_HEADS, tb, LANE), F32),
                        pltpu.VMEM((GDN_HEADS, tb, LANE), F32),
                        pltpu.VMEM((nc, SUBLANE, CHUNK), F32),
                        pltpu.VMEM((tb, GDN_V), F32)],
        compiler_params=_cparams(("parallel", "arbitrary")),
        name="gdn",
    )(pb, pf, pb, conv_w, par, norm_w)


def _dd_make_trip(q_s, k_s, v_s, la_s, o_s, state_ref, c8_s, p_s, g_heads):
    nblk = CHUNK // SUB
    dkh = LANE // g_heads
    dvp = g_heads * LANE
    npar = DD_NPAR
    row = lax.broadcasted_iota(jnp.int32, (CHUNK, CHUNK), 0)
    col = lax.broadcasted_iota(jnp.int32, (CHUNK, CHUNK), 1)
    level_masks = []
    for sh in (5, 4, 3):
        same2b = jnp.right_shift(row, sh + 1) == jnp.right_shift(col, sh + 1)
        upper = (jnp.right_shift(row, sh) & 1) == 1
        lower = (jnp.right_shift(col, sh) & 1) == 0
        level_masks.append(jnp.where(same2b, jnp.where(upper, jnp.where(lower, 1.0, 0.0), 0.0), 0.0))
    tri8 = jnp.where(jnp.right_shift(row, 3) == jnp.right_shift(col, 3),
                     jnp.where(col <= row, 1.0, 0.0), 0.0).astype(BF16)
    diag_mask = tri8.astype(F32)
    lane128 = lax.broadcasted_iota(jnp.int32, (CHUNK, LANE), 1)
    head_masks = [jnp.where((lane128 >= g * dkh) & (lane128 < (g + 1) * dkh), 1.0, 0.0)
                  for g in range(g_heads)]
    sub = lax.broadcasted_iota(jnp.int32, (SUB, LANE), 0)
    dk_sh = dkh.bit_length() - 1
    lane_sh = LANE.bit_length() - 1
    orow = lax.broadcasted_iota(jnp.int32, (LANE, dvp), 0)
    ocol = lax.broadcasted_iota(jnp.int32, (LANE, dvp), 1)
    ones_bd = jnp.where(jnp.right_shift(orow, dk_sh) == jnp.right_shift(ocol, lane_sh), 1.0, 0.0).astype(BF16)
    srow = lax.broadcasted_iota(jnp.int32, (dvp, LANE), 0)
    scol = lax.broadcasted_iota(jnp.int32, (dvp, LANE), 1)
    state_mask = jnp.where(jnp.right_shift(srow, lane_sh) == jnp.right_shift(scol, dk_sh), 1.0, 0.0)

    def block_sums(u, r0):
        c8 = [c8_s[u, r0 + b * SUB:r0 + (b + 1) * SUB, :] for b in range(nblk)]
        t8 = [c8_s[u, r0 + (b + 1) * SUB - 1:r0 + (b + 1) * SUB, :] for b in range(nblk)]
        t16 = [t8[2 * b] + t8[2 * b + 1] for b in range(nblk // 2)]
        t32 = [t16[2 * b] + t16[2 * b + 1] for b in range(nblk // 4)]
        t64 = t32[0] + t32[1]
        c16 = [c8[b] + t8[b - 1] if b % 2 else c8[b] for b in range(nblk)]
        c32 = [c16[b] + t16[b // 2 - 1] if (b // 2) % 2 else c16[b] for b in range(nblk)]
        c64 = [c32[b] + t32[0] if b >= nblk // 2 else c32[b] for b in range(nblk)]
        pre = {8: c8, 16: c16, 32: c32, 64: c64}
        suf = {8: [t8[b] - c8[b] for b in range(nblk)],
               16: [t16[b // 2] - c16[b] for b in range(nblk)],
               32: [t32[b // 4] - c32[b] for b in range(nblk)],
               64: [t64 - c64[b] for b in range(nblk)]}
        return pre, suf

    def cat(pieces):
        return jnp.concatenate(pieces, axis=0)

    def trip(u, t, slot):
        rng = range(npar)
        r0s = [(t * npar + i) * CHUNK for i in rng]
        qc = [q_s[u, r0:r0 + CHUNK, :] for r0 in r0s]
        kc = [k_s[u, r0:r0 + CHUNK, :] for r0 in r0s]
        vc = [v_s[u, r0:r0 + CHUNK, :] for r0 in r0s]
        c8_all = _mm_01(tri8, jnp.concatenate([la_s[u, r0:r0 + CHUNK, :] for r0 in r0s], axis=1))
        for i in rng:
            c8_s[u, r0s[i]:r0s[i] + CHUNK, :] = c8_all[:, i * LANE:(i + 1) * LANE]
        sums = [block_sums(u, r0s[i]) for i in rng]
        attn = [[None] * g_heads for _ in rng]
        for li, b in enumerate((32, 16, 8)):
            qs = [qc[i] * jnp.exp(cat(sums[i][0][b])) for i in rng]
            ks = [kc[i] * jnp.exp(cat(sums[i][1][b])) for i in rng]
            for g in range(g_heads):
                for i in rng:
                    qg = qs[i] * head_masks[g] if g_heads > 1 else qs[i]
                    term = _mm_nt(qg, ks[i]) * level_masks[li]
                    attn[i][g] = term if li == 0 else attn[i][g] + term
        for i in rng:
            dg = diag_terms(qc[i], kc[i], cat(sums[i][0][8]))
            for g in range(g_heads):
                attn[i][g] = attn[i][g] + dg[g]
        o = [apply(attn[i], vc[i]) for i in rng]
        q_inter = [qc[i] * jnp.exp(cat(sums[i][0][64])) for i in rng]
        k_state = [kc[i] * jnp.exp(cat(sums[i][1][64])) for i in rng]
        upd = [_mm_tn(vc[i], k_state[i]) for i in rng]
        st = state_ref[u]
        for i in rng:
            o_s[u, r0s[i]:r0s[i] + CHUNK, :] = o[i] + _mm_nt(q_inter[i], st)
            decay_last = jnp.exp(sums[i][0][64][nblk - 1][SUB - 1:SUB, :])
            st = st * decay_last + (upd[i] * state_mask if g_heads > 1 else upd[i])
        state_ref[u] = st

    def apply(att, v):
        if g_heads > 1:
            return jnp.concatenate([_mm(att[g], v[:, g * LANE:(g + 1) * LANE]) for g in range(g_heads)], axis=1)
        return _mm(att[0], v)

    def diag_terms(q, k, c8):
        qd = q * jnp.exp(c8)
        kd = k * jnp.exp(jnp.minimum(-c8, DD_SAFE_EXP))
        return [_mm_nt(qd * head_masks[g] if g_heads > 1 else qd, kd) * diag_mask for g in range(g_heads)]

    def fix_unit(u, nchunk):
        def body(c, carry):
            r0 = pl.multiple_of(c * CHUNK, CHUNK)
            rows = pl.ds(r0, CHUNK)
            fast = apply(diag_terms(q_s[u, rows, :], k_s[u, rows, :], c8_s[u, rows, :]), v_s[u, rows, :])
            for r in range(nblk):
                qr = q_s[u, pl.ds(r0 + r * SUB, SUB), :]
                cr = c8_s[u, pl.ds(r0 + r * SUB, SUB), :]
                for jj in range(SUB):
                    krow = k_s[u, pl.ds(r0 + r * SUB + jj, 1), :]
                    crow = c8_s[u, pl.ds(r0 + r * SUB + jj, 1), :]
                    pr = qr * krow * jnp.exp(jnp.where(sub >= jj, cr - crow, -jnp.inf))
                    p_s[(r * SUB + jj) * SUB:(r * SUB + jj + 1) * SUB, :] = pr
            rs = jnp.dot(p_s[...].astype(BF16), ones_bd, preferred_element_type=F32)
            od = []
            for r in range(nblk):
                acc = None
                for jj in range(SUB):
                    term = (rs[(r * SUB + jj) * SUB:(r * SUB + jj + 1) * SUB, :]
                            * v_s[u, pl.ds(r0 + r * SUB + jj, 1), :])
                    acc = term if acc is None else acc + term
                od.append(acc)
            o_s[u, rows, :] = o_s[u, rows, :] + (cat(od) - fast)
            return carry

        lax.fori_loop(0, nchunk, body, 0)

    return trip, fix_unit


def _dd_fix_if_unsafe(c8_s, fix_unit, nu, tb, units, epilogue):
    worst = None
    for u in range(nu):
        tot = -c8_s[u, pl.ds(SUB - 1, tb // SUB, stride=SUB), :]
        worst = tot if worst is None else jnp.maximum(worst, tot)

    @pl.when(jnp.max(worst) > DD_SAFE_EXP)
    def _():
        for u in range(nu):
            fix_unit(u, tb // CHUNK)
        for u, t in units:
            epilogue(u, t)


def _dd_scratch(nu, tb, dvp):
    return [pltpu.VMEM((nu, dvp, LANE), F32),
            pltpu.VMEM((nu, tb, LANE), F32), pltpu.VMEM((nu, tb, LANE), F32),
            pltpu.VMEM((nu, tb, dvp), F32),
            pltpu.VMEM((nu, tb, LANE), F32),
            pltpu.VMEM((nu, tb, dvp), F32),
            pltpu.VMEM((nu, tb, LANE), F32),
            pltpu.VMEM((CHUNK * SUB, LANE), F32)]


def _gla_kernel(q_ref, k_ref, v_ref, lr_ref, gate_ref, w2_ref, b2_ref, nw_ref,
                o_ref, state_ref, q_s, k_s, v_s, la_s, o_s, c8_s, p_s, *, tb):
    @pl.when(pl.program_id(1) == 0)
    def _():
        state_ref[...] = jnp.zeros_like(state_ref)

    npair = GLA_HEADS // 2
    rp = DD_NPAR * CHUNK
    trip, fix_unit = _dd_make_trip(q_s, k_s, v_s, la_s, o_s, state_ref, c8_s, p_s, 2)

    def prologue(u, t):
        lo = t * rp
        ls = slice(u * LANE, (u + 1) * LANE)
        vs = slice(u * 2 * LANE, (u + 1) * 2 * LANE)
        q_s[u, lo:lo + rp, :] = q_ref[lo:lo + rp, ls].astype(F32) * (GLA_DK ** -0.5)
        k_s[u, lo:lo + rp, :] = k_ref[lo:lo + rp, ls].astype(F32)
        v_s[u, lo:lo + rp, :] = v_ref[lo:lo + rp, vs].astype(F32)
        z = _mm(lr_ref[lo:lo + rp, :], w2_ref[:, ls]) + b2_ref[:, ls]
        la_s[u, lo:lo + rp, :] = _log_sigmoid(z) * (1.0 / GLA_NORMALIZER)

    def epilogue(u, t):
        lo = t * rp
        for g in range(2):
            hs = slice((2 * u + g) * LANE, (2 * u + g + 1) * LANE)
            o = o_s[u, lo:lo + rp, g * LANE:(g + 1) * LANE]
            oh = o * lax.rsqrt(jnp.mean(o * o, axis=-1, keepdims=True) + RMS_EPS) * nw_ref[...]
            o_ref[lo:lo + rp, hs] = (oh * _silu(gate_ref[lo:lo + rp, hs].astype(F32))).astype(o_ref.dtype)

    units = [(u, t) for u in range(npair) for t in range(tb // rp)]
    _schedule(units, prologue, trip, epilogue)
    _dd_fix_if_unsafe(c8_s, fix_unit, npair, tb, units, epilogue)


def _gla_call(pb, pf, w2p, b2, norm_w, layer, batch, seq):
    tb = min(TB_MIX, seq)
    nb = seq // tb
    t = batch * seq
    kern = functools.partial(_gla_kernel, tb=tb)
    return pl.pallas_call(
        kern,
        grid=(batch, nb),
        in_specs=[pl.BlockSpec((tb, GLA_QK), lambda b, i: (b * nb + i, B_Q // GLA_QK)),
                  pl.BlockSpec((tb, GLA_QK), lambda b, i: (b * nb + i, B_K // GLA_QK)),
                  pl.BlockSpec((tb, GLA_V), lambda b, i: (b * nb + i, B_V // GLA_V)),
                  pl.BlockSpec((tb, LANE), lambda b, i: (b * nb + i, F_LR // LANE)),
                  pl.BlockSpec((tb, GLA_V), lambda b, i: (b * nb + i, B_G // GLA_V)),
                  pl.BlockSpec((None, LANE, GLA_QK), lambda b, i: (layer, 0, 0)),
                  pl.BlockSpec((None, 1, GLA_QK), lambda b, i: (layer, 0, 0)),
                  pl.BlockSpec((None, 1, GLA_DV), lambda b, i: (layer, 0, 0))],
        out_specs=pl.BlockSpec((tb, GLA_V), lambda b, i: (b * nb + i, 0)),
        out_shape=jax.ShapeDtypeStruct((t, GLA_V), BF16),
        scratch_shapes=_dd_scratch(GLA_HEADS // 2, tb, 2 * LANE),
        compiler_params=_cparams(("parallel", "arbitrary")),
        name="gla",
    )(pb, pb, pb, pf, pb, w2p, b2, norm_w)


def _hgrn_kernel(q_ref, f_ref, v_ref, gate_ref, lbl_ref, nw_ref,
                 o_ref, state_ref, q_s, k_s, v_s, la_s, o_s, c8_s, p_s, *, tb, layer):
    @pl.when(pl.program_id(1) == 0)
    def _():
        state_ref[...] = jnp.zeros_like(state_ref)

    logits = lbl_ref[...]
    mx = jnp.max(logits, axis=0, keepdims=True)
    ex = jnp.exp(logits - mx)
    p = ex / jnp.sum(ex, axis=0, keepdims=True)
    acc = p[0:1, :]
    for r in range(1, layer + 1):
        acc = acc + p[r:r + 1, :]
    lb = jnp.clip(acc - p[0:1, :], 0.0, 1.0)
    log_lb = jnp.log(jnp.maximum(lb, LB_FLOOR))
    log_1m = jnp.log(1.0 - lb)

    rp = DD_NPAR * CHUNK
    trip, fix_unit = _dd_make_trip(q_s, k_s, v_s, la_s, o_s, state_ref, c8_s, p_s, 1)

    def prologue(u, t):
        lo = t * rp
        hs = slice(u * LANE, (u + 1) * LANE)
        cf = f_ref[lo:lo + rp, hs]
        second = log_1m[:, hs] + _log_sigmoid(cf)
        llb = log_lb[:, hs]
        la_s[u, lo:lo + rp, :] = jnp.maximum(llb, second) + jnp.log(1.0 + jnp.exp(-jnp.abs(llb - second)))
        k_s[u, lo:lo + rp, :] = (1.0 - lb[:, hs]) * _sigmoid_t(-cf)
        q_s[u, lo:lo + rp, :] = _silu(q_ref[lo:lo + rp, hs].astype(F32)) * (HGRN_EXPAND ** -0.5)
        v_s[u, lo:lo + rp, :] = v_ref[lo:lo + rp, hs].astype(F32)

    def epilogue(u, t):
        lo = t * rp
        hs = slice(u * LANE, (u + 1) * LANE)
        o = o_s[u, lo:lo + rp, :]
        oh = o * lax.rsqrt(jnp.mean(o * o, axis=-1, keepdims=True) + RMS_EPS) * nw_ref[...]
        o_ref[lo:lo + rp, hs] = (oh * _silu(gate_ref[lo:lo + rp, hs].astype(F32))).astype(o_ref.dtype)

    units = [(u, t) for u in range(HGRN_HEADS) for t in range(tb // rp)]
    _schedule(units, prologue, trip, epilogue)
    _dd_fix_if_unsafe(c8_s, fix_unit, HGRN_HEADS, tb, units, epilogue)


def _hgrn_call(pb, pf, lb_logits, norm_w, layer, batch, seq):
    tb = min(TB_MIX, seq)
    nb = seq // tb
    t = batch * seq
    kern = functools.partial(_hgrn_kernel, tb=tb, layer=layer)
    return pl.pallas_call(
        kern,
        grid=(batch, nb),
        in_specs=[pl.BlockSpec((tb, HGRN_QK), lambda b, i: (b * nb + i, C_Q // HGRN_QK)),
                  pl.BlockSpec((tb, HGRN_QK), lambda b, i: (b * nb + i, F_CF // HGRN_QK)),
                  pl.BlockSpec((tb, HGRN_V), lambda b, i: (b * nb + i, C_I // HGRN_V)),
                  pl.BlockSpec((tb, HGRN_V), lambda b, i: (b * nb + i, C_G // HGRN_V)),
                  pl.BlockSpec((DEPTH, HGRN_QK), lambda b, i: (0, 0)),
                  pl.BlockSpec((None, 1, HGRN_DV), lambda b, i: (layer, 0, 0))],
        out_specs=pl.BlockSpec((tb, HGRN_V), lambda b, i: (b * nb + i, 0)),
        out_shape=jax.ShapeDtypeStruct((t, HGRN_V), BF16),
        scratch_shapes=_dd_scratch(HGRN_HEADS, tb, LANE),
        compiler_params=_cparams(("parallel", "arbitrary")),
        name="hgrn",
    )(pb, pf, pb, pb, lb_logits, norm_w)


def _merge_kernel(oa_ref, ob_ref, oc_ref, ma_ref, mb_ref, mc_ref, h_ref,
                  wa_ref, wb_ref, wc_ref, wo_ref, g_ref, b_ref, ox_ref, *, tm):
    y = (_sigmoid_t(ma_ref[...].astype(F32)) * jnp.dot(oa_ref[...], wa_ref[...], preferred_element_type=F32)
         + _sigmoid_t(mb_ref[...].astype(F32)) * jnp.dot(ob_ref[...], wb_ref[...], preferred_element_type=F32)
         + _sigmoid_t(mc_ref[...].astype(F32)) * jnp.dot(oc_ref[...], wc_ref[...], preferred_element_type=F32))
    mix = _mm(y, wo_ref[...])
    _to_token_major(ox_ref, _layer_norm(ALPHA * h_ref[...] + mix, g_ref[...], b_ref[...]), tm)


def _merge_call(o_a, o_b, o_c, pb, h, wa, wb, wc, wo, g, b, layer):
    t, d = h.shape
    tm = min(TM_ROWS, t)

    def row(width):
        return pl.BlockSpec((tm, width), lambda i: (i, 0))

    def wspec(kdim):
        return pl.BlockSpec((None, kdim, d), lambda i: (layer, 0, 0))

    vec = pl.BlockSpec((None, 1, d), lambda i: (layer, 0, 0))
    return pl.pallas_call(
        functools.partial(_merge_kernel, tm=tm),
        grid=(t // tm,),
        in_specs=[row(GDN_V), row(GLA_V), row(HGRN_V),
                  pl.BlockSpec((tm, d), lambda i: (i, M_A // d)),
                  pl.BlockSpec((tm, d), lambda i: (i, M_B // d)),
                  pl.BlockSpec((tm, d), lambda i: (i, M_C // d)),
                  row(d), wspec(GDN_V), wspec(GLA_V), wspec(HGRN_V), wspec(d), vec, vec],
        out_specs=pl.BlockSpec((tm * NPIECE, LANE), lambda i: (i, 0)),
        out_shape=jax.ShapeDtypeStruct((t * NPIECE, LANE), F32),
        compiler_params=_cparams(("parallel",)),
        name="merge",
    )(o_a, o_b, o_c, pb, pb, pb, h, wa, wb, wc, wo, g, b)


def _route_pairs(scores_t, bias_ref):
    s = [scores_t[e:e + 1, :] for e in range(N_EXPERTS)]
    sel = [s[e] + bias_ref[e:e + 1, 0:1] for e in range(N_EXPERTS)]
    gscore = []
    for g in range(N_GROUPS):
        a, b, c, d = sel[4 * g:4 * g + 4]
        hi1, lo1 = jnp.maximum(a, b), jnp.minimum(a, b)
        hi2, lo2 = jnp.maximum(c, d), jnp.minimum(c, d)
        top1 = jnp.maximum(hi1, hi2)
        top2 = jnp.maximum(jnp.minimum(hi1, hi2), jnp.maximum(lo1, lo2))
        gscore.append(top1 + top2)
    best = gscore[0]
    gidx = jnp.zeros_like(best, dtype=jnp.int32)
    for g in range(1, N_GROUPS):
        take = gscore[g] > best
        best = jnp.where(take, gscore[g], best)
        gidx = jnp.where(take, g, gidx)
    ing, raw = [], []
    for kk in range(EXPERTS_PER_GROUP):
        vs, vr = sel[kk], s[kk]
        for g in range(1, N_GROUPS):
            pick = gidx == g
            vs = jnp.where(pick, sel[4 * g + kk], vs)
            vr = jnp.where(pick, s[4 * g + kk], vr)
        ing.append(vs)
        raw.append(vr)
    b1 = ing[0]
    i1 = jnp.zeros_like(gidx)
    for kk in range(1, EXPERTS_PER_GROUP):
        take = ing[kk] > b1
        b1 = jnp.where(take, ing[kk], b1)
        i1 = jnp.where(take, kk, i1)
    neg = jnp.full_like(b1, -jnp.inf)
    b2 = neg
    i2 = jnp.zeros_like(gidx)
    for kk in range(EXPERTS_PER_GROUP):
        cand = jnp.where(i1 == kk, neg, ing[kk])
        take = cand > b2
        b2 = jnp.where(take, cand, b2)
        i2 = jnp.where(take, kk, i2)
    w1 = raw[0]
    w2 = raw[0]
    for kk in range(1, EXPERTS_PER_GROUP):
        w1 = jnp.where(i1 == kk, raw[kk], w1)
        w2 = jnp.where(i2 == kk, raw[kk], w2)
    tot = w1 + w2
    w1 = w1 / tot
    w2 = w2 / tot
    first_lower = i1 < i2
    lo = jnp.where(first_lower, i1, i2)
    hi = jnp.where(first_lower, i2, i1)
    pidx = jnp.where(lo == 0, hi - 1, jnp.where(lo == 1, hi + 1, 5))
    bkt = gidx * 6 + pidx
    return bkt, jnp.where(first_lower, w1, w2), jnp.where(first_lower, w2, w1)


def _moe_route_kernel(hx_ref, wr_ref, rb_ref, bkt_ref, rank_ref, wab_ref, cnt_ref, carry_ref, *, tm):
    @pl.when(pl.program_id(0) == 0)
    def _():
        carry_ref[...] = jnp.zeros_like(carry_ref)

    logits_t = _mm_nt_f32(wr_ref[...], _from_token_major(hx_ref, tm))
    bkt, wa, wb = _route_pairs(_sigmoid(logits_t), rb_ref)
    sub = lax.broadcasted_iota(jnp.int32, (32, tm), 0)
    oh = jnp.where(sub == bkt, 1.0, 0.0)
    r = lax.broadcasted_iota(jnp.int32, (tm, tm), 0)
    c = lax.broadcasted_iota(jnp.int32, (tm, tm), 1)
    earlier = jnp.where(r < c, 1.0, 0.0).astype(BF16)
    before = jnp.dot(oh.astype(BF16), earlier, preferred_element_type=F32)
    carry = carry_ref[...]
    rank = jnp.sum(oh * (before + carry[:, 0:1]), axis=0, keepdims=True)
    carry = carry + jnp.sum(oh, axis=1, keepdims=True)
    carry_ref[...] = carry
    cnt_ref[...] = carry
    bkt_ref[...] = bkt
    rank_ref[...] = rank.astype(jnp.int32)
    pad = jnp.zeros((LANE - 2, tm), F32)
    wab_ref[...] = jnp.transpose(jnp.concatenate([wa, wb, pad], axis=0))


def _moe_tables_kernel(cnt_ref, bkt_ref, rank_ref, pos_ref, tabs_ref, *, t, tr):
    cnt = cnt_ref[...]
    sz = jnp.floor((cnt + (tr - 1)) * (1.0 / tr)) * tr
    r = lax.broadcasted_iota(jnp.int32, (32, 32), 0)
    c = lax.broadcasted_iota(jnp.int32, (32, 32), 1)
    start = jnp.dot(jnp.where(c < r, 1.0, 0.0), sz, preferred_element_type=F32, precision=HIGHEST)
    end = start + sz
    sub = lax.broadcasted_iota(jnp.int32, (32, t), 0)
    pos = jnp.sum(jnp.where(sub == bkt_ref[...], start[:, 0:1], 0.0), axis=0, keepdims=True)
    pos_ref[...] = pos.astype(jnp.int32) + rank_ref[...]
    brow = lax.broadcasted_iota(jnp.int32, (32, LANE), 0)
    tile0 = lax.broadcasted_iota(jnp.int32, (32, LANE), 1).astype(F32) * tr
    tbk = jnp.sum(jnp.where((brow < MOE_NBKT) & (end <= tile0), 1, 0), axis=0, keepdims=True)
    tbk = jnp.minimum(tbk, MOE_NBKT - 1)
    total = end[MOE_NBKT - 1:MOE_NBKT, :]
    valid = jnp.where(tile0[0:1, :] < total, 1, 0)
    g = jnp.where(tbk >= 6, 1, 0) + jnp.where(tbk >= 12, 1, 0) + jnp.where(tbk >= 18, 1, 0)
    p = tbk - 6 * g
    ge3 = jnp.where(p >= 3, 1, 0)
    ge5 = jnp.where(p >= 5, 1, 0)
    ea = 4 * g + ge3 + ge5
    eb = 4 * g + p + 1 - 2 * ge3 - ge5
    lane = lax.broadcasted_iota(jnp.int32, (1, LANE), 1)
    last_tile = (total * (1.0 / tr)).astype(jnp.int32) - 1
    src = jnp.where(valid > 0, lane, jnp.maximum(last_tile, 0))
    zero = jnp.zeros((SUBLANE - 3, LANE), jnp.int32)
    tabs_ref[...] = jnp.concatenate([ea, eb, src, zero], axis=0)


def _moe_group_kernel(ea_ref, eb_ref, vd_ref, xs_ref, ws_ref, wga_ref, wua_ref, wda_ref,
                      wgb_ref, wub_ref, wdb_ref, ys_ref, wg_s, wu_s, wd_s, *, tr):
    j = pl.program_id(0)
    prev = jnp.maximum(j - 1, 0)

    @pl.when((j == 0) | (ea_ref[j] != ea_ref[prev]))
    def _():
        wg_s[0] = wga_ref[...].astype(BF16)
        wu_s[0] = wua_ref[...].astype(BF16)
        wd_s[0] = wda_ref[...].astype(BF16)

    @pl.when((j == 0) | (eb_ref[j] != eb_ref[prev]))
    def _():
        wg_s[1] = wgb_ref[...].astype(BF16)
        wu_s[1] = wub_ref[...].astype(BF16)
        wd_s[1] = wdb_ref[...].astype(BF16)

    @pl.when(vd_ref[j] == j)
    def _():
        x = _from_token_major(xs_ref, tr).astype(BF16)
        w = ws_ref[...]

        def ffn(s, cw):
            hg = jnp.dot(x, wg_s[s], preferred_element_type=F32)
            hu = jnp.dot(x, wu_s[s], preferred_element_type=F32)
            hid = _silu(hg) * hu * cw
            return jnp.dot(hid.astype(BF16), wd_s[s], preferred_element_type=F32)

        y = ffn(0, w[:, 0:1]) + ffn(1, w[:, 1:2])
        _to_token_major(ys_ref, y, tr)


def _moe_ln_kernel(hx_ref, y_ref, g_ref, b_ref, o_ref, ob_ref, *, tm):
    out = _layer_norm(ALPHA * _from_token_major(hx_ref, tm) + _from_token_major(y_ref, tm),
                      g_ref[...], b_ref[...])
    o_ref[...] = out
    ob_ref[...] = out.astype(BF16)


def _sc_mesh_info():
    info = plsc.get_sparse_core_info()
    mesh = plsc.VectorSubcoreMesh(core_axis_name="c", subcore_axis_name="s")
    return mesh, info.num_cores, info.num_subcores


def _sc_scatter_rows(x3, w2, pos, n_rows):
    t = x3.shape[0]
    mesh, nc, ns = _sc_mesh_info()
    per_w = t // (nc * ns)

    @functools.partial(
        pl.kernel, mesh=mesh,
        out_type=[jax.ShapeDtypeStruct((n_rows,) + x3.shape[1:], x3.dtype),
                  jax.ShapeDtypeStruct((n_rows,) + w2.shape[1:], w2.dtype)],
        scratch_types=[pltpu.VMEM((SC_CH,), jnp.int32), pltpu.VMEM((SC_CH,) + x3.shape[1:], x3.dtype),
                       pltpu.VMEM((SC_CH,) + w2.shape[1:], w2.dtype),
                       pltpu.SemaphoreType.DMA, pltpu.SemaphoreType.DMA])
    def k(x_hbm, w_hbm, idx_hbm, ox_hbm, ow_hbm, idx_v, rows_v, wrows_v, sem_x, sem_w):
        base = (lax.axis_index("s") * nc + lax.axis_index("c")) * per_w

        @pl.loop(0, per_w // SC_CH)
        def _(j):
            off = base + j * SC_CH
            pltpu.sync_copy(idx_hbm.at[pl.ds(off, SC_CH)], idx_v)
            pltpu.sync_copy(x_hbm.at[pl.ds(off, SC_CH)], rows_v)
            pltpu.sync_copy(w_hbm.at[pl.ds(off, SC_CH)], wrows_v)
            cx = pltpu.async_copy(rows_v, ox_hbm.at[idx_v], sem_x)
            cw = pltpu.async_copy(wrows_v, ow_hbm.at[idx_v], sem_w)
            cx.wait()
            cw.wait()

    return k(x3, w2, pos)


def _sc_gather_rows(y3, pos):
    t = pos.shape[0]
    mesh, nc, ns = _sc_mesh_info()
    per_w = t // (nc * ns)
    ch = SC_CH // 2

    @functools.partial(
        pl.kernel, mesh=mesh,
        out_type=jax.ShapeDtypeStruct((t,) + y3.shape[1:], y3.dtype),
        scratch_types=[pltpu.VMEM((ch,), jnp.int32), pltpu.VMEM((ch,), jnp.int32),
                       pltpu.VMEM((ch,) + y3.shape[1:], y3.dtype), pltpu.VMEM((ch,) + y3.shape[1:], y3.dtype)]
        + [pltpu.SemaphoreType.DMA] * 4)
    def k(y_hbm, idx_hbm, o_hbm, idx_a, idx_b, rows_a, rows_b, sg_a, sg_b, sw_a, sw_b):
        base = (lax.axis_index("s") * nc + lax.axis_index("c")) * per_w

        @pl.loop(0, per_w // (2 * ch))
        def _(j):
            off_a = base + j * 2 * ch
            off_b = off_a + ch
            pltpu.sync_copy(idx_hbm.at[pl.ds(off_a, ch)], idx_a)
            ga = pltpu.async_copy(y_hbm.at[idx_a], rows_a, sg_a)
            pltpu.sync_copy(idx_hbm.at[pl.ds(off_b, ch)], idx_b)
            gb = pltpu.async_copy(y_hbm.at[idx_b], rows_b, sg_b)
            ga.wait()
            wa = pltpu.async_copy(rows_a, o_hbm.at[pl.ds(off_a, ch)], sw_a)
            gb.wait()
            wb = pltpu.async_copy(rows_b, o_hbm.at[pl.ds(off_b, ch)], sw_b)
            wa.wait()
            wb.wait()

    return k(y3, pos)


def _moe_sparse(hx, wr_t, rbias, wg, wu, wd, g, b, layer):
    t = hx.shape[0] // NPIECE
    d = D_MODEL
    tm = min(TM_MOE, t)
    tr = MOE_TR
    nt = t // tr + MOE_NBKT
    n_rows = nt * tr
    row1 = pl.BlockSpec((1, tm), lambda i: (0, i))
    rows_tm = pl.BlockSpec((tm * NPIECE, LANE), lambda i: (i, 0))
    bkt, rank, wab, cnt = pl.pallas_call(
        functools.partial(_moe_route_kernel, tm=tm),
        grid=(t // tm,),
        in_specs=[rows_tm,
                  pl.BlockSpec((N_EXPERTS, d), lambda i: (0, 0)),
                  pl.BlockSpec((N_EXPERTS, LANE), lambda i: (0, 0))],
        out_specs=[row1, row1, pl.BlockSpec((tm, LANE), lambda i: (i, 0)),
                   pl.BlockSpec((32, LANE), lambda i: (0, 0))],
        out_shape=[jax.ShapeDtypeStruct((1, t), jnp.int32), jax.ShapeDtypeStruct((1, t), jnp.int32),
                   jax.ShapeDtypeStruct((t, LANE), F32), jax.ShapeDtypeStruct((32, LANE), F32)],
        scratch_shapes=[pltpu.VMEM((32, LANE), F32)],
        compiler_params=_cparams(("arbitrary",)),
        name="moe_route",
    )(hx, wr_t, rbias)
    pos, tabs = pl.pallas_call(
        functools.partial(_moe_tables_kernel, t=t, tr=tr),
        out_shape=[jax.ShapeDtypeStruct((1, t), jnp.int32), jax.ShapeDtypeStruct((SUBLANE, LANE), jnp.int32)],
        compiler_params=pltpu.CompilerParams(vmem_limit_bytes=VMEM_LIMIT),
        name="moe_tables",
    )(cnt, bkt, rank)
    pos = pos.reshape(t)
    xs3, ws = _sc_scatter_rows(hx.reshape(t, NPIECE, LANE), wab, pos, n_rows)

    def wspec(which, shape):
        if which == 0:
            return pl.BlockSpec((None, None) + shape, lambda j, ea, eb, vd: (layer, ea[j], 0, 0))
        return pl.BlockSpec((None, None) + shape, lambda j, ea, eb, vd: (layer, eb[j], 0, 0))

    ys = pl.pallas_call(
        functools.partial(_moe_group_kernel, tr=tr),
        grid_spec=pltpu.PrefetchScalarGridSpec(
            num_scalar_prefetch=3,
            grid=(nt,),
            in_specs=[pl.BlockSpec((tr * NPIECE, LANE), lambda j, ea, eb, vd: (vd[j], 0)),
                      pl.BlockSpec((tr, LANE), lambda j, ea, eb, vd: (vd[j], 0)),
                      wspec(0, (d, D_FF)), wspec(0, (d, D_FF)), wspec(0, (D_FF, d)),
                      wspec(1, (d, D_FF)), wspec(1, (d, D_FF)), wspec(1, (D_FF, d))],
            out_specs=pl.BlockSpec((tr * NPIECE, LANE), lambda j, ea, eb, vd: (vd[j], 0)),
            scratch_shapes=[pltpu.VMEM((2, d, D_FF), BF16), pltpu.VMEM((2, d, D_FF), BF16),
                            pltpu.VMEM((2, D_FF, d), BF16)]),
        out_shape=jax.ShapeDtypeStruct((n_rows * NPIECE, LANE), F32),
        compiler_params=_cparams(("arbitrary",)),
        name="moe_experts",
    )(tabs[0, :nt], tabs[1, :nt], tabs[2, :nt], xs3.reshape(n_rows * NPIECE, LANE), ws,
      wg, wu, wd, wg, wu, wd)
    y3 = _sc_gather_rows(ys.reshape(n_rows, NPIECE, LANE), pos)
    vec = pl.BlockSpec((None, 1, d), lambda i: (layer, 0, 0))
    row = pl.BlockSpec((tm, d), lambda i: (i, 0))
    return pl.pallas_call(
        functools.partial(_moe_ln_kernel, tm=tm),
        grid=(t // tm,),
        in_specs=[rows_tm, rows_tm, vec, vec],
        out_specs=[row, row],
        out_shape=[jax.ShapeDtypeStruct((t, d), F32), jax.ShapeDtypeStruct((t, d), BF16)],
        compiler_params=_cparams(("parallel",)),
        name="moe_ln",
    )(hx, y3.reshape(t * NPIECE, LANE), g, b)


def _pack_w_in(w_in):
    wt = jnp.swapaxes(w_in, 1, 2)
    (a_q, a_k, a_v, a_beta, a_dt, a_g, b_q, b_k, b_v, b_lr, b_g,
     c_q, c_f, c_i, c_g, m_a, m_b, m_c) = jnp.split(wt, SPLIT_POINTS, axis=1)
    depth, _, d = wt.shape
    a_s = jnp.concatenate([a_beta, a_dt, jnp.zeros((depth, LANE - 2 * GDN_HEADS, d), wt.dtype)], 1)
    b_lrp = jnp.concatenate([b_lr, jnp.zeros((depth, LANE - GLA_RANK, d), wt.dtype)], 1)
    wb = jnp.concatenate([m_a, m_b, m_c, a_q, a_k, a_v, a_g, b_q, b_k, b_v, b_g, c_q, c_i, c_g], 1).astype(BF16)
    wf = jnp.concatenate([c_f, a_s, b_lrp], 1).astype(BF16)
    assert wb.shape[1] == NPB and wf.shape[1] == NPF
    return wb, wf


def _prepare(w_in, gdn_conv, gdn_a_log, gdn_dt_bias, gdn_norm, gla_w2, gla_b2, gla_norm, hgrn_lb_logits,
             hgrn_norm, w_br_a, w_br_b, w_br_c, w_out, ln1_g, ln1_b, w_router, router_bias, w_gate, w_up,
             w_down, ln2_g, ln2_b):
    depth = w_in.shape[0]
    d = w_out.shape[-1]
    w_pb, w_pf = _pack_w_in(w_in)
    return dict(
        w_pb=w_pb, w_pf=w_pf,
        gdn_conv=gdn_conv,
        gdn_par=jnp.pad(jnp.stack([gdn_a_log, gdn_dt_bias], axis=1),
                        ((0, 0), (0, SUBLANE - 2), (GDN_HEADS, LANE - 2 * GDN_HEADS))),
        gdn_norm=gdn_norm.reshape(depth, 1, GDN_DV),
        w2p=jnp.concatenate([gla_w2, jnp.zeros((depth, LANE - GLA_RANK, GLA_QK), gla_w2.dtype)], axis=1),
        gla_b2=gla_b2.reshape(depth, 1, GLA_QK),
        gla_norm=gla_norm.reshape(depth, 1, GLA_DV),
        lb_logits=hgrn_lb_logits,
        hgrn_norm=hgrn_norm.reshape(depth, 1, HGRN_DV),
        wa=w_br_a.astype(BF16), wb=w_br_b.astype(BF16), wc=w_br_c.astype(BF16), wo=w_out.astype(BF16),
        ln1_g=ln1_g.reshape(depth, 1, d), ln1_b=ln1_b.reshape(depth, 1, d),
        wr_t=jnp.transpose(w_router),
        rbias=jnp.broadcast_to(router_bias[:, None], (N_EXPERTS, LANE)),
        wg=w_gate, wu=w_up, wd=w_down,
        ln2_g=ln2_g.reshape(depth, 1, d), ln2_b=ln2_b.reshape(depth, 1, d),
    )


def _mixer_block(h, hb, p, layer, batch, seq):
    pb = _inproj_call(hb, p["w_pb"], layer, BF16, TM_PROJ, TN_PROJ, "inproj_b")
    pf = _inproj_call(hb, p["w_pf"], layer, F32, TM_PROJ, NPF, "inproj_f")
    o_a = _gdn_call(pb, pf, p["gdn_conv"], p["gdn_par"], p["gdn_norm"], layer, batch, seq)
    o_b = _gla_call(pb, pf, p["w2p"], p["gla_b2"], p["gla_norm"], layer, batch, seq)
    o_c = _hgrn_call(pb, pf, p["lb_logits"], p["hgrn_norm"], layer, batch, seq)
    return _merge_call(o_a, o_b, o_c, pb, h, p["wa"], p["wb"], p["wc"], p["wo"], p["ln1_g"], p["ln1_b"], layer)


def _ffn_block(hx, p, layer):
    return _moe_sparse(hx, p["wr_t"], p["rbias"], p["wg"], p["wu"], p["wd"], p["ln2_g"], p["ln2_b"], layer)


def kernel(x, ln0_g, ln0_b, w_in, gdn_conv, gdn_a_log, gdn_dt_bias, gdn_norm, gla_w2, gla_b2, gla_norm,
           hgrn_lb_logits, hgrn_norm, w_br_a, w_br_b, w_br_c, w_out, ln1_g, ln1_b, w_router, router_bias,
           w_gate, w_up, w_down, ln2_g, ln2_b):
    batch, seq, d = x.shape
    p = _prepare(w_in, gdn_conv, gdn_a_log, gdn_dt_bias, gdn_norm, gla_w2, gla_b2, gla_norm, hgrn_lb_logits,
                 hgrn_norm, w_br_a, w_br_b, w_br_c, w_out, ln1_g, ln1_b, w_router, router_bias, w_gate, w_up,
                 w_down, ln2_g, ln2_b)
    h, hb = _ln_call(x.reshape(batch * seq, d), ln0_g, ln0_b)
    for layer in range(w_in.shape[0]):
        hx = _mixer_block(h, hb, p, layer, batch, seq)
        h, hb = _ffn_block(hx, p, layer)
    return h.reshape(batch, seq, d)
```
